```python
import jax, jax.numpy as jnp
from jax import lax
import numpy as np

D_MODEL = 1024
BATCH = 32
SEQ = 2048
DEPTH = 2

N_MIXERS = 2
N_A_LAYERS = (DEPTH + 1) // 2
N_B_LAYERS = DEPTH // 2
RMS_EPS = 1e-6
LRU_WIDTH = D_MODEL
LRU_HEADS = 4
LRU_BLOCK = LRU_WIDTH // LRU_HEADS
CONV_WIDTH = 4
LRU_C = 8.0
RWKV_HEAD = 64
RWKV_HEADS = D_MODEL // RWKV_HEAD
DECAY_LORA = 64
AAA_LORA = 64
GATE_LORA = 128
RWKV_GN_EPS = 64e-5
MEM_LEN = 256
MEM_HEADS = 4
MEM_HEAD_DIM = D_MODEL // MEM_HEADS
D_FF = 4 * D_MODEL

kernel_name = "hybrid_rglru_rwkv7_memxattn"


def rms_norm(x, g):
    xf = x.astype(jnp.float32)
    y = xf * lax.rsqrt(jnp.mean(xf * xf, axis=-1, keepdims=True) + RMS_EPS)
    return (y * g.astype(jnp.float32)).astype(x.dtype)


def _lru_combine(c1, c2):
    a1, b1 = c1
    a2, b2 = c2
    return a1 * a2, a2 * b1 + b2


def rglru_mixer(x, conv_w, conv_b, w_in, b_in, gate_w, gate_b, lam, w_out, b_out):
    B, S, _ = x.shape
    proj = x @ w_in + b_in
    y_branch, u = jnp.split(proj, 2, axis=-1)
    y_branch = jax.nn.gelu(y_branch, approximate=True)
    u_pad = jnp.pad(u, ((0, 0), (CONV_WIDTH - 1, 0), (0, 0)))
    conv = conv_b + u_pad[:, 0:S] * conv_w[0]
    for tap in range(1, CONV_WIDTH):
        conv = conv + u_pad[:, tap:tap + S] * conv_w[tap]
    ub = conv.reshape(B, S, LRU_HEADS, LRU_BLOCK)
    gates = jax.nn.sigmoid(jnp.einsum('bshi,ghij->gbshj', ub, gate_w) + gate_b[:, None, None])
    r_gate = gates[0].reshape(B, S, LRU_WIDTH).astype(jnp.float32)
    i_gate = gates[1].reshape(B, S, LRU_WIDTH).astype(jnp.float32)
    log_a = -LRU_C * r_gate * jax.nn.softplus(-lam.astype(jnp.float32))
    a = jnp.exp(log_a)
    mult = jnp.sqrt(-jnp.expm1(2.0 * log_a))
    b = mult * i_gate * conv.astype(jnp.float32)
    _, h = lax.associative_scan(_lru_combine, (a, b), axis=1)
    return (h.astype(x.dtype) * y_branch) @ w_out + b_out


def rwkv7_mixer(x, mu, w_rkv, w0, w1, w2, a0, a1, a2, g1, g2, k_k, k_a, r_k, gn_g, gn_b, w_o):
    B, S, D = x.shape
    H, N = RWKV_HEADS, RWKV_HEAD
    x_prev = jnp.pad(x, ((0, 0), (1, 0), (0, 0)))[:, :S]
    xx = x_prev - x
    r = (x + xx * mu[0]) @ w_rkv[0]
    xw = x + xx * mu[1]
    k = (x + xx * mu[2]) @ w_rkv[1]
    v = (x + xx * mu[3]) @ w_rkv[2]
    xa = x + xx * mu[4]
    xg = x + xx * mu[5]
    w_log = -jax.nn.softplus(-(w0 + jnp.tanh(xw @ w1) @ w2).astype(jnp.float32)) - 0.5
    decay = jnp.exp(-jnp.exp(w_log))
    a = jax.nn.sigmoid(a0 + (xa @ a1) @ a2)
    g = jax.nn.sigmoid(xg @ g1) @ g2
    kk = (k * k_k).reshape(B, S, H, N).astype(jnp.float32)
    kk = kk / jnp.maximum(jnp.linalg.norm(kk, axis=-1, keepdims=True), 1e-12)
    k = k * (1.0 + (a - 1.0) * k_a)

    rh = r.reshape(B, S, H, N).astype(jnp.float32)
    kh = k.reshape(B, S, H, N).astype(jnp.float32)
    vh = v.reshape(B, S, H, N).astype(jnp.float32)
    wh = decay.reshape(B, S, H, N)
    ah = a.reshape(B, S, H, N).astype(jnp.float32)
    rem_a = -kk
    rem_b = kk * ah

    def step(state, inp):
        r_t, w_t, k_t, v_t, a_t, b_t = inp
        sa = jnp.einsum('bhij,bhj->bhi', state, a_t)
        state = (state * w_t[:, :, None, :] + sa[..., None] * b_t[:, :, None, :]
                 + v_t[..., None] * k_t[:, :, None, :])
        y_t = jnp.einsum('bhij,bhj->bhi', state, r_t)
        return state, y_t

    seq_inputs = tuple(jnp.moveaxis(t, 1, 0) for t in (rh, wh, kh, vh, rem_a, rem_b))
    state0 = jnp.zeros((B, H, N, N), jnp.float32)
    _, ys = lax.scan(step, state0, seq_inputs)
    y = jnp.moveaxis(ys, 0, 1)
    mean = jnp.mean(y, axis=-1, keepdims=True)
    var = jnp.mean(jnp.square(y - mean), axis=-1, keepdims=True)
    yn = ((y - mean) * lax.rsqrt(var + RWKV_GN_EPS)).reshape(B, S, D)
    yn = yn * gn_g.astype(jnp.float32) + gn_b.astype(jnp.float32)
    bonus = jnp.sum(rh * kh * r_k.astype(jnp.float32), axis=-1, keepdims=True) * vh
    out = (yn + bonus.reshape(B, S, D)).astype(x.dtype)
    return (out * g) @ w_o


def mem_cross_attention(h, mem_n, w_q, w_kv, w_o):
    B, S, D = h.shape
    M = mem_n.shape[1]
    q = (h @ w_q).reshape(B, S, MEM_HEADS, MEM_HEAD_DIM)
    kv = (mem_n @ w_kv).reshape(B, M, 2, MEM_HEADS, MEM_HEAD_DIM)
    k, v = kv[:, :, 0], kv[:, :, 1]
    s = jnp.einsum('bqhd,bkhd->bhqk', q, k).astype(jnp.float32) * (MEM_HEAD_DIM ** -0.5)
    p = jax.nn.softmax(s, axis=-1).astype(h.dtype)
    o = jnp.einsum('bhqk,bkhd->bqhd', p, v).reshape(B, S, D)
    return o @ w_o


def sqrelu_mlp(h, w_up, w_down):
    return jnp.square(jax.nn.relu(h @ w_up)) @ w_down


def _fwd_setup_inputs(seed: int = 0) -> dict:
    key = jax.random.key(seed)
    ks = iter(jax.random.split(key, 48))
    f32 = jnp.float32

    def nrm(shape, scale):
        return jax.random.normal(next(ks), shape, f32) * scale

    def unif(shape, lo, hi):
        return jax.random.uniform(next(ks), shape, f32, lo, hi)

    D, NA, NB = D_MODEL, N_A_LAYERS, N_B_LAYERS
    x = nrm((BATCH, SEQ, D), 1.0)
    mem = nrm((BATCH, MEM_LEN, D), 1.0)
    ln_gains = 1.0 + nrm((DEPTH, 6, D), 0.05)
    mem_norm = 1.0 + nrm((D,), 0.05)
    a_conv_w = nrm((NA, CONV_WIDTH, LRU_WIDTH), CONV_WIDTH ** -0.5)
    a_conv_b = nrm((NA, LRU_WIDTH), 0.01)
    a_w_in = nrm((NA, D, 2 * LRU_WIDTH), D ** -0.5)
    a_b_in = nrm((NA, 2 * LRU_WIDTH), 0.01)
    a_gate_w = nrm((NA, 2, LRU_HEADS, LRU_BLOCK, LRU_BLOCK), LRU_BLOCK ** -0.5)
    a_gate_b = nrm((NA, 2, LRU_HEADS, LRU_BLOCK), 0.01)
    u = unif((NA, LRU_WIDTH), 0.81, 0.998)
    sp = -0.5 * jnp.log(u)
    a_lambda = -jnp.log(jnp.expm1(sp))
    a_w_out = nrm((NA, LRU_WIDTH, D), LRU_WIDTH ** -0.5)
    a_b_out = nrm((NA, D), 0.01)
    b_mu = unif((NB, 6, D), 0.0, 1.0)
    b_w_rkv = nrm((NB, 3, D, D), D ** -0.5)
    b_w0 = unif((NB, D), -6.0, -1.0)
    b_w1 = nrm((NB, D, DECAY_LORA), D ** -0.5)
    b_w2 = nrm((NB, DECAY_LORA, D), 0.1 * DECAY_LORA ** -0.5)
    b_a0 = nrm((NB, D), 0.1)
    b_a1 = nrm((NB, D, AAA_LORA), D ** -0.5)
    b_a2 = nrm((NB, AAA_LORA, D), 0.1 * AAA_LORA ** -0.5)
    b_g1 = nrm((NB, D, GATE_LORA), D ** -0.5)
    b_g2 = nrm((NB, GATE_LORA, D), GATE_LORA ** -0.5)
    b_k_k = 0.85 + nrm((NB, D), 0.05)
    b_k_a = 1.0 + nrm((NB, D), 0.05)
    b_r_k = nrm((NB, RWKV_HEADS, RWKV_HEAD), 0.1)
    b_gn_g = 1.0 + nrm((NB, D), 0.05)
    b_gn_b = nrm((NB, D), 0.01)
    b_w_o = nrm((NB, D, D), D ** -0.5)
    c_w_q = nrm((DEPTH, D, D), D ** -0.5)
    c_w_kv = nrm((DEPTH, D, 2 * D), D ** -0.5)
    c_w_o = nrm((DEPTH, D, D), D ** -0.5)
    m_w_up = nrm((DEPTH, D, D_FF), D ** -0.5)
    m_w_down = nrm((DEPTH, D_FF, D), D_FF ** -0.5)
    return {"x": x, "mem": mem, "ln_gains": ln_gains, "mem_norm": mem_norm,
            "a_conv_w": a_conv_w, "a_conv_b": a_conv_b, "a_w_in": a_w_in, "a_b_in": a_b_in,
            "a_gate_w": a_gate_w, "a_gate_b": a_gate_b, "a_lambda": a_lambda,
            "a_w_out": a_w_out, "a_b_out": a_b_out,
            "b_mu": b_mu, "b_w_rkv": b_w_rkv, "b_w0": b_w0, "b_w1": b_w1, "b_w2": b_w2,
            "b_a0": b_a0, "b_a1": b_a1, "b_a2": b_a2, "b_g1": b_g1, "b_g2": b_g2,
            "b_k_k": b_k_k, "b_k_a": b_k_a, "b_r_k": b_r_k, "b_gn_g": b_gn_g, "b_gn_b": b_gn_b,
            "b_w_o": b_w_o,
            "c_w_q": c_w_q, "c_w_kv": c_w_kv, "c_w_o": c_w_o,
            "m_w_up": m_w_up, "m_w_down": m_w_down}


def _fwd_reference(x, mem, ln_gains, mem_norm,
              a_conv_w, a_conv_b, a_w_in, a_b_in, a_gate_w, a_gate_b, a_lambda, a_w_out, a_b_out,
              b_mu, b_w_rkv, b_w0, b_w1, b_w2, b_a0, b_a1, b_a2, b_g1, b_g2,
              b_k_k, b_k_a, b_r_k, b_gn_g, b_gn_b, b_w_o,
              c_w_q, c_w_kv, c_w_o, m_w_up, m_w_down):
    mem_n = rms_norm(mem, mem_norm)
    for i in range(DEPTH):
        g = ln_gains[i]
        j = i // N_MIXERS
        hn = rms_norm(x, g[0])
        if i % N_MIXERS == 0:
            t = rglru_mixer(hn, a_conv_w[j], a_conv_b[j], a_w_in[j], a_b_in[j], a_gate_w[j],
                            a_gate_b[j], a_lambda[j], a_w_out[j], a_b_out[j])
        else:
            t = rwkv7_mixer(hn, b_mu[j], b_w_rkv[j], b_w0[j], b_w1[j], b_w2[j], b_a0[j],
                            b_a1[j], b_a2[j], b_g1[j], b_g2[j], b_k_k[j], b_k_a[j], b_r_k[j],
                            b_gn_g[j], b_gn_b[j], b_w_o[j])
        x = x + rms_norm(t, g[1])
        c = mem_cross_attention(rms_norm(x, g[2]), mem_n, c_w_q[i], c_w_kv[i], c_w_o[i])
        x = x + rms_norm(c, g[3])
        m = sqrelu_mlp(rms_norm(x, g[4]), m_w_up[i], m_w_down[i])
        x = x + rms_norm(m, g[5])
    return x


import jax as _jax
import jax.numpy as _jnp

TWIN_FORMAT = 'train_step'
FWD_PARAMS = ['x', 'mem', 'ln_gains', 'mem_norm', 'a_conv_w', 'a_conv_b', 'a_w_in', 'a_b_in', 'a_gate_w', 'a_gate_b', 'a_lambda', 'a_w_out', 'a_b_out', 'b_mu', 'b_w_rkv', 'b_w0', 'b_w1', 'b_w2', 'b_a0', 'b_a1', 'b_a2', 'b_g1', 'b_g2', 'b_k_k', 'b_k_a', 'b_r_k', 'b_gn_g', 'b_gn_b', 'b_w_o', 'c_w_q', 'c_w_kv', 'c_w_o', 'm_w_up', 'm_w_down']
TWIN_WEIGHTS = ['ln_gains', 'mem_norm', 'a_conv_w', 'a_conv_b', 'a_w_in', 'a_b_in', 'a_gate_w', 'a_gate_b', 'a_lambda', 'a_w_out', 'a_b_out', 'b_mu', 'b_w_rkv', 'b_w0', 'b_w1', 'b_w2', 'b_a0', 'b_a1', 'b_a2', 'b_g1', 'b_g2', 'b_k_k', 'b_k_a', 'b_r_k', 'b_gn_g', 'b_gn_b', 'b_w_o', 'c_w_q', 'c_w_kv', 'c_w_o', 'm_w_up', 'm_w_down']
TWIN_DIFF_INPUT = 'x'
TWIN_INPUTS = ['x', 'mem', 'ln_gains', 'mem_norm', 'a_conv_w', 'a_conv_b', 'a_w_in', 'a_b_in', 'a_gate_w', 'a_gate_b', 'a_lambda', 'a_w_out', 'a_b_out', 'b_mu', 'b_w_rkv', 'b_w0', 'b_w1', 'b_w2', 'b_a0', 'b_a1', 'b_a2', 'b_g1', 'b_g2', 'b_k_k', 'b_k_a', 'b_r_k', 'b_gn_g', 'b_gn_b', 'b_w_o', 'c_w_q', 'c_w_kv', 'c_w_o', 'm_w_up', 'm_w_down', 'loss_target', 'm_ln_gains', 'm_mem_norm', 'm_a_conv_w', 'm_a_conv_b', 'm_a_w_in', 'm_a_b_in', 'm_a_gate_w', 'm_a_gate_b', 'm_a_lambda', 'm_a_w_out', 'm_a_b_out', 'm_b_mu', 'm_b_w_rkv', 'm_b_w0', 'm_b_w1', 'm_b_w2', 'm_b_a0', 'm_b_a1', 'm_b_a2', 'm_b_g1', 'm_b_g2', 'm_b_k_k', 'm_b_k_a', 'm_b_r_k', 'm_b_gn_g', 'm_b_gn_b', 'm_b_w_o', 'm_c_w_q', 'm_c_w_kv', 'm_c_w_o', 'm_m_w_up', 'm_m_w_down', 'v_ln_gains', 'v_mem_norm', 'v_a_conv_w', 'v_a_conv_b', 'v_a_w_in', 'v_a_b_in', 'v_a_gate_w', 'v_a_gate_b', 'v_a_lambda', 'v_a_w_out', 'v_a_b_out', 'v_b_mu', 'v_b_w_rkv', 'v_b_w0', 'v_b_w1', 'v_b_w2', 'v_b_a0', 'v_b_a1', 'v_b_a2', 'v_b_g1', 'v_b_g2', 'v_b_k_k', 'v_b_k_a', 'v_b_r_k', 'v_b_gn_g', 'v_b_gn_b', 'v_b_w_o', 'v_c_w_q', 'v_c_w_kv', 'v_c_w_o', 'v_m_w_up', 'v_m_w_down']
TWIN_OUTPUTS = ['loss', 'grad_x', 'grad_ln_gains', 'grad_mem_norm', 'grad_a_conv_w', 'grad_a_conv_b', 'grad_a_w_in', 'grad_a_b_in', 'grad_a_gate_w', 'grad_a_gate_b', 'grad_a_lambda', 'grad_a_w_out', 'grad_a_b_out', 'grad_b_mu', 'grad_b_w_rkv', 'grad_b_w0', 'grad_b_w1', 'grad_b_w2', 'grad_b_a0', 'grad_b_a1', 'grad_b_a2', 'grad_b_g1', 'grad_b_g2', 'grad_b_k_k', 'grad_b_k_a', 'grad_b_r_k', 'grad_b_gn_g', 'grad_b_gn_b', 'grad_b_w_o', 'grad_c_w_q', 'grad_c_w_kv', 'grad_c_w_o', 'grad_m_w_up', 'grad_m_w_down', 'delta_ln_gains', 'delta_mem_norm', 'delta_a_conv_w', 'delta_a_conv_b', 'delta_a_w_in', 'delta_a_b_in', 'delta_a_gate_w', 'delta_a_gate_b', 'delta_a_lambda', 'delta_a_w_out', 'delta_a_b_out', 'delta_b_mu', 'delta_b_w_rkv', 'delta_b_w0', 'delta_b_w1', 'delta_b_w2', 'delta_b_a0', 'delta_b_a1', 'delta_b_a2', 'delta_b_g1', 'delta_b_g2', 'delta_b_k_k', 'delta_b_k_a', 'delta_b_r_k', 'delta_b_gn_g', 'delta_b_gn_b', 'delta_b_w_o', 'delta_c_w_q', 'delta_c_w_kv', 'delta_c_w_o', 'delta_m_w_up', 'delta_m_w_down', 'new_m_ln_gains', 'new_m_mem_norm', 'new_m_a_conv_w', 'new_m_a_conv_b', 'new_m_a_w_in', 'new_m_a_b_in', 'new_m_a_gate_w', 'new_m_a_gate_b', 'new_m_a_lambda', 'new_m_a_w_out', 'new_m_a_b_out', 'new_m_b_mu', 'new_m_b_w_rkv', 'new_m_b_w0', 'new_m_b_w1', 'new_m_b_w2', 'new_m_b_a0', 'new_m_b_a1', 'new_m_b_a2', 'new_m_b_g1', 'new_m_b_g2', 'new_m_b_k_k', 'new_m_b_k_a', 'new_m_b_r_k', 'new_m_b_gn_g', 'new_m_b_gn_b', 'new_m_b_w_o', 'new_m_c_w_q', 'new_m_c_w_kv', 'new_m_c_w_o', 'new_m_m_w_up', 'new_m_m_w_down', 'new_v_ln_gains', 'new_v_mem_norm', 'new_v_a_conv_w', 'new_v_a_conv_b', 'new_v_a_w_in', 'new_v_a_b_in', 'new_v_a_gate_w', 'new_v_a_gate_b', 'new_v_a_lambda', 'new_v_a_w_out', 'new_v_a_b_out', 'new_v_b_mu', 'new_v_b_w_rkv', 'new_v_b_w0', 'new_v_b_w1', 'new_v_b_w2', 'new_v_b_a0', 'new_v_b_a1', 'new_v_b_a2', 'new_v_b_g1', 'new_v_b_g2', 'new_v_b_k_k', 'new_v_b_k_a', 'new_v_b_r_k', 'new_v_b_gn_g', 'new_v_b_gn_b', 'new_v_b_w_o', 'new_v_c_w_q', 'new_v_c_w_kv', 'new_v_c_w_o', 'new_v_m_w_up', 'new_v_m_w_down']
TWIN_LEAF_KINDS = {'loss': 'loss', 'grad_x': 'grad_x', 'grad_ln_gains': 'grad_w', 'grad_mem_norm': 'grad_w', 'grad_a_conv_w': 'grad_w', 'grad_a_conv_b': 'grad_w', 'grad_a_w_in': 'grad_w', 'grad_a_b_in': 'grad_w', 'grad_a_gate_w': 'grad_w', 'grad_a_gate_b': 'grad_w', 'grad_a_lambda': 'grad_w', 'grad_a_w_out': 'grad_w', 'grad_a_b_out': 'grad_w', 'grad_b_mu': 'grad_w', 'grad_b_w_rkv': 'grad_w', 'grad_b_w0': 'grad_w', 'grad_b_w1': 'grad_w', 'grad_b_w2': 'grad_w', 'grad_b_a0': 'grad_w', 'grad_b_a1': 'grad_w', 'grad_b_a2': 'grad_w', 'grad_b_g1': 'grad_w', 'grad_b_g2': 'grad_w', 'grad_b_k_k': 'grad_w', 'grad_b_k_a': 'grad_w', 'grad_b_r_k': 'grad_w', 'grad_b_gn_g': 'grad_w', 'grad_b_gn_b': 'grad_w', 'grad_b_w_o': 'grad_w', 'grad_c_w_q': 'grad_w', 'grad_c_w_kv': 'grad_w', 'grad_c_w_o': 'grad_w', 'grad_m_w_up': 'grad_w', 'grad_m_w_down': 'grad_w', 'delta_ln_gains': 'delta_w', 'delta_mem_norm': 'delta_w', 'delta_a_conv_w': 'delta_w', 'delta_a_conv_b': 'delta_w', 'delta_a_w_in': 'delta_w', 'delta_a_b_in': 'delta_w', 'delta_a_gate_w': 'delta_w', 'delta_a_gate_b': 'delta_w', 'delta_a_lambda': 'delta_w', 'delta_a_w_out': 'delta_w', 'delta_a_b_out': 'delta_w', 'delta_b_mu': 'delta_w', 'delta_b_w_rkv': 'delta_w', 'delta_b_w0': 'delta_w', 'delta_b_w1': 'delta_w', 'delta_b_w2': 'delta_w', 'delta_b_a0': 'delta_w', 'delta_b_a1': 'delta_w', 'delta_b_a2': 'delta_w', 'delta_b_g1': 'delta_w', 'delta_b_g2': 'delta_w', 'delta_b_k_k': 'delta_w', 'delta_b_k_a': 'delta_w', 'delta_b_r_k': 'delta_w', 'delta_b_gn_g': 'delta_w', 'delta_b_gn_b': 'delta_w', 'delta_b_w_o': 'delta_w', 'delta_c_w_q': 'delta_w', 'delta_c_w_kv': 'delta_w', 'delta_c_w_o': 'delta_w', 'delta_m_w_up': 'delta_w', 'delta_m_w_down': 'delta_w', 'new_m_ln_gains': 'new_m', 'new_m_mem_norm': 'new_m', 'new_m_a_conv_w': 'new_m', 'new_m_a_conv_b': 'new_m', 'new_m_a_w_in': 'new_m', 'new_m_a_b_in': 'new_m', 'new_m_a_gate_w': 'new_m', 'new_m_a_gate_b': 'new_m', 'new_m_a_lambda': 'new_m', 'new_m_a_w_out': 'new_m', 'new_m_a_b_out': 'new_m', 'new_m_b_mu': 'new_m', 'new_m_b_w_rkv': 'new_m', 'new_m_b_w0': 'new_m', 'new_m_b_w1': 'new_m', 'new_m_b_w2': 'new_m', 'new_m_b_a0': 'new_m', 'new_m_b_a1': 'new_m', 'new_m_b_a2': 'new_m', 'new_m_b_g1': 'new_m', 'new_m_b_g2': 'new_m', 'new_m_b_k_k': 'new_m', 'new_m_b_k_a': 'new_m', 'new_m_b_r_k': 'new_m', 'new_m_b_gn_g': 'new_m', 'new_m_b_gn_b': 'new_m', 'new_m_b_w_o': 'new_m', 'new_m_c_w_q': 'new_m', 'new_m_c_w_kv': 'new_m', 'new_m_c_w_o': 'new_m', 'new_m_m_w_up': 'new_m', 'new_m_m_w_down': 'new_m', 'new_v_ln_gains': 'new_v', 'new_v_mem_norm': 'new_v', 'new_v_a_conv_w': 'new_v', 'new_v_a_conv_b': 'new_v', 'new_v_a_w_in': 'new_v', 'new_v_a_b_in': 'new_v', 'new_v_a_gate_w': 'new_v', 'new_v_a_gate_b': 'new_v', 'new_v_a_lambda': 'new_v', 'new_v_a_w_out': 'new_v', 'new_v_a_b_out': 'new_v', 'new_v_b_mu': 'new_v', 'new_v_b_w_rkv': 'new_v', 'new_v_b_w0': 'new_v', 'new_v_b_w1': 'new_v', 'new_v_b_w2': 'new_v', 'new_v_b_a0': 'new_v', 'new_v_b_a1': 'new_v', 'new_v_b_a2': 'new_v', 'new_v_b_g1': 'new_v', 'new_v_b_g2': 'new_v', 'new_v_b_k_k': 'new_v', 'new_v_b_k_a': 'new_v', 'new_v_b_r_k': 'new_v', 'new_v_b_gn_g': 'new_v', 'new_v_b_gn_b': 'new_v', 'new_v_b_w_o': 'new_v', 'new_v_c_w_q': 'new_v', 'new_v_c_w_kv': 'new_v', 'new_v_c_w_o': 'new_v', 'new_v_m_w_up': 'new_v', 'new_v_m_w_down': 'new_v'}


def _forward(args):
    return _fwd_reference(*[args[k] for k in FWD_PARAMS])


def _output_shape():
    out = _jax.eval_shape(lambda: _forward(_fwd_setup_inputs(0)))
    return out.shape, out.dtype

N_MICROBATCH = 1
ADAM_LR = 0.001
ADAM_B1 = 0.9
ADAM_B2 = 0.999
ADAM_EPS = 1e-08
ADAM_WD = 0.01
ADAM_STEP = 10
PER_EXAMPLE_BATCH_AXIS = {'x': 0, 'mem': 0, 'loss_target': 0}
SHARED_INPUTS = []
_WEIGHT_DTYPES = {'ln_gains': _jnp.float32, 'mem_norm': _jnp.float32, 'a_conv_w': _jnp.float32, 'a_conv_b': _jnp.float32, 'a_w_in': _jnp.float32, 'a_b_in': _jnp.float32, 'a_gate_w': _jnp.float32, 'a_gate_b': _jnp.float32, 'a_lambda': _jnp.float32, 'a_w_out': _jnp.float32, 'a_b_out': _jnp.float32, 'b_mu': _jnp.float32, 'b_w_rkv': _jnp.float32, 'b_w0': _jnp.float32, 'b_w1': _jnp.float32, 'b_w2': _jnp.float32, 'b_a0': _jnp.float32, 'b_a1': _jnp.float32, 'b_a2': _jnp.float32, 'b_g1': _jnp.float32, 'b_g2': _jnp.float32, 'b_k_k': _jnp.float32, 'b_k_a': _jnp.float32, 'b_r_k': _jnp.float32, 'b_gn_g': _jnp.float32, 'b_gn_b': _jnp.float32, 'b_w_o': _jnp.float32, 'c_w_q': _jnp.float32, 'c_w_kv': _jnp.float32, 'c_w_o': _jnp.float32, 'm_w_up': _jnp.float32, 'm_w_down': _jnp.float32}
MOMENT_SCALE = {'ln_gains': 4.664384e+01, 'mem_norm': 2.507860e+01, 'a_conv_w': 3.874623e+00, 'a_conv_b': 9.518693e+01, 'a_w_in': 2.949471e+00, 'a_b_in': 5.524169e+01, 'a_gate_w': 1.707514e+00, 'a_gate_b': 1.330626e+00, 'a_lambda': 1.586744e+00, 'a_w_out': 4.349199e+00, 'a_b_out': 1.427861e+02, 'b_mu': 3.538732e+00, 'b_w_rkv': 5.690704e+00, 'b_w0': 4.033648e+00, 'b_w1': 4.172545e-01, 'b_w2': 9.297039e-01, 'b_a0': 1.265590e+00, 'b_a1': 4.850384e-01, 'b_a2': 1.191464e+00, 'b_g1': 2.448878e+00, 'b_g2': 5.910403e+00, 'b_k_k': 2.953143e+01, 'b_k_a': 2.965900e+01, 'b_r_k': 3.847755e+00, 'b_gn_g': 5.874227e+00, 'b_gn_b': 4.591005e+01, 'b_w_o': 5.419966e+00, 'c_w_q': 4.069149e+00, 'c_w_kv': 1.248388e+01, 'c_w_o': 1.847290e+01, 'm_w_up': 4.013674e+00, 'm_w_down': 1.762688e+01}


def _to_microbatches(a, axis):
    t = _jnp.moveaxis(a, axis, 0)
    t = t.reshape((N_MICROBATCH, t.shape[0] // N_MICROBATCH) + t.shape[1:])
    return _jnp.moveaxis(t, 1, axis + 1)


def setup_inputs(seed: int = 0) -> dict:
    inp = _fwd_setup_inputs(seed)
    key = _jax.random.fold_in(_jax.random.key(seed), 7919)
    shape, _ = _output_shape()
    out = dict(inp)
    out["loss_target"] = _jax.random.normal(_jax.random.fold_in(key, 0), shape, _jnp.float32)
    for i, name in enumerate(TWIN_WEIGHTS):
        w = inp[name].astype(_jnp.float32)
        if MOMENT_SCALE is None:
            s = _jnp.sqrt(_jnp.mean(_jnp.square(w)) + 1e-30)
        else:
            s = MOMENT_SCALE[name]
        km, kv = _jax.random.split(_jax.random.fold_in(key, i + 1))
        out[name] = w
        out["m_" + name] = s * _jax.random.normal(km, w.shape, _jnp.float32)
        out["v_" + name] = (s * s) * _jax.random.uniform(kv, w.shape, _jnp.float32, 0.5, 1.5)
    if N_MICROBATCH > 1:
        for name, axis in PER_EXAMPLE_BATCH_AXIS.items():
            out[name] = _to_microbatches(out[name], axis)
    return {'x': out['x'], 'mem': out['mem'], 'ln_gains': out['ln_gains'], 'mem_norm': out['mem_norm'], 'a_conv_w': out['a_conv_w'], 'a_conv_b': out['a_conv_b'], 'a_w_in': out['a_w_in'], 'a_b_in': out['a_b_in'], 'a_gate_w': out['a_gate_w'], 'a_gate_b': out['a_gate_b'], 'a_lambda': out['a_lambda'], 'a_w_out': out['a_w_out'], 'a_b_out': out['a_b_out'], 'b_mu': out['b_mu'], 'b_w_rkv': out['b_w_rkv'], 'b_w0': out['b_w0'], 'b_w1': out['b_w1'], 'b_w2': out['b_w2'], 'b_a0': out['b_a0'], 'b_a1': out['b_a1'], 'b_a2': out['b_a2'], 'b_g1': out['b_g1'], 'b_g2': out['b_g2'], 'b_k_k': out['b_k_k'], 'b_k_a': out['b_k_a'], 'b_r_k': out['b_r_k'], 'b_gn_g': out['b_gn_g'], 'b_gn_b': out['b_gn_b'], 'b_w_o': out['b_w_o'], 'c_w_q': out['c_w_q'], 'c_w_kv': out['c_w_kv'], 'c_w_o': out['c_w_o'], 'm_w_up': out['m_w_up'], 'm_w_down': out['m_w_down'], 'loss_target': out['loss_target'], 'm_ln_gains': out['m_ln_gains'], 'm_mem_norm': out['m_mem_norm'], 'm_a_conv_w': out['m_a_conv_w'], 'm_a_conv_b': out['m_a_conv_b'], 'm_a_w_in': out['m_a_w_in'], 'm_a_b_in': out['m_a_b_in'], 'm_a_gate_w': out['m_a_gate_w'], 'm_a_gate_b': out['m_a_gate_b'], 'm_a_lambda': out['m_a_lambda'], 'm_a_w_out': out['m_a_w_out'], 'm_a_b_out': out['m_a_b_out'], 'm_b_mu': out['m_b_mu'], 'm_b_w_rkv': out['m_b_w_rkv'], 'm_b_w0': out['m_b_w0'], 'm_b_w1': out['m_b_w1'], 'm_b_w2': out['m_b_w2'], 'm_b_a0': out['m_b_a0'], 'm_b_a1': out['m_b_a1'], 'm_b_a2': out['m_b_a2'], 'm_b_g1': out['m_b_g1'], 'm_b_g2': out['m_b_g2'], 'm_b_k_k': out['m_b_k_k'], 'm_b_k_a': out['m_b_k_a'], 'm_b_r_k': out['m_b_r_k'], 'm_b_gn_g': out['m_b_gn_g'], 'm_b_gn_b': out['m_b_gn_b'], 'm_b_w_o': out['m_b_w_o'], 'm_c_w_q': out['m_c_w_q'], 'm_c_w_kv': out['m_c_w_kv'], 'm_c_w_o': out['m_c_w_o'], 'm_m_w_up': out['m_m_w_up'], 'm_m_w_down': out['m_m_w_down'], 'v_ln_gains': out['v_ln_gains'], 'v_mem_norm': out['v_mem_norm'], 'v_a_conv_w': out['v_a_conv_w'], 'v_a_conv_b': out['v_a_conv_b'], 'v_a_w_in': out['v_a_w_in'], 'v_a_b_in': out['v_a_b_in'], 'v_a_gate_w': out['v_a_gate_w'], 'v_a_gate_b': out['v_a_gate_b'], 'v_a_lambda': out['v_a_lambda'], 'v_a_w_out': out['v_a_w_out'], 'v_a_b_out': out['v_a_b_out'], 'v_b_mu': out['v_b_mu'], 'v_b_w_rkv': out['v_b_w_rkv'], 'v_b_w0': out['v_b_w0'], 'v_b_w1': out['v_b_w1'], 'v_b_w2': out['v_b_w2'], 'v_b_a0': out['v_b_a0'], 'v_b_a1': out['v_b_a1'], 'v_b_a2': out['v_b_a2'], 'v_b_g1': out['v_b_g1'], 'v_b_g2': out['v_b_g2'], 'v_b_k_k': out['v_b_k_k'], 'v_b_k_a': out['v_b_k_a'], 'v_b_r_k': out['v_b_r_k'], 'v_b_gn_g': out['v_b_gn_g'], 'v_b_gn_b': out['v_b_gn_b'], 'v_b_w_o': out['v_b_w_o'], 'v_c_w_q': out['v_c_w_q'], 'v_c_w_kv': out['v_c_w_kv'], 'v_c_w_o': out['v_c_w_o'], 'v_m_w_up': out['v_m_w_up'], 'v_m_w_down': out['v_m_w_down']}


def _loss(weights, diff, rest, loss_target):
    with _jax.named_scope("forward"):
        args = {**rest, TWIN_DIFF_INPUT: diff, **{k: w.astype(_WEIGHT_DTYPES[k]) for k, w in weights.items()}}
        y = _forward(args)
    with _jax.named_scope("loss_head"):
        err = _jnp.square(y.astype(_jnp.float32) - loss_target)
        return 0.5 * _jnp.sum(_jnp.mean(err, axis=-1)) if err.ndim else 0.5 * err


def _adamw(w, g, m, v):
    m = ADAM_B1 * m + (1.0 - ADAM_B1) * g
    v = ADAM_B2 * v + (1.0 - ADAM_B2) * _jnp.square(g)
    m_hat = m / (1.0 - ADAM_B1 ** ADAM_STEP)
    v_hat = v / (1.0 - ADAM_B2 ** ADAM_STEP)
    delta = -ADAM_LR * (m_hat / (_jnp.sqrt(v_hat) + ADAM_EPS) + ADAM_WD * w)
    return delta, m, v


def reference(x, mem, ln_gains, mem_norm, a_conv_w, a_conv_b, a_w_in, a_b_in, a_gate_w, a_gate_b, a_lambda, a_w_out, a_b_out, b_mu, b_w_rkv, b_w0, b_w1, b_w2, b_a0, b_a1, b_a2, b_g1, b_g2, b_k_k, b_k_a, b_r_k, b_gn_g, b_gn_b, b_w_o, c_w_q, c_w_kv, c_w_o, m_w_up, m_w_down, loss_target, m_ln_gains, m_mem_norm, m_a_conv_w, m_a_conv_b, m_a_w_in, m_a_b_in, m_a_gate_w, m_a_gate_b, m_a_lambda, m_a_w_out, m_a_b_out, m_b_mu, m_b_w_rkv, m_b_w0, m_b_w1, m_b_w2, m_b_a0, m_b_a1, m_b_a2, m_b_g1, m_b_g2, m_b_k_k, m_b_k_a, m_b_r_k, m_b_gn_g, m_b_gn_b, m_b_w_o, m_c_w_q, m_c_w_kv, m_c_w_o, m_m_w_up, m_m_w_down, v_ln_gains, v_mem_norm, v_a_conv_w, v_a_conv_b, v_a_w_in, v_a_b_in, v_a_gate_w, v_a_gate_b, v_a_lambda, v_a_w_out, v_a_b_out, v_b_mu, v_b_w_rkv, v_b_w0, v_b_w1, v_b_w2, v_b_a0, v_b_a1, v_b_a2, v_b_g1, v_b_g2, v_b_k_k, v_b_k_a, v_b_r_k, v_b_gn_g, v_b_gn_b, v_b_w_o, v_c_w_q, v_c_w_kv, v_c_w_o, v_m_w_up, v_m_w_down):
    given = dict(x=x, mem=mem, ln_gains=ln_gains, mem_norm=mem_norm, a_conv_w=a_conv_w, a_conv_b=a_conv_b, a_w_in=a_w_in, a_b_in=a_b_in, a_gate_w=a_gate_w, a_gate_b=a_gate_b, a_lambda=a_lambda, a_w_out=a_w_out, a_b_out=a_b_out, b_mu=b_mu, b_w_rkv=b_w_rkv, b_w0=b_w0, b_w1=b_w1, b_w2=b_w2, b_a0=b_a0, b_a1=b_a1, b_a2=b_a2, b_g1=b_g1, b_g2=b_g2, b_k_k=b_k_k, b_k_a=b_k_a, b_r_k=b_r_k, b_gn_g=b_gn_g, b_gn_b=b_gn_b, b_w_o=b_w_o, c_w_q=c_w_q, c_w_kv=c_w_kv, c_w_o=c_w_o, m_w_up=m_w_up, m_w_down=m_w_down, loss_target=loss_target, m_ln_gains=m_ln_gains, m_mem_norm=m_mem_norm, m_a_conv_w=m_a_conv_w, m_a_conv_b=m_a_conv_b, m_a_w_in=m_a_w_in, m_a_b_in=m_a_b_in, m_a_gate_w=m_a_gate_w, m_a_gate_b=m_a_gate_b, m_a_lambda=m_a_lambda, m_a_w_out=m_a_w_out, m_a_b_out=m_a_b_out, m_b_mu=m_b_mu, m_b_w_rkv=m_b_w_rkv, m_b_w0=m_b_w0, m_b_w1=m_b_w1, m_b_w2=m_b_w2, m_b_a0=m_b_a0, m_b_a1=m_b_a1, m_b_a2=m_b_a2, m_b_g1=m_b_g1, m_b_g2=m_b_g2, m_b_k_k=m_b_k_k, m_b_k_a=m_b_k_a, m_b_r_k=m_b_r_k, m_b_gn_g=m_b_gn_g, m_b_gn_b=m_b_gn_b, m_b_w_o=m_b_w_o, m_c_w_q=m_c_w_q, m_c_w_kv=m_c_w_kv, m_c_w_o=m_c_w_o, m_m_w_up=m_m_w_up, m_m_w_down=m_m_w_down, v_ln_gains=v_ln_gains, v_mem_norm=v_mem_norm, v_a_conv_w=v_a_conv_w, v_a_conv_b=v_a_conv_b, v_a_w_in=v_a_w_in, v_a_b_in=v_a_b_in, v_a_gate_w=v_a_gate_w, v_a_gate_b=v_a_gate_b, v_a_lambda=v_a_lambda, v_a_w_out=v_a_w_out, v_a_b_out=v_a_b_out, v_b_mu=v_b_mu, v_b_w_rkv=v_b_w_rkv, v_b_w0=v_b_w0, v_b_w1=v_b_w1, v_b_w2=v_b_w2, v_b_a0=v_b_a0, v_b_a1=v_b_a1, v_b_a2=v_b_a2, v_b_g1=v_b_g1, v_b_g2=v_b_g2, v_b_k_k=v_b_k_k, v_b_k_a=v_b_k_a, v_b_r_k=v_b_r_k, v_b_gn_g=v_b_gn_g, v_b_gn_b=v_b_gn_b, v_b_w_o=v_b_w_o, v_c_w_q=v_c_w_q, v_c_w_kv=v_c_w_kv, v_c_w_o=v_c_w_o, v_m_w_up=v_m_w_up, v_m_w_down=v_m_w_down)
    weights = {n: given[n] for n in TWIN_WEIGHTS}
    shared = {n: given[n] for n in SHARED_INPUTS}
    per_example = {n: given[n] for n in ['x', 'mem']}
    grad_fn = _jax.value_and_grad(_loss, argnums=(0, 1))

    def one_microbatch(ex, loss_target):
        ex = dict(ex)
        diff = ex.pop(TWIN_DIFF_INPUT)
        return grad_fn(weights, diff, {**shared, **ex}, loss_target)

    if N_MICROBATCH == 1:
        loss, (grad_w, grad_x) = one_microbatch(per_example, given["loss_target"])
    else:
        def body(carry, xs):
            loss_sum, grad_sum = carry
            l_k, (gw_k, gx_k) = one_microbatch(xs[0], xs[1])
            with _jax.named_scope("update"):
                return (loss_sum + l_k, _jax.tree.map(_jnp.add, grad_sum, gw_k)), gx_k

        init = (_jnp.zeros((), _jnp.float32), _jax.tree.map(_jnp.zeros_like, weights))
        (loss, grad_w), grad_x = _jax.lax.scan(body, init, (per_example, given["loss_target"]))
    with _jax.named_scope("update"):
        delta_w, new_m, new_v = {}, {}, {}
        for n in TWIN_WEIGHTS:
            delta_w[n], new_m[n], new_v[n] = _adamw(weights[n], grad_w[n], given["m_" + n], given["v_" + n])
    return (loss, grad_x, *[grad_w[n] for n in TWIN_WEIGHTS], *[delta_w[n] for n in TWIN_WEIGHTS],
            *[new_m[n] for n in TWIN_WEIGHTS], *[new_v[n] for n in TWIN_WEIGHTS])
```

```python
import functools

import jax
import jax.numpy as jnp
from jax import lax
from jax.experimental import pallas as pl
from jax.experimental.pallas import tpu as pltpu

F32 = jnp.float32
BF16 = jnp.bfloat16
MESH = pl.DeviceIdType.MESH

LANES = 128
SUBLANES = 8
VMEM_LIMIT_BYTES = 48 * 1024 * 1024

RMS_EPS = 1e-6
LRU_C = 8.0
LRU_HEADS = 4
CONV_WIDTH = 4
RWKV_HEAD = 64
RWKV_GN_EPS = 64e-5
MEM_HEADS = 4
ADAM_LR = 0.001
ADAM_B1 = 0.9
ADAM_B2 = 0.999
ADAM_EPS = 1e-08
ADAM_WD = 0.01
ADAM_STEP = 10
WKV_CHUNK = 16

_PARAMS = functools.partial(pltpu.CompilerParams, vmem_limit_bytes=VMEM_LIMIT_BYTES)


def _tile(n, want):
    if n <= want:
        return n
    t = want
    while t >= SUBLANES:
        if n % t == 0 and t % SUBLANES == 0:
            return t
        t -= SUBLANES
    return n


def _fold8(v):
    tm, d = v.shape
    if tm == SUBLANES:
        return v
    return jnp.sum(v.reshape(tm // SUBLANES, SUBLANES, d), axis=0)


def _rowwise(name, body, rows, consts=(), out_rows=(), out_accs=(), prev=(), nxt=(), tm=256, seq=None):
    rows = [r if isinstance(r, tuple) else (r, r.shape[1], 0) for r in rows]
    t = rows[0][0].shape[0]
    tm = _tile(t, tm)
    if seq is not None:
        tm = _tile(seq, tm)
    nblk = t // tm
    nrow, ncst, nprev, nnxt = len(rows), len(consts), len(prev), len(nxt)
    nor, noa = len(out_rows), len(out_accs)
    hb = tm // SUBLANES

    def kern(*refs):
        i = pl.program_id(0)
        rv = [r[...] for r in refs[:nrow]]
        cv = [c[...] for c in refs[nrow:nrow + ncst]]
        o = nrow + ncst
        pv = []
        for j in range(nprev):
            at_start = (i * tm) % seq == 0
            h = refs[o + j][...]
            pv.append(jnp.where(at_start, jnp.zeros_like(h), h))
        o += nprev
        nv = []
        for j in range(nnxt):
            at_end = ((i + 1) * tm) % seq == 0
            h = refs[o + j][...]
            nv.append(jnp.where(at_end, jnp.zeros_like(h), h))
        o += nnxt
        outs, accs = body(rv, cv, pv, nv)
        for j in range(nor):
            refs[o + j][...] = outs[j].astype(refs[o + j].dtype)
        o += nor
        if noa:
            @pl.when(i == 0)
            def _():
                for j in range(noa):
                    refs[o + j][...] = jnp.zeros_like(refs[o + j])
            for j in range(noa):
                refs[o + j][...] += _fold8(accs[j].astype(F32))

    in_specs = [pl.BlockSpec((tm, w), functools.partial(lambda i, c: (i, c), c=cb)) for (_, w, cb) in rows]
    in_specs += [pl.BlockSpec(c.shape, lambda i: (0, 0)) for c in consts]
    in_specs += [pl.BlockSpec((SUBLANES, rows[j][1]),
                              functools.partial(lambda i, c: (jnp.maximum(i * hb - 1, 0), c), c=rows[j][2])) for j in prev]
    in_specs += [pl.BlockSpec((SUBLANES, rows[j][1]),
                              functools.partial(lambda i, c: (jnp.minimum((i + 1) * hb, t // SUBLANES - 1), c), c=rows[j][2]))
                 for j in nxt]
    out_shape = [jax.ShapeDtypeStruct((t, w), dt) for (w, dt) in out_rows]
    out_shape += [jax.ShapeDtypeStruct((SUBLANES, w), F32) for w in out_accs]
    out_specs = [pl.BlockSpec((tm, w), lambda i: (i, 0)) for (w, _) in out_rows]
    out_specs += [pl.BlockSpec((SUBLANES, w), lambda i: (0, 0)) for w in out_accs]
    args = [r[0] for r in rows] + list(consts) + [rows[j][0] for j in prev] + [rows[j][0] for j in nxt]
    res = pl.pallas_call(
        kern, name=name, grid=(nblk,), in_specs=in_specs, out_specs=out_specs, out_shape=out_shape,
        compiler_params=_PARAMS(dimension_semantics=("arbitrary",)),
    )(*args)
    return list(res[:nor]), list(res[nor:])


def _shift_down(x, halo, k):
    rolled = pltpu.roll(x, k, 0)
    row = lax.broadcasted_iota(jnp.int32, (SUBLANES, x.shape[1]), 0)
    first = jnp.where(row < k, pltpu.roll(halo, k, 0), rolled[:SUBLANES])
    if x.shape[0] == SUBLANES:
        return first
    return jnp.concatenate([first, rolled[SUBLANES:]], axis=0)


def _shift_up(x, halo, k):
    n = x.shape[0]
    rolled = pltpu.roll(x, n - k, 0)
    row = lax.broadcasted_iota(jnp.int32, (SUBLANES, x.shape[1]), 0)
    last = jnp.where(row >= SUBLANES - k, pltpu.roll(halo, SUBLANES - k, 0), rolled[n - SUBLANES:])
    if n == SUBLANES:
        return last
    return jnp.concatenate([rolled[:n - SUBLANES], last], axis=0)


def _mm(name, a, b, out_dtype=F32, trans_a=False, tm=512, tn=512, tk=1024):
    if trans_a:
        kdim, m = a.shape
    else:
        m, kdim = a.shape
    n = b.shape[1]
    assert b.shape[0] == kdim, (name, a.shape, b.shape)
    tm, tn, tk = _tile(m, tm), _tile(n, tn), _tile(kdim, tk)
    nk = kdim // tk
    dims = (((0,), (0,)), ((), ())) if trans_a else (((1,), (0,)), ((), ()))

    def kern(a_ref, b_ref, o_ref, *acc):
        part = lax.dot_general(a_ref[...].astype(BF16), b_ref[...].astype(BF16), dims, preferred_element_type=F32)
        if nk == 1:
            o_ref[...] = part.astype(o_ref.dtype)
        else:
            k = pl.program_id(2)

            @pl.when(k == 0)
            def _():
                acc[0][...] = part

            @pl.when(k > 0)
            def _():
                acc[0][...] += part

            @pl.when(k == nk - 1)
            def _():
                o_ref[...] = acc[0][...].astype(o_ref.dtype)

    a_spec = pl.BlockSpec((tk, tm), lambda i, j, k: (k, i)) if trans_a else pl.BlockSpec((tm, tk), lambda i, j, k: (i, k))
    return pl.pallas_call(
        kern, name=name, grid=(m // tm, n // tn, nk),
        in_specs=[a_spec, pl.BlockSpec((tk, tn), lambda i, j, k: (k, j))],
        out_specs=pl.BlockSpec((tm, tn), lambda i, j, k: (i, j)),
        out_shape=jax.ShapeDtypeStruct((m, n), out_dtype),
        scratch_shapes=[] if nk == 1 else [pltpu.VMEM((tm, tn), F32)],
        compiler_params=_PARAMS(dimension_semantics=("parallel", "parallel", "arbitrary")),
    )(a, b)


def _scan(name, a, b, seq, reverse=False, tm=256):
    t, d = a.shape
    tm = _tile(seq, tm)
    nblk = t // tm
    ntile = tm // SUBLANES

    def kern(a_ref, b_ref, h_ref, carry_h, carry_a):
        i = pl.program_id(0)
        blk = (nblk - 1 - i) if reverse else i
        edge = (((blk + 1) * tm) % seq == 0) if reverse else ((blk * tm) % seq == 0)

        @pl.when(edge)
        def _():
            carry_h[...] = jnp.zeros_like(carry_h)
            carry_a[...] = jnp.zeros_like(carry_a)

        def tile_step(j, c):
            jj = (ntile - 1 - j) if reverse else j
            rows = pl.ds(pl.multiple_of(jj * SUBLANES, SUBLANES), SUBLANES)
            a8 = a_ref[rows, :]
            b8 = b_ref[rows, :]
            h, an = c
            out = [None] * SUBLANES
            order = range(SUBLANES - 1, -1, -1) if reverse else range(SUBLANES)
            for r in order:
                if reverse:
                    h = b8[r:r + 1, :] + an * h
                    an = a8[r:r + 1, :]
                else:
                    h = a8[r:r + 1, :] * h + b8[r:r + 1, :]
                out[r] = h
            h_ref[rows, :] = jnp.concatenate(out, axis=0)
            return (h, an)

        h, an = lax.fori_loop(0, ntile, tile_step, (carry_h[...], carry_a[...]))
        carry_h[...] = h
        carry_a[...] = an

    idx = (lambda i: (nblk - 1 - i, 0)) if reverse else (lambda i: (i, 0))
    return pl.pallas_call(
        kern, name=name, grid=(nblk,),
        in_specs=[pl.BlockSpec((tm, d), idx), pl.BlockSpec((tm, d), idx)],
        out_specs=pl.BlockSpec((tm, d), idx),
        out_shape=jax.ShapeDtypeStruct((t, d), F32),
        scratch_shapes=[pltpu.VMEM((1, d), F32), pltpu.VMEM((1, d), F32)],
        compiler_params=_PARAMS(dimension_semantics=("arbitrary",)),
    )(a, b)


def _eye(n):
    return lax.broadcasted_iota(jnp.int32, (n, n), 0) == lax.broadcasted_iota(jnp.int32, (n, n), 1)


def _col_of_row(row, eye):
    return jnp.sum(jnp.where(eye, row, 0.0), axis=1, keepdims=True)


def _row_of_col(col, eye):
    return jnp.sum(jnp.where(eye, col, 0.0), axis=0, keepdims=True)


def _wkv_fwd(r, w, k, v, a, b, hb=8):
    bh, nc, ch, n = r.shape
    hb = _tile(bh, hb) if bh % hb else hb

    def kern(r_ref, w_ref, k_ref, v_ref, a_ref, b_ref, y_ref, st_ref, s_scr):
        @pl.when(pl.program_id(1) == 0)
        def _():
            s_scr[...] = jnp.zeros_like(s_scr)

        eye = _eye(n)
        for s in range(ch):
            for h in range(hb):
                p = s_scr[h]
                st_ref[h, 0, s] = p
                row = lambda ref: ref[h, 0, s:s + 1, :]
                u = jnp.sum(p * row(a_ref), axis=1, keepdims=True)
                vcol = _col_of_row(row(v_ref), eye)
                snew = p * row(w_ref) + u * row(b_ref) + vcol * row(k_ref)
                s_scr[h] = snew
                y = jnp.sum(snew * row(r_ref), axis=1, keepdims=True)
                y_ref[h, 0, s:s + 1, :] = _row_of_col(y, eye)

    vec = pl.BlockSpec((hb, 1, ch, n), lambda g, c: (g, c, 0, 0))
    return pl.pallas_call(
        kern, name="wkv_fwd", grid=(bh // hb, nc),
        in_specs=[vec] * 6,
        out_specs=[vec, pl.BlockSpec((hb, 1, ch, n, n), lambda g, c: (g, c, 0, 0, 0))],
        out_shape=[jax.ShapeDtypeStruct((bh, nc, ch, n), F32), jax.ShapeDtypeStruct((bh, nc, ch, n, n), F32)],
        scratch_shapes=[pltpu.VMEM((hb, n, n), F32)],
        compiler_params=_PARAMS(dimension_semantics=("parallel", "arbitrary")),
    )(r, w, k, v, a, b)


def _wkv_bwd(r, w, k, v, a, b, st, dy, hb=8):
    bh, nc, ch, n = r.shape
    hb = _tile(bh, hb) if bh % hb else hb

    def kern(r_ref, w_ref, k_ref, v_ref, a_ref, b_ref, st_ref, dy_ref,
             dr_ref, dw_ref, dk_ref, dv_ref, da_ref, db_ref, ds_scr):
        @pl.when(pl.program_id(1) == 0)
        def _():
            ds_scr[...] = jnp.zeros_like(ds_scr)

        eye = _eye(n)
        for s in range(ch - 1, -1, -1):
            for h in range(hb):
                row = lambda ref: ref[h, 0, s:s + 1, :]
                put = lambda ref, val: ref.__setitem__((h, 0, slice(s, s + 1), slice(None)), val)
                p = st_ref[h, 0, s]
                r_, w_, k_, a_, b_ = row(r_ref), row(w_ref), row(k_ref), row(a_ref), row(b_ref)
                vcol = _col_of_row(row(v_ref), eye)
                dycol = _col_of_row(row(dy_ref), eye)
                u = jnp.sum(p * a_, axis=1, keepdims=True)
                snew = p * w_ + u * b_ + vcol * k_
                ds = ds_scr[h] + dycol * r_
                put(dr_ref, jnp.sum(snew * dycol, axis=0, keepdims=True))
                put(dv_ref, _row_of_col(jnp.sum(ds * k_, axis=1, keepdims=True), eye))
                put(dk_ref, jnp.sum(ds * vcol, axis=0, keepdims=True))
                du = jnp.sum(ds * b_, axis=1, keepdims=True)
                put(db_ref, jnp.sum(ds * u, axis=0, keepdims=True))
                put(dw_ref, jnp.sum(ds * p, axis=0, keepdims=True))
                put(da_ref, jnp.sum(p * du, axis=0, keepdims=True))
                ds_scr[h] = ds * w_ + du * a_

    vec = pl.BlockSpec((hb, 1, ch, n), lambda g, c: (g, nc - 1 - c, 0, 0))
    return pl.pallas_call(
        kern, name="wkv_bwd", grid=(bh // hb, nc),
        in_specs=[vec] * 6 + [pl.BlockSpec((hb, 1, ch, n, n), lambda g, c: (g, nc - 1 - c, 0, 0, 0)), vec],
        out_specs=[vec] * 6,
        out_shape=[jax.ShapeDtypeStruct((bh, nc, ch, n), F32)] * 6,
        scratch_shapes=[pltpu.VMEM((hb, n, n), F32)],
        compiler_params=_PARAMS(dimension_semantics=("parallel", "arbitrary")),
    )(r, w, k, v, a, b, st, dy)


_NT = (((1,), (1,)), ((), ()))
_TN = (((0,), (0,)), ((), ()))


def _softmax_rows(s):
    e = jnp.exp(s - jnp.max(s, axis=-1, keepdims=True))
    return e / jnp.sum(e, axis=-1, keepdims=True)


def _attn_fwd(q, kv, seq, mem_len, tq=256):
    t, d = q.shape
    dh = d // MEM_HEADS
    scale = dh ** -0.5
    tq = _tile(seq, tq)
    nq = seq // tq

    def kern(q_ref, kv_ref, o_ref):
        for h in range(MEM_HEADS):
            cols = slice(h * dh, (h + 1) * dh)
            vcols = slice(d + h * dh, d + (h + 1) * dh)
            s = lax.dot_general(q_ref[:, cols], kv_ref[:, cols], _NT, preferred_element_type=F32) * scale
            p = _softmax_rows(s)
            o_ref[:, cols] = jnp.dot(p.astype(BF16), kv_ref[:, vcols], preferred_element_type=F32).astype(o_ref.dtype)

    return pl.pallas_call(
        kern, name="attn_fwd", grid=(t // seq, nq),
        in_specs=[pl.BlockSpec((tq, d), lambda b, i: (b * nq + i, 0)), pl.BlockSpec((mem_len, 2 * d), lambda b, i: (b, 0))],
        out_specs=pl.BlockSpec((tq, d), lambda b, i: (b * nq + i, 0)),
        out_shape=jax.ShapeDtypeStruct((t, d), BF16),
        compiler_params=_PARAMS(dimension_semantics=("parallel", "parallel")),
    )(q, kv)


def _attn_bwd(q, kv, do, seq, mem_len, tq=256):
    t, d = q.shape
    dh = d // MEM_HEADS
    scale = dh ** -0.5
    tq = _tile(seq, tq)
    nq = seq // tq

    def kern(q_ref, kv_ref, do_ref, dq_ref, dkv_ref):
        @pl.when(pl.program_id(1) == 0)
        def _():
            dkv_ref[...] = jnp.zeros_like(dkv_ref)

        for h in range(MEM_HEADS):
            cols = slice(h * dh, (h + 1) * dh)
            vcols = slice(d + h * dh, d + (h + 1) * dh)
            qh, kh, vh, doh = q_ref[:, cols], kv_ref[:, cols], kv_ref[:, vcols], do_ref[:, cols]
            p = _softmax_rows(lax.dot_general(qh, kh, _NT, preferred_element_type=F32) * scale)
            dp = lax.dot_general(doh, vh, _NT, preferred_element_type=F32)
            ds = (p * (dp - jnp.sum(p * dp, axis=-1, keepdims=True)) * scale).astype(BF16)
            dq_ref[:, cols] = jnp.dot(ds, kh, preferred_element_type=F32).astype(dq_ref.dtype)
            dkv_ref[:, cols] += lax.dot_general(ds, qh, _TN, preferred_element_type=F32)
            dkv_ref[:, vcols] += lax.dot_general(p.astype(BF16), doh, _TN, preferred_element_type=F32)

    return pl.pallas_call(
        kern, name="attn_bwd", grid=(t // seq, nq),
        in_specs=[pl.BlockSpec((tq, d), lambda b, i: (b * nq + i, 0)), pl.BlockSpec((mem_len, 2 * d), lambda b, i: (b, 0)),
                  pl.BlockSpec((tq, d), lambda b, i: (b * nq + i, 0))],
        out_specs=[pl.BlockSpec((tq, d), lambda b, i: (b * nq + i, 0)), pl.BlockSpec((mem_len, 2 * d), lambda b, i: (b, 0))],
        out_shape=[jax.ShapeDtypeStruct((t, d), BF16), jax.ShapeDtypeStruct(kv.shape, F32)],
        compiler_params=_PARAMS(dimension_semantics=("parallel", "arbitrary")),
    )(q, kv, do)


def _rstd(x):
    return lax.rsqrt(jnp.mean(x * x, axis=-1, keepdims=True) + RMS_EPS)


def _rms(x, g):
    return x * _rstd(x) * g


def _rms_bwd(dy, x, g):
    rstd = _rstd(x)
    xhat = x * rstd
    dxhat = dy * g
    return rstd * (dxhat - xhat * jnp.mean(dxhat * xhat, axis=-1, keepdims=True)), dy * xhat


def _softplus(x):
    return jnp.maximum(x, 0.0) + jnp.log1p(jnp.exp(-jnp.abs(x)))


def _one_minus_exp(x):
    series = -x * (1.0 + x * (0.5 + x * (1.0 / 6.0 + x * (1.0 / 24.0 + x * (1.0 / 120.0)))))
    return jnp.where(x > -0.05, series, 1.0 - jnp.exp(x))


_GELU_C = 0.7978845608028654
_GELU_K = 0.044715


def _gelu(x):
    return 0.5 * x * (1.0 + jnp.tanh(_GELU_C * (x + _GELU_K * x * x * x)))


def _gelu_grad(x):
    th = jnp.tanh(_GELU_C * (x + _GELU_K * x * x * x))
    return 0.5 * (1.0 + th) + 0.5 * x * (1.0 - th * th) * _GELU_C * (1.0 + 3.0 * _GELU_K * x * x)


def _seg_sum(x, ones_bd):
    hi = x.astype(BF16)
    lo = (x - hi.astype(F32)).astype(BF16)
    return jnp.dot(hi, ones_bd, preferred_element_type=F32) + jnp.dot(lo, ones_bd, preferred_element_type=F32)


def _f32(v):
    return v.astype(F32)


def _norm_fwd(name, x, g, dtype=BF16):
    (hn,), _ = _rowwise(name, lambda rv, cv, pv, nv: ([_rms(_f32(rv[0]), cv[0])], []), [x], [g], [(x.shape[1], dtype)])
    return hn


def _resid_norm_fwd(name, x, t, g, bias=None):
    def body(rv, cv, pv, nv):
        tt = rv[1] if bias is None else rv[1] + cv[1]
        return [rv[0] + _rms(tt, cv[0])], []
    (y,), _ = _rowwise(name, body, [x, t], [g] if bias is None else [g, bias], [(x.shape[1], F32)])
    return y


def _resid_norm_bwd(name, dxn, t, g, bias=None):
    def body(rv, cv, pv, nv):
        tt = rv[1] if bias is None else rv[1] + cv[1]
        dt, dg = _rms_bwd(rv[0], tt, cv[0])
        return [dt], [dg, dt]
    d = t.shape[1]
    (dt,), (dg, db) = _rowwise(name, body, [dxn, t], [g] if bias is None else [g, bias], [(d, BF16)], [d, d])
    return dt, dg.sum(0), db.sum(0)


def _prenorm_bwd(name, dxn, dhn, x, g):
    def body(rv, cv, pv, nv):
        dx, dg = _rms_bwd(_f32(rv[1]), rv[2], cv[0])
        return [rv[0] + dx], [dg]
    d = x.shape[1]
    (dx,), (dg,) = _rowwise(name, body, [dxn, dhn, x], [g], [(d, F32)], [d])
    return dx, dg.sum(0)


def _mlp_fwd(tag, x, g_pre, g_post, w_up, w_down):
    hn = _norm_fwd(tag + "_norm", x, g_pre)
    up = _mm(tag + "_up", hn, w_up)
    dff = up.shape[1]

    def body(rv, cv, pv, nv):
        r = jnp.maximum(rv[0], 0.0)
        return [r * r], []
    (act,), _ = _rowwise(tag + "_act", body, [up], [], [(dff, BF16)], tm=128)
    m = _mm(tag + "_down", act, w_down)
    y = _resid_norm_fwd(tag + "_res", x, m, g_post)
    return y, (x, hn, up, act, m)


def _mlp_bwd(tag, saved, dy, g_pre, g_post, w_up_t, w_down_t):
    x, hn, up, act, m = saved
    dm, dg_post, _ = _resid_norm_bwd(tag + "_dres", dy, m, g_post)
    dact = _mm(tag + "_dact", dm, w_down_t)
    dw_down = _mm(tag + "_dwdown", act, dm, trans_a=True, tk=512)

    def body(rv, cv, pv, nv):
        return [rv[0] * 2.0 * jnp.maximum(rv[1], 0.0)], []
    (dup,), _ = _rowwise(tag + "_dup", body, [dact, up], [], [(up.shape[1], BF16)], tm=128)
    dw_up = _mm(tag + "_dwup", hn, dup, trans_a=True, tk=512)
    dhn = _mm(tag + "_dhn", dup, w_up_t)
    dx, dg_pre = _prenorm_bwd(tag + "_dnorm", dy, dhn, x, g_pre)
    return dx, dict(g_pre=dg_pre, g_post=dg_post, w_up=dw_up, w_down=dw_down)


def _xattn_fwd(tag, x, mem_n, g_pre, g_post, w_q, w_kv, w_o, seq, mem_len):
    hn = _norm_fwd(tag + "_norm", x, g_pre)
    q = _mm(tag + "_q", hn, w_q, out_dtype=BF16)
    kv = _mm(tag + "_kv", mem_n, w_kv, out_dtype=BF16)
    o = _attn_fwd(q, kv, seq, mem_len)
    c = _mm(tag + "_o", o, w_o)
    y = _resid_norm_fwd(tag + "_res", x, c, g_post)
    return y, (x, hn, q, kv, o, c)


def _xattn_bwd(tag, saved, dy, mem_n, g_pre, g_post, w_q_t, w_kv_t, w_o_t, seq, mem_len):
    x, hn, q, kv, o, c = saved
    dc, dg_post, _ = _resid_norm_bwd(tag + "_dres", dy, c, g_post)
    do = _mm(tag + "_do", dc, w_o_t, out_dtype=BF16)
    dw_o = _mm(tag + "_dwo", o, dc, trans_a=True, tk=512)
    dq, dkv = _attn_bwd(q, kv, do, seq, mem_len)
    dw_q = _mm(tag + "_dwq", hn, dq, trans_a=True, tk=512)
    dhn = _mm(tag + "_dhn", dq, w_q_t)
    dw_kv = _mm(tag + "_dwkv", mem_n, dkv, trans_a=True, tk=512)
    dmem_n = _mm(tag + "_dmem", dkv, w_kv_t)
    dx, dg_pre = _prenorm_bwd(tag + "_dnorm", dy, dhn, x, g_pre)
    return dx, dmem_n, dict(g_pre=dg_pre, g_post=dg_post, w_q=dw_q, w_kv=dw_kv, w_o=dw_o)


def _lru_gates(z0, z1, gb0, gb1, sp):
    r = jax.nn.sigmoid(z0 + gb0)
    i = jax.nn.sigmoid(z1 + gb1)
    log_a = -LRU_C * r * sp
    a = jnp.exp(log_a)
    mult = jnp.sqrt(_one_minus_exp(2.0 * log_a))
    return r, i, a, mult


def _rglru_fwd(x, p, seq):
    d = x.shape[1]
    hn = _norm_fwd("a_norm", x, p["g_pre"])
    proj = _mm("a_in", hn, p["w_in"])

    def conv_body(rv, cv, pv, nv):
        u = rv[0] + cv[0]
        halo = pv[0] + cv[0]
        i = pl.program_id(0)
        halo = jnp.where((i * rv[0].shape[0]) % seq == 0, jnp.zeros_like(halo), halo)
        conv = cv[2] + u * cv[1][CONV_WIDTH - 1:CONV_WIDTH]
        for tap in range(CONV_WIDTH - 1):
            conv = conv + _shift_down(u, halo, CONV_WIDTH - 1 - tap) * cv[1][tap:tap + 1]
        return [conv], []
    (conv,), _ = _rowwise("a_conv", conv_body, [(proj, d, 1)], [p["b_in_u"], p["conv_w"], p["conv_b"]], [(d, F32)],
                          prev=[0], seq=seq)
    z = _mm("a_gate", conv, p["w_gate"])

    def gate_body(rv, cv, pv, nv):
        r, i, a, mult = _lru_gates(rv[0], rv[1], cv[0], cv[1], _softplus(-cv[2]))
        return [a, mult * i * rv[2]], []
    (a, bb), _ = _rowwise("a_gates", gate_body, [(z, d, 0), (z, d, 1), conv], [p["gate_b0"], p["gate_b1"], p["lam"]],
                          [(d, F32), (d, F32)])
    h = _scan("a_scan", a, bb, seq)

    def hy_body(rv, cv, pv, nv):
        return [rv[0] * _gelu(rv[1] + cv[0])], []
    (hy,), _ = _rowwise("a_hy", hy_body, [h, (proj, d, 0)], [p["b_in_y"]], [(d, BF16)])
    out = _mm("a_out", hy, p["w_out"])
    y = _resid_norm_fwd("a_res", x, out, p["g_post"], bias=p["b_out"])
    return y, (x, hn, proj, conv, z, a, h, hy, out)


def _rglru_bwd(saved, dy, p, pt, seq):
    x, hn, proj, conv, z, a, h, hy, out = saved
    d = x.shape[1]
    dt, dg_post, db_out = _resid_norm_bwd("a_dres", dy, out, p["g_post"], bias=p["b_out"])
    dhy = _mm("a_dhy", dt, pt["w_out_t"])
    dw_out = _mm("a_dwout", hy, dt, trans_a=True, tk=512)

    def dh_body(rv, cv, pv, nv):
        yb = rv[2] + cv[0]
        return [rv[0] * _gelu(yb), rv[0] * rv[1] * _gelu_grad(yb)], []
    (dh, dyb), _ = _rowwise("a_dh", dh_body, [dhy, h, (proj, d, 0)], [p["b_in_y"]], [(d, F32), (d, BF16)])
    g = _scan("a_rscan", a, dh, seq, reverse=True)

    def dgate_body(rv, cv, pv, nv):
        gg, hh, z0, z1, cnv = rv
        sp = _softplus(-cv[2])
        r, i, aa, mult = _lru_gates(z0, z1, cv[0], cv[1], sp)
        i_blk = pl.program_id(0)
        halo = jnp.where((i_blk * gg.shape[0]) % seq == 0, jnp.zeros_like(pv[0]), pv[0])
        da = gg * _shift_down(hh, halo, 1)
        dmult = gg * i * cnv
        di = gg * mult * cnv
        dconv = gg * mult * i
        dlog_a = da * aa - dmult * aa * aa / mult
        dz0 = dlog_a * (-LRU_C * sp) * r * (1.0 - r)
        dz1 = di * i * (1.0 - i)
        dsp = dlog_a * (-LRU_C * r)
        dlam = dsp * (-jax.nn.sigmoid(-cv[2]))
        return [jnp.concatenate([dz0, dz1], axis=1), dconv], [dz0, dz1, dlam]
    (dz, dconv1), (dgb0, dgb1, dlam) = _rowwise(
        "a_dgates", dgate_body, [g, h, (z, d, 0), (z, d, 1), conv], [p["gate_b0"], p["gate_b1"], p["lam"]],
        [(2 * d, BF16), (d, F32)], [d, d, d], prev=[1], seq=seq)
    dconv2 = _mm("a_dconv", dz, pt["w_gate_t"])
    dw_gate = _mm("a_dwgate", conv, dz, trans_a=True, tk=512)

    def dconv_body(rv, cv, pv, nv):
        dc1, dc2, pu, dyb_ = rv
        dc = dc1 + dc2
        dc_next = nv[0] + nv[1]
        u = pu + cv[0]
        i_blk = pl.program_id(0)
        halo = jnp.where((i_blk * u.shape[0]) % seq == 0, jnp.zeros_like(pv[0]), pv[0] + cv[0])
        du = dc * cv[1][CONV_WIDTH - 1:CONV_WIDTH]
        dws = []
        for tap in range(CONV_WIDTH - 1):
            k = CONV_WIDTH - 1 - tap
            du = du + _shift_up(dc, dc_next, k) * cv[1][tap:tap + 1]
            dws.append(dc * _shift_down(u, halo, k))
        dws.append(dc * u)
        return [jnp.concatenate([_f32(dyb_), du], axis=1)], dws + [dc, _f32(dyb_), du]
    (dproj,), accs = _rowwise(
        "a_dconvw", dconv_body, [dconv1, dconv2, (proj, d, 1), dyb], [p["b_in_u"], p["conv_w"]],
        [(2 * d, BF16)], [d] * (CONV_WIDTH + 3), prev=[2], nxt=[0, 1], seq=seq)
    dconv_w = jnp.stack([acc.sum(0) for acc in accs[:CONV_WIDTH]])
    dconv_b = accs[CONV_WIDTH].sum(0)
    db_in = jnp.concatenate([accs[CONV_WIDTH + 1].sum(0), accs[CONV_WIDTH + 2].sum(0)])
    dhn = _mm("a_dhn", dproj, pt["w_in_t"])
    dw_in = _mm("a_dwin", hn, dproj, trans_a=True, tk=512)
    dx, dg_pre = _prenorm_bwd("a_dnorm", dy, dhn, x, p["g_pre"])
    grads = dict(g_pre=dg_pre, g_post=dg_post, b_out=db_out, w_out=dw_out, gate_b0=dgb0.sum(0), gate_b1=dgb1.sum(0),
                 lam=dlam.sum(0), w_gate=dw_gate, conv_w=dconv_w, conv_b=dconv_b, b_in=db_in, w_in=dw_in)
    return dx, grads


def _head_major(t, nseq, seq):
    d = t.shape[1]
    h = d // RWKV_HEAD
    return t.reshape(nseq, seq, h, RWKV_HEAD).transpose(0, 2, 1, 3).reshape(nseq * h, seq // WKV_CHUNK, WKV_CHUNK, RWKV_HEAD)


def _token_major(t, nseq, seq):
    h = t.shape[0] // nseq
    return t.reshape(nseq, h, seq, RWKV_HEAD).transpose(0, 2, 1, 3).reshape(nseq * seq, h * RWKV_HEAD)


def _rwkv_prep(k, wl, za, w0, a0, k_k, k_a, ones_bd):
    w_in = wl + w0
    e_w = jnp.exp(-_softplus(-w_in) - 0.5)
    decay = jnp.exp(-e_w)
    a = jax.nn.sigmoid(za + a0)
    q = k * k_k
    norm = jnp.sqrt(_seg_sum(q * q, ones_bd))
    n = jnp.maximum(norm, 1e-12)
    kk = q / n
    return w_in, e_w, decay, a, norm, n, kk


def _rwkv_out(y, r, k2, v, gn_g, gn_b, r_k, ones_bd):
    inv = 1.0 / RWKV_HEAD
    yc = y - _seg_sum(y, ones_bd) * inv
    rstd = lax.rsqrt(_seg_sum(yc * yc, ones_bd) * inv + RWKV_GN_EPS)
    yhat = yc * rstd
    s = _seg_sum(r * k2 * r_k, ones_bd)
    return rstd, yhat, s, yhat * gn_g + gn_b + s * v


def _rwkv_fwd(x, p, seq):
    t, d = x.shape
    nseq = t // seq

    def mix_body(rv, cv, pv, nv):
        hn = _rms(rv[0], cv[0])
        xx = _shift_down(hn, _rms(pv[0], cv[0]), 1) - hn
        return [hn] + [hn + xx * cv[1][c:c + 1] for c in range(6)], []
    (hn, xr, xw, xk, xv, xa, xg), _ = _rowwise("b_mix", mix_body, [x], [p["g_pre"], p["mu"]],
                                               [(d, F32)] + [(d, BF16)] * 6, prev=[0], seq=seq)
    r = _mm("b_r", xr, p["w_r"])
    k = _mm("b_k", xk, p["w_k"])
    v = _mm("b_v", xv, p["w_v"])
    lw = _mm("b_w1", xw, p["w1"])
    la = _mm("b_a1", xa, p["a1"], out_dtype=BF16)
    lg = _mm("b_g1", xg, p["g1"])
    (th,), _ = _rowwise("b_tanh", lambda rv, cv, pv, nv: ([jnp.tanh(rv[0])], []), [lw], [], [(lw.shape[1], BF16)])
    (sg,), _ = _rowwise("b_sig", lambda rv, cv, pv, nv: ([jax.nn.sigmoid(rv[0])], []), [lg], [], [(lg.shape[1], BF16)])
    wl = _mm("b_w2", th, p["w2"])
    za = _mm("b_a2", la, p["a2"])
    g = _mm("b_g2", sg, p["g2"])

    def prep_body(rv, cv, pv, nv):
        kk_, wl_, za_ = rv
        _, _, decay, a, _, _, kk = _rwkv_prep(kk_, wl_, za_, cv[0], cv[1], cv[2], cv[3], cv[4])
        return [decay, kk_ * (1.0 + (a - 1.0) * cv[3]), -kk, kk * a], []
    (decay, k2, rem_a, rem_b), _ = _rowwise("b_prep", prep_body, [k, wl, za],
                                            [p["w0"], p["a0"], p["k_k"], p["k_a"], p["ones_bd"]], [(d, F32)] * 4, tm=128)
    hm = [_head_major(z, nseq, seq) for z in (r, decay, k2, v, rem_a, rem_b)]
    y_hm, states = _wkv_fwd(*hm)
    y = _token_major(y_hm, nseq, seq)

    def out_body(rv, cv, pv, nv):
        y_, r_, k2_, v_, g_ = rv
        _, _, _, out = _rwkv_out(y_, r_, k2_, v_, cv[0], cv[1], cv[2], cv[3])
        return [out * g_], []
    (og,), _ = _rowwise("b_out", out_body, [y, r, k2, v, g], [p["gn_g"], p["gn_b"], p["r_k"], p["ones_bd"]], [(d, BF16)], tm=128)
    o = _mm("b_o", og, p["w_o"])
    res = _resid_norm_fwd("b_res", x, o, p["g_post"])
    return res, (x, hn, xr, xw, xk, xv, xa, xg, r, k, v, th, la, sg, wl, za, g, k2, hm, states, y, og, o)


def _rwkv_bwd(saved, dres, p, pt, seq):
    x, hn, xr, xw, xk, xv, xa, xg, r, k, v, th, la, sg, wl, za, g, k2, hm, states, y, og, o = saved
    t, d = x.shape
    nseq = t // seq
    do, dg_post, _ = _resid_norm_bwd("b_dres", dres, o, p["g_post"])
    dog = _mm("b_dog", do, pt["w_o_t"])
    dw_o = _mm("b_dwo", og, do, trans_a=True, tk=512)

    def dout_body(rv, cv, pv, nv):
        dog_, y_, r_, k2_, v_, g_ = rv
        gn_g, gn_b, r_k, bd = cv
        inv = 1.0 / RWKV_HEAD
        rstd, yhat, s, out = _rwkv_out(y_, r_, k2_, v_, gn_g, gn_b, r_k, bd)
        dout = dog_ * g_
        ds = _seg_sum(dout * v_, bd)
        dyhat = dout * gn_g
        dy = rstd * (dyhat - _seg_sum(dyhat, bd) * inv - yhat * _seg_sum(dyhat * yhat, bd) * inv)
        return [dy, dog_ * out, dout * s, ds * k2_ * r_k, ds * r_ * r_k], [ds * r_ * k2_, dout * yhat, dout]
    (dy, dgate, dv_b, dr_b, dk2_b), (dr_k, dgn_g, dgn_b) = _rowwise(
        "b_dout", dout_body, [dog, y, r, k2, v, g], [p["gn_g"], p["gn_b"], p["r_k"], p["ones_bd"]],
        [(d, F32), (d, BF16), (d, F32), (d, F32), (d, F32)], [d, d, d], tm=128)
    rec = _wkv_bwd(*hm, states, _head_major(dy, nseq, seq))
    dr_rec, dw_rec, dk2_rec, dv_rec, da_rec, db_rec = [_token_major(z, nseq, seq) for z in rec]

    def dprep_body(rv, cv, pv, nv):
        dr_rec_, dw_rec_, dk2_rec_, dv_rec_, da_rec_, db_rec_, dr_b_, dk2_b_, dv_b_, k_, wl_, za_ = rv
        w0, a0, k_k, k_a, bd = cv
        w_in, e_w, decay, a, norm, n, kk = _rwkv_prep(k_, wl_, za_, w0, a0, k_k, k_a, bd)
        dk2 = dk2_rec_ + dk2_b_
        dkk = db_rec_ * a - da_rec_
        da = db_rec_ * kk + dk2 * k_ * k_a
        dq = jnp.where(norm > 1e-12, dkk - kk * _seg_sum(kk * dkk, bd), dkk) / n
        dk = dk2 * (1.0 + (a - 1.0) * k_a) + dq * k_k
        dza = da * a * (1.0 - a)
        dwl = dw_rec_ * decay * (-e_w) * jax.nn.sigmoid(-w_in)
        return [dr_rec_ + dr_b_, dk, dv_rec_ + dv_b_, dza, dwl], [dk2 * k_ * (a - 1.0), dq * k_, dza, dwl]
    (dr, dk, dv, dza, dwl), (dk_a, dk_k, da0, dw0) = _rowwise(
        "b_dprep", dprep_body, [dr_rec, dw_rec, dk2_rec, dv_rec, da_rec, db_rec, dr_b, dk2_b, dv_b, k, wl, za],
        [p["w0"], p["a0"], p["k_k"], p["k_a"], p["ones_bd"]], [(d, BF16)] * 5, [d] * 4, tm=128)

    dw_r = _mm("b_dwr", xr, dr, trans_a=True, tk=512)
    dw_k = _mm("b_dwk", xk, dk, trans_a=True, tk=512)
    dw_v = _mm("b_dwv", xv, dv, trans_a=True, tk=512)
    dxr = _mm("b_dxr", dr, pt["w_r_t"])
    dxk = _mm("b_dxk", dk, pt["w_k_t"])
    dxv = _mm("b_dxv", dv, pt["w_v_t"])
    da2 = _mm("b_da2", la, dza, trans_a=True, tk=512)
    dla = _mm("b_dla", dza, pt["a2_t"], out_dtype=BF16)
    da1 = _mm("b_da1", xa, dla, trans_a=True, tk=512)
    dxa = _mm("b_dxa", dla, pt["a1_t"])
    dw2 = _mm("b_dw2", th, dwl, trans_a=True, tk=512)
    dth = _mm("b_dth", dwl, pt["w2_t"])
    (dzw,), _ = _rowwise("b_dtanh", lambda rv, cv, pv, nv: ([rv[0] * (1.0 - _f32(rv[1]) * _f32(rv[1]))], []),
                         [dth, th], [], [(th.shape[1], BF16)])
    dw1 = _mm("b_dw1", xw, dzw, trans_a=True, tk=512)
    dxw = _mm("b_dxw", dzw, pt["w1_t"])
    dg2 = _mm("b_dg2", sg, dgate, trans_a=True, tk=512)
    dsg = _mm("b_dsg", dgate, pt["g2_t"])
    (dzg,), _ = _rowwise("b_dsig", lambda rv, cv, pv, nv: ([rv[0] * _f32(rv[1]) * (1.0 - _f32(rv[1]))], []),
                         [dsg, sg], [], [(sg.shape[1], BF16)])
    dg1 = _mm("b_dg1", xg, dzg, trans_a=True, tk=512)
    dxg = _mm("b_dxg", dzg, pt["g1_t"])

    def dmix_body(rv, cv, pv, nv):
        hn_ = rv[0]
        dxs = rv[1:]
        mu = cv[0]
        xx = _shift_down(hn_, pv[0], 1) - hn_
        dsum = dxs[0]
        dxx = dxs[0] * mu[0:1]
        dxx_next = nv[0] * mu[0:1]
        for c in range(1, 6):
            dsum = dsum + dxs[c]
            dxx = dxx + dxs[c] * mu[c:c + 1]
            dxx_next = dxx_next + nv[c] * mu[c:c + 1]
        return [dsum - dxx + _shift_up(dxx, dxx_next, 1)], [dxs[c] * xx for c in range(6)]
    (dhn,), dmu = _rowwise("b_dmix", dmix_body, [hn, dxr, dxw, dxk, dxv, dxa, dxg], [p["mu"]], [(d, F32)], [d] * 6,
                           prev=[0], nxt=[1, 2, 3, 4, 5, 6], seq=seq, tm=128)
    dx, dg_pre = _prenorm_bwd("b_dnorm", dres, dhn, x, p["g_pre"])
    grads = dict(g_pre=dg_pre, g_post=dg_post, mu=jnp.stack([m.sum(0) for m in dmu]), w_r=dw_r, w_k=dw_k, w_v=dw_v,
                 w0=dw0.sum(0), w1=dw1, w2=dw2, a0=da0.sum(0), a1=da1, a2=da2, g1=dg1, g2=dg2, k_k=dk_k.sum(0),
                 k_a=dk_a.sum(0), r_k=dr_k.sum(0), gn_g=dgn_g.sum(0), gn_b=dgn_b.sum(0), w_o=dw_o)
    return dx, grads


_WEIGHTS = ['ln_gains', 'mem_norm', 'a_conv_w', 'a_conv_b', 'a_w_in', 'a_b_in', 'a_gate_w', 'a_gate_b', 'a_lambda', 'a_w_out',
            'a_b_out', 'b_mu', 'b_w_rkv', 'b_w0', 'b_w1', 'b_w2', 'b_a0', 'b_a1', 'b_a2', 'b_g1', 'b_g2', 'b_k_k', 'b_k_a',
            'b_r_k', 'b_gn_g', 'b_gn_b', 'b_w_o', 'c_w_q', 'c_w_kv', 'c_w_o', 'm_w_up', 'm_w_down']
_SHARD_AXIS = dict(ln_gains=2, mem_norm=None, a_conv_w=2, a_conv_b=None, a_w_in=2, a_b_in=None, a_gate_w=3, a_gate_b=3,
                   a_lambda=None, a_w_out=1, a_b_out=None, b_mu=2, b_w_rkv=2, b_w0=1, b_w1=1, b_w2=2, b_a0=1, b_a1=1, b_a2=2,
                   b_g1=1, b_g2=2, b_k_k=1, b_k_a=1, b_r_k=None, b_gn_g=1, b_gn_b=1, b_w_o=1, c_w_q=1, c_w_kv=2, c_w_o=1,
                   m_w_up=2, m_w_down=1)
_MATRICES = ['a_w_in', 'a_gate_w', 'a_w_out', 'b_w_rkv', 'b_w1', 'b_w2', 'b_a1', 'b_a2', 'b_g1', 'b_g2', 'b_w_o', 'c_w_q',
             'c_w_kv', 'c_w_o', 'm_w_up', 'm_w_down']
_SHARDED = [n for n in _WEIGHTS if _SHARD_AXIS[n] is not None]
_VECTORS = [n for n in _SHARDED if n not in _MATRICES]
_REPLICATED = [n for n in _WEIGHTS if _SHARD_AXIS[n] is None]
N_XY = 4
N_DEV = 8
PACK_W = 1024
PACK_ROWS = 256


def _pack(arrs, dtype, row_mult=PACK_ROWS):
    parts = []
    rows = 0
    for a in arrs:
        n = a.size
        r = -(-n // PACK_W)
        parts.append(jnp.pad(a.reshape(-1).astype(dtype), (0, r * PACK_W - n)))
        rows += r
    pad_rows = -(-rows // row_mult) * row_mult - rows
    if pad_rows:
        parts.append(jnp.zeros((pad_rows * PACK_W,), dtype))
    return jnp.concatenate(parts).reshape(-1, PACK_W)


def _unpack(flat, shapes):
    out = []
    row = 0
    for shp in shapes:
        n = 1
        for s in shp:
            n *= s
        r = -(-n // PACK_W)
        out.append(flat[row:row + r].reshape(-1)[:n].reshape(shp))
        row += r
    return out


_ANY = pl.BlockSpec(memory_space=pl.ANY)


def _xy_peers():
    x, y = lax.axis_index("x"), lax.axis_index("y")
    return [(1 - x, y), (x, 1 - y), (1 - x, 1 - y)]


def _all_gather_xy(wm, wv):
    def body(wm_ref, wv_ref, gm_ref, gv_ref, send_sems, recv_sems, local_sems):
        x, y, c = lax.axis_index("x"), lax.axis_index("y"), lax.axis_index("c")
        me = 2 * x + y
        local = [pltpu.make_async_copy(wm_ref, gm_ref.at[me], local_sems.at[0]),
                 pltpu.make_async_copy(wv_ref, gv_ref.at[me], local_sems.at[1])]
        for cp in local:
            cp.start()
        sends, recvs = [], []
        for j, (px, py) in enumerate(_xy_peers()):
            for b, (src, dst) in enumerate(((wm_ref, gm_ref), (wv_ref, gv_ref))):
                k = 2 * j + b
                mk = functools.partial(pltpu.make_async_remote_copy, src_ref=src, send_sem=send_sems.at[k],
                                       recv_sem=recv_sems.at[k], device_id=(px, py, c), device_id_type=MESH)
                sends.append(mk(dst_ref=dst.at[me]))
                recvs.append(mk(dst_ref=dst.at[2 * px + py]))
        for cp in sends:
            cp.start()
        for cp in recvs:
            cp.wait_recv()
        for cp in sends:
            cp.wait_send()
        for cp in local:
            cp.wait()

    return pl.pallas_call(
        body, name="all_gather_weights",
        in_specs=[_ANY, _ANY], out_specs=[_ANY, _ANY],
        out_shape=[jax.ShapeDtypeStruct((N_XY,) + wm.shape, wm.dtype), jax.ShapeDtypeStruct((N_XY,) + wv.shape, wv.dtype)],
        scratch_shapes=[pltpu.SemaphoreType.DMA((6,)), pltpu.SemaphoreType.DMA((6,)), pltpu.SemaphoreType.DMA((2,))],
    )(wm, wv)


def _exchange_xy(gsend):
    def body(gs_ref, recv_ref, send_sems, recv_sems):
        c = lax.axis_index("c")
        sends = []
        for j, (px, py) in enumerate(_xy_peers()):
            sends.append(pltpu.make_async_remote_copy(
                src_ref=gs_ref.at[2 * px + py], dst_ref=recv_ref.at[j], send_sem=send_sems.at[j], recv_sem=recv_sems.at[j],
                device_id=(px, py, c), device_id_type=MESH))
        for cp in sends:
            cp.start()
        for cp in sends:
            cp.wait_recv()
        for cp in sends:
            cp.wait_send()

    return pl.pallas_call(
        body, name="exchange_grads",
        in_specs=[_ANY], out_specs=_ANY,
        out_shape=jax.ShapeDtypeStruct((N_XY - 1,) + gsend.shape[1:], gsend.dtype),
        scratch_shapes=[pltpu.SemaphoreType.DMA((3,)), pltpu.SemaphoreType.DMA((3,))],
    )(gsend)


def _swap_with_sibling(part):
    def body(p_ref, got_ref, send_sem, recv_sem):
        x, y, c = lax.axis_index("x"), lax.axis_index("y"), lax.axis_index("c")
        cp = pltpu.make_async_remote_copy(src_ref=p_ref, dst_ref=got_ref, send_sem=send_sem, recv_sem=recv_sem,
                                          device_id=(x, y, 1 - c), device_id_type=MESH)
        cp.start()
        cp.wait_recv()
        cp.wait_send()

    return pl.pallas_call(
        body, name="swap_sibling",
        in_specs=[_ANY], out_specs=_ANY, out_shape=jax.ShapeDtypeStruct(part.shape, part.dtype),
        scratch_shapes=[pltpu.SemaphoreType.DMA, pltpu.SemaphoreType.DMA],
    )(part)


def _all_gather_all(vec):
    def body(v_ref, out_ref, send_sems, recv_sems, local_sem):
        x, y, c = lax.axis_index("x"), lax.axis_index("y"), lax.axis_index("c")
        me = 4 * x + 2 * y + c
        local = pltpu.make_async_copy(v_ref, out_ref.at[me], local_sem)
        local.start()
        sends, recvs = [], []
        for f in range(1, N_DEV):
            fx, fy, fc = (f >> 2) & 1, (f >> 1) & 1, f & 1
            px = (1 - x) if fx else x
            py = (1 - y) if fy else y
            pc = (1 - c) if fc else c
            mk = functools.partial(pltpu.make_async_remote_copy, src_ref=v_ref, send_sem=send_sems.at[f - 1],
                                   recv_sem=recv_sems.at[f - 1], device_id=(px, py, pc), device_id_type=MESH)
            sends.append(mk(dst_ref=out_ref.at[me]))
            recvs.append(mk(dst_ref=out_ref.at[4 * px + 2 * py + pc]))
        for cp in sends:
            cp.start()
        for cp in recvs:
            cp.wait_recv()
        for cp in sends:
            cp.wait_send()
        local.wait()

    return pl.pallas_call(
        body, name="all_gather_replicated",
        in_specs=[_ANY], out_specs=_ANY, out_shape=jax.ShapeDtypeStruct((N_DEV,) + vec.shape, vec.dtype),
        scratch_shapes=[pltpu.SemaphoreType.DMA((N_DEV - 1,)), pltpu.SemaphoreType.DMA((N_DEV - 1,)), pltpu.SemaphoreType.DMA],
    )(vec)


def _adamw(g, w, m, v):
    m2 = ADAM_B1 * m + (1.0 - ADAM_B1) * g
    v2 = ADAM_B2 * v + (1.0 - ADAM_B2) * g * g
    m_hat = m2 / (1.0 - ADAM_B1 ** ADAM_STEP)
    v_hat = v2 / (1.0 - ADAM_B2 ** ADAM_STEP)
    return -ADAM_LR * (m_hat / (jnp.sqrt(v_hat) + ADAM_EPS) + ADAM_WD * w), m2, v2


def _sum_contributions(own, recv):
    def body(rv, cv, pv, nv):
        return [((rv[0] + _f32(rv[1])) + _f32(rv[2])) + _f32(rv[3])], []
    (part,), _ = _rowwise("sum_grads", body, [own, recv[0], recv[1], recv[2]], [], [(PACK_W, F32)])
    return part


def _adamw_sharded(part, sib, w, m, v):
    def body(rv, cv, pv, nv):
        g = rv[0] + rv[1]
        return [g, *_adamw(g, rv[2], rv[3], rv[4])], []
    outs, _ = _rowwise("adamw_sharded", body, [part, sib, w, m, v], [], [(PACK_W, F32)] * 4)
    return outs


def _adamw_replicated(parts, w, m, v):
    def body(rv, cv, pv, nv):
        g = rv[0]
        for i in range(1, N_DEV):
            g = g + rv[i]
        return [g, *_adamw(g, rv[N_DEV], rv[N_DEV + 1], rv[N_DEV + 2])], []
    outs, _ = _rowwise("adamw_replicated", body, [parts[i] for i in range(N_DEV)] + [w, m, v], [], [(PACK_W, F32)] * 4)
    return outs


def _row(v):
    return v.reshape(1, -1).astype(F32)


def _t(w):
    return jnp.transpose(w)


def _gate_dense(gate_w):
    _, nh, blk, _ = gate_w.shape
    d = nh * blk
    dense = jnp.zeros((d, 2 * d), gate_w.dtype)
    for g in range(2):
        for h in range(nh):
            dense = lax.dynamic_update_slice(dense, gate_w[g, h], (h * blk, g * d + h * blk))
    return dense


def _gate_blocks(dense, nh):
    d = dense.shape[0]
    blk = d // nh
    return jnp.stack([jnp.stack([dense[h * blk:(h + 1) * blk, g * d + h * blk:g * d + (h + 1) * blk] for h in range(nh)])
                      for g in range(2)])


def _local_step(x3, mem3, target3, fw):
    nseq, seq, d = x3.shape
    mem_len = mem3.shape[1]
    t = nseq * seq
    x0 = x3.reshape(t, d)
    mem2 = mem3.reshape(nseq * mem_len, d)
    target = target3.reshape(t, d)
    ln = fw["ln_gains"]
    gains = [[_row(ln[i, j]) for j in range(6)] for i in range(2)]
    nh = d // RWKV_HEAD
    ones_bd = jnp.kron(jnp.eye(nh, dtype=F32), jnp.ones((RWKV_HEAD, RWKV_HEAD), F32)).astype(BF16)

    w_gate = _gate_dense(fw["a_gate_w"][0])
    pa = dict(g_pre=gains[0][0], g_post=gains[0][1], w_in=fw["a_w_in"][0], b_in_y=_row(fw["a_b_in"][0, :d]),
              b_in_u=_row(fw["a_b_in"][0, d:]), conv_w=fw["a_conv_w"][0].astype(F32), conv_b=_row(fw["a_conv_b"][0]),
              w_gate=w_gate, gate_b0=_row(fw["a_gate_b"][0, 0]), gate_b1=_row(fw["a_gate_b"][0, 1]), lam=_row(fw["a_lambda"][0]),
              w_out=fw["a_w_out"][0], b_out=_row(fw["a_b_out"][0]))
    pta = dict(w_in_t=_t(pa["w_in"]), w_gate_t=_t(w_gate), w_out_t=_t(pa["w_out"]))
    pb = dict(g_pre=gains[1][0], g_post=gains[1][1], mu=fw["b_mu"][0].astype(F32), w_r=fw["b_w_rkv"][0, 0],
              w_k=fw["b_w_rkv"][0, 1], w_v=fw["b_w_rkv"][0, 2], w0=_row(fw["b_w0"][0]), w1=fw["b_w1"][0], w2=fw["b_w2"][0],
              a0=_row(fw["b_a0"][0]), a1=fw["b_a1"][0], a2=fw["b_a2"][0], g1=fw["b_g1"][0], g2=fw["b_g2"][0],
              k_k=_row(fw["b_k_k"][0]), k_a=_row(fw["b_k_a"][0]), r_k=_row(fw["b_r_k"][0]), gn_g=_row(fw["b_gn_g"][0]),
              gn_b=_row(fw["b_gn_b"][0]), w_o=fw["b_w_o"][0], ones_bd=ones_bd)
    ptb = {k + "_t": _t(pb[k]) for k in ("w_r", "w_k", "w_v", "w_o", "w1", "w2", "a1", "a2", "g1", "g2")}
    mem_g = _row(fw["mem_norm"])

    mem_n = _norm_fwd("mem_norm", mem2, mem_g)
    x1, sv_a = _rglru_fwd(x0, pa, seq)
    x2, sv_c0 = _xattn_fwd("c0", x1, mem_n, gains[0][2], gains[0][3], fw["c_w_q"][0], fw["c_w_kv"][0], fw["c_w_o"][0], seq, mem_len)
    x3_, sv_m0 = _mlp_fwd("m0", x2, gains[0][4], gains[0][5], fw["m_w_up"][0], fw["m_w_down"][0])
    x4, sv_b = _rwkv_fwd(x3_, pb, seq)
    x5, sv_c1 = _xattn_fwd("c1", x4, mem_n, gains[1][2], gains[1][3], fw["c_w_q"][1], fw["c_w_kv"][1], fw["c_w_o"][1], seq, mem_len)
    x6, sv_m1 = _mlp_fwd("m1", x5, gains[1][4], gains[1][5], fw["m_w_up"][1], fw["m_w_down"][1])

    def loss_body(rv, cv, pv, nv):
        err = rv[0] - rv[1]
        return [err * (1.0 / d)], [err * err]
    (dx,), (sq,) = _rowwise("loss", loss_body, [x6, target], [], [(d, F32)], [d])
    loss_part = 0.5 / d * jnp.sum(sq)

    dx, g_m1 = _mlp_bwd("m1", sv_m1, dx, gains[1][4], gains[1][5], _t(fw["m_w_up"][1]), _t(fw["m_w_down"][1]))
    dx, dmem1, g_c1 = _xattn_bwd("c1", sv_c1, dx, mem_n, gains[1][2], gains[1][3], _t(fw["c_w_q"][1]), _t(fw["c_w_kv"][1]),
                                 _t(fw["c_w_o"][1]), seq, mem_len)
    dx, g_b = _rwkv_bwd(sv_b, dx, pb, ptb, seq)
    dx, g_m0 = _mlp_bwd("m0", sv_m0, dx, gains[0][4], gains[0][5], _t(fw["m_w_up"][0]), _t(fw["m_w_down"][0]))
    dx, dmem0, g_c0 = _xattn_bwd("c0", sv_c0, dx, mem_n, gains[0][2], gains[0][3], _t(fw["c_w_q"][0]), _t(fw["c_w_kv"][0]),
                                 _t(fw["c_w_o"][0]), seq, mem_len)
    dx, g_a = _rglru_bwd(sv_a, dx, pa, pta, seq)

    def dmem_body(rv, cv, pv, nv):
        _, dg = _rms_bwd(rv[1] + rv[2], rv[0], cv[0])
        return [], [dg]
    _, (dmem_g,) = _rowwise("mem_norm_grad", dmem_body, [mem2, dmem0, dmem1], [mem_g], [], [d])

    lru_heads = fw["a_gate_w"].shape[2]
    blk = d // lru_heads
    grads = dict(
        ln_gains=jnp.stack([jnp.stack([g_a["g_pre"], g_a["g_post"], g_c0["g_pre"], g_c0["g_post"], g_m0["g_pre"], g_m0["g_post"]]),
                            jnp.stack([g_b["g_pre"], g_b["g_post"], g_c1["g_pre"], g_c1["g_post"], g_m1["g_pre"], g_m1["g_post"]])]),
        mem_norm=dmem_g.sum(0),
        a_conv_w=g_a["conv_w"][None], a_conv_b=g_a["conv_b"][None], a_w_in=g_a["w_in"][None], a_b_in=g_a["b_in"][None],
        a_gate_w=_gate_blocks(g_a["w_gate"], lru_heads)[None],
        a_gate_b=jnp.stack([g_a["gate_b0"], g_a["gate_b1"]]).reshape(1, 2, lru_heads, blk),
        a_lambda=g_a["lam"][None], a_w_out=g_a["w_out"][None], a_b_out=g_a["b_out"][None],
        b_mu=g_b["mu"][None], b_w_rkv=jnp.stack([g_b["w_r"], g_b["w_k"], g_b["w_v"]])[None], b_w0=g_b["w0"][None],
        b_w1=g_b["w1"][None], b_w2=g_b["w2"][None], b_a0=g_b["a0"][None], b_a1=g_b["a1"][None], b_a2=g_b["a2"][None],
        b_g1=g_b["g1"][None], b_g2=g_b["g2"][None], b_k_k=g_b["k_k"][None], b_k_a=g_b["k_a"][None],
        b_r_k=g_b["r_k"].reshape(1, nh, RWKV_HEAD), b_gn_g=g_b["gn_g"][None], b_gn_b=g_b["gn_b"][None], b_w_o=g_b["w_o"][None],
        c_w_q=jnp.stack([g_c0["w_q"], g_c1["w_q"]]), c_w_kv=jnp.stack([g_c0["w_kv"], g_c1["w_kv"]]),
        c_w_o=jnp.stack([g_c0["w_o"], g_c1["w_o"]]),
        m_w_up=jnp.stack([g_m0["w_up"], g_m1["w_up"]]), m_w_down=jnp.stack([g_m0["w_down"], g_m1["w_down"]]),
    )
    return loss_part, dx.reshape(nseq, seq, d), grads


def kernel(x, mem, ln_gains, mem_norm, a_conv_w, a_conv_b, a_w_in, a_b_in, a_gate_w, a_gate_b, a_lambda, a_w_out, a_b_out, b_mu, b_w_rkv, b_w0, b_w1, b_w2, b_a0, b_a1, b_a2, b_g1, b_g2, b_k_k, b_k_a, b_r_k, b_gn_g, b_gn_b, b_w_o, c_w_q, c_w_kv, c_w_o, m_w_up, m_w_down, loss_target, m_ln_gains, m_mem_norm, m_a_conv_w, m_a_conv_b, m_a_w_in, m_a_b_in, m_a_gate_w, m_a_gate_b, m_a_lambda, m_a_w_out, m_a_b_out, m_b_mu, m_b_w_rkv, m_b_w0, m_b_w1, m_b_w2, m_b_a0, m_b_a1, m_b_a2, m_b_g1, m_b_g2, m_b_k_k, m_b_k_a, m_b_r_k, m_b_gn_g, m_b_gn_b, m_b_w_o, m_c_w_q, m_c_w_kv, m_c_w_o, m_m_w_up, m_m_w_down, v_ln_gains, v_mem_norm, v_a_conv_w, v_a_conv_b, v_a_w_in, v_a_b_in, v_a_gate_w, v_a_gate_b, v_a_lambda, v_a_w_out, v_a_b_out, v_b_mu, v_b_w_rkv, v_b_w0, v_b_w1, v_b_w2, v_b_a0, v_b_a1, v_b_a2, v_b_g1, v_b_g2, v_b_k_k, v_b_k_a, v_b_r_k, v_b_gn_g, v_b_gn_b, v_b_w_o, v_c_w_q, v_c_w_kv, v_c_w_o, v_m_w_up, v_m_w_down):
    given = dict(locals())
    w = {n: given[n] for n in _WEIGHTS}
    mom1 = {n: given["m_" + n] for n in _WEIGHTS}
    mom2 = {n: given["v_" + n] for n in _WEIGHTS}

    gm, gv = _all_gather_xy(_pack([w[n] for n in _MATRICES], BF16), _pack([w[n] for n in _VECTORS], F32, SUBLANES))
    fw = {n: w[n] for n in _REPLICATED}
    for names, buf in ((_MATRICES, gm), (_VECTORS, gv)):
        shards = [_unpack(buf[s], [w[n].shape for n in names]) for s in range(N_XY)]
        for i, n in enumerate(names):
            fw[n] = jnp.concatenate([shards[s][i] for s in range(N_XY)], axis=_SHARD_AXIS[n])

    loss_part, grad_x, grads = _local_step(x, mem, loss_target, fw)

    def shard_of(n, s):
        ax = _SHARD_AXIS[n]
        size = w[n].shape[ax]
        return lax.slice_in_dim(grads[n], s * size, (s + 1) * size, axis=ax)
    gfull = jnp.stack([_pack([shard_of(n, s) for n in _SHARDED], F32) for s in range(N_XY)])
    me = 2 * lax.axis_index("x") + lax.axis_index("y")
    recv = _exchange_xy(gfull.astype(BF16))
    part = _sum_contributions(lax.dynamic_index_in_dim(gfull, me, 0, keepdims=False), recv)
    sib = _swap_with_sibling(part)
    flat = [_pack([src[n] for n in _SHARDED], F32) for src in (w, mom1, mom2)]
    sharded_out = [_unpack(o, [w[n].shape for n in _SHARDED]) for o in _adamw_sharded(part, sib, *flat)]

    small = _pack([grads[n] for n in _REPLICATED] + [loss_part.reshape(1)], F32, SUBLANES)
    parts = _all_gather_all(small)
    zero = jnp.zeros((1,), F32)
    flat = [_pack([src[n] for n in _REPLICATED] + [zero], F32, SUBLANES) for src in (w, mom1, mom2)]
    repl_out = [_unpack(o, [w[n].shape for n in _REPLICATED] + [(1,)]) for o in _adamw_replicated(parts, *flat)]
    loss = repl_out[0][-1][0]

    result = [loss, grad_x]
    for kind in range(4):
        by_name = dict(zip(_SHARDED, sharded_out[kind])) | dict(zip(_REPLICATED, repl_out[kind][:-1]))
        result += [by_name[n] for n in _WEIGHTS]
    return tuple(result)
```

```python
import functools

import jax
import jax.numpy as jnp
from jax import lax
from jax.experimental import pallas as pl
from jax.experimental.pallas import tpu as pltpu

F32 = jnp.float32
BF16 = jnp.bfloat16
MESH = pl.DeviceIdType.MESH

LANES = 128
SUBLANES = 8
VMEM_LIMIT_BYTES = 48 * 1024 * 1024

RMS_EPS = 1e-6
LRU_C = 8.0
LRU_HEADS = 4
CONV_WIDTH = 4
RWKV_HEAD = 64
RWKV_GN_EPS = 64e-5
MEM_HEADS = 4
ADAM_LR = 0.001
ADAM_B1 = 0.9
ADAM_B2 = 0.999
ADAM_EPS = 1e-08
ADAM_WD = 0.01
ADAM_STEP = 10
WKV_CHUNK = 16

_PARAMS = functools.partial(pltpu.CompilerParams, vmem_limit_bytes=VMEM_LIMIT_BYTES)


def _tile(n, want):
    if n <= want:
        return n
    t = want
    while t >= SUBLANES:
        if n % t == 0 and t % SUBLANES == 0:
            return t
        t -= SUBLANES
    return n


def _fold8(v):
    tm, d = v.shape
    if tm == SUBLANES:
        return v
    return jnp.sum(v.reshape(tm // SUBLANES, SUBLANES, d), axis=0)


def _rowwise(name, body, rows, consts=(), out_rows=(), out_accs=(), prev=(), nxt=(), tm=256, seq=None):
    rows = [r if isinstance(r, tuple) else (r, r.shape[1], 0) for r in rows]
    t = rows[0][0].shape[0]
    tm = _tile(t, tm)
    if seq is not None:
        tm = _tile(seq, tm)
    nblk = t // tm
    nrow, ncst, nprev, nnxt = len(rows), len(consts), len(prev), len(nxt)
    nor, noa = len(out_rows), len(out_accs)
    hb = tm // SUBLANES

    def kern(*refs):
        i = pl.program_id(0)
        rv = [r[...] for r in refs[:nrow]]
        cv = [c[...] for c in refs[nrow:nrow + ncst]]
        o = nrow + ncst
        pv = []
        for j in range(nprev):
            at_start = (i * tm) % seq == 0
            h = refs[o + j][...]
            pv.append(jnp.where(at_start, jnp.zeros_like(h), h))
        o += nprev
        nv = []
        for j in range(nnxt):
            at_end = ((i + 1) * tm) % seq == 0
            h = refs[o + j][...]
            nv.append(jnp.where(at_end, jnp.zeros_like(h), h))
        o += nnxt
        outs, accs = body(rv, cv, pv, nv)
        for j in range(nor):
            refs[o + j][...] = outs[j].astype(refs[o + j].dtype)
        o += nor
        if noa:
            @pl.when(i == 0)
            def _():
                for j in range(noa):
                    refs[o + j][...] = jnp.zeros_like(refs[o + j])
            for j in range(noa):
                refs[o + j][...] += _fold8(accs[j].astype(F32))

    in_specs = [pl.BlockSpec((tm, w), functools.partial(lambda i, c: (i, c), c=cb)) for (_, w, cb) in rows]
    in_specs += [pl.BlockSpec(c.shape, lambda i: (0, 0)) for c in consts]
    in_specs += [pl.BlockSpec((SUBLANES, rows[j][1]),
                              functools.partial(lambda i, c: (jnp.maximum(i * hb - 1, 0), c), c=rows[j][2])) for j in prev]
    in_specs += [pl.BlockSpec((SUBLANES, rows[j][1]),
                              functools.partial(lambda i, c: (jnp.minimum((i + 1) * hb, t // SUBLANES - 1), c), c=rows[j][2]))
                 for j in nxt]
    out_shape = [jax.ShapeDtypeStruct((t, w), dt) for (w, dt) in out_rows]
    out_shape += [jax.ShapeDtypeStruct((SUBLANES, w), F32) for w in out_accs]
    out_specs = [pl.BlockSpec((tm, w), lambda i: (i, 0)) for (w, _) in out_rows]
    out_specs += [pl.BlockSpec((SUBLANES, w), lambda i: (0, 0)) for w in out_accs]
    args = [r[0] for r in rows] + list(consts) + [rows[j][0] for j in prev] + [rows[j][0] for j in nxt]
    res = pl.pallas_call(
        kern, name=name, grid=(nblk,), in_specs=in_specs, out_specs=out_specs, out_shape=out_shape,
        compiler_params=_PARAMS(dimension_semantics=("arbitrary",)),
    )(*args)
    return list(res[:nor]), list(res[nor:])


def _shift_down(x, halo, k):
    rolled = pltpu.roll(x, k, 0)
    row = lax.broadcasted_iota(jnp.int32, (SUBLANES, x.shape[1]), 0)
    first = jnp.where(row < k, pltpu.roll(halo, k, 0), rolled[:SUBLANES])
    if x.shape[0] == SUBLANES:
        return first
    return jnp.concatenate([first, rolled[SUBLANES:]], axis=0)


def _shift_up(x, halo, k):
    n = x.shape[0]
    rolled = pltpu.roll(x, n - k, 0)
    row = lax.broadcasted_iota(jnp.int32, (SUBLANES, x.shape[1]), 0)
    last = jnp.where(row >= SUBLANES - k, pltpu.roll(halo, SUBLANES - k, 0), rolled[n - SUBLANES:])
    if n == SUBLANES:
        return last
    return jnp.concatenate([rolled[:n - SUBLANES], last], axis=0)


def _mm(name, a, b, out_dtype=F32, trans_a=False, tm=512, tn=512, tk=1024):
    if trans_a:
        kdim, m = a.shape
    else:
        m, kdim = a.shape
    n = b.shape[1]
    assert b.shape[0] == kdim, (name, a.shape, b.shape)
    tm, tn, tk = _tile(m, tm), _tile(n, tn), _tile(kdim, tk)
    nk = kdim // tk
    dims = (((0,), (0,)), ((), ())) if trans_a else (((1,), (0,)), ((), ()))

    def kern(a_ref, b_ref, o_ref, *acc):
        part = lax.dot_general(a_ref[...].astype(BF16), b_ref[...].astype(BF16), dims, preferred_element_type=F32)
        if nk == 1:
            o_ref[...] = part.astype(o_ref.dtype)
        else:
            k = pl.program_id(2)

            @pl.when(k == 0)
            def _():
                acc[0][...] = part

            @pl.when(k > 0)
            def _():
                acc[0][...] += part

            @pl.when(k == nk - 1)
            def _():
                o_ref[...] = acc[0][...].astype(o_ref.dtype)

    a_spec = pl.BlockSpec((tk, tm), lambda i, j, k: (k, i)) if trans_a else pl.BlockSpec((tm, tk), lambda i, j, k: (i, k))
    return pl.pallas_call(
        kern, name=name, grid=(m // tm, n // tn, nk),
        in_specs=[a_spec, pl.BlockSpec((tk, tn), lambda i, j, k: (k, j))],
        out_specs=pl.BlockSpec((tm, tn), lambda i, j, k: (i, j)),
        out_shape=jax.ShapeDtypeStruct((m, n), out_dtype),
        scratch_shapes=[] if nk == 1 else [pltpu.VMEM((tm, tn), F32)],
        compiler_params=_PARAMS(dimension_semantics=("parallel", "parallel", "arbitrary")),
    )(a, b)


def _scan(name, a, b, seq, reverse=False, tm=256):
    t, d = a.shape
    tm = _tile(seq, tm)
    nblk = t // tm
    ntile = tm // SUBLANES

    def kern(a_ref, b_ref, h_ref, carry_h, carry_a):
        i = pl.program_id(0)
        blk = (nblk - 1 - i) if reverse else i
        edge = (((blk + 1) * tm) % seq == 0) if reverse else ((blk * tm) % seq == 0)

        @pl.when(edge)
        def _():
            carry_h[...] = jnp.zeros_like(carry_h)
            carry_a[...] = jnp.zeros_like(carry_a)

        def tile_step(j, c):
            jj = (ntile - 1 - j) if reverse else j
            rows = pl.ds(pl.multiple_of(jj * SUBLANES, SUBLANES), SUBLANES)
            a8 = a_ref[rows, :]
            b8 = b_ref[rows, :]
            h, an = c
            out = [None] * SUBLANES
            order = range(SUBLANES - 1, -1, -1) if reverse else range(SUBLANES)
            for r in order:
                if reverse:
                    h = b8[r:r + 1, :] + an * h
                    an = a8[r:r + 1, :]
                else:
                    h = a8[r:r + 1, :] * h + b8[r:r + 1, :]
                out[r] = h
            h_ref[rows, :] = jnp.concatenate(out, axis=0)
            return (h, an)

        h, an = lax.fori_loop(0, ntile, tile_step, (carry_h[...], carry_a[...]))
        carry_h[...] = h
        carry_a[...] = an

    idx = (lambda i: (nblk - 1 - i, 0)) if reverse else (lambda i: (i, 0))
    return pl.pallas_call(
        kern, name=name, grid=(nblk,),
        in_specs=[pl.BlockSpec((tm, d), idx), pl.BlockSpec((tm, d), idx)],
        out_specs=pl.BlockSpec((tm, d), idx),
        out_shape=jax.ShapeDtypeStruct((t, d), F32),
        scratch_shapes=[pltpu.VMEM((1, d), F32), pltpu.VMEM((1, d), F32)],
        compiler_params=_PARAMS(dimension_semantics=("arbitrary",)),
    )(a, b)


def _eye(n):
    return lax.broadcasted_iota(jnp.int32, (n, n), 0) == lax.broadcasted_iota(jnp.int32, (n, n), 1)


def _cols_of_rows(rows, eye):
    return jnp.sum(jnp.where(eye, rows, 0.0), axis=2, keepdims=True)


def _rows_of_cols(cols, eye):
    return jnp.sum(jnp.where(eye, cols, 0.0), axis=1, keepdims=True)


def _wkv_fwd(r, w, k, v, a, b, hb=8):
    bh, nc, ch, n = r.shape
    hb = _tile(bh, hb) if bh % hb else hb

    def kern(r_ref, w_ref, k_ref, v_ref, a_ref, b_ref, y_ref, st_ref, s_scr):
        @pl.when(pl.program_id(1) == 0)
        def _():
            s_scr[...] = jnp.zeros_like(s_scr)

        eye = _eye(n)[None]
        for s in range(ch):
            row = lambda ref: ref[:, 0, s:s + 1, :]
            p = s_scr[...]
            st_ref[:, 0, s] = p
            u = jnp.sum(p * row(a_ref), axis=2, keepdims=True)
            snew = p * row(w_ref) + u * row(b_ref) + _cols_of_rows(row(v_ref), eye) * row(k_ref)
            s_scr[...] = snew
            y_ref[:, 0, s:s + 1, :] = _rows_of_cols(jnp.sum(snew * row(r_ref), axis=2, keepdims=True), eye)

    vec = pl.BlockSpec((hb, 1, ch, n), lambda g, c: (g, c, 0, 0))
    return pl.pallas_call(
        kern, name="wkv_fwd", grid=(bh // hb, nc),
        in_specs=[vec] * 6,
        out_specs=[vec, pl.BlockSpec((hb, 1, ch, n, n), lambda g, c: (g, c, 0, 0, 0))],
        out_shape=[jax.ShapeDtypeStruct((bh, nc, ch, n), F32), jax.ShapeDtypeStruct((bh, nc, ch, n, n), F32)],
        scratch_shapes=[pltpu.VMEM((hb, n, n), F32)],
        compiler_params=_PARAMS(dimension_semantics=("parallel", "arbitrary")),
    )(r, w, k, v, a, b)


def _wkv_bwd(r, w, k, v, a, b, st, dy, hb=8):
    bh, nc, ch, n = r.shape
    hb = _tile(bh, hb) if bh % hb else hb

    def kern(r_ref, w_ref, k_ref, v_ref, a_ref, b_ref, st_ref, dy_ref,
             dr_ref, dw_ref, dk_ref, dv_ref, da_ref, db_ref, ds_scr):
        @pl.when(pl.program_id(1) == 0)
        def _():
            ds_scr[...] = jnp.zeros_like(ds_scr)

        eye = _eye(n)[None]
        for s in range(ch - 1, -1, -1):
            row = lambda ref: ref[:, 0, s:s + 1, :]
            put = lambda ref, val: ref.__setitem__((slice(None), 0, slice(s, s + 1), slice(None)), val)
            colsum = lambda z: jnp.sum(z, axis=1, keepdims=True)
            p = st_ref[:, 0, s]
            r_, w_, k_, a_, b_ = row(r_ref), row(w_ref), row(k_ref), row(a_ref), row(b_ref)
            vcol = _cols_of_rows(row(v_ref), eye)
            dycol = _cols_of_rows(row(dy_ref), eye)
            u = jnp.sum(p * a_, axis=2, keepdims=True)
            snew = p * w_ + u * b_ + vcol * k_
            ds = ds_scr[...] + dycol * r_
            put(dr_ref, colsum(snew * dycol))
            put(dv_ref, _rows_of_cols(jnp.sum(ds * k_, axis=2, keepdims=True), eye))
            put(dk_ref, colsum(ds * vcol))
            du = jnp.sum(ds * b_, axis=2, keepdims=True)
            put(db_ref, colsum(ds * u))
            put(dw_ref, colsum(ds * p))
            put(da_ref, colsum(p * du))
            ds_scr[...] = ds * w_ + du * a_

    vec = pl.BlockSpec((hb, 1, ch, n), lambda g, c: (g, nc - 1 - c, 0, 0))
    return pl.pallas_call(
        kern, name="wkv_bwd", grid=(bh // hb, nc),
        in_specs=[vec] * 6 + [pl.BlockSpec((hb, 1, ch, n, n), lambda g, c: (g, nc - 1 - c, 0, 0, 0)), vec],
        out_specs=[vec] * 6,
        out_shape=[jax.ShapeDtypeStruct((bh, nc, ch, n), F32)] * 6,
        scratch_shapes=[pltpu.VMEM((hb, n, n), F32)],
        compiler_params=_PARAMS(dimension_semantics=("parallel", "arbitrary")),
    )(r, w, k, v, a, b, st, dy)


_NT = (((1,), (1,)), ((), ()))
_TN = (((0,), (0,)), ((), ()))


def _softmax_rows(s):
    e = jnp.exp(s - jnp.max(s, axis=-1, keepdims=True))
    return e / jnp.sum(e, axis=-1, keepdims=True)


def _attn_fwd(q, kv, seq, mem_len, tq=256):
    t, d = q.shape
    dh = d // MEM_HEADS
    scale = dh ** -0.5
    tq = _tile(seq, tq)
    nq = seq // tq

    def kern(q_ref, kv_ref, o_ref):
        for h in range(MEM_HEADS):
            cols = slice(h * dh, (h + 1) * dh)
            vcols = slice(d + h * dh, d + (h + 1) * dh)
            s = lax.dot_general(q_ref[:, cols], kv_ref[:, cols], _NT, preferred_element_type=F32) * scale
            p = _softmax_rows(s)
            o_ref[:, cols] = jnp.dot(p.astype(BF16), kv_ref[:, vcols], preferred_element_type=F32).astype(o_ref.dtype)

    return pl.pallas_call(
        kern, name="attn_fwd", grid=(t // seq, nq),
        in_specs=[pl.BlockSpec((tq, d), lambda b, i: (b * nq + i, 0)), pl.BlockSpec((mem_len, 2 * d), lambda b, i: (b, 0))],
        out_specs=pl.BlockSpec((tq, d), lambda b, i: (b * nq + i, 0)),
        out_shape=jax.ShapeDtypeStruct((t, d), BF16),
        compiler_params=_PARAMS(dimension_semantics=("parallel", "parallel")),
    )(q, kv)


def _attn_bwd(q, kv, do, seq, mem_len, tq=256):
    t, d = q.shape
    dh = d // MEM_HEADS
    scale = dh ** -0.5
    tq = _tile(seq, tq)
    nq = seq // tq

    def kern(q_ref, kv_ref, do_ref, dq_ref, dkv_ref):
        @pl.when(pl.program_id(1) == 0)
        def _():
            dkv_ref[...] = jnp.zeros_like(dkv_ref)

        for h in range(MEM_HEADS):
            cols = slice(h * dh, (h + 1) * dh)
            vcols = slice(d + h * dh, d + (h + 1) * dh)
            qh, kh, vh, doh = q_ref[:, cols], kv_ref[:, cols], kv_ref[:, vcols], do_ref[:, cols]
            p = _softmax_rows(lax.dot_general(qh, kh, _NT, preferred_element_type=F32) * scale)
            dp = lax.dot_general(doh, vh, _NT, preferred_element_type=F32)
            ds = (p * (dp - jnp.sum(p * dp, axis=-1, keepdims=True)) * scale).astype(BF16)
            dq_ref[:, cols] = jnp.dot(ds, kh, preferred_element_type=F32).astype(dq_ref.dtype)
            dkv_ref[:, cols] += lax.dot_general(ds, qh, _TN, preferred_element_type=F32)
            dkv_ref[:, vcols] += lax.dot_general(p.astype(BF16), doh, _TN, preferred_element_type=F32)

    return pl.pallas_call(
        kern, name="attn_bwd", grid=(t // seq, nq),
        in_specs=[pl.BlockSpec((tq, d), lambda b, i: (b * nq + i, 0)), pl.BlockSpec((mem_len, 2 * d), lambda b, i: (b, 0)),
                  pl.BlockSpec((tq, d), lambda b, i: (b * nq + i, 0))],
        out_specs=[pl.BlockSpec((tq, d), lambda b, i: (b * nq + i, 0)), pl.BlockSpec((mem_len, 2 * d), lambda b, i: (b, 0))],
        out_shape=[jax.ShapeDtypeStruct((t, d), BF16), jax.ShapeDtypeStruct(kv.shape, F32)],
        compiler_params=_PARAMS(dimension_semantics=("parallel", "arbitrary")),
    )(q, kv, do)


def _rstd(x):
    return lax.rsqrt(jnp.mean(x * x, axis=-1, keepdims=True) + RMS_EPS)


def _rms(x, g):
    return x * _rstd(x) * g


def _rms_bwd(dy, x, g):
    rstd = _rstd(x)
    xhat = x * rstd
    dxhat = dy * g
    return rstd * (dxhat - xhat * jnp.mean(dxhat * xhat, axis=-1, keepdims=True)), dy * xhat


def _softplus(x):
    return jnp.maximum(x, 0.0) + jnp.log1p(jnp.exp(-jnp.abs(x)))


def _one_minus_exp(x):
    series = -x * (1.0 + x * (0.5 + x * (1.0 / 6.0 + x * (1.0 / 24.0 + x * (1.0 / 120.0)))))
    return jnp.where(x > -0.05, series, 1.0 - jnp.exp(x))


_GELU_C = 0.7978845608028654
_GELU_K = 0.044715


def _gelu(x):
    return 0.5 * x * (1.0 + jnp.tanh(_GELU_C * (x + _GELU_K * x * x * x)))


def _gelu_grad(x):
    th = jnp.tanh(_GELU_C * (x + _GELU_K * x * x * x))
    return 0.5 * (1.0 + th) + 0.5 * x * (1.0 - th * th) * _GELU_C * (1.0 + 3.0 * _GELU_K * x * x)


def _seg_sum(x, ones_bd):
    hi = x.astype(BF16)
    lo = (x - hi.astype(F32)).astype(BF16)
    return jnp.dot(hi, ones_bd, preferred_element_type=F32) + jnp.dot(lo, ones_bd, preferred_element_type=F32)


def _f32(v):
    return v.astype(F32)


def _norm_fwd(name, x, g, dtype=BF16):
    (hn,), _ = _rowwise(name, lambda rv, cv, pv, nv: ([_rms(_f32(rv[0]), cv[0])], []), [x], [g], [(x.shape[1], dtype)])
    return hn


def _resid_norm_fwd(name, x, t, g, bias=None):
    def body(rv, cv, pv, nv):
        tt = rv[1] if bias is None else rv[1] + cv[1]
        return [rv[0] + _rms(tt, cv[0])], []
    (y,), _ = _rowwise(name, body, [x, t], [g] if bias is None else [g, bias], [(x.shape[1], F32)])
    return y


def _resid_norm_bwd(name, dxn, t, g, bias=None):
    def body(rv, cv, pv, nv):
        tt = rv[1] if bias is None else rv[1] + cv[1]
        dt, dg = _rms_bwd(rv[0], tt, cv[0])
        return [dt], [dg, dt]
    d = t.shape[1]
    (dt,), (dg, db) = _rowwise(name, body, [dxn, t], [g] if bias is None else [g, bias], [(d, BF16)], [d, d])
    return dt, dg.sum(0), db.sum(0)


def _prenorm_bwd(name, dxn, dhn, x, g):
    def body(rv, cv, pv, nv):
        dx, dg = _rms_bwd(_f32(rv[1]), rv[2], cv[0])
        return [rv[0] + dx], [dg]
    d = x.shape[1]
    (dx,), (dg,) = _rowwise(name, body, [dxn, dhn, x], [g], [(d, F32)], [d])
    return dx, dg.sum(0)


def _mlp_fwd(tag, x, g_pre, g_post, w_up, w_down):
    hn = _norm_fwd(tag + "_norm", x, g_pre)
    up = _mm(tag + "_up", hn, w_up)
    dff = up.shape[1]

    def body(rv, cv, pv, nv):
        r = jnp.maximum(rv[0], 0.0)
        return [r * r], []
    (act,), _ = _rowwise(tag + "_act", body, [up], [], [(dff, BF16)], tm=128)
    m = _mm(tag + "_down", act, w_down)
    y = _resid_norm_fwd(tag + "_res", x, m, g_post)
    return y, (x, hn, up, act, m)


def _mlp_bwd(tag, saved, dy, g_pre, g_post, w_up_t, w_down_t):
    x, hn, up, act, m = saved
    dm, dg_post, _ = _resid_norm_bwd(tag + "_dres", dy, m, g_post)
    dact = _mm(tag + "_dact", dm, w_down_t)
    dw_down = _mm(tag + "_dwdown", act, dm, trans_a=True, tk=512)

    def body(rv, cv, pv, nv):
        return [rv[0] * 2.0 * jnp.maximum(rv[1], 0.0)], []
    (dup,), _ = _rowwise(tag + "_dup", body, [dact, up], [], [(up.shape[1], BF16)], tm=128)
    dw_up = _mm(tag + "_dwup", hn, dup, trans_a=True, tk=512)
    dhn = _mm(tag + "_dhn", dup, w_up_t)
    dx, dg_pre = _prenorm_bwd(tag + "_dnorm", dy, dhn, x, g_pre)
    return dx, dict(g_pre=dg_pre, g_post=dg_post, w_up=dw_up, w_down=dw_down)


def _xattn_fwd(tag, x, mem_n, g_pre, g_post, w_q, w_kv, w_o, seq, mem_len):
    hn = _norm_fwd(tag + "_norm", x, g_pre)
    q = _mm(tag + "_q", hn, w_q, out_dtype=BF16)
    kv = _mm(tag + "_kv", mem_n, w_kv, out_dtype=BF16)
    o = _attn_fwd(q, kv, seq, mem_len)
    c = _mm(tag + "_o", o, w_o)
    y = _resid_norm_fwd(tag + "_res", x, c, g_post)
    return y, (x, hn, q, kv, o, c)


def _xattn_bwd(tag, saved, dy, mem_n, g_pre, g_post, w_q_t, w_kv_t, w_o_t, seq, mem_len):
    x, hn, q, kv, o, c = saved
    dc, dg_post, _ = _resid_norm_bwd(tag + "_dres", dy, c, g_post)
    do = _mm(tag + "_do", dc, w_o_t, out_dtype=BF16)
    dw_o = _mm(tag + "_dwo", o, dc, trans_a=True, tk=512)
    dq, dkv = _attn_bwd(q, kv, do, seq, mem_len)
    dw_q = _mm(tag + "_dwq", hn, dq, trans_a=True, tk=512)
    dhn = _mm(tag + "_dhn", dq, w_q_t)
    dw_kv = _mm(tag + "_dwkv", mem_n, dkv, trans_a=True, tk=512)
    dmem_n = _mm(tag + "_dmem", dkv, w_kv_t)
    dx, dg_pre = _prenorm_bwd(tag + "_dnorm", dy, dhn, x, g_pre)
    return dx, dmem_n, dict(g_pre=dg_pre, g_post=dg_post, w_q=dw_q, w_kv=dw_kv, w_o=dw_o)


def _lru_gates(z0, z1, gb0, gb1, sp):
    r = jax.nn.sigmoid(z0 + gb0)
    i = jax.nn.sigmoid(z1 + gb1)
    log_a = -LRU_C * r * sp
    a = jnp.exp(log_a)
    mult = jnp.sqrt(_one_minus_exp(2.0 * log_a))
    return r, i, a, mult


def _rglru_fwd(x, p, seq):
    d = x.shape[1]
    hn = _norm_fwd("a_norm", x, p["g_pre"])
    proj = _mm("a_in", hn, p["w_in"])

    def conv_body(rv, cv, pv, nv):
        u = rv[0] + cv[0]
        halo = pv[0] + cv[0]
        i = pl.program_id(0)
        halo = jnp.where((i * rv[0].shape[0]) % seq == 0, jnp.zeros_like(halo), halo)
        conv = cv[2] + u * cv[1][CONV_WIDTH - 1:CONV_WIDTH]
        for tap in range(CONV_WIDTH - 1):
            conv = conv + _shift_down(u, halo, CONV_WIDTH - 1 - tap) * cv[1][tap:tap + 1]
        return [conv], []
    (conv,), _ = _rowwise("a_conv", conv_body, [(proj, d, 1)], [p["b_in_u"], p["conv_w"], p["conv_b"]], [(d, F32)],
                          prev=[0], seq=seq)
    z = _mm("a_gate", conv, p["w_gate"])

    def gate_body(rv, cv, pv, nv):
        r, i, a, mult = _lru_gates(rv[0], rv[1], cv[0], cv[1], _softplus(-cv[2]))
        return [a, mult * i * rv[2]], []
    (a, bb), _ = _rowwise("a_gates", gate_body, [(z, d, 0), (z, d, 1), conv], [p["gate_b0"], p["gate_b1"], p["lam"]],
                          [(d, F32), (d, F32)])
    h = _scan("a_scan", a, bb, seq)

    def hy_body(rv, cv, pv, nv):
        return [rv[0] * _gelu(rv[1] + cv[0])], []
    (hy,), _ = _rowwise("a_hy", hy_body, [h, (proj, d, 0)], [p["b_in_y"]], [(d, BF16)])
    out = _mm("a_out", hy, p["w_out"])
    y = _resid_norm_fwd("a_res", x, out, p["g_post"], bias=p["b_out"])
    return y, (x, hn, proj, conv, z, a, h, hy, out)


def _rglru_bwd(saved, dy, p, pt, seq):
    x, hn, proj, conv, z, a, h, hy, out = saved
    d = x.shape[1]
    dt, dg_post, db_out = _resid_norm_bwd("a_dres", dy, out, p["g_post"], bias=p["b_out"])
    dhy = _mm("a_dhy", dt, pt["w_out_t"])
    dw_out = _mm("a_dwout", hy, dt, trans_a=True, tk=512)

    def dh_body(rv, cv, pv, nv):
        yb = rv[2] + cv[0]
        return [rv[0] * _gelu(yb), rv[0] * rv[1] * _gelu_grad(yb)], []
    (dh, dyb), _ = _rowwise("a_dh", dh_body, [dhy, h, (proj, d, 0)], [p["b_in_y"]], [(d, F32), (d, BF16)])
    g = _scan("a_rscan", a, dh, seq, reverse=True)

    def dgate_body(rv, cv, pv, nv):
        gg, hh, z0, z1, cnv = rv
        sp = _softplus(-cv[2])
        r, i, aa, mult = _lru_gates(z0, z1, cv[0], cv[1], sp)
        i_blk = pl.program_id(0)
        halo = jnp.where((i_blk * gg.shape[0]) % seq == 0, jnp.zeros_like(pv[0]), pv[0])
        da = gg * _shift_down(hh, halo, 1)
        dmult = gg * i * cnv
        di = gg * mult * cnv
        dconv = gg * mult * i
        dlog_a = da * aa - dmult * aa * aa / mult
        dz0 = dlog_a * (-LRU_C * sp) * r * (1.0 - r)
        dz1 = di * i * (1.0 - i)
        dsp = dlog_a * (-LRU_C * r)
        dlam = dsp * (-jax.nn.sigmoid(-cv[2]))
        return [jnp.concatenate([dz0, dz1], axis=1), dconv], [dz0, dz1, dlam]
    (dz, dconv1), (dgb0, dgb1, dlam) = _rowwise(
        "a_dgates", dgate_body, [g, h, (z, d, 0), (z, d, 1), conv], [p["gate_b0"], p["gate_b1"], p["lam"]],
        [(2 * d, BF16), (d, F32)], [d, d, d], prev=[1], seq=seq)
    dconv2 = _mm("a_dconv", dz, pt["w_gate_t"])
    dw_gate = _mm("a_dwgate", conv, dz, trans_a=True, tk=512)

    def dconv_body(rv, cv, pv, nv):
        dc1, dc2, pu, dyb_ = rv
        dc = dc1 + dc2
        dc_next = nv[0] + nv[1]
        u = pu + cv[0]
        i_blk = pl.program_id(0)
        halo = jnp.where((i_blk * u.shape[0]) % seq == 0, jnp.zeros_like(pv[0]), pv[0] + cv[0])
        du = dc * cv[1][CONV_WIDTH - 1:CONV_WIDTH]
        dws = []
        for tap in range(CONV_WIDTH - 1):
            k = CONV_WIDTH - 1 - tap
            du = du + _shift_up(dc, dc_next, k) * cv[1][tap:tap + 1]
            dws.append(dc * _shift_down(u, halo, k))
        dws.append(dc * u)
        return [jnp.concatenate([_f32(dyb_), du], axis=1)], dws + [dc, _f32(dyb_), du]
    (dproj,), accs = _rowwise(
        "a_dconvw", dconv_body, [dconv1, dconv2, (proj, d, 1), dyb], [p["b_in_u"], p["conv_w"]],
        [(2 * d, BF16)], [d] * (CONV_WIDTH + 3), prev=[2], nxt=[0, 1], seq=seq)
    dconv_w = jnp.stack([acc.sum(0) for acc in accs[:CONV_WIDTH]])
    dconv_b = accs[CONV_WIDTH].sum(0)
    db_in = jnp.concatenate([accs[CONV_WIDTH + 1].sum(0), accs[CONV_WIDTH + 2].sum(0)])
    dhn = _mm("a_dhn", dproj, pt["w_in_t"])
    dw_in = _mm("a_dwin", hn, dproj, trans_a=True, tk=512)
    dx, dg_pre = _prenorm_bwd("a_dnorm", dy, dhn, x, p["g_pre"])
    grads = dict(g_pre=dg_pre, g_post=dg_post, b_out=db_out, w_out=dw_out, gate_b0=dgb0.sum(0), gate_b1=dgb1.sum(0),
                 lam=dlam.sum(0), w_gate=dw_gate, conv_w=dconv_w, conv_b=dconv_b, b_in=db_in, w_in=dw_in)
    return dx, grads


def _head_major(t, nseq, seq):
    d = t.shape[1]
    h = d // RWKV_HEAD
    return t.reshape(nseq, seq, h, RWKV_HEAD).transpose(0, 2, 1, 3).reshape(nseq * h, seq // WKV_CHUNK, WKV_CHUNK, RWKV_HEAD)


def _token_major(t, nseq, seq):
    h = t.shape[0] // nseq
    return t.reshape(nseq, h, seq, RWKV_HEAD).transpose(0, 2, 1, 3).reshape(nseq * seq, h * RWKV_HEAD)


def _rwkv_prep(k, wl, za, w0, a0, k_k, k_a, ones_bd):
    w_in = wl + w0
    e_w = jnp.exp(-_softplus(-w_in) - 0.5)
    decay = jnp.exp(-e_w)
    a = jax.nn.sigmoid(za + a0)
    q = k * k_k
    norm = jnp.sqrt(_seg_sum(q * q, ones_bd))
    n = jnp.maximum(norm, 1e-12)
    kk = q / n
    return w_in, e_w, decay, a, norm, n, kk


def _rwkv_out(y, r, k2, v, gn_g, gn_b, r_k, ones_bd):
    inv = 1.0 / RWKV_HEAD
    yc = y - _seg_sum(y, ones_bd) * inv
    rstd = lax.rsqrt(_seg_sum(yc * yc, ones_bd) * inv + RWKV_GN_EPS)
    yhat = yc * rstd
    s = _seg_sum(r * k2 * r_k, ones_bd)
    return rstd, yhat, s, yhat * gn_g + gn_b + s * v


def _rwkv_fwd(x, p, seq):
    t, d = x.shape
    nseq = t // seq

    def mix_body(rv, cv, pv, nv):
        hn = _rms(rv[0], cv[0])
        xx = _shift_down(hn, _rms(pv[0], cv[0]), 1) - hn
        return [hn] + [hn + xx * cv[1][c:c + 1] for c in range(6)], []
    (hn, xr, xw, xk, xv, xa, xg), _ = _rowwise("b_mix", mix_body, [x], [p["g_pre"], p["mu"]],
                                               [(d, F32)] + [(d, BF16)] * 6, prev=[0], seq=seq)
    r = _mm("b_r", xr, p["w_r"])
    k = _mm("b_k", xk, p["w_k"])
    v = _mm("b_v", xv, p["w_v"])
    lw = _mm("b_w1", xw, p["w1"])
    la = _mm("b_a1", xa, p["a1"], out_dtype=BF16)
    lg = _mm("b_g1", xg, p["g1"])
    (th,), _ = _rowwise("b_tanh", lambda rv, cv, pv, nv: ([jnp.tanh(rv[0])], []), [lw], [], [(lw.shape[1], BF16)])
    (sg,), _ = _rowwise("b_sig", lambda rv, cv, pv, nv: ([jax.nn.sigmoid(rv[0])], []), [lg], [], [(lg.shape[1], BF16)])
    wl = _mm("b_w2", th, p["w2"])
    za = _mm("b_a2", la, p["a2"])
    g = _mm("b_g2", sg, p["g2"])

    def prep_body(rv, cv, pv, nv):
        kk_, wl_, za_ = rv
        _, _, decay, a, _, _, kk = _rwkv_prep(kk_, wl_, za_, cv[0], cv[1], cv[2], cv[3], cv[4])
        return [decay, kk_ * (1.0 + (a - 1.0) * cv[3]), -kk, kk * a], []
    (decay, k2, rem_a, rem_b), _ = _rowwise("b_prep", prep_body, [k, wl, za],
                                            [p["w0"], p["a0"], p["k_k"], p["k_a"], p["ones_bd"]], [(d, F32)] * 4, tm=128)
    hm = [_head_major(z, nseq, seq) for z in (r, decay, k2, v, rem_a, rem_b)]
    y_hm, states = _wkv_fwd(*hm)
    y = _token_major(y_hm, nseq, seq)

    def out_body(rv, cv, pv, nv):
        y_, r_, k2_, v_, g_ = rv
        _, _, _, out = _rwkv_out(y_, r_, k2_, v_, cv[0], cv[1], cv[2], cv[3])
        return [out * g_], []
    (og,), _ = _rowwise("b_out", out_body, [y, r, k2, v, g], [p["gn_g"], p["gn_b"], p["r_k"], p["ones_bd"]], [(d, BF16)], tm=128)
    o = _mm("b_o", og, p["w_o"])
    res = _resid_norm_fwd("b_res", x, o, p["g_post"])
    return res, (x, hn, xr, xw, xk, xv, xa, xg, r, k, v, th, la, sg, wl, za, g, k2, hm, states, y, og, o)


def _rwkv_bwd(saved, dres, p, pt, seq):
    x, hn, xr, xw, xk, xv, xa, xg, r, k, v, th, la, sg, wl, za, g, k2, hm, states, y, og, o = saved
    t, d = x.shape
    nseq = t // seq
    do, dg_post, _ = _resid_norm_bwd("b_dres", dres, o, p["g_post"])
    dog = _mm("b_dog", do, pt["w_o_t"])
    dw_o = _mm("b_dwo", og, do, trans_a=True, tk=512)

    def dout_body(rv, cv, pv, nv):
        dog_, y_, r_, k2_, v_, g_ = rv
        gn_g, gn_b, r_k, bd = cv
        inv = 1.0 / RWKV_HEAD
        rstd, yhat, s, out = _rwkv_out(y_, r_, k2_, v_, gn_g, gn_b, r_k, bd)
        dout = dog_ * g_
        ds = _seg_sum(dout * v_, bd)
        dyhat = dout * gn_g
        dy = rstd * (dyhat - _seg_sum(dyhat, bd) * inv - yhat * _seg_sum(dyhat * yhat, bd) * inv)
        return [dy, dog_ * out, dout * s, ds * k2_ * r_k, ds * r_ * r_k], [ds * r_ * k2_, dout * yhat, dout]
    (dy, dgate, dv_b, dr_b, dk2_b), (dr_k, dgn_g, dgn_b) = _rowwise(
        "b_dout", dout_body, [dog, y, r, k2, v, g], [p["gn_g"], p["gn_b"], p["r_k"], p["ones_bd"]],
        [(d, F32), (d, BF16), (d, F32), (d, F32), (d, F32)], [d, d, d], tm=128)
    rec = _wkv_bwd(*hm, states, _head_major(dy, nseq, seq))
    dr_rec, dw_rec, dk2_rec, dv_rec, da_rec, db_rec = [_token_major(z, nseq, seq) for z in rec]

    def dprep_body(rv, cv, pv, nv):
        dr_rec_, dw_rec_, dk2_rec_, dv_rec_, da_rec_, db_rec_, dr_b_, dk2_b_, dv_b_, k_, wl_, za_ = rv
        w0, a0, k_k, k_a, bd = cv
        w_in, e_w, decay, a, norm, n, kk = _rwkv_prep(k_, wl_, za_, w0, a0, k_k, k_a, bd)
        dk2 = dk2_rec_ + dk2_b_
        dkk = db_rec_ * a - da_rec_
        da = db_rec_ * kk + dk2 * k_ * k_a
        dq = jnp.where(norm > 1e-12, dkk - kk * _seg_sum(kk * dkk, bd), dkk) / n
        dk = dk2 * (1.0 + (a - 1.0) * k_a) + dq * k_k
        dza = da * a * (1.0 - a)
        dwl = dw_rec_ * decay * (-e_w) * jax.nn.sigmoid(-w_in)
        return [dr_rec_ + dr_b_, dk, dv_rec_ + dv_b_, dza, dwl], [dk2 * k_ * (a - 1.0), dq * k_, dza, dwl]
    (dr, dk, dv, dza, dwl), (dk_a, dk_k, da0, dw0) = _rowwise(
        "b_dprep", dprep_body, [dr_rec, dw_rec, dk2_rec, dv_rec, da_rec, db_rec, dr_b, dk2_b, dv_b, k, wl, za],
        [p["w0"], p["a0"], p["k_k"], p["k_a"], p["ones_bd"]], [(d, BF16)] * 5, [d] * 4, tm=128)

    dw_r = _mm("b_dwr", xr, dr, trans_a=True, tk=512)
    dw_k = _mm("b_dwk", xk, dk, trans_a=True, tk=512)
    dw_v = _mm("b_dwv", xv, dv, trans_a=True, tk=512)
    dxr = _mm("b_dxr", dr, pt["w_r_t"])
    dxk = _mm("b_dxk", dk, pt["w_k_t"])
    dxv = _mm("b_dxv", dv, pt["w_v_t"])
    da2 = _mm("b_da2", la, dza, trans_a=True, tk=512)
    dla = _mm("b_dla", dza, pt["a2_t"], out_dtype=BF16)
    da1 = _mm("b_da1", xa, dla, trans_a=True, tk=512)
    dxa = _mm("b_dxa", dla, pt["a1_t"])
    dw2 = _mm("b_dw2", th, dwl, trans_a=True, tk=512)
    dth = _mm("b_dth", dwl, pt["w2_t"])
    (dzw,), _ = _rowwise("b_dtanh", lambda rv, cv, pv, nv: ([rv[0] * (1.0 - _f32(rv[1]) * _f32(rv[1]))], []),
                         [dth, th], [], [(th.shape[1], BF16)])
    dw1 = _mm("b_dw1", xw, dzw, trans_a=True, tk=512)
    dxw = _mm("b_dxw", dzw, pt["w1_t"])
    dg2 = _mm("b_dg2", sg, dgate, trans_a=True, tk=512)
    dsg = _mm("b_dsg", dgate, pt["g2_t"])
    (dzg,), _ = _rowwise("b_dsig", lambda rv, cv, pv, nv: ([rv[0] * _f32(rv[1]) * (1.0 - _f32(rv[1]))], []),
                         [dsg, sg], [], [(sg.shape[1], BF16)])
    dg1 = _mm("b_dg1", xg, dzg, trans_a=True, tk=512)
    dxg = _mm("b_dxg", dzg, pt["g1_t"])

    def dmix_body(rv, cv, pv, nv):
        hn_ = rv[0]
        dxs = rv[1:]
        mu = cv[0]
        xx = _shift_down(hn_, pv[0], 1) - hn_
        dsum = dxs[0]
        dxx = dxs[0] * mu[0:1]
        dxx_next = nv[0] * mu[0:1]
        for c in range(1, 6):
            dsum = dsum + dxs[c]
            dxx = dxx + dxs[c] * mu[c:c + 1]
            dxx_next = dxx_next + nv[c] * mu[c:c + 1]
        return [dsum - dxx + _shift_up(dxx, dxx_next, 1)], [dxs[c] * xx for c in range(6)]
    (dhn,), dmu = _rowwise("b_dmix", dmix_body, [hn, dxr, dxw, dxk, dxv, dxa, dxg], [p["mu"]], [(d, F32)], [d] * 6,
                           prev=[0], nxt=[1, 2, 3, 4, 5, 6], seq=seq, tm=128)
    dx, dg_pre = _prenorm_bwd("b_dnorm", dres, dhn, x, p["g_pre"])
    grads = dict(g_pre=dg_pre, g_post=dg_post, mu=jnp.stack([m.sum(0) for m in dmu]), w_r=dw_r, w_k=dw_k, w_v=dw_v,
                 w0=dw0.sum(0), w1=dw1, w2=dw2, a0=da0.sum(0), a1=da1, a2=da2, g1=dg1, g2=dg2, k_k=dk_k.sum(0),
                 k_a=dk_a.sum(0), r_k=dr_k.sum(0), gn_g=dgn_g.sum(0), gn_b=dgn_b.sum(0), w_o=dw_o)
    return dx, grads


_WEIGHTS = ['ln_gains', 'mem_norm', 'a_conv_w', 'a_conv_b', 'a_w_in', 'a_b_in', 'a_gate_w', 'a_gate_b', 'a_lambda', 'a_w_out',
            'a_b_out', 'b_mu', 'b_w_rkv', 'b_w0', 'b_w1', 'b_w2', 'b_a0', 'b_a1', 'b_a2', 'b_g1', 'b_g2', 'b_k_k', 'b_k_a',
            'b_r_k', 'b_gn_g', 'b_gn_b', 'b_w_o', 'c_w_q', 'c_w_kv', 'c_w_o', 'm_w_up', 'm_w_down']
_SHARD_AXIS = dict(ln_gains=2, mem_norm=None, a_conv_w=2, a_conv_b=None, a_w_in=2, a_b_in=None, a_gate_w=3, a_gate_b=3,
                   a_lambda=None, a_w_out=1, a_b_out=None, b_mu=2, b_w_rkv=2, b_w0=1, b_w1=1, b_w2=2, b_a0=1, b_a1=1, b_a2=2,
                   b_g1=1, b_g2=2, b_k_k=1, b_k_a=1, b_r_k=None, b_gn_g=1, b_gn_b=1, b_w_o=1, c_w_q=1, c_w_kv=2, c_w_o=1,
                   m_w_up=2, m_w_down=1)
_MATRICES = ['a_w_in', 'a_gate_w', 'a_w_out', 'b_w_rkv', 'b_w1', 'b_w2', 'b_a1', 'b_a2', 'b_g1', 'b_g2', 'b_w_o', 'c_w_q',
             'c_w_kv', 'c_w_o', 'm_w_up', 'm_w_down']
_SHARDED = [n for n in _WEIGHTS if _SHARD_AXIS[n] is not None]
_VECTORS = [n for n in _SHARDED if n not in _MATRICES]
_REPLICATED = [n for n in _WEIGHTS if _SHARD_AXIS[n] is None]
N_XY = 4
N_DEV = 8
PACK_W = 1024
PACK_ROWS = 256


def _pack(arrs, dtype, row_mult=PACK_ROWS):
    parts = []
    rows = 0
    for a in arrs:
        n = a.size
        r = -(-n // PACK_W)
        parts.append(jnp.pad(a.reshape(-1).astype(dtype), (0, r * PACK_W - n)))
        rows += r
    pad_rows = -(-rows // row_mult) * row_mult - rows
    if pad_rows:
        parts.append(jnp.zeros((pad_rows * PACK_W,), dtype))
    return jnp.concatenate(parts).reshape(-1, PACK_W)


def _unpack(flat, shapes):
    out = []
    row = 0
    for shp in shapes:
        n = 1
        for s in shp:
            n *= s
        r = -(-n // PACK_W)
        out.append(flat[row:row + r].reshape(-1)[:n].reshape(shp))
        row += r
    return out


_ANY = pl.BlockSpec(memory_space=pl.ANY)


def _xy_peers():
    x, y = lax.axis_index("x"), lax.axis_index("y")
    return [(1 - x, y), (x, 1 - y), (1 - x, 1 - y)]


def _all_gather_xy(wm, wv):
    def body(wm_ref, wv_ref, gm_ref, gv_ref, send_sems, recv_sems, local_sems):
        x, y, c = lax.axis_index("x"), lax.axis_index("y"), lax.axis_index("c")
        me = 2 * x + y
        local = [pltpu.make_async_copy(wm_ref, gm_ref.at[me], local_sems.at[0]),
                 pltpu.make_async_copy(wv_ref, gv_ref.at[me], local_sems.at[1])]
        for cp in local:
            cp.start()
        sends, recvs = [], []
        for j, (px, py) in enumerate(_xy_peers()):
            for b, (src, dst) in enumerate(((wm_ref, gm_ref), (wv_ref, gv_ref))):
                k = 2 * j + b
                mk = functools.partial(pltpu.make_async_remote_copy, src_ref=src, send_sem=send_sems.at[k],
                                       recv_sem=recv_sems.at[k], device_id=(px, py, c), device_id_type=MESH)
                sends.append(mk(dst_ref=dst.at[me]))
                recvs.append(mk(dst_ref=dst.at[2 * px + py]))
        for cp in sends:
            cp.start()
        for cp in recvs:
            cp.wait_recv()
        for cp in sends:
            cp.wait_send()
        for cp in local:
            cp.wait()

    return pl.pallas_call(
        body, name="all_gather_weights",
        in_specs=[_ANY, _ANY], out_specs=[_ANY, _ANY],
        out_shape=[jax.ShapeDtypeStruct((N_XY,) + wm.shape, wm.dtype), jax.ShapeDtypeStruct((N_XY,) + wv.shape, wv.dtype)],
        scratch_shapes=[pltpu.SemaphoreType.DMA((6,)), pltpu.SemaphoreType.DMA((6,)), pltpu.SemaphoreType.DMA((2,))],
    )(wm, wv)


def _exchange_xy(gsend):
    def body(gs_ref, recv_ref, send_sems, recv_sems):
        c = lax.axis_index("c")
        sends = []
        for j, (px, py) in enumerate(_xy_peers()):
            sends.append(pltpu.make_async_remote_copy(
                src_ref=gs_ref.at[2 * px + py], dst_ref=recv_ref.at[j], send_sem=send_sems.at[j], recv_sem=recv_sems.at[j],
                device_id=(px, py, c), device_id_type=MESH))
        for cp in sends:
            cp.start()
        for cp in sends:
            cp.wait_recv()
        for cp in sends:
            cp.wait_send()

    return pl.pallas_call(
        body, name="exchange_grads",
        in_specs=[_ANY], out_specs=_ANY,
        out_shape=jax.ShapeDtypeStruct((N_XY - 1,) + gsend.shape[1:], gsend.dtype),
        scratch_shapes=[pltpu.SemaphoreType.DMA((3,)), pltpu.SemaphoreType.DMA((3,))],
    )(gsend)


def _swap_with_sibling(part):
    def body(p_ref, got_ref, send_sem, recv_sem):
        x, y, c = lax.axis_index("x"), lax.axis_index("y"), lax.axis_index("c")
        cp = pltpu.make_async_remote_copy(src_ref=p_ref, dst_ref=got_ref, send_sem=send_sem, recv_sem=recv_sem,
                                          device_id=(x, y, 1 - c), device_id_type=MESH)
        cp.start()
        cp.wait_recv()
        cp.wait_send()

    return pl.pallas_call(
        body, name="swap_sibling",
        in_specs=[_ANY], out_specs=_ANY, out_shape=jax.ShapeDtypeStruct(part.shape, part.dtype),
        scratch_shapes=[pltpu.SemaphoreType.DMA, pltpu.SemaphoreType.DMA],
    )(part)


def _all_gather_all(vec):
    def body(v_ref, out_ref, send_sems, recv_sems, local_sem):
        x, y, c = lax.axis_index("x"), lax.axis_index("y"), lax.axis_index("c")
        me = 4 * x + 2 * y + c
        local = pltpu.make_async_copy(v_ref, out_ref.at[me], local_sem)
        local.start()
        sends, recvs = [], []
        for f in range(1, N_DEV):
            fx, fy, fc = (f >> 2) & 1, (f >> 1) & 1, f & 1
            px = (1 - x) if fx else x
            py = (1 - y) if fy else y
            pc = (1 - c) if fc else c
            mk = functools.partial(pltpu.make_async_remote_copy, src_ref=v_ref, send_sem=send_sems.at[f - 1],
                                   recv_sem=recv_sems.at[f - 1], device_id=(px, py, pc), device_id_type=MESH)
            sends.append(mk(dst_ref=out_ref.at[me]))
            recvs.append(mk(dst_ref=out_ref.at[4 * px + 2 * py + pc]))
        for cp in sends:
            cp.start()
        for cp in recvs:
            cp.wait_recv()
        for cp in sends:
            cp.wait_send()
        local.wait()

    return pl.pallas_call(
        body, name="all_gather_replicated",
        in_specs=[_ANY], out_specs=_ANY, out_shape=jax.ShapeDtypeStruct((N_DEV,) + vec.shape, vec.dtype),
        scratch_shapes=[pltpu.SemaphoreType.DMA((N_DEV - 1,)), pltpu.SemaphoreType.DMA((N_DEV - 1,)), pltpu.SemaphoreType.DMA],
    )(vec)


def _adamw(g, w, m, v):
    m2 = ADAM_B1 * m + (1.0 - ADAM_B1) * g
    v2 = ADAM_B2 * v + (1.0 - ADAM_B2) * g * g
    m_hat = m2 / (1.0 - ADAM_B1 ** ADAM_STEP)
    v_hat = v2 / (1.0 - ADAM_B2 ** ADAM_STEP)
    return -ADAM_LR * (m_hat / (jnp.sqrt(v_hat) + ADAM_EPS) + ADAM_WD * w), m2, v2


def _sum_contributions(own, recv):
    def body(rv, cv, pv, nv):
        return [((rv[0] + _f32(rv[1])) + _f32(rv[2])) + _f32(rv[3])], []
    (part,), _ = _rowwise("sum_grads", body, [own, recv[0], recv[1], recv[2]], [], [(PACK_W, F32)])
    return part


def _adamw_sharded(part, sib, w, m, v):
    def body(rv, cv, pv, nv):
        g = rv[0] + rv[1]
        return [g, *_adamw(g, rv[2], rv[3], rv[4])], []
    outs, _ = _rowwise("adamw_sharded", body, [part, sib, w, m, v], [], [(PACK_W, F32)] * 4)
    return outs


def _adamw_replicated(parts, w, m, v):
    def body(rv, cv, pv, nv):
        g = rv[0]
        for i in range(1, N_DEV):
            g = g + rv[i]
        return [g, *_adamw(g, rv[N_DEV], rv[N_DEV + 1], rv[N_DEV + 2])], []
    outs, _ = _rowwise("adamw_replicated", body, [parts[i] for i in range(N_DEV)] + [w, m, v], [], [(PACK_W, F32)] * 4)
    return outs


def _row(v):
    return v.reshape(1, -1).astype(F32)


def _t(w):
    return jnp.transpose(w)


def _gate_dense(gate_w):
    _, nh, blk, _ = gate_w.shape
    d = nh * blk
    dense = jnp.zeros((d, 2 * d), gate_w.dtype)
    for g in range(2):
        for h in range(nh):
            dense = lax.dynamic_update_slice(dense, gate_w[g, h], (h * blk, g * d + h * blk))
    return dense


def _gate_blocks(dense, nh):
    d = dense.shape[0]
    blk = d // nh
    return jnp.stack([jnp.stack([dense[h * blk:(h + 1) * blk, g * d + h * blk:g * d + (h + 1) * blk] for h in range(nh)])
                      for g in range(2)])


def _local_step(x3, mem3, target3, fw):
    nseq, seq, d = x3.shape
    mem_len = mem3.shape[1]
    t = nseq * seq
    x0 = x3.reshape(t, d)
    mem2 = mem3.reshape(nseq * mem_len, d)
    target = target3.reshape(t, d)
    ln = fw["ln_gains"]
    gains = [[_row(ln[i, j]) for j in range(6)] for i in range(2)]
    nh = d // RWKV_HEAD
    ones_bd = jnp.kron(jnp.eye(nh, dtype=F32), jnp.ones((RWKV_HEAD, RWKV_HEAD), F32)).astype(BF16)

    w_gate = _gate_dense(fw["a_gate_w"][0])
    pa = dict(g_pre=gains[0][0], g_post=gains[0][1], w_in=fw["a_w_in"][0], b_in_y=_row(fw["a_b_in"][0, :d]),
              b_in_u=_row(fw["a_b_in"][0, d:]), conv_w=fw["a_conv_w"][0].astype(F32), conv_b=_row(fw["a_conv_b"][0]),
              w_gate=w_gate, gate_b0=_row(fw["a_gate_b"][0, 0]), gate_b1=_row(fw["a_gate_b"][0, 1]), lam=_row(fw["a_lambda"][0]),
              w_out=fw["a_w_out"][0], b_out=_row(fw["a_b_out"][0]))
    pta = dict(w_in_t=_t(pa["w_in"]), w_gate_t=_t(w_gate), w_out_t=_t(pa["w_out"]))
    pb = dict(g_pre=gains[1][0], g_post=gains[1][1], mu=fw["b_mu"][0].astype(F32), w_r=fw["b_w_rkv"][0, 0],
              w_k=fw["b_w_rkv"][0, 1], w_v=fw["b_w_rkv"][0, 2], w0=_row(fw["b_w0"][0]), w1=fw["b_w1"][0], w2=fw["b_w2"][0],
              a0=_row(fw["b_a0"][0]), a1=fw["b_a1"][0], a2=fw["b_a2"][0], g1=fw["b_g1"][0], g2=fw["b_g2"][0],
              k_k=_row(fw["b_k_k"][0]), k_a=_row(fw["b_k_a"][0]), r_k=_row(fw["b_r_k"][0]), gn_g=_row(fw["b_gn_g"][0]),
              gn_b=_row(fw["b_gn_b"][0]), w_o=fw["b_w_o"][0], ones_bd=ones_bd)
    ptb = {k + "_t": _t(pb[k]) for k in ("w_r", "w_k", "w_v", "w_o", "w1", "w2", "a1", "a2", "g1", "g2")}
    mem_g = _row(fw["mem_norm"])

    mem_n = _norm_fwd("mem_norm", mem2, mem_g)
    x1, sv_a = _rglru_fwd(x0, pa, seq)
    x2, sv_c0 = _xattn_fwd("c0", x1, mem_n, gains[0][2], gains[0][3], fw["c_w_q"][0], fw["c_w_kv"][0], fw["c_w_o"][0], seq, mem_len)
    x3_, sv_m0 = _mlp_fwd("m0", x2, gains[0][4], gains[0][5], fw["m_w_up"][0], fw["m_w_down"][0])
    x4, sv_b = _rwkv_fwd(x3_, pb, seq)
    x5, sv_c1 = _xattn_fwd("c1", x4, mem_n, gains[1][2], gains[1][3], fw["c_w_q"][1], fw["c_w_kv"][1], fw["c_w_o"][1], seq, mem_len)
    x6, sv_m1 = _mlp_fwd("m1", x5, gains[1][4], gains[1][5], fw["m_w_up"][1], fw["m_w_down"][1])

    def loss_body(rv, cv, pv, nv):
        err = rv[0] - rv[1]
        return [err * (1.0 / d)], [err * err]
    (dx,), (sq,) = _rowwise("loss", loss_body, [x6, target], [], [(d, F32)], [d])
    loss_part = 0.5 / d * jnp.sum(sq)

    dx, g_m1 = _mlp_bwd("m1", sv_m1, dx, gains[1][4], gains[1][5], _t(fw["m_w_up"][1]), _t(fw["m_w_down"][1]))
    dx, dmem1, g_c1 = _xattn_bwd("c1", sv_c1, dx, mem_n, gains[1][2], gains[1][3], _t(fw["c_w_q"][1]), _t(fw["c_w_kv"][1]),
                                 _t(fw["c_w_o"][1]), seq, mem_len)
    dx, g_b = _rwkv_bwd(sv_b, dx, pb, ptb, seq)
    dx, g_m0 = _mlp_bwd("m0", sv_m0, dx, gains[0][4], gains[0][5], _t(fw["m_w_up"][0]), _t(fw["m_w_down"][0]))
    dx, dmem0, g_c0 = _xattn_bwd("c0", sv_c0, dx, mem_n, gains[0][2], gains[0][3], _t(fw["c_w_q"][0]), _t(fw["c_w_kv"][0]),
                                 _t(fw["c_w_o"][0]), seq, mem_len)
    dx, g_a = _rglru_bwd(sv_a, dx, pa, pta, seq)

    def dmem_body(rv, cv, pv, nv):
        _, dg = _rms_bwd(rv[1] + rv[2], rv[0], cv[0])
        return [], [dg]
    _, (dmem_g,) = _rowwise("mem_norm_grad", dmem_body, [mem2, dmem0, dmem1], [mem_g], [], [d])

    lru_heads = fw["a_gate_w"].shape[2]
    blk = d // lru_heads
    grads = dict(
        ln_gains=jnp.stack([jnp.stack([g_a["g_pre"], g_a["g_post"], g_c0["g_pre"], g_c0["g_post"], g_m0["g_pre"], g_m0["g_post"]]),
                            jnp.stack([g_b["g_pre"], g_b["g_post"], g_c1["g_pre"], g_c1["g_post"], g_m1["g_pre"], g_m1["g_post"]])]),
        mem_norm=dmem_g.sum(0),
        a_conv_w=g_a["conv_w"][None], a_conv_b=g_a["conv_b"][None], a_w_in=g_a["w_in"][None], a_b_in=g_a["b_in"][None],
        a_gate_w=_gate_blocks(g_a["w_gate"], lru_heads)[None],
        a_gate_b=jnp.stack([g_a["gate_b0"], g_a["gate_b1"]]).reshape(1, 2, lru_heads, blk),
        a_lambda=g_a["lam"][None], a_w_out=g_a["w_out"][None], a_b_out=g_a["b_out"][None],
        b_mu=g_b["mu"][None], b_w_rkv=jnp.stack([g_b["w_r"], g_b["w_k"], g_b["w_v"]])[None], b_w0=g_b["w0"][None],
        b_w1=g_b["w1"][None], b_w2=g_b["w2"][None], b_a0=g_b["a0"][None], b_a1=g_b["a1"][None], b_a2=g_b["a2"][None],
        b_g1=g_b["g1"][None], b_g2=g_b["g2"][None], b_k_k=g_b["k_k"][None], b_k_a=g_b["k_a"][None],
        b_r_k=g_b["r_k"].reshape(1, nh, RWKV_HEAD), b_gn_g=g_b["gn_g"][None], b_gn_b=g_b["gn_b"][None], b_w_o=g_b["w_o"][None],
        c_w_q=jnp.stack([g_c0["w_q"], g_c1["w_q"]]), c_w_kv=jnp.stack([g_c0["w_kv"], g_c1["w_kv"]]),
        c_w_o=jnp.stack([g_c0["w_o"], g_c1["w_o"]]),
        m_w_up=jnp.stack([g_m0["w_up"], g_m1["w_up"]]), m_w_down=jnp.stack([g_m0["w_down"], g_m1["w_down"]]),
    )
    return loss_part, dx.reshape(nseq, seq, d), grads


def kernel(x, mem, ln_gains, mem_norm, a_conv_w, a_conv_b, a_w_in, a_b_in, a_gate_w, a_gate_b, a_lambda, a_w_out, a_b_out, b_mu, b_w_rkv, b_w0, b_w1, b_w2, b_a0, b_a1, b_a2, b_g1, b_g2, b_k_k, b_k_a, b_r_k, b_gn_g, b_gn_b, b_w_o, c_w_q, c_w_kv, c_w_o, m_w_up, m_w_down, loss_target, m_ln_gains, m_mem_norm, m_a_conv_w, m_a_conv_b, m_a_w_in, m_a_b_in, m_a_gate_w, m_a_gate_b, m_a_lambda, m_a_w_out, m_a_b_out, m_b_mu, m_b_w_rkv, m_b_w0, m_b_w1, m_b_w2, m_b_a0, m_b_a1, m_b_a2, m_b_g1, m_b_g2, m_b_k_k, m_b_k_a, m_b_r_k, m_b_gn_g, m_b_gn_b, m_b_w_o, m_c_w_q, m_c_w_kv, m_c_w_o, m_m_w_up, m_m_w_down, v_ln_gains, v_mem_norm, v_a_conv_w, v_a_conv_b, v_a_w_in, v_a_b_in, v_a_gate_w, v_a_gate_b, v_a_lambda, v_a_w_out, v_a_b_out, v_b_mu, v_b_w_rkv, v_b_w0, v_b_w1, v_b_w2, v_b_a0, v_b_a1, v_b_a2, v_b_g1, v_b_g2, v_b_k_k, v_b_k_a, v_b_r_k, v_b_gn_g, v_b_gn_b, v_b_w_o, v_c_w_q, v_c_w_kv, v_c_w_o, v_m_w_up, v_m_w_down):
    given = dict(locals())
    w = {n: given[n] for n in _WEIGHTS}
    mom1 = {n: given["m_" + n] for n in _WEIGHTS}
    mom2 = {n: given["v_" + n] for n in _WEIGHTS}

    gm, gv = _all_gather_xy(_pack([w[n] for n in _MATRICES], BF16), _pack([w[n] for n in _VECTORS], F32, SUBLANES))
    fw = {n: w[n] for n in _REPLICATED}
    for names, buf in ((_MATRICES, gm), (_VECTORS, gv)):
        shards = [_unpack(buf[s], [w[n].shape for n in names]) for s in range(N_XY)]
        for i, n in enumerate(names):
            fw[n] = jnp.concatenate([shards[s][i] for s in range(N_XY)], axis=_SHARD_AXIS[n])

    loss_part, grad_x, grads = _local_step(x, mem, loss_target, fw)

    def shard_of(n, s):
        ax = _SHARD_AXIS[n]
        size = w[n].shape[ax]
        return lax.slice_in_dim(grads[n], s * size, (s + 1) * size, axis=ax)
    gfull = jnp.stack([_pack([shard_of(n, s) for n in _SHARDED], F32) for s in range(N_XY)])
    me = 2 * lax.axis_index("x") + lax.axis_index("y")
    recv = _exchange_xy(gfull.astype(BF16))
    part = _sum_contributions(lax.dynamic_index_in_dim(gfull, me, 0, keepdims=False), recv)
    sib = _swap_with_sibling(part)
    flat = [_pack([src[n] for n in _SHARDED], F32) for src in (w, mom1, mom2)]
    sharded_out = [_unpack(o, [w[n].shape for n in _SHARDED]) for o in _adamw_sharded(part, sib, *flat)]

    small = _pack([grads[n] for n in _REPLICATED] + [loss_part.reshape(1)], F32, SUBLANES)
    parts = _all_gather_all(small)
    zero = jnp.zeros((1,), F32)
    flat = [_pack([src[n] for n in _REPLICATED] + [zero], F32, SUBLANES) for src in (w, mom1, mom2)]
    repl_out = [_unpack(o, [w[n].shape for n in _REPLICATED] + [(1,)]) for o in _adamw_replicated(parts, *flat)]
    loss = repl_out[0][-1][0]

    result = [loss, grad_x]
    for kind in range(4):
        by_name = dict(zip(_SHARDED, sharded_out[kind])) | dict(zip(_REPLICATED, repl_out[kind][:-1]))
        result += [by_name[n] for n in _WEIGHTS]
    return tuple(result)
```

```python
import functools

import jax
import jax.numpy as jnp
from jax import lax
from jax.experimental import pallas as pl
from jax.experimental.pallas import tpu as pltpu

F32 = jnp.float32
BF16 = jnp.bfloat16
MESH = pl.DeviceIdType.MESH

LANES = 128
SUBLANES = 8
VMEM_LIMIT_BYTES = 48 * 1024 * 1024

RMS_EPS = 1e-6
LRU_C = 8.0
LRU_HEADS = 4
CONV_WIDTH = 4
RWKV_HEAD = 64
RWKV_GN_EPS = 64e-5
MEM_HEADS = 4
ADAM_LR = 0.001
ADAM_B1 = 0.9
ADAM_B2 = 0.999
ADAM_EPS = 1e-08
ADAM_WD = 0.01
ADAM_STEP = 10
WKV_CHUNK = 64

_PARAMS = functools.partial(pltpu.CompilerParams, vmem_limit_bytes=VMEM_LIMIT_BYTES)


def _tile(n, want):
    if n <= want:
        return n
    t = want
    while t >= SUBLANES:
        if n % t == 0 and t % SUBLANES == 0:
            return t
        t -= SUBLANES
    return n


def _fold8(v):
    tm, d = v.shape
    if tm == SUBLANES:
        return v
    return jnp.sum(v.reshape(tm // SUBLANES, SUBLANES, d), axis=0)


def _rowwise(name, body, rows, consts=(), out_rows=(), out_accs=(), prev=(), nxt=(), tm=512, seq=None):
    rows = [r if isinstance(r, tuple) else (r, r.shape[1], 0) for r in rows]
    rows = [r if len(r) == 4 else r + (0,) for r in rows]
    t = rows[0][0].shape[0]
    tm = _tile(t, tm)
    if seq is not None:
        tm = _tile(seq, tm)
    nblk = t // tm
    nrow, ncst, nprev, nnxt = len(rows), len(consts), len(prev), len(nxt)
    nor, noa = len(out_rows), len(out_accs)
    hb = tm // SUBLANES

    def kern(*refs):
        i = pl.program_id(0)
        rv = [r[...] for r in refs[:nrow]]
        cv = [c[...] for c in refs[nrow:nrow + ncst]]
        o = nrow + ncst
        pv = []
        for j in range(nprev):
            at_start = (i * tm) % seq == 0
            h = refs[o + j][...]
            pv.append(jnp.where(at_start, jnp.zeros_like(h), h))
        o += nprev
        nv = []
        for j in range(nnxt):
            at_end = ((i + 1) * tm) % seq == 0
            h = refs[o + j][...]
            nv.append(jnp.where(at_end, jnp.zeros_like(h), h))
        o += nnxt
        outs, accs = body(rv, cv, pv, nv)
        for j in range(nor):
            refs[o + j][...] = outs[j].astype(refs[o + j].dtype)
        o += nor
        if noa:
            @pl.when(i == 0)
            def _():
                for j in range(noa):
                    refs[o + j][...] = jnp.zeros_like(refs[o + j])
            for j in range(noa):
                refs[o + j][...] += _fold8(accs[j].astype(F32))

    in_specs = [pl.BlockSpec((tm, w), functools.partial(lambda i, c, o: (i + o, c), c=cb, o=part * nblk))
                for (_, w, cb, part) in rows]
    in_specs += [pl.BlockSpec(c.shape, lambda i: (0, 0)) for c in consts]
    in_specs += [pl.BlockSpec((SUBLANES, rows[j][1]),
                              functools.partial(lambda i, c: (jnp.maximum(i * hb - 1, 0), c), c=rows[j][2])) for j in prev]
    in_specs += [pl.BlockSpec((SUBLANES, rows[j][1]),
                              functools.partial(lambda i, c: (jnp.minimum((i + 1) * hb, t // SUBLANES - 1), c), c=rows[j][2]))
                 for j in nxt]
    out_shape = [jax.ShapeDtypeStruct((t, w), dt) for (w, dt) in out_rows]
    out_shape += [jax.ShapeDtypeStruct((SUBLANES, w), F32) for w in out_accs]
    out_specs = [pl.BlockSpec((tm, w), lambda i: (i, 0)) for (w, _) in out_rows]
    out_specs += [pl.BlockSpec((SUBLANES, w), lambda i: (0, 0)) for w in out_accs]
    args = [r[0] for r in rows] + list(consts) + [rows[j][0] for j in prev] + [rows[j][0] for j in nxt]
    res = pl.pallas_call(
        kern, name=name, grid=(nblk,), in_specs=in_specs, out_specs=out_specs, out_shape=out_shape,
        compiler_params=_PARAMS(dimension_semantics=("arbitrary",)),
    )(*args)
    return list(res[:nor]), list(res[nor:])


def _shift_down(x, halo, k):
    rolled = pltpu.roll(x, k, 0)
    row = lax.broadcasted_iota(jnp.int32, (SUBLANES, x.shape[1]), 0)
    first = jnp.where(row < k, pltpu.roll(halo, k, 0), rolled[:SUBLANES])
    if x.shape[0] == SUBLANES:
        return first
    return jnp.concatenate([first, rolled[SUBLANES:]], axis=0)


def _shift_up(x, halo, k):
    n = x.shape[0]
    rolled = pltpu.roll(x, n - k, 0)
    row = lax.broadcasted_iota(jnp.int32, (SUBLANES, x.shape[1]), 0)
    last = jnp.where(row >= SUBLANES - k, pltpu.roll(halo, SUBLANES - k, 0), rolled[n - SUBLANES:])
    if n == SUBLANES:
        return last
    return jnp.concatenate([rolled[:n - SUBLANES], last], axis=0)


class _Transposed:
    def __init__(self, w):
        self.w = w


def _t(w):
    return _Transposed(w)


def _mm(name, a, b, out_dtype=F32, trans_a=False, tm=1024, tn=1024, tk=1024):
    trans_b = isinstance(b, _Transposed)
    assert not (trans_a and trans_b)
    if trans_b:
        b = b.w
    if trans_a:
        kdim, m = a.shape
    else:
        m, kdim = a.shape
    n = b.shape[0] if trans_b else b.shape[1]
    assert b.shape[1 if trans_b else 0] == kdim, (name, a.shape, b.shape)
    tm, tn, tk = _tile(m, tm), _tile(n, tn), _tile(kdim, tk)
    nk = kdim // tk
    dims = (((0,), (0,)), ((), ())) if trans_a else (((1,), (1 if trans_b else 0,)), ((), ()))

    def kern(a_ref, b_ref, o_ref, *acc):
        part = lax.dot_general(a_ref[...].astype(BF16), b_ref[...].astype(BF16), dims, preferred_element_type=F32)
        if nk == 1:
            o_ref[...] = part.astype(o_ref.dtype)
        else:
            k = pl.program_id(2)

            @pl.when(k == 0)
            def _():
                acc[0][...] = part

            @pl.when(k > 0)
            def _():
                acc[0][...] += part

            @pl.when(k == nk - 1)
            def _():
                o_ref[...] = acc[0][...].astype(o_ref.dtype)

    a_spec = pl.BlockSpec((tk, tm), lambda i, j, k: (k, i)) if trans_a else pl.BlockSpec((tm, tk), lambda i, j, k: (i, k))
    return pl.pallas_call(
        kern, name=name, grid=(m // tm, n // tn, nk),
        in_specs=[a_spec, pl.BlockSpec((tn, tk), lambda i, j, k: (j, k)) if trans_b else pl.BlockSpec((tk, tn), lambda i, j, k: (k, j))],
        out_specs=pl.BlockSpec((tm, tn), lambda i, j, k: (i, j)),
        out_shape=jax.ShapeDtypeStruct((m, n), out_dtype),
        scratch_shapes=[] if nk == 1 else [pltpu.VMEM((tm, tn), F32)],
        compiler_params=_PARAMS(dimension_semantics=("parallel", "parallel", "arbitrary")),
    )(a, b)


def _scan(name, a, b, seq, reverse=False, tm=256):
    t, d = a.shape
    tm = _tile(seq, tm)
    nblk = t // tm
    ntile = tm // SUBLANES

    def kern(a_ref, b_ref, h_ref, carry_h, carry_a):
        i = pl.program_id(0)
        blk = (nblk - 1 - i) if reverse else i
        edge = (((blk + 1) * tm) % seq == 0) if reverse else ((blk * tm) % seq == 0)

        @pl.when(edge)
        def _():
            carry_h[...] = jnp.zeros_like(carry_h)
            carry_a[...] = jnp.zeros_like(carry_a)

        def tile_step(j, c):
            jj = (ntile - 1 - j) if reverse else j
            rows = pl.ds(pl.multiple_of(jj * SUBLANES, SUBLANES), SUBLANES)
            a8 = a_ref[rows, :]
            b8 = b_ref[rows, :]
            h, an = c
            out = [None] * SUBLANES
            order = range(SUBLANES - 1, -1, -1) if reverse else range(SUBLANES)
            for r in order:
                if reverse:
                    h = b8[r:r + 1, :] + an * h
                    an = a8[r:r + 1, :]
                else:
                    h = a8[r:r + 1, :] * h + b8[r:r + 1, :]
                out[r] = h
            h_ref[rows, :] = jnp.concatenate(out, axis=0)
            return (h, an)

        h, an = lax.fori_loop(0, ntile, tile_step, (carry_h[...], carry_a[...]))
        carry_h[...] = h
        carry_a[...] = an

    idx = (lambda i: (nblk - 1 - i, 0)) if reverse else (lambda i: (i, 0))
    return pl.pallas_call(
        kern, name=name, grid=(nblk,),
        in_specs=[pl.BlockSpec((tm, d), idx), pl.BlockSpec((tm, d), idx)],
        out_specs=pl.BlockSpec((tm, d), idx),
        out_shape=jax.ShapeDtypeStruct((t, d), F32),
        scratch_shapes=[pltpu.VMEM((1, d), F32), pltpu.VMEM((1, d), F32)],
        compiler_params=_PARAMS(dimension_semantics=("arbitrary",)),
    )(a, b)


_NN = (((1,), (0,)), ((), ()))
_NT = (((1,), (1,)), ((), ()))
_TN = (((0,), (0,)), ((), ()))


def _dot1(a, b, dims):
    return lax.dot_general(a.astype(BF16), b.astype(BF16), dims, preferred_element_type=F32)


def _dot3(a, b, dims):
    a_hi, b_hi = a.astype(BF16), b.astype(BF16)
    a_lo, b_lo = (a - a_hi.astype(F32)).astype(BF16), (b - b_hi.astype(F32)).astype(BF16)
    dg = lambda p, q: lax.dot_general(p, q, dims, preferred_element_type=F32)
    return dg(a_hi, b_hi) + (dg(a_hi, b_lo) + dg(a_lo, b_hi))


def _make_bmm(dot):
    def make(dims, da_rule, db_rule):
        @jax.custom_vjp
        def f(a, b):
            return dot(a, b, dims)

        def fwd(a, b):
            return dot(a, b, dims), (a, b)

        def bwd(res, g):
            a, b = res
            return da_rule(a, b, g), db_rule(a, b, g)

        f.defvjp(fwd, bwd)
        return f

    return dict(nn=make(_NN, lambda a, b, g: dot(g, b, _NT), lambda a, b, g: dot(a, g, _TN)),
                nt=make(_NT, lambda a, b, g: dot(g, b, _NN), lambda a, b, g: dot(g, a, _TN)),
                tn=make(_TN, lambda a, b, g: dot(b, g, _NT), lambda a, b, g: dot(a, g, _NN)))


_BMM = {1: _make_bmm(_dot1), 3: _make_bmm(_dot3)}
_WKV_PASSES = dict(pair=1, read=1, inv=3, apply=1, write=1)


def _running_sum(x, reverse):
    c = x.shape[0]
    row = lax.broadcasted_iota(jnp.int32, x.shape, 0)
    k = 1
    while k < c:
        if reverse:
            x = x + jnp.where(row < c - k, pltpu.roll(x, c - k, 0), 0.0)
        else:
            x = x + jnp.where(row >= k, pltpu.roll(x, k, 0), 0.0)
        k *= 2
    return x


@jax.custom_vjp
def _cumsum_rows(x):
    return _running_sum(x, False)


_cumsum_rows.defvjp(lambda x: (_running_sum(x, False), None), lambda _, g: (_running_sum(g, True),))


@jax.custom_vjp
def _unit_lower_inverse(nl):
    c = nl[0].shape[0]
    mm = _BMM[_WKV_PASSES["inv"]]["nn"]
    eye = jnp.where(lax.broadcasted_iota(jnp.int32, (c, c), 0) == lax.broadcasted_iota(jnp.int32, (c, c), 1), 1.0, 0.0)
    inv = [eye + z for z in nl]
    p = nl
    for _ in range(c.bit_length() - 2):
        p = [mm(z, z) for z in p]
        inv = [i_ + mm(p_, i_) for i_, p_ in zip(inv, p)]
    return inv


def _unit_lower_inverse_fwd(nl):
    inv = _unit_lower_inverse(nl)
    return inv, inv


def _unit_lower_inverse_bwd(inv, g):
    mm = _BMM[_WKV_PASSES["inv"]]
    left = [mm["tn"](x, g_) for x, g_ in zip(inv, g)]
    return ([mm["nt"](l_, x) for l_, x in zip(left, inv)],)


_unit_lower_inverse.defvjp(_unit_lower_inverse_fwd, _unit_lower_inverse_bwd)


def _wkv_chunk(r, lw, k, v, a, b, s0):
    c = r[0].shape[0]
    ti = lax.broadcasted_iota(jnp.int32, (c, c), 0)
    tj = lax.broadcasted_iota(jnp.int32, (c, c), 1)
    strict, incl = tj < ti, tj <= ti
    eye = jnp.where(ti == tj, 1.0, 0.0)
    last = lax.broadcasted_iota(jnp.int32, r[0].shape, 0) == c - 1
    each = lambda f, *ls: [f(*z) for z in zip(*ls)]
    pair, read, inv_, apply_, write = (_BMM[_WKV_PASSES[role]] for role in ("pair", "read", "inv", "apply", "write"))
    cum = each(_cumsum_rows, lw)
    w_incl = each(jnp.exp, cum)
    w_inv = each(lambda z: jnp.exp(-z), cum)
    at = each(lambda a_, c_, l_: a_ * jnp.exp(c_ - l_), a, cum, lw)
    bt = each(jnp.multiply, b, w_inv)
    kt = each(jnp.multiply, k, w_inv)
    rt = each(jnp.multiply, r, w_incl)
    nl = each(lambda x, y: jnp.where(strict, pair["nt"](x, y), 0.0), at, bt)
    mk = each(lambda x, y: jnp.where(strict, pair["nt"](x, y), 0.0), at, kt)
    rhs = each(lambda x, s, m, v_: read["nt"](x, s) + apply_["nn"](m, v_), at, s0, mk, v)
    ut = each(inv_["nn"], _unit_lower_inverse(nl), rhs)
    rb = each(lambda x, y: jnp.where(incl, pair["nt"](x, y), 0.0), rt, bt)
    rk = each(lambda x, y: jnp.where(incl, pair["nt"](x, y), 0.0), rt, kt)
    y = each(lambda x, s, m1, u, m2, v_: read["nt"](x, s) + apply_["nn"](m1, u) + apply_["nn"](m2, v_), rt, s0, rb, ut, rk, v)
    w_end = each(lambda z: jnp.exp(jnp.sum(jnp.where(last, z, 0.0), axis=0, keepdims=True)), cum)
    s1 = each(lambda s, u, b_, v_, k_, w_: (s + write["tn"](u, b_) + write["tn"](v_, k_)) * w_, s0, ut, bt, v, kt, w_end)
    return y, s1


def _wkv_fwd(r, lw, k, v, a, b, seq, hb=8):
    t, d = r.shape
    n = RWKV_HEAD
    nh = d // n
    hb = min(hb, nh)
    chunk = min(WKV_CHUNK, seq)
    ncs = seq // chunk

    def kern(r_ref, lw_ref, k_ref, v_ref, a_ref, b_ref, y_ref, st_ref, s_scr):
        @pl.when(pl.program_id(2) == 0)
        def _():
            s_scr[...] = jnp.zeros_like(s_scr)

        heads = lambda ref: [ref[:, h * n:(h + 1) * n] for h in range(hb)]
        s0 = [s_scr[h] for h in range(hb)]
        y, s1 = _wkv_chunk(heads(r_ref), heads(lw_ref), heads(k_ref), heads(v_ref), heads(a_ref), heads(b_ref), s0)
        for h in range(hb):
            st_ref[0, h] = s0[h]
            y_ref[:, h * n:(h + 1) * n] = y[h]
            s_scr[h] = s1[h]

    vec = pl.BlockSpec((chunk, hb * n), lambda bb, g, c: (bb * ncs + c, g))
    return pl.pallas_call(
        kern, name="wkv_fwd", grid=(t // seq, nh // hb, ncs), in_specs=[vec] * 6,
        out_specs=[vec, pl.BlockSpec((1, hb, n, n), lambda bb, g, c: (bb * ncs + c, g, 0, 0))],
        out_shape=[jax.ShapeDtypeStruct((t, d), F32), jax.ShapeDtypeStruct((t // chunk, nh, n, n), F32)],
        scratch_shapes=[pltpu.VMEM((hb, n, n), F32)],
        compiler_params=_PARAMS(dimension_semantics=("parallel", "parallel", "arbitrary")),
    )(r, lw, k, v, a, b)


def _wkv_bwd(r, lw, k, v, a, b, st, dy, seq, hb=8):
    t, d = r.shape
    n = RWKV_HEAD
    nh = d // n
    hb = min(hb, nh)
    chunk = min(WKV_CHUNK, seq)
    ncs = seq // chunk

    def kern(r_ref, lw_ref, k_ref, v_ref, a_ref, b_ref, st_ref, dy_ref,
             dr_ref, dlw_ref, dk_ref, dv_ref, da_ref, db_ref, ds_scr):
        @pl.when(pl.program_id(2) == 0)
        def _():
            ds_scr[...] = jnp.zeros_like(ds_scr)

        heads = lambda ref: [ref[:, h * n:(h + 1) * n] for h in range(hb)]
        _, vjp = jax.vjp(_wkv_chunk, heads(r_ref), heads(lw_ref), heads(k_ref), heads(v_ref), heads(a_ref), heads(b_ref),
                         [st_ref[0, h] for h in range(hb)])
        grads = vjp((heads(dy_ref), [ds_scr[h] for h in range(hb)]))
        for h in range(hb):
            for ref, g in zip((dr_ref, dlw_ref, dk_ref, dv_ref, da_ref, db_ref), grads[:6]):
                ref[:, h * n:(h + 1) * n] = g[h]
            ds_scr[h] = grads[6][h]

    vec = pl.BlockSpec((chunk, hb * n), lambda bb, g, c: (bb * ncs + ncs - 1 - c, g))
    return pl.pallas_call(
        kern, name="wkv_bwd", grid=(t // seq, nh // hb, ncs),
        in_specs=[vec] * 6 + [pl.BlockSpec((1, hb, n, n), lambda bb, g, c: (bb * ncs + ncs - 1 - c, g, 0, 0)), vec],
        out_specs=[vec] * 6, out_shape=[jax.ShapeDtypeStruct((t, d), F32)] * 6,
        scratch_shapes=[pltpu.VMEM((hb, n, n), F32)],
        compiler_params=_PARAMS(dimension_semantics=("parallel", "parallel", "arbitrary")),
    )(r, lw, k, v, a, b, st, dy)


def _softmax_rows(s):
    e = jnp.exp(s - jnp.max(s, axis=-1, keepdims=True))
    return e / jnp.sum(e, axis=-1, keepdims=True)


def _attn_fwd(q, kv, seq, mem_len, tq=256):
    t, d = q.shape
    dh = d // MEM_HEADS
    scale = dh ** -0.5
    tq = _tile(seq, tq)
    nq = seq // tq

    def kern(q_ref, kv_ref, o_ref):
        for h in range(MEM_HEADS):
            cols = slice(h * dh, (h + 1) * dh)
            vcols = slice(d + h * dh, d + (h + 1) * dh)
            s = lax.dot_general(q_ref[:, cols], kv_ref[:, cols], _NT, preferred_element_type=F32) * scale
            p = _softmax_rows(s)
            o_ref[:, cols] = jnp.dot(p.astype(BF16), kv_ref[:, vcols], preferred_element_type=F32).astype(o_ref.dtype)

    return pl.pallas_call(
        kern, name="attn_fwd", grid=(t // seq, nq),
        in_specs=[pl.BlockSpec((tq, d), lambda b, i: (b * nq + i, 0)), pl.BlockSpec((mem_len, 2 * d), lambda b, i: (b, 0))],
        out_specs=pl.BlockSpec((tq, d), lambda b, i: (b * nq + i, 0)),
        out_shape=jax.ShapeDtypeStruct((t, d), BF16),
        compiler_params=_PARAMS(dimension_semantics=("parallel", "parallel")),
    )(q, kv)


def _attn_bwd(q, kv, do, seq, mem_len, tq=256):
    t, d = q.shape
    dh = d // MEM_HEADS
    scale = dh ** -0.5
    tq = _tile(seq, tq)
    nq = seq // tq

    def kern(q_ref, kv_ref, do_ref, dq_ref, dkv_ref):
        @pl.when(pl.program_id(1) == 0)
        def _():
            dkv_ref[...] = jnp.zeros_like(dkv_ref)

        for h in range(MEM_HEADS):
            cols = slice(h * dh, (h + 1) * dh)
            vcols = slice(d + h * dh, d + (h + 1) * dh)
            qh, kh, vh, doh = q_ref[:, cols], kv_ref[:, cols], kv_ref[:, vcols], do_ref[:, cols]
            p = _softmax_rows(lax.dot_general(qh, kh, _NT, preferred_element_type=F32) * scale)
            dp = lax.dot_general(doh, vh, _NT, preferred_element_type=F32)
            ds = (p * (dp - jnp.sum(p * dp, axis=-1, keepdims=True)) * scale).astype(BF16)
            dq_ref[:, cols] = jnp.dot(ds, kh, preferred_element_type=F32).astype(dq_ref.dtype)
            dkv_ref[:, cols] += lax.dot_general(ds, qh, _TN, preferred_element_type=F32)
            dkv_ref[:, vcols] += lax.dot_general(p.astype(BF16), doh, _TN, preferred_element_type=F32)

    return pl.pallas_call(
        kern, name="attn_bwd", grid=(t // seq, nq),
        in_specs=[pl.BlockSpec((tq, d), lambda b, i: (b * nq + i, 0)), pl.BlockSpec((mem_len, 2 * d), lambda b, i: (b, 0)),
                  pl.BlockSpec((tq, d), lambda b, i: (b * nq + i, 0))],
        out_specs=[pl.BlockSpec((tq, d), lambda b, i: (b * nq + i, 0)), pl.BlockSpec((mem_len, 2 * d), lambda b, i: (b, 0))],
        out_shape=[jax.ShapeDtypeStruct((t, d), BF16), jax.ShapeDtypeStruct(kv.shape, F32)],
        compiler_params=_PARAMS(dimension_semantics=("parallel", "arbitrary")),
    )(q, kv, do)


def _rstd(x):
    return lax.rsqrt(jnp.mean(x * x, axis=-1, keepdims=True) + RMS_EPS)


def _rms(x, g):
    return x * _rstd(x) * g


def _rms_bwd(dy, x, g):
    rstd = _rstd(x)
    xhat = x * rstd
    dxhat = dy * g
    return rstd * (dxhat - xhat * jnp.mean(dxhat * xhat, axis=-1, keepdims=True)), dy * xhat


def _softplus(x):
    return jnp.maximum(x, 0.0) + jnp.log1p(jnp.exp(-jnp.abs(x)))


def _one_minus_exp(x):
    series = -x * (1.0 + x * (0.5 + x * (1.0 / 6.0 + x * (1.0 / 24.0 + x * (1.0 / 120.0)))))
    return jnp.where(x > -0.05, series, 1.0 - jnp.exp(x))


_GELU_C = 0.7978845608028654
_GELU_K = 0.044715


def _gelu(x):
    return 0.5 * x * (1.0 + jnp.tanh(_GELU_C * (x + _GELU_K * x * x * x)))


def _gelu_grad(x):
    th = jnp.tanh(_GELU_C * (x + _GELU_K * x * x * x))
    return 0.5 * (1.0 + th) + 0.5 * x * (1.0 - th * th) * _GELU_C * (1.0 + 3.0 * _GELU_K * x * x)


def _seg_sum(x, seg):
    def two_terms(v, dims):
        hi = v.astype(BF16)
        lo = (v - hi.astype(F32)).astype(BF16)
        return (lax.dot_general(hi, seg, dims, preferred_element_type=F32)
                + lax.dot_general(lo, seg, dims, preferred_element_type=F32))
    return two_terms(two_terms(x, _NN), _NT)


def _f32(v):
    return v.astype(F32)


def _norm_fwd(name, x, g, dtype=BF16):
    (hn,), _ = _rowwise(name, lambda rv, cv, pv, nv: ([_rms(_f32(rv[0]), cv[0])], []), [x], [g], [(x.shape[1], dtype)])
    return hn


def _resid_norm_fwd(name, x, t, g, bias=None):
    def body(rv, cv, pv, nv):
        tt = rv[1] if bias is None else rv[1] + cv[1]
        return [rv[0] + _rms(tt, cv[0])], []
    (y,), _ = _rowwise(name, body, [x, t], [g] if bias is None else [g, bias], [(x.shape[1], F32)])
    return y


def _resid_norm_bwd(name, dxn, t, g, bias=None):
    def body(rv, cv, pv, nv):
        tt = rv[1] if bias is None else rv[1] + cv[1]
        dt, dg = _rms_bwd(rv[0], tt, cv[0])
        return [dt], [dg, dt]
    d = t.shape[1]
    (dt,), (dg, db) = _rowwise(name, body, [dxn, t], [g] if bias is None else [g, bias], [(d, BF16)], [d, d])
    return dt, dg.sum(0), db.sum(0)


def _prenorm_bwd(name, dxn, dhn, x, g):
    def body(rv, cv, pv, nv):
        dx, dg = _rms_bwd(_f32(rv[1]), rv[2], cv[0])
        return [rv[0] + dx], [dg]
    d = x.shape[1]
    (dx,), (dg,) = _rowwise(name, body, [dxn, dhn, x], [g], [(d, F32)], [d])
    return dx, dg.sum(0)


def _mlp_fwd(tag, x, g_pre, g_post, w_up, w_down):
    hn = _norm_fwd(tag + "_norm", x, g_pre)
    up = _mm(tag + "_up", hn, w_up)
    dff = up.shape[1]

    def body(rv, cv, pv, nv):
        r = jnp.maximum(rv[0], 0.0)
        return [r * r], []
    (act,), _ = _rowwise(tag + "_act", body, [up], [], [(dff, BF16)], tm=256)
    m = _mm(tag + "_down", act, w_down)
    y = _resid_norm_fwd(tag + "_res", x, m, g_post)
    return y, (x, hn, up, act, m)


def _mlp_bwd(tag, saved, dy, g_pre, g_post, w_up_t, w_down_t):
    x, hn, up, act, m = saved
    dm, dg_post, _ = _resid_norm_bwd(tag + "_dres", dy, m, g_post)
    dact = _mm(tag + "_dact", dm, w_down_t)
    dw_down = _mm(tag + "_dwdown", act, dm, trans_a=True)

    def body(rv, cv, pv, nv):
        return [rv[0] * 2.0 * jnp.maximum(rv[1], 0.0)], []
    (dup,), _ = _rowwise(tag + "_dup", body, [dact, up], [], [(up.shape[1], BF16)], tm=256)
    dw_up = _mm(tag + "_dwup", hn, dup, trans_a=True)
    dhn = _mm(tag + "_dhn", dup, w_up_t)
    dx, dg_pre = _prenorm_bwd(tag + "_dnorm", dy, dhn, x, g_pre)
    return dx, dict(g_pre=dg_pre, g_post=dg_post, w_up=dw_up, w_down=dw_down)


def _xattn_fwd(tag, x, mem_n, g_pre, g_post, w_q, w_kv, w_o, seq, mem_len):
    hn = _norm_fwd(tag + "_norm", x, g_pre)
    q = _mm(tag + "_q", hn, w_q, out_dtype=BF16)
    kv = _mm(tag + "_kv", mem_n, w_kv, out_dtype=BF16)
    o = _attn_fwd(q, kv, seq, mem_len)
    c = _mm(tag + "_o", o, w_o)
    y = _resid_norm_fwd(tag + "_res", x, c, g_post)
    return y, (x, hn, q, kv, o, c)


def _xattn_bwd(tag, saved, dy, mem_n, g_pre, g_post, w_q_t, w_kv_t, w_o_t, seq, mem_len):
    x, hn, q, kv, o, c = saved
    dc, dg_post, _ = _resid_norm_bwd(tag + "_dres", dy, c, g_post)
    do = _mm(tag + "_do", dc, w_o_t, out_dtype=BF16)
    dw_o = _mm(tag + "_dwo", o, dc, trans_a=True)
    dq, dkv = _attn_bwd(q, kv, do, seq, mem_len)
    dw_q = _mm(tag + "_dwq", hn, dq, trans_a=True)
    dhn = _mm(tag + "_dhn", dq, w_q_t)
    dw_kv = _mm(tag + "_dwkv", mem_n, dkv, trans_a=True)
    dmem_n = _mm(tag + "_dmem", dkv, w_kv_t)
    dx, dg_pre = _prenorm_bwd(tag + "_dnorm", dy, dhn, x, g_pre)
    return dx, dmem_n, dict(g_pre=dg_pre, g_post=dg_post, w_q=dw_q, w_kv=dw_kv, w_o=dw_o)


def _lru_gates(z0, z1, gb0, gb1, sp):
    r = jax.nn.sigmoid(z0 + gb0)
    i = jax.nn.sigmoid(z1 + gb1)
    log_a = -LRU_C * r * sp
    a = jnp.exp(log_a)
    mult = jnp.sqrt(_one_minus_exp(2.0 * log_a))
    return r, i, a, mult


def _rglru_fwd(x, p, seq):
    d = x.shape[1]
    hn = _norm_fwd("a_norm", x, p["g_pre"])
    proj = _mm("a_in", hn, p["w_in"])

    def conv_body(rv, cv, pv, nv):
        u = rv[0] + cv[0]
        halo = pv[0] + cv[0]
        i = pl.program_id(0)
        halo = jnp.where((i * rv[0].shape[0]) % seq == 0, jnp.zeros_like(halo), halo)
        conv = cv[2] + u * cv[1][CONV_WIDTH - 1:CONV_WIDTH]
        for tap in range(CONV_WIDTH - 1):
            conv = conv + _shift_down(u, halo, CONV_WIDTH - 1 - tap) * cv[1][tap:tap + 1]
        return [conv], []
    (conv,), _ = _rowwise("a_conv", conv_body, [(proj, d, 1)], [p["b_in_u"], p["conv_w"], p["conv_b"]], [(d, F32)],
                          prev=[0], seq=seq)
    z = _mm("a_gate", conv, p["w_gate"])

    def gate_body(rv, cv, pv, nv):
        r, i, a, mult = _lru_gates(rv[0], rv[1], cv[0], cv[1], _softplus(-cv[2]))
        return [a, mult * i * rv[2]], []
    (a, bb), _ = _rowwise("a_gates", gate_body, [(z, d, 0), (z, d, 1), conv], [p["gate_b0"], p["gate_b1"], p["lam"]],
                          [(d, F32), (d, F32)])
    h = _scan("a_scan", a, bb, seq)

    def hy_body(rv, cv, pv, nv):
        return [rv[0] * _gelu(rv[1] + cv[0])], []
    (hy,), _ = _rowwise("a_hy", hy_body, [h, (proj, d, 0)], [p["b_in_y"]], [(d, BF16)])
    out = _mm("a_out", hy, p["w_out"])
    y = _resid_norm_fwd("a_res", x, out, p["g_post"], bias=p["b_out"])
    return y, (x, hn, proj, conv, z, a, h, hy, out)


def _rglru_bwd(saved, dy, p, pt, seq):
    x, hn, proj, conv, z, a, h, hy, out = saved
    d = x.shape[1]
    dt, dg_post, db_out = _resid_norm_bwd("a_dres", dy, out, p["g_post"], bias=p["b_out"])
    dhy = _mm("a_dhy", dt, pt["w_out_t"])
    dw_out = _mm("a_dwout", hy, dt, trans_a=True)

    def dh_body(rv, cv, pv, nv):
        yb = rv[2] + cv[0]
        return [rv[0] * _gelu(yb), rv[0] * rv[1] * _gelu_grad(yb)], []
    (dh, dyb), _ = _rowwise("a_dh", dh_body, [dhy, h, (proj, d, 0)], [p["b_in_y"]], [(d, F32), (d, BF16)])
    g = _scan("a_rscan", a, dh, seq, reverse=True)

    def dgate_body(rv, cv, pv, nv):
        gg, hh, z0, z1, cnv = rv
        sp = _softplus(-cv[2])
        r, i, aa, mult = _lru_gates(z0, z1, cv[0], cv[1], sp)
        i_blk = pl.program_id(0)
        halo = jnp.where((i_blk * gg.shape[0]) % seq == 0, jnp.zeros_like(pv[0]), pv[0])
        da = gg * _shift_down(hh, halo, 1)
        dmult = gg * i * cnv
        di = gg * mult * cnv
        dconv = gg * mult * i
        dlog_a = da * aa - dmult * aa * aa / mult
        dz0 = dlog_a * (-LRU_C * sp) * r * (1.0 - r)
        dz1 = di * i * (1.0 - i)
        dsp = dlog_a * (-LRU_C * r)
        dlam = dsp * (-jax.nn.sigmoid(-cv[2]))
        return [jnp.concatenate([dz0, dz1], axis=1), dconv], [dz0, dz1, dlam]
    (dz, dconv1), (dgb0, dgb1, dlam) = _rowwise(
        "a_dgates", dgate_body, [g, h, (z, d, 0), (z, d, 1), conv], [p["gate_b0"], p["gate_b1"], p["lam"]],
        [(2 * d, BF16), (d, F32)], [d, d, d], prev=[1], seq=seq)
    dconv2 = _mm("a_dconv", dz, pt["w_gate_t"])
    dw_gate = _mm("a_dwgate", conv, dz, trans_a=True)

    def dconv_body(rv, cv, pv, nv):
        dc1, dc2, pu, dyb_ = rv
        dc = dc1 + dc2
        dc_next = nv[0] + nv[1]
        u = pu + cv[0]
        i_blk = pl.program_id(0)
        halo = jnp.where((i_blk * u.shape[0]) % seq == 0, jnp.zeros_like(pv[0]), pv[0] + cv[0])
        du = dc * cv[1][CONV_WIDTH - 1:CONV_WIDTH]
        dws = []
        for tap in range(CONV_WIDTH - 1):
            k = CONV_WIDTH - 1 - tap
            du = du + _shift_up(dc, dc_next, k) * cv[1][tap:tap + 1]
            dws.append(dc * _shift_down(u, halo, k))
        dws.append(dc * u)
        return [jnp.concatenate([_f32(dyb_), du], axis=1)], dws + [dc, _f32(dyb_), du]
    (dproj,), accs = _rowwise(
        "a_dconvw", dconv_body, [dconv1, dconv2, (proj, d, 1), dyb], [p["b_in_u"], p["conv_w"]],
        [(2 * d, BF16)], [d] * (CONV_WIDTH + 3), prev=[2], nxt=[0, 1], seq=seq)
    dconv_w = jnp.stack([acc.sum(0) for acc in accs[:CONV_WIDTH]])
    dconv_b = accs[CONV_WIDTH].sum(0)
    db_in = jnp.concatenate([accs[CONV_WIDTH + 1].sum(0), accs[CONV_WIDTH + 2].sum(0)])
    dhn = _mm("a_dhn", dproj, pt["w_in_t"])
    dw_in = _mm("a_dwin", hn, dproj, trans_a=True)
    dx, dg_pre = _prenorm_bwd("a_dnorm", dy, dhn, x, p["g_pre"])
    grads = dict(g_pre=dg_pre, g_post=dg_post, b_out=db_out, w_out=dw_out, gate_b0=dgb0.sum(0), gate_b1=dgb1.sum(0),
                 lam=dlam.sum(0), w_gate=dw_gate, conv_w=dconv_w, conv_b=dconv_b, b_in=db_in, w_in=dw_in)
    return dx, grads


def _rwkv_prep(k, wl, za, w0, a0, k_k, k_a, seg):
    w_in = wl + w0
    e_w = jnp.exp(-_softplus(-w_in) - 0.5)
    a = jax.nn.sigmoid(za + a0)
    q = k * k_k
    norm = jnp.sqrt(_seg_sum(q * q, seg))
    n = jnp.maximum(norm, 1e-12)
    kk = q / n
    return w_in, e_w, a, norm, n, kk


def _rwkv_out(y, r, k2, v, gn_g, gn_b, r_k, seg):
    inv = 1.0 / RWKV_HEAD
    yc = y - _seg_sum(y, seg) * inv
    rstd = lax.rsqrt(_seg_sum(yc * yc, seg) * inv + RWKV_GN_EPS)
    yhat = yc * rstd
    s = _seg_sum(r * k2 * r_k, seg)
    return rstd, yhat, s, yhat * gn_g + gn_b + s * v


def _rwkv_fwd(x, p, seq):
    t, d = x.shape
    nseq = t // seq

    def mix_body(rv, cv, pv, nv):
        hn = _rms(rv[0], cv[0])
        xx = _shift_down(hn, _rms(pv[0], cv[0]), 1) - hn
        return [hn] + [hn + xx * cv[1][c:c + 1] for c in range(6)], []
    (hn, xr, xw, xk, xv, xa, xg), _ = _rowwise("b_mix", mix_body, [x], [p["g_pre"], p["mu"]],
                                               [(d, F32)] + [(d, BF16)] * 6, prev=[0], seq=seq)
    r = _mm("b_r", xr, p["w_r"])
    k = _mm("b_k", xk, p["w_k"])
    v = _mm("b_v", xv, p["w_v"])
    lw = _mm("b_w1", xw, p["w1"])
    la = _mm("b_a1", xa, p["a1"], out_dtype=BF16)
    lg = _mm("b_g1", xg, p["g1"])
    (th,), _ = _rowwise("b_tanh", lambda rv, cv, pv, nv: ([jnp.tanh(rv[0])], []), [lw], [], [(lw.shape[1], BF16)])
    (sg,), _ = _rowwise("b_sig", lambda rv, cv, pv, nv: ([jax.nn.sigmoid(rv[0])], []), [lg], [], [(lg.shape[1], BF16)])
    wl = _mm("b_w2", th, p["w2"])
    za = _mm("b_a2", la, p["a2"])
    g = _mm("b_g2", sg, p["g2"])

    def prep_body(rv, cv, pv, nv):
        kk_, wl_, za_ = rv
        _, e_w, a, _, _, kk = _rwkv_prep(kk_, wl_, za_, cv[0], cv[1], cv[2], cv[3], cv[4])
        return [-e_w, kk_ * (1.0 + (a - 1.0) * cv[3]), -kk, kk * a], []
    (log_w, k2, rem_a, rem_b), _ = _rowwise("b_prep", prep_body, [k, wl, za],
                                            [p["w0"], p["a0"], p["k_k"], p["k_a"], p["seg"]], [(d, F32)] * 4, tm=256)
    rec_in = (r, log_w, k2, v, rem_a, rem_b)
    y, states = _wkv_fwd(*rec_in, seq)

    def out_body(rv, cv, pv, nv):
        y_, r_, k2_, v_, g_ = rv
        _, _, _, out = _rwkv_out(y_, r_, k2_, v_, cv[0], cv[1], cv[2], cv[3])
        return [out * g_], []
    (og,), _ = _rowwise("b_out", out_body, [y, r, k2, v, g], [p["gn_g"], p["gn_b"], p["r_k"], p["seg"]], [(d, BF16)], tm=256)
    o = _mm("b_o", og, p["w_o"])
    res = _resid_norm_fwd("b_res", x, o, p["g_post"])
    return res, (x, hn, xr, xw, xk, xv, xa, xg, r, k, v, th, la, sg, wl, za, g, k2, rec_in, states, y, og, o)


def _rwkv_bwd(saved, dres, p, pt, seq):
    x, hn, xr, xw, xk, xv, xa, xg, r, k, v, th, la, sg, wl, za, g, k2, rec_in, states, y, og, o = saved
    t, d = x.shape
    nseq = t // seq
    do, dg_post, _ = _resid_norm_bwd("b_dres", dres, o, p["g_post"])
    dog = _mm("b_dog", do, pt["w_o_t"])
    dw_o = _mm("b_dwo", og, do, trans_a=True)

    def dout_body(rv, cv, pv, nv):
        dog_, y_, r_, k2_, v_, g_ = rv
        gn_g, gn_b, r_k, bd = cv
        inv = 1.0 / RWKV_HEAD
        rstd, yhat, s, out = _rwkv_out(y_, r_, k2_, v_, gn_g, gn_b, r_k, bd)
        dout = dog_ * g_
        ds = _seg_sum(dout * v_, bd)
        dyhat = dout * gn_g
        dy = rstd * (dyhat - _seg_sum(dyhat, bd) * inv - yhat * _seg_sum(dyhat * yhat, bd) * inv)
        return [dy, dog_ * out, dout * s, ds * k2_ * r_k, ds * r_ * r_k], [ds * r_ * k2_, dout * yhat, dout]
    (dy, dgate, dv_b, dr_b, dk2_b), (dr_k, dgn_g, dgn_b) = _rowwise(
        "b_dout", dout_body, [dog, y, r, k2, v, g], [p["gn_g"], p["gn_b"], p["r_k"], p["seg"]],
        [(d, F32), (d, BF16), (d, F32), (d, F32), (d, F32)], [d, d, d], tm=256)
    dr_rec, dlw_rec, dk2_rec, dv_rec, da_rec, db_rec = _wkv_bwd(*rec_in, states, dy, seq)

    def dprep_body(rv, cv, pv, nv):
        dr_rec_, dlw_rec_, dk2_rec_, dv_rec_, da_rec_, db_rec_, dr_b_, dk2_b_, dv_b_, k_, wl_, za_ = rv
        w0, a0, k_k, k_a, bd = cv
        w_in, e_w, a, norm, n, kk = _rwkv_prep(k_, wl_, za_, w0, a0, k_k, k_a, bd)
        dk2 = dk2_rec_ + dk2_b_
        dkk = db_rec_ * a - da_rec_
        da = db_rec_ * kk + dk2 * k_ * k_a
        dq = jnp.where(norm > 1e-12, dkk - kk * _seg_sum(kk * dkk, bd), dkk) / n
        dk = dk2 * (1.0 + (a - 1.0) * k_a) + dq * k_k
        dza = da * a * (1.0 - a)
        dwl = dlw_rec_ * (-e_w) * jax.nn.sigmoid(-w_in)
        return [dr_rec_ + dr_b_, dk, dv_rec_ + dv_b_, dza, dwl], [dk2 * k_ * (a - 1.0), dq * k_, dza, dwl]
    (dr, dk, dv, dza, dwl), (dk_a, dk_k, da0, dw0) = _rowwise(
        "b_dprep", dprep_body, [dr_rec, dlw_rec, dk2_rec, dv_rec, da_rec, db_rec, dr_b, dk2_b, dv_b, k, wl, za],
        [p["w0"], p["a0"], p["k_k"], p["k_a"], p["seg"]], [(d, BF16)] * 5, [d] * 4, tm=256)

    dw_r = _mm("b_dwr", xr, dr, trans_a=True)
    dw_k = _mm("b_dwk", xk, dk, trans_a=True)
    dw_v = _mm("b_dwv", xv, dv, trans_a=True)
    dxr = _mm("b_dxr", dr, pt["w_r_t"])
    dxk = _mm("b_dxk", dk, pt["w_k_t"])
    dxv = _mm("b_dxv", dv, pt["w_v_t"])
    da2 = _mm("b_da2", la, dza, trans_a=True)
    dla = _mm("b_dla", dza, pt["a2_t"], out_dtype=BF16)
    da1 = _mm("b_da1", xa, dla, trans_a=True)
    dxa = _mm("b_dxa", dla, pt["a1_t"])
    dw2 = _mm("b_dw2", th, dwl, trans_a=True)
    dth = _mm("b_dth", dwl, pt["w2_t"])
    (dzw,), _ = _rowwise("b_dtanh", lambda rv, cv, pv, nv: ([rv[0] * (1.0 - _f32(rv[1]) * _f32(rv[1]))], []),
                         [dth, th], [], [(th.shape[1], BF16)])
    dw1 = _mm("b_dw1", xw, dzw, trans_a=True)
    dxw = _mm("b_dxw", dzw, pt["w1_t"])
    dg2 = _mm("b_dg2", sg, dgate, trans_a=True)
    dsg = _mm("b_dsg", dgate, pt["g2_t"])
    (dzg,), _ = _rowwise("b_dsig", lambda rv, cv, pv, nv: ([rv[0] * _f32(rv[1]) * (1.0 - _f32(rv[1]))], []),
                         [dsg, sg], [], [(sg.shape[1], BF16)])
    dg1 = _mm("b_dg1", xg, dzg, trans_a=True)
    dxg = _mm("b_dxg", dzg, pt["g1_t"])

    def dmix_body(rv, cv, pv, nv):
        hn_ = rv[0]
        dxs = rv[1:]
        mu = cv[0]
        xx = _shift_down(hn_, pv[0], 1) - hn_
        dsum = dxs[0]
        dxx = dxs[0] * mu[0:1]
        dxx_next = nv[0] * mu[0:1]
        for c in range(1, 6):
            dsum = dsum + dxs[c]
            dxx = dxx + dxs[c] * mu[c:c + 1]
            dxx_next = dxx_next + nv[c] * mu[c:c + 1]
        return [dsum - dxx + _shift_up(dxx, dxx_next, 1)], [dxs[c] * xx for c in range(6)]
    (dhn,), dmu = _rowwise("b_dmix", dmix_body, [hn, dxr, dxw, dxk, dxv, dxa, dxg], [p["mu"]], [(d, F32)], [d] * 6,
                           prev=[0], nxt=[1, 2, 3, 4, 5, 6], seq=seq, tm=256)
    dx, dg_pre = _prenorm_bwd("b_dnorm", dres, dhn, x, p["g_pre"])
    grads = dict(g_pre=dg_pre, g_post=dg_post, mu=jnp.stack([m.sum(0) for m in dmu]), w_r=dw_r, w_k=dw_k, w_v=dw_v,
                 w0=dw0.sum(0), w1=dw1, w2=dw2, a0=da0.sum(0), a1=da1, a2=da2, g1=dg1, g2=dg2, k_k=dk_k.sum(0),
                 k_a=dk_a.sum(0), r_k=dr_k.sum(0), gn_g=dgn_g.sum(0), gn_b=dgn_b.sum(0), w_o=dw_o)
    return dx, grads


_WEIGHTS = ['ln_gains', 'mem_norm', 'a_conv_w', 'a_conv_b', 'a_w_in', 'a_b_in', 'a_gate_w', 'a_gate_b', 'a_lambda', 'a_w_out',
            'a_b_out', 'b_mu', 'b_w_rkv', 'b_w0', 'b_w1', 'b_w2', 'b_a0', 'b_a1', 'b_a2', 'b_g1', 'b_g2', 'b_k_k', 'b_k_a',
            'b_r_k', 'b_gn_g', 'b_gn_b', 'b_w_o', 'c_w_q', 'c_w_kv', 'c_w_o', 'm_w_up', 'm_w_down']
_SHARD_AXIS = dict(ln_gains=2, mem_norm=None, a_conv_w=2, a_conv_b=None, a_w_in=2, a_b_in=None, a_gate_w=3, a_gate_b=3,
                   a_lambda=None, a_w_out=1, a_b_out=None, b_mu=2, b_w_rkv=2, b_w0=1, b_w1=1, b_w2=2, b_a0=1, b_a1=1, b_a2=2,
                   b_g1=1, b_g2=2, b_k_k=1, b_k_a=1, b_r_k=None, b_gn_g=1, b_gn_b=1, b_w_o=1, c_w_q=1, c_w_kv=2, c_w_o=1,
                   m_w_up=2, m_w_down=1)
_MATRICES = ['a_w_in', 'a_gate_w', 'a_w_out', 'b_w_rkv', 'b_w1', 'b_w2', 'b_a1', 'b_a2', 'b_g1', 'b_g2', 'b_w_o', 'c_w_q',
             'c_w_kv', 'c_w_o', 'm_w_up', 'm_w_down']
_SHARDED = [n for n in _WEIGHTS if _SHARD_AXIS[n] is not None]
_VECTORS = [n for n in _SHARDED if n not in _MATRICES]
_REPLICATED = [n for n in _WEIGHTS if _SHARD_AXIS[n] is None]
N_XY = 4
N_DEV = 8
PACK_W = 1024
PACK_ROWS = 256


def _pack(arrs, dtype, row_mult=PACK_ROWS):
    parts = []
    rows = 0
    for a in arrs:
        n = a.size
        r = -(-n // PACK_W)
        parts.append(jnp.pad(a.reshape(-1).astype(dtype), (0, r * PACK_W - n)))
        rows += r
    pad_rows = -(-rows // row_mult) * row_mult - rows
    if pad_rows:
        parts.append(jnp.zeros((pad_rows * PACK_W,), dtype))
    return jnp.concatenate(parts).reshape(-1, PACK_W)


def _unpack(flat, shapes):
    out = []
    row = 0
    for shp in shapes:
        n = 1
        for s in shp:
            n *= s
        r = -(-n // PACK_W)
        out.append(flat[row:row + r].reshape(-1)[:n].reshape(shp))
        row += r
    return out


_ANY = pl.BlockSpec(memory_space=pl.ANY)


def _xy_peers():
    x, y = lax.axis_index("x"), lax.axis_index("y")
    return [(1 - x, y), (x, 1 - y), (1 - x, 1 - y)]


def _all_gather_xy(wm, wv):
    half = wm.shape[0] // 2

    def body(wm_ref, wv_ref, gm_ref, gv_ref, send_sems, recv_sems, local_sems):
        x, y, c = lax.axis_index("x"), lax.axis_index("y"), lax.axis_index("c")
        me = 2 * x + y
        mine = pl.ds(pl.multiple_of(c * half, SUBLANES), half)
        other = pl.ds(pl.multiple_of((1 - c) * half, SUBLANES), half)
        local = [pltpu.make_async_copy(wm_ref, gm_ref.at[me], local_sems.at[0]),
                 pltpu.make_async_copy(wv_ref, gv_ref.at[me], local_sems.at[1])]
        for cp in local:
            cp.start()
        sends, lands, passes, from_sibling = [], [], [], []
        for j, (px, py) in enumerate(_xy_peers()):
            peer = 2 * px + py
            ici = functools.partial(pltpu.make_async_remote_copy, device_id=(px, py, c), device_id_type=MESH)
            sends.append(ici(src_ref=wm_ref.at[mine], dst_ref=gm_ref.at[me, mine], send_sem=send_sems.at[j], recv_sem=recv_sems.at[j]))
            lands.append(ici(src_ref=wm_ref.at[mine], dst_ref=gm_ref.at[peer, mine], send_sem=send_sems.at[j], recv_sem=recv_sems.at[j]))
            sends.append(ici(src_ref=wv_ref, dst_ref=gv_ref.at[me], send_sem=send_sems.at[3 + j], recv_sem=recv_sems.at[3 + j]))
            lands.append(ici(src_ref=wv_ref, dst_ref=gv_ref.at[peer], send_sem=send_sems.at[3 + j], recv_sem=recv_sems.at[3 + j]))
            d2d = functools.partial(pltpu.make_async_remote_copy, send_sem=send_sems.at[6 + j], recv_sem=recv_sems.at[6 + j],
                                    device_id=(x, y, 1 - c), device_id_type=MESH)
            passes.append(d2d(src_ref=gm_ref.at[peer, mine], dst_ref=gm_ref.at[peer, mine]))
            from_sibling.append(d2d(src_ref=gm_ref.at[peer, other], dst_ref=gm_ref.at[peer, other]))
        for cp in sends:
            cp.start()
        for j in range(N_XY - 1):
            lands[2 * j].wait_recv()
            passes[j].start()
        for j in range(N_XY - 1):
            lands[2 * j + 1].wait_recv()
        for cp in from_sibling:
            cp.wait_recv()
        for cp in sends + passes:
            cp.wait_send()
        for cp in local:
            cp.wait()

    return pl.pallas_call(
        body, name="all_gather_weights",
        in_specs=[_ANY, _ANY], out_specs=[_ANY, _ANY],
        out_shape=[jax.ShapeDtypeStruct((N_XY,) + wm.shape, wm.dtype), jax.ShapeDtypeStruct((N_XY,) + wv.shape, wv.dtype)],
        scratch_shapes=[pltpu.SemaphoreType.DMA((9,)), pltpu.SemaphoreType.DMA((9,)), pltpu.SemaphoreType.DMA((2,))],
    )(wm, wv)


def _exchange_xy(gsend):
    def body(gs_ref, recv_ref, send_sems, recv_sems):
        c = lax.axis_index("c")
        sends = []
        for j, (px, py) in enumerate(_xy_peers()):
            sends.append(pltpu.make_async_remote_copy(
                src_ref=gs_ref.at[2 * px + py], dst_ref=recv_ref.at[j], send_sem=send_sems.at[j], recv_sem=recv_sems.at[j],
                device_id=(px, py, c), device_id_type=MESH))
        for cp in sends:
            cp.start()
        for cp in sends:
            cp.wait_recv()
        for cp in sends:
            cp.wait_send()

    return pl.pallas_call(
        body, name="exchange_grads",
        in_specs=[_ANY], out_specs=_ANY,
        out_shape=jax.ShapeDtypeStruct((N_XY - 1,) + gsend.shape[1:], gsend.dtype),
        scratch_shapes=[pltpu.SemaphoreType.DMA((3,)), pltpu.SemaphoreType.DMA((3,))],
    )(gsend)


def _swap_with_sibling(part):
    def body(p_ref, got_ref, send_sem, recv_sem):
        x, y, c = lax.axis_index("x"), lax.axis_index("y"), lax.axis_index("c")
        cp = pltpu.make_async_remote_copy(src_ref=p_ref, dst_ref=got_ref, send_sem=send_sem, recv_sem=recv_sem,
                                          device_id=(x, y, 1 - c), device_id_type=MESH)
        cp.start()
        cp.wait_recv()
        cp.wait_send()

    return pl.pallas_call(
        body, name="swap_sibling",
        in_specs=[_ANY], out_specs=_ANY, out_shape=jax.ShapeDtypeStruct(part.shape, part.dtype),
        scratch_shapes=[pltpu.SemaphoreType.DMA, pltpu.SemaphoreType.DMA],
    )(part)


def _all_gather_all(vec):
    def body(v_ref, out_ref, send_sems, recv_sems, local_sem):
        x, y, c = lax.axis_index("x"), lax.axis_index("y"), lax.axis_index("c")
        me = 4 * x + 2 * y + c
        local = pltpu.make_async_copy(v_ref, out_ref.at[me], local_sem)
        local.start()
        sends, recvs = [], []
        for f in range(1, N_DEV):
            fx, fy, fc = (f >> 2) & 1, (f >> 1) & 1, f & 1
            px = (1 - x) if fx else x
            py = (1 - y) if fy else y
            pc = (1 - c) if fc else c
            mk = functools.partial(pltpu.make_async_remote_copy, src_ref=v_ref, send_sem=send_sems.at[f - 1],
                                   recv_sem=recv_sems.at[f - 1], device_id=(px, py, pc), device_id_type=MESH)
            sends.append(mk(dst_ref=out_ref.at[me]))
            recvs.append(mk(dst_ref=out_ref.at[4 * px + 2 * py + pc]))
        for cp in sends:
            cp.start()
        for cp in recvs:
            cp.wait_recv()
        for cp in sends:
            cp.wait_send()
        local.wait()

    return pl.pallas_call(
        body, name="all_gather_replicated",
        in_specs=[_ANY], out_specs=_ANY, out_shape=jax.ShapeDtypeStruct((N_DEV,) + vec.shape, vec.dtype),
        scratch_shapes=[pltpu.SemaphoreType.DMA((N_DEV - 1,)), pltpu.SemaphoreType.DMA((N_DEV - 1,)), pltpu.SemaphoreType.DMA],
    )(vec)


def _adamw(g, w, m, v):
    m2 = ADAM_B1 * m + (1.0 - ADAM_B1) * g
    v2 = ADAM_B2 * v + (1.0 - ADAM_B2) * g * g
    m_hat = m2 / (1.0 - ADAM_B1 ** ADAM_STEP)
    v_hat = v2 / (1.0 - ADAM_B2 ** ADAM_STEP)
    return -ADAM_LR * (m_hat / (jnp.sqrt(v_hat) + ADAM_EPS) + ADAM_WD * w), m2, v2


def _sum_contributions(own, recv):
    def body(rv, cv, pv, nv):
        return [((rv[0] + _f32(rv[1])) + _f32(rv[2])) + _f32(rv[3])], []
    stacked = recv.reshape(-1, PACK_W)
    (part,), _ = _rowwise("sum_grads", body, [own] + [(stacked, PACK_W, 0, j) for j in range(N_XY - 1)], [], [(PACK_W, F32)])
    return part


def _adamw_sharded(part, sib, w, m, v):
    def body(rv, cv, pv, nv):
        g = rv[0] + rv[1]
        return [g, *_adamw(g, rv[2], rv[3], rv[4])], []
    outs, _ = _rowwise("adamw_sharded", body, [part, sib, w, m, v], [], [(PACK_W, F32)] * 4, tm=256)
    return outs


def _adamw_replicated(parts, w, m, v):
    def body(rv, cv, pv, nv):
        g = rv[0]
        for i in range(1, N_DEV):
            g = g + rv[i]
        return [g, *_adamw(g, rv[N_DEV], rv[N_DEV + 1], rv[N_DEV + 2])], []
    outs, _ = _rowwise("adamw_replicated", body, [parts[i] for i in range(N_DEV)] + [w, m, v], [], [(PACK_W, F32)] * 4)
    return outs


def _row(v):
    return v.reshape(1, -1).astype(F32)


def _gate_dense(gate_w):
    _, nh, blk, _ = gate_w.shape
    d = nh * blk
    dense = jnp.zeros((d, 2 * d), gate_w.dtype)
    for g in range(2):
        for h in range(nh):
            dense = lax.dynamic_update_slice(dense, gate_w[g, h], (h * blk, g * d + h * blk))
    return dense


def _gate_blocks(dense, nh):
    d = dense.shape[0]
    blk = d // nh
    return jnp.stack([jnp.stack([dense[h * blk:(h + 1) * blk, g * d + h * blk:g * d + (h + 1) * blk] for h in range(nh)])
                      for g in range(2)])


def _local_step(x3, mem3, target3, fw):
    nseq, seq, d = x3.shape
    mem_len = mem3.shape[1]
    t = nseq * seq
    x0 = x3.reshape(t, d)
    mem2 = mem3.reshape(nseq * mem_len, d)
    target = target3.reshape(t, d)
    ln = fw["ln_gains"]
    gains = [[_row(ln[i, j]) for j in range(6)] for i in range(2)]
    nh = d // RWKV_HEAD
    seg = (jnp.arange(d)[:, None] // RWKV_HEAD == jnp.arange(LANES)[None, :]).astype(BF16)

    w_gate = _gate_dense(fw["a_gate_w"][0])
    pa = dict(g_pre=gains[0][0], g_post=gains[0][1], w_in=fw["a_w_in"][0], b_in_y=_row(fw["a_b_in"][0, :d]),
              b_in_u=_row(fw["a_b_in"][0, d:]), conv_w=fw["a_conv_w"][0].astype(F32), conv_b=_row(fw["a_conv_b"][0]),
              w_gate=w_gate, gate_b0=_row(fw["a_gate_b"][0, 0]), gate_b1=_row(fw["a_gate_b"][0, 1]), lam=_row(fw["a_lambda"][0]),
              w_out=fw["a_w_out"][0], b_out=_row(fw["a_b_out"][0]))
    pta = dict(w_in_t=_t(pa["w_in"]), w_gate_t=_t(w_gate), w_out_t=_t(pa["w_out"]))
    pb = dict(g_pre=gains[1][0], g_post=gains[1][1], mu=fw["b_mu"][0].astype(F32), w_r=fw["b_w_rkv"][0, 0],
              w_k=fw["b_w_rkv"][0, 1], w_v=fw["b_w_rkv"][0, 2], w0=_row(fw["b_w0"][0]), w1=fw["b_w1"][0], w2=fw["b_w2"][0],
              a0=_row(fw["b_a0"][0]), a1=fw["b_a1"][0], a2=fw["b_a2"][0], g1=fw["b_g1"][0], g2=fw["b_g2"][0],
              k_k=_row(fw["b_k_k"][0]), k_a=_row(fw["b_k_a"][0]), r_k=_row(fw["b_r_k"][0]), gn_g=_row(fw["b_gn_g"][0]),
              gn_b=_row(fw["b_gn_b"][0]), w_o=fw["b_w_o"][0], seg=seg)
    ptb = {k + "_t": _t(pb[k]) for k in ("w_r", "w_k", "w_v", "w_o", "w1", "w2", "a1", "a2", "g1", "g2")}
    mem_g = _row(fw["mem_norm"])

    mem_n = _norm_fwd("mem_norm", mem2, mem_g)
    x1, sv_a = _rglru_fwd(x0, pa, seq)
    x2, sv_c0 = _xattn_fwd("c0", x1, mem_n, gains[0][2], gains[0][3], fw["c_w_q"][0], fw["c_w_kv"][0], fw["c_w_o"][0], seq, mem_len)
    x3_, sv_m0 = _mlp_fwd("m0", x2, gains[0][4], gains[0][5], fw["m_w_up"][0], fw["m_w_down"][0])
    x4, sv_b = _rwkv_fwd(x3_, pb, seq)
    x5, sv_c1 = _xattn_fwd("c1", x4, mem_n, gains[1][2], gains[1][3], fw["c_w_q"][1], fw["c_w_kv"][1], fw["c_w_o"][1], seq, mem_len)
    x6, sv_m1 = _mlp_fwd("m1", x5, gains[1][4], gains[1][5], fw["m_w_up"][1], fw["m_w_down"][1])

    def loss_body(rv, cv, pv, nv):
        err = rv[0] - rv[1]
        return [err * (1.0 / d)], [err * err]
    (dx,), (sq,) = _rowwise("loss", loss_body, [x6, target], [], [(d, F32)], [d])
    loss_part = 0.5 / d * jnp.sum(sq)

    dx, g_m1 = _mlp_bwd("m1", sv_m1, dx, gains[1][4], gains[1][5], _t(fw["m_w_up"][1]), _t(fw["m_w_down"][1]))
    dx, dmem1, g_c1 = _xattn_bwd("c1", sv_c1, dx, mem_n, gains[1][2], gains[1][3], _t(fw["c_w_q"][1]), _t(fw["c_w_kv"][1]),
                                 _t(fw["c_w_o"][1]), seq, mem_len)
    dx, g_b = _rwkv_bwd(sv_b, dx, pb, ptb, seq)
    dx, g_m0 = _mlp_bwd("m0", sv_m0, dx, gains[0][4], gains[0][5], _t(fw["m_w_up"][0]), _t(fw["m_w_down"][0]))
    dx, dmem0, g_c0 = _xattn_bwd("c0", sv_c0, dx, mem_n, gains[0][2], gains[0][3], _t(fw["c_w_q"][0]), _t(fw["c_w_kv"][0]),
                                 _t(fw["c_w_o"][0]), seq, mem_len)
    dx, g_a = _rglru_bwd(sv_a, dx, pa, pta, seq)

    def dmem_body(rv, cv, pv, nv):
        _, dg = _rms_bwd(rv[1] + rv[2], rv[0], cv[0])
        return [], [dg]
    _, (dmem_g,) = _rowwise("mem_norm_grad", dmem_body, [mem2, dmem0, dmem1], [mem_g], [], [d])

    lru_heads = fw["a_gate_w"].shape[2]
    blk = d // lru_heads
    grads = dict(
        ln_gains=jnp.stack([jnp.stack([g_a["g_pre"], g_a["g_post"], g_c0["g_pre"], g_c0["g_post"], g_m0["g_pre"], g_m0["g_post"]]),
                            jnp.stack([g_b["g_pre"], g_b["g_post"], g_c1["g_pre"], g_c1["g_post"], g_m1["g_pre"], g_m1["g_post"]])]),
        mem_norm=dmem_g.sum(0),
        a_conv_w=g_a["conv_w"][None], a_conv_b=g_a["conv_b"][None], a_w_in=g_a["w_in"][None], a_b_in=g_a["b_in"][None],
        a_gate_w=_gate_blocks(g_a["w_gate"], lru_heads)[None],
        a_gate_b=jnp.stack([g_a["gate_b0"], g_a["gate_b1"]]).reshape(1, 2, lru_heads, blk),
        a_lambda=g_a["lam"][None], a_w_out=g_a["w_out"][None], a_b_out=g_a["b_out"][None],
        b_mu=g_b["mu"][None], b_w_rkv=jnp.stack([g_b["w_r"], g_b["w_k"], g_b["w_v"]])[None], b_w0=g_b["w0"][None],
        b_w1=g_b["w1"][None], b_w2=g_b["w2"][None], b_a0=g_b["a0"][None], b_a1=g_b["a1"][None], b_a2=g_b["a2"][None],
        b_g1=g_b["g1"][None], b_g2=g_b["g2"][None], b_k_k=g_b["k_k"][None], b_k_a=g_b["k_a"][None],
        b_r_k=g_b["r_k"].reshape(1, nh, RWKV_HEAD), b_gn_g=g_b["gn_g"][None], b_gn_b=g_b["gn_b"][None], b_w_o=g_b["w_o"][None],
        c_w_q=jnp.stack([g_c0["w_q"], g_c1["w_q"]]), c_w_kv=jnp.stack([g_c0["w_kv"], g_c1["w_kv"]]),
        c_w_o=jnp.stack([g_c0["w_o"], g_c1["w_o"]]),
        m_w_up=jnp.stack([g_m0["w_up"], g_m1["w_up"]]), m_w_down=jnp.stack([g_m0["w_down"], g_m1["w_down"]]),
    )
    return loss_part, dx.reshape(nseq, seq, d), grads


def kernel(x, mem, ln_gains, mem_norm, a_conv_w, a_conv_b, a_w_in, a_b_in, a_gate_w, a_gate_b, a_lambda, a_w_out, a_b_out, b_mu, b_w_rkv, b_w0, b_w1, b_w2, b_a0, b_a1, b_a2, b_g1, b_g2, b_k_k, b_k_a, b_r_k, b_gn_g, b_gn_b, b_w_o, c_w_q, c_w_kv, c_w_o, m_w_up, m_w_down, loss_target, m_ln_gains, m_mem_norm, m_a_conv_w, m_a_conv_b, m_a_w_in, m_a_b_in, m_a_gate_w, m_a_gate_b, m_a_lambda, m_a_w_out, m_a_b_out, m_b_mu, m_b_w_rkv, m_b_w0, m_b_w1, m_b_w2, m_b_a0, m_b_a1, m_b_a2, m_b_g1, m_b_g2, m_b_k_k, m_b_k_a, m_b_r_k, m_b_gn_g, m_b_gn_b, m_b_w_o, m_c_w_q, m_c_w_kv, m_c_w_o, m_m_w_up, m_m_w_down, v_ln_gains, v_mem_norm, v_a_conv_w, v_a_conv_b, v_a_w_in, v_a_b_in, v_a_gate_w, v_a_gate_b, v_a_lambda, v_a_w_out, v_a_b_out, v_b_mu, v_b_w_rkv, v_b_w0, v_b_w1, v_b_w2, v_b_a0, v_b_a1, v_b_a2, v_b_g1, v_b_g2, v_b_k_k, v_b_k_a, v_b_r_k, v_b_gn_g, v_b_gn_b, v_b_w_o, v_c_w_q, v_c_w_kv, v_c_w_o, v_m_w_up, v_m_w_down):
    given = dict(locals())
    w = {n: given[n] for n in _WEIGHTS}
    mom1 = {n: given["m_" + n] for n in _WEIGHTS}
    mom2 = {n: given["v_" + n] for n in _WEIGHTS}

    gm, gv = _all_gather_xy(_pack([w[n] for n in _MATRICES], BF16), _pack([w[n] for n in _VECTORS], F32, SUBLANES))
    fw = {n: w[n] for n in _REPLICATED}
    for names, buf in ((_MATRICES, gm), (_VECTORS, gv)):
        shards = [_unpack(buf[s], [w[n].shape for n in names]) for s in range(N_XY)]
        for i, n in enumerate(names):
            fw[n] = jnp.concatenate([shards[s][i] for s in range(N_XY)], axis=_SHARD_AXIS[n])

    loss_part, grad_x, grads = _local_step(x, mem, loss_target, fw)

    def shard_of(n, s):
        ax = _SHARD_AXIS[n]
        size = w[n].shape[ax]
        return lax.slice_in_dim(grads[n], s * size, (s + 1) * size, axis=ax)
    gfull = jnp.stack([_pack([shard_of(n, s) for n in _SHARDED], F32) for s in range(N_XY)])
    me = 2 * lax.axis_index("x") + lax.axis_index("y")
    recv = _exchange_xy(gfull.astype(BF16))
    part = _sum_contributions(lax.dynamic_index_in_dim(gfull, me, 0, keepdims=False), recv)
    sib = _swap_with_sibling(part)
    flat = [_pack([src[n] for n in _SHARDED], F32) for src in (w, mom1, mom2)]
    sharded_out = [_unpack(o, [w[n].shape for n in _SHARDED]) for o in _adamw_sharded(part, sib, *flat)]

    small = _pack([grads[n] for n in _REPLICATED] + [loss_part.reshape(1)], F32, SUBLANES)
    parts = _all_gather_all(small)
    zero = jnp.zeros((1,), F32)
    flat = [_pack([src[n] for n in _REPLICATED] + [zero], F32, SUBLANES) for src in (w, mom1, mom2)]
    repl_out = [_unpack(o, [w[n].shape for n in _REPLICATED] + [(1,)]) for o in _adamw_replicated(parts, *flat)]
    loss = repl_out[0][-1][0]

    result = [loss, grad_x]
    for kind in range(4):
        by_name = dict(zip(_SHARDED, sharded_out[kind])) | dict(zip(_REPLICATED, repl_out[kind][:-1]))
        result += [by_name[n] for n in _WEIGHTS]
    return tuple(result)
```

```python
import functools

import jax
import jax.numpy as jnp
from jax import lax
from jax.experimental import pallas as pl
from jax.experimental.pallas import tpu as pltpu

F32 = jnp.float32
BF16 = jnp.bfloat16
MESH = pl.DeviceIdType.MESH

LANES = 128
SUBLANES = 8
VMEM_LIMIT_BYTES = 48 * 1024 * 1024

RMS_EPS = 1e-6
LRU_C = 8.0
LRU_HEADS = 4
CONV_WIDTH = 4
RWKV_HEAD = 64
RWKV_GN_EPS = 64e-5
MEM_HEADS = 4
ADAM_LR = 0.001
ADAM_B1 = 0.9
ADAM_B2 = 0.999
ADAM_EPS = 1e-08
ADAM_WD = 0.01
ADAM_STEP = 10
WKV_CHUNK = 64

_PARAMS = functools.partial(pltpu.CompilerParams, vmem_limit_bytes=VMEM_LIMIT_BYTES)


def _tile(n, want):
    if n <= want:
        return n
    t = want
    while t >= SUBLANES:
        if n % t == 0 and t % SUBLANES == 0:
            return t
        t -= SUBLANES
    return n


def _fold8(v):
    tm, d = v.shape
    if tm == SUBLANES:
        return v
    return jnp.sum(v.reshape(tm // SUBLANES, SUBLANES, d), axis=0)


def _rowwise(name, body, rows, consts=(), out_rows=(), out_accs=(), prev=(), nxt=(), tm=512, seq=None):
    rows = [r if isinstance(r, tuple) else (r, r.shape[1], 0) for r in rows]
    rows = [r if len(r) == 4 else r + (0,) for r in rows]
    t = rows[0][0].shape[0]
    tm = _tile(t, tm)
    if seq is not None:
        tm = _tile(seq, tm)
    nblk = t // tm
    nrow, ncst, nprev, nnxt = len(rows), len(consts), len(prev), len(nxt)
    nor, noa = len(out_rows), len(out_accs)
    hb = tm // SUBLANES

    def kern(*refs):
        i = pl.program_id(0)
        rv = [r[...] for r in refs[:nrow]]
        cv = [c[...] for c in refs[nrow:nrow + ncst]]
        o = nrow + ncst
        pv = []
        for j in range(nprev):
            at_start = (i * tm) % seq == 0
            h = refs[o + j][...]
            pv.append(jnp.where(at_start, jnp.zeros_like(h), h))
        o += nprev
        nv = []
        for j in range(nnxt):
            at_end = ((i + 1) * tm) % seq == 0
            h = refs[o + j][...]
            nv.append(jnp.where(at_end, jnp.zeros_like(h), h))
        o += nnxt
        outs, accs = body(rv, cv, pv, nv)
        for j in range(nor):
            refs[o + j][...] = outs[j].astype(refs[o + j].dtype)
        o += nor
        if noa:
            @pl.when(i == 0)
            def _():
                for j in range(noa):
                    refs[o + j][...] = jnp.zeros_like(refs[o + j])
            for j in range(noa):
                refs[o + j][...] += _fold8(accs[j].astype(F32))

    in_specs = [pl.BlockSpec((tm, w), functools.partial(lambda i, c, o: (i + o, c), c=cb, o=part * nblk))
                for (_, w, cb, part) in rows]
    in_specs += [pl.BlockSpec(c.shape, lambda i: (0, 0)) for c in consts]
    in_specs += [pl.BlockSpec((SUBLANES, rows[j][1]),
                              functools.partial(lambda i, c: (jnp.maximum(i * hb - 1, 0), c), c=rows[j][2])) for j in prev]
    in_specs += [pl.BlockSpec((SUBLANES, rows[j][1]),
                              functools.partial(lambda i, c: (jnp.minimum((i + 1) * hb, t // SUBLANES - 1), c), c=rows[j][2]))
                 for j in nxt]
    out_shape = [jax.ShapeDtypeStruct((t, w), dt) for (w, dt) in out_rows]
    out_shape += [jax.ShapeDtypeStruct((SUBLANES, w), F32) for w in out_accs]
    out_specs = [pl.BlockSpec((tm, w), lambda i: (i, 0)) for (w, _) in out_rows]
    out_specs += [pl.BlockSpec((SUBLANES, w), lambda i: (0, 0)) for w in out_accs]
    args = [r[0] for r in rows] + list(consts) + [rows[j][0] for j in prev] + [rows[j][0] for j in nxt]
    res = pl.pallas_call(
        kern, name=name, grid=(nblk,), in_specs=in_specs, out_specs=out_specs, out_shape=out_shape,
        compiler_params=_PARAMS(dimension_semantics=("arbitrary",)),
    )(*args)
    return list(res[:nor]), list(res[nor:])


def _shift_down(x, halo, k):
    rolled = pltpu.roll(x, k, 0)
    row = lax.broadcasted_iota(jnp.int32, (SUBLANES, x.shape[1]), 0)
    first = jnp.where(row < k, pltpu.roll(halo, k, 0), rolled[:SUBLANES])
    if x.shape[0] == SUBLANES:
        return first
    return jnp.concatenate([first, rolled[SUBLANES:]], axis=0)


def _shift_up(x, halo, k):
    n = x.shape[0]
    rolled = pltpu.roll(x, n - k, 0)
    row = lax.broadcasted_iota(jnp.int32, (SUBLANES, x.shape[1]), 0)
    last = jnp.where(row >= SUBLANES - k, pltpu.roll(halo, SUBLANES - k, 0), rolled[n - SUBLANES:])
    if n == SUBLANES:
        return last
    return jnp.concatenate([rolled[:n - SUBLANES], last], axis=0)


class _Transposed:
    def __init__(self, w):
        self.w = w


def _t(w):
    return _Transposed(w)


def _mm(name, a, b, out_dtype=F32, trans_a=False, tm=1024, tn=1024, tk=1024, epilogue=None, extra=None):
    trans_b = isinstance(b, _Transposed)
    assert not (trans_a and trans_b)
    if trans_b:
        b = b.w
    if trans_a:
        kdim, m = a.shape
    else:
        m, kdim = a.shape
    n = b.shape[0] if trans_b else b.shape[1]
    assert b.shape[1 if trans_b else 0] == kdim, (name, a.shape, b.shape)
    tm, tn, tk = _tile(m, tm), _tile(n, tn), _tile(kdim, tk)
    nk = kdim // tk
    dims = (((0,), (0,)), ((), ())) if trans_a else (((1,), (1 if trans_b else 0,)), ((), ()))

    n_in = 2 if extra is None else 3

    def kern(*refs):
        a_ref, b_ref, o_ref, acc = refs[0], refs[1], refs[n_in], refs[n_in + 1:]

        def store(res):
            if epilogue is not None:
                res = epilogue(res) if extra is None else epilogue(res, refs[2][...])
            o_ref[...] = res.astype(o_ref.dtype)

        part = lax.dot_general(a_ref[...].astype(BF16), b_ref[...].astype(BF16), dims, preferred_element_type=F32)
        if nk == 1:
            store(part)
        else:
            k = pl.program_id(2)

            @pl.when(k == 0)
            def _():
                acc[0][...] = part

            @pl.when(k > 0)
            def _():
                acc[0][...] += part

            @pl.when(k == nk - 1)
            def _():
                store(acc[0][...])

    a_spec = pl.BlockSpec((tk, tm), lambda i, j, k: (k, i)) if trans_a else pl.BlockSpec((tm, tk), lambda i, j, k: (i, k))
    b_spec = pl.BlockSpec((tn, tk), lambda i, j, k: (j, k)) if trans_b else pl.BlockSpec((tk, tn), lambda i, j, k: (k, j))
    out_spec = pl.BlockSpec((tm, tn), lambda i, j, k: (i, j))
    return pl.pallas_call(
        kern, name=name, grid=(m // tm, n // tn, nk),
        in_specs=[a_spec, b_spec] + ([] if extra is None else [out_spec]),
        out_specs=out_spec,
        out_shape=jax.ShapeDtypeStruct((m, n), out_dtype),
        scratch_shapes=[] if nk == 1 else [pltpu.VMEM((tm, tn), F32)],
        compiler_params=_PARAMS(dimension_semantics=("parallel", "parallel", "arbitrary")),
    )(*((a, b) if extra is None else (a, b, extra)))


def _scan(name, a, b, seq, reverse=False, tm=256):
    t, d = a.shape
    tm = _tile(seq, tm)
    nblk = t // tm
    ntile = tm // SUBLANES

    def kern(a_ref, b_ref, h_ref, carry_h, carry_a):
        i = pl.program_id(0)
        blk = (nblk - 1 - i) if reverse else i
        edge = (((blk + 1) * tm) % seq == 0) if reverse else ((blk * tm) % seq == 0)

        @pl.when(edge)
        def _():
            carry_h[...] = jnp.zeros_like(carry_h)
            carry_a[...] = jnp.zeros_like(carry_a)

        def tile_step(j, c):
            jj = (ntile - 1 - j) if reverse else j
            rows = pl.ds(pl.multiple_of(jj * SUBLANES, SUBLANES), SUBLANES)
            a8 = a_ref[rows, :]
            b8 = b_ref[rows, :]
            h, an = c
            out = [None] * SUBLANES
            order = range(SUBLANES - 1, -1, -1) if reverse else range(SUBLANES)
            for r in order:
                if reverse:
                    h = b8[r:r + 1, :] + an * h
                    an = a8[r:r + 1, :]
                else:
                    h = a8[r:r + 1, :] * h + b8[r:r + 1, :]
                out[r] = h
            h_ref[rows, :] = jnp.concatenate(out, axis=0)
            return (h, an)

        h, an = lax.fori_loop(0, ntile, tile_step, (carry_h[...], carry_a[...]))
        carry_h[...] = h
        carry_a[...] = an

    idx = (lambda i: (nblk - 1 - i, 0)) if reverse else (lambda i: (i, 0))
    return pl.pallas_call(
        kern, name=name, grid=(nblk,),
        in_specs=[pl.BlockSpec((tm, d), idx), pl.BlockSpec((tm, d), idx)],
        out_specs=pl.BlockSpec((tm, d), idx),
        out_shape=jax.ShapeDtypeStruct((t, d), F32),
        scratch_shapes=[pltpu.VMEM((1, d), F32), pltpu.VMEM((1, d), F32)],
        compiler_params=_PARAMS(dimension_semantics=("arbitrary",)),
    )(a, b)


_NN = (((1,), (0,)), ((), ()))
_NT = (((1,), (1,)), ((), ()))
_TN = (((0,), (0,)), ((), ()))


def _dot1(a, b, dims):
    return lax.dot_general(a.astype(BF16), b.astype(BF16), dims, preferred_element_type=F32)


def _dot3(a, b, dims):
    a_hi, b_hi = a.astype(BF16), b.astype(BF16)
    a_lo, b_lo = (a - a_hi.astype(F32)).astype(BF16), (b - b_hi.astype(F32)).astype(BF16)
    dg = lambda p, q: lax.dot_general(p, q, dims, preferred_element_type=F32)
    return dg(a_hi, b_hi) + (dg(a_hi, b_lo) + dg(a_lo, b_hi))


def _make_bmm(dot):
    def make(dims, da_rule, db_rule):
        @jax.custom_vjp
        def f(a, b):
            return dot(a, b, dims)

        def fwd(a, b):
            return dot(a, b, dims), (a, b)

        def bwd(res, g):
            a, b = res
            return da_rule(a, b, g), db_rule(a, b, g)

        f.defvjp(fwd, bwd)
        return f

    return dict(nn=make(_NN, lambda a, b, g: dot(g, b, _NT), lambda a, b, g: dot(a, g, _TN)),
                nt=make(_NT, lambda a, b, g: dot(g, b, _NN), lambda a, b, g: dot(g, a, _TN)),
                tn=make(_TN, lambda a, b, g: dot(b, g, _NT), lambda a, b, g: dot(a, g, _NN)))


_BMM = {1: _make_bmm(_dot1), 3: _make_bmm(_dot3)}
_WKV_PASSES = dict(pair=1, read=1, inv=3, apply=1, write=1)


def _running_sum(x, reverse):
    c = x.shape[0]
    row = lax.broadcasted_iota(jnp.int32, x.shape, 0)
    k = 1
    while k < c:
        if reverse:
            x = x + jnp.where(row < c - k, pltpu.roll(x, c - k, 0), 0.0)
        else:
            x = x + jnp.where(row >= k, pltpu.roll(x, k, 0), 0.0)
        k *= 2
    return x


@jax.custom_vjp
def _cumsum_rows(x):
    return _running_sum(x, False)


_cumsum_rows.defvjp(lambda x: (_running_sum(x, False), None), lambda _, g: (_running_sum(g, True),))


@jax.custom_vjp
def _unit_lower_inverse(nl):
    c = nl[0].shape[0]
    mm = _BMM[_WKV_PASSES["inv"]]["nn"]
    eye = jnp.where(lax.broadcasted_iota(jnp.int32, (c, c), 0) == lax.broadcasted_iota(jnp.int32, (c, c), 1), 1.0, 0.0)
    inv = [eye + z for z in nl]
    p = nl
    for _ in range(c.bit_length() - 2):
        p = [mm(z, z) for z in p]
        inv = [i_ + mm(p_, i_) for i_, p_ in zip(inv, p)]
    return inv


def _unit_lower_inverse_fwd(nl):
    inv = _unit_lower_inverse(nl)
    return inv, inv


def _unit_lower_inverse_bwd(inv, g):
    mm = _BMM[_WKV_PASSES["inv"]]
    left = [mm["tn"](x, g_) for x, g_ in zip(inv, g)]
    return ([mm["nt"](l_, x) for l_, x in zip(left, inv)],)


_unit_lower_inverse.defvjp(_unit_lower_inverse_fwd, _unit_lower_inverse_bwd)


@jax.custom_vjp
def _kept_inverse(nl, inv):
    return inv


_kept_inverse.defvjp(lambda nl, inv: (inv, inv),
                     lambda inv, g: (_unit_lower_inverse_bwd(inv, g)[0], [jnp.zeros_like(x) for x in inv]))


def _wkv_chunk(r, lw, k, v, a, b, s0, kept_inv=None):
    c = r[0].shape[0]
    ti = lax.broadcasted_iota(jnp.int32, (c, 2 * c), 0)
    tj = lax.broadcasted_iota(jnp.int32, (c, 2 * c), 1)
    right = tj >= c
    tau = jnp.where(right, tj - c, tj)
    strict_left = jnp.logical_and(jnp.logical_not(right), tau < ti)[:, :c]
    strict_right = jnp.logical_and(right, tau < ti)
    incl = tau <= ti
    last = lax.broadcasted_iota(jnp.int32, r[0].shape, 0) == c - 1
    each = lambda f, *ls: [f(*z) for z in zip(*ls)]
    rows2 = lambda x, y: jnp.concatenate([x, y], axis=0)
    pair, read, inv_, apply_, write = (_BMM[_WKV_PASSES[role]] for role in ("pair", "read", "inv", "apply", "write"))
    cum = each(_cumsum_rows, lw)
    w_incl = each(jnp.exp, cum)
    w_inv = each(lambda z: jnp.exp(-z), cum)
    at = each(lambda a_, c_, l_: a_ * jnp.exp(c_ - l_), a, cum, lw)
    ar = each(rows2, at, each(jnp.multiply, r, w_incl))
    bk = each(rows2, each(jnp.multiply, b, w_inv), each(jnp.multiply, k, w_inv))
    pp = each(pair["nt"], ar, bk)
    sr = each(read["nt"], ar, s0)
    nl = [jnp.where(strict_left, z[:c, :c], 0.0) for z in pp]
    zero_v = each(lambda v_: rows2(jnp.zeros_like(v_), v_), v)
    rhs = each(lambda s, z, zv: s[:c] + apply_["nn"](jnp.where(strict_right, z[:c], 0.0), zv), sr, pp, zero_v)
    inv = _unit_lower_inverse(nl) if kept_inv is None else _kept_inverse(nl, kept_inv)
    ut = each(inv_["nn"], inv, rhs)
    uv = each(rows2, ut, v)
    y = each(lambda s, z, uv_: s[c:] + apply_["nn"](jnp.where(incl, z[c:], 0.0), uv_), sr, pp, uv)
    w_end = each(lambda z: jnp.exp(jnp.sum(jnp.where(last, z, 0.0), axis=0, keepdims=True)), cum)
    s1 = each(lambda s, uv_, bk_, w_: (s + write["tn"](uv_, bk_)) * w_, s0, uv, bk, w_end)
    return y, s1, inv


def _wkv_fwd(r, lw, k, v, a, b, seq, hb=8):
    t, d = r.shape
    n = RWKV_HEAD
    nh = d // n
    hb = min(hb, nh)
    chunk = min(WKV_CHUNK, seq)
    ncs = seq // chunk

    def kern(r_ref, lw_ref, k_ref, v_ref, a_ref, b_ref, y_ref, st_ref, inv_ref, s_scr):
        @pl.when(pl.program_id(2) == 0)
        def _():
            s_scr[...] = jnp.zeros_like(s_scr)

        heads = lambda ref: [ref[:, h * n:(h + 1) * n] for h in range(hb)]
        s0 = [s_scr[h] for h in range(hb)]
        y, s1, inv = _wkv_chunk(heads(r_ref), heads(lw_ref), heads(k_ref), heads(v_ref), heads(a_ref), heads(b_ref), s0)
        for h in range(hb):
            st_ref[0, h] = s0[h]
            inv_ref[0, h] = inv[h]
            y_ref[:, h * n:(h + 1) * n] = y[h]
            s_scr[h] = s1[h]

    vec = pl.BlockSpec((chunk, hb * n), lambda bb, g, c: (bb * ncs + c, g))
    per_chunk = lambda rows: pl.BlockSpec((1, hb, rows, rows), lambda bb, g, c: (bb * ncs + c, g, 0, 0))
    return pl.pallas_call(
        kern, name="wkv_fwd", grid=(t // seq, nh // hb, ncs), in_specs=[vec] * 6,
        out_specs=[vec, per_chunk(n), per_chunk(chunk)],
        out_shape=[jax.ShapeDtypeStruct((t, d), F32), jax.ShapeDtypeStruct((t // chunk, nh, n, n), F32),
                   jax.ShapeDtypeStruct((t // chunk, nh, chunk, chunk), F32)],
        scratch_shapes=[pltpu.VMEM((hb, n, n), F32)],
        compiler_params=_PARAMS(dimension_semantics=("parallel", "parallel", "arbitrary")),
    )(r, lw, k, v, a, b)


def _wkv_bwd(r, lw, k, v, a, b, st, inv, dy, seq, hb=8):
    t, d = r.shape
    n = RWKV_HEAD
    nh = d // n
    hb = min(hb, nh)
    chunk = min(WKV_CHUNK, seq)
    ncs = seq // chunk

    def kern(r_ref, lw_ref, k_ref, v_ref, a_ref, b_ref, st_ref, inv_ref, dy_ref,
             dr_ref, dlw_ref, dk_ref, dv_ref, da_ref, db_ref, ds_scr):
        @pl.when(pl.program_id(2) == 0)
        def _():
            ds_scr[...] = jnp.zeros_like(ds_scr)

        heads = lambda ref: [ref[:, h * n:(h + 1) * n] for h in range(hb)]
        kept = [inv_ref[0, h] for h in range(hb)]
        _, vjp = jax.vjp(lambda *args: _wkv_chunk(*args, kept_inv=kept)[:2],
                         heads(r_ref), heads(lw_ref), heads(k_ref), heads(v_ref), heads(a_ref), heads(b_ref),
                         [st_ref[0, h] for h in range(hb)])
        grads = vjp((heads(dy_ref), [ds_scr[h] for h in range(hb)]))
        for h in range(hb):
            for ref, g in zip((dr_ref, dlw_ref, dk_ref, dv_ref, da_ref, db_ref), grads[:6]):
                ref[:, h * n:(h + 1) * n] = g[h]
            ds_scr[h] = grads[6][h]

    vec = pl.BlockSpec((chunk, hb * n), lambda bb, g, c: (bb * ncs + ncs - 1 - c, g))
    per_chunk = lambda rows: pl.BlockSpec((1, hb, rows, rows), lambda bb, g, c: (bb * ncs + ncs - 1 - c, g, 0, 0))
    return pl.pallas_call(
        kern, name="wkv_bwd", grid=(t // seq, nh // hb, ncs),
        in_specs=[vec] * 6 + [per_chunk(n), per_chunk(chunk), vec],
        out_specs=[vec] * 6, out_shape=[jax.ShapeDtypeStruct((t, d), F32)] * 6,
        scratch_shapes=[pltpu.VMEM((hb, n, n), F32)],
        compiler_params=_PARAMS(dimension_semantics=("parallel", "parallel", "arbitrary")),
    )(r, lw, k, v, a, b, st, inv, dy)


def _softmax_rows(s):
    e = jnp.exp(s - jnp.max(s, axis=-1, keepdims=True))
    return e / jnp.sum(e, axis=-1, keepdims=True)


def _attn_fwd(q, kv, seq, mem_len, tq=256):
    t, d = q.shape
    dh = d // MEM_HEADS
    scale = dh ** -0.5
    tq = _tile(seq, tq)
    nq = seq // tq

    def kern(q_ref, kv_ref, o_ref):
        for h in range(MEM_HEADS):
            cols = slice(h * dh, (h + 1) * dh)
            vcols = slice(d + h * dh, d + (h + 1) * dh)
            s = lax.dot_general(q_ref[:, cols], kv_ref[:, cols], _NT, preferred_element_type=F32) * scale
            p = _softmax_rows(s)
            o_ref[:, cols] = jnp.dot(p.astype(BF16), kv_ref[:, vcols], preferred_element_type=F32).astype(o_ref.dtype)

    return pl.pallas_call(
        kern, name="attn_fwd", grid=(t // seq, nq),
        in_specs=[pl.BlockSpec((tq, d), lambda b, i: (b * nq + i, 0)), pl.BlockSpec((mem_len, 2 * d), lambda b, i: (b, 0))],
        out_specs=pl.BlockSpec((tq, d), lambda b, i: (b * nq + i, 0)),
        out_shape=jax.ShapeDtypeStruct((t, d), BF16),
        compiler_params=_PARAMS(dimension_semantics=("parallel", "parallel")),
    )(q, kv)


def _attn_bwd(q, kv, do, seq, mem_len, tq=256):
    t, d = q.shape
    dh = d // MEM_HEADS
    scale = dh ** -0.5
    tq = _tile(seq, tq)
    nq = seq // tq

    def kern(q_ref, kv_ref, do_ref, dq_ref, dkv_ref):
        @pl.when(pl.program_id(1) == 0)
        def _():
            dkv_ref[...] = jnp.zeros_like(dkv_ref)

        for h in range(MEM_HEADS):
            cols = slice(h * dh, (h + 1) * dh)
            vcols = slice(d + h * dh, d + (h + 1) * dh)
            qh, kh, vh, doh = q_ref[:, cols], kv_ref[:, cols], kv_ref[:, vcols], do_ref[:, cols]
            p = _softmax_rows(lax.dot_general(qh, kh, _NT, preferred_element_type=F32) * scale)
            dp = lax.dot_general(doh, vh, _NT, preferred_element_type=F32)
            ds = (p * (dp - jnp.sum(p * dp, axis=-1, keepdims=True)) * scale).astype(BF16)
            dq_ref[:, cols] = jnp.dot(ds, kh, preferred_element_type=F32).astype(dq_ref.dtype)
            dkv_ref[:, cols] += lax.dot_general(ds, qh, _TN, preferred_element_type=F32)
            dkv_ref[:, vcols] += lax.dot_general(p.astype(BF16), doh, _TN, preferred_element_type=F32)

    return pl.pallas_call(
        kern, name="attn_bwd", grid=(t // seq, nq),
        in_specs=[pl.BlockSpec((tq, d), lambda b, i: (b * nq + i, 0)), pl.BlockSpec((mem_len, 2 * d), lambda b, i: (b, 0)),
                  pl.BlockSpec((tq, d), lambda b, i: (b * nq + i, 0))],
        out_specs=[pl.BlockSpec((tq, d), lambda b, i: (b * nq + i, 0)), pl.BlockSpec((mem_len, 2 * d), lambda b, i: (b, 0))],
        out_shape=[jax.ShapeDtypeStruct((t, d), BF16), jax.ShapeDtypeStruct(kv.shape, F32)],
        compiler_params=_PARAMS(dimension_semantics=("parallel", "arbitrary")),
    )(q, kv, do)


def _rstd(x):
    return lax.rsqrt(jnp.mean(x * x, axis=-1, keepdims=True) + RMS_EPS)


def _rms(x, g):
    return x * _rstd(x) * g


def _rms_bwd(dy, x, g):
    rstd = _rstd(x)
    xhat = x * rstd
    dxhat = dy * g
    return rstd * (dxhat - xhat * jnp.mean(dxhat * xhat, axis=-1, keepdims=True)), dy * xhat


def _softplus(x):
    return jnp.maximum(x, 0.0) + jnp.log1p(jnp.exp(-jnp.abs(x)))


def _one_minus_exp(x):
    series = -x * (1.0 + x * (0.5 + x * (1.0 / 6.0 + x * (1.0 / 24.0 + x * (1.0 / 120.0)))))
    return jnp.where(x > -0.05, series, 1.0 - jnp.exp(x))


_GELU_C = 0.7978845608028654
_GELU_K = 0.044715


def _gelu(x):
    return 0.5 * x * (1.0 + jnp.tanh(_GELU_C * (x + _GELU_K * x * x * x)))


def _gelu_grad(x):
    th = jnp.tanh(_GELU_C * (x + _GELU_K * x * x * x))
    return 0.5 * (1.0 + th) + 0.5 * x * (1.0 - th * th) * _GELU_C * (1.0 + 3.0 * _GELU_K * x * x)


def _seg_sum(x, seg):
    def two_terms(v, dims):
        hi = v.astype(BF16)
        lo = (v - hi.astype(F32)).astype(BF16)
        return (lax.dot_general(hi, seg, dims, preferred_element_type=F32)
                + lax.dot_general(lo, seg, dims, preferred_element_type=F32))
    return two_terms(two_terms(x, _NN), _NT)


def _f32(v):
    return v.astype(F32)


def _norm_fwd(name, x, g, dtype=BF16):
    (hn,), _ = _rowwise(name, lambda rv, cv, pv, nv: ([_rms(_f32(rv[0]), cv[0])], []), [x], [g], [(x.shape[1], dtype)])
    return hn


def _resid_norm_fwd(name, x, t, g, bias=None):
    def body(rv, cv, pv, nv):
        tt = rv[1] if bias is None else rv[1] + cv[1]
        return [rv[0] + _rms(tt, cv[0])], []
    (y,), _ = _rowwise(name, body, [x, t], [g] if bias is None else [g, bias], [(x.shape[1], F32)])
    return y


def _resid_norm_bwd(name, dxn, t, g, bias=None):
    def body(rv, cv, pv, nv):
        tt = rv[1] if bias is None else rv[1] + cv[1]
        dt, dg = _rms_bwd(rv[0], tt, cv[0])
        return [dt], [dg, dt]
    d = t.shape[1]
    (dt,), (dg, db) = _rowwise(name, body, [dxn, t], [g] if bias is None else [g, bias], [(d, BF16)], [d, d])
    return dt, dg.sum(0), db.sum(0)


def _prenorm_bwd(name, dxn, dhn, x, g):
    def body(rv, cv, pv, nv):
        dx, dg = _rms_bwd(_f32(rv[1]), rv[2], cv[0])
        return [rv[0] + dx], [dg]
    d = x.shape[1]
    (dx,), (dg,) = _rowwise(name, body, [dxn, dhn, x], [g], [(d, F32)], [d])
    return dx, dg.sum(0)


def _mlp_fwd(tag, x, g_pre, g_post, w_up, w_down):
    hn = _norm_fwd(tag + "_norm", x, g_pre)
    act = _mm(tag + "_up", hn, w_up, out_dtype=BF16, epilogue=lambda up: jnp.square(jnp.maximum(up, 0.0)))
    m = _mm(tag + "_down", act, w_down)
    y = _resid_norm_fwd(tag + "_res", x, m, g_post)
    return y, (x, hn, act, m)


def _mlp_bwd(tag, saved, dy, g_pre, g_post, w_up_t, w_down_t):
    x, hn, act, m = saved
    dm, dg_post, _ = _resid_norm_bwd(tag + "_dres", dy, m, g_post)
    dup = _mm(tag + "_dup", dm, w_down_t, out_dtype=BF16, extra=act,
              epilogue=lambda dact, act_: dact * 2.0 * jnp.sqrt(_f32(act_)))
    dw_down = _mm(tag + "_dwdown", act, dm, trans_a=True)
    dw_up = _mm(tag + "_dwup", hn, dup, trans_a=True)
    dhn = _mm(tag + "_dhn", dup, w_up_t)
    dx, dg_pre = _prenorm_bwd(tag + "_dnorm", dy, dhn, x, g_pre)
    return dx, dict(g_pre=dg_pre, g_post=dg_post, w_up=dw_up, w_down=dw_down)


def _xattn_fwd(tag, x, mem_n, g_pre, g_post, w_q, w_kv, w_o, seq, mem_len):
    hn = _norm_fwd(tag + "_norm", x, g_pre)
    q = _mm(tag + "_q", hn, w_q, out_dtype=BF16)
    kv = _mm(tag + "_kv", mem_n, w_kv, out_dtype=BF16)
    o = _attn_fwd(q, kv, seq, mem_len)
    c = _mm(tag + "_o", o, w_o)
    y = _resid_norm_fwd(tag + "_res", x, c, g_post)
    return y, (x, hn, q, kv, o, c)


def _xattn_bwd(tag, saved, dy, mem_n, g_pre, g_post, w_q_t, w_kv_t, w_o_t, seq, mem_len):
    x, hn, q, kv, o, c = saved
    dc, dg_post, _ = _resid_norm_bwd(tag + "_dres", dy, c, g_post)
    do = _mm(tag + "_do", dc, w_o_t, out_dtype=BF16)
    dw_o = _mm(tag + "_dwo", o, dc, trans_a=True)
    dq, dkv = _attn_bwd(q, kv, do, seq, mem_len)
    dw_q = _mm(tag + "_dwq", hn, dq, trans_a=True)
    dhn = _mm(tag + "_dhn", dq, w_q_t)
    dw_kv = _mm(tag + "_dwkv", mem_n, dkv, trans_a=True)
    dmem_n = _mm(tag + "_dmem", dkv, w_kv_t)
    dx, dg_pre = _prenorm_bwd(tag + "_dnorm", dy, dhn, x, g_pre)
    return dx, dmem_n, dict(g_pre=dg_pre, g_post=dg_post, w_q=dw_q, w_kv=dw_kv, w_o=dw_o)


def _lru_gates(z0, z1, gb0, gb1, sp):
    r = jax.nn.sigmoid(z0 + gb0)
    i = jax.nn.sigmoid(z1 + gb1)
    log_a = -LRU_C * r * sp
    a = jnp.exp(log_a)
    mult = jnp.sqrt(_one_minus_exp(2.0 * log_a))
    return r, i, a, mult


def _rglru_fwd(x, p, seq):
    d = x.shape[1]
    hn = _norm_fwd("a_norm", x, p["g_pre"])
    proj = _mm("a_in", hn, p["w_in"])

    def conv_body(rv, cv, pv, nv):
        u = rv[0] + cv[0]
        halo = pv[0] + cv[0]
        i = pl.program_id(0)
        halo = jnp.where((i * rv[0].shape[0]) % seq == 0, jnp.zeros_like(halo), halo)
        conv = cv[2] + u * cv[1][CONV_WIDTH - 1:CONV_WIDTH]
        for tap in range(CONV_WIDTH - 1):
            conv = conv + _shift_down(u, halo, CONV_WIDTH - 1 - tap) * cv[1][tap:tap + 1]
        return [conv], []
    (conv,), _ = _rowwise("a_conv", conv_body, [(proj, d, 1)], [p["b_in_u"], p["conv_w"], p["conv_b"]], [(d, F32)],
                          prev=[0], seq=seq)
    z = _mm("a_gate", conv, p["w_gate"])

    def gate_body(rv, cv, pv, nv):
        r, i, a, mult = _lru_gates(rv[0], rv[1], cv[0], cv[1], _softplus(-cv[2]))
        return [a, mult * i * rv[2]], []
    (a, bb), _ = _rowwise("a_gates", gate_body, [(z, d, 0), (z, d, 1), conv], [p["gate_b0"], p["gate_b1"], p["lam"]],
                          [(d, F32), (d, F32)])
    h = _scan("a_scan", a, bb, seq)

    def hy_body(rv, cv, pv, nv):
        return [rv[0] * _gelu(rv[1] + cv[0])], []
    (hy,), _ = _rowwise("a_hy", hy_body, [h, (proj, d, 0)], [p["b_in_y"]], [(d, BF16)])
    out = _mm("a_out", hy, p["w_out"])
    y = _resid_norm_fwd("a_res", x, out, p["g_post"], bias=p["b_out"])
    return y, (x, hn, proj, conv, z, a, h, hy, out)


def _rglru_bwd(saved, dy, p, pt, seq):
    x, hn, proj, conv, z, a, h, hy, out = saved
    d = x.shape[1]
    dt, dg_post, db_out = _resid_norm_bwd("a_dres", dy, out, p["g_post"], bias=p["b_out"])
    dhy = _mm("a_dhy", dt, pt["w_out_t"])
    dw_out = _mm("a_dwout", hy, dt, trans_a=True)

    def dh_body(rv, cv, pv, nv):
        yb = rv[2] + cv[0]
        return [rv[0] * _gelu(yb), rv[0] * rv[1] * _gelu_grad(yb)], []
    (dh, dyb), _ = _rowwise("a_dh", dh_body, [dhy, h, (proj, d, 0)], [p["b_in_y"]], [(d, F32), (d, BF16)])
    g = _scan("a_rscan", a, dh, seq, reverse=True)

    def dgate_body(rv, cv, pv, nv):
        gg, hh, z0, z1, cnv = rv
        sp = _softplus(-cv[2])
        r, i, aa, mult = _lru_gates(z0, z1, cv[0], cv[1], sp)
        i_blk = pl.program_id(0)
        halo = jnp.where((i_blk * gg.shape[0]) % seq == 0, jnp.zeros_like(pv[0]), pv[0])
        da = gg * _shift_down(hh, halo, 1)
        dmult = gg * i * cnv
        di = gg * mult * cnv
        dconv = gg * mult * i
        dlog_a = da * aa - dmult * aa * aa / mult
        dz0 = dlog_a * (-LRU_C * sp) * r * (1.0 - r)
        dz1 = di * i * (1.0 - i)
        dsp = dlog_a * (-LRU_C * r)
        dlam = dsp * (-jax.nn.sigmoid(-cv[2]))
        return [jnp.concatenate([dz0, dz1], axis=1), dconv], [dz0, dz1, dlam]
    (dz, dconv1), (dgb0, dgb1, dlam) = _rowwise(
        "a_dgates", dgate_body, [g, h, (z, d, 0), (z, d, 1), conv], [p["gate_b0"], p["gate_b1"], p["lam"]],
        [(2 * d, BF16), (d, F32)], [d, d, d], prev=[1], seq=seq)
    dconv2 = _mm("a_dconv", dz, pt["w_gate_t"])
    dw_gate = _mm("a_dwgate", conv, dz, trans_a=True)

    def dconv_body(rv, cv, pv, nv):
        dc1, dc2, pu, dyb_ = rv
        dc = dc1 + dc2
        dc_next = nv[0] + nv[1]
        u = pu + cv[0]
        i_blk = pl.program_id(0)
        halo = jnp.where((i_blk * u.shape[0]) % seq == 0, jnp.zeros_like(pv[0]), pv[0] + cv[0])
        du = dc * cv[1][CONV_WIDTH - 1:CONV_WIDTH]
        dws = []
        for tap in range(CONV_WIDTH - 1):
            k = CONV_WIDTH - 1 - tap
            du = du + _shift_up(dc, dc_next, k) * cv[1][tap:tap + 1]
            dws.append(dc * _shift_down(u, halo, k))
        dws.append(dc * u)
        return [jnp.concatenate([_f32(dyb_), du], axis=1)], dws + [dc, _f32(dyb_), du]
    (dproj,), accs = _rowwise(
        "a_dconvw", dconv_body, [dconv1, dconv2, (proj, d, 1), dyb], [p["b_in_u"], p["conv_w"]],
        [(2 * d, BF16)], [d] * (CONV_WIDTH + 3), prev=[2], nxt=[0, 1], seq=seq)
    dconv_w = jnp.stack([acc.sum(0) for acc in accs[:CONV_WIDTH]])
    dconv_b = accs[CONV_WIDTH].sum(0)
    db_in = jnp.concatenate([accs[CONV_WIDTH + 1].sum(0), accs[CONV_WIDTH + 2].sum(0)])
    dhn = _mm("a_dhn", dproj, pt["w_in_t"])
    dw_in = _mm("a_dwin", hn, dproj, trans_a=True)
    dx, dg_pre = _prenorm_bwd("a_dnorm", dy, dhn, x, p["g_pre"])
    grads = dict(g_pre=dg_pre, g_post=dg_post, b_out=db_out, w_out=dw_out, gate_b0=dgb0.sum(0), gate_b1=dgb1.sum(0),
                 lam=dlam.sum(0), w_gate=dw_gate, conv_w=dconv_w, conv_b=dconv_b, b_in=db_in, w_in=dw_in)
    return dx, grads


def _rwkv_prep(k, wl, za, w0, a0, k_k, k_a, seg):
    w_in = wl + w0
    e_w = jnp.exp(-_softplus(-w_in) - 0.5)
    a = jax.nn.sigmoid(za + a0)
    q = k * k_k
    norm = jnp.sqrt(_seg_sum(q * q, seg))
    n = jnp.maximum(norm, 1e-12)
    kk = q / n
    return w_in, e_w, a, norm, n, kk


def _rwkv_out(y, r, k2, v, gn_g, gn_b, r_k, seg):
    inv = 1.0 / RWKV_HEAD
    yc = y - _seg_sum(y, seg) * inv
    rstd = lax.rsqrt(_seg_sum(yc * yc, seg) * inv + RWKV_GN_EPS)
    yhat = yc * rstd
    s = _seg_sum(r * k2 * r_k, seg)
    return rstd, yhat, s, yhat * gn_g + gn_b + s * v


def _rwkv_fwd(x, p, seq):
    t, d = x.shape
    nseq = t // seq

    def mix_body(rv, cv, pv, nv):
        hn = _rms(rv[0], cv[0])
        xx = _shift_down(hn, _rms(pv[0], cv[0]), 1) - hn
        return [hn] + [hn + xx * cv[1][c:c + 1] for c in range(6)], []
    (hn, xr, xw, xk, xv, xa, xg), _ = _rowwise("b_mix", mix_body, [x], [p["g_pre"], p["mu"]],
                                               [(d, F32)] + [(d, BF16)] * 6, prev=[0], seq=seq)
    r = _mm("b_r", xr, p["w_r"])
    k = _mm("b_k", xk, p["w_k"])
    v = _mm("b_v", xv, p["w_v"])
    lw = _mm("b_w1", xw, p["w1"])
    la = _mm("b_a1", xa, p["a1"], out_dtype=BF16)
    lg = _mm("b_g1", xg, p["g1"])
    (th,), _ = _rowwise("b_tanh", lambda rv, cv, pv, nv: ([jnp.tanh(rv[0])], []), [lw], [], [(lw.shape[1], BF16)])
    (sg,), _ = _rowwise("b_sig", lambda rv, cv, pv, nv: ([jax.nn.sigmoid(rv[0])], []), [lg], [], [(lg.shape[1], BF16)])
    wl = _mm("b_w2", th, p["w2"])
    za = _mm("b_a2", la, p["a2"])
    g = _mm("b_g2", sg, p["g2"])

    def prep_body(rv, cv, pv, nv):
        kk_, wl_, za_ = rv
        _, e_w, a, _, _, kk = _rwkv_prep(kk_, wl_, za_, cv[0], cv[1], cv[2], cv[3], cv[4])
        return [-e_w, kk_ * (1.0 + (a - 1.0) * cv[3]), -kk, kk * a], []
    (log_w, k2, rem_a, rem_b), _ = _rowwise("b_prep", prep_body, [k, wl, za],
                                            [p["w0"], p["a0"], p["k_k"], p["k_a"], p["seg"]], [(d, F32)] * 4, tm=256)
    rec_in = (r, log_w, k2, v, rem_a, rem_b)
    y, *states = _wkv_fwd(*rec_in, seq)

    def out_body(rv, cv, pv, nv):
        y_, r_, k2_, v_, g_ = rv
        _, _, _, out = _rwkv_out(y_, r_, k2_, v_, cv[0], cv[1], cv[2], cv[3])
        return [out * g_], []
    (og,), _ = _rowwise("b_out", out_body, [y, r, k2, v, g], [p["gn_g"], p["gn_b"], p["r_k"], p["seg"]], [(d, BF16)], tm=256)
    o = _mm("b_o", og, p["w_o"])
    res = _resid_norm_fwd("b_res", x, o, p["g_post"])
    return res, (x, hn, xr, xw, xk, xv, xa, xg, r, k, v, th, la, sg, wl, za, g, k2, rec_in, states, y, og, o)


def _rwkv_bwd(saved, dres, p, pt, seq):
    x, hn, xr, xw, xk, xv, xa, xg, r, k, v, th, la, sg, wl, za, g, k2, rec_in, states, y, og, o = saved
    t, d = x.shape
    nseq = t // seq
    do, dg_post, _ = _resid_norm_bwd("b_dres", dres, o, p["g_post"])
    dog = _mm("b_dog", do, pt["w_o_t"])
    dw_o = _mm("b_dwo", og, do, trans_a=True)

    def dout_body(rv, cv, pv, nv):
        dog_, y_, r_, k2_, v_, g_ = rv
        gn_g, gn_b, r_k, bd = cv
        inv = 1.0 / RWKV_HEAD
        rstd, yhat, s, out = _rwkv_out(y_, r_, k2_, v_, gn_g, gn_b, r_k, bd)
        dout = dog_ * g_
        ds = _seg_sum(dout * v_, bd)
        dyhat = dout * gn_g
        dy = rstd * (dyhat - _seg_sum(dyhat, bd) * inv - yhat * _seg_sum(dyhat * yhat, bd) * inv)
        return [dy, dog_ * out, dout * s, ds * k2_ * r_k, ds * r_ * r_k], [ds * r_ * k2_, dout * yhat, dout]
    (dy, dgate, dv_b, dr_b, dk2_b), (dr_k, dgn_g, dgn_b) = _rowwise(
        "b_dout", dout_body, [dog, y, r, k2, v, g], [p["gn_g"], p["gn_b"], p["r_k"], p["seg"]],
        [(d, F32), (d, BF16), (d, F32), (d, F32), (d, F32)], [d, d, d], tm=256)
    dr_rec, dlw_rec, dk2_rec, dv_rec, da_rec, db_rec = _wkv_bwd(*rec_in, *states, dy, seq)

    def dprep_body(rv, cv, pv, nv):
        dr_rec_, dlw_rec_, dk2_rec_, dv_rec_, da_rec_, db_rec_, dr_b_, dk2_b_, dv_b_, k_, wl_, za_ = rv
        w0, a0, k_k, k_a, bd = cv
        w_in, e_w, a, norm, n, kk = _rwkv_prep(k_, wl_, za_, w0, a0, k_k, k_a, bd)
        dk2 = dk2_rec_ + dk2_b_
        dkk = db_rec_ * a - da_rec_
        da = db_rec_ * kk + dk2 * k_ * k_a
        dq = jnp.where(norm > 1e-12, dkk - kk * _seg_sum(kk * dkk, bd), dkk) / n
        dk = dk2 * (1.0 + (a - 1.0) * k_a) + dq * k_k
        dza = da * a * (1.0 - a)
        dwl = dlw_rec_ * (-e_w) * jax.nn.sigmoid(-w_in)
        return [dr_rec_ + dr_b_, dk, dv_rec_ + dv_b_, dza, dwl], [dk2 * k_ * (a - 1.0), dq * k_, dza, dwl]
    (dr, dk, dv, dza, dwl), (dk_a, dk_k, da0, dw0) = _rowwise(
        "b_dprep", dprep_body, [dr_rec, dlw_rec, dk2_rec, dv_rec, da_rec, db_rec, dr_b, dk2_b, dv_b, k, wl, za],
        [p["w0"], p["a0"], p["k_k"], p["k_a"], p["seg"]], [(d, BF16)] * 5, [d] * 4, tm=256)

    dw_r = _mm("b_dwr", xr, dr, trans_a=True)
    dw_k = _mm("b_dwk", xk, dk, trans_a=True)
    dw_v = _mm("b_dwv", xv, dv, trans_a=True)
    dxr = _mm("b_dxr", dr, pt["w_r_t"])
    dxk = _mm("b_dxk", dk, pt["w_k_t"])
    dxv = _mm("b_dxv", dv, pt["w_v_t"])
    da2 = _mm("b_da2", la, dza, trans_a=True)
    dla = _mm("b_dla", dza, pt["a2_t"], out_dtype=BF16)
    da1 = _mm("b_da1", xa, dla, trans_a=True)
    dxa = _mm("b_dxa", dla, pt["a1_t"])
    dw2 = _mm("b_dw2", th, dwl, trans_a=True)
    dth = _mm("b_dth", dwl, pt["w2_t"])
    (dzw,), _ = _rowwise("b_dtanh", lambda rv, cv, pv, nv: ([rv[0] * (1.0 - _f32(rv[1]) * _f32(rv[1]))], []),
                         [dth, th], [], [(th.shape[1], BF16)])
    dw1 = _mm("b_dw1", xw, dzw, trans_a=True)
    dxw = _mm("b_dxw", dzw, pt["w1_t"])
    dg2 = _mm("b_dg2", sg, dgate, trans_a=True)
    dsg = _mm("b_dsg", dgate, pt["g2_t"])
    (dzg,), _ = _rowwise("b_dsig", lambda rv, cv, pv, nv: ([rv[0] * _f32(rv[1]) * (1.0 - _f32(rv[1]))], []),
                         [dsg, sg], [], [(sg.shape[1], BF16)])
    dg1 = _mm("b_dg1", xg, dzg, trans_a=True)
    dxg = _mm("b_dxg", dzg, pt["g1_t"])

    def dmix_body(rv, cv, pv, nv):
        hn_ = rv[0]
        dxs = rv[1:]
        mu = cv[0]
        xx = _shift_down(hn_, pv[0], 1) - hn_
        dsum = dxs[0]
        dxx = dxs[0] * mu[0:1]
        dxx_next = nv[0] * mu[0:1]
        for c in range(1, 6):
            dsum = dsum + dxs[c]
            dxx = dxx + dxs[c] * mu[c:c + 1]
            dxx_next = dxx_next + nv[c] * mu[c:c + 1]
        return [dsum - dxx + _shift_up(dxx, dxx_next, 1)], [dxs[c] * xx for c in range(6)]
    (dhn,), dmu = _rowwise("b_dmix", dmix_body, [hn, dxr, dxw, dxk, dxv, dxa, dxg], [p["mu"]], [(d, F32)], [d] * 6,
                           prev=[0], nxt=[1, 2, 3, 4, 5, 6], seq=seq, tm=256)
    dx, dg_pre = _prenorm_bwd("b_dnorm", dres, dhn, x, p["g_pre"])
    grads = dict(g_pre=dg_pre, g_post=dg_post, mu=jnp.stack([m.sum(0) for m in dmu]), w_r=dw_r, w_k=dw_k, w_v=dw_v,
                 w0=dw0.sum(0), w1=dw1, w2=dw2, a0=da0.sum(0), a1=da1, a2=da2, g1=dg1, g2=dg2, k_k=dk_k.sum(0),
                 k_a=dk_a.sum(0), r_k=dr_k.sum(0), gn_g=dgn_g.sum(0), gn_b=dgn_b.sum(0), w_o=dw_o)
    return dx, grads


_WEIGHTS = ['ln_gains', 'mem_norm', 'a_conv_w', 'a_conv_b', 'a_w_in', 'a_b_in', 'a_gate_w', 'a_gate_b', 'a_lambda', 'a_w_out',
            'a_b_out', 'b_mu', 'b_w_rkv', 'b_w0', 'b_w1', 'b_w2', 'b_a0', 'b_a1', 'b_a2', 'b_g1', 'b_g2', 'b_k_k', 'b_k_a',
            'b_r_k', 'b_gn_g', 'b_gn_b', 'b_w_o', 'c_w_q', 'c_w_kv', 'c_w_o', 'm_w_up', 'm_w_down']
_SHARD_AXIS = dict(ln_gains=2, mem_norm=None, a_conv_w=2, a_conv_b=None, a_w_in=2, a_b_in=None, a_gate_w=3, a_gate_b=3,
                   a_lambda=None, a_w_out=1, a_b_out=None, b_mu=2, b_w_rkv=2, b_w0=1, b_w1=1, b_w2=2, b_a0=1, b_a1=1, b_a2=2,
                   b_g1=1, b_g2=2, b_k_k=1, b_k_a=1, b_r_k=None, b_gn_g=1, b_gn_b=1, b_w_o=1, c_w_q=1, c_w_kv=2, c_w_o=1,
                   m_w_up=2, m_w_down=1)
_MATRICES = ['a_w_in', 'a_gate_w', 'a_w_out', 'b_w_rkv', 'b_w1', 'b_w2', 'b_a1', 'b_a2', 'b_g1', 'b_g2', 'b_w_o', 'c_w_q',
             'c_w_kv', 'c_w_o', 'm_w_up', 'm_w_down']
_SHARDED = [n for n in _WEIGHTS if _SHARD_AXIS[n] is not None]
_VECTORS = [n for n in _SHARDED if n not in _MATRICES]
_REPLICATED = [n for n in _WEIGHTS if _SHARD_AXIS[n] is None]
N_XY = 4
N_DEV = 8
PACK_W = 1024
PACK_ROWS = 256


def _pack(arrs, dtype, row_mult=PACK_ROWS):
    parts = []
    rows = 0
    for a in arrs:
        n = a.size
        r = -(-n // PACK_W)
        parts.append(jnp.pad(a.reshape(-1).astype(dtype), (0, r * PACK_W - n)))
        rows += r
    pad_rows = -(-rows // row_mult) * row_mult - rows
    if pad_rows:
        parts.append(jnp.zeros((pad_rows * PACK_W,), dtype))
    return jnp.concatenate(parts).reshape(-1, PACK_W)


def _unpack(flat, shapes):
    out = []
    row = 0
    for shp in shapes:
        n = 1
        for s in shp:
            n *= s
        r = -(-n // PACK_W)
        out.append(flat[row:row + r].reshape(-1)[:n].reshape(shp))
        row += r
    return out


_ANY = pl.BlockSpec(memory_space=pl.ANY)


def _xy_peers():
    x, y = lax.axis_index("x"), lax.axis_index("y")
    return [(1 - x, y), (x, 1 - y), (1 - x, 1 - y)]


def _all_gather_xy(wm, wv):
    half = wm.shape[0] // 2

    def body(wm_ref, wv_ref, gm_ref, gv_ref, send_sems, recv_sems, local_sems):
        x, y, c = lax.axis_index("x"), lax.axis_index("y"), lax.axis_index("c")
        me = 2 * x + y
        mine = pl.ds(pl.multiple_of(c * half, SUBLANES), half)
        other = pl.ds(pl.multiple_of((1 - c) * half, SUBLANES), half)
        local = [pltpu.make_async_copy(wm_ref, gm_ref.at[me], local_sems.at[0]),
                 pltpu.make_async_copy(wv_ref, gv_ref.at[me], local_sems.at[1])]
        for cp in local:
            cp.start()
        sends, lands, passes, from_sibling = [], [], [], []
        for j, (px, py) in enumerate(_xy_peers()):
            peer = 2 * px + py
            ici = functools.partial(pltpu.make_async_remote_copy, device_id=(px, py, c), device_id_type=MESH)
            sends.append(ici(src_ref=wm_ref.at[mine], dst_ref=gm_ref.at[me, mine], send_sem=send_sems.at[j], recv_sem=recv_sems.at[j]))
            lands.append(ici(src_ref=wm_ref.at[mine], dst_ref=gm_ref.at[peer, mine], send_sem=send_sems.at[j], recv_sem=recv_sems.at[j]))
            sends.append(ici(src_ref=wv_ref, dst_ref=gv_ref.at[me], send_sem=send_sems.at[3 + j], recv_sem=recv_sems.at[3 + j]))
            lands.append(ici(src_ref=wv_ref, dst_ref=gv_ref.at[peer], send_sem=send_sems.at[3 + j], recv_sem=recv_sems.at[3 + j]))
            d2d = functools.partial(pltpu.make_async_remote_copy, send_sem=send_sems.at[6 + j], recv_sem=recv_sems.at[6 + j],
                                    device_id=(x, y, 1 - c), device_id_type=MESH)
            passes.append(d2d(src_ref=gm_ref.at[peer, mine], dst_ref=gm_ref.at[peer, mine]))
            from_sibling.append(d2d(src_ref=gm_ref.at[peer, other], dst_ref=gm_ref.at[peer, other]))
        for cp in sends:
            cp.start()
        for j in range(N_XY - 1):
            lands[2 * j].wait_recv()
            passes[j].start()
        for j in range(N_XY - 1):
            lands[2 * j + 1].wait_recv()
        for cp in from_sibling:
            cp.wait_recv()
        for cp in sends + passes:
            cp.wait_send()
        for cp in local:
            cp.wait()

    return pl.pallas_call(
        body, name="all_gather_weights",
        in_specs=[_ANY, _ANY], out_specs=[_ANY, _ANY],
        out_shape=[jax.ShapeDtypeStruct((N_XY,) + wm.shape, wm.dtype), jax.ShapeDtypeStruct((N_XY,) + wv.shape, wv.dtype)],
        scratch_shapes=[pltpu.SemaphoreType.DMA((9,)), pltpu.SemaphoreType.DMA((9,)), pltpu.SemaphoreType.DMA((2,))],
    )(wm, wv)


def _exchange_xy(gsend):
    def body(gs_ref, recv_ref, send_sems, recv_sems):
        c = lax.axis_index("c")
        sends = []
        for j, (px, py) in enumerate(_xy_peers()):
            sends.append(pltpu.make_async_remote_copy(
                src_ref=gs_ref.at[2 * px + py], dst_ref=recv_ref.at[j], send_sem=send_sems.at[j], recv_sem=recv_sems.at[j],
                device_id=(px, py, c), device_id_type=MESH))
        for cp in sends:
            cp.start()
        for cp in sends:
            cp.wait_recv()
        for cp in sends:
            cp.wait_send()

    return pl.pallas_call(
        body, name="exchange_grads",
        in_specs=[_ANY], out_specs=_ANY,
        out_shape=jax.ShapeDtypeStruct((N_XY - 1,) + gsend.shape[1:], gsend.dtype),
        scratch_shapes=[pltpu.SemaphoreType.DMA((3,)), pltpu.SemaphoreType.DMA((3,))],
    )(gsend)


def _swap_with_sibling(part):
    def body(p_ref, got_ref, send_sem, recv_sem):
        x, y, c = lax.axis_index("x"), lax.axis_index("y"), lax.axis_index("c")
        cp = pltpu.make_async_remote_copy(src_ref=p_ref, dst_ref=got_ref, send_sem=send_sem, recv_sem=recv_sem,
                                          device_id=(x, y, 1 - c), device_id_type=MESH)
        cp.start()
        cp.wait_recv()
        cp.wait_send()

    return pl.pallas_call(
        body, name="swap_sibling",
        in_specs=[_ANY], out_specs=_ANY, out_shape=jax.ShapeDtypeStruct(part.shape, part.dtype),
        scratch_shapes=[pltpu.SemaphoreType.DMA, pltpu.SemaphoreType.DMA],
    )(part)


def _all_gather_all(vec):
    def body(v_ref, out_ref, send_sems, recv_sems, local_sem):
        x, y, c = lax.axis_index("x"), lax.axis_index("y"), lax.axis_index("c")
        me = 4 * x + 2 * y + c
        local = pltpu.make_async_copy(v_ref, out_ref.at[me], local_sem)
        local.start()
        sends, recvs = [], []
        for f in range(1, N_DEV):
            fx, fy, fc = (f >> 2) & 1, (f >> 1) & 1, f & 1
            px = (1 - x) if fx else x
            py = (1 - y) if fy else y
            pc = (1 - c) if fc else c
            mk = functools.partial(pltpu.make_async_remote_copy, src_ref=v_ref, send_sem=send_sems.at[f - 1],
                                   recv_sem=recv_sems.at[f - 1], device_id=(px, py, pc), device_id_type=MESH)
            sends.append(mk(dst_ref=out_ref.at[me]))
            recvs.append(mk(dst_ref=out_ref.at[4 * px + 2 * py + pc]))
        for cp in sends:
            cp.start()
        for cp in recvs:
            cp.wait_recv()
        for cp in sends:
            cp.wait_send()
        local.wait()

    return pl.pallas_call(
        body, name="all_gather_replicated",
        in_specs=[_ANY], out_specs=_ANY, out_shape=jax.ShapeDtypeStruct((N_DEV,) + vec.shape, vec.dtype),
        scratch_shapes=[pltpu.SemaphoreType.DMA((N_DEV - 1,)), pltpu.SemaphoreType.DMA((N_DEV - 1,)), pltpu.SemaphoreType.DMA],
    )(vec)


def _adamw(g, w, m, v):
    m2 = ADAM_B1 * m + (1.0 - ADAM_B1) * g
    v2 = ADAM_B2 * v + (1.0 - ADAM_B2) * g * g
    m_hat = m2 / (1.0 - ADAM_B1 ** ADAM_STEP)
    v_hat = v2 / (1.0 - ADAM_B2 ** ADAM_STEP)
    return -ADAM_LR * (m_hat / (jnp.sqrt(v_hat) + ADAM_EPS) + ADAM_WD * w), m2, v2


def _sum_contributions(own, recv):
    def body(rv, cv, pv, nv):
        return [((rv[0] + _f32(rv[1])) + _f32(rv[2])) + _f32(rv[3])], []
    stacked = recv.reshape(-1, PACK_W)
    (part,), _ = _rowwise("sum_grads", body, [own] + [(stacked, PACK_W, 0, j) for j in range(N_XY - 1)], [], [(PACK_W, F32)])
    return part


def _adamw_sharded(part, sib, w, m, v):
    def body(rv, cv, pv, nv):
        g = rv[0] + rv[1]
        return [g, *_adamw(g, rv[2], rv[3], rv[4])], []
    outs, _ = _rowwise("adamw_sharded", body, [part, sib, w, m, v], [], [(PACK_W, F32)] * 4, tm=256)
    return outs


def _adamw_replicated(parts, w, m, v):
    def body(rv, cv, pv, nv):
        g = rv[0]
        for i in range(1, N_DEV):
            g = g + rv[i]
        return [g, *_adamw(g, rv[N_DEV], rv[N_DEV + 1], rv[N_DEV + 2])], []
    outs, _ = _rowwise("adamw_replicated", body, [parts[i] for i in range(N_DEV)] + [w, m, v], [], [(PACK_W, F32)] * 4)
    return outs


def _row(v):
    return v.reshape(1, -1).astype(F32)


def _gate_dense(gate_w):
    _, nh, blk, _ = gate_w.shape
    d = nh * blk
    dense = jnp.zeros((d, 2 * d), gate_w.dtype)
    for g in range(2):
        for h in range(nh):
            dense = lax.dynamic_update_slice(dense, gate_w[g, h], (h * blk, g * d + h * blk))
    return dense


def _gate_blocks(dense, nh):
    d = dense.shape[0]
    blk = d // nh
    return jnp.stack([jnp.stack([dense[h * blk:(h + 1) * blk, g * d + h * blk:g * d + (h + 1) * blk] for h in range(nh)])
                      for g in range(2)])


def _local_step(x3, mem3, target3, fw):
    nseq, seq, d = x3.shape
    mem_len = mem3.shape[1]
    t = nseq * seq
    x0 = x3.reshape(t, d)
    mem2 = mem3.reshape(nseq * mem_len, d)
    target = target3.reshape(t, d)
    ln = fw["ln_gains"]
    gains = [[_row(ln[i, j]) for j in range(6)] for i in range(2)]
    nh = d // RWKV_HEAD
    seg = (jnp.arange(d)[:, None] // RWKV_HEAD == jnp.arange(LANES)[None, :]).astype(BF16)

    w_gate = _gate_dense(fw["a_gate_w"][0])
    pa = dict(g_pre=gains[0][0], g_post=gains[0][1], w_in=fw["a_w_in"][0], b_in_y=_row(fw["a_b_in"][0, :d]),
              b_in_u=_row(fw["a_b_in"][0, d:]), conv_w=fw["a_conv_w"][0].astype(F32), conv_b=_row(fw["a_conv_b"][0]),
              w_gate=w_gate, gate_b0=_row(fw["a_gate_b"][0, 0]), gate_b1=_row(fw["a_gate_b"][0, 1]), lam=_row(fw["a_lambda"][0]),
              w_out=fw["a_w_out"][0], b_out=_row(fw["a_b_out"][0]))
    pta = dict(w_in_t=_t(pa["w_in"]), w_gate_t=_t(w_gate), w_out_t=_t(pa["w_out"]))
    pb = dict(g_pre=gains[1][0], g_post=gains[1][1], mu=fw["b_mu"][0].astype(F32), w_r=fw["b_w_rkv"][0, 0],
              w_k=fw["b_w_rkv"][0, 1], w_v=fw["b_w_rkv"][0, 2], w0=_row(fw["b_w0"][0]), w1=fw["b_w1"][0], w2=fw["b_w2"][0],
              a0=_row(fw["b_a0"][0]), a1=fw["b_a1"][0], a2=fw["b_a2"][0], g1=fw["b_g1"][0], g2=fw["b_g2"][0],
              k_k=_row(fw["b_k_k"][0]), k_a=_row(fw["b_k_a"][0]), r_k=_row(fw["b_r_k"][0]), gn_g=_row(fw["b_gn_g"][0]),
              gn_b=_row(fw["b_gn_b"][0]), w_o=fw["b_w_o"][0], seg=seg)
    ptb = {k + "_t": _t(pb[k]) for k in ("w_r", "w_k", "w_v", "w_o", "w1", "w2", "a1", "a2", "g1", "g2")}
    mem_g = _row(fw["mem_norm"])

    mem_n = _norm_fwd("mem_norm", mem2, mem_g)
    x1, sv_a = _rglru_fwd(x0, pa, seq)
    x2, sv_c0 = _xattn_fwd("c0", x1, mem_n, gains[0][2], gains[0][3], fw["c_w_q"][0], fw["c_w_kv"][0], fw["c_w_o"][0], seq, mem_len)
    x3_, sv_m0 = _mlp_fwd("m0", x2, gains[0][4], gains[0][5], fw["m_w_up"][0], fw["m_w_down"][0])
    x4, sv_b = _rwkv_fwd(x3_, pb, seq)
    x5, sv_c1 = _xattn_fwd("c1", x4, mem_n, gains[1][2], gains[1][3], fw["c_w_q"][1], fw["c_w_kv"][1], fw["c_w_o"][1], seq, mem_len)
    x6, sv_m1 = _mlp_fwd("m1", x5, gains[1][4], gains[1][5], fw["m_w_up"][1], fw["m_w_down"][1])

    def loss_body(rv, cv, pv, nv):
        err = rv[0] - rv[1]
        return [err * (1.0 / d)], [err * err]
    (dx,), (sq,) = _rowwise("loss", loss_body, [x6, target], [], [(d, F32)], [d])
    loss_part = 0.5 / d * jnp.sum(sq)

    dx, g_m1 = _mlp_bwd("m1", sv_m1, dx, gains[1][4], gains[1][5], _t(fw["m_w_up"][1]), _t(fw["m_w_down"][1]))
    dx, dmem1, g_c1 = _xattn_bwd("c1", sv_c1, dx, mem_n, gains[1][2], gains[1][3], _t(fw["c_w_q"][1]), _t(fw["c_w_kv"][1]),
                                 _t(fw["c_w_o"][1]), seq, mem_len)
    dx, g_b = _rwkv_bwd(sv_b, dx, pb, ptb, seq)
    dx, g_m0 = _mlp_bwd("m0", sv_m0, dx, gains[0][4], gains[0][5], _t(fw["m_w_up"][0]), _t(fw["m_w_down"][0]))
    dx, dmem0, g_c0 = _xattn_bwd("c0", sv_c0, dx, mem_n, gains[0][2], gains[0][3], _t(fw["c_w_q"][0]), _t(fw["c_w_kv"][0]),
                                 _t(fw["c_w_o"][0]), seq, mem_len)
    dx, g_a = _rglru_bwd(sv_a, dx, pa, pta, seq)

    def dmem_body(rv, cv, pv, nv):
        _, dg = _rms_bwd(rv[1] + rv[2], rv[0], cv[0])
        return [], [dg]
    _, (dmem_g,) = _rowwise("mem_norm_grad", dmem_body, [mem2, dmem0, dmem1], [mem_g], [], [d])

    lru_heads = fw["a_gate_w"].shape[2]
    blk = d // lru_heads
    grads = dict(
        ln_gains=jnp.stack([jnp.stack([g_a["g_pre"], g_a["g_post"], g_c0["g_pre"], g_c0["g_post"], g_m0["g_pre"], g_m0["g_post"]]),
                            jnp.stack([g_b["g_pre"], g_b["g_post"], g_c1["g_pre"], g_c1["g_post"], g_m1["g_pre"], g_m1["g_post"]])]),
        mem_norm=dmem_g.sum(0),
        a_conv_w=g_a["conv_w"][None], a_conv_b=g_a["conv_b"][None], a_w_in=g_a["w_in"][None], a_b_in=g_a["b_in"][None],
        a_gate_w=_gate_blocks(g_a["w_gate"], lru_heads)[None],
        a_gate_b=jnp.stack([g_a["gate_b0"], g_a["gate_b1"]]).reshape(1, 2, lru_heads, blk),
        a_lambda=g_a["lam"][None], a_w_out=g_a["w_out"][None], a_b_out=g_a["b_out"][None],
        b_mu=g_b["mu"][None], b_w_rkv=jnp.stack([g_b["w_r"], g_b["w_k"], g_b["w_v"]])[None], b_w0=g_b["w0"][None],
        b_w1=g_b["w1"][None], b_w2=g_b["w2"][None], b_a0=g_b["a0"][None], b_a1=g_b["a1"][None], b_a2=g_b["a2"][None],
        b_g1=g_b["g1"][None], b_g2=g_b["g2"][None], b_k_k=g_b["k_k"][None], b_k_a=g_b["k_a"][None],
        b_r_k=g_b["r_k"].reshape(1, nh, RWKV_HEAD), b_gn_g=g_b["gn_g"][None], b_gn_b=g_b["gn_b"][None], b_w_o=g_b["w_o"][None],
        c_w_q=[g_c0["w_q"], g_c1["w_q"]], c_w_kv=[g_c0["w_kv"], g_c1["w_kv"]], c_w_o=[g_c0["w_o"], g_c1["w_o"]],
        m_w_up=[g_m0["w_up"], g_m1["w_up"]], m_w_down=[g_m0["w_down"], g_m1["w_down"]],
    )
    return loss_part, dx.reshape(nseq, seq, d), grads


def kernel(x, mem, ln_gains, mem_norm, a_conv_w, a_conv_b, a_w_in, a_b_in, a_gate_w, a_gate_b, a_lambda, a_w_out, a_b_out, b_mu, b_w_rkv, b_w0, b_w1, b_w2, b_a0, b_a1, b_a2, b_g1, b_g2, b_k_k, b_k_a, b_r_k, b_gn_g, b_gn_b, b_w_o, c_w_q, c_w_kv, c_w_o, m_w_up, m_w_down, loss_target, m_ln_gains, m_mem_norm, m_a_conv_w, m_a_conv_b, m_a_w_in, m_a_b_in, m_a_gate_w, m_a_gate_b, m_a_lambda, m_a_w_out, m_a_b_out, m_b_mu, m_b_w_rkv, m_b_w0, m_b_w1, m_b_w2, m_b_a0, m_b_a1, m_b_a2, m_b_g1, m_b_g2, m_b_k_k, m_b_k_a, m_b_r_k, m_b_gn_g, m_b_gn_b, m_b_w_o, m_c_w_q, m_c_w_kv, m_c_w_o, m_m_w_up, m_m_w_down, v_ln_gains, v_mem_norm, v_a_conv_w, v_a_conv_b, v_a_w_in, v_a_b_in, v_a_gate_w, v_a_gate_b, v_a_lambda, v_a_w_out, v_a_b_out, v_b_mu, v_b_w_rkv, v_b_w0, v_b_w1, v_b_w2, v_b_a0, v_b_a1, v_b_a2, v_b_g1, v_b_g2, v_b_k_k, v_b_k_a, v_b_r_k, v_b_gn_g, v_b_gn_b, v_b_w_o, v_c_w_q, v_c_w_kv, v_c_w_o, v_m_w_up, v_m_w_down):
    given = dict(locals())
    w = {n: given[n] for n in _WEIGHTS}
    mom1 = {n: given["m_" + n] for n in _WEIGHTS}
    mom2 = {n: given["v_" + n] for n in _WEIGHTS}

    gm, gv = _all_gather_xy(_pack([w[n] for n in _MATRICES], BF16), _pack([w[n] for n in _VECTORS], F32, SUBLANES))
    fw = {n: w[n] for n in _REPLICATED}
    for names, buf in ((_MATRICES, gm), (_VECTORS, gv)):
        shards = [_unpack(buf[s], [w[n].shape for n in names]) for s in range(N_XY)]
        for i, n in enumerate(names):
            fw[n] = jnp.concatenate([shards[s][i] for s in range(N_XY)], axis=_SHARD_AXIS[n])

    loss_part, grad_x, grads = _local_step(x, mem, loss_target, fw)

    def shards_of(n, s):
        ax = _SHARD_AXIS[n]
        size = w[n].shape[ax]
        if isinstance(grads[n], list):
            assert all(g.size // N_XY % PACK_W == 0 for g in grads[n])
            return [lax.slice_in_dim(g, s * size, (s + 1) * size, axis=ax - 1) for g in grads[n]]
        return [lax.slice_in_dim(grads[n], s * size, (s + 1) * size, axis=ax)]
    gfull = jnp.stack([_pack([g for n in _SHARDED for g in shards_of(n, s)], F32) for s in range(N_XY)])
    me = 2 * lax.axis_index("x") + lax.axis_index("y")
    recv = _exchange_xy(gfull.astype(BF16))
    part = _sum_contributions(lax.dynamic_index_in_dim(gfull, me, 0, keepdims=False), recv)
    sib = _swap_with_sibling(part)
    flat = [_pack([src[n] for n in _SHARDED], F32) for src in (w, mom1, mom2)]
    sharded_out = [_unpack(o, [w[n].shape for n in _SHARDED]) for o in _adamw_sharded(part, sib, *flat)]

    small = _pack([grads[n] for n in _REPLICATED] + [loss_part.reshape(1)], F32, SUBLANES)
    parts = _all_gather_all(small)
    zero = jnp.zeros((1,), F32)
    flat = [_pack([src[n] for n in _REPLICATED] + [zero], F32, SUBLANES) for src in (w, mom1, mom2)]
    repl_out = [_unpack(o, [w[n].shape for n in _REPLICATED] + [(1,)]) for o in _adamw_replicated(parts, *flat)]
    loss = repl_out[0][-1][0]

    result = [loss, grad_x]
    for kind in range(4):
        by_name = dict(zip(_SHARDED, sharded_out[kind])) | dict(zip(_REPLICATED, repl_out[kind][:-1]))
        result += [by_name[n] for n in _WEIGHTS]
    return tuple(result)
```

```python
import functools

import jax
import jax.numpy as jnp
from jax import lax
from jax.experimental import pallas as pl
from jax.experimental.pallas import tpu as pltpu

F32 = jnp.float32
BF16 = jnp.bfloat16
MESH = pl.DeviceIdType.MESH

LANES = 128
SUBLANES = 8
VMEM_LIMIT_BYTES = 48 * 1024 * 1024

RMS_EPS = 1e-6
LRU_C = 8.0
LRU_HEADS = 4
CONV_WIDTH = 4
RWKV_HEAD = 64
RWKV_GN_EPS = 64e-5
MEM_HEADS = 4
ADAM_LR = 0.001
ADAM_B1 = 0.9
ADAM_B2 = 0.999
ADAM_EPS = 1e-08
ADAM_WD = 0.01
ADAM_STEP = 10
WKV_CHUNK = 64

_PARAMS = functools.partial(pltpu.CompilerParams, vmem_limit_bytes=VMEM_LIMIT_BYTES)


def _tile(n, want):
    if n <= want:
        return n
    t = want
    while t >= SUBLANES:
        if n % t == 0 and t % SUBLANES == 0:
            return t
        t -= SUBLANES
    return n


def _fold8(v):
    tm, d = v.shape
    if tm == SUBLANES:
        return v
    return jnp.sum(v.reshape(tm // SUBLANES, SUBLANES, d), axis=0)


def _rowwise(name, body, rows, consts=(), out_rows=(), out_accs=(), prev=(), nxt=(), tm=512, seq=None):
    rows = [r if isinstance(r, tuple) else (r, r.shape[1], 0) for r in rows]
    rows = [r if len(r) == 4 else r + (0,) for r in rows]
    t = rows[0][0].shape[0]
    tm = _tile(t, tm)
    if seq is not None:
        tm = _tile(seq, tm)
    nblk = t // tm
    nrow, ncst, nprev, nnxt = len(rows), len(consts), len(prev), len(nxt)
    nor, noa = len(out_rows), len(out_accs)
    hb = tm // SUBLANES

    def kern(*refs):
        i = pl.program_id(0)
        rv = [r[...] for r in refs[:nrow]]
        cv = [c[...] for c in refs[nrow:nrow + ncst]]
        o = nrow + ncst
        pv = []
        for j in range(nprev):
            at_start = (i * tm) % seq == 0
            h = refs[o + j][...]
            pv.append(jnp.where(at_start, jnp.zeros_like(h), h))
        o += nprev
        nv = []
        for j in range(nnxt):
            at_end = ((i + 1) * tm) % seq == 0
            h = refs[o + j][...]
            nv.append(jnp.where(at_end, jnp.zeros_like(h), h))
        o += nnxt
        outs, accs = body(rv, cv, pv, nv)
        for j in range(nor):
            refs[o + j][...] = outs[j].astype(refs[o + j].dtype)
        o += nor
        if noa:
            @pl.when(i == 0)
            def _():
                for j in range(noa):
                    refs[o + j][...] = jnp.zeros_like(refs[o + j])
            for j in range(noa):
                refs[o + j][...] += _fold8(accs[j].astype(F32))

    assert all(first % tm == 0 for (_, _, _, first) in rows), name
    in_specs = [pl.BlockSpec((tm, w), functools.partial(lambda i, c, o: (i + o, c), c=cb, o=first // tm))
                for (_, w, cb, first) in rows]
    in_specs += [pl.BlockSpec(c.shape, lambda i: (0, 0)) for c in consts]
    in_specs += [pl.BlockSpec((SUBLANES, rows[j][1]),
                              functools.partial(lambda i, c: (jnp.maximum(i * hb - 1, 0), c), c=rows[j][2])) for j in prev]
    in_specs += [pl.BlockSpec((SUBLANES, rows[j][1]),
                              functools.partial(lambda i, c: (jnp.minimum((i + 1) * hb, t // SUBLANES - 1), c), c=rows[j][2]))
                 for j in nxt]
    out_shape = [jax.ShapeDtypeStruct((t, w), dt) for (w, dt) in out_rows]
    out_shape += [jax.ShapeDtypeStruct((SUBLANES, w), F32) for w in out_accs]
    out_specs = [pl.BlockSpec((tm, w), lambda i: (i, 0)) for (w, _) in out_rows]
    out_specs += [pl.BlockSpec((SUBLANES, w), lambda i: (0, 0)) for w in out_accs]
    args = [r[0] for r in rows] + list(consts) + [rows[j][0] for j in prev] + [rows[j][0] for j in nxt]
    res = pl.pallas_call(
        kern, name=name, grid=(nblk,), in_specs=in_specs, out_specs=out_specs, out_shape=out_shape,
        compiler_params=_PARAMS(dimension_semantics=("arbitrary",)),
    )(*args)
    return list(res[:nor]), list(res[nor:])


def _shift_down(x, halo, k):
    rolled = pltpu.roll(x, k, 0)
    row = lax.broadcasted_iota(jnp.int32, (SUBLANES, x.shape[1]), 0)
    first = jnp.where(row < k, pltpu.roll(halo, k, 0), rolled[:SUBLANES])
    if x.shape[0] == SUBLANES:
        return first
    return jnp.concatenate([first, rolled[SUBLANES:]], axis=0)


def _shift_up(x, halo, k):
    n = x.shape[0]
    rolled = pltpu.roll(x, n - k, 0)
    row = lax.broadcasted_iota(jnp.int32, (SUBLANES, x.shape[1]), 0)
    last = jnp.where(row >= SUBLANES - k, pltpu.roll(halo, SUBLANES - k, 0), rolled[n - SUBLANES:])
    if n == SUBLANES:
        return last
    return jnp.concatenate([rolled[:n - SUBLANES], last], axis=0)


class _Transposed:
    def __init__(self, w):
        self.w = w


def _t(w):
    return _Transposed(w)


def _mm(name, a, b, out_dtype=F32, trans_a=False, tm=1024, tn=1024, tk=1024, epilogue=None, extra=None):
    trans_b = isinstance(b, _Transposed)
    assert not (trans_a and trans_b)
    if trans_b:
        b = b.w
    if trans_a:
        kdim, m = a.shape
    else:
        m, kdim = a.shape
    n = b.shape[0] if trans_b else b.shape[1]
    assert b.shape[1 if trans_b else 0] == kdim, (name, a.shape, b.shape)
    tm, tn, tk = _tile(m, tm), _tile(n, tn), _tile(kdim, tk)
    nk = kdim // tk
    dims = (((0,), (0,)), ((), ())) if trans_a else (((1,), (1 if trans_b else 0,)), ((), ()))

    n_in = 2 if extra is None else 3

    def kern(*refs):
        a_ref, b_ref, o_ref, acc = refs[0], refs[1], refs[n_in], refs[n_in + 1:]

        def store(res):
            if epilogue is not None:
                res = epilogue(res) if extra is None else epilogue(res, refs[2][...])
            o_ref[...] = res.astype(o_ref.dtype)

        part = lax.dot_general(a_ref[...].astype(BF16), b_ref[...].astype(BF16), dims, preferred_element_type=F32)
        if nk == 1:
            store(part)
        else:
            k = pl.program_id(2)

            @pl.when(k == 0)
            def _():
                acc[0][...] = part

            @pl.when(k > 0)
            def _():
                acc[0][...] += part

            @pl.when(k == nk - 1)
            def _():
                store(acc[0][...])

    a_spec = pl.BlockSpec((tk, tm), lambda i, j, k: (k, i)) if trans_a else pl.BlockSpec((tm, tk), lambda i, j, k: (i, k))
    b_spec = pl.BlockSpec((tn, tk), lambda i, j, k: (j, k)) if trans_b else pl.BlockSpec((tk, tn), lambda i, j, k: (k, j))
    out_spec = pl.BlockSpec((tm, tn), lambda i, j, k: (i, j))
    return pl.pallas_call(
        kern, name=name, grid=(m // tm, n // tn, nk),
        in_specs=[a_spec, b_spec] + ([] if extra is None else [out_spec]),
        out_specs=out_spec,
        out_shape=jax.ShapeDtypeStruct((m, n), out_dtype),
        scratch_shapes=[] if nk == 1 else [pltpu.VMEM((tm, tn), F32)],
        compiler_params=_PARAMS(dimension_semantics=("parallel", "parallel", "arbitrary")),
    )(*((a, b) if extra is None else (a, b, extra)))


def _scan(name, a, b, seq, reverse=False, tm=256):
    t, d = a.shape
    tm = _tile(seq, tm)
    nblk = t // tm
    ntile = tm // SUBLANES

    def kern(a_ref, b_ref, h_ref, carry_h, carry_a):
        i = pl.program_id(0)
        blk = (nblk - 1 - i) if reverse else i
        edge = (((blk + 1) * tm) % seq == 0) if reverse else ((blk * tm) % seq == 0)

        @pl.when(edge)
        def _():
            carry_h[...] = jnp.zeros_like(carry_h)
            carry_a[...] = jnp.zeros_like(carry_a)

        def tile_step(j, c):
            jj = (ntile - 1 - j) if reverse else j
            rows = pl.ds(pl.multiple_of(jj * SUBLANES, SUBLANES), SUBLANES)
            a8 = a_ref[rows, :]
            b8 = b_ref[rows, :]
            h, an = c
            out = [None] * SUBLANES
            order = range(SUBLANES - 1, -1, -1) if reverse else range(SUBLANES)
            for r in order:
                if reverse:
                    h = b8[r:r + 1, :] + an * h
                    an = a8[r:r + 1, :]
                else:
                    h = a8[r:r + 1, :] * h + b8[r:r + 1, :]
                out[r] = h
            h_ref[rows, :] = jnp.concatenate(out, axis=0)
            return (h, an)

        h, an = lax.fori_loop(0, ntile, tile_step, (carry_h[...], carry_a[...]))
        carry_h[...] = h
        carry_a[...] = an

    idx = (lambda i: (nblk - 1 - i, 0)) if reverse else (lambda i: (i, 0))
    return pl.pallas_call(
        kern, name=name, grid=(nblk,),
        in_specs=[pl.BlockSpec((tm, d), idx), pl.BlockSpec((tm, d), idx)],
        out_specs=pl.BlockSpec((tm, d), idx),
        out_shape=jax.ShapeDtypeStruct((t, d), F32),
        scratch_shapes=[pltpu.VMEM((1, d), F32), pltpu.VMEM((1, d), F32)],
        compiler_params=_PARAMS(dimension_semantics=("arbitrary",)),
    )(a, b)


_NN = (((1,), (0,)), ((), ()))
_NT = (((1,), (1,)), ((), ()))
_TN = (((0,), (0,)), ((), ()))


def _dot1(a, b, dims):
    return lax.dot_general(a.astype(BF16), b.astype(BF16), dims, preferred_element_type=F32)


def _dot3(a, b, dims):
    a_hi, b_hi = a.astype(BF16), b.astype(BF16)
    a_lo, b_lo = (a - a_hi.astype(F32)).astype(BF16), (b - b_hi.astype(F32)).astype(BF16)
    dg = lambda p, q: lax.dot_general(p, q, dims, preferred_element_type=F32)
    return dg(a_hi, b_hi) + (dg(a_hi, b_lo) + dg(a_lo, b_hi))


def _make_bmm(dot):
    def make(dims, da_rule, db_rule):
        @jax.custom_vjp
        def f(a, b):
            return dot(a, b, dims)

        def fwd(a, b):
            return dot(a, b, dims), (a, b)

        def bwd(res, g):
            a, b = res
            return da_rule(a, b, g), db_rule(a, b, g)

        f.defvjp(fwd, bwd)
        return f

    return dict(nn=make(_NN, lambda a, b, g: dot(g, b, _NT), lambda a, b, g: dot(a, g, _TN)),
                nt=make(_NT, lambda a, b, g: dot(g, b, _NN), lambda a, b, g: dot(g, a, _TN)),
                tn=make(_TN, lambda a, b, g: dot(b, g, _NT), lambda a, b, g: dot(a, g, _NN)))


_BMM = {1: _make_bmm(_dot1), 3: _make_bmm(_dot3)}
_WKV_PASSES = dict(pair=1, read=1, inv=3, apply=1, write=1)


def _running_sum(x, reverse):
    c = x.shape[0]
    row = lax.broadcasted_iota(jnp.int32, x.shape, 0)
    k = 1
    while k < c:
        if reverse:
            x = x + jnp.where(row < c - k, pltpu.roll(x, c - k, 0), 0.0)
        else:
            x = x + jnp.where(row >= k, pltpu.roll(x, k, 0), 0.0)
        k *= 2
    return x


@jax.custom_vjp
def _cumsum_rows(x):
    return _running_sum(x, False)


_cumsum_rows.defvjp(lambda x: (_running_sum(x, False), None), lambda _, g: (_running_sum(g, True),))


@jax.custom_vjp
def _unit_lower_inverse(nl):
    c = nl[0].shape[0]
    mm = _BMM[_WKV_PASSES["inv"]]["nn"]
    eye = jnp.where(lax.broadcasted_iota(jnp.int32, (c, c), 0) == lax.broadcasted_iota(jnp.int32, (c, c), 1), 1.0, 0.0)
    inv = [eye + z for z in nl]
    p = nl
    for _ in range(c.bit_length() - 2):
        p = [mm(z, z) for z in p]
        inv = [i_ + mm(p_, i_) for i_, p_ in zip(inv, p)]
    return inv


def _unit_lower_inverse_fwd(nl):
    inv = _unit_lower_inverse(nl)
    return inv, inv


def _unit_lower_inverse_bwd(inv, g):
    mm = _BMM[_WKV_PASSES["inv"]]
    left = [mm["tn"](x, g_) for x, g_ in zip(inv, g)]
    return ([mm["nt"](l_, x) for l_, x in zip(left, inv)],)


_unit_lower_inverse.defvjp(_unit_lower_inverse_fwd, _unit_lower_inverse_bwd)


@jax.custom_vjp
def _kept_inverse(nl, inv):
    return inv


_kept_inverse.defvjp(lambda nl, inv: (inv, inv),
                     lambda inv, g: (_unit_lower_inverse_bwd(inv, g)[0], [jnp.zeros_like(x) for x in inv]))


def _wkv_chunk(r, lw, k, v, a, b, s0, kept_inv=None):
    c = r[0].shape[0]
    ti = lax.broadcasted_iota(jnp.int32, (c, 2 * c), 0)
    tj = lax.broadcasted_iota(jnp.int32, (c, 2 * c), 1)
    right = tj >= c
    tau = jnp.where(right, tj - c, tj)
    strict_left = jnp.logical_and(jnp.logical_not(right), tau < ti)[:, :c]
    strict_right = jnp.logical_and(right, tau < ti)
    incl = tau <= ti
    last = lax.broadcasted_iota(jnp.int32, r[0].shape, 0) == c - 1
    each = lambda f, *ls: [f(*z) for z in zip(*ls)]
    rows2 = lambda x, y: jnp.concatenate([x, y], axis=0)
    pair, read, inv_, apply_, write = (_BMM[_WKV_PASSES[role]] for role in ("pair", "read", "inv", "apply", "write"))
    cum = each(_cumsum_rows, lw)
    w_incl = each(jnp.exp, cum)
    w_inv = each(lambda z: jnp.exp(-z), cum)
    at = each(lambda a_, c_, l_: a_ * jnp.exp(c_ - l_), a, cum, lw)
    ar = each(rows2, at, each(jnp.multiply, r, w_incl))
    bk = each(rows2, each(jnp.multiply, b, w_inv), each(jnp.multiply, k, w_inv))
    pp = each(pair["nt"], ar, bk)
    sr = each(read["nt"], ar, s0)
    nl = [jnp.where(strict_left, z[:c, :c], 0.0) for z in pp]
    zero_v = each(lambda v_: rows2(jnp.zeros_like(v_), v_), v)
    rhs = each(lambda s, z, zv: s[:c] + apply_["nn"](jnp.where(strict_right, z[:c], 0.0), zv), sr, pp, zero_v)
    inv = _unit_lower_inverse(nl) if kept_inv is None else _kept_inverse(nl, kept_inv)
    ut = each(inv_["nn"], inv, rhs)
    uv = each(rows2, ut, v)
    y = each(lambda s, z, uv_: s[c:] + apply_["nn"](jnp.where(incl, z[c:], 0.0), uv_), sr, pp, uv)
    w_end = each(lambda z: jnp.exp(jnp.sum(jnp.where(last, z, 0.0), axis=0, keepdims=True)), cum)
    s1 = each(lambda s, uv_, bk_, w_: (s + write["tn"](uv_, bk_)) * w_, s0, uv, bk, w_end)
    return y, s1, inv


def _wkv_fwd(r, lw, k, v, a, b, seq, hb=16):
    t, d = r.shape
    n = RWKV_HEAD
    nh = d // n
    hb = min(hb, nh)
    chunk = min(WKV_CHUNK, seq)
    ncs = seq // chunk

    def kern(r_ref, lw_ref, k_ref, v_ref, a_ref, b_ref, y_ref, st_ref, inv_ref, s_scr):
        @pl.when(pl.program_id(2) == 0)
        def _():
            s_scr[...] = jnp.zeros_like(s_scr)

        heads = lambda ref: [ref[:, h * n:(h + 1) * n] for h in range(hb)]
        s0 = [s_scr[h] for h in range(hb)]
        y, s1, inv = _wkv_chunk(heads(r_ref), heads(lw_ref), heads(k_ref), heads(v_ref), heads(a_ref), heads(b_ref), s0)
        for h in range(hb):
            st_ref[0, h] = s0[h]
            inv_ref[0, h] = inv[h]
            y_ref[:, h * n:(h + 1) * n] = y[h]
            s_scr[h] = s1[h]

    vec = pl.BlockSpec((chunk, hb * n), lambda bb, g, c: (bb * ncs + c, g))
    per_chunk = lambda rows: pl.BlockSpec((1, hb, rows, rows), lambda bb, g, c: (bb * ncs + c, g, 0, 0))
    return pl.pallas_call(
        kern, name="wkv_fwd", grid=(t // seq, nh // hb, ncs), in_specs=[vec] * 6,
        out_specs=[vec, per_chunk(n), per_chunk(chunk)],
        out_shape=[jax.ShapeDtypeStruct((t, d), F32), jax.ShapeDtypeStruct((t // chunk, nh, n, n), F32),
                   jax.ShapeDtypeStruct((t // chunk, nh, chunk, chunk), F32)],
        scratch_shapes=[pltpu.VMEM((hb, n, n), F32)],
        compiler_params=_PARAMS(dimension_semantics=("parallel", "parallel", "arbitrary")),
    )(r, lw, k, v, a, b)


def _wkv_bwd(r, lw, k, v, a, b, st, inv, dy, seq, hb=16):
    t, d = r.shape
    n = RWKV_HEAD
    nh = d // n
    hb = min(hb, nh)
    chunk = min(WKV_CHUNK, seq)
    ncs = seq // chunk

    def kern(r_ref, lw_ref, k_ref, v_ref, a_ref, b_ref, st_ref, inv_ref, dy_ref,
             dr_ref, dlw_ref, dk_ref, dv_ref, da_ref, db_ref, ds_scr):
        @pl.when(pl.program_id(2) == 0)
        def _():
            ds_scr[...] = jnp.zeros_like(ds_scr)

        heads = lambda ref: [ref[:, h * n:(h + 1) * n] for h in range(hb)]
        kept = [inv_ref[0, h] for h in range(hb)]
        _, vjp = jax.vjp(lambda *args: _wkv_chunk(*args, kept_inv=kept)[:2],
                         heads(r_ref), heads(lw_ref), heads(k_ref), heads(v_ref), heads(a_ref), heads(b_ref),
                         [st_ref[0, h] for h in range(hb)])
        grads = vjp((heads(dy_ref), [ds_scr[h] for h in range(hb)]))
        for h in range(hb):
            for ref, g in zip((dr_ref, dlw_ref, dk_ref, dv_ref, da_ref, db_ref), grads[:6]):
                ref[:, h * n:(h + 1) * n] = g[h]
            ds_scr[h] = grads[6][h]

    vec = pl.BlockSpec((chunk, hb * n), lambda bb, g, c: (bb * ncs + ncs - 1 - c, g))
    per_chunk = lambda rows: pl.BlockSpec((1, hb, rows, rows), lambda bb, g, c: (bb * ncs + ncs - 1 - c, g, 0, 0))
    return pl.pallas_call(
        kern, name="wkv_bwd", grid=(t // seq, nh // hb, ncs),
        in_specs=[vec] * 6 + [per_chunk(n), per_chunk(chunk), vec],
        out_specs=[vec] * 6, out_shape=[jax.ShapeDtypeStruct((t, d), F32)] * 6,
        scratch_shapes=[pltpu.VMEM((hb, n, n), F32)],
        compiler_params=_PARAMS(dimension_semantics=("parallel", "parallel", "arbitrary")),
    )(r, lw, k, v, a, b, st, inv, dy)


def _softmax_rows(s):
    e = jnp.exp(s - jnp.max(s, axis=-1, keepdims=True))
    return e / jnp.sum(e, axis=-1, keepdims=True)


def _attn_fwd(q, kv, seq, mem_len, tq=256):
    t, d = q.shape
    dh = d // MEM_HEADS
    scale = dh ** -0.5
    tq = _tile(seq, tq)
    nq = seq // tq

    def kern(q_ref, kv_ref, o_ref):
        for h in range(MEM_HEADS):
            cols = slice(h * dh, (h + 1) * dh)
            vcols = slice(d + h * dh, d + (h + 1) * dh)
            s = lax.dot_general(q_ref[:, cols], kv_ref[:, cols], _NT, preferred_element_type=F32) * scale
            p = _softmax_rows(s)
            o_ref[:, cols] = jnp.dot(p.astype(BF16), kv_ref[:, vcols], preferred_element_type=F32).astype(o_ref.dtype)

    return pl.pallas_call(
        kern, name="attn_fwd", grid=(t // seq, nq),
        in_specs=[pl.BlockSpec((tq, d), lambda b, i: (b * nq + i, 0)), pl.BlockSpec((mem_len, 2 * d), lambda b, i: (b, 0))],
        out_specs=pl.BlockSpec((tq, d), lambda b, i: (b * nq + i, 0)),
        out_shape=jax.ShapeDtypeStruct((t, d), BF16),
        compiler_params=_PARAMS(dimension_semantics=("parallel", "parallel")),
    )(q, kv)


def _attn_bwd(q, kv, do, seq, mem_len, tq=256):
    t, d = q.shape
    dh = d // MEM_HEADS
    scale = dh ** -0.5
    tq = _tile(seq, tq)
    nq = seq // tq

    def kern(q_ref, kv_ref, do_ref, dq_ref, dkv_ref):
        @pl.when(pl.program_id(1) == 0)
        def _():
            dkv_ref[...] = jnp.zeros_like(dkv_ref)

        for h in range(MEM_HEADS):
            cols = slice(h * dh, (h + 1) * dh)
            vcols = slice(d + h * dh, d + (h + 1) * dh)
            qh, kh, vh, doh = q_ref[:, cols], kv_ref[:, cols], kv_ref[:, vcols], do_ref[:, cols]
            p = _softmax_rows(lax.dot_general(qh, kh, _NT, preferred_element_type=F32) * scale)
            dp = lax.dot_general(doh, vh, _NT, preferred_element_type=F32)
            ds = (p * (dp - jnp.sum(p * dp, axis=-1, keepdims=True)) * scale).astype(BF16)
            dq_ref[:, cols] = jnp.dot(ds, kh, preferred_element_type=F32).astype(dq_ref.dtype)
            dkv_ref[:, cols] += lax.dot_general(ds, qh, _TN, preferred_element_type=F32)
            dkv_ref[:, vcols] += lax.dot_general(p.astype(BF16), doh, _TN, preferred_element_type=F32)

    return pl.pallas_call(
        kern, name="attn_bwd", grid=(t // seq, nq),
        in_specs=[pl.BlockSpec((tq, d), lambda b, i: (b * nq + i, 0)), pl.BlockSpec((mem_len, 2 * d), lambda b, i: (b, 0)),
                  pl.BlockSpec((tq, d), lambda b, i: (b * nq + i, 0))],
        out_specs=[pl.BlockSpec((tq, d), lambda b, i: (b * nq + i, 0)), pl.BlockSpec((mem_len, 2 * d), lambda b, i: (b, 0))],
        out_shape=[jax.ShapeDtypeStruct((t, d), BF16), jax.ShapeDtypeStruct(kv.shape, F32)],
        compiler_params=_PARAMS(dimension_semantics=("parallel", "arbitrary")),
    )(q, kv, do)


def _rstd(x):
    return lax.rsqrt(jnp.mean(x * x, axis=-1, keepdims=True) + RMS_EPS)


def _rms(x, g):
    return x * _rstd(x) * g


def _rms_bwd(dy, x, g):
    rstd = _rstd(x)
    xhat = x * rstd
    dxhat = dy * g
    return rstd * (dxhat - xhat * jnp.mean(dxhat * xhat, axis=-1, keepdims=True)), dy * xhat


def _softplus(x):
    return jnp.maximum(x, 0.0) + jnp.log1p(jnp.exp(-jnp.abs(x)))


def _one_minus_exp(x):
    series = -x * (1.0 + x * (0.5 + x * (1.0 / 6.0 + x * (1.0 / 24.0 + x * (1.0 / 120.0)))))
    return jnp.where(x > -0.05, series, 1.0 - jnp.exp(x))


_GELU_C = 0.7978845608028654
_GELU_K = 0.044715


def _gelu(x):
    return 0.5 * x * (1.0 + jnp.tanh(_GELU_C * (x + _GELU_K * x * x * x)))


def _gelu_grad(x):
    th = jnp.tanh(_GELU_C * (x + _GELU_K * x * x * x))
    return 0.5 * (1.0 + th) + 0.5 * x * (1.0 - th * th) * _GELU_C * (1.0 + 3.0 * _GELU_K * x * x)


def _seg_sum(x, seg):
    def two_terms(v, dims):
        hi = v.astype(BF16)
        lo = (v - hi.astype(F32)).astype(BF16)
        return (lax.dot_general(hi, seg, dims, preferred_element_type=F32)
                + lax.dot_general(lo, seg, dims, preferred_element_type=F32))
    return two_terms(two_terms(x, _NN), _NT)


def _f32(v):
    return v.astype(F32)


def _norm_fwd(name, x, g, dtype=BF16):
    (hn,), _ = _rowwise(name, lambda rv, cv, pv, nv: ([_rms(_f32(rv[0]), cv[0])], []), [x], [g], [(x.shape[1], dtype)])
    return hn


def _resid_norm_fwd(name, x, t, g, bias=None):
    def body(rv, cv, pv, nv):
        tt = rv[1] if bias is None else rv[1] + cv[1]
        return [rv[0] + _rms(tt, cv[0])], []
    (y,), _ = _rowwise(name, body, [x, t], [g] if bias is None else [g, bias], [(x.shape[1], F32)])
    return y


def _resid_norm_bwd(name, dxn, t, g, bias=None):
    def body(rv, cv, pv, nv):
        tt = rv[1] if bias is None else rv[1] + cv[1]
        dt, dg = _rms_bwd(rv[0], tt, cv[0])
        return [dt], [dg, dt]
    d = t.shape[1]
    (dt,), (dg, db) = _rowwise(name, body, [dxn, t], [g] if bias is None else [g, bias], [(d, BF16)], [d, d])
    return dt, dg.sum(0), db.sum(0)


def _prenorm_bwd(name, dxn, dhn, x, g):
    def body(rv, cv, pv, nv):
        dx, dg = _rms_bwd(_f32(rv[1]), rv[2], cv[0])
        return [rv[0] + dx], [dg]
    d = x.shape[1]
    (dx,), (dg,) = _rowwise(name, body, [dxn, dhn, x], [g], [(d, F32)], [d])
    return dx, dg.sum(0)


def _mlp_fwd(tag, x, g_pre, g_post, w_up, w_down):
    hn = _norm_fwd(tag + "_norm", x, g_pre)
    act = _mm(tag + "_up", hn, w_up, out_dtype=BF16, epilogue=lambda up: jnp.square(jnp.maximum(up, 0.0)))
    m = _mm(tag + "_down", act, w_down)
    y = _resid_norm_fwd(tag + "_res", x, m, g_post)
    return y, (x, hn, act, m)


def _mlp_bwd(tag, saved, dy, g_pre, g_post, w_up_t, w_down_t):
    x, hn, act, m = saved
    dm, dg_post, _ = _resid_norm_bwd(tag + "_dres", dy, m, g_post)
    dup = _mm(tag + "_dup", dm, w_down_t, out_dtype=BF16, extra=act,
              epilogue=lambda dact, act_: dact * 2.0 * jnp.sqrt(_f32(act_)))
    dw_down = _mm(tag + "_dwdown", act, dm, trans_a=True)
    dw_up = _mm(tag + "_dwup", hn, dup, trans_a=True)
    dhn = _mm(tag + "_dhn", dup, w_up_t)
    dx, dg_pre = _prenorm_bwd(tag + "_dnorm", dy, dhn, x, g_pre)
    return dx, dict(g_pre=dg_pre, g_post=dg_post, w_up=dw_up, w_down=dw_down)


def _xattn_fwd(tag, x, mem_n, g_pre, g_post, w_q, w_kv, w_o, seq, mem_len):
    hn = _norm_fwd(tag + "_norm", x, g_pre)
    q = _mm(tag + "_q", hn, w_q, out_dtype=BF16)
    kv = _mm(tag + "_kv", mem_n, w_kv, out_dtype=BF16)
    o = _attn_fwd(q, kv, seq, mem_len)
    c = _mm(tag + "_o", o, w_o)
    y = _resid_norm_fwd(tag + "_res", x, c, g_post)
    return y, (x, hn, q, kv, o, c)


def _xattn_bwd(tag, saved, dy, mem_n, g_pre, g_post, w_q_t, w_kv_t, w_o_t, seq, mem_len):
    x, hn, q, kv, o, c = saved
    dc, dg_post, _ = _resid_norm_bwd(tag + "_dres", dy, c, g_post)
    do = _mm(tag + "_do", dc, w_o_t, out_dtype=BF16)
    dw_o = _mm(tag + "_dwo", o, dc, trans_a=True)
    dq, dkv = _attn_bwd(q, kv, do, seq, mem_len)
    dw_q = _mm(tag + "_dwq", hn, dq, trans_a=True)
    dhn = _mm(tag + "_dhn", dq, w_q_t)
    dw_kv = _mm(tag + "_dwkv", mem_n, dkv, trans_a=True)
    dmem_n = _mm(tag + "_dmem", dkv, w_kv_t)
    dx, dg_pre = _prenorm_bwd(tag + "_dnorm", dy, dhn, x, g_pre)
    return dx, dmem_n, dict(g_pre=dg_pre, g_post=dg_post, w_q=dw_q, w_kv=dw_kv, w_o=dw_o)


def _lru_gates(z0, z1, gb0, gb1, sp):
    r = jax.nn.sigmoid(z0 + gb0)
    i = jax.nn.sigmoid(z1 + gb1)
    log_a = -LRU_C * r * sp
    a = jnp.exp(log_a)
    mult = jnp.sqrt(_one_minus_exp(2.0 * log_a))
    return r, i, a, mult


def _rglru_fwd(x, p, seq):
    d = x.shape[1]
    hn = _norm_fwd("a_norm", x, p["g_pre"])
    proj = _mm("a_in", hn, p["w_in"])

    def conv_body(rv, cv, pv, nv):
        u = rv[0] + cv[0]
        halo = pv[0] + cv[0]
        i = pl.program_id(0)
        halo = jnp.where((i * rv[0].shape[0]) % seq == 0, jnp.zeros_like(halo), halo)
        conv = cv[2] + u * cv[1][CONV_WIDTH - 1:CONV_WIDTH]
        for tap in range(CONV_WIDTH - 1):
            conv = conv + _shift_down(u, halo, CONV_WIDTH - 1 - tap) * cv[1][tap:tap + 1]
        return [conv], []
    (conv,), _ = _rowwise("a_conv", conv_body, [(proj, d, 1)], [p["b_in_u"], p["conv_w"], p["conv_b"]], [(d, F32)],
                          prev=[0], seq=seq)
    z = _mm("a_gate", conv, p["w_gate"])

    def gate_body(rv, cv, pv, nv):
        r, i, a, mult = _lru_gates(rv[0], rv[1], cv[0], cv[1], _softplus(-cv[2]))
        return [a, mult * i * rv[2]], []
    (a, bb), _ = _rowwise("a_gates", gate_body, [(z, d, 0), (z, d, 1), conv], [p["gate_b0"], p["gate_b1"], p["lam"]],
                          [(d, F32), (d, F32)])
    h = _scan("a_scan", a, bb, seq)

    def hy_body(rv, cv, pv, nv):
        return [rv[0] * _gelu(rv[1] + cv[0])], []
    (hy,), _ = _rowwise("a_hy", hy_body, [h, (proj, d, 0)], [p["b_in_y"]], [(d, BF16)])
    out = _mm("a_out", hy, p["w_out"])
    y = _resid_norm_fwd("a_res", x, out, p["g_post"], bias=p["b_out"])
    return y, (x, hn, proj, conv, z, a, h, hy, out)


def _rglru_bwd(saved, dy, p, pt, seq):
    x, hn, proj, conv, z, a, h, hy, out = saved
    d = x.shape[1]
    dt, dg_post, db_out = _resid_norm_bwd("a_dres", dy, out, p["g_post"], bias=p["b_out"])
    dhy = _mm("a_dhy", dt, pt["w_out_t"])
    dw_out = _mm("a_dwout", hy, dt, trans_a=True)

    def dh_body(rv, cv, pv, nv):
        yb = rv[2] + cv[0]
        return [rv[0] * _gelu(yb), rv[0] * rv[1] * _gelu_grad(yb)], []
    (dh, dyb), _ = _rowwise("a_dh", dh_body, [dhy, h, (proj, d, 0)], [p["b_in_y"]], [(d, F32), (d, BF16)])
    g = _scan("a_rscan", a, dh, seq, reverse=True)

    def dgate_body(rv, cv, pv, nv):
        gg, hh, z0, z1, cnv = rv
        sp = _softplus(-cv[2])
        r, i, aa, mult = _lru_gates(z0, z1, cv[0], cv[1], sp)
        i_blk = pl.program_id(0)
        halo = jnp.where((i_blk * gg.shape[0]) % seq == 0, jnp.zeros_like(pv[0]), pv[0])
        da = gg * _shift_down(hh, halo, 1)
        dmult = gg * i * cnv
        di = gg * mult * cnv
        dconv = gg * mult * i
        dlog_a = da * aa - dmult * aa * aa / mult
        dz0 = dlog_a * (-LRU_C * sp) * r * (1.0 - r)
        dz1 = di * i * (1.0 - i)
        dsp = dlog_a * (-LRU_C * r)
        dlam = dsp * (-jax.nn.sigmoid(-cv[2]))
        return [jnp.concatenate([dz0, dz1], axis=1), dconv], [dz0, dz1, dlam]
    (dz, dconv1), (dgb0, dgb1, dlam) = _rowwise(
        "a_dgates", dgate_body, [g, h, (z, d, 0), (z, d, 1), conv], [p["gate_b0"], p["gate_b1"], p["lam"]],
        [(2 * d, BF16), (d, F32)], [d, d, d], prev=[1], seq=seq)
    dconv2 = _mm("a_dconv", dz, pt["w_gate_t"])
    dw_gate = _mm("a_dwgate", conv, dz, trans_a=True)

    def dconv_body(rv, cv, pv, nv):
        dc1, dc2, pu, dyb_ = rv
        dc = dc1 + dc2
        dc_next = nv[0] + nv[1]
        u = pu + cv[0]
        i_blk = pl.program_id(0)
        halo = jnp.where((i_blk * u.shape[0]) % seq == 0, jnp.zeros_like(pv[0]), pv[0] + cv[0])
        du = dc * cv[1][CONV_WIDTH - 1:CONV_WIDTH]
        dws = []
        for tap in range(CONV_WIDTH - 1):
            k = CONV_WIDTH - 1 - tap
            du = du + _shift_up(dc, dc_next, k) * cv[1][tap:tap + 1]
            dws.append(dc * _shift_down(u, halo, k))
        dws.append(dc * u)
        return [jnp.concatenate([_f32(dyb_), du], axis=1)], dws + [dc, _f32(dyb_), du]
    (dproj,), accs = _rowwise(
        "a_dconvw", dconv_body, [dconv1, dconv2, (proj, d, 1), dyb], [p["b_in_u"], p["conv_w"]],
        [(2 * d, BF16)], [d] * (CONV_WIDTH + 3), prev=[2], nxt=[0, 1], seq=seq)
    dconv_w = jnp.stack([acc.sum(0) for acc in accs[:CONV_WIDTH]])
    dconv_b = accs[CONV_WIDTH].sum(0)
    db_in = jnp.concatenate([accs[CONV_WIDTH + 1].sum(0), accs[CONV_WIDTH + 2].sum(0)])
    dhn = _mm("a_dhn", dproj, pt["w_in_t"])
    dw_in = _mm("a_dwin", hn, dproj, trans_a=True)
    dx, dg_pre = _prenorm_bwd("a_dnorm", dy, dhn, x, p["g_pre"])
    grads = dict(g_pre=dg_pre, g_post=dg_post, b_out=db_out, w_out=dw_out, gate_b0=dgb0.sum(0), gate_b1=dgb1.sum(0),
                 lam=dlam.sum(0), w_gate=dw_gate, conv_w=dconv_w, conv_b=dconv_b, b_in=db_in, w_in=dw_in)
    return dx, grads


def _rwkv_prep(k, wl, za, w0, a0, k_k, k_a, seg):
    w_in = wl + w0
    e_w = jnp.exp(-_softplus(-w_in) - 0.5)
    a = jax.nn.sigmoid(za + a0)
    q = k * k_k
    norm = jnp.sqrt(_seg_sum(q * q, seg))
    n = jnp.maximum(norm, 1e-12)
    kk = q / n
    return w_in, e_w, a, norm, n, kk


def _rwkv_out(y, r, k2, v, gn_g, gn_b, r_k, seg):
    inv = 1.0 / RWKV_HEAD
    yc = y - _seg_sum(y, seg) * inv
    rstd = lax.rsqrt(_seg_sum(yc * yc, seg) * inv + RWKV_GN_EPS)
    yhat = yc * rstd
    s = _seg_sum(r * k2 * r_k, seg)
    return rstd, yhat, s, yhat * gn_g + gn_b + s * v


def _rwkv_fwd(x, p, seq):
    t, d = x.shape
    nseq = t // seq

    def mix_body(rv, cv, pv, nv):
        hn = _rms(rv[0], cv[0])
        xx = _shift_down(hn, _rms(pv[0], cv[0]), 1) - hn
        return [hn] + [hn + xx * cv[1][c:c + 1] for c in range(6)], []
    (hn, xr, xw, xk, xv, xa, xg), _ = _rowwise("b_mix", mix_body, [x], [p["g_pre"], p["mu"]],
                                               [(d, F32)] + [(d, BF16)] * 6, prev=[0], seq=seq)
    r = _mm("b_r", xr, p["w_r"])
    k = _mm("b_k", xk, p["w_k"])
    v = _mm("b_v", xv, p["w_v"])
    lw = _mm("b_w1", xw, p["w1"])
    la = _mm("b_a1", xa, p["a1"], out_dtype=BF16)
    lg = _mm("b_g1", xg, p["g1"])
    (th,), _ = _rowwise("b_tanh", lambda rv, cv, pv, nv: ([jnp.tanh(rv[0])], []), [lw], [], [(lw.shape[1], BF16)])
    (sg,), _ = _rowwise("b_sig", lambda rv, cv, pv, nv: ([jax.nn.sigmoid(rv[0])], []), [lg], [], [(lg.shape[1], BF16)])
    wl = _mm("b_w2", th, p["w2"])
    za = _mm("b_a2", la, p["a2"])
    g = _mm("b_g2", sg, p["g2"])

    def prep_body(rv, cv, pv, nv):
        kk_, wl_, za_ = rv
        _, e_w, a, _, _, kk = _rwkv_prep(kk_, wl_, za_, cv[0], cv[1], cv[2], cv[3], cv[4])
        return [-e_w, kk_ * (1.0 + (a - 1.0) * cv[3]), -kk, kk * a], []
    (log_w, k2, rem_a, rem_b), _ = _rowwise("b_prep", prep_body, [k, wl, za],
                                            [p["w0"], p["a0"], p["k_k"], p["k_a"], p["seg"]], [(d, F32)] * 4, tm=256)
    rec_in = (r, log_w, k2, v, rem_a, rem_b)
    y, *states = _wkv_fwd(*rec_in, seq)

    def out_body(rv, cv, pv, nv):
        y_, r_, k2_, v_, g_ = rv
        _, _, _, out = _rwkv_out(y_, r_, k2_, v_, cv[0], cv[1], cv[2], cv[3])
        return [out * g_], []
    (og,), _ = _rowwise("b_out", out_body, [y, r, k2, v, g], [p["gn_g"], p["gn_b"], p["r_k"], p["seg"]], [(d, BF16)], tm=256)
    o = _mm("b_o", og, p["w_o"])
    res = _resid_norm_fwd("b_res", x, o, p["g_post"])
    return res, (x, hn, xr, xw, xk, xv, xa, xg, r, k, v, th, la, sg, wl, za, g, k2, rec_in, states, y, og, o)


def _rwkv_bwd(saved, dres, p, pt, seq):
    x, hn, xr, xw, xk, xv, xa, xg, r, k, v, th, la, sg, wl, za, g, k2, rec_in, states, y, og, o = saved
    t, d = x.shape
    nseq = t // seq
    do, dg_post, _ = _resid_norm_bwd("b_dres", dres, o, p["g_post"])
    dog = _mm("b_dog", do, pt["w_o_t"])
    dw_o = _mm("b_dwo", og, do, trans_a=True)

    def dout_body(rv, cv, pv, nv):
        dog_, y_, r_, k2_, v_, g_ = rv
        gn_g, gn_b, r_k, bd = cv
        inv = 1.0 / RWKV_HEAD
        rstd, yhat, s, out = _rwkv_out(y_, r_, k2_, v_, gn_g, gn_b, r_k, bd)
        dout = dog_ * g_
        ds = _seg_sum(dout * v_, bd)
        dyhat = dout * gn_g
        dy = rstd * (dyhat - _seg_sum(dyhat, bd) * inv - yhat * _seg_sum(dyhat * yhat, bd) * inv)
        return [dy, dog_ * out, dout * s, ds * k2_ * r_k, ds * r_ * r_k], [ds * r_ * k2_, dout * yhat, dout]
    (dy, dgate, dv_b, dr_b, dk2_b), (dr_k, dgn_g, dgn_b) = _rowwise(
        "b_dout", dout_body, [dog, y, r, k2, v, g], [p["gn_g"], p["gn_b"], p["r_k"], p["seg"]],
        [(d, F32), (d, BF16), (d, F32), (d, F32), (d, F32)], [d, d, d], tm=256)
    dr_rec, dlw_rec, dk2_rec, dv_rec, da_rec, db_rec = _wkv_bwd(*rec_in, *states, dy, seq)

    def dprep_body(rv, cv, pv, nv):
        dr_rec_, dlw_rec_, dk2_rec_, dv_rec_, da_rec_, db_rec_, dr_b_, dk2_b_, dv_b_, k_, wl_, za_ = rv
        w0, a0, k_k, k_a, bd = cv
        w_in, e_w, a, norm, n, kk = _rwkv_prep(k_, wl_, za_, w0, a0, k_k, k_a, bd)
        dk2 = dk2_rec_ + dk2_b_
        dkk = db_rec_ * a - da_rec_
        da = db_rec_ * kk + dk2 * k_ * k_a
        dq = jnp.where(norm > 1e-12, dkk - kk * _seg_sum(kk * dkk, bd), dkk) / n
        dk = dk2 * (1.0 + (a - 1.0) * k_a) + dq * k_k
        dza = da * a * (1.0 - a)
        dwl = dlw_rec_ * (-e_w) * jax.nn.sigmoid(-w_in)
        return [dr_rec_ + dr_b_, dk, dv_rec_ + dv_b_, dza, dwl], [dk2 * k_ * (a - 1.0), dq * k_, dza, dwl]
    (dr, dk, dv, dza, dwl), (dk_a, dk_k, da0, dw0) = _rowwise(
        "b_dprep", dprep_body, [dr_rec, dlw_rec, dk2_rec, dv_rec, da_rec, db_rec, dr_b, dk2_b, dv_b, k, wl, za],
        [p["w0"], p["a0"], p["k_k"], p["k_a"], p["seg"]], [(d, BF16)] * 5, [d] * 4, tm=256)

    dw_r = _mm("b_dwr", xr, dr, trans_a=True)
    dw_k = _mm("b_dwk", xk, dk, trans_a=True)
    dw_v = _mm("b_dwv", xv, dv, trans_a=True)
    dxr = _mm("b_dxr", dr, pt["w_r_t"])
    dxk = _mm("b_dxk", dk, pt["w_k_t"])
    dxv = _mm("b_dxv", dv, pt["w_v_t"])
    da2 = _mm("b_da2", la, dza, trans_a=True)
    dla = _mm("b_dla", dza, pt["a2_t"], out_dtype=BF16)
    da1 = _mm("b_da1", xa, dla, trans_a=True)
    dxa = _mm("b_dxa", dla, pt["a1_t"])
    dw2 = _mm("b_dw2", th, dwl, trans_a=True)
    dth = _mm("b_dth", dwl, pt["w2_t"])
    (dzw,), _ = _rowwise("b_dtanh", lambda rv, cv, pv, nv: ([rv[0] * (1.0 - _f32(rv[1]) * _f32(rv[1]))], []),
                         [dth, th], [], [(th.shape[1], BF16)])
    dw1 = _mm("b_dw1", xw, dzw, trans_a=True)
    dxw = _mm("b_dxw", dzw, pt["w1_t"])
    dg2 = _mm("b_dg2", sg, dgate, trans_a=True)
    dsg = _mm("b_dsg", dgate, pt["g2_t"])
    (dzg,), _ = _rowwise("b_dsig", lambda rv, cv, pv, nv: ([rv[0] * _f32(rv[1]) * (1.0 - _f32(rv[1]))], []),
                         [dsg, sg], [], [(sg.shape[1], BF16)])
    dg1 = _mm("b_dg1", xg, dzg, trans_a=True)
    dxg = _mm("b_dxg", dzg, pt["g1_t"])

    def dmix_body(rv, cv, pv, nv):
        hn_ = rv[0]
        dxs = rv[1:]
        mu = cv[0]
        xx = _shift_down(hn_, pv[0], 1) - hn_
        dsum = dxs[0]
        dxx = dxs[0] * mu[0:1]
        dxx_next = nv[0] * mu[0:1]
        for c in range(1, 6):
            dsum = dsum + dxs[c]
            dxx = dxx + dxs[c] * mu[c:c + 1]
            dxx_next = dxx_next + nv[c] * mu[c:c + 1]
        return [dsum - dxx + _shift_up(dxx, dxx_next, 1)], [dxs[c] * xx for c in range(6)]
    (dhn,), dmu = _rowwise("b_dmix", dmix_body, [hn, dxr, dxw, dxk, dxv, dxa, dxg], [p["mu"]], [(d, F32)], [d] * 6,
                           prev=[0], nxt=[1, 2, 3, 4, 5, 6], seq=seq, tm=256)
    dx, dg_pre = _prenorm_bwd("b_dnorm", dres, dhn, x, p["g_pre"])
    grads = dict(g_pre=dg_pre, g_post=dg_post, mu=jnp.stack([m.sum(0) for m in dmu]), w_r=dw_r, w_k=dw_k, w_v=dw_v,
                 w0=dw0.sum(0), w1=dw1, w2=dw2, a0=da0.sum(0), a1=da1, a2=da2, g1=dg1, g2=dg2, k_k=dk_k.sum(0),
                 k_a=dk_a.sum(0), r_k=dr_k.sum(0), gn_g=dgn_g.sum(0), gn_b=dgn_b.sum(0), w_o=dw_o)
    return dx, grads


_WEIGHTS = ['ln_gains', 'mem_norm', 'a_conv_w', 'a_conv_b', 'a_w_in', 'a_b_in', 'a_gate_w', 'a_gate_b', 'a_lambda', 'a_w_out',
            'a_b_out', 'b_mu', 'b_w_rkv', 'b_w0', 'b_w1', 'b_w2', 'b_a0', 'b_a1', 'b_a2', 'b_g1', 'b_g2', 'b_k_k', 'b_k_a',
            'b_r_k', 'b_gn_g', 'b_gn_b', 'b_w_o', 'c_w_q', 'c_w_kv', 'c_w_o', 'm_w_up', 'm_w_down']
_SHARD_AXIS = dict(ln_gains=2, mem_norm=None, a_conv_w=2, a_conv_b=None, a_w_in=2, a_b_in=None, a_gate_w=3, a_gate_b=3,
                   a_lambda=None, a_w_out=1, a_b_out=None, b_mu=2, b_w_rkv=2, b_w0=1, b_w1=1, b_w2=2, b_a0=1, b_a1=1, b_a2=2,
                   b_g1=1, b_g2=2, b_k_k=1, b_k_a=1, b_r_k=None, b_gn_g=1, b_gn_b=1, b_w_o=1, c_w_q=1, c_w_kv=2, c_w_o=1,
                   m_w_up=2, m_w_down=1)
_MATRICES = ['a_w_in', 'a_gate_w', 'a_w_out', 'b_w_rkv', 'b_w1', 'b_w2', 'b_a1', 'b_a2', 'b_g1', 'b_g2', 'b_w_o', 'c_w_q',
             'c_w_kv', 'c_w_o', 'm_w_up', 'm_w_down']
_SHARDED = [n for n in _WEIGHTS if _SHARD_AXIS[n] is not None]
_VECTORS = [n for n in _SHARDED if n not in _MATRICES]
_REPLICATED = [n for n in _WEIGHTS if _SHARD_AXIS[n] is None]
N_XY = 4
N_DEV = 8
PACK_W = 1024
PACK_ROWS = 256


def _pack(arrs, dtype, row_mult=PACK_ROWS):
    parts = []
    rows = 0
    for a in arrs:
        n = a.size
        r = -(-n // PACK_W)
        parts.append(jnp.pad(a.reshape(-1).astype(dtype), (0, r * PACK_W - n)))
        rows += r
    pad_rows = -(-rows // row_mult) * row_mult - rows
    if pad_rows:
        parts.append(jnp.zeros((pad_rows * PACK_W,), dtype))
    return jnp.concatenate(parts).reshape(-1, PACK_W)


def _unpack(flat, shapes):
    out = []
    row = 0
    for shp in shapes:
        n = 1
        for s in shp:
            n *= s
        r = -(-n // PACK_W)
        out.append(flat[row:row + r].reshape(-1)[:n].reshape(shp))
        row += r
    return out


_ANY = pl.BlockSpec(memory_space=pl.ANY)


def _xy_peers():
    x, y = lax.axis_index("x"), lax.axis_index("y")
    return [(1 - x, y), (x, 1 - y), (1 - x, 1 - y)]


def _all_gather_xy(wm, wv):
    half = wm.shape[0] // 2

    def body(wm_ref, wv_ref, gm_ref, gv_ref, send_sems, recv_sems, local_sems):
        x, y, c = lax.axis_index("x"), lax.axis_index("y"), lax.axis_index("c")
        me = 2 * x + y
        mine = pl.ds(pl.multiple_of(c * half, SUBLANES), half)
        other = pl.ds(pl.multiple_of((1 - c) * half, SUBLANES), half)
        local = [pltpu.make_async_copy(wm_ref, gm_ref.at[me], local_sems.at[0]),
                 pltpu.make_async_copy(wv_ref, gv_ref.at[me], local_sems.at[1])]
        for cp in local:
            cp.start()
        sends, lands, passes, from_sibling = [], [], [], []
        for j, (px, py) in enumerate(_xy_peers()):
            peer = 2 * px + py
            ici = functools.partial(pltpu.make_async_remote_copy, device_id=(px, py, c), device_id_type=MESH)
            sends.append(ici(src_ref=wm_ref.at[mine], dst_ref=gm_ref.at[me, mine], send_sem=send_sems.at[j], recv_sem=recv_sems.at[j]))
            lands.append(ici(src_ref=wm_ref.at[mine], dst_ref=gm_ref.at[peer, mine], send_sem=send_sems.at[j], recv_sem=recv_sems.at[j]))
            sends.append(ici(src_ref=wv_ref, dst_ref=gv_ref.at[me], send_sem=send_sems.at[3 + j], recv_sem=recv_sems.at[3 + j]))
            lands.append(ici(src_ref=wv_ref, dst_ref=gv_ref.at[peer], send_sem=send_sems.at[3 + j], recv_sem=recv_sems.at[3 + j]))
            d2d = functools.partial(pltpu.make_async_remote_copy, send_sem=send_sems.at[6 + j], recv_sem=recv_sems.at[6 + j],
                                    device_id=(x, y, 1 - c), device_id_type=MESH)
            passes.append(d2d(src_ref=gm_ref.at[peer, mine], dst_ref=gm_ref.at[peer, mine]))
            from_sibling.append(d2d(src_ref=gm_ref.at[peer, other], dst_ref=gm_ref.at[peer, other]))
        for cp in sends:
            cp.start()
        for j in range(N_XY - 1):
            lands[2 * j].wait_recv()
            passes[j].start()
        for j in range(N_XY - 1):
            lands[2 * j + 1].wait_recv()
        for cp in from_sibling:
            cp.wait_recv()
        for cp in sends + passes:
            cp.wait_send()
        for cp in local:
            cp.wait()

    return pl.pallas_call(
        body, name="all_gather_weights",
        in_specs=[_ANY, _ANY], out_specs=[_ANY, _ANY],
        out_shape=[jax.ShapeDtypeStruct((N_XY,) + wm.shape, wm.dtype), jax.ShapeDtypeStruct((N_XY,) + wv.shape, wv.dtype)],
        scratch_shapes=[pltpu.SemaphoreType.DMA((9,)), pltpu.SemaphoreType.DMA((9,)), pltpu.SemaphoreType.DMA((2,))],
    )(wm, wv)


def _exchange_xy(gsend):
    def body(gs_ref, recv_ref, send_sems, recv_sems):
        c = lax.axis_index("c")
        sends = []
        for j, (px, py) in enumerate(_xy_peers()):
            sends.append(pltpu.make_async_remote_copy(
                src_ref=gs_ref.at[2 * px + py], dst_ref=recv_ref.at[j], send_sem=send_sems.at[j], recv_sem=recv_sems.at[j],
                device_id=(px, py, c), device_id_type=MESH))
        for cp in sends:
            cp.start()
        for cp in sends:
            cp.wait_recv()
        for cp in sends:
            cp.wait_send()

    return pl.pallas_call(
        body, name="exchange_grads",
        in_specs=[_ANY], out_specs=_ANY,
        out_shape=jax.ShapeDtypeStruct((N_XY - 1,) + gsend.shape[1:], gsend.dtype),
        scratch_shapes=[pltpu.SemaphoreType.DMA((3,)), pltpu.SemaphoreType.DMA((3,))],
    )(gsend)


def _swap_with_sibling(part):
    def body(p_ref, got_ref, send_sem, recv_sem):
        x, y, c = lax.axis_index("x"), lax.axis_index("y"), lax.axis_index("c")
        cp = pltpu.make_async_remote_copy(src_ref=p_ref, dst_ref=got_ref, send_sem=send_sem, recv_sem=recv_sem,
                                          device_id=(x, y, 1 - c), device_id_type=MESH)
        cp.start()
        cp.wait_recv()
        cp.wait_send()

    return pl.pallas_call(
        body, name="swap_sibling",
        in_specs=[_ANY], out_specs=_ANY, out_shape=jax.ShapeDtypeStruct(part.shape, part.dtype),
        scratch_shapes=[pltpu.SemaphoreType.DMA, pltpu.SemaphoreType.DMA],
    )(part)


def _all_gather_all(vec):
    def body(v_ref, out_ref, send_sems, recv_sems, local_sem):
        x, y, c = lax.axis_index("x"), lax.axis_index("y"), lax.axis_index("c")
        me = 4 * x + 2 * y + c
        local = pltpu.make_async_copy(v_ref, out_ref.at[me], local_sem)
        local.start()
        sends, recvs = [], []
        for f in range(1, N_DEV):
            fx, fy, fc = (f >> 2) & 1, (f >> 1) & 1, f & 1
            px = (1 - x) if fx else x
            py = (1 - y) if fy else y
            pc = (1 - c) if fc else c
            mk = functools.partial(pltpu.make_async_remote_copy, src_ref=v_ref, send_sem=send_sems.at[f - 1],
                                   recv_sem=recv_sems.at[f - 1], device_id=(px, py, pc), device_id_type=MESH)
            sends.append(mk(dst_ref=out_ref.at[me]))
            recvs.append(mk(dst_ref=out_ref.at[4 * px + 2 * py + pc]))
        for cp in sends:
            cp.start()
        for cp in recvs:
            cp.wait_recv()
        for cp in sends:
            cp.wait_send()
        local.wait()

    return pl.pallas_call(
        body, name="all_gather_replicated",
        in_specs=[_ANY], out_specs=_ANY, out_shape=jax.ShapeDtypeStruct((N_DEV,) + vec.shape, vec.dtype),
        scratch_shapes=[pltpu.SemaphoreType.DMA((N_DEV - 1,)), pltpu.SemaphoreType.DMA((N_DEV - 1,)), pltpu.SemaphoreType.DMA],
    )(vec)


def _adamw(g, w, m, v):
    m2 = ADAM_B1 * m + (1.0 - ADAM_B1) * g
    v2 = ADAM_B2 * v + (1.0 - ADAM_B2) * g * g
    m_hat = m2 / (1.0 - ADAM_B1 ** ADAM_STEP)
    v_hat = v2 / (1.0 - ADAM_B2 ** ADAM_STEP)
    return -ADAM_LR * (m_hat / (jnp.sqrt(v_hat) + ADAM_EPS) + ADAM_WD * w), m2, v2


def _sum_contributions(own, recv):
    def body(rv, cv, pv, nv):
        return [((rv[0] + _f32(rv[1])) + _f32(rv[2])) + _f32(rv[3])], []
    stacked = recv.reshape(-1, PACK_W)
    (part,), _ = _rowwise("sum_grads", body, [own] + [(stacked, PACK_W, 0, j * own.shape[0]) for j in range(N_XY - 1)], [],
                          [(PACK_W, F32)])
    return part


def _adamw_sharded(name, part, sib, first, w, m, v):
    def body(rv, cv, pv, nv):
        g = rv[3] + rv[4]
        return [g, *_adamw(g, rv[0], rv[1], rv[2])], []
    outs, _ = _rowwise(name, body, [w, m, v, (part, PACK_W, 0, first), (sib, PACK_W, 0, first)], [], [(PACK_W, F32)] * 4, tm=256)
    return outs


def _adamw_replicated(parts, w, m, v):
    def body(rv, cv, pv, nv):
        g = rv[0]
        for i in range(1, N_DEV):
            g = g + rv[i]
        return [g, *_adamw(g, rv[N_DEV], rv[N_DEV + 1], rv[N_DEV + 2])], []
    outs, _ = _rowwise("adamw_replicated", body, [parts[i] for i in range(N_DEV)] + [w, m, v], [], [(PACK_W, F32)] * 4)
    return outs


def _row(v):
    return v.reshape(1, -1).astype(F32)


def _gate_dense(gate_w):
    _, nh, blk, _ = gate_w.shape
    d = nh * blk
    dense = jnp.zeros((d, 2 * d), gate_w.dtype)
    for g in range(2):
        for h in range(nh):
            dense = lax.dynamic_update_slice(dense, gate_w[g, h], (h * blk, g * d + h * blk))
    return dense


def _gate_blocks(dense, nh):
    d = dense.shape[0]
    blk = d // nh
    return jnp.stack([jnp.stack([dense[h * blk:(h + 1) * blk, g * d + h * blk:g * d + (h + 1) * blk] for h in range(nh)])
                      for g in range(2)])


def _local_step(x3, mem3, target3, fw):
    nseq, seq, d = x3.shape
    mem_len = mem3.shape[1]
    t = nseq * seq
    x0 = x3.reshape(t, d)
    mem2 = mem3.reshape(nseq * mem_len, d)
    target = target3.reshape(t, d)
    ln = fw["ln_gains"]
    gains = [[_row(ln[i, j]) for j in range(6)] for i in range(2)]
    nh = d // RWKV_HEAD
    seg = (jnp.arange(d)[:, None] // RWKV_HEAD == jnp.arange(LANES)[None, :]).astype(BF16)

    w_gate = _gate_dense(fw["a_gate_w"][0])
    pa = dict(g_pre=gains[0][0], g_post=gains[0][1], w_in=fw["a_w_in"][0], b_in_y=_row(fw["a_b_in"][0, :d]),
              b_in_u=_row(fw["a_b_in"][0, d:]), conv_w=fw["a_conv_w"][0].astype(F32), conv_b=_row(fw["a_conv_b"][0]),
              w_gate=w_gate, gate_b0=_row(fw["a_gate_b"][0, 0]), gate_b1=_row(fw["a_gate_b"][0, 1]), lam=_row(fw["a_lambda"][0]),
              w_out=fw["a_w_out"][0], b_out=_row(fw["a_b_out"][0]))
    pta = dict(w_in_t=_t(pa["w_in"]), w_gate_t=_t(w_gate), w_out_t=_t(pa["w_out"]))
    pb = dict(g_pre=gains[1][0], g_post=gains[1][1], mu=fw["b_mu"][0].astype(F32), w_r=fw["b_w_rkv"][0, 0],
              w_k=fw["b_w_rkv"][0, 1], w_v=fw["b_w_rkv"][0, 2], w0=_row(fw["b_w0"][0]), w1=fw["b_w1"][0], w2=fw["b_w2"][0],
              a0=_row(fw["b_a0"][0]), a1=fw["b_a1"][0], a2=fw["b_a2"][0], g1=fw["b_g1"][0], g2=fw["b_g2"][0],
              k_k=_row(fw["b_k_k"][0]), k_a=_row(fw["b_k_a"][0]), r_k=_row(fw["b_r_k"][0]), gn_g=_row(fw["b_gn_g"][0]),
              gn_b=_row(fw["b_gn_b"][0]), w_o=fw["b_w_o"][0], seg=seg)
    ptb = {k + "_t": _t(pb[k]) for k in ("w_r", "w_k", "w_v", "w_o", "w1", "w2", "a1", "a2", "g1", "g2")}
    mem_g = _row(fw["mem_norm"])

    mem_n = _norm_fwd("mem_norm", mem2, mem_g)
    x1, sv_a = _rglru_fwd(x0, pa, seq)
    x2, sv_c0 = _xattn_fwd("c0", x1, mem_n, gains[0][2], gains[0][3], fw["c_w_q"][0], fw["c_w_kv"][0], fw["c_w_o"][0], seq, mem_len)
    x3_, sv_m0 = _mlp_fwd("m0", x2, gains[0][4], gains[0][5], fw["m_w_up"][0], fw["m_w_down"][0])
    x4, sv_b = _rwkv_fwd(x3_, pb, seq)
    x5, sv_c1 = _xattn_fwd("c1", x4, mem_n, gains[1][2], gains[1][3], fw["c_w_q"][1], fw["c_w_kv"][1], fw["c_w_o"][1], seq, mem_len)
    x6, sv_m1 = _mlp_fwd("m1", x5, gains[1][4], gains[1][5], fw["m_w_up"][1], fw["m_w_down"][1])

    def loss_body(rv, cv, pv, nv):
        err = rv[0] - rv[1]
        return [err * (1.0 / d)], [err * err]
    (dx,), (sq,) = _rowwise("loss", loss_body, [x6, target], [], [(d, F32)], [d])
    loss_part = 0.5 / d * jnp.sum(sq)

    dx, g_m1 = _mlp_bwd("m1", sv_m1, dx, gains[1][4], gains[1][5], _t(fw["m_w_up"][1]), _t(fw["m_w_down"][1]))
    dx, dmem1, g_c1 = _xattn_bwd("c1", sv_c1, dx, mem_n, gains[1][2], gains[1][3], _t(fw["c_w_q"][1]), _t(fw["c_w_kv"][1]),
                                 _t(fw["c_w_o"][1]), seq, mem_len)
    dx, g_b = _rwkv_bwd(sv_b, dx, pb, ptb, seq)
    dx, g_m0 = _mlp_bwd("m0", sv_m0, dx, gains[0][4], gains[0][5], _t(fw["m_w_up"][0]), _t(fw["m_w_down"][0]))
    dx, dmem0, g_c0 = _xattn_bwd("c0", sv_c0, dx, mem_n, gains[0][2], gains[0][3], _t(fw["c_w_q"][0]), _t(fw["c_w_kv"][0]),
                                 _t(fw["c_w_o"][0]), seq, mem_len)
    dx, g_a = _rglru_bwd(sv_a, dx, pa, pta, seq)

    def dmem_body(rv, cv, pv, nv):
        _, dg = _rms_bwd(rv[1] + rv[2], rv[0], cv[0])
        return [], [dg]
    _, (dmem_g,) = _rowwise("mem_norm_grad", dmem_body, [mem2, dmem0, dmem1], [mem_g], [], [d])

    lru_heads = fw["a_gate_w"].shape[2]
    blk = d // lru_heads
    grads = dict(
        ln_gains=jnp.stack([jnp.stack([g_a["g_pre"], g_a["g_post"], g_c0["g_pre"], g_c0["g_post"], g_m0["g_pre"], g_m0["g_post"]]),
                            jnp.stack([g_b["g_pre"], g_b["g_post"], g_c1["g_pre"], g_c1["g_post"], g_m1["g_pre"], g_m1["g_post"]])]),
        mem_norm=dmem_g.sum(0),
        a_conv_w=g_a["conv_w"][None], a_conv_b=g_a["conv_b"][None], a_w_in=g_a["w_in"][None], a_b_in=g_a["b_in"][None],
        a_gate_w=_gate_blocks(g_a["w_gate"], lru_heads)[None],
        a_gate_b=jnp.stack([g_a["gate_b0"], g_a["gate_b1"]]).reshape(1, 2, lru_heads, blk),
        a_lambda=g_a["lam"][None], a_w_out=g_a["w_out"][None], a_b_out=g_a["b_out"][None],
        b_mu=g_b["mu"][None], b_w_rkv=jnp.stack([g_b["w_r"], g_b["w_k"], g_b["w_v"]])[None], b_w0=g_b["w0"][None],
        b_w1=g_b["w1"][None], b_w2=g_b["w2"][None], b_a0=g_b["a0"][None], b_a1=g_b["a1"][None], b_a2=g_b["a2"][None],
        b_g1=g_b["g1"][None], b_g2=g_b["g2"][None], b_k_k=g_b["k_k"][None], b_k_a=g_b["k_a"][None],
        b_r_k=g_b["r_k"].reshape(1, nh, RWKV_HEAD), b_gn_g=g_b["gn_g"][None], b_gn_b=g_b["gn_b"][None], b_w_o=g_b["w_o"][None],
        c_w_q=[g_c0["w_q"], g_c1["w_q"]], c_w_kv=[g_c0["w_kv"], g_c1["w_kv"]], c_w_o=[g_c0["w_o"], g_c1["w_o"]],
        m_w_up=[g_m0["w_up"], g_m1["w_up"]], m_w_down=[g_m0["w_down"], g_m1["w_down"]],
    )
    return loss_part, dx.reshape(nseq, seq, d), grads


def kernel(x, mem, ln_gains, mem_norm, a_conv_w, a_conv_b, a_w_in, a_b_in, a_gate_w, a_gate_b, a_lambda, a_w_out, a_b_out, b_mu, b_w_rkv, b_w0, b_w1, b_w2, b_a0, b_a1, b_a2, b_g1, b_g2, b_k_k, b_k_a, b_r_k, b_gn_g, b_gn_b, b_w_o, c_w_q, c_w_kv, c_w_o, m_w_up, m_w_down, loss_target, m_ln_gains, m_mem_norm, m_a_conv_w, m_a_conv_b, m_a_w_in, m_a_b_in, m_a_gate_w, m_a_gate_b, m_a_lambda, m_a_w_out, m_a_b_out, m_b_mu, m_b_w_rkv, m_b_w0, m_b_w1, m_b_w2, m_b_a0, m_b_a1, m_b_a2, m_b_g1, m_b_g2, m_b_k_k, m_b_k_a, m_b_r_k, m_b_gn_g, m_b_gn_b, m_b_w_o, m_c_w_q, m_c_w_kv, m_c_w_o, m_m_w_up, m_m_w_down, v_ln_gains, v_mem_norm, v_a_conv_w, v_a_conv_b, v_a_w_in, v_a_b_in, v_a_gate_w, v_a_gate_b, v_a_lambda, v_a_w_out, v_a_b_out, v_b_mu, v_b_w_rkv, v_b_w0, v_b_w1, v_b_w2, v_b_a0, v_b_a1, v_b_a2, v_b_g1, v_b_g2, v_b_k_k, v_b_k_a, v_b_r_k, v_b_gn_g, v_b_gn_b, v_b_w_o, v_c_w_q, v_c_w_kv, v_c_w_o, v_m_w_up, v_m_w_down):
    given = dict(locals())
    w = {n: given[n] for n in _WEIGHTS}
    mom1 = {n: given["m_" + n] for n in _WEIGHTS}
    mom2 = {n: given["v_" + n] for n in _WEIGHTS}

    gm, gv = _all_gather_xy(_pack([w[n] for n in _MATRICES], BF16), _pack([w[n] for n in _VECTORS], F32, SUBLANES))
    fw = {n: w[n] for n in _REPLICATED}
    for names, buf in ((_MATRICES, gm), (_VECTORS, gv)):
        shards = [_unpack(buf[s], [w[n].shape for n in names]) for s in range(N_XY)]
        for i, n in enumerate(names):
            fw[n] = jnp.concatenate([shards[s][i] for s in range(N_XY)], axis=_SHARD_AXIS[n])

    loss_part, grad_x, grads = _local_step(x, mem, loss_target, fw)

    tile_rows = 256
    in_place = [n for n in _SHARDED if w[n].shape[-1] == PACK_W and w[n].size % (PACK_W * tile_rows) == 0]
    in_place.sort(key=lambda n: -w[n].size)
    packed = [n for n in _SHARDED if n not in in_place]
    order = in_place + packed

    def shards_of(n, s):
        ax = _SHARD_AXIS[n]
        size = w[n].shape[ax]
        if isinstance(grads[n], list):
            assert all(g.size // N_XY % PACK_W == 0 for g in grads[n])
            return [lax.dynamic_slice_in_dim(g, s * size, size, axis=ax - 1) for g in grads[n]]
        return [lax.dynamic_slice_in_dim(grads[n], s * size, size, axis=ax)]
    me = 2 * lax.axis_index("x") + lax.axis_index("y")
    gsend = jnp.stack([_pack([g for n in order for g in shards_of(n, s)], BF16) for s in range(N_XY)])
    own = _pack([g for n in order for g in shards_of(n, me)], F32)
    part = _sum_contributions(own, _exchange_xy(gsend))
    sib = _swap_with_sibling(part)
    out_by_name = {}
    first = 0
    for n in in_place:
        flat = [src[n].reshape(-1, PACK_W) for src in (w, mom1, mom2)]
        out_by_name[n] = [o.reshape(w[n].shape) for o in _adamw_sharded("adamw_" + n, part, sib, first, *flat)]
        first += flat[0].shape[0]
    flat = [_pack([src[n] for n in packed], F32) for src in (w, mom1, mom2)]
    assert first + flat[0].shape[0] == part.shape[0]
    tail = [_unpack(o, [w[n].shape for n in packed]) for o in _adamw_sharded("adamw_packed", part, sib, first, *flat)]
    for i, n in enumerate(packed):
        out_by_name[n] = [tail[kind][i] for kind in range(4)]
    sharded_out = [[out_by_name[n][kind] for n in _SHARDED] for kind in range(4)]

    small = _pack([grads[n] for n in _REPLICATED] + [loss_part.reshape(1)], F32, SUBLANES)
    parts = _all_gather_all(small)
    zero = jnp.zeros((1,), F32)
    flat = [_pack([src[n] for n in _REPLICATED] + [zero], F32, SUBLANES) for src in (w, mom1, mom2)]
    repl_out = [_unpack(o, [w[n].shape for n in _REPLICATED] + [(1,)]) for o in _adamw_replicated(parts, *flat)]
    loss = repl_out[0][-1][0]

    result = [loss, grad_x]
    for kind in range(4):
        by_name = dict(zip(_SHARDED, sharded_out[kind])) | dict(zip(_REPLICATED, repl_out[kind][:-1]))
        result += [by_name[n] for n in _WEIGHTS]
    return tuple(result)
```

```python
import functools

import jax
import jax.numpy as jnp
from jax import lax
from jax.experimental import pallas as pl
from jax.experimental.pallas import tpu as pltpu

F32 = jnp.float32
BF16 = jnp.bfloat16
MESH = pl.DeviceIdType.MESH

LANES = 128
SUBLANES = 8
VMEM_LIMIT_BYTES = 48 * 1024 * 1024

RMS_EPS = 1e-6
LRU_C = 8.0
LRU_HEADS = 4
CONV_WIDTH = 4
RWKV_HEAD = 64
RWKV_GN_EPS = 64e-5
MEM_HEADS = 4
ADAM_LR = 0.001
ADAM_B1 = 0.9
ADAM_B2 = 0.999
ADAM_EPS = 1e-08
ADAM_WD = 0.01
ADAM_STEP = 10
WKV_CHUNK = 64

_PARAMS = functools.partial(pltpu.CompilerParams, vmem_limit_bytes=VMEM_LIMIT_BYTES)


def _tile(n, want):
    if n <= want:
        return n
    t = want
    while t >= SUBLANES:
        if n % t == 0 and t % SUBLANES == 0:
            return t
        t -= SUBLANES
    return n


def _fold8(v):
    tm, d = v.shape
    if tm == SUBLANES:
        return v
    return jnp.sum(v.reshape(tm // SUBLANES, SUBLANES, d), axis=0)


def _rowwise(name, body, rows, consts=(), out_rows=(), out_accs=(), prev=(), nxt=(), tm=512, seq=None):
    rows = [r if isinstance(r, tuple) else (r, r.shape[1], 0) for r in rows]
    rows = [r if len(r) == 4 else r + (0,) for r in rows]
    t = rows[0][0].shape[0]
    tm = _tile(t, tm)
    if seq is not None:
        tm = _tile(seq, tm)
    nblk = t // tm
    nrow, ncst, nprev, nnxt = len(rows), len(consts), len(prev), len(nxt)
    nor, noa = len(out_rows), len(out_accs)
    hb = tm // SUBLANES

    def kern(*refs):
        i = pl.program_id(0)
        rv = [r[...] for r in refs[:nrow]]
        cv = [c[...] for c in refs[nrow:nrow + ncst]]
        o = nrow + ncst
        pv = []
        for j in range(nprev):
            at_start = (i * tm) % seq == 0
            h = refs[o + j][...]
            pv.append(jnp.where(at_start, jnp.zeros_like(h), h))
        o += nprev
        nv = []
        for j in range(nnxt):
            at_end = ((i + 1) * tm) % seq == 0
            h = refs[o + j][...]
            nv.append(jnp.where(at_end, jnp.zeros_like(h), h))
        o += nnxt
        outs, accs = body(rv, cv, pv, nv)
        for j in range(nor):
            refs[o + j][...] = outs[j].astype(refs[o + j].dtype)
        o += nor
        if noa:
            @pl.when(i == 0)
            def _():
                for j in range(noa):
                    refs[o + j][...] = jnp.zeros_like(refs[o + j])
            for j in range(noa):
                refs[o + j][...] += _fold8(accs[j].astype(F32))

    assert all(first % tm == 0 for (_, _, _, first) in rows), name
    in_specs = [pl.BlockSpec((tm, w), functools.partial(lambda i, c, o: (i + o, c), c=cb, o=first // tm))
                for (_, w, cb, first) in rows]
    in_specs += [pl.BlockSpec(c.shape, lambda i: (0, 0)) for c in consts]
    in_specs += [pl.BlockSpec((SUBLANES, rows[j][1]),
                              functools.partial(lambda i, c: (jnp.maximum(i * hb - 1, 0), c), c=rows[j][2])) for j in prev]
    in_specs += [pl.BlockSpec((SUBLANES, rows[j][1]),
                              functools.partial(lambda i, c: (jnp.minimum((i + 1) * hb, t // SUBLANES - 1), c), c=rows[j][2]))
                 for j in nxt]
    out_shape = [jax.ShapeDtypeStruct((t, w), dt) for (w, dt) in out_rows]
    out_shape += [jax.ShapeDtypeStruct((SUBLANES, w), F32) for w in out_accs]
    out_specs = [pl.BlockSpec((tm, w), lambda i: (i, 0)) for (w, _) in out_rows]
    out_specs += [pl.BlockSpec((SUBLANES, w), lambda i: (0, 0)) for w in out_accs]
    args = [r[0] for r in rows] + list(consts) + [rows[j][0] for j in prev] + [rows[j][0] for j in nxt]
    res = pl.pallas_call(
        kern, name=name, grid=(nblk,), in_specs=in_specs, out_specs=out_specs, out_shape=out_shape,
        compiler_params=_PARAMS(dimension_semantics=("arbitrary",)),
    )(*args)
    return list(res[:nor]), list(res[nor:])


def _shift_down(x, halo, k):
    rolled = pltpu.roll(x, k, 0)
    row = lax.broadcasted_iota(jnp.int32, (SUBLANES, x.shape[1]), 0)
    first = jnp.where(row < k, pltpu.roll(halo, k, 0), rolled[:SUBLANES])
    if x.shape[0] == SUBLANES:
        return first
    return jnp.concatenate([first, rolled[SUBLANES:]], axis=0)


def _shift_up(x, halo, k):
    n = x.shape[0]
    rolled = pltpu.roll(x, n - k, 0)
    row = lax.broadcasted_iota(jnp.int32, (SUBLANES, x.shape[1]), 0)
    last = jnp.where(row >= SUBLANES - k, pltpu.roll(halo, SUBLANES - k, 0), rolled[n - SUBLANES:])
    if n == SUBLANES:
        return last
    return jnp.concatenate([rolled[:n - SUBLANES], last], axis=0)


class _Transposed:
    def __init__(self, w):
        self.w = w


def _t(w):
    return _Transposed(w)


def _mm(name, a, b, out_dtype=F32, trans_a=False, tm=1024, tn=1024, tk=1024, epilogue=None, extra=None):
    trans_b = isinstance(b, _Transposed)
    assert not (trans_a and trans_b)
    if trans_b:
        b = b.w
    if trans_a:
        kdim, m = a.shape
    else:
        m, kdim = a.shape
    n = b.shape[0] if trans_b else b.shape[1]
    assert b.shape[1 if trans_b else 0] == kdim, (name, a.shape, b.shape)
    tm, tn, tk = _tile(m, tm), _tile(n, tn), _tile(kdim, tk)
    nk = kdim // tk
    dims = (((0,), (0,)), ((), ())) if trans_a else (((1,), (1 if trans_b else 0,)), ((), ()))

    n_in = 2 if extra is None else 3

    def kern(*refs):
        a_ref, b_ref, o_ref, acc = refs[0], refs[1], refs[n_in], refs[n_in + 1:]

        def store(res):
            if epilogue is not None:
                res = epilogue(res) if extra is None else epilogue(res, refs[2][...])
            o_ref[...] = res.astype(o_ref.dtype)

        part = lax.dot_general(a_ref[...].astype(BF16), b_ref[...].astype(BF16), dims, preferred_element_type=F32)
        if nk == 1:
            store(part)
        else:
            k = pl.program_id(2)

            @pl.when(k == 0)
            def _():
                acc[0][...] = part

            @pl.when(k > 0)
            def _():
                acc[0][...] += part

            @pl.when(k == nk - 1)
            def _():
                store(acc[0][...])

    a_spec = pl.BlockSpec((tk, tm), lambda i, j, k: (k, i)) if trans_a else pl.BlockSpec((tm, tk), lambda i, j, k: (i, k))
    b_spec = pl.BlockSpec((tn, tk), lambda i, j, k: (j, k)) if trans_b else pl.BlockSpec((tk, tn), lambda i, j, k: (k, j))
    out_spec = pl.BlockSpec((tm, tn), lambda i, j, k: (i, j))
    return pl.pallas_call(
        kern, name=name, grid=(m // tm, n // tn, nk),
        in_specs=[a_spec, b_spec] + ([] if extra is None else [out_spec]),
        out_specs=out_spec,
        out_shape=jax.ShapeDtypeStruct((m, n), out_dtype),
        scratch_shapes=[] if nk == 1 else [pltpu.VMEM((tm, tn), F32)],
        compiler_params=_PARAMS(dimension_semantics=("parallel", "parallel", "arbitrary")),
    )(*((a, b) if extra is None else (a, b, extra)))


def _scan(name, a, b, seq, reverse=False, tm=256):
    t, d = a.shape
    tm = _tile(seq, tm)
    nblk = t // tm
    ntile = tm // SUBLANES

    def kern(a_ref, b_ref, h_ref, carry_h, carry_a):
        i = pl.program_id(0)
        blk = (nblk - 1 - i) if reverse else i
        edge = (((blk + 1) * tm) % seq == 0) if reverse else ((blk * tm) % seq == 0)

        @pl.when(edge)
        def _():
            carry_h[...] = jnp.zeros_like(carry_h)
            carry_a[...] = jnp.zeros_like(carry_a)

        def tile_step(j, c):
            jj = (ntile - 1 - j) if reverse else j
            rows = pl.ds(pl.multiple_of(jj * SUBLANES, SUBLANES), SUBLANES)
            a8 = a_ref[rows, :]
            b8 = b_ref[rows, :]
            h, an = c
            out = [None] * SUBLANES
            order = range(SUBLANES - 1, -1, -1) if reverse else range(SUBLANES)
            for r in order:
                if reverse:
                    h = b8[r:r + 1, :] + an * h
                    an = a8[r:r + 1, :]
                else:
                    h = a8[r:r + 1, :] * h + b8[r:r + 1, :]
                out[r] = h
            h_ref[rows, :] = jnp.concatenate(out, axis=0)
            return (h, an)

        h, an = lax.fori_loop(0, ntile, tile_step, (carry_h[...], carry_a[...]))
        carry_h[...] = h
        carry_a[...] = an

    idx = (lambda i: (nblk - 1 - i, 0)) if reverse else (lambda i: (i, 0))
    return pl.pallas_call(
        kern, name=name, grid=(nblk,),
        in_specs=[pl.BlockSpec((tm, d), idx), pl.BlockSpec((tm, d), idx)],
        out_specs=pl.BlockSpec((tm, d), idx),
        out_shape=jax.ShapeDtypeStruct((t, d), F32),
        scratch_shapes=[pltpu.VMEM((1, d), F32), pltpu.VMEM((1, d), F32)],
        compiler_params=_PARAMS(dimension_semantics=("arbitrary",)),
    )(a, b)


_NN = (((1,), (0,)), ((), ()))
_NT = (((1,), (1,)), ((), ()))
_TN = (((0,), (0,)), ((), ()))


def _dot1(a, b, dims):
    return lax.dot_general(a.astype(BF16), b.astype(BF16), dims, preferred_element_type=F32)


def _dot3(a, b, dims):
    a_hi, b_hi = a.astype(BF16), b.astype(BF16)
    a_lo, b_lo = (a - a_hi.astype(F32)).astype(BF16), (b - b_hi.astype(F32)).astype(BF16)
    dg = lambda p, q: lax.dot_general(p, q, dims, preferred_element_type=F32)
    return dg(a_hi, b_hi) + (dg(a_hi, b_lo) + dg(a_lo, b_hi))


def _make_bmm(dot):
    def make(dims, da_rule, db_rule):
        @jax.custom_vjp
        def f(a, b):
            return dot(a, b, dims)

        def fwd(a, b):
            return dot(a, b, dims), (a, b)

        def bwd(res, g):
            a, b = res
            return da_rule(a, b, g), db_rule(a, b, g)

        f.defvjp(fwd, bwd)
        return f

    return dict(nn=make(_NN, lambda a, b, g: dot(g, b, _NT), lambda a, b, g: dot(a, g, _TN)),
                nt=make(_NT, lambda a, b, g: dot(g, b, _NN), lambda a, b, g: dot(g, a, _TN)),
                tn=make(_TN, lambda a, b, g: dot(b, g, _NT), lambda a, b, g: dot(a, g, _NN)))


_BMM = {1: _make_bmm(_dot1), 3: _make_bmm(_dot3)}
_WKV_PASSES = dict(pair=1, read=1, inv=3, apply=1, write=1)


def _running_sum(x, reverse):
    c = x.shape[0]
    row = lax.broadcasted_iota(jnp.int32, x.shape, 0)
    k = 1
    while k < c:
        if reverse:
            x = x + jnp.where(row < c - k, pltpu.roll(x, c - k, 0), 0.0)
        else:
            x = x + jnp.where(row >= k, pltpu.roll(x, k, 0), 0.0)
        k *= 2
    return x


@jax.custom_vjp
def _cumsum_rows(x):
    return _running_sum(x, False)


_cumsum_rows.defvjp(lambda x: (_running_sum(x, False), None), lambda _, g: (_running_sum(g, True),))


@jax.custom_vjp
def _unit_lower_inverse(nl):
    c = nl[0].shape[0]
    mm = _BMM[_WKV_PASSES["inv"]]["nn"]
    eye = jnp.where(lax.broadcasted_iota(jnp.int32, (c, c), 0) == lax.broadcasted_iota(jnp.int32, (c, c), 1), 1.0, 0.0)
    inv = [eye + z for z in nl]
    p = nl
    for _ in range(c.bit_length() - 2):
        p = [mm(z, z) for z in p]
        inv = [i_ + mm(p_, i_) for i_, p_ in zip(inv, p)]
    return inv


def _unit_lower_inverse_fwd(nl):
    inv = _unit_lower_inverse(nl)
    return inv, inv


def _unit_lower_inverse_bwd(inv, g):
    mm = _BMM[_WKV_PASSES["inv"]]
    left = [mm["tn"](x, g_) for x, g_ in zip(inv, g)]
    return ([mm["nt"](l_, x) for l_, x in zip(left, inv)],)


_unit_lower_inverse.defvjp(_unit_lower_inverse_fwd, _unit_lower_inverse_bwd)


@jax.custom_vjp
def _kept_inverse(nl, inv):
    return inv


_kept_inverse.defvjp(lambda nl, inv: (inv, inv),
                     lambda inv, g: (_unit_lower_inverse_bwd(inv, g)[0], [jnp.zeros_like(x) for x in inv]))


def _wkv_chunk(r, lw, k, v, a, b, s0, kept_inv=None):
    c = r[0].shape[0]
    ti = lax.broadcasted_iota(jnp.int32, (c, 2 * c), 0)
    tj = lax.broadcasted_iota(jnp.int32, (c, 2 * c), 1)
    right = tj >= c
    tau = jnp.where(right, tj - c, tj)
    strict_left = jnp.logical_and(jnp.logical_not(right), tau < ti)[:, :c]
    strict_right = jnp.logical_and(right, tau < ti)
    incl = tau <= ti
    last = lax.broadcasted_iota(jnp.int32, r[0].shape, 0) == c - 1
    each = lambda f, *ls: [f(*z) for z in zip(*ls)]
    rows2 = lambda x, y: jnp.concatenate([x, y], axis=0)
    pair, read, inv_, apply_, write = (_BMM[_WKV_PASSES[role]] for role in ("pair", "read", "inv", "apply", "write"))
    cum = each(_cumsum_rows, lw)
    w_incl = each(jnp.exp, cum)
    w_inv = each(lambda z: jnp.exp(-z), cum)
    at = each(lambda a_, c_, l_: a_ * jnp.exp(c_ - l_), a, cum, lw)
    ar = each(rows2, at, each(jnp.multiply, r, w_incl))
    bk = each(rows2, each(jnp.multiply, b, w_inv), each(jnp.multiply, k, w_inv))
    pp = each(pair["nt"], ar, bk)
    sr = each(read["nt"], ar, s0)
    nl = [jnp.where(strict_left, z[:c, :c], 0.0) for z in pp]
    zero_v = each(lambda v_: rows2(jnp.zeros_like(v_), v_), v)
    rhs = each(lambda s, z, zv: s[:c] + apply_["nn"](jnp.where(strict_right, z[:c], 0.0), zv), sr, pp, zero_v)
    inv = _unit_lower_inverse(nl) if kept_inv is None else _kept_inverse(nl, kept_inv)
    ut = each(inv_["nn"], inv, rhs)
    uv = each(rows2, ut, v)
    y = each(lambda s, z, uv_: s[c:] + apply_["nn"](jnp.where(incl, z[c:], 0.0), uv_), sr, pp, uv)
    w_end = each(lambda z: jnp.exp(jnp.sum(jnp.where(last, z, 0.0), axis=0, keepdims=True)), cum)
    s1 = each(lambda s, uv_, bk_, w_: (s + write["tn"](uv_, bk_)) * w_, s0, uv, bk, w_end)
    return y, s1, inv


def _wkv_fwd(r, lw, k, v, a, b, seq, hb=16):
    t, d = r.shape
    n = RWKV_HEAD
    nh = d // n
    hb = min(hb, nh)
    chunk = min(WKV_CHUNK, seq)
    ncs = seq // chunk

    def kern(r_ref, lw_ref, k_ref, v_ref, a_ref, b_ref, y_ref, st_ref, inv_ref, s_scr):
        @pl.when(pl.program_id(2) == 0)
        def _():
            s_scr[...] = jnp.zeros_like(s_scr)

        heads = lambda ref: [ref[:, h * n:(h + 1) * n] for h in range(hb)]
        s0 = [s_scr[h] for h in range(hb)]
        y, s1, inv = _wkv_chunk(heads(r_ref), heads(lw_ref), heads(k_ref), heads(v_ref), heads(a_ref), heads(b_ref), s0)
        for h in range(hb):
            st_ref[0, h] = s0[h]
            inv_ref[0, h] = inv[h]
            y_ref[:, h * n:(h + 1) * n] = y[h]
            s_scr[h] = s1[h]

    vec = pl.BlockSpec((chunk, hb * n), lambda bb, g, c: (bb * ncs + c, g))
    per_chunk = lambda rows: pl.BlockSpec((1, hb, rows, rows), lambda bb, g, c: (bb * ncs + c, g, 0, 0))
    return pl.pallas_call(
        kern, name="wkv_fwd", grid=(t // seq, nh // hb, ncs), in_specs=[vec] * 6,
        out_specs=[vec, per_chunk(n), per_chunk(chunk)],
        out_shape=[jax.ShapeDtypeStruct((t, d), F32), jax.ShapeDtypeStruct((t // chunk, nh, n, n), F32),
                   jax.ShapeDtypeStruct((t // chunk, nh, chunk, chunk), F32)],
        scratch_shapes=[pltpu.VMEM((hb, n, n), F32)],
        compiler_params=_PARAMS(dimension_semantics=("parallel", "parallel", "arbitrary")),
    )(r, lw, k, v, a, b)


def _wkv_bwd(r, lw, k, v, a, b, st, inv, dy, seq, hb=16):
    t, d = r.shape
    n = RWKV_HEAD
    nh = d // n
    hb = min(hb, nh)
    chunk = min(WKV_CHUNK, seq)
    ncs = seq // chunk

    def kern(r_ref, lw_ref, k_ref, v_ref, a_ref, b_ref, st_ref, inv_ref, dy_ref,
             dr_ref, dlw_ref, dk_ref, dv_ref, da_ref, db_ref, ds_scr):
        @pl.when(pl.program_id(2) == 0)
        def _():
            ds_scr[...] = jnp.zeros_like(ds_scr)

        heads = lambda ref: [ref[:, h * n:(h + 1) * n] for h in range(hb)]
        kept = [inv_ref[0, h] for h in range(hb)]
        _, vjp = jax.vjp(lambda *args: _wkv_chunk(*args, kept_inv=kept)[:2],
                         heads(r_ref), heads(lw_ref), heads(k_ref), heads(v_ref), heads(a_ref), heads(b_ref),
                         [st_ref[0, h] for h in range(hb)])
        grads = vjp((heads(dy_ref), [ds_scr[h] for h in range(hb)]))
        for h in range(hb):
            for ref, g in zip((dr_ref, dlw_ref, dk_ref, dv_ref, da_ref, db_ref), grads[:6]):
                ref[:, h * n:(h + 1) * n] = g[h]
            ds_scr[h] = grads[6][h]

    vec = pl.BlockSpec((chunk, hb * n), lambda bb, g, c: (bb * ncs + ncs - 1 - c, g))
    per_chunk = lambda rows: pl.BlockSpec((1, hb, rows, rows), lambda bb, g, c: (bb * ncs + ncs - 1 - c, g, 0, 0))
    return pl.pallas_call(
        kern, name="wkv_bwd", grid=(t // seq, nh // hb, ncs),
        in_specs=[vec] * 6 + [per_chunk(n), per_chunk(chunk), vec],
        out_specs=[vec] * 6, out_shape=[jax.ShapeDtypeStruct((t, d), F32)] * 6,
        scratch_shapes=[pltpu.VMEM((hb, n, n), F32)],
        compiler_params=_PARAMS(dimension_semantics=("parallel", "parallel", "arbitrary")),
    )(r, lw, k, v, a, b, st, inv, dy)


def _softmax_rows(s):
    e = jnp.exp(s - jnp.max(s, axis=-1, keepdims=True))
    return e / jnp.sum(e, axis=-1, keepdims=True)


def _attn_fwd(q, kv, seq, mem_len, tq=256):
    t, d = q.shape
    dh = d // MEM_HEADS
    scale = dh ** -0.5
    tq = _tile(seq, tq)
    nq = seq // tq

    def kern(q_ref, kv_ref, o_ref):
        for h in range(MEM_HEADS):
            cols = slice(h * dh, (h + 1) * dh)
            vcols = slice(d + h * dh, d + (h + 1) * dh)
            s = lax.dot_general(q_ref[:, cols], kv_ref[:, cols], _NT, preferred_element_type=F32) * scale
            p = _softmax_rows(s)
            o_ref[:, cols] = jnp.dot(p.astype(BF16), kv_ref[:, vcols], preferred_element_type=F32).astype(o_ref.dtype)

    return pl.pallas_call(
        kern, name="attn_fwd", grid=(t // seq, nq),
        in_specs=[pl.BlockSpec((tq, d), lambda b, i: (b * nq + i, 0)), pl.BlockSpec((mem_len, 2 * d), lambda b, i: (b, 0))],
        out_specs=pl.BlockSpec((tq, d), lambda b, i: (b * nq + i, 0)),
        out_shape=jax.ShapeDtypeStruct((t, d), BF16),
        compiler_params=_PARAMS(dimension_semantics=("parallel", "parallel")),
    )(q, kv)


def _attn_bwd(q, kv, do, seq, mem_len, tq=256):
    t, d = q.shape
    dh = d // MEM_HEADS
    scale = dh ** -0.5
    tq = _tile(seq, tq)
    nq = seq // tq

    def kern(q_ref, kv_ref, do_ref, dq_ref, dkv_ref):
        @pl.when(pl.program_id(1) == 0)
        def _():
            dkv_ref[...] = jnp.zeros_like(dkv_ref)

        for h in range(MEM_HEADS):
            cols = slice(h * dh, (h + 1) * dh)
            vcols = slice(d + h * dh, d + (h + 1) * dh)
            qh, kh, vh, doh = q_ref[:, cols], kv_ref[:, cols], kv_ref[:, vcols], do_ref[:, cols]
            p = _softmax_rows(lax.dot_general(qh, kh, _NT, preferred_element_type=F32) * scale)
            dp = lax.dot_general(doh, vh, _NT, preferred_element_type=F32)
            ds = (p * (dp - jnp.sum(p * dp, axis=-1, keepdims=True)) * scale).astype(BF16)
            dq_ref[:, cols] = jnp.dot(ds, kh, preferred_element_type=F32).astype(dq_ref.dtype)
            dkv_ref[:, cols] += lax.dot_general(ds, qh, _TN, preferred_element_type=F32)
            dkv_ref[:, vcols] += lax.dot_general(p.astype(BF16), doh, _TN, preferred_element_type=F32)

    return pl.pallas_call(
        kern, name="attn_bwd", grid=(t // seq, nq),
        in_specs=[pl.BlockSpec((tq, d), lambda b, i: (b * nq + i, 0)), pl.BlockSpec((mem_len, 2 * d), lambda b, i: (b, 0)),
                  pl.BlockSpec((tq, d), lambda b, i: (b * nq + i, 0))],
        out_specs=[pl.BlockSpec((tq, d), lambda b, i: (b * nq + i, 0)), pl.BlockSpec((mem_len, 2 * d), lambda b, i: (b, 0))],
        out_shape=[jax.ShapeDtypeStruct((t, d), BF16), jax.ShapeDtypeStruct(kv.shape, F32)],
        compiler_params=_PARAMS(dimension_semantics=("parallel", "arbitrary")),
    )(q, kv, do)


def _rstd(x):
    return lax.rsqrt(jnp.mean(x * x, axis=-1, keepdims=True) + RMS_EPS)


def _rms(x, g):
    return x * _rstd(x) * g


def _rms_bwd(dy, x, g):
    rstd = _rstd(x)
    xhat = x * rstd
    dxhat = dy * g
    return rstd * (dxhat - xhat * jnp.mean(dxhat * xhat, axis=-1, keepdims=True)), dy * xhat


def _softplus(x):
    return jnp.maximum(x, 0.0) + jnp.log1p(jnp.exp(-jnp.abs(x)))


def _one_minus_exp(x):
    series = -x * (1.0 + x * (0.5 + x * (1.0 / 6.0 + x * (1.0 / 24.0 + x * (1.0 / 120.0)))))
    return jnp.where(x > -0.05, series, 1.0 - jnp.exp(x))


_GELU_C = 0.7978845608028654
_GELU_K = 0.044715


def _gelu(x):
    return 0.5 * x * (1.0 + jnp.tanh(_GELU_C * (x + _GELU_K * x * x * x)))


def _gelu_grad(x):
    th = jnp.tanh(_GELU_C * (x + _GELU_K * x * x * x))
    return 0.5 * (1.0 + th) + 0.5 * x * (1.0 - th * th) * _GELU_C * (1.0 + 3.0 * _GELU_K * x * x)


def _seg_sum(x, seg):
    def two_terms(v, dims):
        hi = v.astype(BF16)
        lo = (v - hi.astype(F32)).astype(BF16)
        return (lax.dot_general(hi, seg, dims, preferred_element_type=F32)
                + lax.dot_general(lo, seg, dims, preferred_element_type=F32))
    return two_terms(two_terms(x, _NN), _NT)


def _f32(v):
    return v.astype(F32)


def _norm_fwd(name, x, g, dtype=BF16):
    (hn,), _ = _rowwise(name, lambda rv, cv, pv, nv: ([_rms(_f32(rv[0]), cv[0])], []), [x], [g], [(x.shape[1], dtype)])
    return hn


def _resid_norm_fwd(name, x, t, g, bias=None):
    def body(rv, cv, pv, nv):
        tt = rv[1] if bias is None else rv[1] + cv[1]
        return [rv[0] + _rms(tt, cv[0])], []
    (y,), _ = _rowwise(name, body, [x, t], [g] if bias is None else [g, bias], [(x.shape[1], F32)])
    return y


def _resid_norm_bwd(name, dxn, t, g, bias=None):
    def body(rv, cv, pv, nv):
        tt = rv[1] if bias is None else rv[1] + cv[1]
        dt, dg = _rms_bwd(rv[0], tt, cv[0])
        return [dt], [dg, dt]
    d = t.shape[1]
    (dt,), (dg, db) = _rowwise(name, body, [dxn, t], [g] if bias is None else [g, bias], [(d, BF16)], [d, d])
    return dt, dg.sum(0), db.sum(0)


def _prenorm_bwd(name, dxn, dhn, x, g):
    def body(rv, cv, pv, nv):
        dx, dg = _rms_bwd(_f32(rv[1]), rv[2], cv[0])
        return [rv[0] + dx], [dg]
    d = x.shape[1]
    (dx,), (dg,) = _rowwise(name, body, [dxn, dhn, x], [g], [(d, F32)], [d])
    return dx, dg.sum(0)


def _mlp_fwd(tag, x, g_pre, g_post, w_up, w_down):
    hn = _norm_fwd(tag + "_norm", x, g_pre)
    act = _mm(tag + "_up", hn, w_up, out_dtype=BF16, epilogue=lambda up: jnp.square(jnp.maximum(up, 0.0)))
    m = _mm(tag + "_down", act, w_down)
    y = _resid_norm_fwd(tag + "_res", x, m, g_post)
    return y, (x, hn, act, m)


def _mlp_bwd(tag, saved, dy, g_pre, g_post, w_up_t, w_down_t):
    x, hn, act, m = saved
    dm, dg_post, _ = _resid_norm_bwd(tag + "_dres", dy, m, g_post)
    dup = _mm(tag + "_dup", dm, w_down_t, out_dtype=BF16, extra=act,
              epilogue=lambda dact, act_: dact * 2.0 * jnp.sqrt(_f32(act_)))
    dw_down = _mm(tag + "_dwdown", act, dm, trans_a=True)
    dw_up = _mm(tag + "_dwup", hn, dup, trans_a=True)
    dhn = _mm(tag + "_dhn", dup, w_up_t)
    dx, dg_pre = _prenorm_bwd(tag + "_dnorm", dy, dhn, x, g_pre)
    return dx, dict(g_pre=dg_pre, g_post=dg_post, w_up=dw_up, w_down=dw_down)


def _xattn_fwd(tag, x, mem_n, g_pre, g_post, w_q, w_kv, w_o, seq, mem_len):
    hn = _norm_fwd(tag + "_norm", x, g_pre)
    q = _mm(tag + "_q", hn, w_q, out_dtype=BF16)
    kv = _mm(tag + "_kv", mem_n, w_kv, out_dtype=BF16)
    o = _attn_fwd(q, kv, seq, mem_len)
    c = _mm(tag + "_o", o, w_o)
    y = _resid_norm_fwd(tag + "_res", x, c, g_post)
    return y, (x, hn, q, kv, o, c)


def _xattn_bwd(tag, saved, dy, mem_n, g_pre, g_post, w_q_t, w_kv_t, w_o_t, seq, mem_len):
    x, hn, q, kv, o, c = saved
    dc, dg_post, _ = _resid_norm_bwd(tag + "_dres", dy, c, g_post)
    do = _mm(tag + "_do", dc, w_o_t, out_dtype=BF16)
    dw_o = _mm(tag + "_dwo", o, dc, trans_a=True)
    dq, dkv = _attn_bwd(q, kv, do, seq, mem_len)
    dw_q = _mm(tag + "_dwq", hn, dq, trans_a=True)
    dhn = _mm(tag + "_dhn", dq, w_q_t)
    dw_kv = _mm(tag + "_dwkv", mem_n, dkv, trans_a=True)
    dmem_n = _mm(tag + "_dmem", dkv, w_kv_t)
    dx, dg_pre = _prenorm_bwd(tag + "_dnorm", dy, dhn, x, g_pre)
    return dx, dmem_n, dict(g_pre=dg_pre, g_post=dg_post, w_q=dw_q, w_kv=dw_kv, w_o=dw_o)


def _lru_gates(z0, z1, gb0, gb1, sp):
    r = jax.nn.sigmoid(z0 + gb0)
    i = jax.nn.sigmoid(z1 + gb1)
    log_a = -LRU_C * r * sp
    a = jnp.exp(log_a)
    mult = jnp.sqrt(_one_minus_exp(2.0 * log_a))
    return r, i, a, mult


def _rglru_fwd(x, p, seq):
    d = x.shape[1]
    hn = _norm_fwd("a_norm", x, p["g_pre"])
    proj = _mm("a_in", hn, p["w_in"])

    def conv_body(rv, cv, pv, nv):
        u = rv[0] + cv[0]
        halo = pv[0] + cv[0]
        i = pl.program_id(0)
        halo = jnp.where((i * rv[0].shape[0]) % seq == 0, jnp.zeros_like(halo), halo)
        conv = cv[2] + u * cv[1][CONV_WIDTH - 1:CONV_WIDTH]
        for tap in range(CONV_WIDTH - 1):
            conv = conv + _shift_down(u, halo, CONV_WIDTH - 1 - tap) * cv[1][tap:tap + 1]
        return [conv], []
    (conv,), _ = _rowwise("a_conv", conv_body, [(proj, d, 1)], [p["b_in_u"], p["conv_w"], p["conv_b"]], [(d, F32)],
                          prev=[0], seq=seq)
    z = _mm("a_gate", conv, p["w_gate"])

    def gate_body(rv, cv, pv, nv):
        r, i, a, mult = _lru_gates(rv[0], rv[1], cv[0], cv[1], _softplus(-cv[2]))
        return [a, mult * i * rv[2]], []
    (a, bb), _ = _rowwise("a_gates", gate_body, [(z, d, 0), (z, d, 1), conv], [p["gate_b0"], p["gate_b1"], p["lam"]],
                          [(d, F32), (d, F32)])
    h = _scan("a_scan", a, bb, seq)

    def hy_body(rv, cv, pv, nv):
        return [rv[0] * _gelu(rv[1] + cv[0])], []
    (hy,), _ = _rowwise("a_hy", hy_body, [h, (proj, d, 0)], [p["b_in_y"]], [(d, BF16)])
    out = _mm("a_out", hy, p["w_out"])
    y = _resid_norm_fwd("a_res", x, out, p["g_post"], bias=p["b_out"])
    return y, (x, hn, proj, conv, z, a, h, hy, out)


def _rglru_bwd(saved, dy, p, pt, seq):
    x, hn, proj, conv, z, a, h, hy, out = saved
    d = x.shape[1]
    dt, dg_post, db_out = _resid_norm_bwd("a_dres", dy, out, p["g_post"], bias=p["b_out"])
    dhy = _mm("a_dhy", dt, pt["w_out_t"])
    dw_out = _mm("a_dwout", hy, dt, trans_a=True)

    def dh_body(rv, cv, pv, nv):
        yb = rv[2] + cv[0]
        return [rv[0] * _gelu(yb), rv[0] * rv[1] * _gelu_grad(yb)], []
    (dh, dyb), _ = _rowwise("a_dh", dh_body, [dhy, h, (proj, d, 0)], [p["b_in_y"]], [(d, F32), (d, BF16)])
    g = _scan("a_rscan", a, dh, seq, reverse=True)

    def dgate_body(rv, cv, pv, nv):
        gg, hh, z0, z1, cnv = rv
        sp = _softplus(-cv[2])
        r, i, aa, mult = _lru_gates(z0, z1, cv[0], cv[1], sp)
        i_blk = pl.program_id(0)
        halo = jnp.where((i_blk * gg.shape[0]) % seq == 0, jnp.zeros_like(pv[0]), pv[0])
        da = gg * _shift_down(hh, halo, 1)
        dmult = gg * i * cnv
        di = gg * mult * cnv
        dconv = gg * mult * i
        dlog_a = da * aa - dmult * aa * aa / mult
        dz0 = dlog_a * (-LRU_C * sp) * r * (1.0 - r)
        dz1 = di * i * (1.0 - i)
        dsp = dlog_a * (-LRU_C * r)
        dlam = dsp * (-jax.nn.sigmoid(-cv[2]))
        return [jnp.concatenate([dz0, dz1], axis=1), dconv], [dz0, dz1, dlam]
    (dz, dconv1), (dgb0, dgb1, dlam) = _rowwise(
        "a_dgates", dgate_body, [g, h, (z, d, 0), (z, d, 1), conv], [p["gate_b0"], p["gate_b1"], p["lam"]],
        [(2 * d, BF16), (d, F32)], [d, d, d], prev=[1], seq=seq)
    dconv2 = _mm("a_dconv", dz, pt["w_gate_t"])
    dw_gate = _mm("a_dwgate", conv, dz, trans_a=True)

    def dconv_body(rv, cv, pv, nv):
        dc1, dc2, pu, dyb_ = rv
        dc = dc1 + dc2
        dc_next = nv[0] + nv[1]
        u = pu + cv[0]
        i_blk = pl.program_id(0)
        halo = jnp.where((i_blk * u.shape[0]) % seq == 0, jnp.zeros_like(pv[0]), pv[0] + cv[0])
        du = dc * cv[1][CONV_WIDTH - 1:CONV_WIDTH]
        dws = []
        for tap in range(CONV_WIDTH - 1):
            k = CONV_WIDTH - 1 - tap
            du = du + _shift_up(dc, dc_next, k) * cv[1][tap:tap + 1]
            dws.append(dc * _shift_down(u, halo, k))
        dws.append(dc * u)
        return [jnp.concatenate([_f32(dyb_), du], axis=1)], dws + [dc, _f32(dyb_), du]
    (dproj,), accs = _rowwise(
        "a_dconvw", dconv_body, [dconv1, dconv2, (proj, d, 1), dyb], [p["b_in_u"], p["conv_w"]],
        [(2 * d, BF16)], [d] * (CONV_WIDTH + 3), prev=[2], nxt=[0, 1], seq=seq)
    dconv_w = jnp.stack([acc.sum(0) for acc in accs[:CONV_WIDTH]])
    dconv_b = accs[CONV_WIDTH].sum(0)
    db_in = jnp.concatenate([accs[CONV_WIDTH + 1].sum(0), accs[CONV_WIDTH + 2].sum(0)])
    dhn = _mm("a_dhn", dproj, pt["w_in_t"])
    dw_in = _mm("a_dwin", hn, dproj, trans_a=True)
    dx, dg_pre = _prenorm_bwd("a_dnorm", dy, dhn, x, p["g_pre"])
    grads = dict(g_pre=dg_pre, g_post=dg_post, b_out=db_out, w_out=dw_out, gate_b0=dgb0.sum(0), gate_b1=dgb1.sum(0),
                 lam=dlam.sum(0), w_gate=dw_gate, conv_w=dconv_w, conv_b=dconv_b, b_in=db_in, w_in=dw_in)
    return dx, grads


def _rwkv_prep(k, wl, za, w0, a0, k_k, k_a, seg):
    w_in = wl + w0
    e_w = jnp.exp(-_softplus(-w_in) - 0.5)
    a = jax.nn.sigmoid(za + a0)
    q = k * k_k
    norm = jnp.sqrt(_seg_sum(q * q, seg))
    n = jnp.maximum(norm, 1e-12)
    kk = q / n
    return w_in, e_w, a, norm, n, kk


def _rwkv_out(y, r, k2, v, gn_g, gn_b, r_k, seg):
    inv = 1.0 / RWKV_HEAD
    yc = y - _seg_sum(y, seg) * inv
    rstd = lax.rsqrt(_seg_sum(yc * yc, seg) * inv + RWKV_GN_EPS)
    yhat = yc * rstd
    s = _seg_sum(r * k2 * r_k, seg)
    return rstd, yhat, s, yhat * gn_g + gn_b + s * v


def _rwkv_fwd(x, p, seq):
    t, d = x.shape
    nseq = t // seq

    def mix_body(rv, cv, pv, nv):
        hn = _rms(rv[0], cv[0])
        xx = _shift_down(hn, _rms(pv[0], cv[0]), 1) - hn
        return [hn] + [hn + xx * cv[1][c:c + 1] for c in range(6)], []
    (hn, xr, xw, xk, xv, xa, xg), _ = _rowwise("b_mix", mix_body, [x], [p["g_pre"], p["mu"]],
                                               [(d, F32)] + [(d, BF16)] * 6, prev=[0], seq=seq)
    r = _mm("b_r", xr, p["w_r"])
    k = _mm("b_k", xk, p["w_k"])
    v = _mm("b_v", xv, p["w_v"])
    lw = _mm("b_w1", xw, p["w1"])
    la = _mm("b_a1", xa, p["a1"], out_dtype=BF16)
    lg = _mm("b_g1", xg, p["g1"])
    (th,), _ = _rowwise("b_tanh", lambda rv, cv, pv, nv: ([jnp.tanh(rv[0])], []), [lw], [], [(lw.shape[1], BF16)])
    (sg,), _ = _rowwise("b_sig", lambda rv, cv, pv, nv: ([jax.nn.sigmoid(rv[0])], []), [lg], [], [(lg.shape[1], BF16)])
    wl = _mm("b_w2", th, p["w2"])
    za = _mm("b_a2", la, p["a2"])
    g = _mm("b_g2", sg, p["g2"])

    def prep_body(rv, cv, pv, nv):
        kk_, wl_, za_ = rv
        _, e_w, a, _, _, kk = _rwkv_prep(kk_, wl_, za_, cv[0], cv[1], cv[2], cv[3], cv[4])
        return [-e_w, kk_ * (1.0 + (a - 1.0) * cv[3]), -kk, kk * a], []
    (log_w, k2, rem_a, rem_b), _ = _rowwise("b_prep", prep_body, [k, wl, za],
                                            [p["w0"], p["a0"], p["k_k"], p["k_a"], p["seg"]], [(d, F32)] * 4, tm=256)
    rec_in = (r, log_w, k2, v, rem_a, rem_b)
    y, *states = _wkv_fwd(*rec_in, seq)

    def out_body(rv, cv, pv, nv):
        y_, r_, k2_, v_, g_ = rv
        _, _, _, out = _rwkv_out(y_, r_, k2_, v_, cv[0], cv[1], cv[2], cv[3])
        return [out * g_], []
    (og,), _ = _rowwise("b_out", out_body, [y, r, k2, v, g], [p["gn_g"], p["gn_b"], p["r_k"], p["seg"]], [(d, BF16)], tm=256)
    o = _mm("b_o", og, p["w_o"])
    res = _resid_norm_fwd("b_res", x, o, p["g_post"])
    return res, (x, hn, xr, xw, xk, xv, xa, xg, r, k, v, th, la, sg, wl, za, g, k2, rec_in, states, y, og, o)


def _rwkv_bwd(saved, dres, p, pt, seq):
    x, hn, xr, xw, xk, xv, xa, xg, r, k, v, th, la, sg, wl, za, g, k2, rec_in, states, y, og, o = saved
    t, d = x.shape
    nseq = t // seq
    do, dg_post, _ = _resid_norm_bwd("b_dres", dres, o, p["g_post"])
    dog = _mm("b_dog", do, pt["w_o_t"])
    dw_o = _mm("b_dwo", og, do, trans_a=True)

    def dout_body(rv, cv, pv, nv):
        dog_, y_, r_, k2_, v_, g_ = rv
        gn_g, gn_b, r_k, bd = cv
        inv = 1.0 / RWKV_HEAD
        rstd, yhat, s, out = _rwkv_out(y_, r_, k2_, v_, gn_g, gn_b, r_k, bd)
        dout = dog_ * g_
        ds = _seg_sum(dout * v_, bd)
        dyhat = dout * gn_g
        dy = rstd * (dyhat - _seg_sum(dyhat, bd) * inv - yhat * _seg_sum(dyhat * yhat, bd) * inv)
        return [dy, dog_ * out, dout * s, ds * k2_ * r_k, ds * r_ * r_k], [ds * r_ * k2_, dout * yhat, dout]
    (dy, dgate, dv_b, dr_b, dk2_b), (dr_k, dgn_g, dgn_b) = _rowwise(
        "b_dout", dout_body, [dog, y, r, k2, v, g], [p["gn_g"], p["gn_b"], p["r_k"], p["seg"]],
        [(d, F32), (d, BF16), (d, F32), (d, F32), (d, F32)], [d, d, d], tm=256)
    dr_rec, dlw_rec, dk2_rec, dv_rec, da_rec, db_rec = _wkv_bwd(*rec_in, *states, dy, seq)

    def dprep_body(rv, cv, pv, nv):
        dr_rec_, dlw_rec_, dk2_rec_, dv_rec_, da_rec_, db_rec_, dr_b_, dk2_b_, dv_b_, k_, wl_, za_ = rv
        w0, a0, k_k, k_a, bd = cv
        w_in, e_w, a, norm, n, kk = _rwkv_prep(k_, wl_, za_, w0, a0, k_k, k_a, bd)
        dk2 = dk2_rec_ + dk2_b_
        dkk = db_rec_ * a - da_rec_
        da = db_rec_ * kk + dk2 * k_ * k_a
        dq = jnp.where(norm > 1e-12, dkk - kk * _seg_sum(kk * dkk, bd), dkk) / n
        dk = dk2 * (1.0 + (a - 1.0) * k_a) + dq * k_k
        dza = da * a * (1.0 - a)
        dwl = dlw_rec_ * (-e_w) * jax.nn.sigmoid(-w_in)
        return [dr_rec_ + dr_b_, dk, dv_rec_ + dv_b_, dza, dwl], [dk2 * k_ * (a - 1.0), dq * k_, dza, dwl]
    (dr, dk, dv, dza, dwl), (dk_a, dk_k, da0, dw0) = _rowwise(
        "b_dprep", dprep_body, [dr_rec, dlw_rec, dk2_rec, dv_rec, da_rec, db_rec, dr_b, dk2_b, dv_b, k, wl, za],
        [p["w0"], p["a0"], p["k_k"], p["k_a"], p["seg"]], [(d, BF16)] * 5, [d] * 4, tm=256)

    dw_r = _mm("b_dwr", xr, dr, trans_a=True)
    dw_k = _mm("b_dwk", xk, dk, trans_a=True)
    dw_v = _mm("b_dwv", xv, dv, trans_a=True)
    dxr = _mm("b_dxr", dr, pt["w_r_t"])
    dxk = _mm("b_dxk", dk, pt["w_k_t"])
    dxv = _mm("b_dxv", dv, pt["w_v_t"])
    da2 = _mm("b_da2", la, dza, trans_a=True)
    dla = _mm("b_dla", dza, pt["a2_t"], out_dtype=BF16)
    da1 = _mm("b_da1", xa, dla, trans_a=True)
    dxa = _mm("b_dxa", dla, pt["a1_t"])
    dw2 = _mm("b_dw2", th, dwl, trans_a=True)
    dth = _mm("b_dth", dwl, pt["w2_t"])
    (dzw,), _ = _rowwise("b_dtanh", lambda rv, cv, pv, nv: ([rv[0] * (1.0 - _f32(rv[1]) * _f32(rv[1]))], []),
                         [dth, th], [], [(th.shape[1], BF16)])
    dw1 = _mm("b_dw1", xw, dzw, trans_a=True)
    dxw = _mm("b_dxw", dzw, pt["w1_t"])
    dg2 = _mm("b_dg2", sg, dgate, trans_a=True)
    dsg = _mm("b_dsg", dgate, pt["g2_t"])
    (dzg,), _ = _rowwise("b_dsig", lambda rv, cv, pv, nv: ([rv[0] * _f32(rv[1]) * (1.0 - _f32(rv[1]))], []),
                         [dsg, sg], [], [(sg.shape[1], BF16)])
    dg1 = _mm("b_dg1", xg, dzg, trans_a=True)
    dxg = _mm("b_dxg", dzg, pt["g1_t"])

    def dmix_body(rv, cv, pv, nv):
        hn_ = rv[0]
        dxs = rv[1:]
        mu = cv[0]
        xx = _shift_down(hn_, pv[0], 1) - hn_
        dsum = dxs[0]
        dxx = dxs[0] * mu[0:1]
        dxx_next = nv[0] * mu[0:1]
        for c in range(1, 6):
            dsum = dsum + dxs[c]
            dxx = dxx + dxs[c] * mu[c:c + 1]
            dxx_next = dxx_next + nv[c] * mu[c:c + 1]
        return [dsum - dxx + _shift_up(dxx, dxx_next, 1)], [dxs[c] * xx for c in range(6)]
    (dhn,), dmu = _rowwise("b_dmix", dmix_body, [hn, dxr, dxw, dxk, dxv, dxa, dxg], [p["mu"]], [(d, F32)], [d] * 6,
                           prev=[0], nxt=[1, 2, 3, 4, 5, 6], seq=seq, tm=256)
    dx, dg_pre = _prenorm_bwd("b_dnorm", dres, dhn, x, p["g_pre"])
    grads = dict(g_pre=dg_pre, g_post=dg_post, mu=jnp.stack([m.sum(0) for m in dmu]), w_r=dw_r, w_k=dw_k, w_v=dw_v,
                 w0=dw0.sum(0), w1=dw1, w2=dw2, a0=da0.sum(0), a1=da1, a2=da2, g1=dg1, g2=dg2, k_k=dk_k.sum(0),
                 k_a=dk_a.sum(0), r_k=dr_k.sum(0), gn_g=dgn_g.sum(0), gn_b=dgn_b.sum(0), w_o=dw_o)
    return dx, grads


_WEIGHTS = ['ln_gains', 'mem_norm', 'a_conv_w', 'a_conv_b', 'a_w_in', 'a_b_in', 'a_gate_w', 'a_gate_b', 'a_lambda', 'a_w_out',
            'a_b_out', 'b_mu', 'b_w_rkv', 'b_w0', 'b_w1', 'b_w2', 'b_a0', 'b_a1', 'b_a2', 'b_g1', 'b_g2', 'b_k_k', 'b_k_a',
            'b_r_k', 'b_gn_g', 'b_gn_b', 'b_w_o', 'c_w_q', 'c_w_kv', 'c_w_o', 'm_w_up', 'm_w_down']
_SHARD_AXIS = dict(ln_gains=2, mem_norm=None, a_conv_w=2, a_conv_b=None, a_w_in=2, a_b_in=None, a_gate_w=3, a_gate_b=3,
                   a_lambda=None, a_w_out=1, a_b_out=None, b_mu=2, b_w_rkv=2, b_w0=1, b_w1=1, b_w2=2, b_a0=1, b_a1=1, b_a2=2,
                   b_g1=1, b_g2=2, b_k_k=1, b_k_a=1, b_r_k=None, b_gn_g=1, b_gn_b=1, b_w_o=1, c_w_q=1, c_w_kv=2, c_w_o=1,
                   m_w_up=2, m_w_down=1)
_MATRICES = ['a_w_in', 'a_gate_w', 'a_w_out', 'b_w_rkv', 'b_w1', 'b_w2', 'b_a1', 'b_a2', 'b_g1', 'b_g2', 'b_w_o', 'c_w_q',
             'c_w_kv', 'c_w_o', 'm_w_up', 'm_w_down']
_SHARDED = [n for n in _WEIGHTS if _SHARD_AXIS[n] is not None]
_VECTORS = [n for n in _SHARDED if n not in _MATRICES]
_REPLICATED = [n for n in _WEIGHTS if _SHARD_AXIS[n] is None]
N_XY = 4
N_DEV = 8
PACK_W = 1024
PACK_ROWS = 256


def _pack(arrs, dtype, row_mult=PACK_ROWS):
    parts = []
    rows = 0
    for a in arrs:
        n = a.size
        r = -(-n // PACK_W)
        parts.append(jnp.pad(a.reshape(-1).astype(dtype), (0, r * PACK_W - n)))
        rows += r
    pad_rows = -(-rows // row_mult) * row_mult - rows
    if pad_rows:
        parts.append(jnp.zeros((pad_rows * PACK_W,), dtype))
    return jnp.concatenate(parts).reshape(-1, PACK_W)


def _unpack(flat, shapes):
    out = []
    row = 0
    for shp in shapes:
        n = 1
        for s in shp:
            n *= s
        r = -(-n // PACK_W)
        out.append(flat[row:row + r].reshape(-1)[:n].reshape(shp))
        row += r
    return out


_ANY = pl.BlockSpec(memory_space=pl.ANY)


def _xy_peers():
    x, y = lax.axis_index("x"), lax.axis_index("y")
    return [(1 - x, y), (x, 1 - y), (1 - x, 1 - y)]


def _all_gather_xy(wm, wv):
    half = wm.shape[0] // 2

    def body(wm_ref, wv_ref, gm_ref, gv_ref, send_sems, recv_sems, local_sems):
        x, y, c = lax.axis_index("x"), lax.axis_index("y"), lax.axis_index("c")
        me = 2 * x + y
        mine = pl.ds(pl.multiple_of(c * half, SUBLANES), half)
        other = pl.ds(pl.multiple_of((1 - c) * half, SUBLANES), half)
        local = [pltpu.make_async_copy(wm_ref, gm_ref.at[me], local_sems.at[0]),
                 pltpu.make_async_copy(wv_ref, gv_ref.at[me], local_sems.at[1])]
        for cp in local:
            cp.start()
        sends, lands, passes, from_sibling = [], [], [], []
        for j, (px, py) in enumerate(_xy_peers()):
            peer = 2 * px + py
            ici = functools.partial(pltpu.make_async_remote_copy, device_id=(px, py, c), device_id_type=MESH)
            sends.append(ici(src_ref=wm_ref.at[mine], dst_ref=gm_ref.at[me, mine], send_sem=send_sems.at[j], recv_sem=recv_sems.at[j]))
            lands.append(ici(src_ref=wm_ref.at[mine], dst_ref=gm_ref.at[peer, mine], send_sem=send_sems.at[j], recv_sem=recv_sems.at[j]))
            sends.append(ici(src_ref=wv_ref, dst_ref=gv_ref.at[me], send_sem=send_sems.at[3 + j], recv_sem=recv_sems.at[3 + j]))
            lands.append(ici(src_ref=wv_ref, dst_ref=gv_ref.at[peer], send_sem=send_sems.at[3 + j], recv_sem=recv_sems.at[3 + j]))
            d2d = functools.partial(pltpu.make_async_remote_copy, send_sem=send_sems.at[6 + j], recv_sem=recv_sems.at[6 + j],
                                    device_id=(x, y, 1 - c), device_id_type=MESH)
            passes.append(d2d(src_ref=gm_ref.at[peer, mine], dst_ref=gm_ref.at[peer, mine]))
            from_sibling.append(d2d(src_ref=gm_ref.at[peer, other], dst_ref=gm_ref.at[peer, other]))
        for cp in sends:
            cp.start()
        for j in range(N_XY - 1):
            lands[2 * j].wait_recv()
            passes[j].start()
        for j in range(N_XY - 1):
            lands[2 * j + 1].wait_recv()
        for cp in from_sibling:
            cp.wait_recv()
        for cp in sends + passes:
            cp.wait_send()
        for cp in local:
            cp.wait()

    return pl.pallas_call(
        body, name="all_gather_weights",
        in_specs=[_ANY, _ANY], out_specs=[_ANY, _ANY],
        out_shape=[jax.ShapeDtypeStruct((N_XY,) + wm.shape, wm.dtype), jax.ShapeDtypeStruct((N_XY,) + wv.shape, wv.dtype)],
        scratch_shapes=[pltpu.SemaphoreType.DMA((9,)), pltpu.SemaphoreType.DMA((9,)), pltpu.SemaphoreType.DMA((2,))],
    )(wm, wv)


_HBM = pl.BlockSpec(memory_space=pltpu.HBM)
_SEM = pl.BlockSpec(memory_space=pltpu.SEMAPHORE)
_SPLIT_COPY = functools.partial(pltpu.CompilerParams, has_side_effects=pltpu.SideEffectType.DATAFLOW_SIDE_EFFECTING)


def _gather_xy_start(buf, after):
    def body(src_ref, land_ref, after_ref, send_sems, recv_sems, src_thru, land_thru, token):
        x, y, c = lax.axis_index("x"), lax.axis_index("y"), lax.axis_index("c")
        for j, (px, py) in enumerate(_xy_peers()):
            pltpu.make_async_remote_copy(src_ref=src_ref, dst_ref=land_ref.at[2 * x + y], send_sem=send_sems.at[j],
                                         recv_sem=recv_sems.at[j], device_id=(px, py, c), device_id_type=MESH).start()
        token[...] = jnp.zeros_like(token)

    n_peers = N_XY - 1
    land = pltpu.with_memory_space_constraint(lax.empty((N_XY,) + buf.shape, buf.dtype), pltpu.HBM)
    return pl.pallas_call(
        body, name="gather_late_start",
        out_shape=(pltpu.SemaphoreType.DMA((n_peers,)), pltpu.SemaphoreType.DMA((n_peers,)), pltpu.HBM(buf.shape, buf.dtype),
                   pltpu.HBM(land.shape, buf.dtype), jax.ShapeDtypeStruct((SUBLANES, LANES), F32)),
        in_specs=(_HBM, _HBM, _ANY), out_specs=(_SEM, _SEM, _HBM, _HBM, pl.BlockSpec(memory_space=pltpu.VMEM)),
        input_output_aliases={0: 2, 1: 3}, compiler_params=_SPLIT_COPY(),
    )(pltpu.with_memory_space_constraint(buf, pltpu.HBM), land, after)


def _gather_xy_wait(send_sems, recv_sems, src_thru, land_thru, after):
    def body(src_ref, land_ref, send_sems, recv_sems, after_ref, src_dead, got_ref):
        c = lax.axis_index("c")
        for j, (px, py) in enumerate(_xy_peers()):
            cp = pltpu.make_async_remote_copy(src_ref=src_ref, dst_ref=land_ref.at[2 * px + py], send_sem=send_sems.at[j],
                                              recv_sem=recv_sems.at[j], device_id=(px, py, c), device_id_type=MESH)
            cp.wait_send()
            cp.wait_recv()

    return pl.pallas_call(
        body, name="gather_late_wait",
        out_shape=(pltpu.HBM(src_thru.shape, src_thru.dtype), pltpu.HBM(land_thru.shape, land_thru.dtype)),
        in_specs=(_HBM, _HBM, _SEM, _SEM, _ANY), out_specs=(_HBM, _HBM),
        input_output_aliases={0: 0, 1: 1}, compiler_params=_SPLIT_COPY(),
    )(src_thru, land_thru, send_sems, recv_sems, after)[1]


def _exchange_xy(gsend):
    def body(gs_ref, recv_ref, send_sems, recv_sems):
        c = lax.axis_index("c")
        sends = []
        for j, (px, py) in enumerate(_xy_peers()):
            sends.append(pltpu.make_async_remote_copy(
                src_ref=gs_ref.at[2 * px + py], dst_ref=recv_ref.at[j], send_sem=send_sems.at[j], recv_sem=recv_sems.at[j],
                device_id=(px, py, c), device_id_type=MESH))
        for cp in sends:
            cp.start()
        for cp in sends:
            cp.wait_recv()
        for cp in sends:
            cp.wait_send()

    return pl.pallas_call(
        body, name="exchange_grads",
        in_specs=[_ANY], out_specs=_ANY,
        out_shape=jax.ShapeDtypeStruct((N_XY - 1,) + gsend.shape[1:], gsend.dtype),
        scratch_shapes=[pltpu.SemaphoreType.DMA((3,)), pltpu.SemaphoreType.DMA((3,))],
    )(gsend)


def _swap_with_sibling(part):
    def body(p_ref, got_ref, send_sem, recv_sem):
        x, y, c = lax.axis_index("x"), lax.axis_index("y"), lax.axis_index("c")
        cp = pltpu.make_async_remote_copy(src_ref=p_ref, dst_ref=got_ref, send_sem=send_sem, recv_sem=recv_sem,
                                          device_id=(x, y, 1 - c), device_id_type=MESH)
        cp.start()
        cp.wait_recv()
        cp.wait_send()

    return pl.pallas_call(
        body, name="swap_sibling",
        in_specs=[_ANY], out_specs=_ANY, out_shape=jax.ShapeDtypeStruct(part.shape, part.dtype),
        scratch_shapes=[pltpu.SemaphoreType.DMA, pltpu.SemaphoreType.DMA],
    )(part)


def _all_gather_all(vec):
    def body(v_ref, out_ref, send_sems, recv_sems, local_sem):
        x, y, c = lax.axis_index("x"), lax.axis_index("y"), lax.axis_index("c")
        me = 4 * x + 2 * y + c
        local = pltpu.make_async_copy(v_ref, out_ref.at[me], local_sem)
        local.start()
        sends, recvs = [], []
        for f in range(1, N_DEV):
            fx, fy, fc = (f >> 2) & 1, (f >> 1) & 1, f & 1
            px = (1 - x) if fx else x
            py = (1 - y) if fy else y
            pc = (1 - c) if fc else c
            mk = functools.partial(pltpu.make_async_remote_copy, src_ref=v_ref, send_sem=send_sems.at[f - 1],
                                   recv_sem=recv_sems.at[f - 1], device_id=(px, py, pc), device_id_type=MESH)
            sends.append(mk(dst_ref=out_ref.at[me]))
            recvs.append(mk(dst_ref=out_ref.at[4 * px + 2 * py + pc]))
        for cp in sends:
            cp.start()
        for cp in recvs:
            cp.wait_recv()
        for cp in sends:
            cp.wait_send()
        local.wait()

    return pl.pallas_call(
        body, name="all_gather_replicated",
        in_specs=[_ANY], out_specs=_ANY, out_shape=jax.ShapeDtypeStruct((N_DEV,) + vec.shape, vec.dtype),
        scratch_shapes=[pltpu.SemaphoreType.DMA((N_DEV - 1,)), pltpu.SemaphoreType.DMA((N_DEV - 1,)), pltpu.SemaphoreType.DMA],
    )(vec)


def _adamw(g, w, m, v):
    m2 = ADAM_B1 * m + (1.0 - ADAM_B1) * g
    v2 = ADAM_B2 * v + (1.0 - ADAM_B2) * g * g
    m_hat = m2 / (1.0 - ADAM_B1 ** ADAM_STEP)
    v_hat = v2 / (1.0 - ADAM_B2 ** ADAM_STEP)
    return -ADAM_LR * (m_hat / (jnp.sqrt(v_hat) + ADAM_EPS) + ADAM_WD * w), m2, v2


def _sum_contributions(own, recv):
    def body(rv, cv, pv, nv):
        return [((rv[0] + _f32(rv[1])) + _f32(rv[2])) + _f32(rv[3])], []
    stacked = recv.reshape(-1, PACK_W)
    (part,), _ = _rowwise("sum_grads", body, [own] + [(stacked, PACK_W, 0, j * own.shape[0]) for j in range(N_XY - 1)], [],
                          [(PACK_W, F32)])
    return part


def _adamw_sharded(name, part, sib, first, w, m, v):
    def body(rv, cv, pv, nv):
        g = rv[3] + rv[4]
        return [g, *_adamw(g, rv[0], rv[1], rv[2])], []
    outs, _ = _rowwise(name, body, [w, m, v, (part, PACK_W, 0, first), (sib, PACK_W, 0, first)], [], [(PACK_W, F32)] * 4, tm=256)
    return outs


def _adamw_replicated(parts, w, m, v):
    def body(rv, cv, pv, nv):
        g = rv[0]
        for i in range(1, N_DEV):
            g = g + rv[i]
        return [g, *_adamw(g, rv[N_DEV], rv[N_DEV + 1], rv[N_DEV + 2])], []
    outs, _ = _rowwise("adamw_replicated", body, [parts[i] for i in range(N_DEV)] + [w, m, v], [], [(PACK_W, F32)] * 4)
    return outs


def _row(v):
    return v.reshape(1, -1).astype(F32)


def _gate_dense(gate_w):
    _, nh, blk, _ = gate_w.shape
    d = nh * blk
    dense = jnp.zeros((d, 2 * d), gate_w.dtype)
    for g in range(2):
        for h in range(nh):
            dense = lax.dynamic_update_slice(dense, gate_w[g, h], (h * blk, g * d + h * blk))
    return dense


def _gate_blocks(dense, nh):
    d = dense.shape[0]
    blk = d // nh
    return jnp.stack([jnp.stack([dense[h * blk:(h + 1) * blk, g * d + h * blk:g * d + (h + 1) * blk] for h in range(nh)])
                      for g in range(2)])


def _local_step(x3, mem3, target3, fw, late_weights=None):
    nseq, seq, d = x3.shape
    mem_len = mem3.shape[1]
    t = nseq * seq
    x0 = x3.reshape(t, d)
    mem2 = mem3.reshape(nseq * mem_len, d)
    target = target3.reshape(t, d)
    ln = fw["ln_gains"]
    gains = [[_row(ln[i, j]) for j in range(6)] for i in range(2)]
    nh = d // RWKV_HEAD
    seg = (jnp.arange(d)[:, None] // RWKV_HEAD == jnp.arange(LANES)[None, :]).astype(BF16)

    w_gate = _gate_dense(fw["a_gate_w"][0])
    pa = dict(g_pre=gains[0][0], g_post=gains[0][1], w_in=fw["a_w_in"][0], b_in_y=_row(fw["a_b_in"][0, :d]),
              b_in_u=_row(fw["a_b_in"][0, d:]), conv_w=fw["a_conv_w"][0].astype(F32), conv_b=_row(fw["a_conv_b"][0]),
              w_gate=w_gate, gate_b0=_row(fw["a_gate_b"][0, 0]), gate_b1=_row(fw["a_gate_b"][0, 1]), lam=_row(fw["a_lambda"][0]),
              w_out=fw["a_w_out"][0], b_out=_row(fw["a_b_out"][0]))
    pta = dict(w_in_t=_t(pa["w_in"]), w_gate_t=_t(w_gate), w_out_t=_t(pa["w_out"]))
    mem_g = _row(fw["mem_norm"])

    mem_n = _norm_fwd("mem_norm", mem2, mem_g)
    x1, sv_a = _rglru_fwd(x0, pa, seq)
    x2, sv_c0 = _xattn_fwd("c0", x1, mem_n, gains[0][2], gains[0][3], fw["c_w_q"][0], fw["c_w_kv"][0], fw["c_w_o"][0], seq, mem_len)
    x3_, sv_m0 = _mlp_fwd("m0", x2, gains[0][4], gains[0][5], fw["m_w_up"][0], fw["m_w_down"][0])
    if late_weights is not None:
        fw = {**fw, **late_weights(x3_)}
    pb = dict(g_pre=gains[1][0], g_post=gains[1][1], mu=fw["b_mu"][0].astype(F32), w_r=fw["b_w_rkv"][0, 0],
              w_k=fw["b_w_rkv"][0, 1], w_v=fw["b_w_rkv"][0, 2], w0=_row(fw["b_w0"][0]), w1=fw["b_w1"][0], w2=fw["b_w2"][0],
              a0=_row(fw["b_a0"][0]), a1=fw["b_a1"][0], a2=fw["b_a2"][0], g1=fw["b_g1"][0], g2=fw["b_g2"][0],
              k_k=_row(fw["b_k_k"][0]), k_a=_row(fw["b_k_a"][0]), r_k=_row(fw["b_r_k"][0]), gn_g=_row(fw["b_gn_g"][0]),
              gn_b=_row(fw["b_gn_b"][0]), w_o=fw["b_w_o"][0], seg=seg)
    ptb = {k + "_t": _t(pb[k]) for k in ("w_r", "w_k", "w_v", "w_o", "w1", "w2", "a1", "a2", "g1", "g2")}
    x4, sv_b = _rwkv_fwd(x3_, pb, seq)
    x5, sv_c1 = _xattn_fwd("c1", x4, mem_n, gains[1][2], gains[1][3], fw["c_w_q"][1], fw["c_w_kv"][1], fw["c_w_o"][1], seq, mem_len)
    x6, sv_m1 = _mlp_fwd("m1", x5, gains[1][4], gains[1][5], fw["m_w_up"][1], fw["m_w_down"][1])

    def loss_body(rv, cv, pv, nv):
        err = rv[0] - rv[1]
        return [err * (1.0 / d)], [err * err]
    (dx,), (sq,) = _rowwise("loss", loss_body, [x6, target], [], [(d, F32)], [d])
    loss_part = 0.5 / d * jnp.sum(sq)

    dx, g_m1 = _mlp_bwd("m1", sv_m1, dx, gains[1][4], gains[1][5], _t(fw["m_w_up"][1]), _t(fw["m_w_down"][1]))
    dx, dmem1, g_c1 = _xattn_bwd("c1", sv_c1, dx, mem_n, gains[1][2], gains[1][3], _t(fw["c_w_q"][1]), _t(fw["c_w_kv"][1]),
                                 _t(fw["c_w_o"][1]), seq, mem_len)
    dx, g_b = _rwkv_bwd(sv_b, dx, pb, ptb, seq)
    dx, g_m0 = _mlp_bwd("m0", sv_m0, dx, gains[0][4], gains[0][5], _t(fw["m_w_up"][0]), _t(fw["m_w_down"][0]))
    dx, dmem0, g_c0 = _xattn_bwd("c0", sv_c0, dx, mem_n, gains[0][2], gains[0][3], _t(fw["c_w_q"][0]), _t(fw["c_w_kv"][0]),
                                 _t(fw["c_w_o"][0]), seq, mem_len)
    dx, g_a = _rglru_bwd(sv_a, dx, pa, pta, seq)

    def dmem_body(rv, cv, pv, nv):
        _, dg = _rms_bwd(rv[1] + rv[2], rv[0], cv[0])
        return [], [dg]
    _, (dmem_g,) = _rowwise("mem_norm_grad", dmem_body, [mem2, dmem0, dmem1], [mem_g], [], [d])

    lru_heads = fw["a_gate_w"].shape[2]
    blk = d // lru_heads
    grads = dict(
        ln_gains=jnp.stack([jnp.stack([g_a["g_pre"], g_a["g_post"], g_c0["g_pre"], g_c0["g_post"], g_m0["g_pre"], g_m0["g_post"]]),
                            jnp.stack([g_b["g_pre"], g_b["g_post"], g_c1["g_pre"], g_c1["g_post"], g_m1["g_pre"], g_m1["g_post"]])]),
        mem_norm=dmem_g.sum(0),
        a_conv_w=g_a["conv_w"][None], a_conv_b=g_a["conv_b"][None], a_w_in=g_a["w_in"][None], a_b_in=g_a["b_in"][None],
        a_gate_w=_gate_blocks(g_a["w_gate"], lru_heads)[None],
        a_gate_b=jnp.stack([g_a["gate_b0"], g_a["gate_b1"]]).reshape(1, 2, lru_heads, blk),
        a_lambda=g_a["lam"][None], a_w_out=g_a["w_out"][None], a_b_out=g_a["b_out"][None],
        b_mu=g_b["mu"][None], b_w_rkv=jnp.stack([g_b["w_r"], g_b["w_k"], g_b["w_v"]])[None], b_w0=g_b["w0"][None],
        b_w1=g_b["w1"][None], b_w2=g_b["w2"][None], b_a0=g_b["a0"][None], b_a1=g_b["a1"][None], b_a2=g_b["a2"][None],
        b_g1=g_b["g1"][None], b_g2=g_b["g2"][None], b_k_k=g_b["k_k"][None], b_k_a=g_b["k_a"][None],
        b_r_k=g_b["r_k"].reshape(1, nh, RWKV_HEAD), b_gn_g=g_b["gn_g"][None], b_gn_b=g_b["gn_b"][None], b_w_o=g_b["w_o"][None],
        c_w_q=[g_c0["w_q"], g_c1["w_q"]], c_w_kv=[g_c0["w_kv"], g_c1["w_kv"]], c_w_o=[g_c0["w_o"], g_c1["w_o"]],
        m_w_up=[g_m0["w_up"], g_m1["w_up"]], m_w_down=[g_m0["w_down"], g_m1["w_down"]],
    )
    return loss_part, dx.reshape(nseq, seq, d), grads


def kernel(x, mem, ln_gains, mem_norm, a_conv_w, a_conv_b, a_w_in, a_b_in, a_gate_w, a_gate_b, a_lambda, a_w_out, a_b_out, b_mu, b_w_rkv, b_w0, b_w1, b_w2, b_a0, b_a1, b_a2, b_g1, b_g2, b_k_k, b_k_a, b_r_k, b_gn_g, b_gn_b, b_w_o, c_w_q, c_w_kv, c_w_o, m_w_up, m_w_down, loss_target, m_ln_gains, m_mem_norm, m_a_conv_w, m_a_conv_b, m_a_w_in, m_a_b_in, m_a_gate_w, m_a_gate_b, m_a_lambda, m_a_w_out, m_a_b_out, m_b_mu, m_b_w_rkv, m_b_w0, m_b_w1, m_b_w2, m_b_a0, m_b_a1, m_b_a2, m_b_g1, m_b_g2, m_b_k_k, m_b_k_a, m_b_r_k, m_b_gn_g, m_b_gn_b, m_b_w_o, m_c_w_q, m_c_w_kv, m_c_w_o, m_m_w_up, m_m_w_down, v_ln_gains, v_mem_norm, v_a_conv_w, v_a_conv_b, v_a_w_in, v_a_b_in, v_a_gate_w, v_a_gate_b, v_a_lambda, v_a_w_out, v_a_b_out, v_b_mu, v_b_w_rkv, v_b_w0, v_b_w1, v_b_w2, v_b_a0, v_b_a1, v_b_a2, v_b_g1, v_b_g2, v_b_k_k, v_b_k_a, v_b_r_k, v_b_gn_g, v_b_gn_b, v_b_w_o, v_c_w_q, v_c_w_kv, v_c_w_o, v_m_w_up, v_m_w_down):
    given = dict(locals())
    w = {n: given[n] for n in _WEIGHTS}
    mom1 = {n: given["m_" + n] for n in _WEIGHTS}
    mom2 = {n: given["v_" + n] for n in _WEIGHTS}

    me = 2 * lax.axis_index("x") + lax.axis_index("y")
    per_layer = [n for n in _MATRICES if n[0] in "cm"]
    early = [(n, None) for n in _MATRICES if n[0] == "a"] + [(n, 0) for n in per_layer]
    late = [(n, None) for n in _MATRICES if n[0] == "b"] + [(n, 1) for n in per_layer]
    piece = lambda n, layer: w[n] if layer is None else w[n][layer:layer + 1]

    def gathered(entries, buf):
        shards = [_unpack(buf[s], [piece(n, layer).shape for n, layer in entries]) for s in range(N_XY)]
        out = {}
        for i, (n, layer) in enumerate(entries):
            full = jnp.concatenate([shards[s][i] for s in range(N_XY)], axis=_SHARD_AXIS[n])
            if layer is None:
                out[n] = full
            else:
                out.setdefault(n, {})[layer] = full[0]
        return out

    late_buf = _pack([piece(n, layer) for n, layer in late], BF16)
    gm, gv = _all_gather_xy(_pack([piece(n, layer) for n, layer in early], BF16), _pack([w[n] for n in _VECTORS], F32, SUBLANES))
    send_sems, recv_sems, late_thru, land_thru, token = _gather_xy_start(late_buf, gv)
    fw = {n: w[n] for n in _REPLICATED} | gathered(early, gm)
    vec_shards = [_unpack(gv[s], [w[n].shape for n in _VECTORS]) for s in range(N_XY)]
    for i, n in enumerate(_VECTORS):
        fw[n] = jnp.concatenate([vec_shards[s][i] for s in range(N_XY)], axis=_SHARD_AXIS[n])
    fw["ln_gains"] = fw["ln_gains"] + token[0, 0]

    def late_weights(after):
        land = _gather_xy_wait(send_sems, recv_sems, late_thru, land_thru, after)
        got = gathered(late, lax.dynamic_update_index_in_dim(land, late_buf, me, 0))
        return {n: ({**fw[n], **v} if isinstance(v, dict) else v) for n, v in got.items()}

    loss_part, grad_x, grads = _local_step(x, mem, loss_target, fw, late_weights)

    tile_rows = 256
    in_place = [n for n in _SHARDED if w[n].shape[-1] == PACK_W and w[n].size % (PACK_W * tile_rows) == 0]
    in_place.sort(key=lambda n: -w[n].size)
    packed = [n for n in _SHARDED if n not in in_place]
    order = in_place + packed

    def shards_of(n, s):
        ax = _SHARD_AXIS[n]
        size = w[n].shape[ax]
        if isinstance(grads[n], list):
            assert all(g.size // N_XY % PACK_W == 0 for g in grads[n])
            return [lax.dynamic_slice_in_dim(g, s * size, size, axis=ax - 1) for g in grads[n]]
        return [lax.dynamic_slice_in_dim(grads[n], s * size, size, axis=ax)]
    me = 2 * lax.axis_index("x") + lax.axis_index("y")
    gsend = jnp.stack([_pack([g for n in order for g in shards_of(n, s)], BF16) for s in range(N_XY)])
    own = _pack([g for n in order for g in shards_of(n, me)], F32)
    part = _sum_contributions(own, _exchange_xy(gsend))
    sib = _swap_with_sibling(part)
    out_by_name = {}
    first = 0
    for n in in_place:
        flat = [src[n].reshape(-1, PACK_W) for src in (w, mom1, mom2)]
        out_by_name[n] = [o.reshape(w[n].shape) for o in _adamw_sharded("adamw_" + n, part, sib, first, *flat)]
        first += flat[0].shape[0]
    flat = [_pack([src[n] for n in packed], F32) for src in (w, mom1, mom2)]
    assert first + flat[0].shape[0] == part.shape[0]
    tail = [_unpack(o, [w[n].shape for n in packed]) for o in _adamw_sharded("adamw_packed", part, sib, first, *flat)]
    for i, n in enumerate(packed):
        out_by_name[n] = [tail[kind][i] for kind in range(4)]
    sharded_out = [[out_by_name[n][kind] for n in _SHARDED] for kind in range(4)]

    small = _pack([grads[n] for n in _REPLICATED] + [loss_part.reshape(1)], F32, SUBLANES)
    parts = _all_gather_all(small)
    zero = jnp.zeros((1,), F32)
    flat = [_pack([src[n] for n in _REPLICATED] + [zero], F32, SUBLANES) for src in (w, mom1, mom2)]
    repl_out = [_unpack(o, [w[n].shape for n in _REPLICATED] + [(1,)]) for o in _adamw_replicated(parts, *flat)]
    loss = repl_out[0][-1][0]

    result = [loss, grad_x]
    for kind in range(4):
        by_name = dict(zip(_SHARDED, sharded_out[kind])) | dict(zip(_REPLICATED, repl_out[kind][:-1]))
        result += [by_name[n] for n in _WEIGHTS]
    return tuple(result)
```

```python
import functools

import jax
import jax.numpy as jnp
from jax import lax
from jax.experimental import pallas as pl
from jax.experimental.pallas import tpu as pltpu

F32 = jnp.float32
BF16 = jnp.bfloat16
MESH = pl.DeviceIdType.MESH

LANES = 128
SUBLANES = 8
VMEM_LIMIT_BYTES = 48 * 1024 * 1024

RMS_EPS = 1e-6
LRU_C = 8.0
LRU_HEADS = 4
CONV_WIDTH = 4
RWKV_HEAD = 64
RWKV_GN_EPS = 64e-5
MEM_HEADS = 4
ADAM_LR = 0.001
ADAM_B1 = 0.9
ADAM_B2 = 0.999
ADAM_EPS = 1e-08
ADAM_WD = 0.01
ADAM_STEP = 10
WKV_CHUNK = 64

_PARAMS = functools.partial(pltpu.CompilerParams, vmem_limit_bytes=VMEM_LIMIT_BYTES)


def _tile(n, want):
    if n <= want:
        return n
    t = want
    while t >= SUBLANES:
        if n % t == 0 and t % SUBLANES == 0:
            return t
        t -= SUBLANES
    return n


def _fold8(v):
    tm, d = v.shape
    if tm == SUBLANES:
        return v
    return jnp.sum(v.reshape(tm // SUBLANES, SUBLANES, d), axis=0)


def _rowwise(name, body, rows, consts=(), out_rows=(), out_accs=(), prev=(), nxt=(), tm=512, seq=None, n_rows=None):
    rows = [r if isinstance(r, tuple) else (r, r.shape[1], 0) for r in rows]
    rows = [r if len(r) == 4 else r + (0,) for r in rows]
    t = rows[0][0].shape[0] if n_rows is None else n_rows
    tm = _tile(t, tm)
    if seq is not None:
        tm = _tile(seq, tm)
    nblk = t // tm
    nrow, ncst, nprev, nnxt = len(rows), len(consts), len(prev), len(nxt)
    nor, noa = len(out_rows), len(out_accs)
    hb = tm // SUBLANES

    def kern(*refs):
        i = pl.program_id(0)
        rv = [r[...] for r in refs[:nrow]]
        cv = [c[...] for c in refs[nrow:nrow + ncst]]
        o = nrow + ncst
        pv = []
        for j in range(nprev):
            at_start = (i * tm) % seq == 0
            h = refs[o + j][...]
            pv.append(jnp.where(at_start, jnp.zeros_like(h), h))
        o += nprev
        nv = []
        for j in range(nnxt):
            at_end = ((i + 1) * tm) % seq == 0
            h = refs[o + j][...]
            nv.append(jnp.where(at_end, jnp.zeros_like(h), h))
        o += nnxt
        outs, accs = body(rv, cv, pv, nv)
        for j in range(nor):
            refs[o + j][...] = outs[j].astype(refs[o + j].dtype)
        o += nor
        if noa:
            @pl.when(i == 0)
            def _():
                for j in range(noa):
                    refs[o + j][...] = jnp.zeros_like(refs[o + j])
            for j in range(noa):
                refs[o + j][...] += _fold8(accs[j].astype(F32))

    assert all(first % tm == 0 for (_, _, _, first) in rows), name
    in_specs = [pl.BlockSpec((tm, w), functools.partial(lambda i, c, o: (i + o, c), c=cb, o=first // tm))
                for (_, w, cb, first) in rows]
    in_specs += [pl.BlockSpec(c.shape, lambda i: (0, 0)) for c in consts]
    in_specs += [pl.BlockSpec((SUBLANES, rows[j][1]),
                              functools.partial(lambda i, c: (jnp.maximum(i * hb - 1, 0), c), c=rows[j][2])) for j in prev]
    in_specs += [pl.BlockSpec((SUBLANES, rows[j][1]),
                              functools.partial(lambda i, c: (jnp.minimum((i + 1) * hb, t // SUBLANES - 1), c), c=rows[j][2]))
                 for j in nxt]
    out_shape = [jax.ShapeDtypeStruct((t, w), dt) for (w, dt) in out_rows]
    out_shape += [jax.ShapeDtypeStruct((SUBLANES, w), F32) for w in out_accs]
    out_specs = [pl.BlockSpec((tm, w), lambda i: (i, 0)) for (w, _) in out_rows]
    out_specs += [pl.BlockSpec((SUBLANES, w), lambda i: (0, 0)) for w in out_accs]
    args = [r[0] for r in rows] + list(consts) + [rows[j][0] for j in prev] + [rows[j][0] for j in nxt]
    res = pl.pallas_call(
        kern, name=name, grid=(nblk,), in_specs=in_specs, out_specs=out_specs, out_shape=out_shape,
        compiler_params=_PARAMS(dimension_semantics=("arbitrary",)),
    )(*args)
    return list(res[:nor]), list(res[nor:])


def _shift_down(x, halo, k):
    rolled = pltpu.roll(x, k, 0)
    row = lax.broadcasted_iota(jnp.int32, (SUBLANES, x.shape[1]), 0)
    first = jnp.where(row < k, pltpu.roll(halo, k, 0), rolled[:SUBLANES])
    if x.shape[0] == SUBLANES:
        return first
    return jnp.concatenate([first, rolled[SUBLANES:]], axis=0)


def _shift_up(x, halo, k):
    n = x.shape[0]
    rolled = pltpu.roll(x, n - k, 0)
    row = lax.broadcasted_iota(jnp.int32, (SUBLANES, x.shape[1]), 0)
    last = jnp.where(row >= SUBLANES - k, pltpu.roll(halo, SUBLANES - k, 0), rolled[n - SUBLANES:])
    if n == SUBLANES:
        return last
    return jnp.concatenate([rolled[:n - SUBLANES], last], axis=0)


class _Transposed:
    def __init__(self, w):
        self.w = w


def _t(w):
    return _Transposed(w)


def _mm(name, a, b, out_dtype=F32, trans_a=False, tm=1024, tn=1024, tk=1024, epilogue=None, extra=None):
    trans_b = isinstance(b, _Transposed)
    assert not (trans_a and trans_b)
    if trans_b:
        b = b.w
    if trans_a:
        kdim, m = a.shape
    else:
        m, kdim = a.shape
    n = b.shape[0] if trans_b else b.shape[1]
    assert b.shape[1 if trans_b else 0] == kdim, (name, a.shape, b.shape)
    tm, tn, tk = _tile(m, tm), _tile(n, tn), _tile(kdim, tk)
    nk = kdim // tk
    dims = (((0,), (0,)), ((), ())) if trans_a else (((1,), (1 if trans_b else 0,)), ((), ()))

    n_in = 2 if extra is None else 3

    def kern(*refs):
        a_ref, b_ref, o_ref, acc = refs[0], refs[1], refs[n_in], refs[n_in + 1:]

        def store(res):
            if epilogue is not None:
                res = epilogue(res) if extra is None else epilogue(res, refs[2][...])
            o_ref[...] = res.astype(o_ref.dtype)

        part = lax.dot_general(a_ref[...].astype(BF16), b_ref[...].astype(BF16), dims, preferred_element_type=F32)
        if nk == 1:
            store(part)
        else:
            k = pl.program_id(2)

            @pl.when(k == 0)
            def _():
                acc[0][...] = part

            @pl.when(k > 0)
            def _():
                acc[0][...] += part

            @pl.when(k == nk - 1)
            def _():
                store(acc[0][...])

    a_spec = pl.BlockSpec((tk, tm), lambda i, j, k: (k, i)) if trans_a else pl.BlockSpec((tm, tk), lambda i, j, k: (i, k))
    b_spec = pl.BlockSpec((tn, tk), lambda i, j, k: (j, k)) if trans_b else pl.BlockSpec((tk, tn), lambda i, j, k: (k, j))
    out_spec = pl.BlockSpec((tm, tn), lambda i, j, k: (i, j))
    return pl.pallas_call(
        kern, name=name, grid=(m // tm, n // tn, nk),
        in_specs=[a_spec, b_spec] + ([] if extra is None else [out_spec]),
        out_specs=out_spec,
        out_shape=jax.ShapeDtypeStruct((m, n), out_dtype),
        scratch_shapes=[] if nk == 1 else [pltpu.VMEM((tm, tn), F32)],
        compiler_params=_PARAMS(dimension_semantics=("parallel", "parallel", "arbitrary")),
    )(*((a, b) if extra is None else (a, b, extra)))


def _scan(name, a, b, seq, reverse=False, tm=256):
    t, d = a.shape
    tm = _tile(seq, tm)
    nblk = t // tm
    ntile = tm // SUBLANES

    def kern(a_ref, b_ref, h_ref, carry_h, carry_a):
        i = pl.program_id(0)
        blk = (nblk - 1 - i) if reverse else i
        edge = (((blk + 1) * tm) % seq == 0) if reverse else ((blk * tm) % seq == 0)

        @pl.when(edge)
        def _():
            carry_h[...] = jnp.zeros_like(carry_h)
            carry_a[...] = jnp.zeros_like(carry_a)

        def tile_step(j, c):
            jj = (ntile - 1 - j) if reverse else j
            rows = pl.ds(pl.multiple_of(jj * SUBLANES, SUBLANES), SUBLANES)
            a8 = a_ref[rows, :]
            b8 = b_ref[rows, :]
            h, an = c
            out = [None] * SUBLANES
            order = range(SUBLANES - 1, -1, -1) if reverse else range(SUBLANES)
            for r in order:
                if reverse:
                    h = b8[r:r + 1, :] + an * h
                    an = a8[r:r + 1, :]
                else:
                    h = a8[r:r + 1, :] * h + b8[r:r + 1, :]
                out[r] = h
            h_ref[rows, :] = jnp.concatenate(out, axis=0)
            return (h, an)

        h, an = lax.fori_loop(0, ntile, tile_step, (carry_h[...], carry_a[...]))
        carry_h[...] = h
        carry_a[...] = an

    idx = (lambda i: (nblk - 1 - i, 0)) if reverse else (lambda i: (i, 0))
    return pl.pallas_call(
        kern, name=name, grid=(nblk,),
        in_specs=[pl.BlockSpec((tm, d), idx), pl.BlockSpec((tm, d), idx)],
        out_specs=pl.BlockSpec((tm, d), idx),
        out_shape=jax.ShapeDtypeStruct((t, d), F32),
        scratch_shapes=[pltpu.VMEM((1, d), F32), pltpu.VMEM((1, d), F32)],
        compiler_params=_PARAMS(dimension_semantics=("arbitrary",)),
    )(a, b)


_NN = (((1,), (0,)), ((), ()))
_NT = (((1,), (1,)), ((), ()))
_TN = (((0,), (0,)), ((), ()))


def _dot1(a, b, dims):
    return lax.dot_general(a.astype(BF16), b.astype(BF16), dims, preferred_element_type=F32)


def _dot3(a, b, dims):
    a_hi, b_hi = a.astype(BF16), b.astype(BF16)
    a_lo, b_lo = (a - a_hi.astype(F32)).astype(BF16), (b - b_hi.astype(F32)).astype(BF16)
    dg = lambda p, q: lax.dot_general(p, q, dims, preferred_element_type=F32)
    return dg(a_hi, b_hi) + (dg(a_hi, b_lo) + dg(a_lo, b_hi))


def _make_bmm(dot):
    def make(dims, da_rule, db_rule):
        @jax.custom_vjp
        def f(a, b):
            return dot(a, b, dims)

        def fwd(a, b):
            return dot(a, b, dims), (a, b)

        def bwd(res, g):
            a, b = res
            return da_rule(a, b, g), db_rule(a, b, g)

        f.defvjp(fwd, bwd)
        return f

    return dict(nn=make(_NN, lambda a, b, g: dot(g, b, _NT), lambda a, b, g: dot(a, g, _TN)),
                nt=make(_NT, lambda a, b, g: dot(g, b, _NN), lambda a, b, g: dot(g, a, _TN)),
                tn=make(_TN, lambda a, b, g: dot(b, g, _NT), lambda a, b, g: dot(a, g, _NN)))


_BMM = {1: _make_bmm(_dot1), 3: _make_bmm(_dot3)}
_WKV_PASSES = dict(pair=1, read=1, inv=3, apply=1, write=1)


def _running_sum(x, reverse):
    c = x.shape[0]
    row = lax.broadcasted_iota(jnp.int32, x.shape, 0)
    k = 1
    while k < c:
        if reverse:
            x = x + jnp.where(row < c - k, pltpu.roll(x, c - k, 0), 0.0)
        else:
            x = x + jnp.where(row >= k, pltpu.roll(x, k, 0), 0.0)
        k *= 2
    return x


@jax.custom_vjp
def _cumsum_rows(x):
    return _running_sum(x, False)


_cumsum_rows.defvjp(lambda x: (_running_sum(x, False), None), lambda _, g: (_running_sum(g, True),))


@jax.custom_vjp
def _unit_lower_inverse(nl):
    c = nl[0].shape[0]
    mm = _BMM[_WKV_PASSES["inv"]]["nn"]
    eye = jnp.where(lax.broadcasted_iota(jnp.int32, (c, c), 0) == lax.broadcasted_iota(jnp.int32, (c, c), 1), 1.0, 0.0)
    inv = [eye + z for z in nl]
    p = nl
    for _ in range(c.bit_length() - 2):
        p = [mm(z, z) for z in p]
        inv = [i_ + mm(p_, i_) for i_, p_ in zip(inv, p)]
    return inv


def _unit_lower_inverse_fwd(nl):
    inv = _unit_lower_inverse(nl)
    return inv, inv


def _unit_lower_inverse_bwd(inv, g):
    mm = _BMM[_WKV_PASSES["inv"]]
    left = [mm["tn"](x, g_) for x, g_ in zip(inv, g)]
    return ([mm["nt"](l_, x) for l_, x in zip(left, inv)],)


_unit_lower_inverse.defvjp(_unit_lower_inverse_fwd, _unit_lower_inverse_bwd)


@jax.custom_vjp
def _kept_inverse(nl, inv):
    return inv


_kept_inverse.defvjp(lambda nl, inv: (inv, inv),
                     lambda inv, g: (_unit_lower_inverse_bwd(inv, g)[0], [jnp.zeros_like(x) for x in inv]))


def _wkv_chunk(r, lw, k, v, a, b, s0, kept_inv=None):
    c = r[0].shape[0]
    ti = lax.broadcasted_iota(jnp.int32, (c, 2 * c), 0)
    tj = lax.broadcasted_iota(jnp.int32, (c, 2 * c), 1)
    right = tj >= c
    tau = jnp.where(right, tj - c, tj)
    strict_left = jnp.logical_and(jnp.logical_not(right), tau < ti)[:, :c]
    strict_right = jnp.logical_and(right, tau < ti)
    incl = tau <= ti
    last = lax.broadcasted_iota(jnp.int32, r[0].shape, 0) == c - 1
    each = lambda f, *ls: [f(*z) for z in zip(*ls)]
    rows2 = lambda x, y: jnp.concatenate([x, y], axis=0)
    pair, read, inv_, apply_, write = (_BMM[_WKV_PASSES[role]] for role in ("pair", "read", "inv", "apply", "write"))
    cum = each(_cumsum_rows, lw)
    w_incl = each(jnp.exp, cum)
    w_inv = each(lambda z: jnp.exp(-z), cum)
    at = each(lambda a_, c_, l_: a_ * jnp.exp(c_ - l_), a, cum, lw)
    ar = each(rows2, at, each(jnp.multiply, r, w_incl))
    bk = each(rows2, each(jnp.multiply, b, w_inv), each(jnp.multiply, k, w_inv))
    pp = each(pair["nt"], ar, bk)
    sr = each(read["nt"], ar, s0)
    nl = [jnp.where(strict_left, z[:c, :c], 0.0) for z in pp]
    zero_v = each(lambda v_: rows2(jnp.zeros_like(v_), v_), v)
    rhs = each(lambda s, z, zv: s[:c] + apply_["nn"](jnp.where(strict_right, z[:c], 0.0), zv), sr, pp, zero_v)
    inv = _unit_lower_inverse(nl) if kept_inv is None else _kept_inverse(nl, kept_inv)
    ut = each(inv_["nn"], inv, rhs)
    uv = each(rows2, ut, v)
    y = each(lambda s, z, uv_: s[c:] + apply_["nn"](jnp.where(incl, z[c:], 0.0), uv_), sr, pp, uv)
    w_end = each(lambda z: jnp.exp(jnp.sum(jnp.where(last, z, 0.0), axis=0, keepdims=True)), cum)
    s1 = each(lambda s, uv_, bk_, w_: (s + write["tn"](uv_, bk_)) * w_, s0, uv, bk, w_end)
    return y, s1, inv


def _wkv_fwd(r, lw, k, v, a, b, seq, hb=16):
    t, d = r.shape
    n = RWKV_HEAD
    nh = d // n
    hb = min(hb, nh)
    chunk = min(WKV_CHUNK, seq)
    ncs = seq // chunk

    def kern(r_ref, lw_ref, k_ref, v_ref, a_ref, b_ref, y_ref, st_ref, inv_ref, s_scr):
        @pl.when(pl.program_id(2) == 0)
        def _():
            s_scr[...] = jnp.zeros_like(s_scr)

        heads = lambda ref: [ref[:, h * n:(h + 1) * n] for h in range(hb)]
        s0 = [s_scr[h] for h in range(hb)]
        y, s1, inv = _wkv_chunk(heads(r_ref), heads(lw_ref), heads(k_ref), heads(v_ref), heads(a_ref), heads(b_ref), s0)
        for h in range(hb):
            st_ref[0, h] = s0[h]
            inv_ref[0, h] = inv[h]
            y_ref[:, h * n:(h + 1) * n] = y[h]
            s_scr[h] = s1[h]

    vec = pl.BlockSpec((chunk, hb * n), lambda bb, g, c: (bb * ncs + c, g))
    per_chunk = lambda rows: pl.BlockSpec((1, hb, rows, rows), lambda bb, g, c: (bb * ncs + c, g, 0, 0))
    return pl.pallas_call(
        kern, name="wkv_fwd", grid=(t // seq, nh // hb, ncs), in_specs=[vec] * 6,
        out_specs=[vec, per_chunk(n), per_chunk(chunk)],
        out_shape=[jax.ShapeDtypeStruct((t, d), F32), jax.ShapeDtypeStruct((t // chunk, nh, n, n), F32),
                   jax.ShapeDtypeStruct((t // chunk, nh, chunk, chunk), F32)],
        scratch_shapes=[pltpu.VMEM((hb, n, n), F32)],
        compiler_params=_PARAMS(dimension_semantics=("parallel", "parallel", "arbitrary")),
    )(r, lw, k, v, a, b)


def _wkv_bwd(r, lw, k, v, a, b, st, inv, dy, seq, hb=16):
    t, d = r.shape
    n = RWKV_HEAD
    nh = d // n
    hb = min(hb, nh)
    chunk = min(WKV_CHUNK, seq)
    ncs = seq // chunk

    def kern(r_ref, lw_ref, k_ref, v_ref, a_ref, b_ref, st_ref, inv_ref, dy_ref,
             dr_ref, dlw_ref, dk_ref, dv_ref, da_ref, db_ref, ds_scr):
        @pl.when(pl.program_id(2) == 0)
        def _():
            ds_scr[...] = jnp.zeros_like(ds_scr)

        heads = lambda ref: [ref[:, h * n:(h + 1) * n] for h in range(hb)]
        kept = [inv_ref[0, h] for h in range(hb)]
        _, vjp = jax.vjp(lambda *args: _wkv_chunk(*args, kept_inv=kept)[:2],
                         heads(r_ref), heads(lw_ref), heads(k_ref), heads(v_ref), heads(a_ref), heads(b_ref),
                         [st_ref[0, h] for h in range(hb)])
        grads = vjp((heads(dy_ref), [ds_scr[h] for h in range(hb)]))
        for h in range(hb):
            for ref, g in zip((dr_ref, dlw_ref, dk_ref, dv_ref, da_ref, db_ref), grads[:6]):
                ref[:, h * n:(h + 1) * n] = g[h]
            ds_scr[h] = grads[6][h]

    vec = pl.BlockSpec((chunk, hb * n), lambda bb, g, c: (bb * ncs + ncs - 1 - c, g))
    per_chunk = lambda rows: pl.BlockSpec((1, hb, rows, rows), lambda bb, g, c: (bb * ncs + ncs - 1 - c, g, 0, 0))
    return pl.pallas_call(
        kern, name="wkv_bwd", grid=(t // seq, nh // hb, ncs),
        in_specs=[vec] * 6 + [per_chunk(n), per_chunk(chunk), vec],
        out_specs=[vec] * 6, out_shape=[jax.ShapeDtypeStruct((t, d), F32)] * 6,
        scratch_shapes=[pltpu.VMEM((hb, n, n), F32)],
        compiler_params=_PARAMS(dimension_semantics=("parallel", "parallel", "arbitrary")),
    )(r, lw, k, v, a, b, st, inv, dy)


def _softmax_rows(s):
    e = jnp.exp(s - jnp.max(s, axis=-1, keepdims=True))
    return e / jnp.sum(e, axis=-1, keepdims=True)


def _attn_fwd(q, kv, seq, mem_len, tq=256):
    t, d = q.shape
    dh = d // MEM_HEADS
    scale = dh ** -0.5
    tq = _tile(seq, tq)
    nq = seq // tq

    def kern(q_ref, kv_ref, o_ref):
        for h in range(MEM_HEADS):
            cols = slice(h * dh, (h + 1) * dh)
            vcols = slice(d + h * dh, d + (h + 1) * dh)
            s = lax.dot_general(q_ref[:, cols], kv_ref[:, cols], _NT, preferred_element_type=F32) * scale
            p = _softmax_rows(s)
            o_ref[:, cols] = jnp.dot(p.astype(BF16), kv_ref[:, vcols], preferred_element_type=F32).astype(o_ref.dtype)

    return pl.pallas_call(
        kern, name="attn_fwd", grid=(t // seq, nq),
        in_specs=[pl.BlockSpec((tq, d), lambda b, i: (b * nq + i, 0)), pl.BlockSpec((mem_len, 2 * d), lambda b, i: (b, 0))],
        out_specs=pl.BlockSpec((tq, d), lambda b, i: (b * nq + i, 0)),
        out_shape=jax.ShapeDtypeStruct((t, d), BF16),
        compiler_params=_PARAMS(dimension_semantics=("parallel", "parallel")),
    )(q, kv)


def _attn_bwd(q, kv, do, seq, mem_len, tq=256):
    t, d = q.shape
    dh = d // MEM_HEADS
    scale = dh ** -0.5
    tq = _tile(seq, tq)
    nq = seq // tq

    def kern(q_ref, kv_ref, do_ref, dq_ref, dkv_ref):
        @pl.when(pl.program_id(1) == 0)
        def _():
            dkv_ref[...] = jnp.zeros_like(dkv_ref)

        for h in range(MEM_HEADS):
            cols = slice(h * dh, (h + 1) * dh)
            vcols = slice(d + h * dh, d + (h + 1) * dh)
            qh, kh, vh, doh = q_ref[:, cols], kv_ref[:, cols], kv_ref[:, vcols], do_ref[:, cols]
            p = _softmax_rows(lax.dot_general(qh, kh, _NT, preferred_element_type=F32) * scale)
            dp = lax.dot_general(doh, vh, _NT, preferred_element_type=F32)
            ds = (p * (dp - jnp.sum(p * dp, axis=-1, keepdims=True)) * scale).astype(BF16)
            dq_ref[:, cols] = jnp.dot(ds, kh, preferred_element_type=F32).astype(dq_ref.dtype)
            dkv_ref[:, cols] += lax.dot_general(ds, qh, _TN, preferred_element_type=F32)
            dkv_ref[:, vcols] += lax.dot_general(p.astype(BF16), doh, _TN, preferred_element_type=F32)

    return pl.pallas_call(
        kern, name="attn_bwd", grid=(t // seq, nq),
        in_specs=[pl.BlockSpec((tq, d), lambda b, i: (b * nq + i, 0)), pl.BlockSpec((mem_len, 2 * d), lambda b, i: (b, 0)),
                  pl.BlockSpec((tq, d), lambda b, i: (b * nq + i, 0))],
        out_specs=[pl.BlockSpec((tq, d), lambda b, i: (b * nq + i, 0)), pl.BlockSpec((mem_len, 2 * d), lambda b, i: (b, 0))],
        out_shape=[jax.ShapeDtypeStruct((t, d), BF16), jax.ShapeDtypeStruct(kv.shape, F32)],
        compiler_params=_PARAMS(dimension_semantics=("parallel", "arbitrary")),
    )(q, kv, do)


def _rstd(x):
    return lax.rsqrt(jnp.mean(x * x, axis=-1, keepdims=True) + RMS_EPS)


def _rms(x, g):
    return x * _rstd(x) * g


def _rms_bwd(dy, x, g):
    rstd = _rstd(x)
    xhat = x * rstd
    dxhat = dy * g
    return rstd * (dxhat - xhat * jnp.mean(dxhat * xhat, axis=-1, keepdims=True)), dy * xhat


def _softplus(x):
    return jnp.maximum(x, 0.0) + jnp.log1p(jnp.exp(-jnp.abs(x)))


def _one_minus_exp(x):
    series = -x * (1.0 + x * (0.5 + x * (1.0 / 6.0 + x * (1.0 / 24.0 + x * (1.0 / 120.0)))))
    return jnp.where(x > -0.05, series, 1.0 - jnp.exp(x))


_GELU_C = 0.7978845608028654
_GELU_K = 0.044715


def _gelu(x):
    return 0.5 * x * (1.0 + jnp.tanh(_GELU_C * (x + _GELU_K * x * x * x)))


def _gelu_grad(x):
    th = jnp.tanh(_GELU_C * (x + _GELU_K * x * x * x))
    return 0.5 * (1.0 + th) + 0.5 * x * (1.0 - th * th) * _GELU_C * (1.0 + 3.0 * _GELU_K * x * x)


def _seg_sum(x, seg):
    def two_terms(v, dims):
        hi = v.astype(BF16)
        lo = (v - hi.astype(F32)).astype(BF16)
        return (lax.dot_general(hi, seg, dims, preferred_element_type=F32)
                + lax.dot_general(lo, seg, dims, preferred_element_type=F32))
    return two_terms(two_terms(x, _NN), _NT)


def _f32(v):
    return v.astype(F32)


def _norm_fwd(name, x, g, dtype=BF16):
    (hn,), _ = _rowwise(name, lambda rv, cv, pv, nv: ([_rms(_f32(rv[0]), cv[0])], []), [x], [g], [(x.shape[1], dtype)])
    return hn


def _resid_norm_fwd(name, x, t, g, bias=None):
    def body(rv, cv, pv, nv):
        tt = rv[1] if bias is None else rv[1] + cv[1]
        return [rv[0] + _rms(tt, cv[0])], []
    (y,), _ = _rowwise(name, body, [x, t], [g] if bias is None else [g, bias], [(x.shape[1], F32)])
    return y


def _resid_norm_bwd(name, dxn, t, g, bias=None):
    def body(rv, cv, pv, nv):
        tt = rv[1] if bias is None else rv[1] + cv[1]
        dt, dg = _rms_bwd(rv[0], tt, cv[0])
        return [dt], [dg, dt]
    d = t.shape[1]
    (dt,), (dg, db) = _rowwise(name, body, [dxn, t], [g] if bias is None else [g, bias], [(d, BF16)], [d, d])
    return dt, dg.sum(0), db.sum(0)


def _prenorm_bwd(name, dxn, dhn, x, g):
    def body(rv, cv, pv, nv):
        dx, dg = _rms_bwd(_f32(rv[1]), rv[2], cv[0])
        return [rv[0] + dx], [dg]
    d = x.shape[1]
    (dx,), (dg,) = _rowwise(name, body, [dxn, dhn, x], [g], [(d, F32)], [d])
    return dx, dg.sum(0)


def _mlp_fwd(tag, x, g_pre, g_post, w_up, w_down):
    hn = _norm_fwd(tag + "_norm", x, g_pre)
    act = _mm(tag + "_up", hn, w_up, out_dtype=BF16, epilogue=lambda up: jnp.square(jnp.maximum(up, 0.0)))
    m = _mm(tag + "_down", act, w_down)
    y = _resid_norm_fwd(tag + "_res", x, m, g_post)
    return y, (x, hn, act, m)


def _mlp_bwd(tag, saved, dy, g_pre, g_post, w_up_t, w_down_t):
    x, hn, act, m = saved
    dm, dg_post, _ = _resid_norm_bwd(tag + "_dres", dy, m, g_post)
    dup = _mm(tag + "_dup", dm, w_down_t, out_dtype=BF16, extra=act,
              epilogue=lambda dact, act_: dact * 2.0 * jnp.sqrt(_f32(act_)))
    dw_down = _mm(tag + "_dwdown", act, dm, trans_a=True)
    dw_up = _mm(tag + "_dwup", hn, dup, trans_a=True)
    dhn = _mm(tag + "_dhn", dup, w_up_t)
    dx, dg_pre = _prenorm_bwd(tag + "_dnorm", dy, dhn, x, g_pre)
    return dx, dict(g_pre=dg_pre, g_post=dg_post, w_up=dw_up, w_down=dw_down)


def _xattn_fwd(tag, x, mem_n, g_pre, g_post, w_q, w_kv, w_o, seq, mem_len):
    hn = _norm_fwd(tag + "_norm", x, g_pre)
    q = _mm(tag + "_q", hn, w_q, out_dtype=BF16)
    kv = _mm(tag + "_kv", mem_n, w_kv, out_dtype=BF16)
    o = _attn_fwd(q, kv, seq, mem_len)
    c = _mm(tag + "_o", o, w_o)
    y = _resid_norm_fwd(tag + "_res", x, c, g_post)
    return y, (x, hn, q, kv, o, c)


def _xattn_bwd(tag, saved, dy, mem_n, g_pre, g_post, w_q_t, w_kv_t, w_o_t, seq, mem_len):
    x, hn, q, kv, o, c = saved
    dc, dg_post, _ = _resid_norm_bwd(tag + "_dres", dy, c, g_post)
    do = _mm(tag + "_do", dc, w_o_t, out_dtype=BF16)
    dw_o = _mm(tag + "_dwo", o, dc, trans_a=True)
    dq, dkv = _attn_bwd(q, kv, do, seq, mem_len)
    dw_q = _mm(tag + "_dwq", hn, dq, trans_a=True)
    dhn = _mm(tag + "_dhn", dq, w_q_t)
    dw_kv = _mm(tag + "_dwkv", mem_n, dkv, trans_a=True)
    dmem_n = _mm(tag + "_dmem", dkv, w_kv_t)
    dx, dg_pre = _prenorm_bwd(tag + "_dnorm", dy, dhn, x, g_pre)
    return dx, dmem_n, dict(g_pre=dg_pre, g_post=dg_post, w_q=dw_q, w_kv=dw_kv, w_o=dw_o)


def _lru_gates(z0, z1, gb0, gb1, sp):
    r = jax.nn.sigmoid(z0 + gb0)
    i = jax.nn.sigmoid(z1 + gb1)
    log_a = -LRU_C * r * sp
    a = jnp.exp(log_a)
    mult = jnp.sqrt(_one_minus_exp(2.0 * log_a))
    return r, i, a, mult


def _rglru_fwd(x, p, seq):
    d = x.shape[1]
    hn = _norm_fwd("a_norm", x, p["g_pre"])
    proj = _mm("a_in", hn, p["w_in"])

    def conv_body(rv, cv, pv, nv):
        u = rv[0] + cv[0]
        halo = pv[0] + cv[0]
        i = pl.program_id(0)
        halo = jnp.where((i * rv[0].shape[0]) % seq == 0, jnp.zeros_like(halo), halo)
        conv = cv[2] + u * cv[1][CONV_WIDTH - 1:CONV_WIDTH]
        for tap in range(CONV_WIDTH - 1):
            conv = conv + _shift_down(u, halo, CONV_WIDTH - 1 - tap) * cv[1][tap:tap + 1]
        return [conv], []
    (conv,), _ = _rowwise("a_conv", conv_body, [(proj, d, 1)], [p["b_in_u"], p["conv_w"], p["conv_b"]], [(d, F32)],
                          prev=[0], seq=seq)
    z = _mm("a_gate", conv, p["w_gate"])

    def gate_body(rv, cv, pv, nv):
        r, i, a, mult = _lru_gates(rv[0], rv[1], cv[0], cv[1], _softplus(-cv[2]))
        return [a, mult * i * rv[2]], []
    (a, bb), _ = _rowwise("a_gates", gate_body, [(z, d, 0), (z, d, 1), conv], [p["gate_b0"], p["gate_b1"], p["lam"]],
                          [(d, F32), (d, F32)])
    h = _scan("a_scan", a, bb, seq)

    def hy_body(rv, cv, pv, nv):
        return [rv[0] * _gelu(rv[1] + cv[0])], []
    (hy,), _ = _rowwise("a_hy", hy_body, [h, (proj, d, 0)], [p["b_in_y"]], [(d, BF16)])
    out = _mm("a_out", hy, p["w_out"])
    y = _resid_norm_fwd("a_res", x, out, p["g_post"], bias=p["b_out"])
    return y, (x, hn, proj, conv, z, a, h, hy, out)


def _rglru_bwd(saved, dy, p, pt, seq):
    x, hn, proj, conv, z, a, h, hy, out = saved
    d = x.shape[1]
    dt, dg_post, db_out = _resid_norm_bwd("a_dres", dy, out, p["g_post"], bias=p["b_out"])
    dhy = _mm("a_dhy", dt, pt["w_out_t"])
    dw_out = _mm("a_dwout", hy, dt, trans_a=True)

    def dh_body(rv, cv, pv, nv):
        yb = rv[2] + cv[0]
        return [rv[0] * _gelu(yb), rv[0] * rv[1] * _gelu_grad(yb)], []
    (dh, dyb), _ = _rowwise("a_dh", dh_body, [dhy, h, (proj, d, 0)], [p["b_in_y"]], [(d, F32), (d, BF16)])
    g = _scan("a_rscan", a, dh, seq, reverse=True)

    def dgate_body(rv, cv, pv, nv):
        gg, hh, z0, z1, cnv = rv
        sp = _softplus(-cv[2])
        r, i, aa, mult = _lru_gates(z0, z1, cv[0], cv[1], sp)
        i_blk = pl.program_id(0)
        halo = jnp.where((i_blk * gg.shape[0]) % seq == 0, jnp.zeros_like(pv[0]), pv[0])
        da = gg * _shift_down(hh, halo, 1)
        dmult = gg * i * cnv
        di = gg * mult * cnv
        dconv = gg * mult * i
        dlog_a = da * aa - dmult * aa * aa / mult
        dz0 = dlog_a * (-LRU_C * sp) * r * (1.0 - r)
        dz1 = di * i * (1.0 - i)
        dsp = dlog_a * (-LRU_C * r)
        dlam = dsp * (-jax.nn.sigmoid(-cv[2]))
        return [jnp.concatenate([dz0, dz1], axis=1), dconv], [dz0, dz1, dlam]
    (dz, dconv1), (dgb0, dgb1, dlam) = _rowwise(
        "a_dgates", dgate_body, [g, h, (z, d, 0), (z, d, 1), conv], [p["gate_b0"], p["gate_b1"], p["lam"]],
        [(2 * d, BF16), (d, F32)], [d, d, d], prev=[1], seq=seq)
    dconv2 = _mm("a_dconv", dz, pt["w_gate_t"])
    dw_gate = _mm("a_dwgate", conv, dz, trans_a=True)

    def dconv_body(rv, cv, pv, nv):
        dc1, dc2, pu, dyb_ = rv
        dc = dc1 + dc2
        dc_next = nv[0] + nv[1]
        u = pu + cv[0]
        i_blk = pl.program_id(0)
        halo = jnp.where((i_blk * u.shape[0]) % seq == 0, jnp.zeros_like(pv[0]), pv[0] + cv[0])
        du = dc * cv[1][CONV_WIDTH - 1:CONV_WIDTH]
        dws = []
        for tap in range(CONV_WIDTH - 1):
            k = CONV_WIDTH - 1 - tap
            du = du + _shift_up(dc, dc_next, k) * cv[1][tap:tap + 1]
            dws.append(dc * _shift_down(u, halo, k))
        dws.append(dc * u)
        return [jnp.concatenate([_f32(dyb_), du], axis=1)], dws + [dc, _f32(dyb_), du]
    (dproj,), accs = _rowwise(
        "a_dconvw", dconv_body, [dconv1, dconv2, (proj, d, 1), dyb], [p["b_in_u"], p["conv_w"]],
        [(2 * d, BF16)], [d] * (CONV_WIDTH + 3), prev=[2], nxt=[0, 1], seq=seq)
    dconv_w = jnp.stack([acc.sum(0) for acc in accs[:CONV_WIDTH]])
    dconv_b = accs[CONV_WIDTH].sum(0)
    db_in = jnp.concatenate([accs[CONV_WIDTH + 1].sum(0), accs[CONV_WIDTH + 2].sum(0)])
    dhn = _mm("a_dhn", dproj, pt["w_in_t"])
    dw_in = _mm("a_dwin", hn, dproj, trans_a=True)
    dx, dg_pre = _prenorm_bwd("a_dnorm", dy, dhn, x, p["g_pre"])
    grads = dict(g_pre=dg_pre, g_post=dg_post, b_out=db_out, w_out=dw_out, gate_b0=dgb0.sum(0), gate_b1=dgb1.sum(0),
                 lam=dlam.sum(0), w_gate=dw_gate, conv_w=dconv_w, conv_b=dconv_b, b_in=db_in, w_in=dw_in)
    return dx, grads


def _rwkv_prep(k, wl, za, w0, a0, k_k, k_a, seg):
    w_in = wl + w0
    e_w = jnp.exp(-_softplus(-w_in) - 0.5)
    a = jax.nn.sigmoid(za + a0)
    q = k * k_k
    norm = jnp.sqrt(_seg_sum(q * q, seg))
    n = jnp.maximum(norm, 1e-12)
    kk = q / n
    return w_in, e_w, a, norm, n, kk


def _rwkv_out(y, r, k2, v, gn_g, gn_b, r_k, seg):
    inv = 1.0 / RWKV_HEAD
    yc = y - _seg_sum(y, seg) * inv
    rstd = lax.rsqrt(_seg_sum(yc * yc, seg) * inv + RWKV_GN_EPS)
    yhat = yc * rstd
    s = _seg_sum(r * k2 * r_k, seg)
    return rstd, yhat, s, yhat * gn_g + gn_b + s * v


def _rwkv_fwd(x, p, seq):
    t, d = x.shape
    nseq = t // seq

    def mix_body(rv, cv, pv, nv):
        hn = _rms(rv[0], cv[0])
        xx = _shift_down(hn, _rms(pv[0], cv[0]), 1) - hn
        return [hn] + [hn + xx * cv[1][c:c + 1] for c in range(6)], []
    (hn, xr, xw, xk, xv, xa, xg), _ = _rowwise("b_mix", mix_body, [x], [p["g_pre"], p["mu"]],
                                               [(d, F32)] + [(d, BF16)] * 6, prev=[0], seq=seq)
    r = _mm("b_r", xr, p["w_r"])
    k = _mm("b_k", xk, p["w_k"])
    v = _mm("b_v", xv, p["w_v"])
    lw = _mm("b_w1", xw, p["w1"])
    la = _mm("b_a1", xa, p["a1"], out_dtype=BF16)
    lg = _mm("b_g1", xg, p["g1"])
    (th,), _ = _rowwise("b_tanh", lambda rv, cv, pv, nv: ([jnp.tanh(rv[0])], []), [lw], [], [(lw.shape[1], BF16)])
    (sg,), _ = _rowwise("b_sig", lambda rv, cv, pv, nv: ([jax.nn.sigmoid(rv[0])], []), [lg], [], [(lg.shape[1], BF16)])
    wl = _mm("b_w2", th, p["w2"])
    za = _mm("b_a2", la, p["a2"])
    g = _mm("b_g2", sg, p["g2"])

    def prep_body(rv, cv, pv, nv):
        kk_, wl_, za_ = rv
        _, e_w, a, _, _, kk = _rwkv_prep(kk_, wl_, za_, cv[0], cv[1], cv[2], cv[3], cv[4])
        return [-e_w, kk_ * (1.0 + (a - 1.0) * cv[3]), -kk, kk * a], []
    (log_w, k2, rem_a, rem_b), _ = _rowwise("b_prep", prep_body, [k, wl, za],
                                            [p["w0"], p["a0"], p["k_k"], p["k_a"], p["seg"]], [(d, F32)] * 4, tm=256)
    rec_in = (r, log_w, k2, v, rem_a, rem_b)
    y, *states = _wkv_fwd(*rec_in, seq)

    def out_body(rv, cv, pv, nv):
        y_, r_, k2_, v_, g_ = rv
        _, _, _, out = _rwkv_out(y_, r_, k2_, v_, cv[0], cv[1], cv[2], cv[3])
        return [out * g_], []
    (og,), _ = _rowwise("b_out", out_body, [y, r, k2, v, g], [p["gn_g"], p["gn_b"], p["r_k"], p["seg"]], [(d, BF16)], tm=256)
    o = _mm("b_o", og, p["w_o"])
    res = _resid_norm_fwd("b_res", x, o, p["g_post"])
    return res, (x, hn, xr, xw, xk, xv, xa, xg, r, k, v, th, la, sg, wl, za, g, k2, rec_in, states, y, og, o)


def _rwkv_bwd(saved, dres, p, pt, seq):
    x, hn, xr, xw, xk, xv, xa, xg, r, k, v, th, la, sg, wl, za, g, k2, rec_in, states, y, og, o = saved
    t, d = x.shape
    nseq = t // seq
    do, dg_post, _ = _resid_norm_bwd("b_dres", dres, o, p["g_post"])
    dog = _mm("b_dog", do, pt["w_o_t"])
    dw_o = _mm("b_dwo", og, do, trans_a=True)

    def dout_body(rv, cv, pv, nv):
        dog_, y_, r_, k2_, v_, g_ = rv
        gn_g, gn_b, r_k, bd = cv
        inv = 1.0 / RWKV_HEAD
        rstd, yhat, s, out = _rwkv_out(y_, r_, k2_, v_, gn_g, gn_b, r_k, bd)
        dout = dog_ * g_
        ds = _seg_sum(dout * v_, bd)
        dyhat = dout * gn_g
        dy = rstd * (dyhat - _seg_sum(dyhat, bd) * inv - yhat * _seg_sum(dyhat * yhat, bd) * inv)
        return [dy, dog_ * out, dout * s, ds * k2_ * r_k, ds * r_ * r_k], [ds * r_ * k2_, dout * yhat, dout]
    (dy, dgate, dv_b, dr_b, dk2_b), (dr_k, dgn_g, dgn_b) = _rowwise(
        "b_dout", dout_body, [dog, y, r, k2, v, g], [p["gn_g"], p["gn_b"], p["r_k"], p["seg"]],
        [(d, F32), (d, BF16), (d, F32), (d, F32), (d, F32)], [d, d, d], tm=256)
    dr_rec, dlw_rec, dk2_rec, dv_rec, da_rec, db_rec = _wkv_bwd(*rec_in, *states, dy, seq)

    def dprep_body(rv, cv, pv, nv):
        dr_rec_, dlw_rec_, dk2_rec_, dv_rec_, da_rec_, db_rec_, dr_b_, dk2_b_, dv_b_, k_, wl_, za_ = rv
        w0, a0, k_k, k_a, bd = cv
        w_in, e_w, a, norm, n, kk = _rwkv_prep(k_, wl_, za_, w0, a0, k_k, k_a, bd)
        dk2 = dk2_rec_ + dk2_b_
        dkk = db_rec_ * a - da_rec_
        da = db_rec_ * kk + dk2 * k_ * k_a
        dq = jnp.where(norm > 1e-12, dkk - kk * _seg_sum(kk * dkk, bd), dkk) / n
        dk = dk2 * (1.0 + (a - 1.0) * k_a) + dq * k_k
        dza = da * a * (1.0 - a)
        dwl = dlw_rec_ * (-e_w) * jax.nn.sigmoid(-w_in)
        return [dr_rec_ + dr_b_, dk, dv_rec_ + dv_b_, dza, dwl], [dk2 * k_ * (a - 1.0), dq * k_, dza, dwl]
    (dr, dk, dv, dza, dwl), (dk_a, dk_k, da0, dw0) = _rowwise(
        "b_dprep", dprep_body, [dr_rec, dlw_rec, dk2_rec, dv_rec, da_rec, db_rec, dr_b, dk2_b, dv_b, k, wl, za],
        [p["w0"], p["a0"], p["k_k"], p["k_a"], p["seg"]], [(d, BF16)] * 5, [d] * 4, tm=256)

    dw_r = _mm("b_dwr", xr, dr, trans_a=True)
    dw_k = _mm("b_dwk", xk, dk, trans_a=True)
    dw_v = _mm("b_dwv", xv, dv, trans_a=True)
    dxr = _mm("b_dxr", dr, pt["w_r_t"])
    dxk = _mm("b_dxk", dk, pt["w_k_t"])
    dxv = _mm("b_dxv", dv, pt["w_v_t"])
    da2 = _mm("b_da2", la, dza, trans_a=True)
    dla = _mm("b_dla", dza, pt["a2_t"], out_dtype=BF16)
    da1 = _mm("b_da1", xa, dla, trans_a=True)
    dxa = _mm("b_dxa", dla, pt["a1_t"])
    dw2 = _mm("b_dw2", th, dwl, trans_a=True)
    dth = _mm("b_dth", dwl, pt["w2_t"])
    (dzw,), _ = _rowwise("b_dtanh", lambda rv, cv, pv, nv: ([rv[0] * (1.0 - _f32(rv[1]) * _f32(rv[1]))], []),
                         [dth, th], [], [(th.shape[1], BF16)])
    dw1 = _mm("b_dw1", xw, dzw, trans_a=True)
    dxw = _mm("b_dxw", dzw, pt["w1_t"])
    dg2 = _mm("b_dg2", sg, dgate, trans_a=True)
    dsg = _mm("b_dsg", dgate, pt["g2_t"])
    (dzg,), _ = _rowwise("b_dsig", lambda rv, cv, pv, nv: ([rv[0] * _f32(rv[1]) * (1.0 - _f32(rv[1]))], []),
                         [dsg, sg], [], [(sg.shape[1], BF16)])
    dg1 = _mm("b_dg1", xg, dzg, trans_a=True)
    dxg = _mm("b_dxg", dzg, pt["g1_t"])

    def dmix_body(rv, cv, pv, nv):
        hn_ = rv[0]
        dxs = rv[1:]
        mu = cv[0]
        xx = _shift_down(hn_, pv[0], 1) - hn_
        dsum = dxs[0]
        dxx = dxs[0] * mu[0:1]
        dxx_next = nv[0] * mu[0:1]
        for c in range(1, 6):
            dsum = dsum + dxs[c]
            dxx = dxx + dxs[c] * mu[c:c + 1]
            dxx_next = dxx_next + nv[c] * mu[c:c + 1]
        return [dsum - dxx + _shift_up(dxx, dxx_next, 1)], [dxs[c] * xx for c in range(6)]
    (dhn,), dmu = _rowwise("b_dmix", dmix_body, [hn, dxr, dxw, dxk, dxv, dxa, dxg], [p["mu"]], [(d, F32)], [d] * 6,
                           prev=[0], nxt=[1, 2, 3, 4, 5, 6], seq=seq, tm=256)
    dx, dg_pre = _prenorm_bwd("b_dnorm", dres, dhn, x, p["g_pre"])
    grads = dict(g_pre=dg_pre, g_post=dg_post, mu=jnp.stack([m.sum(0) for m in dmu]), w_r=dw_r, w_k=dw_k, w_v=dw_v,
                 w0=dw0.sum(0), w1=dw1, w2=dw2, a0=da0.sum(0), a1=da1, a2=da2, g1=dg1, g2=dg2, k_k=dk_k.sum(0),
                 k_a=dk_a.sum(0), r_k=dr_k.sum(0), gn_g=dgn_g.sum(0), gn_b=dgn_b.sum(0), w_o=dw_o)
    return dx, grads


_WEIGHTS = ['ln_gains', 'mem_norm', 'a_conv_w', 'a_conv_b', 'a_w_in', 'a_b_in', 'a_gate_w', 'a_gate_b', 'a_lambda', 'a_w_out',
            'a_b_out', 'b_mu', 'b_w_rkv', 'b_w0', 'b_w1', 'b_w2', 'b_a0', 'b_a1', 'b_a2', 'b_g1', 'b_g2', 'b_k_k', 'b_k_a',
            'b_r_k', 'b_gn_g', 'b_gn_b', 'b_w_o', 'c_w_q', 'c_w_kv', 'c_w_o', 'm_w_up', 'm_w_down']
_SHARD_AXIS = dict(ln_gains=2, mem_norm=None, a_conv_w=2, a_conv_b=None, a_w_in=2, a_b_in=None, a_gate_w=3, a_gate_b=3,
                   a_lambda=None, a_w_out=1, a_b_out=None, b_mu=2, b_w_rkv=2, b_w0=1, b_w1=1, b_w2=2, b_a0=1, b_a1=1, b_a2=2,
                   b_g1=1, b_g2=2, b_k_k=1, b_k_a=1, b_r_k=None, b_gn_g=1, b_gn_b=1, b_w_o=1, c_w_q=1, c_w_kv=2, c_w_o=1,
                   m_w_up=2, m_w_down=1)
_MATRICES = ['a_w_in', 'a_gate_w', 'a_w_out', 'b_w_rkv', 'b_w1', 'b_w2', 'b_a1', 'b_a2', 'b_g1', 'b_g2', 'b_w_o', 'c_w_q',
             'c_w_kv', 'c_w_o', 'm_w_up', 'm_w_down']
_SHARDED = [n for n in _WEIGHTS if _SHARD_AXIS[n] is not None]
_VECTORS = [n for n in _SHARDED if n not in _MATRICES]
_REPLICATED = [n for n in _WEIGHTS if _SHARD_AXIS[n] is None]
N_XY = 4
N_DEV = 8
PACK_W = 1024
PACK_ROWS = 256


def _pack(arrs, dtype, row_mult=PACK_ROWS):
    parts = []
    rows = 0
    for a in arrs:
        n = a.size
        r = -(-n // PACK_W)
        parts.append(jnp.pad(a.reshape(-1).astype(dtype), (0, r * PACK_W - n)))
        rows += r
    pad_rows = -(-rows // row_mult) * row_mult - rows
    if pad_rows:
        parts.append(jnp.zeros((pad_rows * PACK_W,), dtype))
    return jnp.concatenate(parts).reshape(-1, PACK_W)


def _unpack(flat, shapes):
    out = []
    row = 0
    for shp in shapes:
        n = 1
        for s in shp:
            n *= s
        r = -(-n // PACK_W)
        out.append(flat[row:row + r].reshape(-1)[:n].reshape(shp))
        row += r
    return out


_ANY = pl.BlockSpec(memory_space=pl.ANY)


def _xy_peers():
    x, y = lax.axis_index("x"), lax.axis_index("y")
    return [(1 - x, y), (x, 1 - y), (1 - x, 1 - y)]


def _all_gather_xy(wm, wv):
    half = wm.shape[0] // 2

    def body(wm_ref, wv_ref, gm_ref, gv_ref, send_sems, recv_sems, local_sems):
        x, y, c = lax.axis_index("x"), lax.axis_index("y"), lax.axis_index("c")
        me = 2 * x + y
        mine = pl.ds(pl.multiple_of(c * half, SUBLANES), half)
        other = pl.ds(pl.multiple_of((1 - c) * half, SUBLANES), half)
        local = [pltpu.make_async_copy(wm_ref, gm_ref.at[me], local_sems.at[0]),
                 pltpu.make_async_copy(wv_ref, gv_ref.at[me], local_sems.at[1])]
        for cp in local:
            cp.start()
        sends, lands, passes, from_sibling = [], [], [], []
        for j, (px, py) in enumerate(_xy_peers()):
            peer = 2 * px + py
            ici = functools.partial(pltpu.make_async_remote_copy, device_id=(px, py, c), device_id_type=MESH)
            sends.append(ici(src_ref=wm_ref.at[mine], dst_ref=gm_ref.at[me, mine], send_sem=send_sems.at[j], recv_sem=recv_sems.at[j]))
            lands.append(ici(src_ref=wm_ref.at[mine], dst_ref=gm_ref.at[peer, mine], send_sem=send_sems.at[j], recv_sem=recv_sems.at[j]))
            sends.append(ici(src_ref=wv_ref, dst_ref=gv_ref.at[me], send_sem=send_sems.at[3 + j], recv_sem=recv_sems.at[3 + j]))
            lands.append(ici(src_ref=wv_ref, dst_ref=gv_ref.at[peer], send_sem=send_sems.at[3 + j], recv_sem=recv_sems.at[3 + j]))
            d2d = functools.partial(pltpu.make_async_remote_copy, send_sem=send_sems.at[6 + j], recv_sem=recv_sems.at[6 + j],
                                    device_id=(x, y, 1 - c), device_id_type=MESH)
            passes.append(d2d(src_ref=gm_ref.at[peer, mine], dst_ref=gm_ref.at[peer, mine]))
            from_sibling.append(d2d(src_ref=gm_ref.at[peer, other], dst_ref=gm_ref.at[peer, other]))
        for cp in sends:
            cp.start()
        for j in range(N_XY - 1):
            lands[2 * j].wait_recv()
            passes[j].start()
        for j in range(N_XY - 1):
            lands[2 * j + 1].wait_recv()
        for cp in from_sibling:
            cp.wait_recv()
        for cp in sends + passes:
            cp.wait_send()
        for cp in local:
            cp.wait()

    return pl.pallas_call(
        body, name="all_gather_weights",
        in_specs=[_ANY, _ANY], out_specs=[_ANY, _ANY],
        out_shape=[jax.ShapeDtypeStruct((N_XY,) + wm.shape, wm.dtype), jax.ShapeDtypeStruct((N_XY,) + wv.shape, wv.dtype)],
        scratch_shapes=[pltpu.SemaphoreType.DMA((9,)), pltpu.SemaphoreType.DMA((9,)), pltpu.SemaphoreType.DMA((2,))],
    )(wm, wv)


_HBM = pl.BlockSpec(memory_space=pltpu.HBM)
_SEM = pl.BlockSpec(memory_space=pltpu.SEMAPHORE)
_SPLIT_COPY = functools.partial(pltpu.CompilerParams, has_side_effects=pltpu.SideEffectType.DATAFLOW_SIDE_EFFECTING)


def _gather_xy_start(buf, after):
    def body(src_ref, land_ref, after_ref, send_sems, recv_sems, src_thru, land_thru, token):
        x, y, c = lax.axis_index("x"), lax.axis_index("y"), lax.axis_index("c")
        for j, (px, py) in enumerate(_xy_peers()):
            pltpu.make_async_remote_copy(src_ref=src_ref, dst_ref=land_ref.at[2 * x + y], send_sem=send_sems.at[j],
                                         recv_sem=recv_sems.at[j], device_id=(px, py, c), device_id_type=MESH).start()
        token[...] = jnp.zeros_like(token)

    n_peers = N_XY - 1
    land = pltpu.with_memory_space_constraint(lax.empty((N_XY,) + buf.shape, buf.dtype), pltpu.HBM)
    return pl.pallas_call(
        body, name="gather_late_start",
        out_shape=(pltpu.SemaphoreType.DMA((n_peers,)), pltpu.SemaphoreType.DMA((n_peers,)), pltpu.HBM(buf.shape, buf.dtype),
                   pltpu.HBM(land.shape, buf.dtype), jax.ShapeDtypeStruct((SUBLANES, LANES), F32)),
        in_specs=(_HBM, _HBM, _ANY), out_specs=(_SEM, _SEM, _HBM, _HBM, pl.BlockSpec(memory_space=pltpu.VMEM)),
        input_output_aliases={0: 2, 1: 3}, compiler_params=_SPLIT_COPY(),
    )(pltpu.with_memory_space_constraint(buf, pltpu.HBM), land, after)


def _gather_xy_wait(send_sems, recv_sems, src_thru, land_thru, after):
    def body(src_ref, land_ref, send_sems, recv_sems, after_ref, src_dead, got_ref):
        c = lax.axis_index("c")
        for j, (px, py) in enumerate(_xy_peers()):
            cp = pltpu.make_async_remote_copy(src_ref=src_ref, dst_ref=land_ref.at[2 * px + py], send_sem=send_sems.at[j],
                                              recv_sem=recv_sems.at[j], device_id=(px, py, c), device_id_type=MESH)
            cp.wait_send()
            cp.wait_recv()

    return pl.pallas_call(
        body, name="gather_late_wait",
        out_shape=(pltpu.HBM(src_thru.shape, src_thru.dtype), pltpu.HBM(land_thru.shape, land_thru.dtype)),
        in_specs=(_HBM, _HBM, _SEM, _SEM, _ANY), out_specs=(_HBM, _HBM),
        input_output_aliases={0: 0, 1: 1}, compiler_params=_SPLIT_COPY(),
    )(src_thru, land_thru, send_sems, recv_sems, after)[1]


def _exchange_xy_start(gsend, after):
    def body(src_ref, land_ref, after_ref, send_sems, recv_sems, src_thru, land_thru, token):
        c = lax.axis_index("c")
        for j, (px, py) in enumerate(_xy_peers()):
            pltpu.make_async_remote_copy(src_ref=src_ref.at[2 * px + py], dst_ref=land_ref.at[j], send_sem=send_sems.at[j],
                                         recv_sem=recv_sems.at[j], device_id=(px, py, c), device_id_type=MESH).start()
        token[...] = jnp.zeros_like(token)

    n_peers = N_XY - 1
    land = pltpu.with_memory_space_constraint(lax.empty((n_peers,) + gsend.shape[1:], gsend.dtype), pltpu.HBM)
    return pl.pallas_call(
        body, name="exchange_early_start",
        out_shape=(pltpu.SemaphoreType.DMA((n_peers,)), pltpu.SemaphoreType.DMA((n_peers,)), pltpu.HBM(gsend.shape, gsend.dtype),
                   pltpu.HBM(land.shape, gsend.dtype), jax.ShapeDtypeStruct((SUBLANES, LANES), F32)),
        in_specs=(_HBM, _HBM, _ANY), out_specs=(_SEM, _SEM, _HBM, _HBM, pl.BlockSpec(memory_space=pltpu.VMEM)),
        input_output_aliases={0: 2, 1: 3}, compiler_params=_SPLIT_COPY(),
    )(pltpu.with_memory_space_constraint(gsend, pltpu.HBM), land, after)


def _exchange_xy_wait(send_sems, recv_sems, src_thru, land_thru, after):
    def body(src_ref, land_ref, send_sems, recv_sems, after_ref, src_dead, got_ref):
        c = lax.axis_index("c")
        for j, (px, py) in enumerate(_xy_peers()):
            cp = pltpu.make_async_remote_copy(src_ref=src_ref.at[2 * px + py], dst_ref=land_ref.at[j], send_sem=send_sems.at[j],
                                              recv_sem=recv_sems.at[j], device_id=(px, py, c), device_id_type=MESH)
            cp.wait_send()
            cp.wait_recv()

    return pl.pallas_call(
        body, name="exchange_early_wait",
        out_shape=(pltpu.HBM(src_thru.shape, src_thru.dtype), pltpu.HBM(land_thru.shape, land_thru.dtype)),
        in_specs=(_HBM, _HBM, _SEM, _SEM, _ANY), out_specs=(_HBM, _HBM),
        input_output_aliases={0: 0, 1: 1}, compiler_params=_SPLIT_COPY(),
    )(src_thru, land_thru, send_sems, recv_sems, after)[1]


def _exchange_xy(gsend):
    def body(gs_ref, recv_ref, send_sems, recv_sems):
        c = lax.axis_index("c")
        sends = []
        for j, (px, py) in enumerate(_xy_peers()):
            sends.append(pltpu.make_async_remote_copy(
                src_ref=gs_ref.at[2 * px + py], dst_ref=recv_ref.at[j], send_sem=send_sems.at[j], recv_sem=recv_sems.at[j],
                device_id=(px, py, c), device_id_type=MESH))
        for cp in sends:
            cp.start()
        for cp in sends:
            cp.wait_recv()
        for cp in sends:
            cp.wait_send()

    return pl.pallas_call(
        body, name="exchange_grads",
        in_specs=[_ANY], out_specs=_ANY,
        out_shape=jax.ShapeDtypeStruct((N_XY - 1,) + gsend.shape[1:], gsend.dtype),
        scratch_shapes=[pltpu.SemaphoreType.DMA((3,)), pltpu.SemaphoreType.DMA((3,))],
    )(gsend)


def _swap_with_sibling(name, part):
    def body(p_ref, got_ref, send_sem, recv_sem):
        x, y, c = lax.axis_index("x"), lax.axis_index("y"), lax.axis_index("c")
        cp = pltpu.make_async_remote_copy(src_ref=p_ref, dst_ref=got_ref, send_sem=send_sem, recv_sem=recv_sem,
                                          device_id=(x, y, 1 - c), device_id_type=MESH)
        cp.start()
        cp.wait_recv()
        cp.wait_send()

    return pl.pallas_call(
        body, name=name,
        in_specs=[_ANY], out_specs=_ANY, out_shape=jax.ShapeDtypeStruct(part.shape, part.dtype),
        scratch_shapes=[pltpu.SemaphoreType.DMA, pltpu.SemaphoreType.DMA],
    )(part)


def _all_gather_all(vec):
    def body(v_ref, out_ref, send_sems, recv_sems, local_sem):
        x, y, c = lax.axis_index("x"), lax.axis_index("y"), lax.axis_index("c")
        me = 4 * x + 2 * y + c
        local = pltpu.make_async_copy(v_ref, out_ref.at[me], local_sem)
        local.start()
        sends, recvs = [], []
        for f in range(1, N_DEV):
            fx, fy, fc = (f >> 2) & 1, (f >> 1) & 1, f & 1
            px = (1 - x) if fx else x
            py = (1 - y) if fy else y
            pc = (1 - c) if fc else c
            mk = functools.partial(pltpu.make_async_remote_copy, src_ref=v_ref, send_sem=send_sems.at[f - 1],
                                   recv_sem=recv_sems.at[f - 1], device_id=(px, py, pc), device_id_type=MESH)
            sends.append(mk(dst_ref=out_ref.at[me]))
            recvs.append(mk(dst_ref=out_ref.at[4 * px + 2 * py + pc]))
        for cp in sends:
            cp.start()
        for cp in recvs:
            cp.wait_recv()
        for cp in sends:
            cp.wait_send()
        local.wait()

    return pl.pallas_call(
        body, name="all_gather_replicated",
        in_specs=[_ANY], out_specs=_ANY, out_shape=jax.ShapeDtypeStruct((N_DEV,) + vec.shape, vec.dtype),
        scratch_shapes=[pltpu.SemaphoreType.DMA((N_DEV - 1,)), pltpu.SemaphoreType.DMA((N_DEV - 1,)), pltpu.SemaphoreType.DMA],
    )(vec)


def _adamw(g, w, m, v):
    m2 = ADAM_B1 * m + (1.0 - ADAM_B1) * g
    v2 = ADAM_B2 * v + (1.0 - ADAM_B2) * g * g
    m_hat = m2 / (1.0 - ADAM_B1 ** ADAM_STEP)
    v_hat = v2 / (1.0 - ADAM_B2 ** ADAM_STEP)
    return -ADAM_LR * (m_hat / (jnp.sqrt(v_hat) + ADAM_EPS) + ADAM_WD * w), m2, v2


def _sum_contributions(name, own, recv):
    def body(rv, cv, pv, nv):
        return [((rv[0] + _f32(rv[1])) + _f32(rv[2])) + _f32(rv[3])], []
    stacked = recv.reshape(-1, PACK_W)
    (part,), _ = _rowwise(name, body, [own] + [(stacked, PACK_W, 0, j * own.shape[0]) for j in range(N_XY - 1)], [],
                          [(PACK_W, F32)])
    return part


def _adamw_sharded(name, part, sib, first, w, m, v, w_first=0, n_rows=None):
    def body(rv, cv, pv, nv):
        g = rv[3] + rv[4]
        return [g, *_adamw(g, rv[0], rv[1], rv[2])], []
    n_rows = w.shape[0] if n_rows is None else n_rows
    rows = [(z, PACK_W, 0, w_first) for z in (w, m, v)] + [(part, PACK_W, 0, first), (sib, PACK_W, 0, first)]
    outs, _ = _rowwise(name, body, rows, [], [(PACK_W, F32)] * 4, tm=256, n_rows=n_rows)
    return outs


def _adamw_replicated(parts, w, m, v):
    def body(rv, cv, pv, nv):
        g = rv[0]
        for i in range(1, N_DEV):
            g = g + rv[i]
        return [g, *_adamw(g, rv[N_DEV], rv[N_DEV + 1], rv[N_DEV + 2])], []
    outs, _ = _rowwise("adamw_replicated", body, [parts[i] for i in range(N_DEV)] + [w, m, v], [], [(PACK_W, F32)] * 4)
    return outs


def _row(v):
    return v.reshape(1, -1).astype(F32)


def _gate_dense(gate_w):
    _, nh, blk, _ = gate_w.shape
    d = nh * blk
    dense = jnp.zeros((d, 2 * d), gate_w.dtype)
    for g in range(2):
        for h in range(nh):
            dense = lax.dynamic_update_slice(dense, gate_w[g, h], (h * blk, g * d + h * blk))
    return dense


def _gate_blocks(dense, nh):
    d = dense.shape[0]
    blk = d // nh
    return jnp.stack([jnp.stack([dense[h * blk:(h + 1) * blk, g * d + h * blk:g * d + (h + 1) * blk] for h in range(nh)])
                      for g in range(2)])


def _local_step(x3, mem3, target3, fw, late_weights=None, on_layer1_grads=None):
    nseq, seq, d = x3.shape
    mem_len = mem3.shape[1]
    t = nseq * seq
    x0 = x3.reshape(t, d)
    mem2 = mem3.reshape(nseq * mem_len, d)
    target = target3.reshape(t, d)
    ln = fw["ln_gains"]
    gains = [[_row(ln[i, j]) for j in range(6)] for i in range(2)]
    nh = d // RWKV_HEAD
    seg = (jnp.arange(d)[:, None] // RWKV_HEAD == jnp.arange(LANES)[None, :]).astype(BF16)

    w_gate = _gate_dense(fw["a_gate_w"][0])
    pa = dict(g_pre=gains[0][0], g_post=gains[0][1], w_in=fw["a_w_in"][0], b_in_y=_row(fw["a_b_in"][0, :d]),
              b_in_u=_row(fw["a_b_in"][0, d:]), conv_w=fw["a_conv_w"][0].astype(F32), conv_b=_row(fw["a_conv_b"][0]),
              w_gate=w_gate, gate_b0=_row(fw["a_gate_b"][0, 0]), gate_b1=_row(fw["a_gate_b"][0, 1]), lam=_row(fw["a_lambda"][0]),
              w_out=fw["a_w_out"][0], b_out=_row(fw["a_b_out"][0]))
    pta = dict(w_in_t=_t(pa["w_in"]), w_gate_t=_t(w_gate), w_out_t=_t(pa["w_out"]))
    mem_g = _row(fw["mem_norm"])

    mem_n = _norm_fwd("mem_norm", mem2, mem_g)
    x1, sv_a = _rglru_fwd(x0, pa, seq)
    x2, sv_c0 = _xattn_fwd("c0", x1, mem_n, gains[0][2], gains[0][3], fw["c_w_q"][0], fw["c_w_kv"][0], fw["c_w_o"][0], seq, mem_len)
    x3_, sv_m0 = _mlp_fwd("m0", x2, gains[0][4], gains[0][5], fw["m_w_up"][0], fw["m_w_down"][0])
    if late_weights is not None:
        fw = {**fw, **late_weights(x3_)}
    pb = dict(g_pre=gains[1][0], g_post=gains[1][1], mu=fw["b_mu"][0].astype(F32), w_r=fw["b_w_rkv"][0, 0],
              w_k=fw["b_w_rkv"][0, 1], w_v=fw["b_w_rkv"][0, 2], w0=_row(fw["b_w0"][0]), w1=fw["b_w1"][0], w2=fw["b_w2"][0],
              a0=_row(fw["b_a0"][0]), a1=fw["b_a1"][0], a2=fw["b_a2"][0], g1=fw["b_g1"][0], g2=fw["b_g2"][0],
              k_k=_row(fw["b_k_k"][0]), k_a=_row(fw["b_k_a"][0]), r_k=_row(fw["b_r_k"][0]), gn_g=_row(fw["b_gn_g"][0]),
              gn_b=_row(fw["b_gn_b"][0]), w_o=fw["b_w_o"][0], seg=seg)
    ptb = {k + "_t": _t(pb[k]) for k in ("w_r", "w_k", "w_v", "w_o", "w1", "w2", "a1", "a2", "g1", "g2")}
    x4, sv_b = _rwkv_fwd(x3_, pb, seq)
    x5, sv_c1 = _xattn_fwd("c1", x4, mem_n, gains[1][2], gains[1][3], fw["c_w_q"][1], fw["c_w_kv"][1], fw["c_w_o"][1], seq, mem_len)
    x6, sv_m1 = _mlp_fwd("m1", x5, gains[1][4], gains[1][5], fw["m_w_up"][1], fw["m_w_down"][1])

    def loss_body(rv, cv, pv, nv):
        err = rv[0] - rv[1]
        return [err * (1.0 / d)], [err * err]
    (dx,), (sq,) = _rowwise("loss", loss_body, [x6, target], [], [(d, F32)], [d])
    loss_part = 0.5 / d * jnp.sum(sq)

    dx, g_m1 = _mlp_bwd("m1", sv_m1, dx, gains[1][4], gains[1][5], _t(fw["m_w_up"][1]), _t(fw["m_w_down"][1]))
    dx, dmem1, g_c1 = _xattn_bwd("c1", sv_c1, dx, mem_n, gains[1][2], gains[1][3], _t(fw["c_w_q"][1]), _t(fw["c_w_kv"][1]),
                                 _t(fw["c_w_o"][1]), seq, mem_len)
    dx, g_b = _rwkv_bwd(sv_b, dx, pb, ptb, seq)
    grads_b = dict(
        b_mu=g_b["mu"][None], b_w_rkv=jnp.stack([g_b["w_r"], g_b["w_k"], g_b["w_v"]])[None], b_w0=g_b["w0"][None],
        b_w1=g_b["w1"][None], b_w2=g_b["w2"][None], b_a0=g_b["a0"][None], b_a1=g_b["a1"][None], b_a2=g_b["a2"][None],
        b_g1=g_b["g1"][None], b_g2=g_b["g2"][None], b_k_k=g_b["k_k"][None], b_k_a=g_b["k_a"][None],
        b_r_k=g_b["r_k"].reshape(1, nh, RWKV_HEAD), b_gn_g=g_b["gn_g"][None], b_gn_b=g_b["gn_b"][None], b_w_o=g_b["w_o"][None])
    g_m0_pre = gains[0][4]
    if on_layer1_grads is not None:
        layer1 = dict(c_w_q=g_c1["w_q"], c_w_kv=g_c1["w_kv"], c_w_o=g_c1["w_o"], m_w_up=g_m1["w_up"], m_w_down=g_m1["w_down"])
        token = on_layer1_grads({**grads_b, **{n: [None, g] for n, g in layer1.items()}})
        g_m0_pre = g_m0_pre + token[0, 0]
    dx, g_m0 = _mlp_bwd("m0", sv_m0, dx, g_m0_pre, gains[0][5], _t(fw["m_w_up"][0]), _t(fw["m_w_down"][0]))
    dx, dmem0, g_c0 = _xattn_bwd("c0", sv_c0, dx, mem_n, gains[0][2], gains[0][3], _t(fw["c_w_q"][0]), _t(fw["c_w_kv"][0]),
                                 _t(fw["c_w_o"][0]), seq, mem_len)
    dx, g_a = _rglru_bwd(sv_a, dx, pa, pta, seq)

    def dmem_body(rv, cv, pv, nv):
        _, dg = _rms_bwd(rv[1] + rv[2], rv[0], cv[0])
        return [], [dg]
    _, (dmem_g,) = _rowwise("mem_norm_grad", dmem_body, [mem2, dmem0, dmem1], [mem_g], [], [d])

    lru_heads = fw["a_gate_w"].shape[2]
    blk = d // lru_heads
    grads = dict(
        ln_gains=jnp.stack([jnp.stack([g_a["g_pre"], g_a["g_post"], g_c0["g_pre"], g_c0["g_post"], g_m0["g_pre"], g_m0["g_post"]]),
                            jnp.stack([g_b["g_pre"], g_b["g_post"], g_c1["g_pre"], g_c1["g_post"], g_m1["g_pre"], g_m1["g_post"]])]),
        mem_norm=dmem_g.sum(0),
        a_conv_w=g_a["conv_w"][None], a_conv_b=g_a["conv_b"][None], a_w_in=g_a["w_in"][None], a_b_in=g_a["b_in"][None],
        a_gate_w=_gate_blocks(g_a["w_gate"], lru_heads)[None],
        a_gate_b=jnp.stack([g_a["gate_b0"], g_a["gate_b1"]]).reshape(1, 2, lru_heads, blk),
        a_lambda=g_a["lam"][None], a_w_out=g_a["w_out"][None], a_b_out=g_a["b_out"][None],
        **grads_b,
        c_w_q=[g_c0["w_q"], g_c1["w_q"]], c_w_kv=[g_c0["w_kv"], g_c1["w_kv"]], c_w_o=[g_c0["w_o"], g_c1["w_o"]],
        m_w_up=[g_m0["w_up"], g_m1["w_up"]], m_w_down=[g_m0["w_down"], g_m1["w_down"]],
    )
    return loss_part, dx.reshape(nseq, seq, d), grads


def kernel(x, mem, ln_gains, mem_norm, a_conv_w, a_conv_b, a_w_in, a_b_in, a_gate_w, a_gate_b, a_lambda, a_w_out, a_b_out, b_mu, b_w_rkv, b_w0, b_w1, b_w2, b_a0, b_a1, b_a2, b_g1, b_g2, b_k_k, b_k_a, b_r_k, b_gn_g, b_gn_b, b_w_o, c_w_q, c_w_kv, c_w_o, m_w_up, m_w_down, loss_target, m_ln_gains, m_mem_norm, m_a_conv_w, m_a_conv_b, m_a_w_in, m_a_b_in, m_a_gate_w, m_a_gate_b, m_a_lambda, m_a_w_out, m_a_b_out, m_b_mu, m_b_w_rkv, m_b_w0, m_b_w1, m_b_w2, m_b_a0, m_b_a1, m_b_a2, m_b_g1, m_b_g2, m_b_k_k, m_b_k_a, m_b_r_k, m_b_gn_g, m_b_gn_b, m_b_w_o, m_c_w_q, m_c_w_kv, m_c_w_o, m_m_w_up, m_m_w_down, v_ln_gains, v_mem_norm, v_a_conv_w, v_a_conv_b, v_a_w_in, v_a_b_in, v_a_gate_w, v_a_gate_b, v_a_lambda, v_a_w_out, v_a_b_out, v_b_mu, v_b_w_rkv, v_b_w0, v_b_w1, v_b_w2, v_b_a0, v_b_a1, v_b_a2, v_b_g1, v_b_g2, v_b_k_k, v_b_k_a, v_b_r_k, v_b_gn_g, v_b_gn_b, v_b_w_o, v_c_w_q, v_c_w_kv, v_c_w_o, v_m_w_up, v_m_w_down):
    given = dict(locals())
    w = {n: given[n] for n in _WEIGHTS}
    mom1 = {n: given["m_" + n] for n in _WEIGHTS}
    mom2 = {n: given["v_" + n] for n in _WEIGHTS}

    me = 2 * lax.axis_index("x") + lax.axis_index("y")
    per_layer = [n for n in _MATRICES if n[0] in "cm"]
    early = [(n, None) for n in _MATRICES if n[0] == "a"] + [(n, 0) for n in per_layer]
    late = [(n, None) for n in _MATRICES if n[0] == "b"] + [(n, 1) for n in per_layer]
    piece = lambda n, layer: w[n] if layer is None else w[n][layer:layer + 1]

    def gathered(entries, buf):
        shards = [_unpack(buf[s], [piece(n, layer).shape for n, layer in entries]) for s in range(N_XY)]
        out = {}
        for i, (n, layer) in enumerate(entries):
            full = jnp.concatenate([shards[s][i] for s in range(N_XY)], axis=_SHARD_AXIS[n])
            if layer is None:
                out[n] = full
            else:
                out.setdefault(n, {})[layer] = full[0]
        return out

    late_buf = _pack([piece(n, layer) for n, layer in late], BF16)
    gm, gv = _all_gather_xy(_pack([piece(n, layer) for n, layer in early], BF16), _pack([w[n] for n in _VECTORS], F32, SUBLANES))
    send_sems, recv_sems, late_thru, land_thru, token = _gather_xy_start(late_buf, gv)
    fw = {n: w[n] for n in _REPLICATED} | gathered(early, gm)
    vec_shards = [_unpack(gv[s], [w[n].shape for n in _VECTORS]) for s in range(N_XY)]
    for i, n in enumerate(_VECTORS):
        fw[n] = jnp.concatenate([vec_shards[s][i] for s in range(N_XY)], axis=_SHARD_AXIS[n])
    fw["ln_gains"] = fw["ln_gains"] + token[0, 0]

    def late_weights(after):
        land = _gather_xy_wait(send_sems, recv_sems, late_thru, land_thru, after)
        got = gathered(late, lax.dynamic_update_index_in_dim(land, late_buf, me, 0))
        return {n: ({**fw[n], **v} if isinstance(v, dict) else v) for n, v in got.items()}

    tile_rows = 256
    group1 = [(n, 1 if n in per_layer else None) for n in _SHARDED if n in per_layer or n[0] == "b"]
    group0 = [(n, 0 if n in per_layer else None) for n in _SHARDED if n in per_layer or n[0] != "b"]

    def piece_shape(n, layer):
        return w[n].shape if layer is None else (1,) + w[n].shape[1:]

    def rows_of(n, layer):
        size = 1
        for s_ in piece_shape(n, layer):
            size *= s_
        return size // PACK_W

    def split(entries):
        in_place = [e for e in entries if w[e[0]].shape[-1] == PACK_W and rows_of(*e) % tile_rows == 0 and rows_of(*e) > 0]
        in_place.sort(key=lambda e: -rows_of(*e))
        return in_place, [e for e in entries if e not in in_place]

    def buffers(entries, grads):
        in_place, packed = split(entries)

        def pieces(s):
            out = []
            for n, layer in in_place + packed:
                ax = _SHARD_AXIS[n]
                size = w[n].shape[ax]
                g = grads[n] if layer is None else grads[n][layer]
                out.append(lax.dynamic_slice_in_dim(g, s * size, size, axis=ax if layer is None else ax - 1))
            return out
        return jnp.stack([_pack(pieces(s), BF16) for s in range(N_XY)]), _pack(pieces(me), F32)

    def update(tag, entries, own, recv):
        in_place, packed = split(entries)
        part = _sum_contributions("sum_grads_" + tag, own, recv)
        sib = _swap_with_sibling("swap_sibling_" + tag, part)
        out = {}
        first = 0
        for n, layer in in_place:
            flat = [src[n].reshape(-1, PACK_W) for src in (w, mom1, mom2)]
            rows = rows_of(n, layer)
            res = _adamw_sharded("adamw_%s_%s" % (n, tag), part, sib, first, *flat, w_first=(layer or 0) * rows, n_rows=rows)
            out[(n, layer)] = [o.reshape(piece_shape(n, layer)) for o in res]
            first += rows
        take = lambda src, n, layer: src[n] if layer is None else src[n][layer:layer + 1]
        flat = [_pack([take(src, n, layer) for n, layer in packed], F32) for src in (w, mom1, mom2)]
        assert first + flat[0].shape[0] == part.shape[0], (first, flat[0].shape, part.shape)
        res = _adamw_sharded("adamw_packed_" + tag, part, sib, first, *flat)
        tail = [_unpack(o, [piece_shape(n, layer) for n, layer in packed]) for o in res]
        for i, e in enumerate(packed):
            out[e] = [tail[kind][i] for kind in range(4)]
        return out

    early = {}

    def on_layer1_grads(grads1):
        gsend, early["own"] = buffers(group1, grads1)
        *early["handles"], token = _exchange_xy_start(gsend, early["own"])
        return token

    loss_part, grad_x, grads = _local_step(x, mem, loss_target, fw, late_weights, on_layer1_grads)

    gsend, own = buffers(group0, grads)
    out0 = update("0", group0, own, _exchange_xy(gsend))
    out1 = update("1", group1, early["own"], _exchange_xy_wait(*early["handles"], grad_x))
    sharded_out = [[] for _ in range(4)]
    for n in _SHARDED:
        for kind in range(4):
            if n in per_layer:
                sharded_out[kind].append(jnp.concatenate([out0[(n, 0)][kind], out1[(n, 1)][kind]], axis=0))
            else:
                sharded_out[kind].append((out1 if n[0] == "b" else out0)[(n, None)][kind])

    small = _pack([grads[n] for n in _REPLICATED] + [loss_part.reshape(1)], F32, SUBLANES)
    parts = _all_gather_all(small)
    zero = jnp.zeros((1,), F32)
    flat = [_pack([src[n] for n in _REPLICATED] + [zero], F32, SUBLANES) for src in (w, mom1, mom2)]
    repl_out = [_unpack(o, [w[n].shape for n in _REPLICATED] + [(1,)]) for o in _adamw_replicated(parts, *flat)]
    loss = repl_out[0][-1][0]

    result = [loss, grad_x]
    for kind in range(4):
        by_name = dict(zip(_SHARDED, sharded_out[kind])) | dict(zip(_REPLICATED, repl_out[kind][:-1]))
        result += [by_name[n] for n in _WEIGHTS]
    return tuple(result)
```

```python
import functools

import jax
import jax.numpy as jnp
from jax import lax
from jax.experimental import pallas as pl
from jax.experimental.pallas import tpu as pltpu

F32 = jnp.float32
BF16 = jnp.bfloat16
MESH = pl.DeviceIdType.MESH

LANES = 128
SUBLANES = 8
VMEM_LIMIT_BYTES = 48 * 1024 * 1024

RMS_EPS = 1e-6
LRU_C = 8.0
LRU_HEADS = 4
CONV_WIDTH = 4
RWKV_HEAD = 64
RWKV_GN_EPS = 64e-5
MEM_HEADS = 4
ADAM_LR = 0.001
ADAM_B1 = 0.9
ADAM_B2 = 0.999
ADAM_EPS = 1e-08
ADAM_WD = 0.01
ADAM_STEP = 10
WKV_CHUNK = 64

_PARAMS = functools.partial(pltpu.CompilerParams, vmem_limit_bytes=VMEM_LIMIT_BYTES)


def _tile(n, want):
    if n <= want:
        return n
    t = want
    while t >= SUBLANES:
        if n % t == 0 and t % SUBLANES == 0:
            return t
        t -= SUBLANES
    return n


def _fold8(v):
    tm, d = v.shape
    if tm == SUBLANES:
        return v
    return jnp.sum(v.reshape(tm // SUBLANES, SUBLANES, d), axis=0)


def _rowwise(name, body, rows, consts=(), out_rows=(), out_accs=(), prev=(), nxt=(), tm=512, seq=None, n_rows=None):
    rows = [r if isinstance(r, tuple) else (r, r.shape[1], 0) for r in rows]
    rows = [r if len(r) == 4 else r + (0,) for r in rows]
    t = rows[0][0].shape[0] if n_rows is None else n_rows
    tm = _tile(t, tm)
    if seq is not None:
        tm = _tile(seq, tm)
    nblk = t // tm
    nrow, ncst, nprev, nnxt = len(rows), len(consts), len(prev), len(nxt)
    nor, noa = len(out_rows), len(out_accs)
    hb = tm // SUBLANES

    def kern(*refs):
        i = pl.program_id(0)
        rv = [r[...] for r in refs[:nrow]]
        cv = [c[...] for c in refs[nrow:nrow + ncst]]
        o = nrow + ncst
        pv = []
        for j in range(nprev):
            at_start = (i * tm) % seq == 0
            h = refs[o + j][...]
            pv.append(jnp.where(at_start, jnp.zeros_like(h), h))
        o += nprev
        nv = []
        for j in range(nnxt):
            at_end = ((i + 1) * tm) % seq == 0
            h = refs[o + j][...]
            nv.append(jnp.where(at_end, jnp.zeros_like(h), h))
        o += nnxt
        outs, accs = body(rv, cv, pv, nv)
        for j in range(nor):
            refs[o + j][...] = outs[j].astype(refs[o + j].dtype)
        o += nor
        if noa:
            @pl.when(i == 0)
            def _():
                for j in range(noa):
                    refs[o + j][...] = jnp.zeros_like(refs[o + j])
            for j in range(noa):
                refs[o + j][...] += _fold8(accs[j].astype(F32))

    assert all(first % tm == 0 for (_, _, _, first) in rows), name
    in_specs = [pl.BlockSpec((tm, w), functools.partial(lambda i, c, o: (i + o, c), c=cb, o=first // tm))
                for (_, w, cb, first) in rows]
    in_specs += [pl.BlockSpec(c.shape, lambda i: (0, 0)) for c in consts]
    in_specs += [pl.BlockSpec((SUBLANES, rows[j][1]),
                              functools.partial(lambda i, c: (jnp.maximum(i * hb - 1, 0), c), c=rows[j][2])) for j in prev]
    in_specs += [pl.BlockSpec((SUBLANES, rows[j][1]),
                              functools.partial(lambda i, c: (jnp.minimum((i + 1) * hb, t // SUBLANES - 1), c), c=rows[j][2]))
                 for j in nxt]
    out_shape = [jax.ShapeDtypeStruct((t, w), dt) for (w, dt) in out_rows]
    out_shape += [jax.ShapeDtypeStruct((SUBLANES, w), F32) for w in out_accs]
    out_specs = [pl.BlockSpec((tm, w), lambda i: (i, 0)) for (w, _) in out_rows]
    out_specs += [pl.BlockSpec((SUBLANES, w), lambda i: (0, 0)) for w in out_accs]
    args = [r[0] for r in rows] + list(consts) + [rows[j][0] for j in prev] + [rows[j][0] for j in nxt]
    res = pl.pallas_call(
        kern, name=name, grid=(nblk,), in_specs=in_specs, out_specs=out_specs, out_shape=out_shape,
        compiler_params=_PARAMS(dimension_semantics=("arbitrary",)),
    )(*args)
    return list(res[:nor]), list(res[nor:])


def _shift_down(x, halo, k):
    rolled = pltpu.roll(x, k, 0)
    row = lax.broadcasted_iota(jnp.int32, (SUBLANES, x.shape[1]), 0)
    first = jnp.where(row < k, pltpu.roll(halo, k, 0), rolled[:SUBLANES])
    if x.shape[0] == SUBLANES:
        return first
    return jnp.concatenate([first, rolled[SUBLANES:]], axis=0)


def _shift_up(x, halo, k):
    n = x.shape[0]
    rolled = pltpu.roll(x, n - k, 0)
    row = lax.broadcasted_iota(jnp.int32, (SUBLANES, x.shape[1]), 0)
    last = jnp.where(row >= SUBLANES - k, pltpu.roll(halo, SUBLANES - k, 0), rolled[n - SUBLANES:])
    if n == SUBLANES:
        return last
    return jnp.concatenate([rolled[:n - SUBLANES], last], axis=0)


class _Transposed:
    def __init__(self, w):
        self.w = w


def _t(w):
    return _Transposed(w)


def _mm(name, a, b, out_dtype=F32, trans_a=False, tm=1024, tn=1024, tk=1024, epilogue=None, extra=None):
    trans_b = isinstance(b, _Transposed)
    assert not (trans_a and trans_b)
    if trans_b:
        b = b.w
    if trans_a:
        kdim, m = a.shape
    else:
        m, kdim = a.shape
    n = b.shape[0] if trans_b else b.shape[1]
    assert b.shape[1 if trans_b else 0] == kdim, (name, a.shape, b.shape)
    tm, tn, tk = _tile(m, tm), _tile(n, tn), _tile(kdim, tk)
    nk = kdim // tk
    dims = (((0,), (0,)), ((), ())) if trans_a else (((1,), (1 if trans_b else 0,)), ((), ()))

    n_in = 2 if extra is None else 3

    def kern(*refs):
        a_ref, b_ref, o_ref, acc = refs[0], refs[1], refs[n_in], refs[n_in + 1:]

        def store(res):
            if epilogue is not None:
                res = epilogue(res) if extra is None else epilogue(res, refs[2][...])
            o_ref[...] = res.astype(o_ref.dtype)

        part = lax.dot_general(a_ref[...].astype(BF16), b_ref[...].astype(BF16), dims, preferred_element_type=F32)
        if nk == 1:
            store(part)
        else:
            k = pl.program_id(2)

            @pl.when(k == 0)
            def _():
                acc[0][...] = part

            @pl.when(k > 0)
            def _():
                acc[0][...] += part

            @pl.when(k == nk - 1)
            def _():
                store(acc[0][...])

    a_spec = pl.BlockSpec((tk, tm), lambda i, j, k: (k, i)) if trans_a else pl.BlockSpec((tm, tk), lambda i, j, k: (i, k))
    b_spec = pl.BlockSpec((tn, tk), lambda i, j, k: (j, k)) if trans_b else pl.BlockSpec((tk, tn), lambda i, j, k: (k, j))
    out_spec = pl.BlockSpec((tm, tn), lambda i, j, k: (i, j))
    return pl.pallas_call(
        kern, name=name, grid=(m // tm, n // tn, nk),
        in_specs=[a_spec, b_spec] + ([] if extra is None else [out_spec]),
        out_specs=out_spec,
        out_shape=jax.ShapeDtypeStruct((m, n), out_dtype),
        scratch_shapes=[] if nk == 1 else [pltpu.VMEM((tm, tn), F32)],
        compiler_params=_PARAMS(dimension_semantics=("parallel", "parallel", "arbitrary")),
    )(*((a, b) if extra is None else (a, b, extra)))


def _scan(name, a, b, seq, reverse=False, tm=256):
    t, d = a.shape
    tm = _tile(seq, tm)
    nblk = t // tm
    ntile = tm // SUBLANES

    def kern(a_ref, b_ref, h_ref, carry_h, carry_a):
        i = pl.program_id(0)
        blk = (nblk - 1 - i) if reverse else i
        edge = (((blk + 1) * tm) % seq == 0) if reverse else ((blk * tm) % seq == 0)

        @pl.when(edge)
        def _():
            carry_h[...] = jnp.zeros_like(carry_h)
            carry_a[...] = jnp.zeros_like(carry_a)

        def tile_step(j, c):
            jj = (ntile - 1 - j) if reverse else j
            rows = pl.ds(pl.multiple_of(jj * SUBLANES, SUBLANES), SUBLANES)
            a8 = a_ref[rows, :]
            b8 = b_ref[rows, :]
            h, an = c
            out = [None] * SUBLANES
            order = range(SUBLANES - 1, -1, -1) if reverse else range(SUBLANES)
            for r in order:
                if reverse:
                    h = b8[r:r + 1, :] + an * h
                    an = a8[r:r + 1, :]
                else:
                    h = a8[r:r + 1, :] * h + b8[r:r + 1, :]
                out[r] = h
            h_ref[rows, :] = jnp.concatenate(out, axis=0)
            return (h, an)

        h, an = lax.fori_loop(0, ntile, tile_step, (carry_h[...], carry_a[...]))
        carry_h[...] = h
        carry_a[...] = an

    idx = (lambda i: (nblk - 1 - i, 0)) if reverse else (lambda i: (i, 0))
    return pl.pallas_call(
        kern, name=name, grid=(nblk,),
        in_specs=[pl.BlockSpec((tm, d), idx), pl.BlockSpec((tm, d), idx)],
        out_specs=pl.BlockSpec((tm, d), idx),
        out_shape=jax.ShapeDtypeStruct((t, d), F32),
        scratch_shapes=[pltpu.VMEM((1, d), F32), pltpu.VMEM((1, d), F32)],
        compiler_params=_PARAMS(dimension_semantics=("arbitrary",)),
    )(a, b)


_NN = (((1,), (0,)), ((), ()))
_NT = (((1,), (1,)), ((), ()))
_TN = (((0,), (0,)), ((), ()))


def _dot1(a, b, dims):
    return lax.dot_general(a.astype(BF16), b.astype(BF16), dims, preferred_element_type=F32)


def _dot3(a, b, dims):
    a_hi, b_hi = a.astype(BF16), b.astype(BF16)
    a_lo, b_lo = (a - a_hi.astype(F32)).astype(BF16), (b - b_hi.astype(F32)).astype(BF16)
    dg = lambda p, q: lax.dot_general(p, q, dims, preferred_element_type=F32)
    return dg(a_hi, b_hi) + (dg(a_hi, b_lo) + dg(a_lo, b_hi))


def _make_bmm(dot):
    def make(dims, da_rule, db_rule):
        @jax.custom_vjp
        def f(a, b):
            return dot(a, b, dims)

        def fwd(a, b):
            return dot(a, b, dims), (a, b)

        def bwd(res, g):
            a, b = res
            return da_rule(a, b, g), db_rule(a, b, g)

        f.defvjp(fwd, bwd)
        return f

    return dict(nn=make(_NN, lambda a, b, g: dot(g, b, _NT), lambda a, b, g: dot(a, g, _TN)),
                nt=make(_NT, lambda a, b, g: dot(g, b, _NN), lambda a, b, g: dot(g, a, _TN)),
                tn=make(_TN, lambda a, b, g: dot(b, g, _NT), lambda a, b, g: dot(a, g, _NN)))


_BMM = {1: _make_bmm(_dot1), 3: _make_bmm(_dot3)}
_WKV_PASSES = dict(pair=1, read=1, inv=3, apply=1, write=1)


def _running_sum(x, reverse):
    c = x.shape[0]
    row = lax.broadcasted_iota(jnp.int32, x.shape, 0)
    k = 1
    while k < c:
        if reverse:
            x = x + jnp.where(row < c - k, pltpu.roll(x, c - k, 0), 0.0)
        else:
            x = x + jnp.where(row >= k, pltpu.roll(x, k, 0), 0.0)
        k *= 2
    return x


@jax.custom_vjp
def _cumsum_rows(x):
    return _running_sum(x, False)


_cumsum_rows.defvjp(lambda x: (_running_sum(x, False), None), lambda _, g: (_running_sum(g, True),))


@jax.custom_vjp
def _unit_lower_inverse(nl):
    c = nl[0].shape[0]
    mm = _BMM[_WKV_PASSES["inv"]]["nn"]
    eye = jnp.where(lax.broadcasted_iota(jnp.int32, (c, c), 0) == lax.broadcasted_iota(jnp.int32, (c, c), 1), 1.0, 0.0)
    inv = [eye + z for z in nl]
    p = nl
    for _ in range(c.bit_length() - 2):
        p = [mm(z, z) for z in p]
        inv = [i_ + mm(p_, i_) for i_, p_ in zip(inv, p)]
    return inv


def _unit_lower_inverse_fwd(nl):
    inv = _unit_lower_inverse(nl)
    return inv, inv


def _unit_lower_inverse_bwd(inv, g):
    mm = _BMM[_WKV_PASSES["inv"]]
    left = [mm["tn"](x, g_) for x, g_ in zip(inv, g)]
    return ([mm["nt"](l_, x) for l_, x in zip(left, inv)],)


_unit_lower_inverse.defvjp(_unit_lower_inverse_fwd, _unit_lower_inverse_bwd)


@jax.custom_vjp
def _kept_inverse(nl, inv):
    return inv


_kept_inverse.defvjp(lambda nl, inv: (inv, inv),
                     lambda inv, g: (_unit_lower_inverse_bwd(inv, g)[0], [jnp.zeros_like(x) for x in inv]))


def _wkv_chunk(r, lw, k, v, a, b, s0, kept_inv=None):
    c = r[0].shape[0]
    ti = lax.broadcasted_iota(jnp.int32, (c, 2 * c), 0)
    tj = lax.broadcasted_iota(jnp.int32, (c, 2 * c), 1)
    right = tj >= c
    tau = jnp.where(right, tj - c, tj)
    strict_left = jnp.logical_and(jnp.logical_not(right), tau < ti)[:, :c]
    strict_right = jnp.logical_and(right, tau < ti)
    incl = tau <= ti
    last = lax.broadcasted_iota(jnp.int32, r[0].shape, 0) == c - 1
    each = lambda f, *ls: [f(*z) for z in zip(*ls)]
    rows2 = lambda x, y: jnp.concatenate([x, y], axis=0)
    pair, read, inv_, apply_, write = (_BMM[_WKV_PASSES[role]] for role in ("pair", "read", "inv", "apply", "write"))
    cum = each(_cumsum_rows, lw)
    w_incl = each(jnp.exp, cum)
    w_inv = each(lambda z: jnp.exp(-z), cum)
    at = each(lambda a_, c_, l_: a_ * jnp.exp(c_ - l_), a, cum, lw)
    ar = each(rows2, at, each(jnp.multiply, r, w_incl))
    bk = each(rows2, each(jnp.multiply, b, w_inv), each(jnp.multiply, k, w_inv))
    pp = each(pair["nt"], ar, bk)
    sr = each(read["nt"], ar, s0)
    nl = [jnp.where(strict_left, z[:c, :c], 0.0) for z in pp]
    zero_v = each(lambda v_: rows2(jnp.zeros_like(v_), v_), v)
    rhs = each(lambda s, z, zv: s[:c] + apply_["nn"](jnp.where(strict_right, z[:c], 0.0), zv), sr, pp, zero_v)
    inv = _unit_lower_inverse(nl) if kept_inv is None else _kept_inverse(nl, kept_inv)
    ut = each(inv_["nn"], inv, rhs)
    uv = each(rows2, ut, v)
    y = each(lambda s, z, uv_: s[c:] + apply_["nn"](jnp.where(incl, z[c:], 0.0), uv_), sr, pp, uv)
    w_end = each(lambda z: jnp.exp(jnp.sum(jnp.where(last, z, 0.0), axis=0, keepdims=True)), cum)
    s1 = each(lambda s, uv_, bk_, w_: (s + write["tn"](uv_, bk_)) * w_, s0, uv, bk, w_end)
    return y, s1, inv


def _wkv_fwd(r, lw, k, v, a, b, seq, hb=16):
    t, d = r.shape
    n = RWKV_HEAD
    nh = d // n
    hb = min(hb, nh)
    chunk = min(WKV_CHUNK, seq)
    ncs = seq // chunk

    def kern(r_ref, lw_ref, k_ref, v_ref, a_ref, b_ref, y_ref, st_ref, inv_ref, s_scr):
        @pl.when(pl.program_id(2) == 0)
        def _():
            s_scr[...] = jnp.zeros_like(s_scr)

        heads = lambda ref: [ref[:, h * n:(h + 1) * n] for h in range(hb)]
        s0 = [s_scr[h] for h in range(hb)]
        y, s1, inv = _wkv_chunk(heads(r_ref), heads(lw_ref), heads(k_ref), heads(v_ref), heads(a_ref), heads(b_ref), s0)
        for h in range(hb):
            st_ref[0, h] = s0[h]
            inv_ref[0, h] = inv[h]
            y_ref[:, h * n:(h + 1) * n] = y[h]
            s_scr[h] = s1[h]

    vec = pl.BlockSpec((chunk, hb * n), lambda bb, g, c: (bb * ncs + c, g))
    per_chunk = lambda rows: pl.BlockSpec((1, hb, rows, rows), lambda bb, g, c: (bb * ncs + c, g, 0, 0))
    return pl.pallas_call(
        kern, name="wkv_fwd", grid=(t // seq, nh // hb, ncs), in_specs=[vec] * 6,
        out_specs=[vec, per_chunk(n), per_chunk(chunk)],
        out_shape=[jax.ShapeDtypeStruct((t, d), F32), jax.ShapeDtypeStruct((t // chunk, nh, n, n), F32),
                   jax.ShapeDtypeStruct((t // chunk, nh, chunk, chunk), F32)],
        scratch_shapes=[pltpu.VMEM((hb, n, n), F32)],
        compiler_params=_PARAMS(dimension_semantics=("parallel", "parallel", "arbitrary")),
    )(r, lw, k, v, a, b)


def _wkv_bwd(r, lw, k, v, a, b, st, inv, dy, seq, hb=16):
    t, d = r.shape
    n = RWKV_HEAD
    nh = d // n
    hb = min(hb, nh)
    chunk = min(WKV_CHUNK, seq)
    ncs = seq // chunk

    def kern(r_ref, lw_ref, k_ref, v_ref, a_ref, b_ref, st_ref, inv_ref, dy_ref,
             dr_ref, dlw_ref, dk_ref, dv_ref, da_ref, db_ref, ds_scr):
        @pl.when(pl.program_id(2) == 0)
        def _():
            ds_scr[...] = jnp.zeros_like(ds_scr)

        heads = lambda ref: [ref[:, h * n:(h + 1) * n] for h in range(hb)]
        kept = [inv_ref[0, h] for h in range(hb)]
        _, vjp = jax.vjp(lambda *args: _wkv_chunk(*args, kept_inv=kept)[:2],
                         heads(r_ref), heads(lw_ref), heads(k_ref), heads(v_ref), heads(a_ref), heads(b_ref),
                         [st_ref[0, h] for h in range(hb)])
        grads = vjp((heads(dy_ref), [ds_scr[h] for h in range(hb)]))
        for h in range(hb):
            for ref, g in zip((dr_ref, dlw_ref, dk_ref, dv_ref, da_ref, db_ref), grads[:6]):
                ref[:, h * n:(h + 1) * n] = g[h]
            ds_scr[h] = grads[6][h]

    vec = pl.BlockSpec((chunk, hb * n), lambda bb, g, c: (bb * ncs + ncs - 1 - c, g))
    per_chunk = lambda rows: pl.BlockSpec((1, hb, rows, rows), lambda bb, g, c: (bb * ncs + ncs - 1 - c, g, 0, 0))
    return pl.pallas_call(
        kern, name="wkv_bwd", grid=(t // seq, nh // hb, ncs),
        in_specs=[vec] * 6 + [per_chunk(n), per_chunk(chunk), vec],
        out_specs=[vec] * 6, out_shape=[jax.ShapeDtypeStruct((t, d), F32)] * 6,
        scratch_shapes=[pltpu.VMEM((hb, n, n), F32)],
        compiler_params=_PARAMS(dimension_semantics=("parallel", "parallel", "arbitrary")),
    )(r, lw, k, v, a, b, st, inv, dy)


def _softmax_rows(s):
    e = jnp.exp(s - jnp.max(s, axis=-1, keepdims=True))
    return e / jnp.sum(e, axis=-1, keepdims=True)


def _attn_fwd(q, kv, seq, mem_len, tq=256):
    t, d = q.shape
    dh = d // MEM_HEADS
    scale = dh ** -0.5
    tq = _tile(seq, tq)
    nq = seq // tq

    def kern(q_ref, kv_ref, o_ref):
        for h in range(MEM_HEADS):
            cols = slice(h * dh, (h + 1) * dh)
            vcols = slice(d + h * dh, d + (h + 1) * dh)
            s = lax.dot_general(q_ref[:, cols], kv_ref[:, cols], _NT, preferred_element_type=F32) * scale
            p = _softmax_rows(s)
            o_ref[:, cols] = jnp.dot(p.astype(BF16), kv_ref[:, vcols], preferred_element_type=F32).astype(o_ref.dtype)

    return pl.pallas_call(
        kern, name="attn_fwd", grid=(t // seq, nq),
        in_specs=[pl.BlockSpec((tq, d), lambda b, i: (b * nq + i, 0)), pl.BlockSpec((mem_len, 2 * d), lambda b, i: (b, 0))],
        out_specs=pl.BlockSpec((tq, d), lambda b, i: (b * nq + i, 0)),
        out_shape=jax.ShapeDtypeStruct((t, d), BF16),
        compiler_params=_PARAMS(dimension_semantics=("parallel", "parallel")),
    )(q, kv)


def _attn_bwd(q, kv, do, seq, mem_len, tq=256):
    t, d = q.shape
    dh = d // MEM_HEADS
    scale = dh ** -0.5
    tq = _tile(seq, tq)
    nq = seq // tq

    def kern(q_ref, kv_ref, do_ref, dq_ref, dkv_ref):
        @pl.when(pl.program_id(1) == 0)
        def _():
            dkv_ref[...] = jnp.zeros_like(dkv_ref)

        for h in range(MEM_HEADS):
            cols = slice(h * dh, (h + 1) * dh)
            vcols = slice(d + h * dh, d + (h + 1) * dh)
            qh, kh, vh, doh = q_ref[:, cols], kv_ref[:, cols], kv_ref[:, vcols], do_ref[:, cols]
            p = _softmax_rows(lax.dot_general(qh, kh, _NT, preferred_element_type=F32) * scale)
            dp = lax.dot_general(doh, vh, _NT, preferred_element_type=F32)
            ds = (p * (dp - jnp.sum(p * dp, axis=-1, keepdims=True)) * scale).astype(BF16)
            dq_ref[:, cols] = jnp.dot(ds, kh, preferred_element_type=F32).astype(dq_ref.dtype)
            dkv_ref[:, cols] += lax.dot_general(ds, qh, _TN, preferred_element_type=F32)
            dkv_ref[:, vcols] += lax.dot_general(p.astype(BF16), doh, _TN, preferred_element_type=F32)

    return pl.pallas_call(
        kern, name="attn_bwd", grid=(t // seq, nq),
        in_specs=[pl.BlockSpec((tq, d), lambda b, i: (b * nq + i, 0)), pl.BlockSpec((mem_len, 2 * d), lambda b, i: (b, 0)),
                  pl.BlockSpec((tq, d), lambda b, i: (b * nq + i, 0))],
        out_specs=[pl.BlockSpec((tq, d), lambda b, i: (b * nq + i, 0)), pl.BlockSpec((mem_len, 2 * d), lambda b, i: (b, 0))],
        out_shape=[jax.ShapeDtypeStruct((t, d), BF16), jax.ShapeDtypeStruct(kv.shape, F32)],
        compiler_params=_PARAMS(dimension_semantics=("parallel", "arbitrary")),
    )(q, kv, do)


def _rstd(x):
    return lax.rsqrt(jnp.mean(x * x, axis=-1, keepdims=True) + RMS_EPS)


def _rms(x, g):
    return x * _rstd(x) * g


def _rms_bwd(dy, x, g):
    rstd = _rstd(x)
    xhat = x * rstd
    dxhat = dy * g
    return rstd * (dxhat - xhat * jnp.mean(dxhat * xhat, axis=-1, keepdims=True)), dy * xhat


def _softplus(x):
    return jnp.maximum(x, 0.0) + jnp.log1p(jnp.exp(-jnp.abs(x)))


def _one_minus_exp(x):
    series = -x * (1.0 + x * (0.5 + x * (1.0 / 6.0 + x * (1.0 / 24.0 + x * (1.0 / 120.0)))))
    return jnp.where(x > -0.05, series, 1.0 - jnp.exp(x))


_GELU_C = 0.7978845608028654
_GELU_K = 0.044715


def _gelu(x):
    return 0.5 * x * (1.0 + jnp.tanh(_GELU_C * (x + _GELU_K * x * x * x)))


def _gelu_grad(x):
    th = jnp.tanh(_GELU_C * (x + _GELU_K * x * x * x))
    return 0.5 * (1.0 + th) + 0.5 * x * (1.0 - th * th) * _GELU_C * (1.0 + 3.0 * _GELU_K * x * x)


def _seg_sum(x, seg):
    def two_terms(v, dims):
        hi = v.astype(BF16)
        lo = (v - hi.astype(F32)).astype(BF16)
        return (lax.dot_general(hi, seg, dims, preferred_element_type=F32)
                + lax.dot_general(lo, seg, dims, preferred_element_type=F32))
    return two_terms(two_terms(x, _NN), _NT)


def _f32(v):
    return v.astype(F32)


def _norm_fwd(name, x, g, dtype=BF16):
    (hn,), _ = _rowwise(name, lambda rv, cv, pv, nv: ([_rms(_f32(rv[0]), cv[0])], []), [x], [g], [(x.shape[1], dtype)])
    return hn


def _resid_norm_fwd(name, x, t, g, bias=None):
    def body(rv, cv, pv, nv):
        tt = rv[1] if bias is None else rv[1] + cv[1]
        return [rv[0] + _rms(tt, cv[0])], []
    (y,), _ = _rowwise(name, body, [x, t], [g] if bias is None else [g, bias], [(x.shape[1], F32)])
    return y


def _resid_norm_bwd(name, dxn, t, g, bias=None):
    def body(rv, cv, pv, nv):
        tt = rv[1] if bias is None else rv[1] + cv[1]
        dt, dg = _rms_bwd(rv[0], tt, cv[0])
        return [dt], [dg, dt]
    d = t.shape[1]
    (dt,), (dg, db) = _rowwise(name, body, [dxn, t], [g] if bias is None else [g, bias], [(d, BF16)], [d, d])
    return dt, dg.sum(0), db.sum(0)


def _prenorm_bwd(name, dxn, dhn, x, g):
    def body(rv, cv, pv, nv):
        dx, dg = _rms_bwd(_f32(rv[1]), rv[2], cv[0])
        return [rv[0] + dx], [dg]
    d = x.shape[1]
    (dx,), (dg,) = _rowwise(name, body, [dxn, dhn, x], [g], [(d, F32)], [d])
    return dx, dg.sum(0)


def _mlp_fwd(tag, x, g_pre, g_post, w_up, w_down):
    hn = _norm_fwd(tag + "_norm", x, g_pre)
    act = _mm(tag + "_up", hn, w_up, out_dtype=BF16, epilogue=lambda up: jnp.square(jnp.maximum(up, 0.0)))
    m = _mm(tag + "_down", act, w_down)
    y = _resid_norm_fwd(tag + "_res", x, m, g_post)
    return y, (x, hn, act, m)


def _mlp_bwd(tag, saved, dy, g_pre, g_post, w_up_t, w_down_t):
    x, hn, act, m = saved
    dm, dg_post, _ = _resid_norm_bwd(tag + "_dres", dy, m, g_post)
    dup = _mm(tag + "_dup", dm, w_down_t, out_dtype=BF16, extra=act,
              epilogue=lambda dact, act_: dact * 2.0 * jnp.sqrt(_f32(act_)))
    dw_down = _mm(tag + "_dwdown", act, dm, trans_a=True)
    dw_up = _mm(tag + "_dwup", hn, dup, trans_a=True)
    dhn = _mm(tag + "_dhn", dup, w_up_t)
    dx, dg_pre = _prenorm_bwd(tag + "_dnorm", dy, dhn, x, g_pre)
    return dx, dict(g_pre=dg_pre, g_post=dg_post, w_up=dw_up, w_down=dw_down)


def _xattn_fwd(tag, x, mem_n, g_pre, g_post, w_q, w_kv, w_o, seq, mem_len):
    hn = _norm_fwd(tag + "_norm", x, g_pre)
    q = _mm(tag + "_q", hn, w_q, out_dtype=BF16)
    kv = _mm(tag + "_kv", mem_n, w_kv, out_dtype=BF16)
    o = _attn_fwd(q, kv, seq, mem_len)
    c = _mm(tag + "_o", o, w_o)
    y = _resid_norm_fwd(tag + "_res", x, c, g_post)
    return y, (x, hn, q, kv, o, c)


def _xattn_bwd(tag, saved, dy, mem_n, g_pre, g_post, w_q_t, w_kv_t, w_o_t, seq, mem_len):
    x, hn, q, kv, o, c = saved
    dc, dg_post, _ = _resid_norm_bwd(tag + "_dres", dy, c, g_post)
    do = _mm(tag + "_do", dc, w_o_t, out_dtype=BF16)
    dw_o = _mm(tag + "_dwo", o, dc, trans_a=True)
    dq, dkv = _attn_bwd(q, kv, do, seq, mem_len)
    dw_q = _mm(tag + "_dwq", hn, dq, trans_a=True)
    dhn = _mm(tag + "_dhn", dq, w_q_t)
    dw_kv = _mm(tag + "_dwkv", mem_n, dkv, trans_a=True)
    dmem_n = _mm(tag + "_dmem", dkv, w_kv_t)
    dx, dg_pre = _prenorm_bwd(tag + "_dnorm", dy, dhn, x, g_pre)
    return dx, dmem_n, dict(g_pre=dg_pre, g_post=dg_post, w_q=dw_q, w_kv=dw_kv, w_o=dw_o)


def _lru_gates(z0, z1, gb0, gb1, sp):
    r = jax.nn.sigmoid(z0 + gb0)
    i = jax.nn.sigmoid(z1 + gb1)
    log_a = -LRU_C * r * sp
    a = jnp.exp(log_a)
    mult = jnp.sqrt(_one_minus_exp(2.0 * log_a))
    return r, i, a, mult


def _rglru_fwd(x, p, seq):
    d = x.shape[1]
    hn = _norm_fwd("a_norm", x, p["g_pre"])
    proj = _mm("a_in", hn, p["w_in"])

    def conv_body(rv, cv, pv, nv):
        u = rv[0] + cv[0]
        halo = pv[0] + cv[0]
        i = pl.program_id(0)
        halo = jnp.where((i * rv[0].shape[0]) % seq == 0, jnp.zeros_like(halo), halo)
        conv = cv[2] + u * cv[1][CONV_WIDTH - 1:CONV_WIDTH]
        for tap in range(CONV_WIDTH - 1):
            conv = conv + _shift_down(u, halo, CONV_WIDTH - 1 - tap) * cv[1][tap:tap + 1]
        return [conv], []
    (conv,), _ = _rowwise("a_conv", conv_body, [(proj, d, 1)], [p["b_in_u"], p["conv_w"], p["conv_b"]], [(d, F32)],
                          prev=[0], seq=seq)
    z = _mm("a_gate", conv, p["w_gate"])

    def gate_body(rv, cv, pv, nv):
        r, i, a, mult = _lru_gates(rv[0], rv[1], cv[0], cv[1], _softplus(-cv[2]))
        return [a, mult * i * rv[2]], []
    (a, bb), _ = _rowwise("a_gates", gate_body, [(z, d, 0), (z, d, 1), conv], [p["gate_b0"], p["gate_b1"], p["lam"]],
                          [(d, F32), (d, F32)])
    h = _scan("a_scan", a, bb, seq)

    def hy_body(rv, cv, pv, nv):
        return [rv[0] * _gelu(rv[1] + cv[0])], []
    (hy,), _ = _rowwise("a_hy", hy_body, [h, (proj, d, 0)], [p["b_in_y"]], [(d, BF16)])
    out = _mm("a_out", hy, p["w_out"])
    y = _resid_norm_fwd("a_res", x, out, p["g_post"], bias=p["b_out"])
    return y, (x, hn, proj, conv, z, a, h, hy, out)


def _rglru_bwd(saved, dy, p, pt, seq):
    x, hn, proj, conv, z, a, h, hy, out = saved
    d = x.shape[1]
    dt, dg_post, db_out = _resid_norm_bwd("a_dres", dy, out, p["g_post"], bias=p["b_out"])
    dhy = _mm("a_dhy", dt, pt["w_out_t"])
    dw_out = _mm("a_dwout", hy, dt, trans_a=True)

    def dh_body(rv, cv, pv, nv):
        yb = rv[2] + cv[0]
        return [rv[0] * _gelu(yb), rv[0] * rv[1] * _gelu_grad(yb)], []
    (dh, dyb), _ = _rowwise("a_dh", dh_body, [dhy, h, (proj, d, 0)], [p["b_in_y"]], [(d, F32), (d, BF16)])
    g = _scan("a_rscan", a, dh, seq, reverse=True)

    def dgate_body(rv, cv, pv, nv):
        gg, hh, z0, z1, cnv = rv
        sp = _softplus(-cv[2])
        r, i, aa, mult = _lru_gates(z0, z1, cv[0], cv[1], sp)
        i_blk = pl.program_id(0)
        halo = jnp.where((i_blk * gg.shape[0]) % seq == 0, jnp.zeros_like(pv[0]), pv[0])
        da = gg * _shift_down(hh, halo, 1)
        dmult = gg * i * cnv
        di = gg * mult * cnv
        dconv = gg * mult * i
        dlog_a = da * aa - dmult * aa * aa / mult
        dz0 = dlog_a * (-LRU_C * sp) * r * (1.0 - r)
        dz1 = di * i * (1.0 - i)
        dsp = dlog_a * (-LRU_C * r)
        dlam = dsp * (-jax.nn.sigmoid(-cv[2]))
        return [jnp.concatenate([dz0, dz1], axis=1), dconv], [dz0, dz1, dlam]
    (dz, dconv1), (dgb0, dgb1, dlam) = _rowwise(
        "a_dgates", dgate_body, [g, h, (z, d, 0), (z, d, 1), conv], [p["gate_b0"], p["gate_b1"], p["lam"]],
        [(2 * d, BF16), (d, F32)], [d, d, d], prev=[1], seq=seq)
    dconv2 = _mm("a_dconv", dz, pt["w_gate_t"])
    dw_gate = _mm("a_dwgate", conv, dz, trans_a=True)

    def dconv_body(rv, cv, pv, nv):
        dc1, dc2, pu, dyb_ = rv
        dc = dc1 + dc2
        dc_next = nv[0] + nv[1]
        u = pu + cv[0]
        i_blk = pl.program_id(0)
        halo = jnp.where((i_blk * u.shape[0]) % seq == 0, jnp.zeros_like(pv[0]), pv[0] + cv[0])
        du = dc * cv[1][CONV_WIDTH - 1:CONV_WIDTH]
        dws = []
        for tap in range(CONV_WIDTH - 1):
            k = CONV_WIDTH - 1 - tap
            du = du + _shift_up(dc, dc_next, k) * cv[1][tap:tap + 1]
            dws.append(dc * _shift_down(u, halo, k))
        dws.append(dc * u)
        return [jnp.concatenate([_f32(dyb_), du], axis=1)], dws + [dc, _f32(dyb_), du]
    (dproj,), accs = _rowwise(
        "a_dconvw", dconv_body, [dconv1, dconv2, (proj, d, 1), dyb], [p["b_in_u"], p["conv_w"]],
        [(2 * d, BF16)], [d] * (CONV_WIDTH + 3), prev=[2], nxt=[0, 1], seq=seq)
    dconv_w = jnp.stack([acc.sum(0) for acc in accs[:CONV_WIDTH]])
    dconv_b = accs[CONV_WIDTH].sum(0)
    db_in = jnp.concatenate([accs[CONV_WIDTH + 1].sum(0), accs[CONV_WIDTH + 2].sum(0)])
    dhn = _mm("a_dhn", dproj, pt["w_in_t"])
    dw_in = _mm("a_dwin", hn, dproj, trans_a=True)
    dx, dg_pre = _prenorm_bwd("a_dnorm", dy, dhn, x, p["g_pre"])
    grads = dict(g_pre=dg_pre, g_post=dg_post, b_out=db_out, w_out=dw_out, gate_b0=dgb0.sum(0), gate_b1=dgb1.sum(0),
                 lam=dlam.sum(0), w_gate=dw_gate, conv_w=dconv_w, conv_b=dconv_b, b_in=db_in, w_in=dw_in)
    return dx, grads


def _rwkv_prep(k, wl, za, w0, a0, k_k, k_a, seg):
    w_in = wl + w0
    e_w = jnp.exp(-_softplus(-w_in) - 0.5)
    a = jax.nn.sigmoid(za + a0)
    q = k * k_k
    norm = jnp.sqrt(_seg_sum(q * q, seg))
    n = jnp.maximum(norm, 1e-12)
    kk = q / n
    return w_in, e_w, a, norm, n, kk


def _rwkv_out(y, r, k2, v, gn_g, gn_b, r_k, seg):
    inv = 1.0 / RWKV_HEAD
    yc = y - _seg_sum(y, seg) * inv
    rstd = lax.rsqrt(_seg_sum(yc * yc, seg) * inv + RWKV_GN_EPS)
    yhat = yc * rstd
    s = _seg_sum(r * k2 * r_k, seg)
    return rstd, yhat, s, yhat * gn_g + gn_b + s * v


def _rwkv_fwd(x, p, seq):
    t, d = x.shape
    nseq = t // seq

    def mix_body(rv, cv, pv, nv):
        hn = _rms(rv[0], cv[0])
        xx = _shift_down(hn, _rms(pv[0], cv[0]), 1) - hn
        return [hn] + [hn + xx * cv[1][c:c + 1] for c in range(6)], []
    (hn, xr, xw, xk, xv, xa, xg), _ = _rowwise("b_mix", mix_body, [x], [p["g_pre"], p["mu"]],
                                               [(d, F32)] + [(d, BF16)] * 6, prev=[0], seq=seq)
    r = _mm("b_r", xr, p["w_r"])
    k = _mm("b_k", xk, p["w_k"])
    v = _mm("b_v", xv, p["w_v"])
    lw = _mm("b_w1", xw, p["w1"])
    la = _mm("b_a1", xa, p["a1"], out_dtype=BF16)
    lg = _mm("b_g1", xg, p["g1"])
    (th,), _ = _rowwise("b_tanh", lambda rv, cv, pv, nv: ([jnp.tanh(rv[0])], []), [lw], [], [(lw.shape[1], BF16)])
    (sg,), _ = _rowwise("b_sig", lambda rv, cv, pv, nv: ([jax.nn.sigmoid(rv[0])], []), [lg], [], [(lg.shape[1], BF16)])
    wl = _mm("b_w2", th, p["w2"])
    za = _mm("b_a2", la, p["a2"])
    g = _mm("b_g2", sg, p["g2"])

    def prep_body(rv, cv, pv, nv):
        kk_, wl_, za_ = rv
        _, e_w, a, _, _, kk = _rwkv_prep(kk_, wl_, za_, cv[0], cv[1], cv[2], cv[3], cv[4])
        return [-e_w, kk_ * (1.0 + (a - 1.0) * cv[3]), -kk, kk * a], []
    (log_w, k2, rem_a, rem_b), _ = _rowwise("b_prep", prep_body, [k, wl, za],
                                            [p["w0"], p["a0"], p["k_k"], p["k_a"], p["seg"]], [(d, F32)] * 4, tm=256)
    rec_in = (r, log_w, k2, v, rem_a, rem_b)
    y, *states = _wkv_fwd(*rec_in, seq)

    def out_body(rv, cv, pv, nv):
        y_, r_, k2_, v_, g_ = rv
        _, _, _, out = _rwkv_out(y_, r_, k2_, v_, cv[0], cv[1], cv[2], cv[3])
        return [out * g_], []
    (og,), _ = _rowwise("b_out", out_body, [y, r, k2, v, g], [p["gn_g"], p["gn_b"], p["r_k"], p["seg"]], [(d, BF16)], tm=256)
    o = _mm("b_o", og, p["w_o"])
    res = _resid_norm_fwd("b_res", x, o, p["g_post"])
    return res, (x, hn, xr, xw, xk, xv, xa, xg, r, k, v, th, la, sg, wl, za, g, k2, rec_in, states, y, og, o)


def _rwkv_bwd(saved, dres, p, pt, seq):
    x, hn, xr, xw, xk, xv, xa, xg, r, k, v, th, la, sg, wl, za, g, k2, rec_in, states, y, og, o = saved
    t, d = x.shape
    nseq = t // seq
    do, dg_post, _ = _resid_norm_bwd("b_dres", dres, o, p["g_post"])
    dog = _mm("b_dog", do, pt["w_o_t"])
    dw_o = _mm("b_dwo", og, do, trans_a=True)

    def dout_body(rv, cv, pv, nv):
        dog_, y_, r_, k2_, v_, g_ = rv
        gn_g, gn_b, r_k, bd = cv
        inv = 1.0 / RWKV_HEAD
        rstd, yhat, s, out = _rwkv_out(y_, r_, k2_, v_, gn_g, gn_b, r_k, bd)
        dout = dog_ * g_
        ds = _seg_sum(dout * v_, bd)
        dyhat = dout * gn_g
        dy = rstd * (dyhat - _seg_sum(dyhat, bd) * inv - yhat * _seg_sum(dyhat * yhat, bd) * inv)
        return [dy, dog_ * out, dout * s, ds * k2_ * r_k, ds * r_ * r_k], [ds * r_ * k2_, dout * yhat, dout]
    (dy, dgate, dv_b, dr_b, dk2_b), (dr_k, dgn_g, dgn_b) = _rowwise(
        "b_dout", dout_body, [dog, y, r, k2, v, g], [p["gn_g"], p["gn_b"], p["r_k"], p["seg"]],
        [(d, F32), (d, BF16), (d, F32), (d, F32), (d, F32)], [d, d, d], tm=256)
    dr_rec, dlw_rec, dk2_rec, dv_rec, da_rec, db_rec = _wkv_bwd(*rec_in, *states, dy, seq)

    def dprep_body(rv, cv, pv, nv):
        dr_rec_, dlw_rec_, dk2_rec_, dv_rec_, da_rec_, db_rec_, dr_b_, dk2_b_, dv_b_, k_, wl_, za_ = rv
        w0, a0, k_k, k_a, bd = cv
        w_in, e_w, a, norm, n, kk = _rwkv_prep(k_, wl_, za_, w0, a0, k_k, k_a, bd)
        dk2 = dk2_rec_ + dk2_b_
        dkk = db_rec_ * a - da_rec_
        da = db_rec_ * kk + dk2 * k_ * k_a
        dq = jnp.where(norm > 1e-12, dkk - kk * _seg_sum(kk * dkk, bd), dkk) / n
        dk = dk2 * (1.0 + (a - 1.0) * k_a) + dq * k_k
        dza = da * a * (1.0 - a)
        dwl = dlw_rec_ * (-e_w) * jax.nn.sigmoid(-w_in)
        return [dr_rec_ + dr_b_, dk, dv_rec_ + dv_b_, dza, dwl], [dk2 * k_ * (a - 1.0), dq * k_, dza, dwl]
    (dr, dk, dv, dza, dwl), (dk_a, dk_k, da0, dw0) = _rowwise(
        "b_dprep", dprep_body, [dr_rec, dlw_rec, dk2_rec, dv_rec, da_rec, db_rec, dr_b, dk2_b, dv_b, k, wl, za],
        [p["w0"], p["a0"], p["k_k"], p["k_a"], p["seg"]], [(d, BF16)] * 5, [d] * 4, tm=256)

    dw_r = _mm("b_dwr", xr, dr, trans_a=True)
    dw_k = _mm("b_dwk", xk, dk, trans_a=True)
    dw_v = _mm("b_dwv", xv, dv, trans_a=True)
    dxr = _mm("b_dxr", dr, pt["w_r_t"])
    dxk = _mm("b_dxk", dk, pt["w_k_t"])
    dxv = _mm("b_dxv", dv, pt["w_v_t"])
    da2 = _mm("b_da2", la, dza, trans_a=True)
    dla = _mm("b_dla", dza, pt["a2_t"], out_dtype=BF16)
    da1 = _mm("b_da1", xa, dla, trans_a=True)
    dxa = _mm("b_dxa", dla, pt["a1_t"])
    dw2 = _mm("b_dw2", th, dwl, trans_a=True)
    dth = _mm("b_dth", dwl, pt["w2_t"])
    (dzw,), _ = _rowwise("b_dtanh", lambda rv, cv, pv, nv: ([rv[0] * (1.0 - _f32(rv[1]) * _f32(rv[1]))], []),
                         [dth, th], [], [(th.shape[1], BF16)])
    dw1 = _mm("b_dw1", xw, dzw, trans_a=True)
    dxw = _mm("b_dxw", dzw, pt["w1_t"])
    dg2 = _mm("b_dg2", sg, dgate, trans_a=True)
    dsg = _mm("b_dsg", dgate, pt["g2_t"])
    (dzg,), _ = _rowwise("b_dsig", lambda rv, cv, pv, nv: ([rv[0] * _f32(rv[1]) * (1.0 - _f32(rv[1]))], []),
                         [dsg, sg], [], [(sg.shape[1], BF16)])
    dg1 = _mm("b_dg1", xg, dzg, trans_a=True)
    dxg = _mm("b_dxg", dzg, pt["g1_t"])

    def dmix_body(rv, cv, pv, nv):
        hn_ = rv[0]
        dxs = rv[1:]
        mu = cv[0]
        xx = _shift_down(hn_, pv[0], 1) - hn_
        dsum = dxs[0]
        dxx = dxs[0] * mu[0:1]
        dxx_next = nv[0] * mu[0:1]
        for c in range(1, 6):
            dsum = dsum + dxs[c]
            dxx = dxx + dxs[c] * mu[c:c + 1]
            dxx_next = dxx_next + nv[c] * mu[c:c + 1]
        return [dsum - dxx + _shift_up(dxx, dxx_next, 1)], [dxs[c] * xx for c in range(6)]
    (dhn,), dmu = _rowwise("b_dmix", dmix_body, [hn, dxr, dxw, dxk, dxv, dxa, dxg], [p["mu"]], [(d, F32)], [d] * 6,
                           prev=[0], nxt=[1, 2, 3, 4, 5, 6], seq=seq, tm=256)
    dx, dg_pre = _prenorm_bwd("b_dnorm", dres, dhn, x, p["g_pre"])
    grads = dict(g_pre=dg_pre, g_post=dg_post, mu=jnp.stack([m.sum(0) for m in dmu]), w_r=dw_r, w_k=dw_k, w_v=dw_v,
                 w0=dw0.sum(0), w1=dw1, w2=dw2, a0=da0.sum(0), a1=da1, a2=da2, g1=dg1, g2=dg2, k_k=dk_k.sum(0),
                 k_a=dk_a.sum(0), r_k=dr_k.sum(0), gn_g=dgn_g.sum(0), gn_b=dgn_b.sum(0), w_o=dw_o)
    return dx, grads


_WEIGHTS = ['ln_gains', 'mem_norm', 'a_conv_w', 'a_conv_b', 'a_w_in', 'a_b_in', 'a_gate_w', 'a_gate_b', 'a_lambda', 'a_w_out',
            'a_b_out', 'b_mu', 'b_w_rkv', 'b_w0', 'b_w1', 'b_w2', 'b_a0', 'b_a1', 'b_a2', 'b_g1', 'b_g2', 'b_k_k', 'b_k_a',
            'b_r_k', 'b_gn_g', 'b_gn_b', 'b_w_o', 'c_w_q', 'c_w_kv', 'c_w_o', 'm_w_up', 'm_w_down']
_SHARD_AXIS = dict(ln_gains=2, mem_norm=None, a_conv_w=2, a_conv_b=None, a_w_in=2, a_b_in=None, a_gate_w=3, a_gate_b=3,
                   a_lambda=None, a_w_out=1, a_b_out=None, b_mu=2, b_w_rkv=2, b_w0=1, b_w1=1, b_w2=2, b_a0=1, b_a1=1, b_a2=2,
                   b_g1=1, b_g2=2, b_k_k=1, b_k_a=1, b_r_k=None, b_gn_g=1, b_gn_b=1, b_w_o=1, c_w_q=1, c_w_kv=2, c_w_o=1,
                   m_w_up=2, m_w_down=1)
_MATRICES = ['a_w_in', 'a_gate_w', 'a_w_out', 'b_w_rkv', 'b_w1', 'b_w2', 'b_a1', 'b_a2', 'b_g1', 'b_g2', 'b_w_o', 'c_w_q',
             'c_w_kv', 'c_w_o', 'm_w_up', 'm_w_down']
_SHARDED = [n for n in _WEIGHTS if _SHARD_AXIS[n] is not None]
_VECTORS = [n for n in _SHARDED if n not in _MATRICES]
_REPLICATED = [n for n in _WEIGHTS if _SHARD_AXIS[n] is None]
N_XY = 4
N_DEV = 8
PACK_W = 1024
PACK_ROWS = 256


def _pack(arrs, dtype, row_mult=PACK_ROWS):
    parts = []
    rows = 0
    for a in arrs:
        n = a.size
        r = -(-n // PACK_W)
        parts.append(jnp.pad(a.reshape(-1).astype(dtype), (0, r * PACK_W - n)))
        rows += r
    pad_rows = -(-rows // row_mult) * row_mult - rows
    if pad_rows:
        parts.append(jnp.zeros((pad_rows * PACK_W,), dtype))
    return jnp.concatenate(parts).reshape(-1, PACK_W)


def _unpack(flat, shapes):
    out = []
    row = 0
    for shp in shapes:
        n = 1
        for s in shp:
            n *= s
        r = -(-n // PACK_W)
        out.append(flat[row:row + r].reshape(-1)[:n].reshape(shp))
        row += r
    return out


_ANY = pl.BlockSpec(memory_space=pl.ANY)


def _xy_peers():
    x, y = lax.axis_index("x"), lax.axis_index("y")
    return [(1 - x, y), (x, 1 - y), (1 - x, 1 - y)]


def _all_gather_xy(wm, wv):
    half = wm.shape[0] // 2

    def body(wm_ref, wv_ref, gm_ref, gv_ref, send_sems, recv_sems, local_sems):
        x, y, c = lax.axis_index("x"), lax.axis_index("y"), lax.axis_index("c")
        me = 2 * x + y
        mine = pl.ds(pl.multiple_of(c * half, SUBLANES), half)
        other = pl.ds(pl.multiple_of((1 - c) * half, SUBLANES), half)
        local = [pltpu.make_async_copy(wm_ref, gm_ref.at[me], local_sems.at[0]),
                 pltpu.make_async_copy(wv_ref, gv_ref.at[me], local_sems.at[1])]
        for cp in local:
            cp.start()
        sends, lands, passes, from_sibling = [], [], [], []
        for j, (px, py) in enumerate(_xy_peers()):
            peer = 2 * px + py
            ici = functools.partial(pltpu.make_async_remote_copy, device_id=(px, py, c), device_id_type=MESH)
            sends.append(ici(src_ref=wm_ref.at[mine], dst_ref=gm_ref.at[me, mine], send_sem=send_sems.at[j], recv_sem=recv_sems.at[j]))
            lands.append(ici(src_ref=wm_ref.at[mine], dst_ref=gm_ref.at[peer, mine], send_sem=send_sems.at[j], recv_sem=recv_sems.at[j]))
            sends.append(ici(src_ref=wv_ref, dst_ref=gv_ref.at[me], send_sem=send_sems.at[3 + j], recv_sem=recv_sems.at[3 + j]))
            lands.append(ici(src_ref=wv_ref, dst_ref=gv_ref.at[peer], send_sem=send_sems.at[3 + j], recv_sem=recv_sems.at[3 + j]))
            d2d = functools.partial(pltpu.make_async_remote_copy, send_sem=send_sems.at[6 + j], recv_sem=recv_sems.at[6 + j],
                                    device_id=(x, y, 1 - c), device_id_type=MESH)
            passes.append(d2d(src_ref=gm_ref.at[peer, mine], dst_ref=gm_ref.at[peer, mine]))
            from_sibling.append(d2d(src_ref=gm_ref.at[peer, other], dst_ref=gm_ref.at[peer, other]))
        for cp in sends:
            cp.start()
        for j in range(N_XY - 1):
            lands[2 * j].wait_recv()
            passes[j].start()
        for j in range(N_XY - 1):
            lands[2 * j + 1].wait_recv()
        for cp in from_sibling:
            cp.wait_recv()
        for cp in sends + passes:
            cp.wait_send()
        for cp in local:
            cp.wait()

    return pl.pallas_call(
        body, name="all_gather_weights",
        in_specs=[_ANY, _ANY], out_specs=[_ANY, _ANY],
        out_shape=[jax.ShapeDtypeStruct((N_XY,) + wm.shape, wm.dtype), jax.ShapeDtypeStruct((N_XY,) + wv.shape, wv.dtype)],
        scratch_shapes=[pltpu.SemaphoreType.DMA((9,)), pltpu.SemaphoreType.DMA((9,)), pltpu.SemaphoreType.DMA((2,))],
    )(wm, wv)


_HBM = pl.BlockSpec(memory_space=pltpu.HBM)
_SEM = pl.BlockSpec(memory_space=pltpu.SEMAPHORE)
_SPLIT_COPY = functools.partial(pltpu.CompilerParams, has_side_effects=pltpu.SideEffectType.DATAFLOW_SIDE_EFFECTING)


def _gather_xy_start(name, buf, after):
    def body(src_ref, land_ref, after_ref, send_sems, recv_sems, src_thru, land_thru, token):
        x, y, c = lax.axis_index("x"), lax.axis_index("y"), lax.axis_index("c")
        for j, (px, py) in enumerate(_xy_peers()):
            pltpu.make_async_remote_copy(src_ref=src_ref, dst_ref=land_ref.at[2 * x + y], send_sem=send_sems.at[j],
                                         recv_sem=recv_sems.at[j], device_id=(px, py, c), device_id_type=MESH).start()
        token[...] = jnp.zeros_like(token)

    n_peers = N_XY - 1
    land = pltpu.with_memory_space_constraint(lax.empty((N_XY,) + buf.shape, buf.dtype), pltpu.HBM)
    return pl.pallas_call(
        body, name=name,
        out_shape=(pltpu.SemaphoreType.DMA((n_peers,)), pltpu.SemaphoreType.DMA((n_peers,)), pltpu.HBM(buf.shape, buf.dtype),
                   pltpu.HBM(land.shape, buf.dtype), jax.ShapeDtypeStruct((SUBLANES, LANES), F32)),
        in_specs=(_HBM, _HBM, _ANY), out_specs=(_SEM, _SEM, _HBM, _HBM, pl.BlockSpec(memory_space=pltpu.VMEM)),
        input_output_aliases={0: 2, 1: 3}, compiler_params=_SPLIT_COPY(),
    )(pltpu.with_memory_space_constraint(buf, pltpu.HBM), land, after)


def _gather_xy_wait(name, send_sems, recv_sems, src_thru, land_thru, after):
    def body(src_ref, land_ref, send_sems, recv_sems, after_ref, src_dead, got_ref):
        c = lax.axis_index("c")
        for j, (px, py) in enumerate(_xy_peers()):
            cp = pltpu.make_async_remote_copy(src_ref=src_ref, dst_ref=land_ref.at[2 * px + py], send_sem=send_sems.at[j],
                                              recv_sem=recv_sems.at[j], device_id=(px, py, c), device_id_type=MESH)
            cp.wait_send()
            cp.wait_recv()

    return pl.pallas_call(
        body, name=name,
        out_shape=(pltpu.HBM(src_thru.shape, src_thru.dtype), pltpu.HBM(land_thru.shape, land_thru.dtype)),
        in_specs=(_HBM, _HBM, _SEM, _SEM, _ANY), out_specs=(_HBM, _HBM),
        input_output_aliases={0: 0, 1: 1}, compiler_params=_SPLIT_COPY(),
    )(src_thru, land_thru, send_sems, recv_sems, after)[1]


def _exchange_xy_start(name, gsend, after):
    def body(src_ref, land_ref, after_ref, send_sems, recv_sems, src_thru, land_thru, token):
        c = lax.axis_index("c")
        for j, (px, py) in enumerate(_xy_peers()):
            pltpu.make_async_remote_copy(src_ref=src_ref.at[2 * px + py], dst_ref=land_ref.at[j], send_sem=send_sems.at[j],
                                         recv_sem=recv_sems.at[j], device_id=(px, py, c), device_id_type=MESH).start()
        token[...] = jnp.zeros_like(token)

    n_peers = N_XY - 1
    land = pltpu.with_memory_space_constraint(lax.empty((n_peers,) + gsend.shape[1:], gsend.dtype), pltpu.HBM)
    return pl.pallas_call(
        body, name=name,
        out_shape=(pltpu.SemaphoreType.DMA((n_peers,)), pltpu.SemaphoreType.DMA((n_peers,)), pltpu.HBM(gsend.shape, gsend.dtype),
                   pltpu.HBM(land.shape, gsend.dtype), jax.ShapeDtypeStruct((SUBLANES, LANES), F32)),
        in_specs=(_HBM, _HBM, _ANY), out_specs=(_SEM, _SEM, _HBM, _HBM, pl.BlockSpec(memory_space=pltpu.VMEM)),
        input_output_aliases={0: 2, 1: 3}, compiler_params=_SPLIT_COPY(),
    )(pltpu.with_memory_space_constraint(gsend, pltpu.HBM), land, after)


def _exchange_xy_wait(name, send_sems, recv_sems, src_thru, land_thru, after):
    def body(src_ref, land_ref, send_sems, recv_sems, after_ref, src_dead, got_ref):
        c = lax.axis_index("c")
        for j, (px, py) in enumerate(_xy_peers()):
            cp = pltpu.make_async_remote_copy(src_ref=src_ref.at[2 * px + py], dst_ref=land_ref.at[j], send_sem=send_sems.at[j],
                                              recv_sem=recv_sems.at[j], device_id=(px, py, c), device_id_type=MESH)
            cp.wait_send()
            cp.wait_recv()

    return pl.pallas_call(
        body, name=name,
        out_shape=(pltpu.HBM(src_thru.shape, src_thru.dtype), pltpu.HBM(land_thru.shape, land_thru.dtype)),
        in_specs=(_HBM, _HBM, _SEM, _SEM, _ANY), out_specs=(_HBM, _HBM),
        input_output_aliases={0: 0, 1: 1}, compiler_params=_SPLIT_COPY(),
    )(src_thru, land_thru, send_sems, recv_sems, after)[1]


def _exchange_xy(gsend):
    def body(gs_ref, recv_ref, send_sems, recv_sems):
        c = lax.axis_index("c")
        sends = []
        for j, (px, py) in enumerate(_xy_peers()):
            sends.append(pltpu.make_async_remote_copy(
                src_ref=gs_ref.at[2 * px + py], dst_ref=recv_ref.at[j], send_sem=send_sems.at[j], recv_sem=recv_sems.at[j],
                device_id=(px, py, c), device_id_type=MESH))
        for cp in sends:
            cp.start()
        for cp in sends:
            cp.wait_recv()
        for cp in sends:
            cp.wait_send()

    return pl.pallas_call(
        body, name="exchange_grads",
        in_specs=[_ANY], out_specs=_ANY,
        out_shape=jax.ShapeDtypeStruct((N_XY - 1,) + gsend.shape[1:], gsend.dtype),
        scratch_shapes=[pltpu.SemaphoreType.DMA((3,)), pltpu.SemaphoreType.DMA((3,))],
    )(gsend)


def _swap_with_sibling(name, part):
    def body(p_ref, got_ref, send_sem, recv_sem):
        x, y, c = lax.axis_index("x"), lax.axis_index("y"), lax.axis_index("c")
        cp = pltpu.make_async_remote_copy(src_ref=p_ref, dst_ref=got_ref, send_sem=send_sem, recv_sem=recv_sem,
                                          device_id=(x, y, 1 - c), device_id_type=MESH)
        cp.start()
        cp.wait_recv()
        cp.wait_send()

    return pl.pallas_call(
        body, name=name,
        in_specs=[_ANY], out_specs=_ANY, out_shape=jax.ShapeDtypeStruct(part.shape, part.dtype),
        scratch_shapes=[pltpu.SemaphoreType.DMA, pltpu.SemaphoreType.DMA],
    )(part)


def _all_gather_all(vec):
    def body(v_ref, out_ref, send_sems, recv_sems, local_sem):
        x, y, c = lax.axis_index("x"), lax.axis_index("y"), lax.axis_index("c")
        me = 4 * x + 2 * y + c
        local = pltpu.make_async_copy(v_ref, out_ref.at[me], local_sem)
        local.start()
        sends, recvs = [], []
        for f in range(1, N_DEV):
            fx, fy, fc = (f >> 2) & 1, (f >> 1) & 1, f & 1
            px = (1 - x) if fx else x
            py = (1 - y) if fy else y
            pc = (1 - c) if fc else c
            mk = functools.partial(pltpu.make_async_remote_copy, src_ref=v_ref, send_sem=send_sems.at[f - 1],
                                   recv_sem=recv_sems.at[f - 1], device_id=(px, py, pc), device_id_type=MESH)
            sends.append(mk(dst_ref=out_ref.at[me]))
            recvs.append(mk(dst_ref=out_ref.at[4 * px + 2 * py + pc]))
        for cp in sends:
            cp.start()
        for cp in recvs:
            cp.wait_recv()
        for cp in sends:
            cp.wait_send()
        local.wait()

    return pl.pallas_call(
        body, name="all_gather_replicated",
        in_specs=[_ANY], out_specs=_ANY, out_shape=jax.ShapeDtypeStruct((N_DEV,) + vec.shape, vec.dtype),
        scratch_shapes=[pltpu.SemaphoreType.DMA((N_DEV - 1,)), pltpu.SemaphoreType.DMA((N_DEV - 1,)), pltpu.SemaphoreType.DMA],
    )(vec)


def _adamw(g, w, m, v):
    m2 = ADAM_B1 * m + (1.0 - ADAM_B1) * g
    v2 = ADAM_B2 * v + (1.0 - ADAM_B2) * g * g
    m_hat = m2 / (1.0 - ADAM_B1 ** ADAM_STEP)
    v_hat = v2 / (1.0 - ADAM_B2 ** ADAM_STEP)
    return -ADAM_LR * (m_hat / (jnp.sqrt(v_hat) + ADAM_EPS) + ADAM_WD * w), m2, v2


def _sum_contributions(name, own, recv):
    def body(rv, cv, pv, nv):
        return [((rv[0] + _f32(rv[1])) + _f32(rv[2])) + _f32(rv[3])], []
    stacked = recv.reshape(-1, PACK_W)
    (part,), _ = _rowwise(name, body, [own] + [(stacked, PACK_W, 0, j * own.shape[0]) for j in range(N_XY - 1)], [],
                          [(PACK_W, F32)])
    return part


def _adamw_sharded(name, part, sib, first, w, m, v, w_first=0, n_rows=None):
    def body(rv, cv, pv, nv):
        g = rv[3] + rv[4]
        return [g, *_adamw(g, rv[0], rv[1], rv[2])], []
    n_rows = w.shape[0] if n_rows is None else n_rows
    rows = [(z, PACK_W, 0, w_first) for z in (w, m, v)] + [(part, PACK_W, 0, first), (sib, PACK_W, 0, first)]
    outs, _ = _rowwise(name, body, rows, [], [(PACK_W, F32)] * 4, tm=256, n_rows=n_rows)
    return outs


def _adamw_replicated(parts, w, m, v):
    def body(rv, cv, pv, nv):
        g = rv[0]
        for i in range(1, N_DEV):
            g = g + rv[i]
        return [g, *_adamw(g, rv[N_DEV], rv[N_DEV + 1], rv[N_DEV + 2])], []
    outs, _ = _rowwise("adamw_replicated", body, [parts[i] for i in range(N_DEV)] + [w, m, v], [], [(PACK_W, F32)] * 4)
    return outs


def _row(v):
    return v.reshape(1, -1).astype(F32)


def _gate_dense(gate_w):
    _, nh, blk, _ = gate_w.shape
    d = nh * blk
    dense = jnp.zeros((d, 2 * d), gate_w.dtype)
    for g in range(2):
        for h in range(nh):
            dense = lax.dynamic_update_slice(dense, gate_w[g, h], (h * blk, g * d + h * blk))
    return dense


def _gate_blocks(dense, nh):
    d = dense.shape[0]
    blk = d // nh
    return jnp.stack([jnp.stack([dense[h * blk:(h + 1) * blk, g * d + h * blk:g * d + (h + 1) * blk] for h in range(nh)])
                      for g in range(2)])


def _local_step(x3, mem3, target3, fw, late_weights=None, on_grads=None):
    def with_late(stage, after):
        if late_weights is None:
            return fw
        got = late_weights(stage, after)
        return {**fw, **{n: ({**fw.get(n, {}), **v} if isinstance(v, dict) else v) for n, v in got.items()}}

    nseq, seq, d = x3.shape
    mem_len = mem3.shape[1]
    t = nseq * seq
    x0 = x3.reshape(t, d)
    mem2 = mem3.reshape(nseq * mem_len, d)
    target = target3.reshape(t, d)
    ln = fw["ln_gains"]
    gains = [[_row(ln[i, j]) for j in range(6)] for i in range(2)]
    nh = d // RWKV_HEAD
    seg = (jnp.arange(d)[:, None] // RWKV_HEAD == jnp.arange(LANES)[None, :]).astype(BF16)

    w_gate = _gate_dense(fw["a_gate_w"][0])
    pa = dict(g_pre=gains[0][0], g_post=gains[0][1], w_in=fw["a_w_in"][0], b_in_y=_row(fw["a_b_in"][0, :d]),
              b_in_u=_row(fw["a_b_in"][0, d:]), conv_w=fw["a_conv_w"][0].astype(F32), conv_b=_row(fw["a_conv_b"][0]),
              w_gate=w_gate, gate_b0=_row(fw["a_gate_b"][0, 0]), gate_b1=_row(fw["a_gate_b"][0, 1]), lam=_row(fw["a_lambda"][0]),
              w_out=fw["a_w_out"][0], b_out=_row(fw["a_b_out"][0]))
    pta = dict(w_in_t=_t(pa["w_in"]), w_gate_t=_t(w_gate), w_out_t=_t(pa["w_out"]))
    mem_g = _row(fw["mem_norm"])

    mem_n = _norm_fwd("mem_norm", mem2, mem_g)
    x1, sv_a = _rglru_fwd(x0, pa, seq)
    fw = with_late("cm0", x1)
    x2, sv_c0 = _xattn_fwd("c0", x1, mem_n, gains[0][2], gains[0][3], fw["c_w_q"][0], fw["c_w_kv"][0], fw["c_w_o"][0], seq, mem_len)
    x3_, sv_m0 = _mlp_fwd("m0", x2, gains[0][4], gains[0][5], fw["m_w_up"][0], fw["m_w_down"][0])
    fw = with_late("layer1", x3_)
    pb = dict(g_pre=gains[1][0], g_post=gains[1][1], mu=fw["b_mu"][0].astype(F32), w_r=fw["b_w_rkv"][0, 0],
              w_k=fw["b_w_rkv"][0, 1], w_v=fw["b_w_rkv"][0, 2], w0=_row(fw["b_w0"][0]), w1=fw["b_w1"][0], w2=fw["b_w2"][0],
              a0=_row(fw["b_a0"][0]), a1=fw["b_a1"][0], a2=fw["b_a2"][0], g1=fw["b_g1"][0], g2=fw["b_g2"][0],
              k_k=_row(fw["b_k_k"][0]), k_a=_row(fw["b_k_a"][0]), r_k=_row(fw["b_r_k"][0]), gn_g=_row(fw["b_gn_g"][0]),
              gn_b=_row(fw["b_gn_b"][0]), w_o=fw["b_w_o"][0], seg=seg)
    ptb = {k + "_t": _t(pb[k]) for k in ("w_r", "w_k", "w_v", "w_o", "w1", "w2", "a1", "a2", "g1", "g2")}
    x4, sv_b = _rwkv_fwd(x3_, pb, seq)
    x5, sv_c1 = _xattn_fwd("c1", x4, mem_n, gains[1][2], gains[1][3], fw["c_w_q"][1], fw["c_w_kv"][1], fw["c_w_o"][1], seq, mem_len)
    x6, sv_m1 = _mlp_fwd("m1", x5, gains[1][4], gains[1][5], fw["m_w_up"][1], fw["m_w_down"][1])

    def loss_body(rv, cv, pv, nv):
        err = rv[0] - rv[1]
        return [err * (1.0 / d)], [err * err]
    (dx,), (sq,) = _rowwise("loss", loss_body, [x6, target], [], [(d, F32)], [d])
    loss_part = 0.5 / d * jnp.sum(sq)

    dx, g_m1 = _mlp_bwd("m1", sv_m1, dx, gains[1][4], gains[1][5], _t(fw["m_w_up"][1]), _t(fw["m_w_down"][1]))
    dx, dmem1, g_c1 = _xattn_bwd("c1", sv_c1, dx, mem_n, gains[1][2], gains[1][3], _t(fw["c_w_q"][1]), _t(fw["c_w_kv"][1]),
                                 _t(fw["c_w_o"][1]), seq, mem_len)
    dx, g_b = _rwkv_bwd(sv_b, dx, pb, ptb, seq)
    grads_b = dict(
        b_mu=g_b["mu"][None], b_w_rkv=jnp.stack([g_b["w_r"], g_b["w_k"], g_b["w_v"]])[None], b_w0=g_b["w0"][None],
        b_w1=g_b["w1"][None], b_w2=g_b["w2"][None], b_a0=g_b["a0"][None], b_a1=g_b["a1"][None], b_a2=g_b["a2"][None],
        b_g1=g_b["g1"][None], b_g2=g_b["g2"][None], b_k_k=g_b["k_k"][None], b_k_a=g_b["k_a"][None],
        b_r_k=g_b["r_k"].reshape(1, nh, RWKV_HEAD), b_gn_g=g_b["gn_g"][None], b_gn_b=g_b["gn_b"][None], b_w_o=g_b["w_o"][None])
    g_m0_pre = gains[0][4]
    if on_grads is not None:
        layer1 = dict(c_w_q=g_c1["w_q"], c_w_kv=g_c1["w_kv"], c_w_o=g_c1["w_o"], m_w_up=g_m1["w_up"], m_w_down=g_m1["w_down"])
        g_m0_pre = g_m0_pre + on_grads("layer1", {**grads_b, **{n: [None, g] for n, g in layer1.items()}})[0, 0]
    dx, g_m0 = _mlp_bwd("m0", sv_m0, dx, g_m0_pre, gains[0][5], _t(fw["m_w_up"][0]), _t(fw["m_w_down"][0]))
    dx, dmem0, g_c0 = _xattn_bwd("c0", sv_c0, dx, mem_n, gains[0][2], gains[0][3], _t(fw["c_w_q"][0]), _t(fw["c_w_kv"][0]),
                                 _t(fw["c_w_o"][0]), seq, mem_len)
    if on_grads is not None:
        cm0 = dict(c_w_q=g_c0["w_q"], c_w_kv=g_c0["w_kv"], c_w_o=g_c0["w_o"], m_w_up=g_m0["w_up"], m_w_down=g_m0["w_down"])
        pa = {**pa, "g_post": pa["g_post"] + on_grads("cm0", {n: [g, None] for n, g in cm0.items()})[0, 0]}
    dx, g_a = _rglru_bwd(sv_a, dx, pa, pta, seq)

    def dmem_body(rv, cv, pv, nv):
        _, dg = _rms_bwd(rv[1] + rv[2], rv[0], cv[0])
        return [], [dg]
    _, (dmem_g,) = _rowwise("mem_norm_grad", dmem_body, [mem2, dmem0, dmem1], [mem_g], [], [d])

    lru_heads = fw["a_gate_w"].shape[2]
    blk = d // lru_heads
    grads = dict(
        ln_gains=jnp.stack([jnp.stack([g_a["g_pre"], g_a["g_post"], g_c0["g_pre"], g_c0["g_post"], g_m0["g_pre"], g_m0["g_post"]]),
                            jnp.stack([g_b["g_pre"], g_b["g_post"], g_c1["g_pre"], g_c1["g_post"], g_m1["g_pre"], g_m1["g_post"]])]),
        mem_norm=dmem_g.sum(0),
        a_conv_w=g_a["conv_w"][None], a_conv_b=g_a["conv_b"][None], a_w_in=g_a["w_in"][None], a_b_in=g_a["b_in"][None],
        a_gate_w=_gate_blocks(g_a["w_gate"], lru_heads)[None],
        a_gate_b=jnp.stack([g_a["gate_b0"], g_a["gate_b1"]]).reshape(1, 2, lru_heads, blk),
        a_lambda=g_a["lam"][None], a_w_out=g_a["w_out"][None], a_b_out=g_a["b_out"][None],
        **grads_b,
        c_w_q=[g_c0["w_q"], g_c1["w_q"]], c_w_kv=[g_c0["w_kv"], g_c1["w_kv"]], c_w_o=[g_c0["w_o"], g_c1["w_o"]],
        m_w_up=[g_m0["w_up"], g_m1["w_up"]], m_w_down=[g_m0["w_down"], g_m1["w_down"]],
    )
    return loss_part, dx.reshape(nseq, seq, d), grads


def kernel(x, mem, ln_gains, mem_norm, a_conv_w, a_conv_b, a_w_in, a_b_in, a_gate_w, a_gate_b, a_lambda, a_w_out, a_b_out, b_mu, b_w_rkv, b_w0, b_w1, b_w2, b_a0, b_a1, b_a2, b_g1, b_g2, b_k_k, b_k_a, b_r_k, b_gn_g, b_gn_b, b_w_o, c_w_q, c_w_kv, c_w_o, m_w_up, m_w_down, loss_target, m_ln_gains, m_mem_norm, m_a_conv_w, m_a_conv_b, m_a_w_in, m_a_b_in, m_a_gate_w, m_a_gate_b, m_a_lambda, m_a_w_out, m_a_b_out, m_b_mu, m_b_w_rkv, m_b_w0, m_b_w1, m_b_w2, m_b_a0, m_b_a1, m_b_a2, m_b_g1, m_b_g2, m_b_k_k, m_b_k_a, m_b_r_k, m_b_gn_g, m_b_gn_b, m_b_w_o, m_c_w_q, m_c_w_kv, m_c_w_o, m_m_w_up, m_m_w_down, v_ln_gains, v_mem_norm, v_a_conv_w, v_a_conv_b, v_a_w_in, v_a_b_in, v_a_gate_w, v_a_gate_b, v_a_lambda, v_a_w_out, v_a_b_out, v_b_mu, v_b_w_rkv, v_b_w0, v_b_w1, v_b_w2, v_b_a0, v_b_a1, v_b_a2, v_b_g1, v_b_g2, v_b_k_k, v_b_k_a, v_b_r_k, v_b_gn_g, v_b_gn_b, v_b_w_o, v_c_w_q, v_c_w_kv, v_c_w_o, v_m_w_up, v_m_w_down):
    given = dict(locals())
    w = {n: given[n] for n in _WEIGHTS}
    mom1 = {n: given["m_" + n] for n in _WEIGHTS}
    mom2 = {n: given["v_" + n] for n in _WEIGHTS}

    me = 2 * lax.axis_index("x") + lax.axis_index("y")
    per_layer = [n for n in _MATRICES if n[0] in "cm"]
    early = [(n, None) for n in _MATRICES if n[0] == "a"]
    late = dict(cm0=[(n, 0) for n in per_layer], layer1=[(n, None) for n in _MATRICES if n[0] == "b"] + [(n, 1) for n in per_layer])
    piece = lambda n, layer: w[n] if layer is None else w[n][layer:layer + 1]

    def gathered(entries, buf):
        shards = [_unpack(buf[s], [piece(n, layer).shape for n, layer in entries]) for s in range(N_XY)]
        out = {}
        for i, (n, layer) in enumerate(entries):
            full = jnp.concatenate([shards[s][i] for s in range(N_XY)], axis=_SHARD_AXIS[n])
            if layer is None:
                out[n] = full
            else:
                out.setdefault(n, {})[layer] = full[0]
        return out

    late_bufs = {stage: _pack([piece(n, layer) for n, layer in entries], BF16) for stage, entries in late.items()}
    gm, gv = _all_gather_xy(_pack([piece(n, layer) for n, layer in early], BF16), _pack([w[n] for n in _VECTORS], F32, SUBLANES))
    in_flight = {}
    token = gv
    for stage in late:
        *in_flight[stage], token = _gather_xy_start("gather_%s_start" % stage, late_bufs[stage], token)
    fw = {n: w[n] for n in _REPLICATED} | gathered(early, gm)
    vec_shards = [_unpack(gv[s], [w[n].shape for n in _VECTORS]) for s in range(N_XY)]
    for i, n in enumerate(_VECTORS):
        fw[n] = jnp.concatenate([vec_shards[s][i] for s in range(N_XY)], axis=_SHARD_AXIS[n])
    fw["ln_gains"] = fw["ln_gains"] + token[0, 0]

    def late_weights(stage, after):
        land = _gather_xy_wait("gather_%s_wait" % stage, *in_flight[stage], after)
        return gathered(late[stage], lax.dynamic_update_index_in_dim(land, late_bufs[stage], me, 0))

    tile_rows = 256
    groups = dict(layer1=[(n, 1 if n in per_layer else None) for n in _SHARDED if n in per_layer or n[0] == "b"],
                  cm0=[(n, 0) for n in _SHARDED if n in per_layer],
                  rest=[(n, None) for n in _SHARDED if n not in per_layer and n[0] != "b"])

    def piece_shape(n, layer):
        return w[n].shape if layer is None else (1,) + w[n].shape[1:]

    def rows_of(n, layer):
        size = 1
        for s_ in piece_shape(n, layer):
            size *= s_
        return size // PACK_W

    def split(entries):
        in_place = [e for e in entries if w[e[0]].shape[-1] == PACK_W and rows_of(*e) % tile_rows == 0 and rows_of(*e) > 0]
        in_place.sort(key=lambda e: -rows_of(*e))
        return in_place, [e for e in entries if e not in in_place]

    def buffers(entries, grads):
        in_place, packed = split(entries)

        def pieces(s):
            out = []
            for n, layer in in_place + packed:
                ax = _SHARD_AXIS[n]
                size = w[n].shape[ax]
                g = grads[n] if layer is None else grads[n][layer]
                out.append(lax.dynamic_slice_in_dim(g, s * size, size, axis=ax if layer is None else ax - 1))
            return out
        return jnp.stack([_pack(pieces(s), BF16) for s in range(N_XY)]), _pack(pieces(me), F32)

    def update(tag, entries, own, recv):
        in_place, packed = split(entries)
        part = _sum_contributions("sum_grads_" + tag, own, recv)
        sib = _swap_with_sibling("swap_sibling_" + tag, part)
        out = {}
        first = 0
        for n, layer in in_place:
            flat = [src[n].reshape(-1, PACK_W) for src in (w, mom1, mom2)]
            rows = rows_of(n, layer)
            res = _adamw_sharded("adamw_%s_%s" % (n, tag), part, sib, first, *flat, w_first=(layer or 0) * rows, n_rows=rows)
            out[(n, layer)] = [o.reshape(piece_shape(n, layer)) for o in res]
            first += rows
        take = lambda src, n, layer: src[n] if layer is None else src[n][layer:layer + 1]
        flat = [_pack([take(src, n, layer) for n, layer in packed], F32) for src in (w, mom1, mom2)]
        assert first + flat[0].shape[0] == part.shape[0], (first, flat[0].shape, part.shape)
        res = _adamw_sharded("adamw_packed_" + tag, part, sib, first, *flat)
        tail = [_unpack(o, [piece_shape(n, layer) for n, layer in packed]) for o in res]
        for i, e in enumerate(packed):
            out[e] = [tail[kind][i] for kind in range(4)]
        return out

    sent = {}

    def on_grads(stage, grads_so_far):
        gsend, own = buffers(groups[stage], grads_so_far)
        *handles, token = _exchange_xy_start("exchange_%s_start" % stage, gsend, own)
        sent[stage] = (own, handles)
        return token

    loss_part, grad_x, grads = _local_step(x, mem, loss_target, fw, late_weights, on_grads)

    gsend, own = buffers(groups["rest"], grads)
    out = update("rest", groups["rest"], own, _exchange_xy(gsend))
    for stage, (own, handles) in sent.items():
        out |= update(stage, groups[stage], own, _exchange_xy_wait("exchange_%s_wait" % stage, *handles, grad_x))
    sharded_out = [[] for _ in range(4)]
    for n in _SHARDED:
        for kind in range(4):
            if n in per_layer:
                sharded_out[kind].append(jnp.concatenate([out[(n, 0)][kind], out[(n, 1)][kind]], axis=0))
            else:
                sharded_out[kind].append(out[(n, None)][kind])

    small = _pack([grads[n] for n in _REPLICATED] + [loss_part.reshape(1)], F32, SUBLANES)
    parts = _all_gather_all(small)
    zero = jnp.zeros((1,), F32)
    flat = [_pack([src[n] for n in _REPLICATED] + [zero], F32, SUBLANES) for src in (w, mom1, mom2)]
    repl_out = [_unpack(o, [w[n].shape for n in _REPLICATED] + [(1,)]) for o in _adamw_replicated(parts, *flat)]
    loss = repl_out[0][-1][0]

    result = [loss, grad_x]
    for kind in range(4):
        by_name = dict(zip(_SHARDED, sharded_out[kind])) | dict(zip(_REPLICATED, repl_out[kind][:-1]))
        result += [by_name[n] for n in _WEIGHTS]
    return tuple(result)
```

```python
import functools

import jax
import jax.numpy as jnp
from jax import lax
from jax.experimental import pallas as pl
from jax.experimental.pallas import tpu as pltpu

F32 = jnp.float32
BF16 = jnp.bfloat16
MESH = pl.DeviceIdType.MESH

LANES = 128
SUBLANES = 8
VMEM_LIMIT_BYTES = 48 * 1024 * 1024

RMS_EPS = 1e-6
LRU_C = 8.0
LRU_HEADS = 4
CONV_WIDTH = 4
RWKV_HEAD = 64
RWKV_GN_EPS = 64e-5
MEM_HEADS = 4
ADAM_LR = 0.001
ADAM_B1 = 0.9
ADAM_B2 = 0.999
ADAM_EPS = 1e-08
ADAM_WD = 0.01
ADAM_STEP = 10
WKV_CHUNK = 64

_PARAMS = functools.partial(pltpu.CompilerParams, vmem_limit_bytes=VMEM_LIMIT_BYTES)


def _tile(n, want):
    if n <= want:
        return n
    t = want
    while t >= SUBLANES:
        if n % t == 0 and t % SUBLANES == 0:
            return t
        t -= SUBLANES
    return n


def _fold8(v):
    tm, d = v.shape
    if tm == SUBLANES:
        return v
    return jnp.sum(v.reshape(tm // SUBLANES, SUBLANES, d), axis=0)


def _rowwise(name, body, rows, consts=(), out_rows=(), out_accs=(), prev=(), nxt=(), tm=512, seq=None, n_rows=None):
    rows = [r if isinstance(r, tuple) else (r, r.shape[1], 0) for r in rows]
    rows = [r if len(r) == 4 else r + (0,) for r in rows]
    t = rows[0][0].shape[0] if n_rows is None else n_rows
    tm = _tile(t, tm)
    if seq is not None:
        tm = _tile(seq, tm)
    nblk = t // tm
    nrow, ncst, nprev, nnxt = len(rows), len(consts), len(prev), len(nxt)
    nor, noa = len(out_rows), len(out_accs)
    hb = tm // SUBLANES

    def kern(*refs):
        i = pl.program_id(0)
        rv = [r[...] for r in refs[:nrow]]
        cv = [c[...] for c in refs[nrow:nrow + ncst]]
        o = nrow + ncst
        pv = []
        for j in range(nprev):
            at_start = (i * tm) % seq == 0
            h = refs[o + j][...]
            pv.append(jnp.where(at_start, jnp.zeros_like(h), h))
        o += nprev
        nv = []
        for j in range(nnxt):
            at_end = ((i + 1) * tm) % seq == 0
            h = refs[o + j][...]
            nv.append(jnp.where(at_end, jnp.zeros_like(h), h))
        o += nnxt
        outs, accs = body(rv, cv, pv, nv)
        for j in range(nor):
            refs[o + j][...] = outs[j].astype(refs[o + j].dtype)
        o += nor
        if noa:
            @pl.when(i == 0)
            def _():
                for j in range(noa):
                    refs[o + j][...] = jnp.zeros_like(refs[o + j])
            for j in range(noa):
                refs[o + j][...] += _fold8(accs[j].astype(F32))

    assert all(first % tm == 0 for (_, _, _, first) in rows), name
    in_specs = [pl.BlockSpec((tm, w), functools.partial(lambda i, c, o: (i + o, c), c=cb, o=first // tm))
                for (_, w, cb, first) in rows]
    in_specs += [pl.BlockSpec(c.shape, lambda i: (0, 0)) for c in consts]
    in_specs += [pl.BlockSpec((SUBLANES, rows[j][1]),
                              functools.partial(lambda i, c: (jnp.maximum(i * hb - 1, 0), c), c=rows[j][2])) for j in prev]
    in_specs += [pl.BlockSpec((SUBLANES, rows[j][1]),
                              functools.partial(lambda i, c: (jnp.minimum((i + 1) * hb, t // SUBLANES - 1), c), c=rows[j][2]))
                 for j in nxt]
    out_shape = [jax.ShapeDtypeStruct((t, w), dt) for (w, dt) in out_rows]
    out_shape += [jax.ShapeDtypeStruct((SUBLANES, w), F32) for w in out_accs]
    out_specs = [pl.BlockSpec((tm, w), lambda i: (i, 0)) for (w, _) in out_rows]
    out_specs += [pl.BlockSpec((SUBLANES, w), lambda i: (0, 0)) for w in out_accs]
    args = [r[0] for r in rows] + list(consts) + [rows[j][0] for j in prev] + [rows[j][0] for j in nxt]
    res = pl.pallas_call(
        kern, name=name, grid=(nblk,), in_specs=in_specs, out_specs=out_specs, out_shape=out_shape,
        compiler_params=_PARAMS(dimension_semantics=("arbitrary",)),
    )(*args)
    return list(res[:nor]), list(res[nor:])


def _shift_down(x, halo, k):
    rolled = pltpu.roll(x, k, 0)
    row = lax.broadcasted_iota(jnp.int32, (SUBLANES, x.shape[1]), 0)
    first = jnp.where(row < k, pltpu.roll(halo, k, 0), rolled[:SUBLANES])
    if x.shape[0] == SUBLANES:
        return first
    return jnp.concatenate([first, rolled[SUBLANES:]], axis=0)


def _shift_up(x, halo, k):
    n = x.shape[0]
    rolled = pltpu.roll(x, n - k, 0)
    row = lax.broadcasted_iota(jnp.int32, (SUBLANES, x.shape[1]), 0)
    last = jnp.where(row >= SUBLANES - k, pltpu.roll(halo, SUBLANES - k, 0), rolled[n - SUBLANES:])
    if n == SUBLANES:
        return last
    return jnp.concatenate([rolled[:n - SUBLANES], last], axis=0)


class _Transposed:
    def __init__(self, w):
        self.w = w


def _t(w):
    return _Transposed(w)


def _mm(name, a, b, out_dtype=F32, trans_a=False, tm=1024, tn=1024, tk=1024, epilogue=None, extra=None):
    trans_b = isinstance(b, _Transposed)
    assert not (trans_a and trans_b)
    if trans_b:
        b = b.w
    if trans_a:
        kdim, m = a.shape
    else:
        m, kdim = a.shape
    n = b.shape[0] if trans_b else b.shape[1]
    assert b.shape[1 if trans_b else 0] == kdim, (name, a.shape, b.shape)
    tm, tn, tk = _tile(m, tm), _tile(n, tn), _tile(kdim, tk)
    nk = kdim // tk
    dims = (((0,), (0,)), ((), ())) if trans_a else (((1,), (1 if trans_b else 0,)), ((), ()))

    n_in = 2 if extra is None else 3

    def kern(*refs):
        a_ref, b_ref, o_ref, acc = refs[0], refs[1], refs[n_in], refs[n_in + 1:]

        def store(res):
            if epilogue is not None:
                res = epilogue(res) if extra is None else epilogue(res, refs[2][...])
            o_ref[...] = res.astype(o_ref.dtype)

        part = lax.dot_general(a_ref[...].astype(BF16), b_ref[...].astype(BF16), dims, preferred_element_type=F32)
        if nk == 1:
            store(part)
        else:
            k = pl.program_id(2)

            @pl.when(k == 0)
            def _():
                acc[0][...] = part

            @pl.when(k > 0)
            def _():
                acc[0][...] += part

            @pl.when(k == nk - 1)
            def _():
                store(acc[0][...])

    a_spec = pl.BlockSpec((tk, tm), lambda i, j, k: (k, i)) if trans_a else pl.BlockSpec((tm, tk), lambda i, j, k: (i, k))
    b_spec = pl.BlockSpec((tn, tk), lambda i, j, k: (j, k)) if trans_b else pl.BlockSpec((tk, tn), lambda i, j, k: (k, j))
    out_spec = pl.BlockSpec((tm, tn), lambda i, j, k: (i, j))
    return pl.pallas_call(
        kern, name=name, grid=(m // tm, n // tn, nk),
        in_specs=[a_spec, b_spec] + ([] if extra is None else [out_spec]),
        out_specs=out_spec,
        out_shape=jax.ShapeDtypeStruct((m, n), out_dtype),
        scratch_shapes=[] if nk == 1 else [pltpu.VMEM((tm, tn), F32)],
        compiler_params=_PARAMS(dimension_semantics=("parallel", "parallel", "arbitrary")),
    )(*((a, b) if extra is None else (a, b, extra)))


def _gates_mm(name, x, gate_w, mode, dz=None, tm=1024):
    _, nh, blk, _ = gate_w.shape
    d = nh * blk
    t = x.shape[0]
    tm = _tile(t, tm)
    cols = lambda g, h: slice(g * d + h * blk, g * d + (h + 1) * blk)

    def fwd(x_ref, w_ref, o_ref):
        for h in range(nh):
            xh = x_ref[:, cols(0, h)].astype(BF16)
            for g in range(2):
                o_ref[:, cols(g, h)] = jnp.dot(xh, w_ref[g, h].astype(BF16), preferred_element_type=F32)

    def dx(dz_ref, w_ref, o_ref):
        for h in range(nh):
            o_ref[:, cols(0, h)] = sum(lax.dot_general(dz_ref[:, cols(g, h)].astype(BF16), w_ref[g, h].astype(BF16), _NT,
                                                       preferred_element_type=F32) for g in range(2))

    def dw(x_ref, dz_ref, o_ref):
        @pl.when(pl.program_id(0) == 0)
        def _():
            o_ref[...] = jnp.zeros_like(o_ref)

        for h in range(nh):
            xh = x_ref[:, cols(0, h)].astype(BF16)
            for g in range(2):
                o_ref[g, h] += lax.dot_general(xh, dz_ref[:, cols(g, h)].astype(BF16), _TN, preferred_element_type=F32)

    row = lambda width: pl.BlockSpec((tm, width), lambda i: (i, 0))
    whole = pl.BlockSpec(gate_w.shape, lambda i: (0, 0, 0, 0))
    kern, args, in_specs, out_spec, out_shape, sem = {
        "fwd": (fwd, (x, gate_w), [row(d), whole], row(2 * d), jax.ShapeDtypeStruct((t, 2 * d), F32), "parallel"),
        "dx": (dx, (x, gate_w), [row(2 * d), whole], row(d), jax.ShapeDtypeStruct((t, d), F32), "parallel"),
        "dw": (dw, (x, dz), [row(d), row(2 * d)], whole, jax.ShapeDtypeStruct(gate_w.shape, F32), "arbitrary"),
    }[mode]
    return pl.pallas_call(kern, name=name, grid=(t // tm,), in_specs=in_specs, out_specs=out_spec, out_shape=out_shape,
                          compiler_params=_PARAMS(dimension_semantics=(sem,)))(*args)


def _scan(name, a, b, seq, reverse=False, tm=256):
    t, d = a.shape
    tm = _tile(seq, tm)
    nblk = t // tm
    ntile = tm // SUBLANES

    def kern(a_ref, b_ref, h_ref, carry_h, carry_a):
        i = pl.program_id(0)
        blk = (nblk - 1 - i) if reverse else i
        edge = (((blk + 1) * tm) % seq == 0) if reverse else ((blk * tm) % seq == 0)

        @pl.when(edge)
        def _():
            carry_h[...] = jnp.zeros_like(carry_h)
            carry_a[...] = jnp.zeros_like(carry_a)

        def tile_step(j, c):
            jj = (ntile - 1 - j) if reverse else j
            rows = pl.ds(pl.multiple_of(jj * SUBLANES, SUBLANES), SUBLANES)
            a8 = a_ref[rows, :]
            b8 = b_ref[rows, :]
            h, an = c
            out = [None] * SUBLANES
            order = range(SUBLANES - 1, -1, -1) if reverse else range(SUBLANES)
            for r in order:
                if reverse:
                    h = b8[r:r + 1, :] + an * h
                    an = a8[r:r + 1, :]
                else:
                    h = a8[r:r + 1, :] * h + b8[r:r + 1, :]
                out[r] = h
            h_ref[rows, :] = jnp.concatenate(out, axis=0)
            return (h, an)

        h, an = lax.fori_loop(0, ntile, tile_step, (carry_h[...], carry_a[...]))
        carry_h[...] = h
        carry_a[...] = an

    idx = (lambda i: (nblk - 1 - i, 0)) if reverse else (lambda i: (i, 0))
    return pl.pallas_call(
        kern, name=name, grid=(nblk,),
        in_specs=[pl.BlockSpec((tm, d), idx), pl.BlockSpec((tm, d), idx)],
        out_specs=pl.BlockSpec((tm, d), idx),
        out_shape=jax.ShapeDtypeStruct((t, d), F32),
        scratch_shapes=[pltpu.VMEM((1, d), F32), pltpu.VMEM((1, d), F32)],
        compiler_params=_PARAMS(dimension_semantics=("arbitrary",)),
    )(a, b)


_NN = (((1,), (0,)), ((), ()))
_NT = (((1,), (1,)), ((), ()))
_TN = (((0,), (0,)), ((), ()))


def _dot1(a, b, dims):
    return lax.dot_general(a.astype(BF16), b.astype(BF16), dims, preferred_element_type=F32)


def _dot3(a, b, dims):
    a_hi, b_hi = a.astype(BF16), b.astype(BF16)
    a_lo, b_lo = (a - a_hi.astype(F32)).astype(BF16), (b - b_hi.astype(F32)).astype(BF16)
    dg = lambda p, q: lax.dot_general(p, q, dims, preferred_element_type=F32)
    return dg(a_hi, b_hi) + (dg(a_hi, b_lo) + dg(a_lo, b_hi))


def _make_bmm(dot):
    def make(dims, da_rule, db_rule):
        @jax.custom_vjp
        def f(a, b):
            return dot(a, b, dims)

        def fwd(a, b):
            return dot(a, b, dims), (a, b)

        def bwd(res, g):
            a, b = res
            return da_rule(a, b, g), db_rule(a, b, g)

        f.defvjp(fwd, bwd)
        return f

    return dict(nn=make(_NN, lambda a, b, g: dot(g, b, _NT), lambda a, b, g: dot(a, g, _TN)),
                nt=make(_NT, lambda a, b, g: dot(g, b, _NN), lambda a, b, g: dot(g, a, _TN)),
                tn=make(_TN, lambda a, b, g: dot(b, g, _NT), lambda a, b, g: dot(a, g, _NN)))


_BMM = {1: _make_bmm(_dot1), 3: _make_bmm(_dot3)}
_WKV_PASSES = dict(pair=1, read=1, inv=3, apply=1, write=1)


def _running_sum(x, reverse):
    c = x.shape[0]
    row = lax.broadcasted_iota(jnp.int32, x.shape, 0)
    k = 1
    while k < c:
        if reverse:
            x = x + jnp.where(row < c - k, pltpu.roll(x, c - k, 0), 0.0)
        else:
            x = x + jnp.where(row >= k, pltpu.roll(x, k, 0), 0.0)
        k *= 2
    return x


@jax.custom_vjp
def _cumsum_rows(x):
    return _running_sum(x, False)


_cumsum_rows.defvjp(lambda x: (_running_sum(x, False), None), lambda _, g: (_running_sum(g, True),))


@jax.custom_vjp
def _unit_lower_inverse(nl):
    c = nl[0].shape[0]
    mm = _BMM[_WKV_PASSES["inv"]]["nn"]
    eye = jnp.where(lax.broadcasted_iota(jnp.int32, (c, c), 0) == lax.broadcasted_iota(jnp.int32, (c, c), 1), 1.0, 0.0)
    inv = [eye + z for z in nl]
    p = nl
    for _ in range(c.bit_length() - 2):
        p = [mm(z, z) for z in p]
        inv = [i_ + mm(p_, i_) for i_, p_ in zip(inv, p)]
    return inv


def _unit_lower_inverse_fwd(nl):
    inv = _unit_lower_inverse(nl)
    return inv, inv


def _unit_lower_inverse_bwd(inv, g):
    mm = _BMM[_WKV_PASSES["inv"]]
    left = [mm["tn"](x, g_) for x, g_ in zip(inv, g)]
    return ([mm["nt"](l_, x) for l_, x in zip(left, inv)],)


_unit_lower_inverse.defvjp(_unit_lower_inverse_fwd, _unit_lower_inverse_bwd)


@jax.custom_vjp
def _kept_inverse(nl, inv):
    return inv


_kept_inverse.defvjp(lambda nl, inv: (inv, inv),
                     lambda inv, g: (_unit_lower_inverse_bwd(inv, g)[0], [jnp.zeros_like(x) for x in inv]))


def _wkv_chunk(r, lw, k, v, a, b, s0, kept_inv=None):
    c = r[0].shape[0]
    ti = lax.broadcasted_iota(jnp.int32, (c, 2 * c), 0)
    tj = lax.broadcasted_iota(jnp.int32, (c, 2 * c), 1)
    right = tj >= c
    tau = jnp.where(right, tj - c, tj)
    strict_left = jnp.logical_and(jnp.logical_not(right), tau < ti)[:, :c]
    strict_right = jnp.logical_and(right, tau < ti)
    incl = tau <= ti
    last = lax.broadcasted_iota(jnp.int32, r[0].shape, 0) == c - 1
    each = lambda f, *ls: [f(*z) for z in zip(*ls)]
    rows2 = lambda x, y: jnp.concatenate([x, y], axis=0)
    pair, read, inv_, apply_, write = (_BMM[_WKV_PASSES[role]] for role in ("pair", "read", "inv", "apply", "write"))
    cum = each(_cumsum_rows, lw)
    w_incl = each(jnp.exp, cum)
    w_inv = each(lambda z: jnp.exp(-z), cum)
    at = each(lambda a_, c_, l_: a_ * jnp.exp(c_ - l_), a, cum, lw)
    ar = each(rows2, at, each(jnp.multiply, r, w_incl))
    bk = each(rows2, each(jnp.multiply, b, w_inv), each(jnp.multiply, k, w_inv))
    pp = each(pair["nt"], ar, bk)
    sr = each(read["nt"], ar, s0)
    nl = [jnp.where(strict_left, z[:c, :c], 0.0) for z in pp]
    zero_v = each(lambda v_: rows2(jnp.zeros_like(v_), v_), v)
    rhs = each(lambda s, z, zv: s[:c] + apply_["nn"](jnp.where(strict_right, z[:c], 0.0), zv), sr, pp, zero_v)
    inv = _unit_lower_inverse(nl) if kept_inv is None else _kept_inverse(nl, kept_inv)
    ut = each(inv_["nn"], inv, rhs)
    uv = each(rows2, ut, v)
    y = each(lambda s, z, uv_: s[c:] + apply_["nn"](jnp.where(incl, z[c:], 0.0), uv_), sr, pp, uv)
    w_end = each(lambda z: jnp.exp(jnp.sum(jnp.where(last, z, 0.0), axis=0, keepdims=True)), cum)
    s1 = each(lambda s, uv_, bk_, w_: (s + write["tn"](uv_, bk_)) * w_, s0, uv, bk, w_end)
    return y, s1, inv


def _wkv_fwd(r, lw, k, v, a, b, seq, hb=16):
    t, d = r.shape
    n = RWKV_HEAD
    nh = d // n
    hb = min(hb, nh)
    chunk = min(WKV_CHUNK, seq)
    ncs = seq // chunk

    def kern(r_ref, lw_ref, k_ref, v_ref, a_ref, b_ref, y_ref, st_ref, inv_ref, s_scr):
        @pl.when(pl.program_id(2) == 0)
        def _():
            s_scr[...] = jnp.zeros_like(s_scr)

        heads = lambda ref: [ref[:, h * n:(h + 1) * n] for h in range(hb)]
        s0 = [s_scr[h] for h in range(hb)]
        y, s1, inv = _wkv_chunk(heads(r_ref), heads(lw_ref), heads(k_ref), heads(v_ref), heads(a_ref), heads(b_ref), s0)
        for h in range(hb):
            st_ref[0, h] = s0[h]
            inv_ref[0, h] = inv[h]
            y_ref[:, h * n:(h + 1) * n] = y[h]
            s_scr[h] = s1[h]

    vec = pl.BlockSpec((chunk, hb * n), lambda bb, g, c: (bb * ncs + c, g))
    per_chunk = lambda rows: pl.BlockSpec((1, hb, rows, rows), lambda bb, g, c: (bb * ncs + c, g, 0, 0))
    return pl.pallas_call(
        kern, name="wkv_fwd", grid=(t // seq, nh // hb, ncs), in_specs=[vec] * 6,
        out_specs=[vec, per_chunk(n), per_chunk(chunk)],
        out_shape=[jax.ShapeDtypeStruct((t, d), F32), jax.ShapeDtypeStruct((t // chunk, nh, n, n), F32),
                   jax.ShapeDtypeStruct((t // chunk, nh, chunk, chunk), F32)],
        scratch_shapes=[pltpu.VMEM((hb, n, n), F32)],
        compiler_params=_PARAMS(dimension_semantics=("parallel", "parallel", "arbitrary")),
    )(r, lw, k, v, a, b)


def _wkv_bwd(r, lw, k, v, a, b, st, inv, dy, seq, hb=16):
    t, d = r.shape
    n = RWKV_HEAD
    nh = d // n
    hb = min(hb, nh)
    chunk = min(WKV_CHUNK, seq)
    ncs = seq // chunk

    def kern(r_ref, lw_ref, k_ref, v_ref, a_ref, b_ref, st_ref, inv_ref, dy_ref,
             dr_ref, dlw_ref, dk_ref, dv_ref, da_ref, db_ref, ds_scr):
        @pl.when(pl.program_id(2) == 0)
        def _():
            ds_scr[...] = jnp.zeros_like(ds_scr)

        heads = lambda ref: [ref[:, h * n:(h + 1) * n] for h in range(hb)]
        kept = [inv_ref[0, h] for h in range(hb)]
        _, vjp = jax.vjp(lambda *args: _wkv_chunk(*args, kept_inv=kept)[:2],
                         heads(r_ref), heads(lw_ref), heads(k_ref), heads(v_ref), heads(a_ref), heads(b_ref),
                         [st_ref[0, h] for h in range(hb)])
        grads = vjp((heads(dy_ref), [ds_scr[h] for h in range(hb)]))
        for h in range(hb):
            for ref, g in zip((dr_ref, dlw_ref, dk_ref, dv_ref, da_ref, db_ref), grads[:6]):
                ref[:, h * n:(h + 1) * n] = g[h]
            ds_scr[h] = grads[6][h]

    vec = pl.BlockSpec((chunk, hb * n), lambda bb, g, c: (bb * ncs + ncs - 1 - c, g))
    per_chunk = lambda rows: pl.BlockSpec((1, hb, rows, rows), lambda bb, g, c: (bb * ncs + ncs - 1 - c, g, 0, 0))
    return pl.pallas_call(
        kern, name="wkv_bwd", grid=(t // seq, nh // hb, ncs),
        in_specs=[vec] * 6 + [per_chunk(n), per_chunk(chunk), vec],
        out_specs=[vec] * 6, out_shape=[jax.ShapeDtypeStruct((t, d), F32)] * 6,
        scratch_shapes=[pltpu.VMEM((hb, n, n), F32)],
        compiler_params=_PARAMS(dimension_semantics=("parallel", "parallel", "arbitrary")),
    )(r, lw, k, v, a, b, st, inv, dy)


def _softmax_rows(s):
    e = jnp.exp(s - jnp.max(s, axis=-1, keepdims=True))
    return e / jnp.sum(e, axis=-1, keepdims=True)


def _attn_fwd(q, kv, seq, mem_len, tq=256):
    t, d = q.shape
    dh = d // MEM_HEADS
    scale = dh ** -0.5
    tq = _tile(seq, tq)
    nq = seq // tq

    def kern(q_ref, kv_ref, o_ref):
        for h in range(MEM_HEADS):
            cols = slice(h * dh, (h + 1) * dh)
            vcols = slice(d + h * dh, d + (h + 1) * dh)
            s = lax.dot_general(q_ref[:, cols], kv_ref[:, cols], _NT, preferred_element_type=F32) * scale
            p = _softmax_rows(s)
            o_ref[:, cols] = jnp.dot(p.astype(BF16), kv_ref[:, vcols], preferred_element_type=F32).astype(o_ref.dtype)

    return pl.pallas_call(
        kern, name="attn_fwd", grid=(t // seq, nq),
        in_specs=[pl.BlockSpec((tq, d), lambda b, i: (b * nq + i, 0)), pl.BlockSpec((mem_len, 2 * d), lambda b, i: (b, 0))],
        out_specs=pl.BlockSpec((tq, d), lambda b, i: (b * nq + i, 0)),
        out_shape=jax.ShapeDtypeStruct((t, d), BF16),
        compiler_params=_PARAMS(dimension_semantics=("parallel", "parallel")),
    )(q, kv)


def _attn_bwd(q, kv, do, seq, mem_len, tq=256):
    t, d = q.shape
    dh = d // MEM_HEADS
    scale = dh ** -0.5
    tq = _tile(seq, tq)
    nq = seq // tq

    def kern(q_ref, kv_ref, do_ref, dq_ref, dkv_ref):
        @pl.when(pl.program_id(1) == 0)
        def _():
            dkv_ref[...] = jnp.zeros_like(dkv_ref)

        for h in range(MEM_HEADS):
            cols = slice(h * dh, (h + 1) * dh)
            vcols = slice(d + h * dh, d + (h + 1) * dh)
            qh, kh, vh, doh = q_ref[:, cols], kv_ref[:, cols], kv_ref[:, vcols], do_ref[:, cols]
            p = _softmax_rows(lax.dot_general(qh, kh, _NT, preferred_element_type=F32) * scale)
            dp = lax.dot_general(doh, vh, _NT, preferred_element_type=F32)
            ds = (p * (dp - jnp.sum(p * dp, axis=-1, keepdims=True)) * scale).astype(BF16)
            dq_ref[:, cols] = jnp.dot(ds, kh, preferred_element_type=F32).astype(dq_ref.dtype)
            dkv_ref[:, cols] += lax.dot_general(ds, qh, _TN, preferred_element_type=F32)
            dkv_ref[:, vcols] += lax.dot_general(p.astype(BF16), doh, _TN, preferred_element_type=F32)

    return pl.pallas_call(
        kern, name="attn_bwd", grid=(t // seq, nq),
        in_specs=[pl.BlockSpec((tq, d), lambda b, i: (b * nq + i, 0)), pl.BlockSpec((mem_len, 2 * d), lambda b, i: (b, 0)),
                  pl.BlockSpec((tq, d), lambda b, i: (b * nq + i, 0))],
        out_specs=[pl.BlockSpec((tq, d), lambda b, i: (b * nq + i, 0)), pl.BlockSpec((mem_len, 2 * d), lambda b, i: (b, 0))],
        out_shape=[jax.ShapeDtypeStruct((t, d), BF16), jax.ShapeDtypeStruct(kv.shape, F32)],
        compiler_params=_PARAMS(dimension_semantics=("parallel", "arbitrary")),
    )(q, kv, do)


def _rstd(x):
    return lax.rsqrt(jnp.mean(x * x, axis=-1, keepdims=True) + RMS_EPS)


def _rms(x, g):
    return x * _rstd(x) * g


def _rms_bwd(dy, x, g):
    rstd = _rstd(x)
    xhat = x * rstd
    dxhat = dy * g
    return rstd * (dxhat - xhat * jnp.mean(dxhat * xhat, axis=-1, keepdims=True)), dy * xhat


def _softplus(x):
    return jnp.maximum(x, 0.0) + jnp.log1p(jnp.exp(-jnp.abs(x)))


def _one_minus_exp(x):
    series = -x * (1.0 + x * (0.5 + x * (1.0 / 6.0 + x * (1.0 / 24.0 + x * (1.0 / 120.0)))))
    return jnp.where(x > -0.05, series, 1.0 - jnp.exp(x))


_GELU_C = 0.7978845608028654
_GELU_K = 0.044715


def _gelu(x):
    return 0.5 * x * (1.0 + jnp.tanh(_GELU_C * (x + _GELU_K * x * x * x)))


def _gelu_grad(x):
    th = jnp.tanh(_GELU_C * (x + _GELU_K * x * x * x))
    return 0.5 * (1.0 + th) + 0.5 * x * (1.0 - th * th) * _GELU_C * (1.0 + 3.0 * _GELU_K * x * x)


def _seg_sum(x, seg):
    def two_terms(v, dims):
        hi = v.astype(BF16)
        lo = (v - hi.astype(F32)).astype(BF16)
        return (lax.dot_general(hi, seg, dims, preferred_element_type=F32)
                + lax.dot_general(lo, seg, dims, preferred_element_type=F32))
    return two_terms(two_terms(x, _NN), _NT)


def _f32(v):
    return v.astype(F32)


def _norm_fwd(name, x, g, dtype=BF16):
    (hn,), _ = _rowwise(name, lambda rv, cv, pv, nv: ([_rms(_f32(rv[0]), cv[0])], []), [x], [g], [(x.shape[1], dtype)])
    return hn


def _resid_norm_fwd(name, x, t, g, bias=None):
    def body(rv, cv, pv, nv):
        tt = rv[1] if bias is None else rv[1] + cv[1]
        return [rv[0] + _rms(tt, cv[0])], []
    (y,), _ = _rowwise(name, body, [x, t], [g] if bias is None else [g, bias], [(x.shape[1], F32)])
    return y


def _resid_norm_bwd(name, dxn, t, g, bias=None):
    def body(rv, cv, pv, nv):
        tt = rv[1] if bias is None else rv[1] + cv[1]
        dt, dg = _rms_bwd(rv[0], tt, cv[0])
        return [dt], [dg, dt]
    d = t.shape[1]
    (dt,), (dg, db) = _rowwise(name, body, [dxn, t], [g] if bias is None else [g, bias], [(d, BF16)], [d, d])
    return dt, dg.sum(0), db.sum(0)


def _prenorm_bwd(name, dxn, dhn, x, g):
    def body(rv, cv, pv, nv):
        dx, dg = _rms_bwd(_f32(rv[1]), rv[2], cv[0])
        return [rv[0] + dx], [dg]
    d = x.shape[1]
    (dx,), (dg,) = _rowwise(name, body, [dxn, dhn, x], [g], [(d, F32)], [d])
    return dx, dg.sum(0)


def _mlp_fwd(tag, x, g_pre, g_post, w_up, w_down):
    hn = _norm_fwd(tag + "_norm", x, g_pre)
    act = _mm(tag + "_up", hn, w_up, out_dtype=BF16, epilogue=lambda up: jnp.square(jnp.maximum(up, 0.0)))
    m = _mm(tag + "_down", act, w_down)
    y = _resid_norm_fwd(tag + "_res", x, m, g_post)
    return y, (x, hn, act, m)


def _mlp_bwd(tag, saved, dy, g_pre, g_post, w_up_t, w_down_t):
    x, hn, act, m = saved
    dm, dg_post, _ = _resid_norm_bwd(tag + "_dres", dy, m, g_post)
    dup = _mm(tag + "_dup", dm, w_down_t, out_dtype=BF16, extra=act,
              epilogue=lambda dact, act_: dact * 2.0 * jnp.sqrt(_f32(act_)))
    dw_down = _mm(tag + "_dwdown", act, dm, trans_a=True)
    dw_up = _mm(tag + "_dwup", hn, dup, trans_a=True)
    dhn = _mm(tag + "_dhn", dup, w_up_t)
    dx, dg_pre = _prenorm_bwd(tag + "_dnorm", dy, dhn, x, g_pre)
    return dx, dict(g_pre=dg_pre, g_post=dg_post, w_up=dw_up, w_down=dw_down)


def _xattn_fwd(tag, x, mem_n, g_pre, g_post, w_q, w_kv, w_o, seq, mem_len):
    hn = _norm_fwd(tag + "_norm", x, g_pre)
    q = _mm(tag + "_q", hn, w_q, out_dtype=BF16)
    kv = _mm(tag + "_kv", mem_n, w_kv, out_dtype=BF16)
    o = _attn_fwd(q, kv, seq, mem_len)
    c = _mm(tag + "_o", o, w_o)
    y = _resid_norm_fwd(tag + "_res", x, c, g_post)
    return y, (x, hn, q, kv, o, c)


def _xattn_bwd(tag, saved, dy, mem_n, g_pre, g_post, w_q_t, w_kv_t, w_o_t, seq, mem_len):
    x, hn, q, kv, o, c = saved
    dc, dg_post, _ = _resid_norm_bwd(tag + "_dres", dy, c, g_post)
    do = _mm(tag + "_do", dc, w_o_t, out_dtype=BF16)
    dw_o = _mm(tag + "_dwo", o, dc, trans_a=True)
    dq, dkv = _attn_bwd(q, kv, do, seq, mem_len)
    dw_q = _mm(tag + "_dwq", hn, dq, trans_a=True)
    dhn = _mm(tag + "_dhn", dq, w_q_t)
    dw_kv = _mm(tag + "_dwkv", mem_n, dkv, trans_a=True)
    dmem_n = _mm(tag + "_dmem", dkv, w_kv_t)
    dx, dg_pre = _prenorm_bwd(tag + "_dnorm", dy, dhn, x, g_pre)
    return dx, dmem_n, dict(g_pre=dg_pre, g_post=dg_post, w_q=dw_q, w_kv=dw_kv, w_o=dw_o)


def _lru_gates(z0, z1, gb0, gb1, sp):
    r = jax.nn.sigmoid(z0 + gb0)
    i = jax.nn.sigmoid(z1 + gb1)
    log_a = -LRU_C * r * sp
    a = jnp.exp(log_a)
    mult = jnp.sqrt(_one_minus_exp(2.0 * log_a))
    return r, i, a, mult


def _rglru_fwd(x, p, seq):
    d = x.shape[1]
    hn = _norm_fwd("a_norm", x, p["g_pre"])
    proj = _mm("a_in", hn, p["w_in"])

    def conv_body(rv, cv, pv, nv):
        u = rv[0] + cv[0]
        halo = pv[0] + cv[0]
        i = pl.program_id(0)
        halo = jnp.where((i * rv[0].shape[0]) % seq == 0, jnp.zeros_like(halo), halo)
        conv = cv[2] + u * cv[1][CONV_WIDTH - 1:CONV_WIDTH]
        for tap in range(CONV_WIDTH - 1):
            conv = conv + _shift_down(u, halo, CONV_WIDTH - 1 - tap) * cv[1][tap:tap + 1]
        return [conv], []
    (conv,), _ = _rowwise("a_conv", conv_body, [(proj, d, 1)], [p["b_in_u"], p["conv_w"], p["conv_b"]], [(d, F32)],
                          prev=[0], seq=seq)
    z = _gates_mm("a_gate", conv, p["w_gate"], "fwd")

    def gate_body(rv, cv, pv, nv):
        r, i, a, mult = _lru_gates(rv[0], rv[1], cv[0], cv[1], _softplus(-cv[2]))
        return [a, mult * i * rv[2]], []
    (a, bb), _ = _rowwise("a_gates", gate_body, [(z, d, 0), (z, d, 1), conv], [p["gate_b0"], p["gate_b1"], p["lam"]],
                          [(d, F32), (d, F32)])
    h = _scan("a_scan", a, bb, seq)

    def hy_body(rv, cv, pv, nv):
        return [rv[0] * _gelu(rv[1] + cv[0])], []
    (hy,), _ = _rowwise("a_hy", hy_body, [h, (proj, d, 0)], [p["b_in_y"]], [(d, BF16)])
    out = _mm("a_out", hy, p["w_out"])
    y = _resid_norm_fwd("a_res", x, out, p["g_post"], bias=p["b_out"])
    return y, (x, hn, proj, conv, z, a, h, hy, out)


def _rglru_bwd(saved, dy, p, pt, seq):
    x, hn, proj, conv, z, a, h, hy, out = saved
    d = x.shape[1]
    dt, dg_post, db_out = _resid_norm_bwd("a_dres", dy, out, p["g_post"], bias=p["b_out"])
    dhy = _mm("a_dhy", dt, pt["w_out_t"])
    dw_out = _mm("a_dwout", hy, dt, trans_a=True)

    def dh_body(rv, cv, pv, nv):
        yb = rv[2] + cv[0]
        return [rv[0] * _gelu(yb), rv[0] * rv[1] * _gelu_grad(yb)], []
    (dh, dyb), _ = _rowwise("a_dh", dh_body, [dhy, h, (proj, d, 0)], [p["b_in_y"]], [(d, F32), (d, BF16)])
    g = _scan("a_rscan", a, dh, seq, reverse=True)

    def dgate_body(rv, cv, pv, nv):
        gg, hh, z0, z1, cnv = rv
        sp = _softplus(-cv[2])
        r, i, aa, mult = _lru_gates(z0, z1, cv[0], cv[1], sp)
        i_blk = pl.program_id(0)
        halo = jnp.where((i_blk * gg.shape[0]) % seq == 0, jnp.zeros_like(pv[0]), pv[0])
        da = gg * _shift_down(hh, halo, 1)
        dmult = gg * i * cnv
        di = gg * mult * cnv
        dconv = gg * mult * i
        dlog_a = da * aa - dmult * aa * aa / mult
        dz0 = dlog_a * (-LRU_C * sp) * r * (1.0 - r)
        dz1 = di * i * (1.0 - i)
        dsp = dlog_a * (-LRU_C * r)
        dlam = dsp * (-jax.nn.sigmoid(-cv[2]))
        return [jnp.concatenate([dz0, dz1], axis=1), dconv], [dz0, dz1, dlam]
    (dz, dconv1), (dgb0, dgb1, dlam) = _rowwise(
        "a_dgates", dgate_body, [g, h, (z, d, 0), (z, d, 1), conv], [p["gate_b0"], p["gate_b1"], p["lam"]],
        [(2 * d, BF16), (d, F32)], [d, d, d], prev=[1], seq=seq)
    dconv2 = _gates_mm("a_dconv", dz, p["w_gate"], "dx")
    dw_gate = _gates_mm("a_dwgate", conv, p["w_gate"], "dw", dz=dz)

    def dconv_body(rv, cv, pv, nv):
        dc1, dc2, pu, dyb_ = rv
        dc = dc1 + dc2
        dc_next = nv[0] + nv[1]
        u = pu + cv[0]
        i_blk = pl.program_id(0)
        halo = jnp.where((i_blk * u.shape[0]) % seq == 0, jnp.zeros_like(pv[0]), pv[0] + cv[0])
        du = dc * cv[1][CONV_WIDTH - 1:CONV_WIDTH]
        dws = []
        for tap in range(CONV_WIDTH - 1):
            k = CONV_WIDTH - 1 - tap
            du = du + _shift_up(dc, dc_next, k) * cv[1][tap:tap + 1]
            dws.append(dc * _shift_down(u, halo, k))
        dws.append(dc * u)
        return [jnp.concatenate([_f32(dyb_), du], axis=1)], dws + [dc, _f32(dyb_), du]
    (dproj,), accs = _rowwise(
        "a_dconvw", dconv_body, [dconv1, dconv2, (proj, d, 1), dyb], [p["b_in_u"], p["conv_w"]],
        [(2 * d, BF16)], [d] * (CONV_WIDTH + 3), prev=[2], nxt=[0, 1], seq=seq)
    dconv_w = jnp.stack([acc.sum(0) for acc in accs[:CONV_WIDTH]])
    dconv_b = accs[CONV_WIDTH].sum(0)
    db_in = jnp.concatenate([accs[CONV_WIDTH + 1].sum(0), accs[CONV_WIDTH + 2].sum(0)])
    dhn = _mm("a_dhn", dproj, pt["w_in_t"])
    dw_in = _mm("a_dwin", hn, dproj, trans_a=True)
    dx, dg_pre = _prenorm_bwd("a_dnorm", dy, dhn, x, p["g_pre"])
    grads = dict(g_pre=dg_pre, g_post=dg_post, b_out=db_out, w_out=dw_out, gate_b0=dgb0.sum(0), gate_b1=dgb1.sum(0),
                 lam=dlam.sum(0), w_gate=dw_gate, conv_w=dconv_w, conv_b=dconv_b, b_in=db_in, w_in=dw_in)
    return dx, grads


def _rwkv_prep(k, wl, za, w0, a0, k_k, k_a, seg):
    w_in = wl + w0
    e_w = jnp.exp(-_softplus(-w_in) - 0.5)
    a = jax.nn.sigmoid(za + a0)
    q = k * k_k
    norm = jnp.sqrt(_seg_sum(q * q, seg))
    n = jnp.maximum(norm, 1e-12)
    kk = q / n
    return w_in, e_w, a, norm, n, kk


def _rwkv_out(y, r, k2, v, gn_g, gn_b, r_k, seg):
    inv = 1.0 / RWKV_HEAD
    yc = y - _seg_sum(y, seg) * inv
    rstd = lax.rsqrt(_seg_sum(yc * yc, seg) * inv + RWKV_GN_EPS)
    yhat = yc * rstd
    s = _seg_sum(r * k2 * r_k, seg)
    return rstd, yhat, s, yhat * gn_g + gn_b + s * v


def _rwkv_fwd(x, p, seq):
    t, d = x.shape
    nseq = t // seq

    def mix_body(rv, cv, pv, nv):
        hn = _rms(rv[0], cv[0])
        xx = _shift_down(hn, _rms(pv[0], cv[0]), 1) - hn
        return [hn] + [hn + xx * cv[1][c:c + 1] for c in range(6)], []
    (hn, xr, xw, xk, xv, xa, xg), _ = _rowwise("b_mix", mix_body, [x], [p["g_pre"], p["mu"]],
                                               [(d, F32)] + [(d, BF16)] * 6, prev=[0], seq=seq)
    r = _mm("b_r", xr, p["w_r"])
    k = _mm("b_k", xk, p["w_k"])
    v = _mm("b_v", xv, p["w_v"])
    lw = _mm("b_w1", xw, p["w1"])
    la = _mm("b_a1", xa, p["a1"], out_dtype=BF16)
    lg = _mm("b_g1", xg, p["g1"])
    (th,), _ = _rowwise("b_tanh", lambda rv, cv, pv, nv: ([jnp.tanh(rv[0])], []), [lw], [], [(lw.shape[1], BF16)])
    (sg,), _ = _rowwise("b_sig", lambda rv, cv, pv, nv: ([jax.nn.sigmoid(rv[0])], []), [lg], [], [(lg.shape[1], BF16)])
    wl = _mm("b_w2", th, p["w2"])
    za = _mm("b_a2", la, p["a2"])
    g = _mm("b_g2", sg, p["g2"])

    def prep_body(rv, cv, pv, nv):
        kk_, wl_, za_ = rv
        _, e_w, a, _, _, kk = _rwkv_prep(kk_, wl_, za_, cv[0], cv[1], cv[2], cv[3], cv[4])
        return [-e_w, kk_ * (1.0 + (a - 1.0) * cv[3]), -kk, kk * a], []
    (log_w, k2, rem_a, rem_b), _ = _rowwise("b_prep", prep_body, [k, wl, za],
                                            [p["w0"], p["a0"], p["k_k"], p["k_a"], p["seg"]], [(d, F32)] * 4, tm=256)
    rec_in = (r, log_w, k2, v, rem_a, rem_b)
    y, *states = _wkv_fwd(*rec_in, seq)

    def out_body(rv, cv, pv, nv):
        y_, r_, k2_, v_, g_ = rv
        _, _, _, out = _rwkv_out(y_, r_, k2_, v_, cv[0], cv[1], cv[2], cv[3])
        return [out * g_], []
    (og,), _ = _rowwise("b_out", out_body, [y, r, k2, v, g], [p["gn_g"], p["gn_b"], p["r_k"], p["seg"]], [(d, BF16)], tm=256)
    o = _mm("b_o", og, p["w_o"])
    res = _resid_norm_fwd("b_res", x, o, p["g_post"])
    return res, (x, hn, xr, xw, xk, xv, xa, xg, r, k, v, th, la, sg, wl, za, g, k2, rec_in, states, y, og, o)


def _rwkv_bwd(saved, dres, p, pt, seq):
    x, hn, xr, xw, xk, xv, xa, xg, r, k, v, th, la, sg, wl, za, g, k2, rec_in, states, y, og, o = saved
    t, d = x.shape
    nseq = t // seq
    do, dg_post, _ = _resid_norm_bwd("b_dres", dres, o, p["g_post"])
    dog = _mm("b_dog", do, pt["w_o_t"])
    dw_o = _mm("b_dwo", og, do, trans_a=True)

    def dout_body(rv, cv, pv, nv):
        dog_, y_, r_, k2_, v_, g_ = rv
        gn_g, gn_b, r_k, bd = cv
        inv = 1.0 / RWKV_HEAD
        rstd, yhat, s, out = _rwkv_out(y_, r_, k2_, v_, gn_g, gn_b, r_k, bd)
        dout = dog_ * g_
        ds = _seg_sum(dout * v_, bd)
        dyhat = dout * gn_g
        dy = rstd * (dyhat - _seg_sum(dyhat, bd) * inv - yhat * _seg_sum(dyhat * yhat, bd) * inv)
        return [dy, dog_ * out, dout * s, ds * k2_ * r_k, ds * r_ * r_k], [ds * r_ * k2_, dout * yhat, dout]
    (dy, dgate, dv_b, dr_b, dk2_b), (dr_k, dgn_g, dgn_b) = _rowwise(
        "b_dout", dout_body, [dog, y, r, k2, v, g], [p["gn_g"], p["gn_b"], p["r_k"], p["seg"]],
        [(d, F32), (d, BF16), (d, F32), (d, F32), (d, F32)], [d, d, d], tm=256)
    dr_rec, dlw_rec, dk2_rec, dv_rec, da_rec, db_rec = _wkv_bwd(*rec_in, *states, dy, seq)

    def dprep_body(rv, cv, pv, nv):
        dr_rec_, dlw_rec_, dk2_rec_, dv_rec_, da_rec_, db_rec_, dr_b_, dk2_b_, dv_b_, k_, wl_, za_ = rv
        w0, a0, k_k, k_a, bd = cv
        w_in, e_w, a, norm, n, kk = _rwkv_prep(k_, wl_, za_, w0, a0, k_k, k_a, bd)
        dk2 = dk2_rec_ + dk2_b_
        dkk = db_rec_ * a - da_rec_
        da = db_rec_ * kk + dk2 * k_ * k_a
        dq = jnp.where(norm > 1e-12, dkk - kk * _seg_sum(kk * dkk, bd), dkk) / n
        dk = dk2 * (1.0 + (a - 1.0) * k_a) + dq * k_k
        dza = da * a * (1.0 - a)
        dwl = dlw_rec_ * (-e_w) * jax.nn.sigmoid(-w_in)
        return [dr_rec_ + dr_b_, dk, dv_rec_ + dv_b_, dza, dwl], [dk2 * k_ * (a - 1.0), dq * k_, dza, dwl]
    (dr, dk, dv, dza, dwl), (dk_a, dk_k, da0, dw0) = _rowwise(
        "b_dprep", dprep_body, [dr_rec, dlw_rec, dk2_rec, dv_rec, da_rec, db_rec, dr_b, dk2_b, dv_b, k, wl, za],
        [p["w0"], p["a0"], p["k_k"], p["k_a"], p["seg"]], [(d, BF16)] * 5, [d] * 4, tm=256)

    dw_r = _mm("b_dwr", xr, dr, trans_a=True)
    dw_k = _mm("b_dwk", xk, dk, trans_a=True)
    dw_v = _mm("b_dwv", xv, dv, trans_a=True)
    dxr = _mm("b_dxr", dr, pt["w_r_t"])
    dxk = _mm("b_dxk", dk, pt["w_k_t"])
    dxv = _mm("b_dxv", dv, pt["w_v_t"])
    da2 = _mm("b_da2", la, dza, trans_a=True)
    dla = _mm("b_dla", dza, pt["a2_t"], out_dtype=BF16)
    da1 = _mm("b_da1", xa, dla, trans_a=True)
    dxa = _mm("b_dxa", dla, pt["a1_t"])
    dw2 = _mm("b_dw2", th, dwl, trans_a=True)
    dth = _mm("b_dth", dwl, pt["w2_t"])
    (dzw,), _ = _rowwise("b_dtanh", lambda rv, cv, pv, nv: ([rv[0] * (1.0 - _f32(rv[1]) * _f32(rv[1]))], []),
                         [dth, th], [], [(th.shape[1], BF16)])
    dw1 = _mm("b_dw1", xw, dzw, trans_a=True)
    dxw = _mm("b_dxw", dzw, pt["w1_t"])
    dg2 = _mm("b_dg2", sg, dgate, trans_a=True)
    dsg = _mm("b_dsg", dgate, pt["g2_t"])
    (dzg,), _ = _rowwise("b_dsig", lambda rv, cv, pv, nv: ([rv[0] * _f32(rv[1]) * (1.0 - _f32(rv[1]))], []),
                         [dsg, sg], [], [(sg.shape[1], BF16)])
    dg1 = _mm("b_dg1", xg, dzg, trans_a=True)
    dxg = _mm("b_dxg", dzg, pt["g1_t"])

    def dmix_body(rv, cv, pv, nv):
        hn_ = rv[0]
        dxs = rv[1:]
        mu = cv[0]
        xx = _shift_down(hn_, pv[0], 1) - hn_
        dsum = dxs[0]
        dxx = dxs[0] * mu[0:1]
        dxx_next = nv[0] * mu[0:1]
        for c in range(1, 6):
            dsum = dsum + dxs[c]
            dxx = dxx + dxs[c] * mu[c:c + 1]
            dxx_next = dxx_next + nv[c] * mu[c:c + 1]
        return [dsum - dxx + _shift_up(dxx, dxx_next, 1)], [dxs[c] * xx for c in range(6)]
    (dhn,), dmu = _rowwise("b_dmix", dmix_body, [hn, dxr, dxw, dxk, dxv, dxa, dxg], [p["mu"]], [(d, F32)], [d] * 6,
                           prev=[0], nxt=[1, 2, 3, 4, 5, 6], seq=seq, tm=256)
    dx, dg_pre = _prenorm_bwd("b_dnorm", dres, dhn, x, p["g_pre"])
    grads = dict(g_pre=dg_pre, g_post=dg_post, mu=jnp.stack([m.sum(0) for m in dmu]), w_r=dw_r, w_k=dw_k, w_v=dw_v,
                 w0=dw0.sum(0), w1=dw1, w2=dw2, a0=da0.sum(0), a1=da1, a2=da2, g1=dg1, g2=dg2, k_k=dk_k.sum(0),
                 k_a=dk_a.sum(0), r_k=dr_k.sum(0), gn_g=dgn_g.sum(0), gn_b=dgn_b.sum(0), w_o=dw_o)
    return dx, grads


_WEIGHTS = ['ln_gains', 'mem_norm', 'a_conv_w', 'a_conv_b', 'a_w_in', 'a_b_in', 'a_gate_w', 'a_gate_b', 'a_lambda', 'a_w_out',
            'a_b_out', 'b_mu', 'b_w_rkv', 'b_w0', 'b_w1', 'b_w2', 'b_a0', 'b_a1', 'b_a2', 'b_g1', 'b_g2', 'b_k_k', 'b_k_a',
            'b_r_k', 'b_gn_g', 'b_gn_b', 'b_w_o', 'c_w_q', 'c_w_kv', 'c_w_o', 'm_w_up', 'm_w_down']
_SHARD_AXIS = dict(ln_gains=2, mem_norm=None, a_conv_w=2, a_conv_b=None, a_w_in=2, a_b_in=None, a_gate_w=3, a_gate_b=3,
                   a_lambda=None, a_w_out=1, a_b_out=None, b_mu=2, b_w_rkv=2, b_w0=1, b_w1=1, b_w2=2, b_a0=1, b_a1=1, b_a2=2,
                   b_g1=1, b_g2=2, b_k_k=1, b_k_a=1, b_r_k=None, b_gn_g=1, b_gn_b=1, b_w_o=1, c_w_q=1, c_w_kv=2, c_w_o=1,
                   m_w_up=2, m_w_down=1)
_MATRICES = ['a_w_in', 'a_gate_w', 'a_w_out', 'b_w_rkv', 'b_w1', 'b_w2', 'b_a1', 'b_a2', 'b_g1', 'b_g2', 'b_w_o', 'c_w_q',
             'c_w_kv', 'c_w_o', 'm_w_up', 'm_w_down']
_SHARDED = [n for n in _WEIGHTS if _SHARD_AXIS[n] is not None]
_VECTORS = [n for n in _SHARDED if n not in _MATRICES]
_REPLICATED = [n for n in _WEIGHTS if _SHARD_AXIS[n] is None]
N_XY = 4
N_DEV = 8
PACK_W = 1024
PACK_ROWS = 256


def _pack(arrs, dtype, row_mult=PACK_ROWS):
    parts = []
    rows = 0
    for a in arrs:
        n = a.size
        r = -(-n // PACK_W)
        parts.append(jnp.pad(a.reshape(-1).astype(dtype), (0, r * PACK_W - n)))
        rows += r
    pad_rows = -(-rows // row_mult) * row_mult - rows
    if pad_rows:
        parts.append(jnp.zeros((pad_rows * PACK_W,), dtype))
    return jnp.concatenate(parts).reshape(-1, PACK_W)


def _unpack(flat, shapes):
    out = []
    row = 0
    for shp in shapes:
        n = 1
        for s in shp:
            n *= s
        r = -(-n // PACK_W)
        out.append(flat[row:row + r].reshape(-1)[:n].reshape(shp))
        row += r
    return out


_ANY = pl.BlockSpec(memory_space=pl.ANY)


def _xy_peers():
    x, y = lax.axis_index("x"), lax.axis_index("y")
    return [(1 - x, y), (x, 1 - y), (1 - x, 1 - y)]


def _all_gather_xy(wm, wv):
    half = wm.shape[0] // 2

    def body(wm_ref, wv_ref, gm_ref, gv_ref, send_sems, recv_sems, local_sems):
        x, y, c = lax.axis_index("x"), lax.axis_index("y"), lax.axis_index("c")
        me = 2 * x + y
        mine = pl.ds(pl.multiple_of(c * half, SUBLANES), half)
        other = pl.ds(pl.multiple_of((1 - c) * half, SUBLANES), half)
        local = [pltpu.make_async_copy(wm_ref, gm_ref.at[me], local_sems.at[0]),
                 pltpu.make_async_copy(wv_ref, gv_ref.at[me], local_sems.at[1])]
        for cp in local:
            cp.start()
        sends, lands, passes, from_sibling = [], [], [], []
        for j, (px, py) in enumerate(_xy_peers()):
            peer = 2 * px + py
            ici = functools.partial(pltpu.make_async_remote_copy, device_id=(px, py, c), device_id_type=MESH)
            sends.append(ici(src_ref=wm_ref.at[mine], dst_ref=gm_ref.at[me, mine], send_sem=send_sems.at[j], recv_sem=recv_sems.at[j]))
            lands.append(ici(src_ref=wm_ref.at[mine], dst_ref=gm_ref.at[peer, mine], send_sem=send_sems.at[j], recv_sem=recv_sems.at[j]))
            sends.append(ici(src_ref=wv_ref, dst_ref=gv_ref.at[me], send_sem=send_sems.at[3 + j], recv_sem=recv_sems.at[3 + j]))
            lands.append(ici(src_ref=wv_ref, dst_ref=gv_ref.at[peer], send_sem=send_sems.at[3 + j], recv_sem=recv_sems.at[3 + j]))
            d2d = functools.partial(pltpu.make_async_remote_copy, send_sem=send_sems.at[6 + j], recv_sem=recv_sems.at[6 + j],
                                    device_id=(x, y, 1 - c), device_id_type=MESH)
            passes.append(d2d(src_ref=gm_ref.at[peer, mine], dst_ref=gm_ref.at[peer, mine]))
            from_sibling.append(d2d(src_ref=gm_ref.at[peer, other], dst_ref=gm_ref.at[peer, other]))
        for cp in sends:
            cp.start()
        for j in range(N_XY - 1):
            lands[2 * j].wait_recv()
            passes[j].start()
        for j in range(N_XY - 1):
            lands[2 * j + 1].wait_recv()
        for cp in from_sibling:
            cp.wait_recv()
        for cp in sends + passes:
            cp.wait_send()
        for cp in local:
            cp.wait()

    return pl.pallas_call(
        body, name="all_gather_weights",
        in_specs=[_ANY, _ANY], out_specs=[_ANY, _ANY],
        out_shape=[jax.ShapeDtypeStruct((N_XY,) + wm.shape, wm.dtype), jax.ShapeDtypeStruct((N_XY,) + wv.shape, wv.dtype)],
        scratch_shapes=[pltpu.SemaphoreType.DMA((9,)), pltpu.SemaphoreType.DMA((9,)), pltpu.SemaphoreType.DMA((2,))],
    )(wm, wv)


_HBM = pl.BlockSpec(memory_space=pltpu.HBM)
_SEM = pl.BlockSpec(memory_space=pltpu.SEMAPHORE)
_SPLIT_COPY = functools.partial(pltpu.CompilerParams, has_side_effects=pltpu.SideEffectType.DATAFLOW_SIDE_EFFECTING)


def _gather_xy_start(name, buf, after):
    def body(src_ref, land_ref, after_ref, send_sems, recv_sems, src_thru, land_thru, token):
        x, y, c = lax.axis_index("x"), lax.axis_index("y"), lax.axis_index("c")
        for j, (px, py) in enumerate(_xy_peers()):
            pltpu.make_async_remote_copy(src_ref=src_ref, dst_ref=land_ref.at[2 * x + y], send_sem=send_sems.at[j],
                                         recv_sem=recv_sems.at[j], device_id=(px, py, c), device_id_type=MESH).start()
        token[...] = jnp.zeros_like(token)

    n_peers = N_XY - 1
    land = pltpu.with_memory_space_constraint(lax.empty((N_XY,) + buf.shape, buf.dtype), pltpu.HBM)
    return pl.pallas_call(
        body, name=name,
        out_shape=(pltpu.SemaphoreType.DMA((n_peers,)), pltpu.SemaphoreType.DMA((n_peers,)), pltpu.HBM(buf.shape, buf.dtype),
                   pltpu.HBM(land.shape, buf.dtype), jax.ShapeDtypeStruct((SUBLANES, LANES), F32)),
        in_specs=(_HBM, _HBM, _ANY), out_specs=(_SEM, _SEM, _HBM, _HBM, pl.BlockSpec(memory_space=pltpu.VMEM)),
        input_output_aliases={0: 2, 1: 3}, compiler_params=_SPLIT_COPY(),
    )(pltpu.with_memory_space_constraint(buf, pltpu.HBM), land, after)


def _gather_xy_wait(name, send_sems, recv_sems, src_thru, land_thru, after):
    def body(src_ref, land_ref, send_sems, recv_sems, after_ref, src_dead, got_ref):
        c = lax.axis_index("c")
        for j, (px, py) in enumerate(_xy_peers()):
            cp = pltpu.make_async_remote_copy(src_ref=src_ref, dst_ref=land_ref.at[2 * px + py], send_sem=send_sems.at[j],
                                              recv_sem=recv_sems.at[j], device_id=(px, py, c), device_id_type=MESH)
            cp.wait_send()
            cp.wait_recv()

    return pl.pallas_call(
        body, name=name,
        out_shape=(pltpu.HBM(src_thru.shape, src_thru.dtype), pltpu.HBM(land_thru.shape, land_thru.dtype)),
        in_specs=(_HBM, _HBM, _SEM, _SEM, _ANY), out_specs=(_HBM, _HBM),
        input_output_aliases={0: 0, 1: 1}, compiler_params=_SPLIT_COPY(),
    )(src_thru, land_thru, send_sems, recv_sems, after)[1]


def _exchange_xy_start(name, gsend, after):
    def body(src_ref, land_ref, after_ref, send_sems, recv_sems, src_thru, land_thru, token):
        c = lax.axis_index("c")
        for j, (px, py) in enumerate(_xy_peers()):
            pltpu.make_async_remote_copy(src_ref=src_ref.at[2 * px + py], dst_ref=land_ref.at[j], send_sem=send_sems.at[j],
                                         recv_sem=recv_sems.at[j], device_id=(px, py, c), device_id_type=MESH).start()
        token[...] = jnp.zeros_like(token)

    n_peers = N_XY - 1
    land = pltpu.with_memory_space_constraint(lax.empty((n_peers,) + gsend.shape[1:], gsend.dtype), pltpu.HBM)
    return pl.pallas_call(
        body, name=name,
        out_shape=(pltpu.SemaphoreType.DMA((n_peers,)), pltpu.SemaphoreType.DMA((n_peers,)), pltpu.HBM(gsend.shape, gsend.dtype),
                   pltpu.HBM(land.shape, gsend.dtype), jax.ShapeDtypeStruct((SUBLANES, LANES), F32)),
        in_specs=(_HBM, _HBM, _ANY), out_specs=(_SEM, _SEM, _HBM, _HBM, pl.BlockSpec(memory_space=pltpu.VMEM)),
        input_output_aliases={0: 2, 1: 3}, compiler_params=_SPLIT_COPY(),
    )(pltpu.with_memory_space_constraint(gsend, pltpu.HBM), land, after)


def _exchange_xy_wait(name, send_sems, recv_sems, src_thru, land_thru, after):
    def body(src_ref, land_ref, send_sems, recv_sems, after_ref, src_dead, got_ref):
        c = lax.axis_index("c")
        for j, (px, py) in enumerate(_xy_peers()):
            cp = pltpu.make_async_remote_copy(src_ref=src_ref.at[2 * px + py], dst_ref=land_ref.at[j], send_sem=send_sems.at[j],
                                              recv_sem=recv_sems.at[j], device_id=(px, py, c), device_id_type=MESH)
            cp.wait_send()
            cp.wait_recv()

    return pl.pallas_call(
        body, name=name,
        out_shape=(pltpu.HBM(src_thru.shape, src_thru.dtype), pltpu.HBM(land_thru.shape, land_thru.dtype)),
        in_specs=(_HBM, _HBM, _SEM, _SEM, _ANY), out_specs=(_HBM, _HBM),
        input_output_aliases={0: 0, 1: 1}, compiler_params=_SPLIT_COPY(),
    )(src_thru, land_thru, send_sems, recv_sems, after)[1]


def _exchange_xy(gsend):
    def body(gs_ref, recv_ref, send_sems, recv_sems):
        c = lax.axis_index("c")
        sends = []
        for j, (px, py) in enumerate(_xy_peers()):
            sends.append(pltpu.make_async_remote_copy(
                src_ref=gs_ref.at[2 * px + py], dst_ref=recv_ref.at[j], send_sem=send_sems.at[j], recv_sem=recv_sems.at[j],
                device_id=(px, py, c), device_id_type=MESH))
        for cp in sends:
            cp.start()
        for cp in sends:
            cp.wait_recv()
        for cp in sends:
            cp.wait_send()

    return pl.pallas_call(
        body, name="exchange_grads",
        in_specs=[_ANY], out_specs=_ANY,
        out_shape=jax.ShapeDtypeStruct((N_XY - 1,) + gsend.shape[1:], gsend.dtype),
        scratch_shapes=[pltpu.SemaphoreType.DMA((3,)), pltpu.SemaphoreType.DMA((3,))],
    )(gsend)


def _swap_with_sibling(name, part):
    def body(p_ref, got_ref, send_sem, recv_sem):
        x, y, c = lax.axis_index("x"), lax.axis_index("y"), lax.axis_index("c")
        cp = pltpu.make_async_remote_copy(src_ref=p_ref, dst_ref=got_ref, send_sem=send_sem, recv_sem=recv_sem,
                                          device_id=(x, y, 1 - c), device_id_type=MESH)
        cp.start()
        cp.wait_recv()
        cp.wait_send()

    return pl.pallas_call(
        body, name=name,
        in_specs=[_ANY], out_specs=_ANY, out_shape=jax.ShapeDtypeStruct(part.shape, part.dtype),
        scratch_shapes=[pltpu.SemaphoreType.DMA, pltpu.SemaphoreType.DMA],
    )(part)


def _all_gather_all(vec):
    def body(v_ref, out_ref, send_sems, recv_sems, local_sem):
        x, y, c = lax.axis_index("x"), lax.axis_index("y"), lax.axis_index("c")
        me = 4 * x + 2 * y + c
        local = pltpu.make_async_copy(v_ref, out_ref.at[me], local_sem)
        local.start()
        sends, recvs = [], []
        for f in range(1, N_DEV):
            fx, fy, fc = (f >> 2) & 1, (f >> 1) & 1, f & 1
            px = (1 - x) if fx else x
            py = (1 - y) if fy else y
            pc = (1 - c) if fc else c
            mk = functools.partial(pltpu.make_async_remote_copy, src_ref=v_ref, send_sem=send_sems.at[f - 1],
                                   recv_sem=recv_sems.at[f - 1], device_id=(px, py, pc), device_id_type=MESH)
            sends.append(mk(dst_ref=out_ref.at[me]))
            recvs.append(mk(dst_ref=out_ref.at[4 * px + 2 * py + pc]))
        for cp in sends:
            cp.start()
        for cp in recvs:
            cp.wait_recv()
        for cp in sends:
            cp.wait_send()
        local.wait()

    return pl.pallas_call(
        body, name="all_gather_replicated",
        in_specs=[_ANY], out_specs=_ANY, out_shape=jax.ShapeDtypeStruct((N_DEV,) + vec.shape, vec.dtype),
        scratch_shapes=[pltpu.SemaphoreType.DMA((N_DEV - 1,)), pltpu.SemaphoreType.DMA((N_DEV - 1,)), pltpu.SemaphoreType.DMA],
    )(vec)


def _adamw(g, w, m, v):
    m2 = ADAM_B1 * m + (1.0 - ADAM_B1) * g
    v2 = ADAM_B2 * v + (1.0 - ADAM_B2) * g * g
    m_hat = m2 / (1.0 - ADAM_B1 ** ADAM_STEP)
    v_hat = v2 / (1.0 - ADAM_B2 ** ADAM_STEP)
    return -ADAM_LR * (m_hat / (jnp.sqrt(v_hat) + ADAM_EPS) + ADAM_WD * w), m2, v2


def _sum_contributions(name, own, recv):
    def body(rv, cv, pv, nv):
        return [((rv[0] + _f32(rv[1])) + _f32(rv[2])) + _f32(rv[3])], []
    stacked = recv.reshape(-1, PACK_W)
    (part,), _ = _rowwise(name, body, [own] + [(stacked, PACK_W, 0, j * own.shape[0]) for j in range(N_XY - 1)], [],
                          [(PACK_W, F32)])
    return part


def _adamw_sharded(name, part, sib, first, w, m, v, w_first=0, n_rows=None):
    def body(rv, cv, pv, nv):
        g = rv[3] + rv[4]
        return [g, *_adamw(g, rv[0], rv[1], rv[2])], []
    n_rows = w.shape[0] if n_rows is None else n_rows
    rows = [(z, PACK_W, 0, w_first) for z in (w, m, v)] + [(part, PACK_W, 0, first), (sib, PACK_W, 0, first)]
    outs, _ = _rowwise(name, body, rows, [], [(PACK_W, F32)] * 4, tm=256, n_rows=n_rows)
    return outs


def _adamw_replicated(parts, w, m, v):
    def body(rv, cv, pv, nv):
        g = rv[0]
        for i in range(1, N_DEV):
            g = g + rv[i]
        return [g, *_adamw(g, rv[N_DEV], rv[N_DEV + 1], rv[N_DEV + 2])], []
    outs, _ = _rowwise("adamw_replicated", body, [parts[i] for i in range(N_DEV)] + [w, m, v], [], [(PACK_W, F32)] * 4)
    return outs


def _row(v):
    return v.reshape(1, -1).astype(F32)


def _local_step(x3, mem3, target3, fw, late_weights=None, on_grads=None):
    def with_late(stage, after):
        if late_weights is None:
            return fw
        got = late_weights(stage, after)
        return {**fw, **{n: ({**fw.get(n, {}), **v} if isinstance(v, dict) else v) for n, v in got.items()}}

    nseq, seq, d = x3.shape
    mem_len = mem3.shape[1]
    t = nseq * seq
    x0 = x3.reshape(t, d)
    mem2 = mem3.reshape(nseq * mem_len, d)
    target = target3.reshape(t, d)
    ln = fw["ln_gains"]
    gains = [[_row(ln[i, j]) for j in range(6)] for i in range(2)]
    nh = d // RWKV_HEAD
    seg = (jnp.arange(d)[:, None] // RWKV_HEAD == jnp.arange(LANES)[None, :]).astype(BF16)

    w_gate = fw["a_gate_w"][0]
    pa = dict(g_pre=gains[0][0], g_post=gains[0][1], w_in=fw["a_w_in"][0], b_in_y=_row(fw["a_b_in"][0, :d]),
              b_in_u=_row(fw["a_b_in"][0, d:]), conv_w=fw["a_conv_w"][0].astype(F32), conv_b=_row(fw["a_conv_b"][0]),
              w_gate=w_gate, gate_b0=_row(fw["a_gate_b"][0, 0]), gate_b1=_row(fw["a_gate_b"][0, 1]), lam=_row(fw["a_lambda"][0]),
              w_out=fw["a_w_out"][0], b_out=_row(fw["a_b_out"][0]))
    pta = dict(w_in_t=_t(pa["w_in"]), w_out_t=_t(pa["w_out"]))
    mem_g = _row(fw["mem_norm"])

    mem_n = _norm_fwd("mem_norm", mem2, mem_g)
    x1, sv_a = _rglru_fwd(x0, pa, seq)
    fw = with_late("cm0", x1)
    x2, sv_c0 = _xattn_fwd("c0", x1, mem_n, gains[0][2], gains[0][3], fw["c_w_q"][0], fw["c_w_kv"][0], fw["c_w_o"][0], seq, mem_len)
    x3_, sv_m0 = _mlp_fwd("m0", x2, gains[0][4], gains[0][5], fw["m_w_up"][0], fw["m_w_down"][0])
    fw = with_late("layer1", x3_)
    pb = dict(g_pre=gains[1][0], g_post=gains[1][1], mu=fw["b_mu"][0].astype(F32), w_r=fw["b_w_rkv"][0, 0],
              w_k=fw["b_w_rkv"][0, 1], w_v=fw["b_w_rkv"][0, 2], w0=_row(fw["b_w0"][0]), w1=fw["b_w1"][0], w2=fw["b_w2"][0],
              a0=_row(fw["b_a0"][0]), a1=fw["b_a1"][0], a2=fw["b_a2"][0], g1=fw["b_g1"][0], g2=fw["b_g2"][0],
              k_k=_row(fw["b_k_k"][0]), k_a=_row(fw["b_k_a"][0]), r_k=_row(fw["b_r_k"][0]), gn_g=_row(fw["b_gn_g"][0]),
              gn_b=_row(fw["b_gn_b"][0]), w_o=fw["b_w_o"][0], seg=seg)
    ptb = {k + "_t": _t(pb[k]) for k in ("w_r", "w_k", "w_v", "w_o", "w1", "w2", "a1", "a2", "g1", "g2")}
    x4, sv_b = _rwkv_fwd(x3_, pb, seq)
    x5, sv_c1 = _xattn_fwd("c1", x4, mem_n, gains[1][2], gains[1][3], fw["c_w_q"][1], fw["c_w_kv"][1], fw["c_w_o"][1], seq, mem_len)
    x6, sv_m1 = _mlp_fwd("m1", x5, gains[1][4], gains[1][5], fw["m_w_up"][1], fw["m_w_down"][1])

    def loss_body(rv, cv, pv, nv):
        err = rv[0] - rv[1]
        return [err * (1.0 / d)], [err * err]
    (dx,), (sq,) = _rowwise("loss", loss_body, [x6, target], [], [(d, F32)], [d])
    loss_part = 0.5 / d * jnp.sum(sq)

    dx, g_m1 = _mlp_bwd("m1", sv_m1, dx, gains[1][4], gains[1][5], _t(fw["m_w_up"][1]), _t(fw["m_w_down"][1]))
    dx, dmem1, g_c1 = _xattn_bwd("c1", sv_c1, dx, mem_n, gains[1][2], gains[1][3], _t(fw["c_w_q"][1]), _t(fw["c_w_kv"][1]),
                                 _t(fw["c_w_o"][1]), seq, mem_len)
    dx, g_b = _rwkv_bwd(sv_b, dx, pb, ptb, seq)
    grads_b = dict(
        b_mu=g_b["mu"][None], b_w_rkv=jnp.stack([g_b["w_r"], g_b["w_k"], g_b["w_v"]])[None], b_w0=g_b["w0"][None],
        b_w1=g_b["w1"][None], b_w2=g_b["w2"][None], b_a0=g_b["a0"][None], b_a1=g_b["a1"][None], b_a2=g_b["a2"][None],
        b_g1=g_b["g1"][None], b_g2=g_b["g2"][None], b_k_k=g_b["k_k"][None], b_k_a=g_b["k_a"][None],
        b_r_k=g_b["r_k"].reshape(1, nh, RWKV_HEAD), b_gn_g=g_b["gn_g"][None], b_gn_b=g_b["gn_b"][None], b_w_o=g_b["w_o"][None])
    g_m0_pre = gains[0][4]
    if on_grads is not None:
        layer1 = dict(c_w_q=g_c1["w_q"], c_w_kv=g_c1["w_kv"], c_w_o=g_c1["w_o"], m_w_up=g_m1["w_up"], m_w_down=g_m1["w_down"])
        g_m0_pre = g_m0_pre + on_grads("layer1", {**grads_b, **{n: [None, g] for n, g in layer1.items()}})[0, 0]
    dx, g_m0 = _mlp_bwd("m0", sv_m0, dx, g_m0_pre, gains[0][5], _t(fw["m_w_up"][0]), _t(fw["m_w_down"][0]))
    dx, dmem0, g_c0 = _xattn_bwd("c0", sv_c0, dx, mem_n, gains[0][2], gains[0][3], _t(fw["c_w_q"][0]), _t(fw["c_w_kv"][0]),
                                 _t(fw["c_w_o"][0]), seq, mem_len)
    if on_grads is not None:
        cm0 = dict(c_w_q=g_c0["w_q"], c_w_kv=g_c0["w_kv"], c_w_o=g_c0["w_o"], m_w_up=g_m0["w_up"], m_w_down=g_m0["w_down"])
        pa = {**pa, "g_post": pa["g_post"] + on_grads("cm0", {n: [g, None] for n, g in cm0.items()})[0, 0]}
    dx, g_a = _rglru_bwd(sv_a, dx, pa, pta, seq)

    def dmem_body(rv, cv, pv, nv):
        _, dg = _rms_bwd(rv[1] + rv[2], rv[0], cv[0])
        return [], [dg]
    _, (dmem_g,) = _rowwise("mem_norm_grad", dmem_body, [mem2, dmem0, dmem1], [mem_g], [], [d])

    lru_heads = fw["a_gate_w"].shape[2]
    blk = d // lru_heads
    grads = dict(
        ln_gains=jnp.stack([jnp.stack([g_a["g_pre"], g_a["g_post"], g_c0["g_pre"], g_c0["g_post"], g_m0["g_pre"], g_m0["g_post"]]),
                            jnp.stack([g_b["g_pre"], g_b["g_post"], g_c1["g_pre"], g_c1["g_post"], g_m1["g_pre"], g_m1["g_post"]])]),
        mem_norm=dmem_g.sum(0),
        a_conv_w=g_a["conv_w"][None], a_conv_b=g_a["conv_b"][None], a_w_in=g_a["w_in"][None], a_b_in=g_a["b_in"][None],
        a_gate_w=g_a["w_gate"][None],
        a_gate_b=jnp.stack([g_a["gate_b0"], g_a["gate_b1"]]).reshape(1, 2, lru_heads, blk),
        a_lambda=g_a["lam"][None], a_w_out=g_a["w_out"][None], a_b_out=g_a["b_out"][None],
        **grads_b,
        c_w_q=[g_c0["w_q"], g_c1["w_q"]], c_w_kv=[g_c0["w_kv"], g_c1["w_kv"]], c_w_o=[g_c0["w_o"], g_c1["w_o"]],
        m_w_up=[g_m0["w_up"], g_m1["w_up"]], m_w_down=[g_m0["w_down"], g_m1["w_down"]],
    )
    return loss_part, dx.reshape(nseq, seq, d), grads


def kernel(x, mem, ln_gains, mem_norm, a_conv_w, a_conv_b, a_w_in, a_b_in, a_gate_w, a_gate_b, a_lambda, a_w_out, a_b_out, b_mu, b_w_rkv, b_w0, b_w1, b_w2, b_a0, b_a1, b_a2, b_g1, b_g2, b_k_k, b_k_a, b_r_k, b_gn_g, b_gn_b, b_w_o, c_w_q, c_w_kv, c_w_o, m_w_up, m_w_down, loss_target, m_ln_gains, m_mem_norm, m_a_conv_w, m_a_conv_b, m_a_w_in, m_a_b_in, m_a_gate_w, m_a_gate_b, m_a_lambda, m_a_w_out, m_a_b_out, m_b_mu, m_b_w_rkv, m_b_w0, m_b_w1, m_b_w2, m_b_a0, m_b_a1, m_b_a2, m_b_g1, m_b_g2, m_b_k_k, m_b_k_a, m_b_r_k, m_b_gn_g, m_b_gn_b, m_b_w_o, m_c_w_q, m_c_w_kv, m_c_w_o, m_m_w_up, m_m_w_down, v_ln_gains, v_mem_norm, v_a_conv_w, v_a_conv_b, v_a_w_in, v_a_b_in, v_a_gate_w, v_a_gate_b, v_a_lambda, v_a_w_out, v_a_b_out, v_b_mu, v_b_w_rkv, v_b_w0, v_b_w1, v_b_w2, v_b_a0, v_b_a1, v_b_a2, v_b_g1, v_b_g2, v_b_k_k, v_b_k_a, v_b_r_k, v_b_gn_g, v_b_gn_b, v_b_w_o, v_c_w_q, v_c_w_kv, v_c_w_o, v_m_w_up, v_m_w_down):
    given = dict(locals())
    w = {n: given[n] for n in _WEIGHTS}
    mom1 = {n: given["m_" + n] for n in _WEIGHTS}
    mom2 = {n: given["v_" + n] for n in _WEIGHTS}

    me = 2 * lax.axis_index("x") + lax.axis_index("y")
    per_layer = [n for n in _MATRICES if n[0] in "cm"]
    early = [(n, None) for n in _MATRICES if n[0] == "a"]
    late = dict(cm0=[(n, 0) for n in per_layer], layer1=[(n, None) for n in _MATRICES if n[0] == "b"] + [(n, 1) for n in per_layer])
    piece = lambda n, layer: w[n] if layer is None else w[n][layer:layer + 1]

    def gathered(entries, buf):
        shards = [_unpack(buf[s], [piece(n, layer).shape for n, layer in entries]) for s in range(N_XY)]
        out = {}
        for i, (n, layer) in enumerate(entries):
            full = jnp.concatenate([shards[s][i] for s in range(N_XY)], axis=_SHARD_AXIS[n])
            if layer is None:
                out[n] = full
            else:
                out.setdefault(n, {})[layer] = full[0]
        return out

    late_bufs = {stage: _pack([piece(n, layer) for n, layer in entries], BF16) for stage, entries in late.items()}
    gm, gv = _all_gather_xy(_pack([piece(n, layer) for n, layer in early], BF16), _pack([w[n] for n in _VECTORS], F32, SUBLANES))
    in_flight = {}
    token = gv
    for stage in late:
        *in_flight[stage], token = _gather_xy_start("gather_%s_start" % stage, late_bufs[stage], token)
    fw = {n: w[n] for n in _REPLICATED} | gathered(early, gm)
    vec_shards = [_unpack(gv[s], [w[n].shape for n in _VECTORS]) for s in range(N_XY)]
    for i, n in enumerate(_VECTORS):
        fw[n] = jnp.concatenate([vec_shards[s][i] for s in range(N_XY)], axis=_SHARD_AXIS[n])
    fw["ln_gains"] = fw["ln_gains"] + token[0, 0]

    def late_weights(stage, after):
        land = _gather_xy_wait("gather_%s_wait" % stage, *in_flight[stage], after)
        return gathered(late[stage], lax.dynamic_update_index_in_dim(land, late_bufs[stage], me, 0))

    tile_rows = 256
    groups = dict(layer1=[(n, 1 if n in per_layer else None) for n in _SHARDED if n in per_layer or n[0] == "b"],
                  cm0=[(n, 0) for n in _SHARDED if n in per_layer],
                  rest=[(n, None) for n in _SHARDED if n not in per_layer and n[0] != "b"])

    def piece_shape(n, layer):
        return w[n].shape if layer is None else (1,) + w[n].shape[1:]

    def rows_of(n, layer):
        size = 1
        for s_ in piece_shape(n, layer):
            size *= s_
        return size // PACK_W

    def split(entries):
        in_place = [e for e in entries if w[e[0]].shape[-1] == PACK_W and rows_of(*e) % tile_rows == 0 and rows_of(*e) > 0]
        in_place.sort(key=lambda e: -rows_of(*e))
        return in_place, [e for e in entries if e not in in_place]

    def buffers(entries, grads):
        in_place, packed = split(entries)

        def pieces(s):
            out = []
            for n, layer in in_place + packed:
                ax = _SHARD_AXIS[n]
                size = w[n].shape[ax]
                g = grads[n] if layer is None else grads[n][layer]
                out.append(lax.dynamic_slice_in_dim(g, s * size, size, axis=ax if layer is None else ax - 1))
            return out
        return jnp.stack([_pack(pieces(s), BF16) for s in range(N_XY)]), _pack(pieces(me), F32)

    def update(tag, entries, own, recv):
        in_place, packed = split(entries)
        part = _sum_contributions("sum_grads_" + tag, own, recv)
        sib = _swap_with_sibling("swap_sibling_" + tag, part)
        out = {}
        first = 0
        for n, layer in in_place:
            flat = [src[n].reshape(-1, PACK_W) for src in (w, mom1, mom2)]
            rows = rows_of(n, layer)
            res = _adamw_sharded("adamw_%s_%s" % (n, tag), part, sib, first, *flat, w_first=(layer or 0) * rows, n_rows=rows)
            out[(n, layer)] = [o.reshape(piece_shape(n, layer)) for o in res]
            first += rows
        take = lambda src, n, layer: src[n] if layer is None else src[n][layer:layer + 1]
        flat = [_pack([take(src, n, layer) for n, layer in packed], F32) for src in (w, mom1, mom2)]
        assert first + flat[0].shape[0] == part.shape[0], (first, flat[0].shape, part.shape)
        res = _adamw_sharded("adamw_packed_" + tag, part, sib, first, *flat)
        tail = [_unpack(o, [piece_shape(n, layer) for n, layer in packed]) for o in res]
        for i, e in enumerate(packed):
            out[e] = [tail[kind][i] for kind in range(4)]
        return out

    sent = {}

    def on_grads(stage, grads_so_far):
        gsend, own = buffers(groups[stage], grads_so_far)
        *handles, token = _exchange_xy_start("exchange_%s_start" % stage, gsend, own)
        sent[stage] = (own, handles)
        return token

    loss_part, grad_x, grads = _local_step(x, mem, loss_target, fw, late_weights, on_grads)

    gsend, own = buffers(groups["rest"], grads)
    out = update("rest", groups["rest"], own, _exchange_xy(gsend))
    for stage, (own, handles) in sent.items():
        out |= update(stage, groups[stage], own, _exchange_xy_wait("exchange_%s_wait" % stage, *handles, grad_x))
    sharded_out = [[] for _ in range(4)]
    for n in _SHARDED:
        for kind in range(4):
            if n in per_layer:
                sharded_out[kind].append(jnp.concatenate([out[(n, 0)][kind], out[(n, 1)][kind]], axis=0))
            else:
                sharded_out[kind].append(out[(n, None)][kind])

    small = _pack([grads[n] for n in _REPLICATED] + [loss_part.reshape(1)], F32, SUBLANES)
    parts = _all_gather_all(small)
    zero = jnp.zeros((1,), F32)
    flat = [_pack([src[n] for n in _REPLICATED] + [zero], F32, SUBLANES) for src in (w, mom1, mom2)]
    repl_out = [_unpack(o, [w[n].shape for n in _REPLICATED] + [(1,)]) for o in _adamw_replicated(parts, *flat)]
    loss = repl_out[0][-1][0]

    result = [loss, grad_x]
    for kind in range(4):
        by_name = dict(zip(_SHARDED, sharded_out[kind])) | dict(zip(_REPLICATED, repl_out[kind][:-1]))
        result += [by_name[n] for n in _WEIGHTS]
    return tuple(result)
```

```python
import functools

import jax
import jax.numpy as jnp
from jax import lax
from jax.experimental import pallas as pl
from jax.experimental.pallas import tpu as pltpu

F32 = jnp.float32
BF16 = jnp.bfloat16
MESH = pl.DeviceIdType.MESH

LANES = 128
SUBLANES = 8
VMEM_LIMIT_BYTES = 48 * 1024 * 1024

RMS_EPS = 1e-6
LRU_C = 8.0
LRU_HEADS = 4
CONV_WIDTH = 4
RWKV_HEAD = 64
RWKV_GN_EPS = 64e-5
MEM_HEADS = 4
ADAM_LR = 0.001
ADAM_B1 = 0.9
ADAM_B2 = 0.999
ADAM_EPS = 1e-08
ADAM_WD = 0.01
ADAM_STEP = 10
WKV_CHUNK = 64

_PARAMS = functools.partial(pltpu.CompilerParams, vmem_limit_bytes=VMEM_LIMIT_BYTES)


def _tile(n, want):
    if n <= want:
        return n
    t = want
    while t >= SUBLANES:
        if n % t == 0 and t % SUBLANES == 0:
            return t
        t -= SUBLANES
    return n


def _fold8(v):
    tm, d = v.shape
    if tm == SUBLANES:
        return v
    return jnp.sum(v.reshape(tm // SUBLANES, SUBLANES, d), axis=0)


def _rowwise(name, body, rows, consts=(), out_rows=(), out_accs=(), prev=(), nxt=(), tm=512, seq=None, n_rows=None):
    rows = [r if isinstance(r, tuple) else (r, r.shape[1], 0) for r in rows]
    rows = [r if len(r) == 4 else r + (0,) for r in rows]
    t = rows[0][0].shape[0] if n_rows is None else n_rows
    tm = _tile(t, tm)
    if seq is not None:
        tm = _tile(seq, tm)
    nblk = t // tm
    nrow, ncst, nprev, nnxt = len(rows), len(consts), len(prev), len(nxt)
    nor, noa = len(out_rows), len(out_accs)
    hb = tm // SUBLANES

    def kern(*refs):
        i = pl.program_id(0)
        rv = [r[...] for r in refs[:nrow]]
        cv = [c[...] for c in refs[nrow:nrow + ncst]]
        o = nrow + ncst
        pv = []
        for j in range(nprev):
            at_start = (i * tm) % seq == 0
            h = refs[o + j][...]
            pv.append(jnp.where(at_start, jnp.zeros_like(h), h))
        o += nprev
        nv = []
        for j in range(nnxt):
            at_end = ((i + 1) * tm) % seq == 0
            h = refs[o + j][...]
            nv.append(jnp.where(at_end, jnp.zeros_like(h), h))
        o += nnxt
        outs, accs = body(rv, cv, pv, nv)
        for j in range(nor):
            refs[o + j][...] = outs[j].astype(refs[o + j].dtype)
        o += nor
        if noa:
            @pl.when(i == 0)
            def _():
                for j in range(noa):
                    refs[o + j][...] = jnp.zeros_like(refs[o + j])
            for j in range(noa):
                refs[o + j][...] += _fold8(accs[j].astype(F32))

    assert all(first % tm == 0 for (_, _, _, first) in rows), name
    in_specs = [pl.BlockSpec((tm, w), functools.partial(lambda i, c, o: (i + o, c), c=cb, o=first // tm))
                for (_, w, cb, first) in rows]
    in_specs += [pl.BlockSpec(c.shape, lambda i: (0, 0)) for c in consts]
    in_specs += [pl.BlockSpec((SUBLANES, rows[j][1]),
                              functools.partial(lambda i, c: (jnp.maximum(i * hb - 1, 0), c), c=rows[j][2])) for j in prev]
    in_specs += [pl.BlockSpec((SUBLANES, rows[j][1]),
                              functools.partial(lambda i, c: (jnp.minimum((i + 1) * hb, t // SUBLANES - 1), c), c=rows[j][2]))
                 for j in nxt]
    out_shape = [jax.ShapeDtypeStruct((t, w), dt) for (w, dt) in out_rows]
    out_shape += [jax.ShapeDtypeStruct((SUBLANES, w), F32) for w in out_accs]
    out_specs = [pl.BlockSpec((tm, w), lambda i: (i, 0)) for (w, _) in out_rows]
    out_specs += [pl.BlockSpec((SUBLANES, w), lambda i: (0, 0)) for w in out_accs]
    args = [r[0] for r in rows] + list(consts) + [rows[j][0] for j in prev] + [rows[j][0] for j in nxt]
    res = pl.pallas_call(
        kern, name=name, grid=(nblk,), in_specs=in_specs, out_specs=out_specs, out_shape=out_shape,
        compiler_params=_PARAMS(dimension_semantics=("arbitrary",)),
    )(*args)
    return list(res[:nor]), list(res[nor:])


def _shift_down(x, halo, k):
    rolled = pltpu.roll(x, k, 0)
    row = lax.broadcasted_iota(jnp.int32, (SUBLANES, x.shape[1]), 0)
    first = jnp.where(row < k, pltpu.roll(halo, k, 0), rolled[:SUBLANES])
    if x.shape[0] == SUBLANES:
        return first
    return jnp.concatenate([first, rolled[SUBLANES:]], axis=0)


def _shift_up(x, halo, k):
    n = x.shape[0]
    rolled = pltpu.roll(x, n - k, 0)
    row = lax.broadcasted_iota(jnp.int32, (SUBLANES, x.shape[1]), 0)
    last = jnp.where(row >= SUBLANES - k, pltpu.roll(halo, SUBLANES - k, 0), rolled[n - SUBLANES:])
    if n == SUBLANES:
        return last
    return jnp.concatenate([rolled[:n - SUBLANES], last], axis=0)


class _Transposed:
    def __init__(self, w):
        self.w = w


def _t(w):
    return _Transposed(w)


def _mm(name, a, b, out_dtype=F32, trans_a=False, tm=1024, tn=1024, tk=1024, epilogue=None, extra=None):
    trans_b = isinstance(b, _Transposed)
    assert not (trans_a and trans_b)
    if trans_b:
        b = b.w
    if trans_a:
        kdim, m = a.shape
    else:
        m, kdim = a.shape
    n = b.shape[0] if trans_b else b.shape[1]
    assert b.shape[1 if trans_b else 0] == kdim, (name, a.shape, b.shape)
    tm, tn, tk = _tile(m, tm), _tile(n, tn), _tile(kdim, tk)
    nk = kdim // tk
    dims = (((0,), (0,)), ((), ())) if trans_a else (((1,), (1 if trans_b else 0,)), ((), ()))

    n_in = 2 if extra is None else 3

    def kern(*refs):
        a_ref, b_ref, o_ref, acc = refs[0], refs[1], refs[n_in], refs[n_in + 1:]

        def store(res):
            if epilogue is not None:
                res = epilogue(res) if extra is None else epilogue(res, refs[2][...])
            o_ref[...] = res.astype(o_ref.dtype)

        part = lax.dot_general(a_ref[...].astype(BF16), b_ref[...].astype(BF16), dims, preferred_element_type=F32)
        if nk == 1:
            store(part)
        else:
            k = pl.program_id(2)

            @pl.when(k == 0)
            def _():
                acc[0][...] = part

            @pl.when(k > 0)
            def _():
                acc[0][...] += part

            @pl.when(k == nk - 1)
            def _():
                store(acc[0][...])

    a_spec = pl.BlockSpec((tk, tm), lambda i, j, k: (k, i)) if trans_a else pl.BlockSpec((tm, tk), lambda i, j, k: (i, k))
    b_spec = pl.BlockSpec((tn, tk), lambda i, j, k: (j, k)) if trans_b else pl.BlockSpec((tk, tn), lambda i, j, k: (k, j))
    out_spec = pl.BlockSpec((tm, tn), lambda i, j, k: (i, j))
    return pl.pallas_call(
        kern, name=name, grid=(m // tm, n // tn, nk),
        in_specs=[a_spec, b_spec] + ([] if extra is None else [out_spec]),
        out_specs=out_spec,
        out_shape=jax.ShapeDtypeStruct((m, n), out_dtype),
        scratch_shapes=[] if nk == 1 else [pltpu.VMEM((tm, tn), F32)],
        compiler_params=_PARAMS(dimension_semantics=("parallel", "parallel", "arbitrary")),
    )(*((a, b) if extra is None else (a, b, extra)))


def _gates_mm(name, x, gate_w, mode, dz=None, tm=1024):
    _, nh, blk, _ = gate_w.shape
    d = nh * blk
    t = x.shape[0]
    tm = _tile(t, tm)
    cols = lambda g, h: slice(g * d + h * blk, g * d + (h + 1) * blk)

    def fwd(x_ref, w_ref, o_ref):
        for h in range(nh):
            xh = x_ref[:, cols(0, h)].astype(BF16)
            for g in range(2):
                o_ref[:, cols(g, h)] = jnp.dot(xh, w_ref[g, h].astype(BF16), preferred_element_type=F32)

    def dx(dz_ref, w_ref, o_ref):
        for h in range(nh):
            o_ref[:, cols(0, h)] = sum(lax.dot_general(dz_ref[:, cols(g, h)].astype(BF16), w_ref[g, h].astype(BF16), _NT,
                                                       preferred_element_type=F32) for g in range(2))

    def dw(x_ref, dz_ref, o_ref):
        @pl.when(pl.program_id(0) == 0)
        def _():
            o_ref[...] = jnp.zeros_like(o_ref)

        for h in range(nh):
            xh = x_ref[:, cols(0, h)].astype(BF16)
            for g in range(2):
                o_ref[g, h] += lax.dot_general(xh, dz_ref[:, cols(g, h)].astype(BF16), _TN, preferred_element_type=F32)

    row = lambda width: pl.BlockSpec((tm, width), lambda i: (i, 0))
    whole = pl.BlockSpec(gate_w.shape, lambda i: (0, 0, 0, 0))
    kern, args, in_specs, out_spec, out_shape, sem = {
        "fwd": (fwd, (x, gate_w), [row(d), whole], row(2 * d), jax.ShapeDtypeStruct((t, 2 * d), F32), "parallel"),
        "dx": (dx, (x, gate_w), [row(2 * d), whole], row(d), jax.ShapeDtypeStruct((t, d), F32), "parallel"),
        "dw": (dw, (x, dz), [row(d), row(2 * d)], whole, jax.ShapeDtypeStruct(gate_w.shape, F32), "arbitrary"),
    }[mode]
    return pl.pallas_call(kern, name=name, grid=(t // tm,), in_specs=in_specs, out_specs=out_spec, out_shape=out_shape,
                          compiler_params=_PARAMS(dimension_semantics=(sem,)))(*args)


def _scan(name, a, b, seq, reverse=False, tm=256):
    t, d = a.shape
    tm = _tile(seq, tm)
    nblk = t // tm
    ntile = tm // SUBLANES

    def kern(a_ref, b_ref, h_ref, carry_h, carry_a):
        i = pl.program_id(0)
        blk = (nblk - 1 - i) if reverse else i
        edge = (((blk + 1) * tm) % seq == 0) if reverse else ((blk * tm) % seq == 0)

        @pl.when(edge)
        def _():
            carry_h[...] = jnp.zeros_like(carry_h)
            carry_a[...] = jnp.zeros_like(carry_a)

        def tile_step(j, c):
            jj = (ntile - 1 - j) if reverse else j
            rows = pl.ds(pl.multiple_of(jj * SUBLANES, SUBLANES), SUBLANES)
            a8 = a_ref[rows, :]
            b8 = b_ref[rows, :]
            h, an = c
            out = [None] * SUBLANES
            order = range(SUBLANES - 1, -1, -1) if reverse else range(SUBLANES)
            for r in order:
                if reverse:
                    h = b8[r:r + 1, :] + an * h
                    an = a8[r:r + 1, :]
                else:
                    h = a8[r:r + 1, :] * h + b8[r:r + 1, :]
                out[r] = h
            h_ref[rows, :] = jnp.concatenate(out, axis=0)
            return (h, an)

        h, an = lax.fori_loop(0, ntile, tile_step, (carry_h[...], carry_a[...]))
        carry_h[...] = h
        carry_a[...] = an

    idx = (lambda i: (nblk - 1 - i, 0)) if reverse else (lambda i: (i, 0))
    return pl.pallas_call(
        kern, name=name, grid=(nblk,),
        in_specs=[pl.BlockSpec((tm, d), idx), pl.BlockSpec((tm, d), idx)],
        out_specs=pl.BlockSpec((tm, d), idx),
        out_shape=jax.ShapeDtypeStruct((t, d), F32),
        scratch_shapes=[pltpu.VMEM((1, d), F32), pltpu.VMEM((1, d), F32)],
        compiler_params=_PARAMS(dimension_semantics=("arbitrary",)),
    )(a, b)


_NN = (((1,), (0,)), ((), ()))
_NT = (((1,), (1,)), ((), ()))
_TN = (((0,), (0,)), ((), ()))


def _dot1(a, b, dims):
    return lax.dot_general(a.astype(BF16), b.astype(BF16), dims, preferred_element_type=F32)


def _dot3(a, b, dims):
    a_hi, b_hi = a.astype(BF16), b.astype(BF16)
    a_lo, b_lo = (a - a_hi.astype(F32)).astype(BF16), (b - b_hi.astype(F32)).astype(BF16)
    dg = lambda p, q: lax.dot_general(p, q, dims, preferred_element_type=F32)
    return dg(a_hi, b_hi) + (dg(a_hi, b_lo) + dg(a_lo, b_hi))


def _make_bmm(dot):
    def make(dims, da_rule, db_rule):
        @jax.custom_vjp
        def f(a, b):
            return dot(a, b, dims)

        def fwd(a, b):
            return dot(a, b, dims), (a, b)

        def bwd(res, g):
            a, b = res
            return da_rule(a, b, g), db_rule(a, b, g)

        f.defvjp(fwd, bwd)
        return f

    return dict(nn=make(_NN, lambda a, b, g: dot(g, b, _NT), lambda a, b, g: dot(a, g, _TN)),
                nt=make(_NT, lambda a, b, g: dot(g, b, _NN), lambda a, b, g: dot(g, a, _TN)),
                tn=make(_TN, lambda a, b, g: dot(b, g, _NT), lambda a, b, g: dot(a, g, _NN)))


_BMM = {1: _make_bmm(_dot1), 3: _make_bmm(_dot3)}
_WKV_PASSES = dict(pair=1, read=1, inv=3, apply=1, write=1)


def _running_sum(x, reverse):
    c = x.shape[0]
    row = lax.broadcasted_iota(jnp.int32, x.shape, 0)
    k = 1
    while k < c:
        if reverse:
            x = x + jnp.where(row < c - k, pltpu.roll(x, c - k, 0), 0.0)
        else:
            x = x + jnp.where(row >= k, pltpu.roll(x, k, 0), 0.0)
        k *= 2
    return x


@jax.custom_vjp
def _cumsum_rows(x):
    return _running_sum(x, False)


_cumsum_rows.defvjp(lambda x: (_running_sum(x, False), None), lambda _, g: (_running_sum(g, True),))


@jax.custom_vjp
def _unit_lower_inverse(nl):
    c = nl[0].shape[0]
    mm = _BMM[_WKV_PASSES["inv"]]["nn"]
    eye = jnp.where(lax.broadcasted_iota(jnp.int32, (c, c), 0) == lax.broadcasted_iota(jnp.int32, (c, c), 1), 1.0, 0.0)
    inv = [eye + z for z in nl]
    p = nl
    for _ in range(c.bit_length() - 2):
        p = [mm(z, z) for z in p]
        inv = [i_ + mm(p_, i_) for i_, p_ in zip(inv, p)]
    return inv


def _unit_lower_inverse_fwd(nl):
    inv = _unit_lower_inverse(nl)
    return inv, inv


def _unit_lower_inverse_bwd(inv, g):
    mm = _BMM[_WKV_PASSES["inv"]]
    left = [mm["tn"](x, g_) for x, g_ in zip(inv, g)]
    return ([mm["nt"](l_, x) for l_, x in zip(left, inv)],)


_unit_lower_inverse.defvjp(_unit_lower_inverse_fwd, _unit_lower_inverse_bwd)


@jax.custom_vjp
def _kept_inverse(nl, inv):
    return inv


_kept_inverse.defvjp(lambda nl, inv: (inv, inv),
                     lambda inv, g: (_unit_lower_inverse_bwd(inv, g)[0], [jnp.zeros_like(x) for x in inv]))


def _wkv_chunk(r, lw, k, v, a, b, s0, kept_inv=None):
    c = r[0].shape[0]
    ti = lax.broadcasted_iota(jnp.int32, (c, 2 * c), 0)
    tj = lax.broadcasted_iota(jnp.int32, (c, 2 * c), 1)
    right = tj >= c
    tau = jnp.where(right, tj - c, tj)
    strict_left = jnp.logical_and(jnp.logical_not(right), tau < ti)[:, :c]
    strict_right = jnp.logical_and(right, tau < ti)
    incl = tau <= ti
    last = lax.broadcasted_iota(jnp.int32, r[0].shape, 0) == c - 1
    each = lambda f, *ls: [f(*z) for z in zip(*ls)]
    rows2 = lambda x, y: jnp.concatenate([x, y], axis=0)
    pair, read, inv_, apply_, write = (_BMM[_WKV_PASSES[role]] for role in ("pair", "read", "inv", "apply", "write"))
    cum = each(_cumsum_rows, lw)
    w_incl = each(jnp.exp, cum)
    w_inv = each(lambda z: jnp.exp(-z), cum)
    at = each(lambda a_, c_, l_: a_ * jnp.exp(c_ - l_), a, cum, lw)
    ar = each(rows2, at, each(jnp.multiply, r, w_incl))
    bk = each(rows2, each(jnp.multiply, b, w_inv), each(jnp.multiply, k, w_inv))
    pp = each(pair["nt"], ar, bk)
    sr = each(read["nt"], ar, s0)
    nl = [jnp.where(strict_left, z[:c, :c], 0.0) for z in pp]
    zero_v = each(lambda v_: rows2(jnp.zeros_like(v_), v_), v)
    rhs = each(lambda s, z, zv: s[:c] + apply_["nn"](jnp.where(strict_right, z[:c], 0.0), zv), sr, pp, zero_v)
    inv = _unit_lower_inverse(nl) if kept_inv is None else _kept_inverse(nl, kept_inv)
    ut = each(inv_["nn"], inv, rhs)
    uv = each(rows2, ut, v)
    y = each(lambda s, z, uv_: s[c:] + apply_["nn"](jnp.where(incl, z[c:], 0.0), uv_), sr, pp, uv)
    w_end = each(lambda z: jnp.exp(jnp.sum(jnp.where(last, z, 0.0), axis=0, keepdims=True)), cum)
    s1 = each(lambda s, uv_, bk_, w_: (s + write["tn"](uv_, bk_)) * w_, s0, uv, bk, w_end)
    return y, s1, inv


def _wkv_fwd(r, lw, k, v, a, b, seq, hb=16):
    t, d = r.shape
    n = RWKV_HEAD
    nh = d // n
    hb = min(hb, nh)
    chunk = min(WKV_CHUNK, seq)
    ncs = seq // chunk

    def kern(r_ref, lw_ref, k_ref, v_ref, a_ref, b_ref, y_ref, st_ref, inv_ref, s_scr):
        @pl.when(pl.program_id(2) == 0)
        def _():
            s_scr[...] = jnp.zeros_like(s_scr)

        heads = lambda ref: [ref[:, h * n:(h + 1) * n] for h in range(hb)]
        s0 = [s_scr[h] for h in range(hb)]
        y, s1, inv = _wkv_chunk(heads(r_ref), heads(lw_ref), heads(k_ref), heads(v_ref), heads(a_ref), heads(b_ref), s0)
        for h in range(hb):
            st_ref[0, h] = s0[h]
            inv_ref[0, h] = inv[h]
            y_ref[:, h * n:(h + 1) * n] = y[h]
            s_scr[h] = s1[h]

    vec = pl.BlockSpec((chunk, hb * n), lambda bb, g, c: (bb * ncs + c, g))
    per_chunk = lambda rows: pl.BlockSpec((1, hb, rows, rows), lambda bb, g, c: (bb * ncs + c, g, 0, 0))
    return pl.pallas_call(
        kern, name="wkv_fwd", grid=(t // seq, nh // hb, ncs), in_specs=[vec] * 6,
        out_specs=[vec, per_chunk(n), per_chunk(chunk)],
        out_shape=[jax.ShapeDtypeStruct((t, d), F32), jax.ShapeDtypeStruct((t // chunk, nh, n, n), F32),
                   jax.ShapeDtypeStruct((t // chunk, nh, chunk, chunk), F32)],
        scratch_shapes=[pltpu.VMEM((hb, n, n), F32)],
        compiler_params=_PARAMS(dimension_semantics=("parallel", "parallel", "arbitrary")),
    )(r, lw, k, v, a, b)


def _wkv_bwd(r, lw, k, v, a, b, st, inv, dy, seq, hb=16):
    t, d = r.shape
    n = RWKV_HEAD
    nh = d // n
    hb = min(hb, nh)
    chunk = min(WKV_CHUNK, seq)
    ncs = seq // chunk

    def kern(r_ref, lw_ref, k_ref, v_ref, a_ref, b_ref, st_ref, inv_ref, dy_ref,
             dr_ref, dlw_ref, dk_ref, dv_ref, da_ref, db_ref, ds_scr):
        @pl.when(pl.program_id(2) == 0)
        def _():
            ds_scr[...] = jnp.zeros_like(ds_scr)

        heads = lambda ref: [ref[:, h * n:(h + 1) * n] for h in range(hb)]
        kept = [inv_ref[0, h] for h in range(hb)]
        _, vjp = jax.vjp(lambda *args: _wkv_chunk(*args, kept_inv=kept)[:2],
                         heads(r_ref), heads(lw_ref), heads(k_ref), heads(v_ref), heads(a_ref), heads(b_ref),
                         [st_ref[0, h] for h in range(hb)])
        grads = vjp((heads(dy_ref), [ds_scr[h] for h in range(hb)]))
        for h in range(hb):
            for ref, g in zip((dr_ref, dlw_ref, dk_ref, dv_ref, da_ref, db_ref), grads[:6]):
                ref[:, h * n:(h + 1) * n] = g[h]
            ds_scr[h] = grads[6][h]

    vec = pl.BlockSpec((chunk, hb * n), lambda bb, g, c: (bb * ncs + ncs - 1 - c, g))
    per_chunk = lambda rows: pl.BlockSpec((1, hb, rows, rows), lambda bb, g, c: (bb * ncs + ncs - 1 - c, g, 0, 0))
    return pl.pallas_call(
        kern, name="wkv_bwd", grid=(t // seq, nh // hb, ncs),
        in_specs=[vec] * 6 + [per_chunk(n), per_chunk(chunk), vec],
        out_specs=[vec] * 6, out_shape=[jax.ShapeDtypeStruct((t, d), F32)] * 6,
        scratch_shapes=[pltpu.VMEM((hb, n, n), F32)],
        compiler_params=_PARAMS(dimension_semantics=("parallel", "parallel", "arbitrary")),
    )(r, lw, k, v, a, b, st, inv, dy)


def _softmax_rows(s):
    e = jnp.exp(s - jnp.max(s, axis=-1, keepdims=True))
    return e / jnp.sum(e, axis=-1, keepdims=True)


def _attn_fwd(q, kv, seq, mem_len, tq=512):
    t, d = q.shape
    dh = d // MEM_HEADS
    scale = dh ** -0.5
    tq = _tile(seq, tq)
    nq = seq // tq

    def kern(q_ref, kv_ref, o_ref):
        for h in range(MEM_HEADS):
            cols = slice(h * dh, (h + 1) * dh)
            vcols = slice(d + h * dh, d + (h + 1) * dh)
            s = lax.dot_general(q_ref[:, cols], kv_ref[:, cols], _NT, preferred_element_type=F32) * scale
            p = _softmax_rows(s)
            o_ref[:, cols] = jnp.dot(p.astype(BF16), kv_ref[:, vcols], preferred_element_type=F32).astype(o_ref.dtype)

    return pl.pallas_call(
        kern, name="attn_fwd", grid=(t // seq, nq),
        in_specs=[pl.BlockSpec((tq, d), lambda b, i: (b * nq + i, 0)), pl.BlockSpec((mem_len, 2 * d), lambda b, i: (b, 0))],
        out_specs=pl.BlockSpec((tq, d), lambda b, i: (b * nq + i, 0)),
        out_shape=jax.ShapeDtypeStruct((t, d), BF16),
        compiler_params=_PARAMS(dimension_semantics=("parallel", "parallel")),
    )(q, kv)


def _attn_bwd(q, kv, do, seq, mem_len, tq=512):
    t, d = q.shape
    dh = d // MEM_HEADS
    scale = dh ** -0.5
    tq = _tile(seq, tq)
    nq = seq // tq

    def kern(q_ref, kv_ref, do_ref, dq_ref, dkv_ref):
        @pl.when(pl.program_id(1) == 0)
        def _():
            dkv_ref[...] = jnp.zeros_like(dkv_ref)

        for h in range(MEM_HEADS):
            cols = slice(h * dh, (h + 1) * dh)
            vcols = slice(d + h * dh, d + (h + 1) * dh)
            qh, kh, vh, doh = q_ref[:, cols], kv_ref[:, cols], kv_ref[:, vcols], do_ref[:, cols]
            p = _softmax_rows(lax.dot_general(qh, kh, _NT, preferred_element_type=F32) * scale)
            dp = lax.dot_general(doh, vh, _NT, preferred_element_type=F32)
            ds = (p * (dp - jnp.sum(p * dp, axis=-1, keepdims=True)) * scale).astype(BF16)
            dq_ref[:, cols] = jnp.dot(ds, kh, preferred_element_type=F32).astype(dq_ref.dtype)
            dkv_ref[:, cols] += lax.dot_general(ds, qh, _TN, preferred_element_type=F32)
            dkv_ref[:, vcols] += lax.dot_general(p.astype(BF16), doh, _TN, preferred_element_type=F32)

    return pl.pallas_call(
        kern, name="attn_bwd", grid=(t // seq, nq),
        in_specs=[pl.BlockSpec((tq, d), lambda b, i: (b * nq + i, 0)), pl.BlockSpec((mem_len, 2 * d), lambda b, i: (b, 0)),
                  pl.BlockSpec((tq, d), lambda b, i: (b * nq + i, 0))],
        out_specs=[pl.BlockSpec((tq, d), lambda b, i: (b * nq + i, 0)), pl.BlockSpec((mem_len, 2 * d), lambda b, i: (b, 0))],
        out_shape=[jax.ShapeDtypeStruct((t, d), BF16), jax.ShapeDtypeStruct(kv.shape, F32)],
        compiler_params=_PARAMS(dimension_semantics=("parallel", "arbitrary")),
    )(q, kv, do)


def _rstd(x):
    return lax.rsqrt(jnp.mean(x * x, axis=-1, keepdims=True) + RMS_EPS)


def _rms(x, g):
    return x * _rstd(x) * g


def _rms_bwd(dy, x, g):
    rstd = _rstd(x)
    xhat = x * rstd
    dxhat = dy * g
    return rstd * (dxhat - xhat * jnp.mean(dxhat * xhat, axis=-1, keepdims=True)), dy * xhat


def _softplus(x):
    return jnp.maximum(x, 0.0) + jnp.log1p(jnp.exp(-jnp.abs(x)))


def _one_minus_exp(x):
    series = -x * (1.0 + x * (0.5 + x * (1.0 / 6.0 + x * (1.0 / 24.0 + x * (1.0 / 120.0)))))
    return jnp.where(x > -0.05, series, 1.0 - jnp.exp(x))


_GELU_C = 0.7978845608028654
_GELU_K = 0.044715


def _gelu(x):
    return 0.5 * x * (1.0 + jnp.tanh(_GELU_C * (x + _GELU_K * x * x * x)))


def _gelu_grad(x):
    th = jnp.tanh(_GELU_C * (x + _GELU_K * x * x * x))
    return 0.5 * (1.0 + th) + 0.5 * x * (1.0 - th * th) * _GELU_C * (1.0 + 3.0 * _GELU_K * x * x)


def _seg_sum(x, seg):
    def two_terms(v, dims):
        hi = v.astype(BF16)
        lo = (v - hi.astype(F32)).astype(BF16)
        return (lax.dot_general(hi, seg, dims, preferred_element_type=F32)
                + lax.dot_general(lo, seg, dims, preferred_element_type=F32))
    return two_terms(two_terms(x, _NN), _NT)


def _f32(v):
    return v.astype(F32)


def _norm_fwd(name, x, g, dtype=BF16):
    (hn,), _ = _rowwise(name, lambda rv, cv, pv, nv: ([_rms(_f32(rv[0]), cv[0])], []), [x], [g], [(x.shape[1], dtype)])
    return hn


def _resid_norm_fwd(name, x, t, g, bias=None, chain=None):
    d = x.shape[1]
    chain = {} if chain is None else chain

    def body(rv, cv, pv, nv):
        tt = rv[1] if bias is None else rv[1] + cv[1]
        y = rv[0] + _rms(tt, cv[0])
        if "gain" in chain:
            return [y, _rms(y, cv[-1])], []
        if "target" in chain:
            err = y - rv[2]
            return [err * (1.0 / d)], [err * err]
        return [y], []
    consts = [g] if bias is None else [g, bias]
    if "gain" in chain:
        (y, chain["hn"]), _ = _rowwise(name, body, [x, t], consts + [chain["gain"]], [(d, F32), (d, BF16)])
        return y
    if "target" in chain:
        (chain["dy"],), (chain["sq"],) = _rowwise(name, body, [x, t, chain["target"]], consts, [(d, F32)], [d])
        return None
    (y,), _ = _rowwise(name, body, [x, t], consts, [(d, F32)])
    return y


def _resid_norm_bwd(name, dxn, t, g, bias=None):
    def body(rv, cv, pv, nv):
        tt = rv[1] if bias is None else rv[1] + cv[1]
        dt, dg = _rms_bwd(rv[0], tt, cv[0])
        return [dt], [dg, dt]
    d = t.shape[1]
    (dt,), (dg, db) = _rowwise(name, body, [dxn, t], [g] if bias is None else [g, bias], [(d, BF16)], [d, d])
    return dt, dg.sum(0), db.sum(0)


def _prenorm_bwd(name, dxn, dhn, x, g):
    def body(rv, cv, pv, nv):
        dx, dg = _rms_bwd(_f32(rv[1]), rv[2], cv[0])
        return [rv[0] + dx], [dg]
    d = x.shape[1]
    (dx,), (dg,) = _rowwise(name, body, [dxn, dhn, x], [g], [(d, F32)], [d])
    return dx, dg.sum(0)


def _mlp_fwd(tag, x, g_pre, g_post, w_up, w_down, hn=None, chain=None):
    hn = _norm_fwd(tag + "_norm", x, g_pre) if hn is None else hn
    act = _mm(tag + "_up", hn, w_up, out_dtype=BF16, epilogue=lambda up: jnp.square(jnp.maximum(up, 0.0)))
    m = _mm(tag + "_down", act, w_down)
    y = _resid_norm_fwd(tag + "_res", x, m, g_post, chain=chain)
    return y, (x, hn, act, m)


def _mlp_bwd(tag, saved, dy, g_pre, g_post, w_up_t, w_down_t):
    x, hn, act, m = saved
    dm, dg_post, _ = _resid_norm_bwd(tag + "_dres", dy, m, g_post)
    dup = _mm(tag + "_dup", dm, w_down_t, out_dtype=BF16, extra=act,
              epilogue=lambda dact, act_: dact * 2.0 * jnp.sqrt(_f32(act_)))
    dw_down = _mm(tag + "_dwdown", act, dm, trans_a=True)
    dw_up = _mm(tag + "_dwup", hn, dup, trans_a=True)
    dhn = _mm(tag + "_dhn", dup, w_up_t)
    dx, dg_pre = _prenorm_bwd(tag + "_dnorm", dy, dhn, x, g_pre)
    return dx, dict(g_pre=dg_pre, g_post=dg_post, w_up=dw_up, w_down=dw_down)


def _xattn_fwd(tag, x, mem_n, g_pre, g_post, w_q, w_kv, w_o, seq, mem_len, hn=None, chain=None):
    hn = _norm_fwd(tag + "_norm", x, g_pre) if hn is None else hn
    q = _mm(tag + "_q", hn, w_q, out_dtype=BF16)
    kv = _mm(tag + "_kv", mem_n, w_kv, out_dtype=BF16)
    o = _attn_fwd(q, kv, seq, mem_len)
    c = _mm(tag + "_o", o, w_o)
    y = _resid_norm_fwd(tag + "_res", x, c, g_post, chain=chain)
    return y, (x, hn, q, kv, o, c)


def _xattn_bwd(tag, saved, dy, mem_n, g_pre, g_post, w_q_t, w_kv_t, w_o_t, seq, mem_len):
    x, hn, q, kv, o, c = saved
    dc, dg_post, _ = _resid_norm_bwd(tag + "_dres", dy, c, g_post)
    do = _mm(tag + "_do", dc, w_o_t, out_dtype=BF16)
    dw_o = _mm(tag + "_dwo", o, dc, trans_a=True)
    dq, dkv = _attn_bwd(q, kv, do, seq, mem_len)
    dw_q = _mm(tag + "_dwq", hn, dq, trans_a=True)
    dhn = _mm(tag + "_dhn", dq, w_q_t)
    dw_kv = _mm(tag + "_dwkv", mem_n, dkv, trans_a=True)
    dmem_n = _mm(tag + "_dmem", dkv, w_kv_t)
    dx, dg_pre = _prenorm_bwd(tag + "_dnorm", dy, dhn, x, g_pre)
    return dx, dmem_n, dict(g_pre=dg_pre, g_post=dg_post, w_q=dw_q, w_kv=dw_kv, w_o=dw_o)


def _lru_gates(z0, z1, gb0, gb1, sp):
    r = jax.nn.sigmoid(z0 + gb0)
    i = jax.nn.sigmoid(z1 + gb1)
    log_a = -LRU_C * r * sp
    a = jnp.exp(log_a)
    mult = jnp.sqrt(_one_minus_exp(2.0 * log_a))
    return r, i, a, mult


def _rglru_fwd(x, p, seq, chain=None):
    d = x.shape[1]
    hn = _norm_fwd("a_norm", x, p["g_pre"])
    proj = _mm("a_in", hn, p["w_in"])

    def conv_body(rv, cv, pv, nv):
        u = rv[0] + cv[0]
        halo = pv[0] + cv[0]
        i = pl.program_id(0)
        halo = jnp.where((i * rv[0].shape[0]) % seq == 0, jnp.zeros_like(halo), halo)
        conv = cv[2] + u * cv[1][CONV_WIDTH - 1:CONV_WIDTH]
        for tap in range(CONV_WIDTH - 1):
            conv = conv + _shift_down(u, halo, CONV_WIDTH - 1 - tap) * cv[1][tap:tap + 1]
        return [conv], []
    (conv,), _ = _rowwise("a_conv", conv_body, [(proj, d, 1)], [p["b_in_u"], p["conv_w"], p["conv_b"]], [(d, F32)],
                          prev=[0], seq=seq)
    z = _gates_mm("a_gate", conv, p["w_gate"], "fwd")

    def gate_body(rv, cv, pv, nv):
        r, i, a, mult = _lru_gates(rv[0], rv[1], cv[0], cv[1], _softplus(-cv[2]))
        return [a, mult * i * rv[2]], []
    (a, bb), _ = _rowwise("a_gates", gate_body, [(z, d, 0), (z, d, 1), conv], [p["gate_b0"], p["gate_b1"], p["lam"]],
                          [(d, F32), (d, F32)])
    h = _scan("a_scan", a, bb, seq)

    def hy_body(rv, cv, pv, nv):
        return [rv[0] * _gelu(rv[1] + cv[0])], []
    (hy,), _ = _rowwise("a_hy", hy_body, [h, (proj, d, 0)], [p["b_in_y"]], [(d, BF16)])
    out = _mm("a_out", hy, p["w_out"])
    y = _resid_norm_fwd("a_res", x, out, p["g_post"], bias=p["b_out"], chain=chain)
    return y, (x, hn, proj, conv, z, a, h, hy, out)


def _rglru_bwd(saved, dy, p, pt, seq):
    x, hn, proj, conv, z, a, h, hy, out = saved
    d = x.shape[1]
    dt, dg_post, db_out = _resid_norm_bwd("a_dres", dy, out, p["g_post"], bias=p["b_out"])
    dhy = _mm("a_dhy", dt, pt["w_out_t"])
    dw_out = _mm("a_dwout", hy, dt, trans_a=True)

    def dh_body(rv, cv, pv, nv):
        yb = rv[2] + cv[0]
        return [rv[0] * _gelu(yb), rv[0] * rv[1] * _gelu_grad(yb)], []
    (dh, dyb), _ = _rowwise("a_dh", dh_body, [dhy, h, (proj, d, 0)], [p["b_in_y"]], [(d, F32), (d, BF16)])
    g = _scan("a_rscan", a, dh, seq, reverse=True)

    def dgate_body(rv, cv, pv, nv):
        gg, hh, z0, z1, cnv = rv
        sp = _softplus(-cv[2])
        r, i, aa, mult = _lru_gates(z0, z1, cv[0], cv[1], sp)
        i_blk = pl.program_id(0)
        halo = jnp.where((i_blk * gg.shape[0]) % seq == 0, jnp.zeros_like(pv[0]), pv[0])
        da = gg * _shift_down(hh, halo, 1)
        dmult = gg * i * cnv
        di = gg * mult * cnv
        dconv = gg * mult * i
        dlog_a = da * aa - dmult * aa * aa / mult
        dz0 = dlog_a * (-LRU_C * sp) * r * (1.0 - r)
        dz1 = di * i * (1.0 - i)
        dsp = dlog_a * (-LRU_C * r)
        dlam = dsp * (-jax.nn.sigmoid(-cv[2]))
        return [jnp.concatenate([dz0, dz1], axis=1), dconv], [dz0, dz1, dlam]
    (dz, dconv1), (dgb0, dgb1, dlam) = _rowwise(
        "a_dgates", dgate_body, [g, h, (z, d, 0), (z, d, 1), conv], [p["gate_b0"], p["gate_b1"], p["lam"]],
        [(2 * d, BF16), (d, F32)], [d, d, d], prev=[1], seq=seq)
    dconv2 = _gates_mm("a_dconv", dz, p["w_gate"], "dx")
    dw_gate = _gates_mm("a_dwgate", conv, p["w_gate"], "dw", dz=dz)

    def dconv_body(rv, cv, pv, nv):
        dc1, dc2, pu, dyb_ = rv
        dc = dc1 + dc2
        dc_next = nv[0] + nv[1]
        u = pu + cv[0]
        i_blk = pl.program_id(0)
        halo = jnp.where((i_blk * u.shape[0]) % seq == 0, jnp.zeros_like(pv[0]), pv[0] + cv[0])
        du = dc * cv[1][CONV_WIDTH - 1:CONV_WIDTH]
        dws = []
        for tap in range(CONV_WIDTH - 1):
            k = CONV_WIDTH - 1 - tap
            du = du + _shift_up(dc, dc_next, k) * cv[1][tap:tap + 1]
            dws.append(dc * _shift_down(u, halo, k))
        dws.append(dc * u)
        return [jnp.concatenate([_f32(dyb_), du], axis=1)], dws + [dc, _f32(dyb_), du]
    (dproj,), accs = _rowwise(
        "a_dconvw", dconv_body, [dconv1, dconv2, (proj, d, 1), dyb], [p["b_in_u"], p["conv_w"]],
        [(2 * d, BF16)], [d] * (CONV_WIDTH + 3), prev=[2], nxt=[0, 1], seq=seq)
    dconv_w = jnp.stack([acc.sum(0) for acc in accs[:CONV_WIDTH]])
    dconv_b = accs[CONV_WIDTH].sum(0)
    db_in = jnp.concatenate([accs[CONV_WIDTH + 1].sum(0), accs[CONV_WIDTH + 2].sum(0)])
    dhn = _mm("a_dhn", dproj, pt["w_in_t"])
    dw_in = _mm("a_dwin", hn, dproj, trans_a=True)
    dx, dg_pre = _prenorm_bwd("a_dnorm", dy, dhn, x, p["g_pre"])
    grads = dict(g_pre=dg_pre, g_post=dg_post, b_out=db_out, w_out=dw_out, gate_b0=dgb0.sum(0), gate_b1=dgb1.sum(0),
                 lam=dlam.sum(0), w_gate=dw_gate, conv_w=dconv_w, conv_b=dconv_b, b_in=db_in, w_in=dw_in)
    return dx, grads


def _rwkv_prep(k, wl, za, w0, a0, k_k, k_a, seg):
    w_in = wl + w0
    e_w = jnp.exp(-_softplus(-w_in) - 0.5)
    a = jax.nn.sigmoid(za + a0)
    q = k * k_k
    norm = jnp.sqrt(_seg_sum(q * q, seg))
    n = jnp.maximum(norm, 1e-12)
    kk = q / n
    return w_in, e_w, a, norm, n, kk


def _rwkv_out(y, r, k2, v, gn_g, gn_b, r_k, seg):
    inv = 1.0 / RWKV_HEAD
    yc = y - _seg_sum(y, seg) * inv
    rstd = lax.rsqrt(_seg_sum(yc * yc, seg) * inv + RWKV_GN_EPS)
    yhat = yc * rstd
    s = _seg_sum(r * k2 * r_k, seg)
    return rstd, yhat, s, yhat * gn_g + gn_b + s * v


def _rwkv_fwd(x, p, seq, chain=None):
    t, d = x.shape
    nseq = t // seq

    def mix_body(rv, cv, pv, nv):
        hn = _rms(rv[0], cv[0])
        xx = _shift_down(hn, _rms(pv[0], cv[0]), 1) - hn
        return [hn] + [hn + xx * cv[1][c:c + 1] for c in range(6)], []
    (hn, xr, xw, xk, xv, xa, xg), _ = _rowwise("b_mix", mix_body, [x], [p["g_pre"], p["mu"]],
                                               [(d, F32)] + [(d, BF16)] * 6, prev=[0], seq=seq)
    r = _mm("b_r", xr, p["w_r"])
    k = _mm("b_k", xk, p["w_k"])
    v = _mm("b_v", xv, p["w_v"])
    lw = _mm("b_w1", xw, p["w1"])
    la = _mm("b_a1", xa, p["a1"], out_dtype=BF16)
    lg = _mm("b_g1", xg, p["g1"])
    (th,), _ = _rowwise("b_tanh", lambda rv, cv, pv, nv: ([jnp.tanh(rv[0])], []), [lw], [], [(lw.shape[1], BF16)])
    (sg,), _ = _rowwise("b_sig", lambda rv, cv, pv, nv: ([jax.nn.sigmoid(rv[0])], []), [lg], [], [(lg.shape[1], BF16)])
    wl = _mm("b_w2", th, p["w2"])
    za = _mm("b_a2", la, p["a2"])
    g = _mm("b_g2", sg, p["g2"])

    def prep_body(rv, cv, pv, nv):
        kk_, wl_, za_ = rv
        _, e_w, a, _, _, kk = _rwkv_prep(kk_, wl_, za_, cv[0], cv[1], cv[2], cv[3], cv[4])
        return [-e_w, kk_ * (1.0 + (a - 1.0) * cv[3]), -kk, kk * a], []
    (log_w, k2, rem_a, rem_b), _ = _rowwise("b_prep", prep_body, [k, wl, za],
                                            [p["w0"], p["a0"], p["k_k"], p["k_a"], p["seg"]], [(d, F32)] * 4, tm=256)
    rec_in = (r, log_w, k2, v, rem_a, rem_b)
    y, *states = _wkv_fwd(*rec_in, seq)

    def out_body(rv, cv, pv, nv):
        y_, r_, k2_, v_, g_ = rv
        _, _, _, out = _rwkv_out(y_, r_, k2_, v_, cv[0], cv[1], cv[2], cv[3])
        return [out * g_], []
    (og,), _ = _rowwise("b_out", out_body, [y, r, k2, v, g], [p["gn_g"], p["gn_b"], p["r_k"], p["seg"]], [(d, BF16)], tm=256)
    o = _mm("b_o", og, p["w_o"])
    res = _resid_norm_fwd("b_res", x, o, p["g_post"], chain=chain)
    return res, (x, hn, xr, xw, xk, xv, xa, xg, r, k, v, th, la, sg, wl, za, g, k2, rec_in, states, y, og, o)


def _rwkv_bwd(saved, dres, p, pt, seq):
    x, hn, xr, xw, xk, xv, xa, xg, r, k, v, th, la, sg, wl, za, g, k2, rec_in, states, y, og, o = saved
    t, d = x.shape
    nseq = t // seq
    do, dg_post, _ = _resid_norm_bwd("b_dres", dres, o, p["g_post"])
    dog = _mm("b_dog", do, pt["w_o_t"])
    dw_o = _mm("b_dwo", og, do, trans_a=True)

    def dout_body(rv, cv, pv, nv):
        dog_, y_, r_, k2_, v_, g_ = rv
        gn_g, gn_b, r_k, bd = cv
        inv = 1.0 / RWKV_HEAD
        rstd, yhat, s, out = _rwkv_out(y_, r_, k2_, v_, gn_g, gn_b, r_k, bd)
        dout = dog_ * g_
        ds = _seg_sum(dout * v_, bd)
        dyhat = dout * gn_g
        dy = rstd * (dyhat - _seg_sum(dyhat, bd) * inv - yhat * _seg_sum(dyhat * yhat, bd) * inv)
        return [dy, dog_ * out, dout * s, ds * k2_ * r_k, ds * r_ * r_k], [ds * r_ * k2_, dout * yhat, dout]
    (dy, dgate, dv_b, dr_b, dk2_b), (dr_k, dgn_g, dgn_b) = _rowwise(
        "b_dout", dout_body, [dog, y, r, k2, v, g], [p["gn_g"], p["gn_b"], p["r_k"], p["seg"]],
        [(d, F32), (d, BF16), (d, F32), (d, F32), (d, F32)], [d, d, d], tm=256)
    dr_rec, dlw_rec, dk2_rec, dv_rec, da_rec, db_rec = _wkv_bwd(*rec_in, *states, dy, seq)

    def dprep_body(rv, cv, pv, nv):
        dr_rec_, dlw_rec_, dk2_rec_, dv_rec_, da_rec_, db_rec_, dr_b_, dk2_b_, dv_b_, k_, wl_, za_ = rv
        w0, a0, k_k, k_a, bd = cv
        w_in, e_w, a, norm, n, kk = _rwkv_prep(k_, wl_, za_, w0, a0, k_k, k_a, bd)
        dk2 = dk2_rec_ + dk2_b_
        dkk = db_rec_ * a - da_rec_
        da = db_rec_ * kk + dk2 * k_ * k_a
        dq = jnp.where(norm > 1e-12, dkk - kk * _seg_sum(kk * dkk, bd), dkk) / n
        dk = dk2 * (1.0 + (a - 1.0) * k_a) + dq * k_k
        dza = da * a * (1.0 - a)
        dwl = dlw_rec_ * (-e_w) * jax.nn.sigmoid(-w_in)
        return [dr_rec_ + dr_b_, dk, dv_rec_ + dv_b_, dza, dwl], [dk2 * k_ * (a - 1.0), dq * k_, dza, dwl]
    (dr, dk, dv, dza, dwl), (dk_a, dk_k, da0, dw0) = _rowwise(
        "b_dprep", dprep_body, [dr_rec, dlw_rec, dk2_rec, dv_rec, da_rec, db_rec, dr_b, dk2_b, dv_b, k, wl, za],
        [p["w0"], p["a0"], p["k_k"], p["k_a"], p["seg"]], [(d, BF16)] * 5, [d] * 4, tm=256)

    dw_r = _mm("b_dwr", xr, dr, trans_a=True)
    dw_k = _mm("b_dwk", xk, dk, trans_a=True)
    dw_v = _mm("b_dwv", xv, dv, trans_a=True)
    dxr = _mm("b_dxr", dr, pt["w_r_t"])
    dxk = _mm("b_dxk", dk, pt["w_k_t"])
    dxv = _mm("b_dxv", dv, pt["w_v_t"])
    da2 = _mm("b_da2", la, dza, trans_a=True)
    dla = _mm("b_dla", dza, pt["a2_t"], out_dtype=BF16)
    da1 = _mm("b_da1", xa, dla, trans_a=True)
    dxa = _mm("b_dxa", dla, pt["a1_t"])
    dw2 = _mm("b_dw2", th, dwl, trans_a=True)
    dth = _mm("b_dth", dwl, pt["w2_t"])
    (dzw,), _ = _rowwise("b_dtanh", lambda rv, cv, pv, nv: ([rv[0] * (1.0 - _f32(rv[1]) * _f32(rv[1]))], []),
                         [dth, th], [], [(th.shape[1], BF16)])
    dw1 = _mm("b_dw1", xw, dzw, trans_a=True)
    dxw = _mm("b_dxw", dzw, pt["w1_t"])
    dg2 = _mm("b_dg2", sg, dgate, trans_a=True)
    dsg = _mm("b_dsg", dgate, pt["g2_t"])
    (dzg,), _ = _rowwise("b_dsig", lambda rv, cv, pv, nv: ([rv[0] * _f32(rv[1]) * (1.0 - _f32(rv[1]))], []),
                         [dsg, sg], [], [(sg.shape[1], BF16)])
    dg1 = _mm("b_dg1", xg, dzg, trans_a=True)
    dxg = _mm("b_dxg", dzg, pt["g1_t"])

    def dmix_body(rv, cv, pv, nv):
        hn_ = rv[0]
        dxs = rv[1:]
        mu = cv[0]
        xx = _shift_down(hn_, pv[0], 1) - hn_
        dsum = dxs[0]
        dxx = dxs[0] * mu[0:1]
        dxx_next = nv[0] * mu[0:1]
        for c in range(1, 6):
            dsum = dsum + dxs[c]
            dxx = dxx + dxs[c] * mu[c:c + 1]
            dxx_next = dxx_next + nv[c] * mu[c:c + 1]
        return [dsum - dxx + _shift_up(dxx, dxx_next, 1)], [dxs[c] * xx for c in range(6)]
    (dhn,), dmu = _rowwise("b_dmix", dmix_body, [hn, dxr, dxw, dxk, dxv, dxa, dxg], [p["mu"]], [(d, F32)], [d] * 6,
                           prev=[0], nxt=[1, 2, 3, 4, 5, 6], seq=seq, tm=256)
    dx, dg_pre = _prenorm_bwd("b_dnorm", dres, dhn, x, p["g_pre"])
    grads = dict(g_pre=dg_pre, g_post=dg_post, mu=jnp.stack([m.sum(0) for m in dmu]), w_r=dw_r, w_k=dw_k, w_v=dw_v,
                 w0=dw0.sum(0), w1=dw1, w2=dw2, a0=da0.sum(0), a1=da1, a2=da2, g1=dg1, g2=dg2, k_k=dk_k.sum(0),
                 k_a=dk_a.sum(0), r_k=dr_k.sum(0), gn_g=dgn_g.sum(0), gn_b=dgn_b.sum(0), w_o=dw_o)
    return dx, grads


_WEIGHTS = ['ln_gains', 'mem_norm', 'a_conv_w', 'a_conv_b', 'a_w_in', 'a_b_in', 'a_gate_w', 'a_gate_b', 'a_lambda', 'a_w_out',
            'a_b_out', 'b_mu', 'b_w_rkv', 'b_w0', 'b_w1', 'b_w2', 'b_a0', 'b_a1', 'b_a2', 'b_g1', 'b_g2', 'b_k_k', 'b_k_a',
            'b_r_k', 'b_gn_g', 'b_gn_b', 'b_w_o', 'c_w_q', 'c_w_kv', 'c_w_o', 'm_w_up', 'm_w_down']
_SHARD_AXIS = dict(ln_gains=2, mem_norm=None, a_conv_w=2, a_conv_b=None, a_w_in=2, a_b_in=None, a_gate_w=3, a_gate_b=3,
                   a_lambda=None, a_w_out=1, a_b_out=None, b_mu=2, b_w_rkv=2, b_w0=1, b_w1=1, b_w2=2, b_a0=1, b_a1=1, b_a2=2,
                   b_g1=1, b_g2=2, b_k_k=1, b_k_a=1, b_r_k=None, b_gn_g=1, b_gn_b=1, b_w_o=1, c_w_q=1, c_w_kv=2, c_w_o=1,
                   m_w_up=2, m_w_down=1)
_MATRICES = ['a_w_in', 'a_gate_w', 'a_w_out', 'b_w_rkv', 'b_w1', 'b_w2', 'b_a1', 'b_a2', 'b_g1', 'b_g2', 'b_w_o', 'c_w_q',
             'c_w_kv', 'c_w_o', 'm_w_up', 'm_w_down']
_SHARDED = [n for n in _WEIGHTS if _SHARD_AXIS[n] is not None]
_VECTORS = [n for n in _SHARDED if n not in _MATRICES]
_REPLICATED = [n for n in _WEIGHTS if _SHARD_AXIS[n] is None]
N_XY = 4
N_DEV = 8
PACK_W = 1024
PACK_ROWS = 256


def _pack(arrs, dtype, row_mult=PACK_ROWS):
    parts = []
    rows = 0
    for a in arrs:
        n = a.size
        r = -(-n // PACK_W)
        parts.append(jnp.pad(a.reshape(-1).astype(dtype), (0, r * PACK_W - n)))
        rows += r
    pad_rows = -(-rows // row_mult) * row_mult - rows
    if pad_rows:
        parts.append(jnp.zeros((pad_rows * PACK_W,), dtype))
    return jnp.concatenate(parts).reshape(-1, PACK_W)


def _unpack(flat, shapes):
    out = []
    row = 0
    for shp in shapes:
        n = 1
        for s in shp:
            n *= s
        r = -(-n // PACK_W)
        out.append(flat[row:row + r].reshape(-1)[:n].reshape(shp))
        row += r
    return out


_ANY = pl.BlockSpec(memory_space=pl.ANY)


def _xy_peers():
    x, y = lax.axis_index("x"), lax.axis_index("y")
    return [(1 - x, y), (x, 1 - y), (1 - x, 1 - y)]


def _all_gather_xy(wm, wv):
    half = wm.shape[0] // 2

    def body(wm_ref, wv_ref, gm_ref, gv_ref, send_sems, recv_sems, local_sems):
        x, y, c = lax.axis_index("x"), lax.axis_index("y"), lax.axis_index("c")
        me = 2 * x + y
        mine = pl.ds(pl.multiple_of(c * half, SUBLANES), half)
        other = pl.ds(pl.multiple_of((1 - c) * half, SUBLANES), half)
        local = [pltpu.make_async_copy(wm_ref, gm_ref.at[me], local_sems.at[0]),
                 pltpu.make_async_copy(wv_ref, gv_ref.at[me], local_sems.at[1])]
        for cp in local:
            cp.start()
        sends, lands, passes, from_sibling = [], [], [], []
        for j, (px, py) in enumerate(_xy_peers()):
            peer = 2 * px + py
            ici = functools.partial(pltpu.make_async_remote_copy, device_id=(px, py, c), device_id_type=MESH)
            sends.append(ici(src_ref=wm_ref.at[mine], dst_ref=gm_ref.at[me, mine], send_sem=send_sems.at[j], recv_sem=recv_sems.at[j]))
            lands.append(ici(src_ref=wm_ref.at[mine], dst_ref=gm_ref.at[peer, mine], send_sem=send_sems.at[j], recv_sem=recv_sems.at[j]))
            sends.append(ici(src_ref=wv_ref, dst_ref=gv_ref.at[me], send_sem=send_sems.at[3 + j], recv_sem=recv_sems.at[3 + j]))
            lands.append(ici(src_ref=wv_ref, dst_ref=gv_ref.at[peer], send_sem=send_sems.at[3 + j], recv_sem=recv_sems.at[3 + j]))
            d2d = functools.partial(pltpu.make_async_remote_copy, send_sem=send_sems.at[6 + j], recv_sem=recv_sems.at[6 + j],
                                    device_id=(x, y, 1 - c), device_id_type=MESH)
            passes.append(d2d(src_ref=gm_ref.at[peer, mine], dst_ref=gm_ref.at[peer, mine]))
            from_sibling.append(d2d(src_ref=gm_ref.at[peer, other], dst_ref=gm_ref.at[peer, other]))
        for cp in sends:
            cp.start()
        for j in range(N_XY - 1):
            lands[2 * j].wait_recv()
            passes[j].start()
        for j in range(N_XY - 1):
            lands[2 * j + 1].wait_recv()
        for cp in from_sibling:
            cp.wait_recv()
        for cp in sends + passes:
            cp.wait_send()
        for cp in local:
            cp.wait()

    return pl.pallas_call(
        body, name="all_gather_weights",
        in_specs=[_ANY, _ANY], out_specs=[_ANY, _ANY],
        out_shape=[jax.ShapeDtypeStruct((N_XY,) + wm.shape, wm.dtype), jax.ShapeDtypeStruct((N_XY,) + wv.shape, wv.dtype)],
        scratch_shapes=[pltpu.SemaphoreType.DMA((9,)), pltpu.SemaphoreType.DMA((9,)), pltpu.SemaphoreType.DMA((2,))],
    )(wm, wv)


_HBM = pl.BlockSpec(memory_space=pltpu.HBM)
_SEM = pl.BlockSpec(memory_space=pltpu.SEMAPHORE)
_SPLIT_COPY = functools.partial(pltpu.CompilerParams, has_side_effects=pltpu.SideEffectType.DATAFLOW_SIDE_EFFECTING)


def _gather_xy_start(name, buf, after):
    def body(src_ref, land_ref, after_ref, send_sems, recv_sems, src_thru, land_thru, token):
        x, y, c = lax.axis_index("x"), lax.axis_index("y"), lax.axis_index("c")
        for j, (px, py) in enumerate(_xy_peers()):
            pltpu.make_async_remote_copy(src_ref=src_ref, dst_ref=land_ref.at[2 * x + y], send_sem=send_sems.at[j],
                                         recv_sem=recv_sems.at[j], device_id=(px, py, c), device_id_type=MESH).start()
        token[...] = jnp.zeros_like(token)

    n_peers = N_XY - 1
    land = pltpu.with_memory_space_constraint(lax.empty((N_XY,) + buf.shape, buf.dtype), pltpu.HBM)
    return pl.pallas_call(
        body, name=name,
        out_shape=(pltpu.SemaphoreType.DMA((n_peers,)), pltpu.SemaphoreType.DMA((n_peers,)), pltpu.HBM(buf.shape, buf.dtype),
                   pltpu.HBM(land.shape, buf.dtype), jax.ShapeDtypeStruct((SUBLANES, LANES), F32)),
        in_specs=(_HBM, _HBM, _ANY), out_specs=(_SEM, _SEM, _HBM, _HBM, pl.BlockSpec(memory_space=pltpu.VMEM)),
        input_output_aliases={0: 2, 1: 3}, compiler_params=_SPLIT_COPY(),
    )(pltpu.with_memory_space_constraint(buf, pltpu.HBM), land, after)


def _gather_xy_wait(name, send_sems, recv_sems, src_thru, land_thru, after):
    def body(src_ref, land_ref, send_sems, recv_sems, after_ref, src_dead, got_ref):
        c = lax.axis_index("c")
        for j, (px, py) in enumerate(_xy_peers()):
            cp = pltpu.make_async_remote_copy(src_ref=src_ref, dst_ref=land_ref.at[2 * px + py], send_sem=send_sems.at[j],
                                              recv_sem=recv_sems.at[j], device_id=(px, py, c), device_id_type=MESH)
            cp.wait_send()
            cp.wait_recv()

    return pl.pallas_call(
        body, name=name,
        out_shape=(pltpu.HBM(src_thru.shape, src_thru.dtype), pltpu.HBM(land_thru.shape, land_thru.dtype)),
        in_specs=(_HBM, _HBM, _SEM, _SEM, _ANY), out_specs=(_HBM, _HBM),
        input_output_aliases={0: 0, 1: 1}, compiler_params=_SPLIT_COPY(),
    )(src_thru, land_thru, send_sems, recv_sems, after)[1]


def _exchange_xy_start(name, gsend, after):
    def body(src_ref, land_ref, after_ref, send_sems, recv_sems, src_thru, land_thru, token):
        c = lax.axis_index("c")
        for j, (px, py) in enumerate(_xy_peers()):
            pltpu.make_async_remote_copy(src_ref=src_ref.at[2 * px + py], dst_ref=land_ref.at[j], send_sem=send_sems.at[j],
                                         recv_sem=recv_sems.at[j], device_id=(px, py, c), device_id_type=MESH).start()
        token[...] = jnp.zeros_like(token)

    n_peers = N_XY - 1
    land = pltpu.with_memory_space_constraint(lax.empty((n_peers,) + gsend.shape[1:], gsend.dtype), pltpu.HBM)
    return pl.pallas_call(
        body, name=name,
        out_shape=(pltpu.SemaphoreType.DMA((n_peers,)), pltpu.SemaphoreType.DMA((n_peers,)), pltpu.HBM(gsend.shape, gsend.dtype),
                   pltpu.HBM(land.shape, gsend.dtype), jax.ShapeDtypeStruct((SUBLANES, LANES), F32)),
        in_specs=(_HBM, _HBM, _ANY), out_specs=(_SEM, _SEM, _HBM, _HBM, pl.BlockSpec(memory_space=pltpu.VMEM)),
        input_output_aliases={0: 2, 1: 3}, compiler_params=_SPLIT_COPY(),
    )(pltpu.with_memory_space_constraint(gsend, pltpu.HBM), land, after)


def _exchange_xy_wait(name, send_sems, recv_sems, src_thru, land_thru, after):
    def body(src_ref, land_ref, send_sems, recv_sems, after_ref, src_dead, got_ref):
        c = lax.axis_index("c")
        for j, (px, py) in enumerate(_xy_peers()):
            cp = pltpu.make_async_remote_copy(src_ref=src_ref.at[2 * px + py], dst_ref=land_ref.at[j], send_sem=send_sems.at[j],
                                              recv_sem=recv_sems.at[j], device_id=(px, py, c), device_id_type=MESH)
            cp.wait_send()
            cp.wait_recv()

    return pl.pallas_call(
        body, name=name,
        out_shape=(pltpu.HBM(src_thru.shape, src_thru.dtype), pltpu.HBM(land_thru.shape, land_thru.dtype)),
        in_specs=(_HBM, _HBM, _SEM, _SEM, _ANY), out_specs=(_HBM, _HBM),
        input_output_aliases={0: 0, 1: 1}, compiler_params=_SPLIT_COPY(),
    )(src_thru, land_thru, send_sems, recv_sems, after)[1]


def _exchange_xy(gsend):
    def body(gs_ref, recv_ref, send_sems, recv_sems):
        c = lax.axis_index("c")
        sends = []
        for j, (px, py) in enumerate(_xy_peers()):
            sends.append(pltpu.make_async_remote_copy(
                src_ref=gs_ref.at[2 * px + py], dst_ref=recv_ref.at[j], send_sem=send_sems.at[j], recv_sem=recv_sems.at[j],
                device_id=(px, py, c), device_id_type=MESH))
        for cp in sends:
            cp.start()
        for cp in sends:
            cp.wait_recv()
        for cp in sends:
            cp.wait_send()

    return pl.pallas_call(
        body, name="exchange_grads",
        in_specs=[_ANY], out_specs=_ANY,
        out_shape=jax.ShapeDtypeStruct((N_XY - 1,) + gsend.shape[1:], gsend.dtype),
        scratch_shapes=[pltpu.SemaphoreType.DMA((3,)), pltpu.SemaphoreType.DMA((3,))],
    )(gsend)


def _swap_with_sibling(name, part):
    def body(p_ref, got_ref, send_sem, recv_sem):
        x, y, c = lax.axis_index("x"), lax.axis_index("y"), lax.axis_index("c")
        cp = pltpu.make_async_remote_copy(src_ref=p_ref, dst_ref=got_ref, send_sem=send_sem, recv_sem=recv_sem,
                                          device_id=(x, y, 1 - c), device_id_type=MESH)
        cp.start()
        cp.wait_recv()
        cp.wait_send()

    return pl.pallas_call(
        body, name=name,
        in_specs=[_ANY], out_specs=_ANY, out_shape=jax.ShapeDtypeStruct(part.shape, part.dtype),
        scratch_shapes=[pltpu.SemaphoreType.DMA, pltpu.SemaphoreType.DMA],
    )(part)


def _all_gather_all(vec):
    def body(v_ref, out_ref, send_sems, recv_sems, local_sem):
        x, y, c = lax.axis_index("x"), lax.axis_index("y"), lax.axis_index("c")
        me = 4 * x + 2 * y + c
        local = pltpu.make_async_copy(v_ref, out_ref.at[me], local_sem)
        local.start()
        sends, recvs = [], []
        for f in range(1, N_DEV):
            fx, fy, fc = (f >> 2) & 1, (f >> 1) & 1, f & 1
            px = (1 - x) if fx else x
            py = (1 - y) if fy else y
            pc = (1 - c) if fc else c
            mk = functools.partial(pltpu.make_async_remote_copy, src_ref=v_ref, send_sem=send_sems.at[f - 1],
                                   recv_sem=recv_sems.at[f - 1], device_id=(px, py, pc), device_id_type=MESH)
            sends.append(mk(dst_ref=out_ref.at[me]))
            recvs.append(mk(dst_ref=out_ref.at[4 * px + 2 * py + pc]))
        for cp in sends:
            cp.start()
        for cp in recvs:
            cp.wait_recv()
        for cp in sends:
            cp.wait_send()
        local.wait()

    return pl.pallas_call(
        body, name="all_gather_replicated",
        in_specs=[_ANY], out_specs=_ANY, out_shape=jax.ShapeDtypeStruct((N_DEV,) + vec.shape, vec.dtype),
        scratch_shapes=[pltpu.SemaphoreType.DMA((N_DEV - 1,)), pltpu.SemaphoreType.DMA((N_DEV - 1,)), pltpu.SemaphoreType.DMA],
    )(vec)


def _adamw(g, w, m, v):
    m2 = ADAM_B1 * m + (1.0 - ADAM_B1) * g
    v2 = ADAM_B2 * v + (1.0 - ADAM_B2) * g * g
    m_hat = m2 / (1.0 - ADAM_B1 ** ADAM_STEP)
    v_hat = v2 / (1.0 - ADAM_B2 ** ADAM_STEP)
    return -ADAM_LR * (m_hat / (jnp.sqrt(v_hat) + ADAM_EPS) + ADAM_WD * w), m2, v2


def _sum_contributions(name, own, recv):
    def body(rv, cv, pv, nv):
        return [((rv[0] + _f32(rv[1])) + _f32(rv[2])) + _f32(rv[3])], []
    stacked = recv.reshape(-1, PACK_W)
    (part,), _ = _rowwise(name, body, [own] + [(stacked, PACK_W, 0, j * own.shape[0]) for j in range(N_XY - 1)], [],
                          [(PACK_W, F32)])
    return part


def _adamw_sharded(name, part, sib, first, w, m, v, w_first=0, n_rows=None):
    def body(rv, cv, pv, nv):
        g = rv[3] + rv[4]
        return [g, *_adamw(g, rv[0], rv[1], rv[2])], []
    n_rows = w.shape[0] if n_rows is None else n_rows
    rows = [(z, PACK_W, 0, w_first) for z in (w, m, v)] + [(part, PACK_W, 0, first), (sib, PACK_W, 0, first)]
    outs, _ = _rowwise(name, body, rows, [], [(PACK_W, F32)] * 4, tm=256, n_rows=n_rows)
    return outs


def _adamw_replicated(parts, w, m, v):
    def body(rv, cv, pv, nv):
        g = rv[0]
        for i in range(1, N_DEV):
            g = g + rv[i]
        return [g, *_adamw(g, rv[N_DEV], rv[N_DEV + 1], rv[N_DEV + 2])], []
    outs, _ = _rowwise("adamw_replicated", body, [parts[i] for i in range(N_DEV)] + [w, m, v], [], [(PACK_W, F32)] * 4)
    return outs


def _row(v):
    return v.reshape(1, -1).astype(F32)


def _local_step(x3, mem3, target3, fw, late_weights=None, on_grads=None):
    def with_late(stage, after):
        if late_weights is None:
            return fw
        got = late_weights(stage, after)
        return {**fw, **{n: ({**fw.get(n, {}), **v} if isinstance(v, dict) else v) for n, v in got.items()}}

    nseq, seq, d = x3.shape
    mem_len = mem3.shape[1]
    t = nseq * seq
    x0 = x3.reshape(t, d)
    mem2 = mem3.reshape(nseq * mem_len, d)
    target = target3.reshape(t, d)
    ln = fw["ln_gains"]
    gains = [[_row(ln[i, j]) for j in range(6)] for i in range(2)]
    nh = d // RWKV_HEAD
    seg = (jnp.arange(d)[:, None] // RWKV_HEAD == jnp.arange(LANES)[None, :]).astype(BF16)

    w_gate = fw["a_gate_w"][0]
    pa = dict(g_pre=gains[0][0], g_post=gains[0][1], w_in=fw["a_w_in"][0], b_in_y=_row(fw["a_b_in"][0, :d]),
              b_in_u=_row(fw["a_b_in"][0, d:]), conv_w=fw["a_conv_w"][0].astype(F32), conv_b=_row(fw["a_conv_b"][0]),
              w_gate=w_gate, gate_b0=_row(fw["a_gate_b"][0, 0]), gate_b1=_row(fw["a_gate_b"][0, 1]), lam=_row(fw["a_lambda"][0]),
              w_out=fw["a_w_out"][0], b_out=_row(fw["a_b_out"][0]))
    pta = dict(w_in_t=_t(pa["w_in"]), w_out_t=_t(pa["w_out"]))
    mem_g = _row(fw["mem_norm"])

    mem_n = _norm_fwd("mem_norm", mem2, mem_g)
    to_c0, to_m0, to_c1, to_m1, to_loss = ({"gain": gains[0][2]}, {"gain": gains[0][4]}, {"gain": gains[1][2]},
                                           {"gain": gains[1][4]}, {"target": target})
    x1, sv_a = _rglru_fwd(x0, pa, seq, chain=to_c0)
    fw = with_late("cm0", x1)
    x2, sv_c0 = _xattn_fwd("c0", x1, mem_n, gains[0][2], gains[0][3], fw["c_w_q"][0], fw["c_w_kv"][0], fw["c_w_o"][0], seq, mem_len,
                           hn=to_c0["hn"], chain=to_m0)
    x3_, sv_m0 = _mlp_fwd("m0", x2, gains[0][4], gains[0][5], fw["m_w_up"][0], fw["m_w_down"][0], hn=to_m0["hn"])
    fw = with_late("layer1", x3_)
    pb = dict(g_pre=gains[1][0], g_post=gains[1][1], mu=fw["b_mu"][0].astype(F32), w_r=fw["b_w_rkv"][0, 0],
              w_k=fw["b_w_rkv"][0, 1], w_v=fw["b_w_rkv"][0, 2], w0=_row(fw["b_w0"][0]), w1=fw["b_w1"][0], w2=fw["b_w2"][0],
              a0=_row(fw["b_a0"][0]), a1=fw["b_a1"][0], a2=fw["b_a2"][0], g1=fw["b_g1"][0], g2=fw["b_g2"][0],
              k_k=_row(fw["b_k_k"][0]), k_a=_row(fw["b_k_a"][0]), r_k=_row(fw["b_r_k"][0]), gn_g=_row(fw["b_gn_g"][0]),
              gn_b=_row(fw["b_gn_b"][0]), w_o=fw["b_w_o"][0], seg=seg)
    ptb = {k + "_t": _t(pb[k]) for k in ("w_r", "w_k", "w_v", "w_o", "w1", "w2", "a1", "a2", "g1", "g2")}
    x4, sv_b = _rwkv_fwd(x3_, pb, seq, chain=to_c1)
    x5, sv_c1 = _xattn_fwd("c1", x4, mem_n, gains[1][2], gains[1][3], fw["c_w_q"][1], fw["c_w_kv"][1], fw["c_w_o"][1], seq, mem_len,
                           hn=to_c1["hn"], chain=to_m1)
    _, sv_m1 = _mlp_fwd("m1", x5, gains[1][4], gains[1][5], fw["m_w_up"][1], fw["m_w_down"][1], hn=to_m1["hn"], chain=to_loss)
    dx = to_loss["dy"]
    loss_part = 0.5 / d * jnp.sum(to_loss["sq"])

    dx, g_m1 = _mlp_bwd("m1", sv_m1, dx, gains[1][4], gains[1][5], _t(fw["m_w_up"][1]), _t(fw["m_w_down"][1]))
    dx, dmem1, g_c1 = _xattn_bwd("c1", sv_c1, dx, mem_n, gains[1][2], gains[1][3], _t(fw["c_w_q"][1]), _t(fw["c_w_kv"][1]),
                                 _t(fw["c_w_o"][1]), seq, mem_len)
    dx, g_b = _rwkv_bwd(sv_b, dx, pb, ptb, seq)
    grads_b = dict(
        b_mu=g_b["mu"][None], b_w_rkv=jnp.stack([g_b["w_r"], g_b["w_k"], g_b["w_v"]])[None], b_w0=g_b["w0"][None],
        b_w1=g_b["w1"][None], b_w2=g_b["w2"][None], b_a0=g_b["a0"][None], b_a1=g_b["a1"][None], b_a2=g_b["a2"][None],
        b_g1=g_b["g1"][None], b_g2=g_b["g2"][None], b_k_k=g_b["k_k"][None], b_k_a=g_b["k_a"][None],
        b_r_k=g_b["r_k"].reshape(1, nh, RWKV_HEAD), b_gn_g=g_b["gn_g"][None], b_gn_b=g_b["gn_b"][None], b_w_o=g_b["w_o"][None])
    g_m0_pre = gains[0][4]
    if on_grads is not None:
        layer1 = dict(c_w_q=g_c1["w_q"], c_w_kv=g_c1["w_kv"], c_w_o=g_c1["w_o"], m_w_up=g_m1["w_up"], m_w_down=g_m1["w_down"])
        g_m0_pre = g_m0_pre + on_grads("layer1", {**grads_b, **{n: [None, g] for n, g in layer1.items()}})[0, 0]
    dx, g_m0 = _mlp_bwd("m0", sv_m0, dx, g_m0_pre, gains[0][5], _t(fw["m_w_up"][0]), _t(fw["m_w_down"][0]))
    dx, dmem0, g_c0 = _xattn_bwd("c0", sv_c0, dx, mem_n, gains[0][2], gains[0][3], _t(fw["c_w_q"][0]), _t(fw["c_w_kv"][0]),
                                 _t(fw["c_w_o"][0]), seq, mem_len)
    if on_grads is not None:
        cm0 = dict(c_w_q=g_c0["w_q"], c_w_kv=g_c0["w_kv"], c_w_o=g_c0["w_o"], m_w_up=g_m0["w_up"], m_w_down=g_m0["w_down"])
        pa = {**pa, "g_post": pa["g_post"] + on_grads("cm0", {n: [g, None] for n, g in cm0.items()})[0, 0]}
    dx, g_a = _rglru_bwd(sv_a, dx, pa, pta, seq)

    def dmem_body(rv, cv, pv, nv):
        _, dg = _rms_bwd(rv[1] + rv[2], rv[0], cv[0])
        return [], [dg]
    _, (dmem_g,) = _rowwise("mem_norm_grad", dmem_body, [mem2, dmem0, dmem1], [mem_g], [], [d])

    lru_heads = fw["a_gate_w"].shape[2]
    blk = d // lru_heads
    grads = dict(
        ln_gains=jnp.stack([jnp.stack([g_a["g_pre"], g_a["g_post"], g_c0["g_pre"], g_c0["g_post"], g_m0["g_pre"], g_m0["g_post"]]),
                            jnp.stack([g_b["g_pre"], g_b["g_post"], g_c1["g_pre"], g_c1["g_post"], g_m1["g_pre"], g_m1["g_post"]])]),
        mem_norm=dmem_g.sum(0),
        a_conv_w=g_a["conv_w"][None], a_conv_b=g_a["conv_b"][None], a_w_in=g_a["w_in"][None], a_b_in=g_a["b_in"][None],
        a_gate_w=g_a["w_gate"][None],
        a_gate_b=jnp.stack([g_a["gate_b0"], g_a["gate_b1"]]).reshape(1, 2, lru_heads, blk),
        a_lambda=g_a["lam"][None], a_w_out=g_a["w_out"][None], a_b_out=g_a["b_out"][None],
        **grads_b,
        c_w_q=[g_c0["w_q"], g_c1["w_q"]], c_w_kv=[g_c0["w_kv"], g_c1["w_kv"]], c_w_o=[g_c0["w_o"], g_c1["w_o"]],
        m_w_up=[g_m0["w_up"], g_m1["w_up"]], m_w_down=[g_m0["w_down"], g_m1["w_down"]],
    )
    return loss_part, dx.reshape(nseq, seq, d), grads


def kernel(x, mem, ln_gains, mem_norm, a_conv_w, a_conv_b, a_w_in, a_b_in, a_gate_w, a_gate_b, a_lambda, a_w_out, a_b_out, b_mu, b_w_rkv, b_w0, b_w1, b_w2, b_a0, b_a1, b_a2, b_g1, b_g2, b_k_k, b_k_a, b_r_k, b_gn_g, b_gn_b, b_w_o, c_w_q, c_w_kv, c_w_o, m_w_up, m_w_down, loss_target, m_ln_gains, m_mem_norm, m_a_conv_w, m_a_conv_b, m_a_w_in, m_a_b_in, m_a_gate_w, m_a_gate_b, m_a_lambda, m_a_w_out, m_a_b_out, m_b_mu, m_b_w_rkv, m_b_w0, m_b_w1, m_b_w2, m_b_a0, m_b_a1, m_b_a2, m_b_g1, m_b_g2, m_b_k_k, m_b_k_a, m_b_r_k, m_b_gn_g, m_b_gn_b, m_b_w_o, m_c_w_q, m_c_w_kv, m_c_w_o, m_m_w_up, m_m_w_down, v_ln_gains, v_mem_norm, v_a_conv_w, v_a_conv_b, v_a_w_in, v_a_b_in, v_a_gate_w, v_a_gate_b, v_a_lambda, v_a_w_out, v_a_b_out, v_b_mu, v_b_w_rkv, v_b_w0, v_b_w1, v_b_w2, v_b_a0, v_b_a1, v_b_a2, v_b_g1, v_b_g2, v_b_k_k, v_b_k_a, v_b_r_k, v_b_gn_g, v_b_gn_b, v_b_w_o, v_c_w_q, v_c_w_kv, v_c_w_o, v_m_w_up, v_m_w_down):
    given = dict(locals())
    w = {n: given[n] for n in _WEIGHTS}
    mom1 = {n: given["m_" + n] for n in _WEIGHTS}
    mom2 = {n: given["v_" + n] for n in _WEIGHTS}

    me = 2 * lax.axis_index("x") + lax.axis_index("y")
    per_layer = [n for n in _MATRICES if n[0] in "cm"]
    early = [(n, None) for n in _MATRICES if n[0] == "a"]
    late = dict(cm0=[(n, 0) for n in per_layer], layer1=[(n, None) for n in _MATRICES if n[0] == "b"] + [(n, 1) for n in per_layer])
    piece = lambda n, layer: w[n] if layer is None else w[n][layer:layer + 1]

    def gathered(entries, buf):
        shards = [_unpack(buf[s], [piece(n, layer).shape for n, layer in entries]) for s in range(N_XY)]
        out = {}
        for i, (n, layer) in enumerate(entries):
            full = jnp.concatenate([shards[s][i] for s in range(N_XY)], axis=_SHARD_AXIS[n])
            if layer is None:
                out[n] = full
            else:
                out.setdefault(n, {})[layer] = full[0]
        return out

    late_bufs = {stage: _pack([piece(n, layer) for n, layer in entries], BF16) for stage, entries in late.items()}
    gm, gv = _all_gather_xy(_pack([piece(n, layer) for n, layer in early], BF16), _pack([w[n] for n in _VECTORS], F32, SUBLANES))
    in_flight = {}
    token = gv
    for stage in late:
        *in_flight[stage], token = _gather_xy_start("gather_%s_start" % stage, late_bufs[stage], token)
    fw = {n: w[n] for n in _REPLICATED} | gathered(early, gm)
    vec_shards = [_unpack(gv[s], [w[n].shape for n in _VECTORS]) for s in range(N_XY)]
    for i, n in enumerate(_VECTORS):
        fw[n] = jnp.concatenate([vec_shards[s][i] for s in range(N_XY)], axis=_SHARD_AXIS[n])
    fw["ln_gains"] = fw["ln_gains"] + token[0, 0]

    def late_weights(stage, after):
        land = _gather_xy_wait("gather_%s_wait" % stage, *in_flight[stage], after)
        return gathered(late[stage], lax.dynamic_update_index_in_dim(land, late_bufs[stage], me, 0))

    tile_rows = 256
    groups = dict(layer1=[(n, 1 if n in per_layer else None) for n in _SHARDED if n in per_layer or n[0] == "b"],
                  cm0=[(n, 0) for n in _SHARDED if n in per_layer],
                  rest=[(n, None) for n in _SHARDED if n not in per_layer and n[0] != "b"])

    def piece_shape(n, layer):
        return w[n].shape if layer is None else (1,) + w[n].shape[1:]

    def rows_of(n, layer):
        size = 1
        for s_ in piece_shape(n, layer):
            size *= s_
        return size // PACK_W

    def split(entries):
        in_place = [e for e in entries if w[e[0]].shape[-1] == PACK_W and rows_of(*e) % tile_rows == 0 and rows_of(*e) > 0]
        in_place.sort(key=lambda e: -rows_of(*e))
        return in_place, [e for e in entries if e not in in_place]

    def buffers(entries, grads):
        in_place, packed = split(entries)

        def pieces(s):
            out = []
            for n, layer in in_place + packed:
                ax = _SHARD_AXIS[n]
                size = w[n].shape[ax]
                g = grads[n] if layer is None else grads[n][layer]
                out.append(lax.dynamic_slice_in_dim(g, s * size, size, axis=ax if layer is None else ax - 1))
            return out
        return jnp.stack([_pack(pieces(s), BF16) for s in range(N_XY)]), _pack(pieces(me), F32)

    def update(tag, entries, own, recv):
        in_place, packed = split(entries)
        part = _sum_contributions("sum_grads_" + tag, own, recv)
        sib = _swap_with_sibling("swap_sibling_" + tag, part)
        out = {}
        first = 0
        for n, layer in in_place:
            flat = [src[n].reshape(-1, PACK_W) for src in (w, mom1, mom2)]
            rows = rows_of(n, layer)
            res = _adamw_sharded("adamw_%s_%s" % (n, tag), part, sib, first, *flat, w_first=(layer or 0) * rows, n_rows=rows)
            out[(n, layer)] = [o.reshape(piece_shape(n, layer)) for o in res]
            first += rows
        take = lambda src, n, layer: src[n] if layer is None else src[n][layer:layer + 1]
        flat = [_pack([take(src, n, layer) for n, layer in packed], F32) for src in (w, mom1, mom2)]
        assert first + flat[0].shape[0] == part.shape[0], (first, flat[0].shape, part.shape)
        res = _adamw_sharded("adamw_packed_" + tag, part, sib, first, *flat)
        tail = [_unpack(o, [piece_shape(n, layer) for n, layer in packed]) for o in res]
        for i, e in enumerate(packed):
            out[e] = [tail[kind][i] for kind in range(4)]
        return out

    sent = {}

    def on_grads(stage, grads_so_far):
        gsend, own = buffers(groups[stage], grads_so_far)
        *handles, token = _exchange_xy_start("exchange_%s_start" % stage, gsend, own)
        sent[stage] = (own, handles)
        return token

    loss_part, grad_x, grads = _local_step(x, mem, loss_target, fw, late_weights, on_grads)

    gsend, own = buffers(groups["rest"], grads)
    out = update("rest", groups["rest"], own, _exchange_xy(gsend))
    for stage, (own, handles) in sent.items():
        out |= update(stage, groups[stage], own, _exchange_xy_wait("exchange_%s_wait" % stage, *handles, grad_x))
    sharded_out = [[] for _ in range(4)]
    for n in _SHARDED:
        for kind in range(4):
            if n in per_layer:
                sharded_out[kind].append(jnp.concatenate([out[(n, 0)][kind], out[(n, 1)][kind]], axis=0))
            else:
                sharded_out[kind].append(out[(n, None)][kind])

    small = _pack([grads[n] for n in _REPLICATED] + [loss_part.reshape(1)], F32, SUBLANES)
    parts = _all_gather_all(small)
    zero = jnp.zeros((1,), F32)
    flat = [_pack([src[n] for n in _REPLICATED] + [zero], F32, SUBLANES) for src in (w, mom1, mom2)]
    repl_out = [_unpack(o, [w[n].shape for n in _REPLICATED] + [(1,)]) for o in _adamw_replicated(parts, *flat)]
    loss = repl_out[0][-1][0]

    result = [loss, grad_x]
    for kind in range(4):
        by_name = dict(zip(_SHARDED, sharded_out[kind])) | dict(zip(_REPLICATED, repl_out[kind][:-1]))
        result += [by_name[n] for n in _WEIGHTS]
    return tuple(result)
```

```python
import functools

import jax
import jax.numpy as jnp
from jax import lax
from jax.experimental import pallas as pl
from jax.experimental.pallas import tpu as pltpu

F32 = jnp.float32
BF16 = jnp.bfloat16
MESH = pl.DeviceIdType.MESH

LANES = 128
SUBLANES = 8
VMEM_LIMIT_BYTES = 48 * 1024 * 1024

RMS_EPS = 1e-6
LRU_C = 8.0
LRU_HEADS = 4
CONV_WIDTH = 4
RWKV_HEAD = 64
RWKV_GN_EPS = 64e-5
MEM_HEADS = 4
ADAM_LR = 0.001
ADAM_B1 = 0.9
ADAM_B2 = 0.999
ADAM_EPS = 1e-08
ADAM_WD = 0.01
ADAM_STEP = 10
WKV_CHUNK = 64

_PARAMS = functools.partial(pltpu.CompilerParams, vmem_limit_bytes=VMEM_LIMIT_BYTES)


def _tile(n, want):
    if n <= want:
        return n
    t = want
    while t >= SUBLANES:
        if n % t == 0 and t % SUBLANES == 0:
            return t
        t -= SUBLANES
    return n


def _fold8(v):
    tm, d = v.shape
    if tm == SUBLANES:
        return v
    return jnp.sum(v.reshape(tm // SUBLANES, SUBLANES, d), axis=0)


def _rowwise(name, body, rows, consts=(), out_rows=(), out_accs=(), prev=(), nxt=(), tm=512, seq=None, n_rows=None):
    rows = [r if isinstance(r, tuple) else (r, r.shape[1], 0) for r in rows]
    rows = [r if len(r) == 4 else r + (0,) for r in rows]
    t = rows[0][0].shape[0] if n_rows is None else n_rows
    tm = _tile(t, tm)
    if seq is not None:
        tm = _tile(seq, tm)
    nblk = t // tm
    nrow, ncst, nprev, nnxt = len(rows), len(consts), len(prev), len(nxt)
    nor, noa = len(out_rows), len(out_accs)
    hb = tm // SUBLANES

    def kern(*refs):
        i = pl.program_id(0)
        rv = [r[...] for r in refs[:nrow]]
        cv = [c[...] for c in refs[nrow:nrow + ncst]]
        o = nrow + ncst
        pv = []
        for j in range(nprev):
            at_start = (i * tm) % seq == 0
            h = refs[o + j][...]
            pv.append(jnp.where(at_start, jnp.zeros_like(h), h))
        o += nprev
        nv = []
        for j in range(nnxt):
            at_end = ((i + 1) * tm) % seq == 0
            h = refs[o + j][...]
            nv.append(jnp.where(at_end, jnp.zeros_like(h), h))
        o += nnxt
        outs, accs = body(rv, cv, pv, nv)
        for j in range(nor):
            refs[o + j][...] = outs[j].astype(refs[o + j].dtype)
        o += nor
        if noa:
            @pl.when(i == 0)
            def _():
                for j in range(noa):
                    refs[o + j][...] = jnp.zeros_like(refs[o + j])
            for j in range(noa):
                refs[o + j][...] += _fold8(accs[j].astype(F32))

    assert all(first % tm == 0 for (_, _, _, first) in rows), name
    in_specs = [pl.BlockSpec((tm, w), functools.partial(lambda i, c, o: (i + o, c), c=cb, o=first // tm))
                for (_, w, cb, first) in rows]
    in_specs += [pl.BlockSpec(c.shape, lambda i: (0, 0)) for c in consts]
    in_specs += [pl.BlockSpec((SUBLANES, rows[j][1]),
                              functools.partial(lambda i, c: (jnp.maximum(i * hb - 1, 0), c), c=rows[j][2])) for j in prev]
    in_specs += [pl.BlockSpec((SUBLANES, rows[j][1]),
                              functools.partial(lambda i, c: (jnp.minimum((i + 1) * hb, t // SUBLANES - 1), c), c=rows[j][2]))
                 for j in nxt]
    out_shape = [jax.ShapeDtypeStruct((t, w), dt) for (w, dt) in out_rows]
    out_shape += [jax.ShapeDtypeStruct((SUBLANES, w), F32) for w in out_accs]
    out_specs = [pl.BlockSpec((tm, w), lambda i: (i, 0)) for (w, _) in out_rows]
    out_specs += [pl.BlockSpec((SUBLANES, w), lambda i: (0, 0)) for w in out_accs]
    args = [r[0] for r in rows] + list(consts) + [rows[j][0] for j in prev] + [rows[j][0] for j in nxt]
    res = pl.pallas_call(
        kern, name=name, grid=(nblk,), in_specs=in_specs, out_specs=out_specs, out_shape=out_shape,
        compiler_params=_PARAMS(dimension_semantics=("arbitrary",)),
    )(*args)
    return list(res[:nor]), list(res[nor:])


def _shift_down(x, halo, k):
    rolled = pltpu.roll(x, k, 0)
    row = lax.broadcasted_iota(jnp.int32, (SUBLANES, x.shape[1]), 0)
    first = jnp.where(row < k, pltpu.roll(halo, k, 0), rolled[:SUBLANES])
    if x.shape[0] == SUBLANES:
        return first
    return jnp.concatenate([first, rolled[SUBLANES:]], axis=0)


def _shift_up(x, halo, k):
    n = x.shape[0]
    rolled = pltpu.roll(x, n - k, 0)
    row = lax.broadcasted_iota(jnp.int32, (SUBLANES, x.shape[1]), 0)
    last = jnp.where(row >= SUBLANES - k, pltpu.roll(halo, SUBLANES - k, 0), rolled[n - SUBLANES:])
    if n == SUBLANES:
        return last
    return jnp.concatenate([rolled[:n - SUBLANES], last], axis=0)


class _Transposed:
    def __init__(self, w):
        self.w = w


def _t(w):
    return _Transposed(w)


def _mm(name, a, b, out_dtype=F32, trans_a=False, tm=1024, tn=1024, tk=1024, epilogue=None, extra=None):
    trans_b = isinstance(b, _Transposed)
    assert not (trans_a and trans_b)
    if trans_b:
        b = b.w
    if trans_a:
        kdim, m = a.shape
    else:
        m, kdim = a.shape
    n = b.shape[0] if trans_b else b.shape[1]
    assert b.shape[1 if trans_b else 0] == kdim, (name, a.shape, b.shape)
    tm, tn, tk = _tile(m, tm), _tile(n, tn), _tile(kdim, tk)
    nk = kdim // tk
    dims = (((0,), (0,)), ((), ())) if trans_a else (((1,), (1 if trans_b else 0,)), ((), ()))

    n_in = 2 if extra is None else 3

    def kern(*refs):
        a_ref, b_ref, o_ref, acc = refs[0], refs[1], refs[n_in], refs[n_in + 1:]

        def store(res):
            if epilogue is not None:
                res = epilogue(res) if extra is None else epilogue(res, refs[2][...])
            o_ref[...] = res.astype(o_ref.dtype)

        part = lax.dot_general(a_ref[...].astype(BF16), b_ref[...].astype(BF16), dims, preferred_element_type=F32)
        if nk == 1:
            store(part)
        else:
            k = pl.program_id(2)

            @pl.when(k == 0)
            def _():
                acc[0][...] = part

            @pl.when(k > 0)
            def _():
                acc[0][...] += part

            @pl.when(k == nk - 1)
            def _():
                store(acc[0][...])

    a_spec = pl.BlockSpec((tk, tm), lambda i, j, k: (k, i)) if trans_a else pl.BlockSpec((tm, tk), lambda i, j, k: (i, k))
    b_spec = pl.BlockSpec((tn, tk), lambda i, j, k: (j, k)) if trans_b else pl.BlockSpec((tk, tn), lambda i, j, k: (k, j))
    out_spec = pl.BlockSpec((tm, tn), lambda i, j, k: (i, j))
    return pl.pallas_call(
        kern, name=name, grid=(m // tm, n // tn, nk),
        in_specs=[a_spec, b_spec] + ([] if extra is None else [out_spec]),
        out_specs=out_spec,
        out_shape=jax.ShapeDtypeStruct((m, n), out_dtype),
        scratch_shapes=[] if nk == 1 else [pltpu.VMEM((tm, tn), F32)],
        compiler_params=_PARAMS(dimension_semantics=("parallel", "parallel", "arbitrary")),
    )(*((a, b) if extra is None else (a, b, extra)))


def _gates_mm(name, x, gate_w, mode, dz=None, tm=1024):
    _, nh, blk, _ = gate_w.shape
    d = nh * blk
    t = x.shape[0]
    tm = _tile(t, tm)
    cols = lambda g, h: slice(g * d + h * blk, g * d + (h + 1) * blk)

    def fwd(x_ref, w_ref, o_ref):
        for h in range(nh):
            xh = x_ref[:, cols(0, h)].astype(BF16)
            for g in range(2):
                o_ref[:, cols(g, h)] = jnp.dot(xh, w_ref[g, h].astype(BF16), preferred_element_type=F32)

    def dx(dz_ref, w_ref, o_ref):
        for h in range(nh):
            o_ref[:, cols(0, h)] = sum(lax.dot_general(dz_ref[:, cols(g, h)].astype(BF16), w_ref[g, h].astype(BF16), _NT,
                                                       preferred_element_type=F32) for g in range(2))

    def dw(x_ref, dz_ref, o_ref):
        @pl.when(pl.program_id(0) == 0)
        def _():
            o_ref[...] = jnp.zeros_like(o_ref)

        for h in range(nh):
            xh = x_ref[:, cols(0, h)].astype(BF16)
            for g in range(2):
                o_ref[g, h] += lax.dot_general(xh, dz_ref[:, cols(g, h)].astype(BF16), _TN, preferred_element_type=F32)

    row = lambda width: pl.BlockSpec((tm, width), lambda i: (i, 0))
    whole = pl.BlockSpec(gate_w.shape, lambda i: (0, 0, 0, 0))
    kern, args, in_specs, out_spec, out_shape, sem = {
        "fwd": (fwd, (x, gate_w), [row(d), whole], row(2 * d), jax.ShapeDtypeStruct((t, 2 * d), F32), "parallel"),
        "dx": (dx, (x, gate_w), [row(2 * d), whole], row(d), jax.ShapeDtypeStruct((t, d), F32), "parallel"),
        "dw": (dw, (x, dz), [row(d), row(2 * d)], whole, jax.ShapeDtypeStruct(gate_w.shape, F32), "arbitrary"),
    }[mode]
    return pl.pallas_call(kern, name=name, grid=(t // tm,), in_specs=in_specs, out_specs=out_spec, out_shape=out_shape,
                          compiler_params=_PARAMS(dimension_semantics=(sem,)))(*args)


def _scan(name, a, b, seq, reverse=False, tm=256):
    t, d = a.shape
    tm = _tile(seq, tm)
    nblk = t // tm
    ntile = tm // SUBLANES

    def kern(a_ref, b_ref, h_ref, carry_h, carry_a):
        i = pl.program_id(0)
        blk = (nblk - 1 - i) if reverse else i
        edge = (((blk + 1) * tm) % seq == 0) if reverse else ((blk * tm) % seq == 0)

        @pl.when(edge)
        def _():
            carry_h[...] = jnp.zeros_like(carry_h)
            carry_a[...] = jnp.zeros_like(carry_a)

        def tile_step(j, c):
            jj = (ntile - 1 - j) if reverse else j
            rows = pl.ds(pl.multiple_of(jj * SUBLANES, SUBLANES), SUBLANES)
            a8 = a_ref[rows, :]
            b8 = b_ref[rows, :]
            h, an = c
            out = [None] * SUBLANES
            order = range(SUBLANES - 1, -1, -1) if reverse else range(SUBLANES)
            for r in order:
                if reverse:
                    h = b8[r:r + 1, :] + an * h
                    an = a8[r:r + 1, :]
                else:
                    h = a8[r:r + 1, :] * h + b8[r:r + 1, :]
                out[r] = h
            h_ref[rows, :] = jnp.concatenate(out, axis=0)
            return (h, an)

        h, an = lax.fori_loop(0, ntile, tile_step, (carry_h[...], carry_a[...]))
        carry_h[...] = h
        carry_a[...] = an

    idx = (lambda i: (nblk - 1 - i, 0)) if reverse else (lambda i: (i, 0))
    return pl.pallas_call(
        kern, name=name, grid=(nblk,),
        in_specs=[pl.BlockSpec((tm, d), idx), pl.BlockSpec((tm, d), idx)],
        out_specs=pl.BlockSpec((tm, d), idx),
        out_shape=jax.ShapeDtypeStruct((t, d), F32),
        scratch_shapes=[pltpu.VMEM((1, d), F32), pltpu.VMEM((1, d), F32)],
        compiler_params=_PARAMS(dimension_semantics=("arbitrary",)),
    )(a, b)


_NN = (((1,), (0,)), ((), ()))
_NT = (((1,), (1,)), ((), ()))
_TN = (((0,), (0,)), ((), ()))


def _dot1(a, b, dims):
    return lax.dot_general(a.astype(BF16), b.astype(BF16), dims, preferred_element_type=F32)


def _dot3(a, b, dims):
    a_hi, b_hi = a.astype(BF16), b.astype(BF16)
    a_lo, b_lo = (a - a_hi.astype(F32)).astype(BF16), (b - b_hi.astype(F32)).astype(BF16)
    dg = lambda p, q: lax.dot_general(p, q, dims, preferred_element_type=F32)
    return dg(a_hi, b_hi) + (dg(a_hi, b_lo) + dg(a_lo, b_hi))


def _make_bmm(dot):
    def make(dims, da_rule, db_rule):
        @jax.custom_vjp
        def f(a, b):
            return dot(a, b, dims)

        def fwd(a, b):
            return dot(a, b, dims), (a, b)

        def bwd(res, g):
            a, b = res
            return da_rule(a, b, g), db_rule(a, b, g)

        f.defvjp(fwd, bwd)
        return f

    return dict(nn=make(_NN, lambda a, b, g: dot(g, b, _NT), lambda a, b, g: dot(a, g, _TN)),
                nt=make(_NT, lambda a, b, g: dot(g, b, _NN), lambda a, b, g: dot(g, a, _TN)),
                tn=make(_TN, lambda a, b, g: dot(b, g, _NT), lambda a, b, g: dot(a, g, _NN)))


_BMM = {1: _make_bmm(_dot1), 3: _make_bmm(_dot3)}
_WKV_PASSES = dict(pair=1, read=1, inv=3, apply=1, write=1)


def _running_sum(x, reverse):
    c = x.shape[0]
    row = lax.broadcasted_iota(jnp.int32, x.shape, 0)
    k = 1
    while k < c:
        if reverse:
            x = x + jnp.where(row < c - k, pltpu.roll(x, c - k, 0), 0.0)
        else:
            x = x + jnp.where(row >= k, pltpu.roll(x, k, 0), 0.0)
        k *= 2
    return x


@jax.custom_vjp
def _cumsum_rows(x):
    return _running_sum(x, False)


_cumsum_rows.defvjp(lambda x: (_running_sum(x, False), None), lambda _, g: (_running_sum(g, True),))


@jax.custom_vjp
def _unit_lower_inverse(nl):
    c = nl[0].shape[0]
    mm = _BMM[_WKV_PASSES["inv"]]["nn"]
    eye = jnp.where(lax.broadcasted_iota(jnp.int32, (c, c), 0) == lax.broadcasted_iota(jnp.int32, (c, c), 1), 1.0, 0.0)
    inv = [eye + z for z in nl]
    p = nl
    for _ in range(c.bit_length() - 2):
        p = [mm(z, z) for z in p]
        inv = [i_ + mm(p_, i_) for i_, p_ in zip(inv, p)]
    return inv


def _unit_lower_inverse_fwd(nl):
    inv = _unit_lower_inverse(nl)
    return inv, inv


def _unit_lower_inverse_bwd(inv, g):
    mm = _BMM[_WKV_PASSES["inv"]]
    left = [mm["tn"](x, g_) for x, g_ in zip(inv, g)]
    return ([mm["nt"](l_, x) for l_, x in zip(left, inv)],)


_unit_lower_inverse.defvjp(_unit_lower_inverse_fwd, _unit_lower_inverse_bwd)


@jax.custom_vjp
def _kept_inverse(nl, inv):
    return inv


_kept_inverse.defvjp(lambda nl, inv: (inv, inv),
                     lambda inv, g: (_unit_lower_inverse_bwd(inv, g)[0], [jnp.zeros_like(x) for x in inv]))


def _wkv_chunk(r, lw, k, v, a, b, s0, kept_inv=None):
    c = r[0].shape[0]
    ti = lax.broadcasted_iota(jnp.int32, (c, 2 * c), 0)
    tj = lax.broadcasted_iota(jnp.int32, (c, 2 * c), 1)
    right = tj >= c
    tau = jnp.where(right, tj - c, tj)
    strict_left = jnp.logical_and(jnp.logical_not(right), tau < ti)[:, :c]
    strict_right = jnp.logical_and(right, tau < ti)
    incl = tau <= ti
    last = lax.broadcasted_iota(jnp.int32, r[0].shape, 0) == c - 1
    each = lambda f, *ls: [f(*z) for z in zip(*ls)]
    rows2 = lambda x, y: jnp.concatenate([x, y], axis=0)
    pair, read, inv_, apply_, write = (_BMM[_WKV_PASSES[role]] for role in ("pair", "read", "inv", "apply", "write"))
    cum = each(_cumsum_rows, lw)
    w_incl = each(jnp.exp, cum)
    w_inv = each(lambda z: jnp.exp(-z), cum)
    at = each(lambda a_, c_, l_: a_ * jnp.exp(c_ - l_), a, cum, lw)
    ar = each(rows2, at, each(jnp.multiply, r, w_incl))
    bk = each(rows2, each(jnp.multiply, b, w_inv), each(jnp.multiply, k, w_inv))
    pp = each(pair["nt"], ar, bk)
    sr = each(read["nt"], ar, s0)
    nl = [jnp.where(strict_left, z[:c, :c], 0.0) for z in pp]
    zero_v = each(lambda v_: rows2(jnp.zeros_like(v_), v_), v)
    rhs = each(lambda s, z, zv: s[:c] + apply_["nn"](jnp.where(strict_right, z[:c], 0.0), zv), sr, pp, zero_v)
    inv = _unit_lower_inverse(nl) if kept_inv is None else _kept_inverse(nl, kept_inv)
    ut = each(inv_["nn"], inv, rhs)
    uv = each(rows2, ut, v)
    y = each(lambda s, z, uv_: s[c:] + apply_["nn"](jnp.where(incl, z[c:], 0.0), uv_), sr, pp, uv)
    w_end = each(lambda z: jnp.exp(jnp.sum(jnp.where(last, z, 0.0), axis=0, keepdims=True)), cum)
    s1 = each(lambda s, uv_, bk_, w_: (s + write["tn"](uv_, bk_)) * w_, s0, uv, bk, w_end)
    return y, s1, inv


def _wkv_fwd(r, lw, k, v, a, b, seq, hb=16):
    t, d = r.shape
    n = RWKV_HEAD
    nh = d // n
    hb = min(hb, nh)
    chunk = min(WKV_CHUNK, seq)
    ncs = seq // chunk

    def kern(r_ref, lw_ref, k_ref, v_ref, a_ref, b_ref, y_ref, st_ref, inv_ref, s_scr):
        @pl.when(pl.program_id(2) == 0)
        def _():
            s_scr[...] = jnp.zeros_like(s_scr)

        heads = lambda ref: [ref[:, h * n:(h + 1) * n] for h in range(hb)]
        s0 = [s_scr[h] for h in range(hb)]
        y, s1, inv = _wkv_chunk(heads(r_ref), heads(lw_ref), heads(k_ref), heads(v_ref), heads(a_ref), heads(b_ref), s0)
        for h in range(hb):
            st_ref[0, h] = s0[h]
            inv_ref[0, h] = inv[h]
            y_ref[:, h * n:(h + 1) * n] = y[h]
            s_scr[h] = s1[h]

    vec = pl.BlockSpec((chunk, hb * n), lambda bb, g, c: (bb * ncs + c, g))
    per_chunk = lambda rows: pl.BlockSpec((1, hb, rows, rows), lambda bb, g, c: (bb * ncs + c, g, 0, 0))
    return pl.pallas_call(
        kern, name="wkv_fwd", grid=(t // seq, nh // hb, ncs), in_specs=[vec] * 6,
        out_specs=[vec, per_chunk(n), per_chunk(chunk)],
        out_shape=[jax.ShapeDtypeStruct((t, d), F32), jax.ShapeDtypeStruct((t // chunk, nh, n, n), F32),
                   jax.ShapeDtypeStruct((t // chunk, nh, chunk, chunk), F32)],
        scratch_shapes=[pltpu.VMEM((hb, n, n), F32)],
        compiler_params=_PARAMS(dimension_semantics=("parallel", "parallel", "arbitrary")),
    )(r, lw, k, v, a, b)


def _wkv_bwd(r, lw, k, v, a, b, st, inv, dy, seq, hb=16):
    t, d = r.shape
    n = RWKV_HEAD
    nh = d // n
    hb = min(hb, nh)
    chunk = min(WKV_CHUNK, seq)
    ncs = seq // chunk

    def kern(r_ref, lw_ref, k_ref, v_ref, a_ref, b_ref, st_ref, inv_ref, dy_ref,
             dr_ref, dlw_ref, dk_ref, dv_ref, da_ref, db_ref, ds_scr):
        @pl.when(pl.program_id(2) == 0)
        def _():
            ds_scr[...] = jnp.zeros_like(ds_scr)

        heads = lambda ref: [ref[:, h * n:(h + 1) * n] for h in range(hb)]
        kept = [inv_ref[0, h] for h in range(hb)]
        _, vjp = jax.vjp(lambda *args: _wkv_chunk(*args, kept_inv=kept)[:2],
                         heads(r_ref), heads(lw_ref), heads(k_ref), heads(v_ref), heads(a_ref), heads(b_ref),
                         [st_ref[0, h] for h in range(hb)])
        grads = vjp((heads(dy_ref), [ds_scr[h] for h in range(hb)]))
        for h in range(hb):
            for ref, g in zip((dr_ref, dlw_ref, dk_ref, dv_ref, da_ref, db_ref), grads[:6]):
                ref[:, h * n:(h + 1) * n] = g[h]
            ds_scr[h] = grads[6][h]

    vec = pl.BlockSpec((chunk, hb * n), lambda bb, g, c: (bb * ncs + ncs - 1 - c, g))
    per_chunk = lambda rows: pl.BlockSpec((1, hb, rows, rows), lambda bb, g, c: (bb * ncs + ncs - 1 - c, g, 0, 0))
    return pl.pallas_call(
        kern, name="wkv_bwd", grid=(t // seq, nh // hb, ncs),
        in_specs=[vec] * 6 + [per_chunk(n), per_chunk(chunk), vec],
        out_specs=[vec] * 6, out_shape=[jax.ShapeDtypeStruct((t, d), F32)] * 6,
        scratch_shapes=[pltpu.VMEM((hb, n, n), F32)],
        compiler_params=_PARAMS(dimension_semantics=("parallel", "parallel", "arbitrary")),
    )(r, lw, k, v, a, b, st, inv, dy)


def _softmax_rows(s):
    e = jnp.exp(s - jnp.max(s, axis=-1, keepdims=True))
    return e / jnp.sum(e, axis=-1, keepdims=True)


def _attn_fwd(q, kv, seq, mem_len, tq=1024):
    t, d = q.shape
    dh = d // MEM_HEADS
    scale = dh ** -0.5
    tq = _tile(seq, tq)
    nq = seq // tq

    def kern(q_ref, kv_ref, o_ref):
        for h in range(MEM_HEADS):
            cols = slice(h * dh, (h + 1) * dh)
            vcols = slice(d + h * dh, d + (h + 1) * dh)
            s = lax.dot_general(q_ref[:, cols], kv_ref[:, cols], _NT, preferred_element_type=F32) * scale
            p = _softmax_rows(s)
            o_ref[:, cols] = jnp.dot(p.astype(BF16), kv_ref[:, vcols], preferred_element_type=F32).astype(o_ref.dtype)

    return pl.pallas_call(
        kern, name="attn_fwd", grid=(t // seq, nq),
        in_specs=[pl.BlockSpec((tq, d), lambda b, i: (b * nq + i, 0)), pl.BlockSpec((mem_len, 2 * d), lambda b, i: (b, 0))],
        out_specs=pl.BlockSpec((tq, d), lambda b, i: (b * nq + i, 0)),
        out_shape=jax.ShapeDtypeStruct((t, d), BF16),
        compiler_params=_PARAMS(dimension_semantics=("parallel", "parallel")),
    )(q, kv)


def _attn_bwd(q, kv, do, seq, mem_len, tq=1024):
    t, d = q.shape
    dh = d // MEM_HEADS
    scale = dh ** -0.5
    tq = _tile(seq, tq)
    nq = seq // tq

    def kern(q_ref, kv_ref, do_ref, dq_ref, dkv_ref):
        @pl.when(pl.program_id(1) == 0)
        def _():
            dkv_ref[...] = jnp.zeros_like(dkv_ref)

        for h in range(MEM_HEADS):
            cols = slice(h * dh, (h + 1) * dh)
            vcols = slice(d + h * dh, d + (h + 1) * dh)
            qh, kh, vh, doh = q_ref[:, cols], kv_ref[:, cols], kv_ref[:, vcols], do_ref[:, cols]
            p = _softmax_rows(lax.dot_general(qh, kh, _NT, preferred_element_type=F32) * scale)
            dp = lax.dot_general(doh, vh, _NT, preferred_element_type=F32)
            ds = (p * (dp - jnp.sum(p * dp, axis=-1, keepdims=True)) * scale).astype(BF16)
            dq_ref[:, cols] = jnp.dot(ds, kh, preferred_element_type=F32).astype(dq_ref.dtype)
            dkv_ref[:, cols] += lax.dot_general(ds, qh, _TN, preferred_element_type=F32)
            dkv_ref[:, vcols] += lax.dot_general(p.astype(BF16), doh, _TN, preferred_element_type=F32)

    return pl.pallas_call(
        kern, name="attn_bwd", grid=(t // seq, nq),
        in_specs=[pl.BlockSpec((tq, d), lambda b, i: (b * nq + i, 0)), pl.BlockSpec((mem_len, 2 * d), lambda b, i: (b, 0)),
                  pl.BlockSpec((tq, d), lambda b, i: (b * nq + i, 0))],
        out_specs=[pl.BlockSpec((tq, d), lambda b, i: (b * nq + i, 0)), pl.BlockSpec((mem_len, 2 * d), lambda b, i: (b, 0))],
        out_shape=[jax.ShapeDtypeStruct((t, d), BF16), jax.ShapeDtypeStruct(kv.shape, F32)],
        compiler_params=_PARAMS(dimension_semantics=("parallel", "arbitrary")),
    )(q, kv, do)


def _rstd(x):
    return lax.rsqrt(jnp.mean(x * x, axis=-1, keepdims=True) + RMS_EPS)


def _rms(x, g):
    return x * _rstd(x) * g


def _rms_bwd(dy, x, g):
    rstd = _rstd(x)
    xhat = x * rstd
    dxhat = dy * g
    return rstd * (dxhat - xhat * jnp.mean(dxhat * xhat, axis=-1, keepdims=True)), dy * xhat


def _softplus(x):
    return jnp.maximum(x, 0.0) + jnp.log1p(jnp.exp(-jnp.abs(x)))


def _one_minus_exp(x):
    series = -x * (1.0 + x * (0.5 + x * (1.0 / 6.0 + x * (1.0 / 24.0 + x * (1.0 / 120.0)))))
    return jnp.where(x > -0.05, series, 1.0 - jnp.exp(x))


_GELU_C = 0.7978845608028654
_GELU_K = 0.044715


def _gelu(x):
    return 0.5 * x * (1.0 + jnp.tanh(_GELU_C * (x + _GELU_K * x * x * x)))


def _gelu_grad(x):
    th = jnp.tanh(_GELU_C * (x + _GELU_K * x * x * x))
    return 0.5 * (1.0 + th) + 0.5 * x * (1.0 - th * th) * _GELU_C * (1.0 + 3.0 * _GELU_K * x * x)


def _seg_sum(x, seg):
    def two_terms(v, dims):
        hi = v.astype(BF16)
        lo = (v - hi.astype(F32)).astype(BF16)
        return (lax.dot_general(hi, seg, dims, preferred_element_type=F32)
                + lax.dot_general(lo, seg, dims, preferred_element_type=F32))
    return two_terms(two_terms(x, _NN), _NT)


def _f32(v):
    return v.astype(F32)


def _norm_fwd(name, x, g, dtype=BF16):
    (hn,), _ = _rowwise(name, lambda rv, cv, pv, nv: ([_rms(_f32(rv[0]), cv[0])], []), [x], [g], [(x.shape[1], dtype)])
    return hn


def _resid_norm_fwd(name, x, t, g, bias=None, chain=None):
    d = x.shape[1]
    chain = {} if chain is None else chain

    def body(rv, cv, pv, nv):
        tt = rv[1] if bias is None else rv[1] + cv[1]
        y = rv[0] + _rms(tt, cv[0])
        if "gain" in chain:
            return [y, _rms(y, cv[-1])], []
        if "target" in chain:
            err = y - rv[2]
            return [err * (1.0 / d)], [err * err]
        return [y], []
    consts = [g] if bias is None else [g, bias]
    if "gain" in chain:
        (y, chain["hn"]), _ = _rowwise(name, body, [x, t], consts + [chain["gain"]], [(d, F32), (d, BF16)])
        return y
    if "target" in chain:
        (chain["dy"],), (chain["sq"],) = _rowwise(name, body, [x, t, chain["target"]], consts, [(d, F32)], [d])
        return None
    (y,), _ = _rowwise(name, body, [x, t], consts, [(d, F32)])
    return y


def _resid_norm_bwd(name, dxn, t, g, bias=None):
    def body(rv, cv, pv, nv):
        tt = rv[1] if bias is None else rv[1] + cv[1]
        dt, dg = _rms_bwd(rv[0], tt, cv[0])
        return [dt], [dg, dt]
    d = t.shape[1]
    (dt,), (dg, db) = _rowwise(name, body, [dxn, t], [g] if bias is None else [g, bias], [(d, BF16)], [d, d])
    return dt, dg.sum(0), db.sum(0)


def _prenorm_bwd(name, dxn, dhn, x, g):
    def body(rv, cv, pv, nv):
        dx, dg = _rms_bwd(_f32(rv[1]), rv[2], cv[0])
        return [rv[0] + dx], [dg]
    d = x.shape[1]
    (dx,), (dg,) = _rowwise(name, body, [dxn, dhn, x], [g], [(d, F32)], [d])
    return dx, dg.sum(0)


def _mlp_fwd(tag, x, g_pre, g_post, w_up, w_down, hn=None, chain=None):
    hn = _norm_fwd(tag + "_norm", x, g_pre) if hn is None else hn
    act = _mm(tag + "_up", hn, w_up, out_dtype=BF16, epilogue=lambda up: jnp.square(jnp.maximum(up, 0.0)))
    m = _mm(tag + "_down", act, w_down)
    y = _resid_norm_fwd(tag + "_res", x, m, g_post, chain=chain)
    return y, (x, hn, act, m)


def _mlp_bwd(tag, saved, dy, g_pre, g_post, w_up_t, w_down_t):
    x, hn, act, m = saved
    dm, dg_post, _ = _resid_norm_bwd(tag + "_dres", dy, m, g_post)
    dup = _mm(tag + "_dup", dm, w_down_t, out_dtype=BF16, extra=act,
              epilogue=lambda dact, act_: dact * 2.0 * jnp.sqrt(_f32(act_)))
    dw_down = _mm(tag + "_dwdown", act, dm, trans_a=True)
    dw_up = _mm(tag + "_dwup", hn, dup, trans_a=True)
    dhn = _mm(tag + "_dhn", dup, w_up_t)
    dx, dg_pre = _prenorm_bwd(tag + "_dnorm", dy, dhn, x, g_pre)
    return dx, dict(g_pre=dg_pre, g_post=dg_post, w_up=dw_up, w_down=dw_down)


def _xattn_fwd(tag, x, mem_n, g_pre, g_post, w_q, w_kv, w_o, seq, mem_len, hn=None, chain=None):
    hn = _norm_fwd(tag + "_norm", x, g_pre) if hn is None else hn
    q = _mm(tag + "_q", hn, w_q, out_dtype=BF16)
    kv = _mm(tag + "_kv", mem_n, w_kv, out_dtype=BF16)
    o = _attn_fwd(q, kv, seq, mem_len)
    c = _mm(tag + "_o", o, w_o)
    y = _resid_norm_fwd(tag + "_res", x, c, g_post, chain=chain)
    return y, (x, hn, q, kv, o, c)


def _xattn_bwd(tag, saved, dy, mem_n, g_pre, g_post, w_q_t, w_kv_t, w_o_t, seq, mem_len):
    x, hn, q, kv, o, c = saved
    dc, dg_post, _ = _resid_norm_bwd(tag + "_dres", dy, c, g_post)
    do = _mm(tag + "_do", dc, w_o_t, out_dtype=BF16)
    dw_o = _mm(tag + "_dwo", o, dc, trans_a=True)
    dq, dkv = _attn_bwd(q, kv, do, seq, mem_len)
    dw_q = _mm(tag + "_dwq", hn, dq, trans_a=True)
    dhn = _mm(tag + "_dhn", dq, w_q_t)
    dw_kv = _mm(tag + "_dwkv", mem_n, dkv, trans_a=True)
    dmem_n = _mm(tag + "_dmem", dkv, w_kv_t)
    dx, dg_pre = _prenorm_bwd(tag + "_dnorm", dy, dhn, x, g_pre)
    return dx, dmem_n, dict(g_pre=dg_pre, g_post=dg_post, w_q=dw_q, w_kv=dw_kv, w_o=dw_o)


def _lru_gates(z0, z1, gb0, gb1, sp):
    r = jax.nn.sigmoid(z0 + gb0)
    i = jax.nn.sigmoid(z1 + gb1)
    log_a = -LRU_C * r * sp
    a = jnp.exp(log_a)
    mult = jnp.sqrt(_one_minus_exp(2.0 * log_a))
    return r, i, a, mult


def _rglru_fwd(x, p, seq, chain=None):
    d = x.shape[1]
    hn = _norm_fwd("a_norm", x, p["g_pre"])
    proj = _mm("a_in", hn, p["w_in"])

    def conv_body(rv, cv, pv, nv):
        u = rv[0] + cv[0]
        halo = pv[0] + cv[0]
        i = pl.program_id(0)
        halo = jnp.where((i * rv[0].shape[0]) % seq == 0, jnp.zeros_like(halo), halo)
        conv = cv[2] + u * cv[1][CONV_WIDTH - 1:CONV_WIDTH]
        for tap in range(CONV_WIDTH - 1):
            conv = conv + _shift_down(u, halo, CONV_WIDTH - 1 - tap) * cv[1][tap:tap + 1]
        return [conv], []
    (conv,), _ = _rowwise("a_conv", conv_body, [(proj, d, 1)], [p["b_in_u"], p["conv_w"], p["conv_b"]], [(d, F32)],
                          prev=[0], seq=seq)
    z = _gates_mm("a_gate", conv, p["w_gate"], "fwd")

    def gate_body(rv, cv, pv, nv):
        r, i, a, mult = _lru_gates(rv[0], rv[1], cv[0], cv[1], _softplus(-cv[2]))
        return [a, mult * i * rv[2]], []
    (a, bb), _ = _rowwise("a_gates", gate_body, [(z, d, 0), (z, d, 1), conv], [p["gate_b0"], p["gate_b1"], p["lam"]],
                          [(d, F32), (d, F32)])
    h = _scan("a_scan", a, bb, seq)

    def hy_body(rv, cv, pv, nv):
        return [rv[0] * _gelu(rv[1] + cv[0])], []
    (hy,), _ = _rowwise("a_hy", hy_body, [h, (proj, d, 0)], [p["b_in_y"]], [(d, BF16)])
    out = _mm("a_out", hy, p["w_out"])
    y = _resid_norm_fwd("a_res", x, out, p["g_post"], bias=p["b_out"], chain=chain)
    return y, (x, hn, proj, conv, z, a, h, hy, out)


def _rglru_bwd(saved, dy, p, pt, seq):
    x, hn, proj, conv, z, a, h, hy, out = saved
    d = x.shape[1]
    dt, dg_post, db_out = _resid_norm_bwd("a_dres", dy, out, p["g_post"], bias=p["b_out"])
    dhy = _mm("a_dhy", dt, pt["w_out_t"])
    dw_out = _mm("a_dwout", hy, dt, trans_a=True)

    def dh_body(rv, cv, pv, nv):
        yb = rv[2] + cv[0]
        return [rv[0] * _gelu(yb), rv[0] * rv[1] * _gelu_grad(yb)], []
    (dh, dyb), _ = _rowwise("a_dh", dh_body, [dhy, h, (proj, d, 0)], [p["b_in_y"]], [(d, F32), (d, BF16)])
    g = _scan("a_rscan", a, dh, seq, reverse=True)

    def dgate_body(rv, cv, pv, nv):
        gg, hh, z0, z1, cnv = rv
        sp = _softplus(-cv[2])
        r, i, aa, mult = _lru_gates(z0, z1, cv[0], cv[1], sp)
        i_blk = pl.program_id(0)
        halo = jnp.where((i_blk * gg.shape[0]) % seq == 0, jnp.zeros_like(pv[0]), pv[0])
        da = gg * _shift_down(hh, halo, 1)
        dmult = gg * i * cnv
        di = gg * mult * cnv
        dconv = gg * mult * i
        dlog_a = da * aa - dmult * aa * aa / mult
        dz0 = dlog_a * (-LRU_C * sp) * r * (1.0 - r)
        dz1 = di * i * (1.0 - i)
        dsp = dlog_a * (-LRU_C * r)
        dlam = dsp * (-jax.nn.sigmoid(-cv[2]))
        return [jnp.concatenate([dz0, dz1], axis=1), dconv], [dz0, dz1, dlam]
    (dz, dconv1), (dgb0, dgb1, dlam) = _rowwise(
        "a_dgates", dgate_body, [g, h, (z, d, 0), (z, d, 1), conv], [p["gate_b0"], p["gate_b1"], p["lam"]],
        [(2 * d, BF16), (d, F32)], [d, d, d], prev=[1], seq=seq)
    dconv2 = _gates_mm("a_dconv", dz, p["w_gate"], "dx")
    dw_gate = _gates_mm("a_dwgate", conv, p["w_gate"], "dw", dz=dz)

    def dconv_body(rv, cv, pv, nv):
        dc1, dc2, pu, dyb_ = rv
        dc = dc1 + dc2
        dc_next = nv[0] + nv[1]
        u = pu + cv[0]
        i_blk = pl.program_id(0)
        halo = jnp.where((i_blk * u.shape[0]) % seq == 0, jnp.zeros_like(pv[0]), pv[0] + cv[0])
        du = dc * cv[1][CONV_WIDTH - 1:CONV_WIDTH]
        dws = []
        for tap in range(CONV_WIDTH - 1):
            k = CONV_WIDTH - 1 - tap
            du = du + _shift_up(dc, dc_next, k) * cv[1][tap:tap + 1]
            dws.append(dc * _shift_down(u, halo, k))
        dws.append(dc * u)
        return [jnp.concatenate([_f32(dyb_), du], axis=1)], dws + [dc, _f32(dyb_), du]
    (dproj,), accs = _rowwise(
        "a_dconvw", dconv_body, [dconv1, dconv2, (proj, d, 1), dyb], [p["b_in_u"], p["conv_w"]],
        [(2 * d, BF16)], [d] * (CONV_WIDTH + 3), prev=[2], nxt=[0, 1], seq=seq)
    dconv_w = jnp.stack([acc.sum(0) for acc in accs[:CONV_WIDTH]])
    dconv_b = accs[CONV_WIDTH].sum(0)
    db_in = jnp.concatenate([accs[CONV_WIDTH + 1].sum(0), accs[CONV_WIDTH + 2].sum(0)])
    dhn = _mm("a_dhn", dproj, pt["w_in_t"])
    dw_in = _mm("a_dwin", hn, dproj, trans_a=True)
    dx, dg_pre = _prenorm_bwd("a_dnorm", dy, dhn, x, p["g_pre"])
    grads = dict(g_pre=dg_pre, g_post=dg_post, b_out=db_out, w_out=dw_out, gate_b0=dgb0.sum(0), gate_b1=dgb1.sum(0),
                 lam=dlam.sum(0), w_gate=dw_gate, conv_w=dconv_w, conv_b=dconv_b, b_in=db_in, w_in=dw_in)
    return dx, grads


def _rwkv_prep(k, wl, za, w0, a0, k_k, k_a, seg):
    w_in = wl + w0
    e_w = jnp.exp(-_softplus(-w_in) - 0.5)
    a = jax.nn.sigmoid(za + a0)
    q = k * k_k
    norm = jnp.sqrt(_seg_sum(q * q, seg))
    n = jnp.maximum(norm, 1e-12)
    kk = q / n
    return w_in, e_w, a, norm, n, kk


def _rwkv_out(y, r, k2, v, gn_g, gn_b, r_k, seg):
    inv = 1.0 / RWKV_HEAD
    yc = y - _seg_sum(y, seg) * inv
    rstd = lax.rsqrt(_seg_sum(yc * yc, seg) * inv + RWKV_GN_EPS)
    yhat = yc * rstd
    s = _seg_sum(r * k2 * r_k, seg)
    return rstd, yhat, s, yhat * gn_g + gn_b + s * v


def _rwkv_fwd(x, p, seq, chain=None):
    t, d = x.shape
    nseq = t // seq

    def mix_body(rv, cv, pv, nv):
        hn = _rms(rv[0], cv[0])
        xx = _shift_down(hn, _rms(pv[0], cv[0]), 1) - hn
        return [hn] + [hn + xx * cv[1][c:c + 1] for c in range(6)], []
    (hn, xr, xw, xk, xv, xa, xg), _ = _rowwise("b_mix", mix_body, [x], [p["g_pre"], p["mu"]],
                                               [(d, F32)] + [(d, BF16)] * 6, prev=[0], seq=seq)
    r = _mm("b_r", xr, p["w_r"])
    k = _mm("b_k", xk, p["w_k"])
    v = _mm("b_v", xv, p["w_v"])
    lw = _mm("b_w1", xw, p["w1"])
    la = _mm("b_a1", xa, p["a1"], out_dtype=BF16)
    lg = _mm("b_g1", xg, p["g1"])
    (th,), _ = _rowwise("b_tanh", lambda rv, cv, pv, nv: ([jnp.tanh(rv[0])], []), [lw], [], [(lw.shape[1], BF16)])
    (sg,), _ = _rowwise("b_sig", lambda rv, cv, pv, nv: ([jax.nn.sigmoid(rv[0])], []), [lg], [], [(lg.shape[1], BF16)])
    wl = _mm("b_w2", th, p["w2"])
    za = _mm("b_a2", la, p["a2"])
    g = _mm("b_g2", sg, p["g2"])

    def prep_body(rv, cv, pv, nv):
        kk_, wl_, za_ = rv
        _, e_w, a, _, _, kk = _rwkv_prep(kk_, wl_, za_, cv[0], cv[1], cv[2], cv[3], cv[4])
        return [-e_w, kk_ * (1.0 + (a - 1.0) * cv[3]), -kk, kk * a], []
    (log_w, k2, rem_a, rem_b), _ = _rowwise("b_prep", prep_body, [k, wl, za],
                                            [p["w0"], p["a0"], p["k_k"], p["k_a"], p["seg"]], [(d, F32)] * 4, tm=256)
    rec_in = (r, log_w, k2, v, rem_a, rem_b)
    y, *states = _wkv_fwd(*rec_in, seq)

    def out_body(rv, cv, pv, nv):
        y_, r_, k2_, v_, g_ = rv
        _, _, _, out = _rwkv_out(y_, r_, k2_, v_, cv[0], cv[1], cv[2], cv[3])
        return [out * g_], []
    (og,), _ = _rowwise("b_out", out_body, [y, r, k2, v, g], [p["gn_g"], p["gn_b"], p["r_k"], p["seg"]], [(d, BF16)], tm=256)
    o = _mm("b_o", og, p["w_o"])
    res = _resid_norm_fwd("b_res", x, o, p["g_post"], chain=chain)
    return res, (x, hn, xr, xw, xk, xv, xa, xg, r, k, v, th, la, sg, wl, za, g, k2, rec_in, states, y, og, o)


def _rwkv_bwd(saved, dres, p, pt, seq):
    x, hn, xr, xw, xk, xv, xa, xg, r, k, v, th, la, sg, wl, za, g, k2, rec_in, states, y, og, o = saved
    t, d = x.shape
    nseq = t // seq
    do, dg_post, _ = _resid_norm_bwd("b_dres", dres, o, p["g_post"])
    dog = _mm("b_dog", do, pt["w_o_t"])
    dw_o = _mm("b_dwo", og, do, trans_a=True)

    def dout_body(rv, cv, pv, nv):
        dog_, y_, r_, k2_, v_, g_ = rv
        gn_g, gn_b, r_k, bd = cv
        inv = 1.0 / RWKV_HEAD
        rstd, yhat, s, out = _rwkv_out(y_, r_, k2_, v_, gn_g, gn_b, r_k, bd)
        dout = dog_ * g_
        ds = _seg_sum(dout * v_, bd)
        dyhat = dout * gn_g
        dy = rstd * (dyhat - _seg_sum(dyhat, bd) * inv - yhat * _seg_sum(dyhat * yhat, bd) * inv)
        return [dy, dog_ * out, dout * s, ds * k2_ * r_k, ds * r_ * r_k], [ds * r_ * k2_, dout * yhat, dout]
    (dy, dgate, dv_b, dr_b, dk2_b), (dr_k, dgn_g, dgn_b) = _rowwise(
        "b_dout", dout_body, [dog, y, r, k2, v, g], [p["gn_g"], p["gn_b"], p["r_k"], p["seg"]],
        [(d, F32), (d, BF16), (d, F32), (d, F32), (d, F32)], [d, d, d], tm=256)
    dr_rec, dlw_rec, dk2_rec, dv_rec, da_rec, db_rec = _wkv_bwd(*rec_in, *states, dy, seq)

    def dprep_body(rv, cv, pv, nv):
        dr_rec_, dlw_rec_, dk2_rec_, dv_rec_, da_rec_, db_rec_, dr_b_, dk2_b_, dv_b_, k_, wl_, za_ = rv
        w0, a0, k_k, k_a, bd = cv
        w_in, e_w, a, norm, n, kk = _rwkv_prep(k_, wl_, za_, w0, a0, k_k, k_a, bd)
        dk2 = dk2_rec_ + dk2_b_
        dkk = db_rec_ * a - da_rec_
        da = db_rec_ * kk + dk2 * k_ * k_a
        dq = jnp.where(norm > 1e-12, dkk - kk * _seg_sum(kk * dkk, bd), dkk) / n
        dk = dk2 * (1.0 + (a - 1.0) * k_a) + dq * k_k
        dza = da * a * (1.0 - a)
        dwl = dlw_rec_ * (-e_w) * jax.nn.sigmoid(-w_in)
        return [dr_rec_ + dr_b_, dk, dv_rec_ + dv_b_, dza, dwl], [dk2 * k_ * (a - 1.0), dq * k_, dza, dwl]
    (dr, dk, dv, dza, dwl), (dk_a, dk_k, da0, dw0) = _rowwise(
        "b_dprep", dprep_body, [dr_rec, dlw_rec, dk2_rec, dv_rec, da_rec, db_rec, dr_b, dk2_b, dv_b, k, wl, za],
        [p["w0"], p["a0"], p["k_k"], p["k_a"], p["seg"]], [(d, BF16)] * 5, [d] * 4, tm=256)

    dw_r = _mm("b_dwr", xr, dr, trans_a=True)
    dw_k = _mm("b_dwk", xk, dk, trans_a=True)
    dw_v = _mm("b_dwv", xv, dv, trans_a=True)
    dxr = _mm("b_dxr", dr, pt["w_r_t"])
    dxk = _mm("b_dxk", dk, pt["w_k_t"])
    dxv = _mm("b_dxv", dv, pt["w_v_t"])
    da2 = _mm("b_da2", la, dza, trans_a=True)
    dla = _mm("b_dla", dza, pt["a2_t"], out_dtype=BF16)
    da1 = _mm("b_da1", xa, dla, trans_a=True)
    dxa = _mm("b_dxa", dla, pt["a1_t"])
    dw2 = _mm("b_dw2", th, dwl, trans_a=True)
    dth = _mm("b_dth", dwl, pt["w2_t"])
    (dzw,), _ = _rowwise("b_dtanh", lambda rv, cv, pv, nv: ([rv[0] * (1.0 - _f32(rv[1]) * _f32(rv[1]))], []),
                         [dth, th], [], [(th.shape[1], BF16)])
    dw1 = _mm("b_dw1", xw, dzw, trans_a=True)
    dxw = _mm("b_dxw", dzw, pt["w1_t"])
    dg2 = _mm("b_dg2", sg, dgate, trans_a=True)
    dsg = _mm("b_dsg", dgate, pt["g2_t"])
    (dzg,), _ = _rowwise("b_dsig", lambda rv, cv, pv, nv: ([rv[0] * _f32(rv[1]) * (1.0 - _f32(rv[1]))], []),
                         [dsg, sg], [], [(sg.shape[1], BF16)])
    dg1 = _mm("b_dg1", xg, dzg, trans_a=True)
    dxg = _mm("b_dxg", dzg, pt["g1_t"])

    def dmix_body(rv, cv, pv, nv):
        hn_ = rv[0]
        dxs = rv[1:]
        mu = cv[0]
        xx = _shift_down(hn_, pv[0], 1) - hn_
        dsum = dxs[0]
        dxx = dxs[0] * mu[0:1]
        dxx_next = nv[0] * mu[0:1]
        for c in range(1, 6):
            dsum = dsum + dxs[c]
            dxx = dxx + dxs[c] * mu[c:c + 1]
            dxx_next = dxx_next + nv[c] * mu[c:c + 1]
        return [dsum - dxx + _shift_up(dxx, dxx_next, 1)], [dxs[c] * xx for c in range(6)]
    (dhn,), dmu = _rowwise("b_dmix", dmix_body, [hn, dxr, dxw, dxk, dxv, dxa, dxg], [p["mu"]], [(d, F32)], [d] * 6,
                           prev=[0], nxt=[1, 2, 3, 4, 5, 6], seq=seq, tm=256)
    dx, dg_pre = _prenorm_bwd("b_dnorm", dres, dhn, x, p["g_pre"])
    grads = dict(g_pre=dg_pre, g_post=dg_post, mu=jnp.stack([m.sum(0) for m in dmu]), w_r=dw_r, w_k=dw_k, w_v=dw_v,
                 w0=dw0.sum(0), w1=dw1, w2=dw2, a0=da0.sum(0), a1=da1, a2=da2, g1=dg1, g2=dg2, k_k=dk_k.sum(0),
                 k_a=dk_a.sum(0), r_k=dr_k.sum(0), gn_g=dgn_g.sum(0), gn_b=dgn_b.sum(0), w_o=dw_o)
    return dx, grads


_WEIGHTS = ['ln_gains', 'mem_norm', 'a_conv_w', 'a_conv_b', 'a_w_in', 'a_b_in', 'a_gate_w', 'a_gate_b', 'a_lambda', 'a_w_out',
            'a_b_out', 'b_mu', 'b_w_rkv', 'b_w0', 'b_w1', 'b_w2', 'b_a0', 'b_a1', 'b_a2', 'b_g1', 'b_g2', 'b_k_k', 'b_k_a',
            'b_r_k', 'b_gn_g', 'b_gn_b', 'b_w_o', 'c_w_q', 'c_w_kv', 'c_w_o', 'm_w_up', 'm_w_down']
_SHARD_AXIS = dict(ln_gains=2, mem_norm=None, a_conv_w=2, a_conv_b=None, a_w_in=2, a_b_in=None, a_gate_w=3, a_gate_b=3,
                   a_lambda=None, a_w_out=1, a_b_out=None, b_mu=2, b_w_rkv=2, b_w0=1, b_w1=1, b_w2=2, b_a0=1, b_a1=1, b_a2=2,
                   b_g1=1, b_g2=2, b_k_k=1, b_k_a=1, b_r_k=None, b_gn_g=1, b_gn_b=1, b_w_o=1, c_w_q=1, c_w_kv=2, c_w_o=1,
                   m_w_up=2, m_w_down=1)
_MATRICES = ['a_w_in', 'a_gate_w', 'a_w_out', 'b_w_rkv', 'b_w1', 'b_w2', 'b_a1', 'b_a2', 'b_g1', 'b_g2', 'b_w_o', 'c_w_q',
             'c_w_kv', 'c_w_o', 'm_w_up', 'm_w_down']
_SHARDED = [n for n in _WEIGHTS if _SHARD_AXIS[n] is not None]
_VECTORS = [n for n in _SHARDED if n not in _MATRICES]
_REPLICATED = [n for n in _WEIGHTS if _SHARD_AXIS[n] is None]
N_XY = 4
N_DEV = 8
PACK_W = 1024
PACK_ROWS = 256


def _pack(arrs, dtype, row_mult=PACK_ROWS):
    parts = []
    rows = 0
    for a in arrs:
        n = a.size
        r = -(-n // PACK_W)
        parts.append(jnp.pad(a.reshape(-1).astype(dtype), (0, r * PACK_W - n)))
        rows += r
    pad_rows = -(-rows // row_mult) * row_mult - rows
    if pad_rows:
        parts.append(jnp.zeros((pad_rows * PACK_W,), dtype))
    return jnp.concatenate(parts).reshape(-1, PACK_W)


def _unpack(flat, shapes):
    out = []
    row = 0
    for shp in shapes:
        n = 1
        for s in shp:
            n *= s
        r = -(-n // PACK_W)
        out.append(flat[row:row + r].reshape(-1)[:n].reshape(shp))
        row += r
    return out


_ANY = pl.BlockSpec(memory_space=pl.ANY)


def _xy_peers():
    x, y = lax.axis_index("x"), lax.axis_index("y")
    return [(1 - x, y), (x, 1 - y), (1 - x, 1 - y)]


def _all_gather_xy(wm, wv):
    half = wm.shape[0] // 2

    def body(wm_ref, wv_ref, gm_ref, gv_ref, send_sems, recv_sems, local_sems):
        x, y, c = lax.axis_index("x"), lax.axis_index("y"), lax.axis_index("c")
        me = 2 * x + y
        mine = pl.ds(pl.multiple_of(c * half, SUBLANES), half)
        other = pl.ds(pl.multiple_of((1 - c) * half, SUBLANES), half)
        local = [pltpu.make_async_copy(wm_ref, gm_ref.at[me], local_sems.at[0]),
                 pltpu.make_async_copy(wv_ref, gv_ref.at[me], local_sems.at[1])]
        for cp in local:
            cp.start()
        sends, lands, passes, from_sibling = [], [], [], []
        for j, (px, py) in enumerate(_xy_peers()):
            peer = 2 * px + py
            ici = functools.partial(pltpu.make_async_remote_copy, device_id=(px, py, c), device_id_type=MESH)
            sends.append(ici(src_ref=wm_ref.at[mine], dst_ref=gm_ref.at[me, mine], send_sem=send_sems.at[j], recv_sem=recv_sems.at[j]))
            lands.append(ici(src_ref=wm_ref.at[mine], dst_ref=gm_ref.at[peer, mine], send_sem=send_sems.at[j], recv_sem=recv_sems.at[j]))
            sends.append(ici(src_ref=wv_ref, dst_ref=gv_ref.at[me], send_sem=send_sems.at[3 + j], recv_sem=recv_sems.at[3 + j]))
            lands.append(ici(src_ref=wv_ref, dst_ref=gv_ref.at[peer], send_sem=send_sems.at[3 + j], recv_sem=recv_sems.at[3 + j]))
            d2d = functools.partial(pltpu.make_async_remote_copy, send_sem=send_sems.at[6 + j], recv_sem=recv_sems.at[6 + j],
                                    device_id=(x, y, 1 - c), device_id_type=MESH)
            passes.append(d2d(src_ref=gm_ref.at[peer, mine], dst_ref=gm_ref.at[peer, mine]))
            from_sibling.append(d2d(src_ref=gm_ref.at[peer, other], dst_ref=gm_ref.at[peer, other]))
        for cp in sends:
            cp.start()
        for j in range(N_XY - 1):
            lands[2 * j].wait_recv()
            passes[j].start()
        for j in range(N_XY - 1):
            lands[2 * j + 1].wait_recv()
        for cp in from_sibling:
            cp.wait_recv()
        for cp in sends + passes:
            cp.wait_send()
        for cp in local:
            cp.wait()

    return pl.pallas_call(
        body, name="all_gather_weights",
        in_specs=[_ANY, _ANY], out_specs=[_ANY, _ANY],
        out_shape=[jax.ShapeDtypeStruct((N_XY,) + wm.shape, wm.dtype), jax.ShapeDtypeStruct((N_XY,) + wv.shape, wv.dtype)],
        scratch_shapes=[pltpu.SemaphoreType.DMA((9,)), pltpu.SemaphoreType.DMA((9,)), pltpu.SemaphoreType.DMA((2,))],
    )(wm, wv)


_HBM = pl.BlockSpec(memory_space=pltpu.HBM)
_SEM = pl.BlockSpec(memory_space=pltpu.SEMAPHORE)
_SPLIT_COPY = functools.partial(pltpu.CompilerParams, has_side_effects=pltpu.SideEffectType.DATAFLOW_SIDE_EFFECTING)


def _gather_xy_start(name, buf, after):
    def body(src_ref, land_ref, after_ref, send_sems, recv_sems, src_thru, land_thru, token):
        x, y, c = lax.axis_index("x"), lax.axis_index("y"), lax.axis_index("c")
        for j, (px, py) in enumerate(_xy_peers()):
            pltpu.make_async_remote_copy(src_ref=src_ref, dst_ref=land_ref.at[2 * x + y], send_sem=send_sems.at[j],
                                         recv_sem=recv_sems.at[j], device_id=(px, py, c), device_id_type=MESH).start()
        token[...] = jnp.zeros_like(token)

    n_peers = N_XY - 1
    land = pltpu.with_memory_space_constraint(lax.empty((N_XY,) + buf.shape, buf.dtype), pltpu.HBM)
    return pl.pallas_call(
        body, name=name,
        out_shape=(pltpu.SemaphoreType.DMA((n_peers,)), pltpu.SemaphoreType.DMA((n_peers,)), pltpu.HBM(buf.shape, buf.dtype),
                   pltpu.HBM(land.shape, buf.dtype), jax.ShapeDtypeStruct((SUBLANES, LANES), F32)),
        in_specs=(_HBM, _HBM, _ANY), out_specs=(_SEM, _SEM, _HBM, _HBM, pl.BlockSpec(memory_space=pltpu.VMEM)),
        input_output_aliases={0: 2, 1: 3}, compiler_params=_SPLIT_COPY(),
    )(pltpu.with_memory_space_constraint(buf, pltpu.HBM), land, after)


def _gather_xy_wait(name, send_sems, recv_sems, src_thru, land_thru, after):
    def body(src_ref, land_ref, send_sems, recv_sems, after_ref, src_dead, got_ref):
        c = lax.axis_index("c")
        for j, (px, py) in enumerate(_xy_peers()):
            cp = pltpu.make_async_remote_copy(src_ref=src_ref, dst_ref=land_ref.at[2 * px + py], send_sem=send_sems.at[j],
                                              recv_sem=recv_sems.at[j], device_id=(px, py, c), device_id_type=MESH)
            cp.wait_send()
            cp.wait_recv()

    return pl.pallas_call(
        body, name=name,
        out_shape=(pltpu.HBM(src_thru.shape, src_thru.dtype), pltpu.HBM(land_thru.shape, land_thru.dtype)),
        in_specs=(_HBM, _HBM, _SEM, _SEM, _ANY), out_specs=(_HBM, _HBM),
        input_output_aliases={0: 0, 1: 1}, compiler_params=_SPLIT_COPY(),
    )(src_thru, land_thru, send_sems, recv_sems, after)[1]


def _exchange_xy_start(name, gsend, after):
    def body(src_ref, land_ref, after_ref, send_sems, recv_sems, src_thru, land_thru, token):
        c = lax.axis_index("c")
        for j, (px, py) in enumerate(_xy_peers()):
            pltpu.make_async_remote_copy(src_ref=src_ref.at[2 * px + py], dst_ref=land_ref.at[j], send_sem=send_sems.at[j],
                                         recv_sem=recv_sems.at[j], device_id=(px, py, c), device_id_type=MESH).start()
        token[...] = jnp.zeros_like(token)

    n_peers = N_XY - 1
    land = pltpu.with_memory_space_constraint(lax.empty((n_peers,) + gsend.shape[1:], gsend.dtype), pltpu.HBM)
    return pl.pallas_call(
        body, name=name,
        out_shape=(pltpu.SemaphoreType.DMA((n_peers,)), pltpu.SemaphoreType.DMA((n_peers,)), pltpu.HBM(gsend.shape, gsend.dtype),
                   pltpu.HBM(land.shape, gsend.dtype), jax.ShapeDtypeStruct((SUBLANES, LANES), F32)),
        in_specs=(_HBM, _HBM, _ANY), out_specs=(_SEM, _SEM, _HBM, _HBM, pl.BlockSpec(memory_space=pltpu.VMEM)),
        input_output_aliases={0: 2, 1: 3}, compiler_params=_SPLIT_COPY(),
    )(pltpu.with_memory_space_constraint(gsend, pltpu.HBM), land, after)


def _exchange_xy_wait(name, send_sems, recv_sems, src_thru, land_thru, after):
    def body(src_ref, land_ref, send_sems, recv_sems, after_ref, src_dead, got_ref):
        c = lax.axis_index("c")
        for j, (px, py) in enumerate(_xy_peers()):
            cp = pltpu.make_async_remote_copy(src_ref=src_ref.at[2 * px + py], dst_ref=land_ref.at[j], send_sem=send_sems.at[j],
                                              recv_sem=recv_sems.at[j], device_id=(px, py, c), device_id_type=MESH)
            cp.wait_send()
            cp.wait_recv()

    return pl.pallas_call(
        body, name=name,
        out_shape=(pltpu.HBM(src_thru.shape, src_thru.dtype), pltpu.HBM(land_thru.shape, land_thru.dtype)),
        in_specs=(_HBM, _HBM, _SEM, _SEM, _ANY), out_specs=(_HBM, _HBM),
        input_output_aliases={0: 0, 1: 1}, compiler_params=_SPLIT_COPY(),
    )(src_thru, land_thru, send_sems, recv_sems, after)[1]


def _exchange_xy(gsend):
    def body(gs_ref, recv_ref, send_sems, recv_sems):
        c = lax.axis_index("c")
        sends = []
        for j, (px, py) in enumerate(_xy_peers()):
            sends.append(pltpu.make_async_remote_copy(
                src_ref=gs_ref.at[2 * px + py], dst_ref=recv_ref.at[j], send_sem=send_sems.at[j], recv_sem=recv_sems.at[j],
                device_id=(px, py, c), device_id_type=MESH))
        for cp in sends:
            cp.start()
        for cp in sends:
            cp.wait_recv()
        for cp in sends:
            cp.wait_send()

    return pl.pallas_call(
        body, name="exchange_grads",
        in_specs=[_ANY], out_specs=_ANY,
        out_shape=jax.ShapeDtypeStruct((N_XY - 1,) + gsend.shape[1:], gsend.dtype),
        scratch_shapes=[pltpu.SemaphoreType.DMA((3,)), pltpu.SemaphoreType.DMA((3,))],
    )(gsend)


def _swap_with_sibling(name, part):
    def body(p_ref, got_ref, send_sem, recv_sem):
        x, y, c = lax.axis_index("x"), lax.axis_index("y"), lax.axis_index("c")
        cp = pltpu.make_async_remote_copy(src_ref=p_ref, dst_ref=got_ref, send_sem=send_sem, recv_sem=recv_sem,
                                          device_id=(x, y, 1 - c), device_id_type=MESH)
        cp.start()
        cp.wait_recv()
        cp.wait_send()

    return pl.pallas_call(
        body, name=name,
        in_specs=[_ANY], out_specs=_ANY, out_shape=jax.ShapeDtypeStruct(part.shape, part.dtype),
        scratch_shapes=[pltpu.SemaphoreType.DMA, pltpu.SemaphoreType.DMA],
    )(part)


def _all_gather_all(vec):
    def body(v_ref, out_ref, send_sems, recv_sems, local_sem):
        x, y, c = lax.axis_index("x"), lax.axis_index("y"), lax.axis_index("c")
        me = 4 * x + 2 * y + c
        local = pltpu.make_async_copy(v_ref, out_ref.at[me], local_sem)
        local.start()
        sends, recvs = [], []
        for f in range(1, N_DEV):
            fx, fy, fc = (f >> 2) & 1, (f >> 1) & 1, f & 1
            px = (1 - x) if fx else x
            py = (1 - y) if fy else y
            pc = (1 - c) if fc else c
            mk = functools.partial(pltpu.make_async_remote_copy, src_ref=v_ref, send_sem=send_sems.at[f - 1],
                                   recv_sem=recv_sems.at[f - 1], device_id=(px, py, pc), device_id_type=MESH)
            sends.append(mk(dst_ref=out_ref.at[me]))
            recvs.append(mk(dst_ref=out_ref.at[4 * px + 2 * py + pc]))
        for cp in sends:
            cp.start()
        for cp in recvs:
            cp.wait_recv()
        for cp in sends:
            cp.wait_send()
        local.wait()

    return pl.pallas_call(
        body, name="all_gather_replicated",
        in_specs=[_ANY], out_specs=_ANY, out_shape=jax.ShapeDtypeStruct((N_DEV,) + vec.shape, vec.dtype),
        scratch_shapes=[pltpu.SemaphoreType.DMA((N_DEV - 1,)), pltpu.SemaphoreType.DMA((N_DEV - 1,)), pltpu.SemaphoreType.DMA],
    )(vec)


def _adamw(g, w, m, v):
    m2 = ADAM_B1 * m + (1.0 - ADAM_B1) * g
    v2 = ADAM_B2 * v + (1.0 - ADAM_B2) * g * g
    m_hat = m2 / (1.0 - ADAM_B1 ** ADAM_STEP)
    v_hat = v2 / (1.0 - ADAM_B2 ** ADAM_STEP)
    return -ADAM_LR * (m_hat / (jnp.sqrt(v_hat) + ADAM_EPS) + ADAM_WD * w), m2, v2


def _sum_contributions(name, own, recv):
    def body(rv, cv, pv, nv):
        return [((rv[0] + _f32(rv[1])) + _f32(rv[2])) + _f32(rv[3])], []
    stacked = recv.reshape(-1, PACK_W)
    (part,), _ = _rowwise(name, body, [own] + [(stacked, PACK_W, 0, j * own.shape[0]) for j in range(N_XY - 1)], [],
                          [(PACK_W, F32)])
    return part


def _adamw_sharded(name, part, sib, first, w, m, v, w_first=0, n_rows=None):
    def body(rv, cv, pv, nv):
        g = rv[3] + rv[4]
        return [g, *_adamw(g, rv[0], rv[1], rv[2])], []
    n_rows = w.shape[0] if n_rows is None else n_rows
    rows = [(z, PACK_W, 0, w_first) for z in (w, m, v)] + [(part, PACK_W, 0, first), (sib, PACK_W, 0, first)]
    outs, _ = _rowwise(name, body, rows, [], [(PACK_W, F32)] * 4, tm=256, n_rows=n_rows)
    return outs


def _adamw_replicated(parts, w, m, v):
    def body(rv, cv, pv, nv):
        g = rv[0]
        for i in range(1, N_DEV):
            g = g + rv[i]
        return [g, *_adamw(g, rv[N_DEV], rv[N_DEV + 1], rv[N_DEV + 2])], []
    outs, _ = _rowwise("adamw_replicated", body, [parts[i] for i in range(N_DEV)] + [w, m, v], [], [(PACK_W, F32)] * 4)
    return outs


def _row(v):
    return v.reshape(1, -1).astype(F32)


def _local_step(x3, mem3, target3, fw, late_weights=None, on_grads=None):
    def with_late(stage, after):
        if late_weights is None:
            return fw
        got = late_weights(stage, after)
        return {**fw, **{n: ({**fw.get(n, {}), **v} if isinstance(v, dict) else v) for n, v in got.items()}}

    nseq, seq, d = x3.shape
    mem_len = mem3.shape[1]
    t = nseq * seq
    x0 = x3.reshape(t, d)
    mem2 = mem3.reshape(nseq * mem_len, d)
    target = target3.reshape(t, d)
    ln = fw["ln_gains"]
    gains = [[_row(ln[i, j]) for j in range(6)] for i in range(2)]
    nh = d // RWKV_HEAD
    seg = (jnp.arange(d)[:, None] // RWKV_HEAD == jnp.arange(LANES)[None, :]).astype(BF16)

    w_gate = fw["a_gate_w"][0]
    pa = dict(g_pre=gains[0][0], g_post=gains[0][1], w_in=fw["a_w_in"][0], b_in_y=_row(fw["a_b_in"][0, :d]),
              b_in_u=_row(fw["a_b_in"][0, d:]), conv_w=fw["a_conv_w"][0].astype(F32), conv_b=_row(fw["a_conv_b"][0]),
              w_gate=w_gate, gate_b0=_row(fw["a_gate_b"][0, 0]), gate_b1=_row(fw["a_gate_b"][0, 1]), lam=_row(fw["a_lambda"][0]),
              w_out=fw["a_w_out"][0], b_out=_row(fw["a_b_out"][0]))
    pta = dict(w_in_t=_t(pa["w_in"]), w_out_t=_t(pa["w_out"]))
    mem_g = _row(fw["mem_norm"])

    mem_n = _norm_fwd("mem_norm", mem2, mem_g)
    to_c0, to_m0, to_c1, to_m1, to_loss = ({"gain": gains[0][2]}, {"gain": gains[0][4]}, {"gain": gains[1][2]},
                                           {"gain": gains[1][4]}, {"target": target})
    x1, sv_a = _rglru_fwd(x0, pa, seq, chain=to_c0)
    fw = with_late("cm0", x1)
    x2, sv_c0 = _xattn_fwd("c0", x1, mem_n, gains[0][2], gains[0][3], fw["c_w_q"][0], fw["c_w_kv"][0], fw["c_w_o"][0], seq, mem_len,
                           hn=to_c0["hn"], chain=to_m0)
    x3_, sv_m0 = _mlp_fwd("m0", x2, gains[0][4], gains[0][5], fw["m_w_up"][0], fw["m_w_down"][0], hn=to_m0["hn"])
    fw = with_late("layer1", x3_)
    pb = dict(g_pre=gains[1][0], g_post=gains[1][1], mu=fw["b_mu"][0].astype(F32), w_r=fw["b_w_rkv"][0, 0],
              w_k=fw["b_w_rkv"][0, 1], w_v=fw["b_w_rkv"][0, 2], w0=_row(fw["b_w0"][0]), w1=fw["b_w1"][0], w2=fw["b_w2"][0],
              a0=_row(fw["b_a0"][0]), a1=fw["b_a1"][0], a2=fw["b_a2"][0], g1=fw["b_g1"][0], g2=fw["b_g2"][0],
              k_k=_row(fw["b_k_k"][0]), k_a=_row(fw["b_k_a"][0]), r_k=_row(fw["b_r_k"][0]), gn_g=_row(fw["b_gn_g"][0]),
              gn_b=_row(fw["b_gn_b"][0]), w_o=fw["b_w_o"][0], seg=seg)
    ptb = {k + "_t": _t(pb[k]) for k in ("w_r", "w_k", "w_v", "w_o", "w1", "w2", "a1", "a2", "g1", "g2")}
    x4, sv_b = _rwkv_fwd(x3_, pb, seq, chain=to_c1)
    x5, sv_c1 = _xattn_fwd("c1", x4, mem_n, gains[1][2], gains[1][3], fw["c_w_q"][1], fw["c_w_kv"][1], fw["c_w_o"][1], seq, mem_len,
                           hn=to_c1["hn"], chain=to_m1)
    _, sv_m1 = _mlp_fwd("m1", x5, gains[1][4], gains[1][5], fw["m_w_up"][1], fw["m_w_down"][1], hn=to_m1["hn"], chain=to_loss)
    dx = to_loss["dy"]
    loss_part = 0.5 / d * jnp.sum(to_loss["sq"])

    dx, g_m1 = _mlp_bwd("m1", sv_m1, dx, gains[1][4], gains[1][5], _t(fw["m_w_up"][1]), _t(fw["m_w_down"][1]))
    dx, dmem1, g_c1 = _xattn_bwd("c1", sv_c1, dx, mem_n, gains[1][2], gains[1][3], _t(fw["c_w_q"][1]), _t(fw["c_w_kv"][1]),
                                 _t(fw["c_w_o"][1]), seq, mem_len)
    dx, g_b = _rwkv_bwd(sv_b, dx, pb, ptb, seq)
    grads_b = dict(
        b_mu=g_b["mu"][None], b_w_rkv=jnp.stack([g_b["w_r"], g_b["w_k"], g_b["w_v"]])[None], b_w0=g_b["w0"][None],
        b_w1=g_b["w1"][None], b_w2=g_b["w2"][None], b_a0=g_b["a0"][None], b_a1=g_b["a1"][None], b_a2=g_b["a2"][None],
        b_g1=g_b["g1"][None], b_g2=g_b["g2"][None], b_k_k=g_b["k_k"][None], b_k_a=g_b["k_a"][None],
        b_r_k=g_b["r_k"].reshape(1, nh, RWKV_HEAD), b_gn_g=g_b["gn_g"][None], b_gn_b=g_b["gn_b"][None], b_w_o=g_b["w_o"][None])
    g_m0_pre = gains[0][4]
    if on_grads is not None:
        layer1 = dict(c_w_q=g_c1["w_q"], c_w_kv=g_c1["w_kv"], c_w_o=g_c1["w_o"], m_w_up=g_m1["w_up"], m_w_down=g_m1["w_down"])
        g_m0_pre = g_m0_pre + on_grads("layer1", {**grads_b, **{n: [None, g] for n, g in layer1.items()}})[0, 0]
    dx, g_m0 = _mlp_bwd("m0", sv_m0, dx, g_m0_pre, gains[0][5], _t(fw["m_w_up"][0]), _t(fw["m_w_down"][0]))
    dx, dmem0, g_c0 = _xattn_bwd("c0", sv_c0, dx, mem_n, gains[0][2], gains[0][3], _t(fw["c_w_q"][0]), _t(fw["c_w_kv"][0]),
                                 _t(fw["c_w_o"][0]), seq, mem_len)
    if on_grads is not None:
        cm0 = dict(c_w_q=g_c0["w_q"], c_w_kv=g_c0["w_kv"], c_w_o=g_c0["w_o"], m_w_up=g_m0["w_up"], m_w_down=g_m0["w_down"])
        pa = {**pa, "g_post": pa["g_post"] + on_grads("cm0", {n: [g, None] for n, g in cm0.items()})[0, 0]}
    dx, g_a = _rglru_bwd(sv_a, dx, pa, pta, seq)

    def dmem_body(rv, cv, pv, nv):
        _, dg = _rms_bwd(rv[1] + rv[2], rv[0], cv[0])
        return [], [dg]
    _, (dmem_g,) = _rowwise("mem_norm_grad", dmem_body, [mem2, dmem0, dmem1], [mem_g], [], [d])

    lru_heads = fw["a_gate_w"].shape[2]
    blk = d // lru_heads
    grads = dict(
        ln_gains=jnp.stack([jnp.stack([g_a["g_pre"], g_a["g_post"], g_c0["g_pre"], g_c0["g_post"], g_m0["g_pre"], g_m0["g_post"]]),
                            jnp.stack([g_b["g_pre"], g_b["g_post"], g_c1["g_pre"], g_c1["g_post"], g_m1["g_pre"], g_m1["g_post"]])]),
        mem_norm=dmem_g.sum(0),
        a_conv_w=g_a["conv_w"][None], a_conv_b=g_a["conv_b"][None], a_w_in=g_a["w_in"][None], a_b_in=g_a["b_in"][None],
        a_gate_w=g_a["w_gate"][None],
        a_gate_b=jnp.stack([g_a["gate_b0"], g_a["gate_b1"]]).reshape(1, 2, lru_heads, blk),
        a_lambda=g_a["lam"][None], a_w_out=g_a["w_out"][None], a_b_out=g_a["b_out"][None],
        **grads_b,
        c_w_q=[g_c0["w_q"], g_c1["w_q"]], c_w_kv=[g_c0["w_kv"], g_c1["w_kv"]], c_w_o=[g_c0["w_o"], g_c1["w_o"]],
        m_w_up=[g_m0["w_up"], g_m1["w_up"]], m_w_down=[g_m0["w_down"], g_m1["w_down"]],
    )
    return loss_part, dx.reshape(nseq, seq, d), grads


def kernel(x, mem, ln_gains, mem_norm, a_conv_w, a_conv_b, a_w_in, a_b_in, a_gate_w, a_gate_b, a_lambda, a_w_out, a_b_out, b_mu, b_w_rkv, b_w0, b_w1, b_w2, b_a0, b_a1, b_a2, b_g1, b_g2, b_k_k, b_k_a, b_r_k, b_gn_g, b_gn_b, b_w_o, c_w_q, c_w_kv, c_w_o, m_w_up, m_w_down, loss_target, m_ln_gains, m_mem_norm, m_a_conv_w, m_a_conv_b, m_a_w_in, m_a_b_in, m_a_gate_w, m_a_gate_b, m_a_lambda, m_a_w_out, m_a_b_out, m_b_mu, m_b_w_rkv, m_b_w0, m_b_w1, m_b_w2, m_b_a0, m_b_a1, m_b_a2, m_b_g1, m_b_g2, m_b_k_k, m_b_k_a, m_b_r_k, m_b_gn_g, m_b_gn_b, m_b_w_o, m_c_w_q, m_c_w_kv, m_c_w_o, m_m_w_up, m_m_w_down, v_ln_gains, v_mem_norm, v_a_conv_w, v_a_conv_b, v_a_w_in, v_a_b_in, v_a_gate_w, v_a_gate_b, v_a_lambda, v_a_w_out, v_a_b_out, v_b_mu, v_b_w_rkv, v_b_w0, v_b_w1, v_b_w2, v_b_a0, v_b_a1, v_b_a2, v_b_g1, v_b_g2, v_b_k_k, v_b_k_a, v_b_r_k, v_b_gn_g, v_b_gn_b, v_b_w_o, v_c_w_q, v_c_w_kv, v_c_w_o, v_m_w_up, v_m_w_down):
    given = dict(locals())
    w = {n: given[n] for n in _WEIGHTS}
    mom1 = {n: given["m_" + n] for n in _WEIGHTS}
    mom2 = {n: given["v_" + n] for n in _WEIGHTS}

    me = 2 * lax.axis_index("x") + lax.axis_index("y")
    per_layer = [n for n in _MATRICES if n[0] in "cm"]
    early = [(n, None) for n in _MATRICES if n[0] == "a"]
    late = dict(cm0=[(n, 0) for n in per_layer], layer1=[(n, None) for n in _MATRICES if n[0] == "b"] + [(n, 1) for n in per_layer])
    piece = lambda n, layer: w[n] if layer is None else w[n][layer:layer + 1]

    def gathered(entries, buf):
        shards = [_unpack(buf[s], [piece(n, layer).shape for n, layer in entries]) for s in range(N_XY)]
        out = {}
        for i, (n, layer) in enumerate(entries):
            full = jnp.concatenate([shards[s][i] for s in range(N_XY)], axis=_SHARD_AXIS[n])
            if layer is None:
                out[n] = full
            else:
                out.setdefault(n, {})[layer] = full[0]
        return out

    late_bufs = {stage: _pack([piece(n, layer) for n, layer in entries], BF16) for stage, entries in late.items()}
    gm, gv = _all_gather_xy(_pack([piece(n, layer) for n, layer in early], BF16), _pack([w[n] for n in _VECTORS], F32, SUBLANES))
    in_flight = {}
    token = gv
    for stage in late:
        *in_flight[stage], token = _gather_xy_start("gather_%s_start" % stage, late_bufs[stage], token)
    fw = {n: w[n] for n in _REPLICATED} | gathered(early, gm)
    vec_shards = [_unpack(gv[s], [w[n].shape for n in _VECTORS]) for s in range(N_XY)]
    for i, n in enumerate(_VECTORS):
        fw[n] = jnp.concatenate([vec_shards[s][i] for s in range(N_XY)], axis=_SHARD_AXIS[n])
    fw["ln_gains"] = fw["ln_gains"] + token[0, 0]

    def late_weights(stage, after):
        land = _gather_xy_wait("gather_%s_wait" % stage, *in_flight[stage], after)
        return gathered(late[stage], lax.dynamic_update_index_in_dim(land, late_bufs[stage], me, 0))

    tile_rows = 256
    groups = dict(layer1=[(n, 1 if n in per_layer else None) for n in _SHARDED if n in per_layer or n[0] == "b"],
                  cm0=[(n, 0) for n in _SHARDED if n in per_layer],
                  rest=[(n, None) for n in _SHARDED if n not in per_layer and n[0] != "b"])

    def piece_shape(n, layer):
        return w[n].shape if layer is None else (1,) + w[n].shape[1:]

    def rows_of(n, layer):
        size = 1
        for s_ in piece_shape(n, layer):
            size *= s_
        return size // PACK_W

    def split(entries):
        in_place = [e for e in entries if w[e[0]].shape[-1] == PACK_W and rows_of(*e) % tile_rows == 0 and rows_of(*e) > 0]
        in_place.sort(key=lambda e: -rows_of(*e))
        return in_place, [e for e in entries if e not in in_place]

    def buffers(entries, grads):
        in_place, packed = split(entries)

        def pieces(s):
            out = []
            for n, layer in in_place + packed:
                ax = _SHARD_AXIS[n]
                size = w[n].shape[ax]
                g = grads[n] if layer is None else grads[n][layer]
                out.append(lax.dynamic_slice_in_dim(g, s * size, size, axis=ax if layer is None else ax - 1))
            return out
        own = _pack(pieces(me), F32)
        n_rows = own.shape[0]
        used = sum(-(-rows_elems // PACK_W) for rows_elems in (p.size for p in pieces(0)))
        fill = [jnp.zeros(((n_rows - used) * PACK_W,), BF16)] if n_rows > used else []
        gsend = _pack([p for s in range(N_XY) for p in pieces(s) + fill], BF16, row_mult=1)
        return gsend.reshape(N_XY, n_rows, PACK_W), own

    def update(tag, entries, own, recv):
        in_place, packed = split(entries)
        part = _sum_contributions("sum_grads_" + tag, own, recv)
        sib = _swap_with_sibling("swap_sibling_" + tag, part)
        out = {}
        first = 0
        for n, layer in in_place:
            flat = [src[n].reshape(-1, PACK_W) for src in (w, mom1, mom2)]
            rows = rows_of(n, layer)
            res = _adamw_sharded("adamw_%s_%s" % (n, tag), part, sib, first, *flat, w_first=(layer or 0) * rows, n_rows=rows)
            out[(n, layer)] = [o.reshape(piece_shape(n, layer)) for o in res]
            first += rows
        take = lambda src, n, layer: src[n] if layer is None else src[n][layer:layer + 1]
        flat = [_pack([take(src, n, layer) for n, layer in packed], F32) for src in (w, mom1, mom2)]
        assert first + flat[0].shape[0] == part.shape[0], (first, flat[0].shape, part.shape)
        res = _adamw_sharded("adamw_packed_" + tag, part, sib, first, *flat)
        tail = [_unpack(o, [piece_shape(n, layer) for n, layer in packed]) for o in res]
        for i, e in enumerate(packed):
            out[e] = [tail[kind][i] for kind in range(4)]
        return out

    sent = {}

    def on_grads(stage, grads_so_far):
        gsend, own = buffers(groups[stage], grads_so_far)
        *handles, token = _exchange_xy_start("exchange_%s_start" % stage, gsend, own)
        sent[stage] = (own, handles)
        return token

    loss_part, grad_x, grads = _local_step(x, mem, loss_target, fw, late_weights, on_grads)

    gsend, own = buffers(groups["rest"], grads)
    out = update("rest", groups["rest"], own, _exchange_xy(gsend))
    for stage, (own, handles) in sent.items():
        out |= update(stage, groups[stage], own, _exchange_xy_wait("exchange_%s_wait" % stage, *handles, grad_x))
    sharded_out = [[] for _ in range(4)]
    for n in _SHARDED:
        for kind in range(4):
            if n in per_layer:
                sharded_out[kind].append(jnp.concatenate([out[(n, 0)][kind], out[(n, 1)][kind]], axis=0))
            else:
                sharded_out[kind].append(out[(n, None)][kind])

    small = _pack([grads[n] for n in _REPLICATED] + [loss_part.reshape(1)], F32, SUBLANES)
    parts = _all_gather_all(small)
    zero = jnp.zeros((1,), F32)
    flat = [_pack([src[n] for n in _REPLICATED] + [zero], F32, SUBLANES) for src in (w, mom1, mom2)]
    repl_out = [_unpack(o, [w[n].shape for n in _REPLICATED] + [(1,)]) for o in _adamw_replicated(parts, *flat)]
    loss = repl_out[0][-1][0]

    result = [loss, grad_x]
    for kind in range(4):
        by_name = dict(zip(_SHARDED, sharded_out[kind])) | dict(zip(_REPLICATED, repl_out[kind][:-1]))
        result += [by_name[n] for n in _WEIGHTS]
    return tuple(result)
```

```python
import functools

import jax
import jax.numpy as jnp
from jax import lax
from jax.experimental import pallas as pl
from jax.experimental.pallas import tpu as pltpu

F32 = jnp.float32
BF16 = jnp.bfloat16
MESH = pl.DeviceIdType.MESH

LANES = 128
SUBLANES = 8
VMEM_LIMIT_BYTES = 48 * 1024 * 1024

RMS_EPS = 1e-6
LRU_C = 8.0
LRU_HEADS = 4
CONV_WIDTH = 4
RWKV_HEAD = 64
RWKV_GN_EPS = 64e-5
MEM_HEADS = 4
ADAM_LR = 0.001
ADAM_B1 = 0.9
ADAM_B2 = 0.999
ADAM_EPS = 1e-08
ADAM_WD = 0.01
ADAM_STEP = 10
WKV_CHUNK = 64

_PARAMS = functools.partial(pltpu.CompilerParams, vmem_limit_bytes=VMEM_LIMIT_BYTES)


def _tile(n, want):
    if n <= want:
        return n
    t = want
    while t >= SUBLANES:
        if n % t == 0 and t % SUBLANES == 0:
            return t
        t -= SUBLANES
    return n


def _fold8(v):
    tm, d = v.shape
    if tm == SUBLANES:
        return v
    return jnp.sum(v.reshape(tm // SUBLANES, SUBLANES, d), axis=0)


def _rowwise(name, body, rows, consts=(), out_rows=(), out_accs=(), prev=(), nxt=(), tm=512, seq=None, n_rows=None):
    rows = [r if isinstance(r, tuple) else (r, r.shape[1], 0) for r in rows]
    rows = [r if len(r) == 4 else r + (0,) for r in rows]
    t = rows[0][0].shape[0] if n_rows is None else n_rows
    tm = _tile(t, tm)
    if seq is not None:
        tm = _tile(seq, tm)
    nblk = t // tm
    nrow, ncst, nprev, nnxt = len(rows), len(consts), len(prev), len(nxt)
    nor, noa = len(out_rows), len(out_accs)
    hb = tm // SUBLANES

    def kern(*refs):
        i = pl.program_id(0)
        rv = [r[...] for r in refs[:nrow]]
        cv = [c[...] for c in refs[nrow:nrow + ncst]]
        o = nrow + ncst
        pv = []
        for j in range(nprev):
            at_start = (i * tm) % seq == 0
            h = refs[o + j][...]
            pv.append(jnp.where(at_start, jnp.zeros_like(h), h))
        o += nprev
        nv = []
        for j in range(nnxt):
            at_end = ((i + 1) * tm) % seq == 0
            h = refs[o + j][...]
            nv.append(jnp.where(at_end, jnp.zeros_like(h), h))
        o += nnxt
        outs, accs = body(rv, cv, pv, nv)
        for j in range(nor):
            refs[o + j][...] = outs[j].astype(refs[o + j].dtype)
        o += nor
        if noa:
            @pl.when(i == 0)
            def _():
                for j in range(noa):
                    refs[o + j][...] = jnp.zeros_like(refs[o + j])
            for j in range(noa):
                refs[o + j][...] += _fold8(accs[j].astype(F32))

    assert all(first % tm == 0 for (_, _, _, first) in rows), name
    in_specs = [pl.BlockSpec((tm, w), functools.partial(lambda i, c, o: (i + o, c), c=cb, o=first // tm))
                for (_, w, cb, first) in rows]
    in_specs += [pl.BlockSpec(c.shape, lambda i: (0, 0)) for c in consts]
    in_specs += [pl.BlockSpec((SUBLANES, rows[j][1]),
                              functools.partial(lambda i, c: (jnp.maximum(i * hb - 1, 0), c), c=rows[j][2])) for j in prev]
    in_specs += [pl.BlockSpec((SUBLANES, rows[j][1]),
                              functools.partial(lambda i, c: (jnp.minimum((i + 1) * hb, t // SUBLANES - 1), c), c=rows[j][2]))
                 for j in nxt]
    out_shape = [jax.ShapeDtypeStruct((t, w), dt) for (w, dt) in out_rows]
    out_shape += [jax.ShapeDtypeStruct((SUBLANES, w), F32) for w in out_accs]
    out_specs = [pl.BlockSpec((tm, w), lambda i: (i, 0)) for (w, _) in out_rows]
    out_specs += [pl.BlockSpec((SUBLANES, w), lambda i: (0, 0)) for w in out_accs]
    args = [r[0] for r in rows] + list(consts) + [rows[j][0] for j in prev] + [rows[j][0] for j in nxt]
    res = pl.pallas_call(
        kern, name=name, grid=(nblk,), in_specs=in_specs, out_specs=out_specs, out_shape=out_shape,
        compiler_params=_PARAMS(dimension_semantics=("arbitrary",)),
    )(*args)
    return list(res[:nor]), list(res[nor:])


def _shift_down(x, halo, k):
    rolled = pltpu.roll(x, k, 0)
    row = lax.broadcasted_iota(jnp.int32, (SUBLANES, x.shape[1]), 0)
    first = jnp.where(row < k, pltpu.roll(halo, k, 0), rolled[:SUBLANES])
    if x.shape[0] == SUBLANES:
        return first
    return jnp.concatenate([first, rolled[SUBLANES:]], axis=0)


def _shift_up(x, halo, k):
    n = x.shape[0]
    rolled = pltpu.roll(x, n - k, 0)
    row = lax.broadcasted_iota(jnp.int32, (SUBLANES, x.shape[1]), 0)
    last = jnp.where(row >= SUBLANES - k, pltpu.roll(halo, SUBLANES - k, 0), rolled[n - SUBLANES:])
    if n == SUBLANES:
        return last
    return jnp.concatenate([rolled[:n - SUBLANES], last], axis=0)


class _Transposed:
    def __init__(self, w):
        self.w = w


def _t(w):
    return _Transposed(w)


def _mm(name, a, b, out_dtype=F32, trans_a=False, tm=1024, tn=1024, tk=1024, epilogue=None, extra=None):
    trans_b = isinstance(b, _Transposed)
    assert not (trans_a and trans_b)
    if trans_b:
        b = b.w
    if trans_a:
        kdim, m = a.shape
    else:
        m, kdim = a.shape
    n = b.shape[0] if trans_b else b.shape[1]
    assert b.shape[1 if trans_b else 0] == kdim, (name, a.shape, b.shape)
    tm, tn, tk = _tile(m, tm), _tile(n, tn), _tile(kdim, tk)
    nk = kdim // tk
    dims = (((0,), (0,)), ((), ())) if trans_a else (((1,), (1 if trans_b else 0,)), ((), ()))

    n_in = 2 if extra is None else 3

    def kern(*refs):
        a_ref, b_ref, o_ref, acc = refs[0], refs[1], refs[n_in], refs[n_in + 1:]

        def store(res):
            if epilogue is not None:
                res = epilogue(res) if extra is None else epilogue(res, refs[2][...])
            o_ref[...] = res.astype(o_ref.dtype)

        part = lax.dot_general(a_ref[...].astype(BF16), b_ref[...].astype(BF16), dims, preferred_element_type=F32)
        if nk == 1:
            store(part)
        else:
            k = pl.program_id(2)

            @pl.when(k == 0)
            def _():
                acc[0][...] = part

            @pl.when(k > 0)
            def _():
                acc[0][...] += part

            @pl.when(k == nk - 1)
            def _():
                store(acc[0][...])

    a_spec = pl.BlockSpec((tk, tm), lambda i, j, k: (k, i)) if trans_a else pl.BlockSpec((tm, tk), lambda i, j, k: (i, k))
    b_spec = pl.BlockSpec((tn, tk), lambda i, j, k: (j, k)) if trans_b else pl.BlockSpec((tk, tn), lambda i, j, k: (k, j))
    out_spec = pl.BlockSpec((tm, tn), lambda i, j, k: (i, j))
    return pl.pallas_call(
        kern, name=name, grid=(m // tm, n // tn, nk),
        in_specs=[a_spec, b_spec] + ([] if extra is None else [out_spec]),
        out_specs=out_spec,
        out_shape=jax.ShapeDtypeStruct((m, n), out_dtype),
        scratch_shapes=[] if nk == 1 else [pltpu.VMEM((tm, tn), F32)],
        compiler_params=_PARAMS(dimension_semantics=("parallel", "parallel", "arbitrary")),
    )(*((a, b) if extra is None else (a, b, extra)))


def _gates_mm(name, x, gate_w, mode, dz=None, tm=1024):
    _, nh, blk, _ = gate_w.shape
    d = nh * blk
    t = x.shape[0]
    tm = _tile(t, tm)
    cols = lambda g, h: slice(g * d + h * blk, g * d + (h + 1) * blk)

    def fwd(x_ref, w_ref, o_ref):
        for h in range(nh):
            xh = x_ref[:, cols(0, h)].astype(BF16)
            for g in range(2):
                o_ref[:, cols(g, h)] = jnp.dot(xh, w_ref[g, h].astype(BF16), preferred_element_type=F32)

    def dx(dz_ref, w_ref, o_ref):
        for h in range(nh):
            o_ref[:, cols(0, h)] = sum(lax.dot_general(dz_ref[:, cols(g, h)].astype(BF16), w_ref[g, h].astype(BF16), _NT,
                                                       preferred_element_type=F32) for g in range(2))

    def dw(x_ref, dz_ref, o_ref):
        @pl.when(pl.program_id(0) == 0)
        def _():
            o_ref[...] = jnp.zeros_like(o_ref)

        for h in range(nh):
            xh = x_ref[:, cols(0, h)].astype(BF16)
            for g in range(2):
                o_ref[g, h] += lax.dot_general(xh, dz_ref[:, cols(g, h)].astype(BF16), _TN, preferred_element_type=F32)

    row = lambda width: pl.BlockSpec((tm, width), lambda i: (i, 0))
    whole = pl.BlockSpec(gate_w.shape, lambda i: (0, 0, 0, 0))
    kern, args, in_specs, out_spec, out_shape, sem = {
        "fwd": (fwd, (x, gate_w), [row(d), whole], row(2 * d), jax.ShapeDtypeStruct((t, 2 * d), F32), "parallel"),
        "dx": (dx, (x, gate_w), [row(2 * d), whole], row(d), jax.ShapeDtypeStruct((t, d), F32), "parallel"),
        "dw": (dw, (x, dz), [row(d), row(2 * d)], whole, jax.ShapeDtypeStruct(gate_w.shape, F32), "arbitrary"),
    }[mode]
    return pl.pallas_call(kern, name=name, grid=(t // tm,), in_specs=in_specs, out_specs=out_spec, out_shape=out_shape,
                          compiler_params=_PARAMS(dimension_semantics=(sem,)))(*args)


def _scan(name, a, b, seq, reverse=False, tm=256):
    t, d = a.shape
    tm = _tile(seq, tm)
    nblk = t // tm
    ntile = tm // SUBLANES

    def kern(a_ref, b_ref, h_ref, carry_h, carry_a):
        i = pl.program_id(0)
        blk = (nblk - 1 - i) if reverse else i
        edge = (((blk + 1) * tm) % seq == 0) if reverse else ((blk * tm) % seq == 0)

        @pl.when(edge)
        def _():
            carry_h[...] = jnp.zeros_like(carry_h)
            carry_a[...] = jnp.zeros_like(carry_a)

        def tile_step(j, c):
            jj = (ntile - 1 - j) if reverse else j
            rows = pl.ds(pl.multiple_of(jj * SUBLANES, SUBLANES), SUBLANES)
            a8 = a_ref[rows, :]
            b8 = b_ref[rows, :]
            h, an = c
            out = [None] * SUBLANES
            order = range(SUBLANES - 1, -1, -1) if reverse else range(SUBLANES)
            for r in order:
                if reverse:
                    h = b8[r:r + 1, :] + an * h
                    an = a8[r:r + 1, :]
                else:
                    h = a8[r:r + 1, :] * h + b8[r:r + 1, :]
                out[r] = h
            h_ref[rows, :] = jnp.concatenate(out, axis=0)
            return (h, an)

        h, an = lax.fori_loop(0, ntile, tile_step, (carry_h[...], carry_a[...]))
        carry_h[...] = h
        carry_a[...] = an

    idx = (lambda i: (nblk - 1 - i, 0)) if reverse else (lambda i: (i, 0))
    return pl.pallas_call(
        kern, name=name, grid=(nblk,),
        in_specs=[pl.BlockSpec((tm, d), idx), pl.BlockSpec((tm, d), idx)],
        out_specs=pl.BlockSpec((tm, d), idx),
        out_shape=jax.ShapeDtypeStruct((t, d), F32),
        scratch_shapes=[pltpu.VMEM((1, d), F32), pltpu.VMEM((1, d), F32)],
        compiler_params=_PARAMS(dimension_semantics=("arbitrary",)),
    )(a, b)


_NN = (((1,), (0,)), ((), ()))
_NT = (((1,), (1,)), ((), ()))
_TN = (((0,), (0,)), ((), ()))


def _dot1(a, b, dims):
    return lax.dot_general(a.astype(BF16), b.astype(BF16), dims, preferred_element_type=F32)


def _dot3(a, b, dims):
    a_hi, b_hi = a.astype(BF16), b.astype(BF16)
    a_lo, b_lo = (a - a_hi.astype(F32)).astype(BF16), (b - b_hi.astype(F32)).astype(BF16)
    dg = lambda p, q: lax.dot_general(p, q, dims, preferred_element_type=F32)
    return dg(a_hi, b_hi) + (dg(a_hi, b_lo) + dg(a_lo, b_hi))


def _make_bmm(dot):
    def make(dims, da_rule, db_rule):
        @jax.custom_vjp
        def f(a, b):
            return dot(a, b, dims)

        def fwd(a, b):
            return dot(a, b, dims), (a, b)

        def bwd(res, g):
            a, b = res
            return da_rule(a, b, g), db_rule(a, b, g)

        f.defvjp(fwd, bwd)
        return f

    return dict(nn=make(_NN, lambda a, b, g: dot(g, b, _NT), lambda a, b, g: dot(a, g, _TN)),
                nt=make(_NT, lambda a, b, g: dot(g, b, _NN), lambda a, b, g: dot(g, a, _TN)),
                tn=make(_TN, lambda a, b, g: dot(b, g, _NT), lambda a, b, g: dot(a, g, _NN)))


_BMM = {1: _make_bmm(_dot1), 3: _make_bmm(_dot3)}
_WKV_PASSES = dict(pair=1, read=1, inv=3, apply=1, write=1)


def _running_sum(x, reverse):
    c = x.shape[0]
    row = lax.broadcasted_iota(jnp.int32, x.shape, 0)
    k = 1
    while k < c:
        if reverse:
            x = x + jnp.where(row < c - k, pltpu.roll(x, c - k, 0), 0.0)
        else:
            x = x + jnp.where(row >= k, pltpu.roll(x, k, 0), 0.0)
        k *= 2
    return x


@jax.custom_vjp
def _cumsum_rows(x):
    return _running_sum(x, False)


_cumsum_rows.defvjp(lambda x: (_running_sum(x, False), None), lambda _, g: (_running_sum(g, True),))


@jax.custom_vjp
def _unit_lower_inverse(nl):
    c = nl[0].shape[0]
    mm = _BMM[_WKV_PASSES["inv"]]["nn"]
    eye = jnp.where(lax.broadcasted_iota(jnp.int32, (c, c), 0) == lax.broadcasted_iota(jnp.int32, (c, c), 1), 1.0, 0.0)
    inv = [eye + z for z in nl]
    p = nl
    for _ in range(c.bit_length() - 2):
        p = [mm(z, z) for z in p]
        inv = [i_ + mm(p_, i_) for i_, p_ in zip(inv, p)]
    return inv


def _unit_lower_inverse_fwd(nl):
    inv = _unit_lower_inverse(nl)
    return inv, inv


def _unit_lower_inverse_bwd(inv, g):
    mm = _BMM[_WKV_PASSES["inv"]]
    left = [mm["tn"](x, g_) for x, g_ in zip(inv, g)]
    return ([mm["nt"](l_, x) for l_, x in zip(left, inv)],)


_unit_lower_inverse.defvjp(_unit_lower_inverse_fwd, _unit_lower_inverse_bwd)


@jax.custom_vjp
def _kept_inverse(nl, inv):
    return inv


_kept_inverse.defvjp(lambda nl, inv: (inv, inv),
                     lambda inv, g: (_unit_lower_inverse_bwd(inv, g)[0], [jnp.zeros_like(x) for x in inv]))


def _wkv_chunk(r, lw, k, v, a, b, s0, kept_inv=None):
    c = r[0].shape[0]
    ti = lax.broadcasted_iota(jnp.int32, (c, 2 * c), 0)
    tj = lax.broadcasted_iota(jnp.int32, (c, 2 * c), 1)
    right = tj >= c
    tau = jnp.where(right, tj - c, tj)
    strict_left = jnp.logical_and(jnp.logical_not(right), tau < ti)[:, :c]
    strict_right = jnp.logical_and(right, tau < ti)
    incl = tau <= ti
    last = lax.broadcasted_iota(jnp.int32, r[0].shape, 0) == c - 1
    each = lambda f, *ls: [f(*z) for z in zip(*ls)]
    rows2 = lambda x, y: jnp.concatenate([x, y], axis=0)
    pair, read, inv_, apply_, write = (_BMM[_WKV_PASSES[role]] for role in ("pair", "read", "inv", "apply", "write"))
    cum = each(_cumsum_rows, lw)
    w_incl = each(jnp.exp, cum)
    w_inv = each(lambda z: jnp.exp(-z), cum)
    at = each(lambda a_, c_, l_: a_ * jnp.exp(c_ - l_), a, cum, lw)
    ar = each(rows2, at, each(jnp.multiply, r, w_incl))
    bk = each(rows2, each(jnp.multiply, b, w_inv), each(jnp.multiply, k, w_inv))
    pp = each(pair["nt"], ar, bk)
    sr = each(read["nt"], ar, s0)
    nl = [jnp.where(strict_left, z[:c, :c], 0.0) for z in pp]
    zero_v = each(lambda v_: rows2(jnp.zeros_like(v_), v_), v)
    rhs = each(lambda s, z, zv: s[:c] + apply_["nn"](jnp.where(strict_right, z[:c], 0.0), zv), sr, pp, zero_v)
    inv = _unit_lower_inverse(nl) if kept_inv is None else _kept_inverse(nl, kept_inv)
    ut = each(inv_["nn"], inv, rhs)
    uv = each(rows2, ut, v)
    y = each(lambda s, z, uv_: s[c:] + apply_["nn"](jnp.where(incl, z[c:], 0.0), uv_), sr, pp, uv)
    w_end = each(lambda z: jnp.exp(jnp.sum(jnp.where(last, z, 0.0), axis=0, keepdims=True)), cum)
    s1 = each(lambda s, uv_, bk_, w_: (s + write["tn"](uv_, bk_)) * w_, s0, uv, bk, w_end)
    return y, s1, inv


def _wkv_fwd(r, lw, k, v, a, b, seq, hb=16):
    t, d = r.shape
    n = RWKV_HEAD
    nh = d // n
    hb = min(hb, nh)
    chunk = min(WKV_CHUNK, seq)
    ncs = seq // chunk

    def kern(r_ref, lw_ref, k_ref, v_ref, a_ref, b_ref, y_ref, st_ref, inv_ref, s_scr):
        @pl.when(pl.program_id(2) == 0)
        def _():
            s_scr[...] = jnp.zeros_like(s_scr)

        heads = lambda ref: [ref[:, h * n:(h + 1) * n] for h in range(hb)]
        s0 = [s_scr[h] for h in range(hb)]
        y, s1, inv = _wkv_chunk(heads(r_ref), heads(lw_ref), heads(k_ref), heads(v_ref), heads(a_ref), heads(b_ref), s0)
        for h in range(hb):
            st_ref[0, h] = s0[h]
            inv_ref[0, h] = inv[h]
            y_ref[:, h * n:(h + 1) * n] = y[h]
            s_scr[h] = s1[h]

    vec = pl.BlockSpec((chunk, hb * n), lambda bb, g, c: (bb * ncs + c, g))
    per_chunk = lambda rows: pl.BlockSpec((1, hb, rows, rows), lambda bb, g, c: (bb * ncs + c, g, 0, 0))
    return pl.pallas_call(
        kern, name="wkv_fwd", grid=(t // seq, nh // hb, ncs), in_specs=[vec] * 6,
        out_specs=[vec, per_chunk(n), per_chunk(chunk)],
        out_shape=[jax.ShapeDtypeStruct((t, d), F32), jax.ShapeDtypeStruct((t // chunk, nh, n, n), F32),
                   jax.ShapeDtypeStruct((t // chunk, nh, chunk, chunk), F32)],
        scratch_shapes=[pltpu.VMEM((hb, n, n), F32)],
        compiler_params=_PARAMS(dimension_semantics=("parallel", "parallel", "arbitrary")),
    )(r, lw, k, v, a, b)


def _wkv_bwd(r, lw, k, v, a, b, st, inv, dy, seq, hb=16):
    t, d = r.shape
    n = RWKV_HEAD
    nh = d // n
    hb = min(hb, nh)
    chunk = min(WKV_CHUNK, seq)
    ncs = seq // chunk

    def kern(r_ref, lw_ref, k_ref, v_ref, a_ref, b_ref, st_ref, inv_ref, dy_ref,
             dr_ref, dlw_ref, dk_ref, dv_ref, da_ref, db_ref, ds_scr):
        @pl.when(pl.program_id(2) == 0)
        def _():
            ds_scr[...] = jnp.zeros_like(ds_scr)

        heads = lambda ref: [ref[:, h * n:(h + 1) * n] for h in range(hb)]
        kept = [inv_ref[0, h] for h in range(hb)]
        _, vjp = jax.vjp(lambda *args: _wkv_chunk(*args, kept_inv=kept)[:2],
                         heads(r_ref), heads(lw_ref), heads(k_ref), heads(v_ref), heads(a_ref), heads(b_ref),
                         [st_ref[0, h] for h in range(hb)])
        grads = vjp((heads(dy_ref), [ds_scr[h] for h in range(hb)]))
        for h in range(hb):
            for ref, g in zip((dr_ref, dlw_ref, dk_ref, dv_ref, da_ref, db_ref), grads[:6]):
                ref[:, h * n:(h + 1) * n] = g[h]
            ds_scr[h] = grads[6][h]

    vec = pl.BlockSpec((chunk, hb * n), lambda bb, g, c: (bb * ncs + ncs - 1 - c, g))
    per_chunk = lambda rows: pl.BlockSpec((1, hb, rows, rows), lambda bb, g, c: (bb * ncs + ncs - 1 - c, g, 0, 0))
    return pl.pallas_call(
        kern, name="wkv_bwd", grid=(t // seq, nh // hb, ncs),
        in_specs=[vec] * 6 + [per_chunk(n), per_chunk(chunk), vec],
        out_specs=[vec] * 6, out_shape=[jax.ShapeDtypeStruct((t, d), F32)] * 6,
        scratch_shapes=[pltpu.VMEM((hb, n, n), F32)],
        compiler_params=_PARAMS(dimension_semantics=("parallel", "parallel", "arbitrary")),
    )(r, lw, k, v, a, b, st, inv, dy)


def _softmax_rows(s):
    e = jnp.exp(s - jnp.max(s, axis=-1, keepdims=True))
    return e / jnp.sum(e, axis=-1, keepdims=True)


def _attn_fwd(q, kv, seq, mem_len, tq=1024):
    t, d = q.shape
    dh = d // MEM_HEADS
    scale = dh ** -0.5
    tq = _tile(seq, tq)
    nq = seq // tq

    def kern(q_ref, kv_ref, o_ref):
        for h in range(MEM_HEADS):
            cols = slice(h * dh, (h + 1) * dh)
            vcols = slice(d + h * dh, d + (h + 1) * dh)
            s = lax.dot_general(q_ref[:, cols], kv_ref[:, cols], _NT, preferred_element_type=F32) * scale
            p = _softmax_rows(s)
            o_ref[:, cols] = jnp.dot(p.astype(BF16), kv_ref[:, vcols], preferred_element_type=F32).astype(o_ref.dtype)

    return pl.pallas_call(
        kern, name="attn_fwd", grid=(t // seq, nq),
        in_specs=[pl.BlockSpec((tq, d), lambda b, i: (b * nq + i, 0)), pl.BlockSpec((mem_len, 2 * d), lambda b, i: (b, 0))],
        out_specs=pl.BlockSpec((tq, d), lambda b, i: (b * nq + i, 0)),
        out_shape=jax.ShapeDtypeStruct((t, d), BF16),
        compiler_params=_PARAMS(dimension_semantics=("parallel", "parallel")),
    )(q, kv)


def _attn_bwd(q, kv, do, seq, mem_len, tq=1024):
    t, d = q.shape
    dh = d // MEM_HEADS
    scale = dh ** -0.5
    tq = _tile(seq, tq)
    nq = seq // tq

    def kern(q_ref, kv_ref, do_ref, dq_ref, dkv_ref):
        @pl.when(pl.program_id(1) == 0)
        def _():
            dkv_ref[...] = jnp.zeros_like(dkv_ref)

        for h in range(MEM_HEADS):
            cols = slice(h * dh, (h + 1) * dh)
            vcols = slice(d + h * dh, d + (h + 1) * dh)
            qh, kh, vh, doh = q_ref[:, cols], kv_ref[:, cols], kv_ref[:, vcols], do_ref[:, cols]
            p = _softmax_rows(lax.dot_general(qh, kh, _NT, preferred_element_type=F32) * scale)
            dp = lax.dot_general(doh, vh, _NT, preferred_element_type=F32)
            ds = (p * (dp - jnp.sum(p * dp, axis=-1, keepdims=True)) * scale).astype(BF16)
            dq_ref[:, cols] = jnp.dot(ds, kh, preferred_element_type=F32).astype(dq_ref.dtype)
            dkv_ref[:, cols] += lax.dot_general(ds, qh, _TN, preferred_element_type=F32)
            dkv_ref[:, vcols] += lax.dot_general(p.astype(BF16), doh, _TN, preferred_element_type=F32)

    return pl.pallas_call(
        kern, name="attn_bwd", grid=(t // seq, nq),
        in_specs=[pl.BlockSpec((tq, d), lambda b, i: (b * nq + i, 0)), pl.BlockSpec((mem_len, 2 * d), lambda b, i: (b, 0)),
                  pl.BlockSpec((tq, d), lambda b, i: (b * nq + i, 0))],
        out_specs=[pl.BlockSpec((tq, d), lambda b, i: (b * nq + i, 0)), pl.BlockSpec((mem_len, 2 * d), lambda b, i: (b, 0))],
        out_shape=[jax.ShapeDtypeStruct((t, d), BF16), jax.ShapeDtypeStruct(kv.shape, F32)],
        compiler_params=_PARAMS(dimension_semantics=("parallel", "arbitrary")),
    )(q, kv, do)


def _rstd(x):
    return lax.rsqrt(jnp.mean(x * x, axis=-1, keepdims=True) + RMS_EPS)


def _rms(x, g):
    return x * _rstd(x) * g


def _rms_bwd(dy, x, g):
    rstd = _rstd(x)
    xhat = x * rstd
    dxhat = dy * g
    return rstd * (dxhat - xhat * jnp.mean(dxhat * xhat, axis=-1, keepdims=True)), dy * xhat


def _softplus(x):
    return jnp.maximum(x, 0.0) + jnp.log1p(jnp.exp(-jnp.abs(x)))


def _one_minus_exp(x):
    series = -x * (1.0 + x * (0.5 + x * (1.0 / 6.0 + x * (1.0 / 24.0 + x * (1.0 / 120.0)))))
    return jnp.where(x > -0.05, series, 1.0 - jnp.exp(x))


_GELU_C = 0.7978845608028654
_GELU_K = 0.044715


def _gelu(x):
    return 0.5 * x * (1.0 + jnp.tanh(_GELU_C * (x + _GELU_K * x * x * x)))


def _gelu_grad(x):
    th = jnp.tanh(_GELU_C * (x + _GELU_K * x * x * x))
    return 0.5 * (1.0 + th) + 0.5 * x * (1.0 - th * th) * _GELU_C * (1.0 + 3.0 * _GELU_K * x * x)


def _seg_sum(x, seg):
    mm = lambda v, dims: lax.dot_general(v.astype(BF16), seg, dims, preferred_element_type=F32)
    return mm(mm(x, _NN), _NT)


def _f32(v):
    return v.astype(F32)


def _norm_fwd(name, x, g, dtype=BF16):
    (hn,), _ = _rowwise(name, lambda rv, cv, pv, nv: ([_rms(_f32(rv[0]), cv[0])], []), [x], [g], [(x.shape[1], dtype)])
    return hn


def _resid_norm_fwd(name, x, t, g, bias=None, chain=None):
    d = x.shape[1]
    chain = {} if chain is None else chain

    def body(rv, cv, pv, nv):
        tt = rv[1] if bias is None else rv[1] + cv[1]
        y = rv[0] + _rms(tt, cv[0])
        if "gain" in chain:
            return [y, _rms(y, cv[-1])], []
        if "target" in chain:
            err = y - rv[2]
            return [err * (1.0 / d)], [err * err]
        return [y], []
    consts = [g] if bias is None else [g, bias]
    if "gain" in chain:
        (y, chain["hn"]), _ = _rowwise(name, body, [x, t], consts + [chain["gain"]], [(d, F32), (d, BF16)])
        return y
    if "target" in chain:
        (chain["dy"],), (chain["sq"],) = _rowwise(name, body, [x, t, chain["target"]], consts, [(d, F32)], [d])
        return None
    (y,), _ = _rowwise(name, body, [x, t], consts, [(d, F32)])
    return y


def _resid_norm_bwd(name, dxn, t, g, bias=None):
    def body(rv, cv, pv, nv):
        tt = rv[1] if bias is None else rv[1] + cv[1]
        dt, dg = _rms_bwd(rv[0], tt, cv[0])
        return [dt], [dg, dt]
    d = t.shape[1]
    (dt,), (dg, db) = _rowwise(name, body, [dxn, t], [g] if bias is None else [g, bias], [(d, BF16)], [d, d])
    return dt, dg.sum(0), db.sum(0)


def _prenorm_bwd(name, dxn, dhn, x, g):
    def body(rv, cv, pv, nv):
        dx, dg = _rms_bwd(_f32(rv[1]), rv[2], cv[0])
        return [rv[0] + dx], [dg]
    d = x.shape[1]
    (dx,), (dg,) = _rowwise(name, body, [dxn, dhn, x], [g], [(d, F32)], [d])
    return dx, dg.sum(0)


def _mlp_fwd(tag, x, g_pre, g_post, w_up, w_down, hn=None, chain=None):
    hn = _norm_fwd(tag + "_norm", x, g_pre) if hn is None else hn
    act = _mm(tag + "_up", hn, w_up, out_dtype=BF16, epilogue=lambda up: jnp.square(jnp.maximum(up, 0.0)))
    m = _mm(tag + "_down", act, w_down)
    y = _resid_norm_fwd(tag + "_res", x, m, g_post, chain=chain)
    return y, (x, hn, act, m)


def _mlp_bwd(tag, saved, dy, g_pre, g_post, w_up_t, w_down_t):
    x, hn, act, m = saved
    dm, dg_post, _ = _resid_norm_bwd(tag + "_dres", dy, m, g_post)
    dup = _mm(tag + "_dup", dm, w_down_t, out_dtype=BF16, extra=act,
              epilogue=lambda dact, act_: dact * 2.0 * jnp.sqrt(_f32(act_)))
    dw_down = _mm(tag + "_dwdown", act, dm, trans_a=True)
    dw_up = _mm(tag + "_dwup", hn, dup, trans_a=True)
    dhn = _mm(tag + "_dhn", dup, w_up_t)
    dx, dg_pre = _prenorm_bwd(tag + "_dnorm", dy, dhn, x, g_pre)
    return dx, dict(g_pre=dg_pre, g_post=dg_post, w_up=dw_up, w_down=dw_down)


def _xattn_fwd(tag, x, mem_n, g_pre, g_post, w_q, w_kv, w_o, seq, mem_len, hn=None, chain=None):
    hn = _norm_fwd(tag + "_norm", x, g_pre) if hn is None else hn
    q = _mm(tag + "_q", hn, w_q, out_dtype=BF16)
    kv = _mm(tag + "_kv", mem_n, w_kv, out_dtype=BF16)
    o = _attn_fwd(q, kv, seq, mem_len)
    c = _mm(tag + "_o", o, w_o)
    y = _resid_norm_fwd(tag + "_res", x, c, g_post, chain=chain)
    return y, (x, hn, q, kv, o, c)


def _xattn_bwd(tag, saved, dy, mem_n, g_pre, g_post, w_q_t, w_kv_t, w_o_t, seq, mem_len):
    x, hn, q, kv, o, c = saved
    dc, dg_post, _ = _resid_norm_bwd(tag + "_dres", dy, c, g_post)
    do = _mm(tag + "_do", dc, w_o_t, out_dtype=BF16)
    dw_o = _mm(tag + "_dwo", o, dc, trans_a=True)
    dq, dkv = _attn_bwd(q, kv, do, seq, mem_len)
    dw_q = _mm(tag + "_dwq", hn, dq, trans_a=True)
    dhn = _mm(tag + "_dhn", dq, w_q_t)
    dw_kv = _mm(tag + "_dwkv", mem_n, dkv, trans_a=True)
    dmem_n = _mm(tag + "_dmem", dkv, w_kv_t)
    dx, dg_pre = _prenorm_bwd(tag + "_dnorm", dy, dhn, x, g_pre)
    return dx, dmem_n, dict(g_pre=dg_pre, g_post=dg_post, w_q=dw_q, w_kv=dw_kv, w_o=dw_o)


def _lru_gates(z0, z1, gb0, gb1, sp):
    r = jax.nn.sigmoid(z0 + gb0)
    i = jax.nn.sigmoid(z1 + gb1)
    log_a = -LRU_C * r * sp
    a = jnp.exp(log_a)
    mult = jnp.sqrt(_one_minus_exp(2.0 * log_a))
    return r, i, a, mult


def _rglru_fwd(x, p, seq, chain=None):
    d = x.shape[1]
    hn = _norm_fwd("a_norm", x, p["g_pre"])
    proj = _mm("a_in", hn, p["w_in"])

    def conv_body(rv, cv, pv, nv):
        u = rv[0] + cv[0]
        halo = pv[0] + cv[0]
        i = pl.program_id(0)
        halo = jnp.where((i * rv[0].shape[0]) % seq == 0, jnp.zeros_like(halo), halo)
        conv = cv[2] + u * cv[1][CONV_WIDTH - 1:CONV_WIDTH]
        for tap in range(CONV_WIDTH - 1):
            conv = conv + _shift_down(u, halo, CONV_WIDTH - 1 - tap) * cv[1][tap:tap + 1]
        return [conv], []
    (conv,), _ = _rowwise("a_conv", conv_body, [(proj, d, 1)], [p["b_in_u"], p["conv_w"], p["conv_b"]], [(d, F32)],
                          prev=[0], seq=seq)
    z = _gates_mm("a_gate", conv, p["w_gate"], "fwd")

    def gate_body(rv, cv, pv, nv):
        r, i, a, mult = _lru_gates(rv[0], rv[1], cv[0], cv[1], _softplus(-cv[2]))
        return [a, mult * i * rv[2]], []
    (a, bb), _ = _rowwise("a_gates", gate_body, [(z, d, 0), (z, d, 1), conv], [p["gate_b0"], p["gate_b1"], p["lam"]],
                          [(d, F32), (d, F32)])
    h = _scan("a_scan", a, bb, seq)

    def hy_body(rv, cv, pv, nv):
        return [rv[0] * _gelu(rv[1] + cv[0])], []
    (hy,), _ = _rowwise("a_hy", hy_body, [h, (proj, d, 0)], [p["b_in_y"]], [(d, BF16)])
    out = _mm("a_out", hy, p["w_out"])
    y = _resid_norm_fwd("a_res", x, out, p["g_post"], bias=p["b_out"], chain=chain)
    return y, (x, hn, proj, conv, z, a, h, hy, out)


def _rglru_bwd(saved, dy, p, pt, seq):
    x, hn, proj, conv, z, a, h, hy, out = saved
    d = x.shape[1]
    dt, dg_post, db_out = _resid_norm_bwd("a_dres", dy, out, p["g_post"], bias=p["b_out"])
    dhy = _mm("a_dhy", dt, pt["w_out_t"])
    dw_out = _mm("a_dwout", hy, dt, trans_a=True)

    def dh_body(rv, cv, pv, nv):
        yb = rv[2] + cv[0]
        return [rv[0] * _gelu(yb), rv[0] * rv[1] * _gelu_grad(yb)], []
    (dh, dyb), _ = _rowwise("a_dh", dh_body, [dhy, h, (proj, d, 0)], [p["b_in_y"]], [(d, F32), (d, BF16)])
    g = _scan("a_rscan", a, dh, seq, reverse=True)

    def dgate_body(rv, cv, pv, nv):
        gg, hh, z0, z1, cnv = rv
        sp = _softplus(-cv[2])
        r, i, aa, mult = _lru_gates(z0, z1, cv[0], cv[1], sp)
        i_blk = pl.program_id(0)
        halo = jnp.where((i_blk * gg.shape[0]) % seq == 0, jnp.zeros_like(pv[0]), pv[0])
        da = gg * _shift_down(hh, halo, 1)
        dmult = gg * i * cnv
        di = gg * mult * cnv
        dconv = gg * mult * i
        dlog_a = da * aa - dmult * aa * aa / mult
        dz0 = dlog_a * (-LRU_C * sp) * r * (1.0 - r)
        dz1 = di * i * (1.0 - i)
        dsp = dlog_a * (-LRU_C * r)
        dlam = dsp * (-jax.nn.sigmoid(-cv[2]))
        return [jnp.concatenate([dz0, dz1], axis=1), dconv], [dz0, dz1, dlam]
    (dz, dconv1), (dgb0, dgb1, dlam) = _rowwise(
        "a_dgates", dgate_body, [g, h, (z, d, 0), (z, d, 1), conv], [p["gate_b0"], p["gate_b1"], p["lam"]],
        [(2 * d, BF16), (d, F32)], [d, d, d], prev=[1], seq=seq)
    dconv2 = _gates_mm("a_dconv", dz, p["w_gate"], "dx")
    dw_gate = _gates_mm("a_dwgate", conv, p["w_gate"], "dw", dz=dz)

    def dconv_body(rv, cv, pv, nv):
        dc1, dc2, pu, dyb_ = rv
        dc = dc1 + dc2
        dc_next = nv[0] + nv[1]
        u = pu + cv[0]
        i_blk = pl.program_id(0)
        halo = jnp.where((i_blk * u.shape[0]) % seq == 0, jnp.zeros_like(pv[0]), pv[0] + cv[0])
        du = dc * cv[1][CONV_WIDTH - 1:CONV_WIDTH]
        dws = []
        for tap in range(CONV_WIDTH - 1):
            k = CONV_WIDTH - 1 - tap
            du = du + _shift_up(dc, dc_next, k) * cv[1][tap:tap + 1]
            dws.append(dc * _shift_down(u, halo, k))
        dws.append(dc * u)
        return [jnp.concatenate([_f32(dyb_), du], axis=1)], dws + [dc, _f32(dyb_), du]
    (dproj,), accs = _rowwise(
        "a_dconvw", dconv_body, [dconv1, dconv2, (proj, d, 1), dyb], [p["b_in_u"], p["conv_w"]],
        [(2 * d, BF16)], [d] * (CONV_WIDTH + 3), prev=[2], nxt=[0, 1], seq=seq)
    dconv_w = jnp.stack([acc.sum(0) for acc in accs[:CONV_WIDTH]])
    dconv_b = accs[CONV_WIDTH].sum(0)
    db_in = jnp.concatenate([accs[CONV_WIDTH + 1].sum(0), accs[CONV_WIDTH + 2].sum(0)])
    dhn = _mm("a_dhn", dproj, pt["w_in_t"])
    dw_in = _mm("a_dwin", hn, dproj, trans_a=True)
    dx, dg_pre = _prenorm_bwd("a_dnorm", dy, dhn, x, p["g_pre"])
    grads = dict(g_pre=dg_pre, g_post=dg_post, b_out=db_out, w_out=dw_out, gate_b0=dgb0.sum(0), gate_b1=dgb1.sum(0),
                 lam=dlam.sum(0), w_gate=dw_gate, conv_w=dconv_w, conv_b=dconv_b, b_in=db_in, w_in=dw_in)
    return dx, grads


def _rwkv_prep(k, wl, za, w0, a0, k_k, k_a, seg):
    w_in = wl + w0
    e_w = jnp.exp(-_softplus(-w_in) - 0.5)
    a = jax.nn.sigmoid(za + a0)
    q = k * k_k
    norm = jnp.sqrt(_seg_sum(q * q, seg))
    n = jnp.maximum(norm, 1e-12)
    kk = q / n
    return w_in, e_w, a, norm, n, kk


def _rwkv_out(y, r, k2, v, gn_g, gn_b, r_k, seg):
    inv = 1.0 / RWKV_HEAD
    yc = y - _seg_sum(y, seg) * inv
    rstd = lax.rsqrt(_seg_sum(yc * yc, seg) * inv + RWKV_GN_EPS)
    yhat = yc * rstd
    s = _seg_sum(r * k2 * r_k, seg)
    return rstd, yhat, s, yhat * gn_g + gn_b + s * v


def _rwkv_fwd(x, p, seq, chain=None):
    t, d = x.shape
    nseq = t // seq

    def mix_body(rv, cv, pv, nv):
        hn = _rms(rv[0], cv[0])
        xx = _shift_down(hn, _rms(pv[0], cv[0]), 1) - hn
        return [hn] + [hn + xx * cv[1][c:c + 1] for c in range(6)], []
    (hn, xr, xw, xk, xv, xa, xg), _ = _rowwise("b_mix", mix_body, [x], [p["g_pre"], p["mu"]],
                                               [(d, F32)] + [(d, BF16)] * 6, prev=[0], seq=seq)
    r = _mm("b_r", xr, p["w_r"])
    k = _mm("b_k", xk, p["w_k"])
    v = _mm("b_v", xv, p["w_v"])
    lw = _mm("b_w1", xw, p["w1"])
    la = _mm("b_a1", xa, p["a1"], out_dtype=BF16)
    lg = _mm("b_g1", xg, p["g1"])
    (th,), _ = _rowwise("b_tanh", lambda rv, cv, pv, nv: ([jnp.tanh(rv[0])], []), [lw], [], [(lw.shape[1], BF16)])
    (sg,), _ = _rowwise("b_sig", lambda rv, cv, pv, nv: ([jax.nn.sigmoid(rv[0])], []), [lg], [], [(lg.shape[1], BF16)])
    wl = _mm("b_w2", th, p["w2"])
    za = _mm("b_a2", la, p["a2"])
    g = _mm("b_g2", sg, p["g2"])

    def prep_body(rv, cv, pv, nv):
        kk_, wl_, za_ = rv
        _, e_w, a, _, _, kk = _rwkv_prep(kk_, wl_, za_, cv[0], cv[1], cv[2], cv[3], cv[4])
        return [-e_w, kk_ * (1.0 + (a - 1.0) * cv[3]), -kk, kk * a], []
    (log_w, k2, rem_a, rem_b), _ = _rowwise("b_prep", prep_body, [k, wl, za],
                                            [p["w0"], p["a0"], p["k_k"], p["k_a"], p["seg"]], [(d, F32)] * 4, tm=256)
    rec_in = (r, log_w, k2, v, rem_a, rem_b)
    y, *states = _wkv_fwd(*rec_in, seq)

    def out_body(rv, cv, pv, nv):
        y_, r_, k2_, v_, g_ = rv
        _, _, _, out = _rwkv_out(y_, r_, k2_, v_, cv[0], cv[1], cv[2], cv[3])
        return [out * g_], []
    (og,), _ = _rowwise("b_out", out_body, [y, r, k2, v, g], [p["gn_g"], p["gn_b"], p["r_k"], p["seg"]], [(d, BF16)], tm=256)
    o = _mm("b_o", og, p["w_o"])
    res = _resid_norm_fwd("b_res", x, o, p["g_post"], chain=chain)
    return res, (x, hn, xr, xw, xk, xv, xa, xg, r, k, v, th, la, sg, wl, za, g, k2, rec_in, states, y, og, o)


def _rwkv_bwd(saved, dres, p, pt, seq):
    x, hn, xr, xw, xk, xv, xa, xg, r, k, v, th, la, sg, wl, za, g, k2, rec_in, states, y, og, o = saved
    t, d = x.shape
    nseq = t // seq
    do, dg_post, _ = _resid_norm_bwd("b_dres", dres, o, p["g_post"])
    dog = _mm("b_dog", do, pt["w_o_t"])
    dw_o = _mm("b_dwo", og, do, trans_a=True)

    def dout_body(rv, cv, pv, nv):
        dog_, y_, r_, k2_, v_, g_ = rv
        gn_g, gn_b, r_k, bd = cv
        inv = 1.0 / RWKV_HEAD
        rstd, yhat, s, out = _rwkv_out(y_, r_, k2_, v_, gn_g, gn_b, r_k, bd)
        dout = dog_ * g_
        ds = _seg_sum(dout * v_, bd)
        dyhat = dout * gn_g
        dy = rstd * (dyhat - _seg_sum(dyhat, bd) * inv - yhat * _seg_sum(dyhat * yhat, bd) * inv)
        return [dy, dog_ * out, dout * s, ds * k2_ * r_k, ds * r_ * r_k], [ds * r_ * k2_, dout * yhat, dout]
    (dy, dgate, dv_b, dr_b, dk2_b), (dr_k, dgn_g, dgn_b) = _rowwise(
        "b_dout", dout_body, [dog, y, r, k2, v, g], [p["gn_g"], p["gn_b"], p["r_k"], p["seg"]],
        [(d, F32), (d, BF16), (d, F32), (d, F32), (d, F32)], [d, d, d], tm=256)
    dr_rec, dlw_rec, dk2_rec, dv_rec, da_rec, db_rec = _wkv_bwd(*rec_in, *states, dy, seq)

    def dprep_body(rv, cv, pv, nv):
        dr_rec_, dlw_rec_, dk2_rec_, dv_rec_, da_rec_, db_rec_, dr_b_, dk2_b_, dv_b_, k_, wl_, za_ = rv
        w0, a0, k_k, k_a, bd = cv
        w_in, e_w, a, norm, n, kk = _rwkv_prep(k_, wl_, za_, w0, a0, k_k, k_a, bd)
        dk2 = dk2_rec_ + dk2_b_
        dkk = db_rec_ * a - da_rec_
        da = db_rec_ * kk + dk2 * k_ * k_a
        dq = jnp.where(norm > 1e-12, dkk - kk * _seg_sum(kk * dkk, bd), dkk) / n
        dk = dk2 * (1.0 + (a - 1.0) * k_a) + dq * k_k
        dza = da * a * (1.0 - a)
        dwl = dlw_rec_ * (-e_w) * jax.nn.sigmoid(-w_in)
        return [dr_rec_ + dr_b_, dk, dv_rec_ + dv_b_, dza, dwl], [dk2 * k_ * (a - 1.0), dq * k_, dza, dwl]
    (dr, dk, dv, dza, dwl), (dk_a, dk_k, da0, dw0) = _rowwise(
        "b_dprep", dprep_body, [dr_rec, dlw_rec, dk2_rec, dv_rec, da_rec, db_rec, dr_b, dk2_b, dv_b, k, wl, za],
        [p["w0"], p["a0"], p["k_k"], p["k_a"], p["seg"]], [(d, BF16)] * 5, [d] * 4, tm=256)

    dw_r = _mm("b_dwr", xr, dr, trans_a=True)
    dw_k = _mm("b_dwk", xk, dk, trans_a=True)
    dw_v = _mm("b_dwv", xv, dv, trans_a=True)
    dxr = _mm("b_dxr", dr, pt["w_r_t"])
    dxk = _mm("b_dxk", dk, pt["w_k_t"])
    dxv = _mm("b_dxv", dv, pt["w_v_t"])
    da2 = _mm("b_da2", la, dza, trans_a=True)
    dla = _mm("b_dla", dza, pt["a2_t"], out_dtype=BF16)
    da1 = _mm("b_da1", xa, dla, trans_a=True)
    dxa = _mm("b_dxa", dla, pt["a1_t"])
    dw2 = _mm("b_dw2", th, dwl, trans_a=True)
    dth = _mm("b_dth", dwl, pt["w2_t"])
    (dzw,), _ = _rowwise("b_dtanh", lambda rv, cv, pv, nv: ([rv[0] * (1.0 - _f32(rv[1]) * _f32(rv[1]))], []),
                         [dth, th], [], [(th.shape[1], BF16)])
    dw1 = _mm("b_dw1", xw, dzw, trans_a=True)
    dxw = _mm("b_dxw", dzw, pt["w1_t"])
    dg2 = _mm("b_dg2", sg, dgate, trans_a=True)
    dsg = _mm("b_dsg", dgate, pt["g2_t"])
    (dzg,), _ = _rowwise("b_dsig", lambda rv, cv, pv, nv: ([rv[0] * _f32(rv[1]) * (1.0 - _f32(rv[1]))], []),
                         [dsg, sg], [], [(sg.shape[1], BF16)])
    dg1 = _mm("b_dg1", xg, dzg, trans_a=True)
    dxg = _mm("b_dxg", dzg, pt["g1_t"])

    def dmix_body(rv, cv, pv, nv):
        hn_ = rv[0]
        dxs = rv[1:]
        mu = cv[0]
        xx = _shift_down(hn_, pv[0], 1) - hn_
        dsum = dxs[0]
        dxx = dxs[0] * mu[0:1]
        dxx_next = nv[0] * mu[0:1]
        for c in range(1, 6):
            dsum = dsum + dxs[c]
            dxx = dxx + dxs[c] * mu[c:c + 1]
            dxx_next = dxx_next + nv[c] * mu[c:c + 1]
        return [dsum - dxx + _shift_up(dxx, dxx_next, 1)], [dxs[c] * xx for c in range(6)]
    (dhn,), dmu = _rowwise("b_dmix", dmix_body, [hn, dxr, dxw, dxk, dxv, dxa, dxg], [p["mu"]], [(d, F32)], [d] * 6,
                           prev=[0], nxt=[1, 2, 3, 4, 5, 6], seq=seq, tm=256)
    dx, dg_pre = _prenorm_bwd("b_dnorm", dres, dhn, x, p["g_pre"])
    grads = dict(g_pre=dg_pre, g_post=dg_post, mu=jnp.stack([m.sum(0) for m in dmu]), w_r=dw_r, w_k=dw_k, w_v=dw_v,
                 w0=dw0.sum(0), w1=dw1, w2=dw2, a0=da0.sum(0), a1=da1, a2=da2, g1=dg1, g2=dg2, k_k=dk_k.sum(0),
                 k_a=dk_a.sum(0), r_k=dr_k.sum(0), gn_g=dgn_g.sum(0), gn_b=dgn_b.sum(0), w_o=dw_o)
    return dx, grads


_WEIGHTS = ['ln_gains', 'mem_norm', 'a_conv_w', 'a_conv_b', 'a_w_in', 'a_b_in', 'a_gate_w', 'a_gate_b', 'a_lambda', 'a_w_out',
            'a_b_out', 'b_mu', 'b_w_rkv', 'b_w0', 'b_w1', 'b_w2', 'b_a0', 'b_a1', 'b_a2', 'b_g1', 'b_g2', 'b_k_k', 'b_k_a',
            'b_r_k', 'b_gn_g', 'b_gn_b', 'b_w_o', 'c_w_q', 'c_w_kv', 'c_w_o', 'm_w_up', 'm_w_down']
_SHARD_AXIS = dict(ln_gains=2, mem_norm=None, a_conv_w=2, a_conv_b=None, a_w_in=2, a_b_in=None, a_gate_w=3, a_gate_b=3,
                   a_lambda=None, a_w_out=1, a_b_out=None, b_mu=2, b_w_rkv=2, b_w0=1, b_w1=1, b_w2=2, b_a0=1, b_a1=1, b_a2=2,
                   b_g1=1, b_g2=2, b_k_k=1, b_k_a=1, b_r_k=None, b_gn_g=1, b_gn_b=1, b_w_o=1, c_w_q=1, c_w_kv=2, c_w_o=1,
                   m_w_up=2, m_w_down=1)
_MATRICES = ['a_w_in', 'a_gate_w', 'a_w_out', 'b_w_rkv', 'b_w1', 'b_w2', 'b_a1', 'b_a2', 'b_g1', 'b_g2', 'b_w_o', 'c_w_q',
             'c_w_kv', 'c_w_o', 'm_w_up', 'm_w_down']
_SHARDED = [n for n in _WEIGHTS if _SHARD_AXIS[n] is not None]
_VECTORS = [n for n in _SHARDED if n not in _MATRICES]
_REPLICATED = [n for n in _WEIGHTS if _SHARD_AXIS[n] is None]
N_XY = 4
N_DEV = 8
PACK_W = 1024
PACK_ROWS = 256


def _pack(arrs, dtype, row_mult=PACK_ROWS):
    parts = []
    rows = 0
    for a in arrs:
        n = a.size
        r = -(-n // PACK_W)
        parts.append(jnp.pad(a.reshape(-1).astype(dtype), (0, r * PACK_W - n)))
        rows += r
    pad_rows = -(-rows // row_mult) * row_mult - rows
    if pad_rows:
        parts.append(jnp.zeros((pad_rows * PACK_W,), dtype))
    return jnp.concatenate(parts).reshape(-1, PACK_W)


def _unpack(flat, shapes):
    out = []
    row = 0
    for shp in shapes:
        n = 1
        for s in shp:
            n *= s
        r = -(-n // PACK_W)
        out.append(flat[row:row + r].reshape(-1)[:n].reshape(shp))
        row += r
    return out


_ANY = pl.BlockSpec(memory_space=pl.ANY)


def _xy_peers():
    x, y = lax.axis_index("x"), lax.axis_index("y")
    return [(1 - x, y), (x, 1 - y), (1 - x, 1 - y)]


def _all_gather_xy(wm, wv):
    half = wm.shape[0] // 2

    def body(wm_ref, wv_ref, gm_ref, gv_ref, send_sems, recv_sems, local_sems):
        x, y, c = lax.axis_index("x"), lax.axis_index("y"), lax.axis_index("c")
        me = 2 * x + y
        mine = pl.ds(pl.multiple_of(c * half, SUBLANES), half)
        other = pl.ds(pl.multiple_of((1 - c) * half, SUBLANES), half)
        local = [pltpu.make_async_copy(wm_ref, gm_ref.at[me], local_sems.at[0]),
                 pltpu.make_async_copy(wv_ref, gv_ref.at[me], local_sems.at[1])]
        for cp in local:
            cp.start()
        sends, lands, passes, from_sibling = [], [], [], []
        for j, (px, py) in enumerate(_xy_peers()):
            peer = 2 * px + py
            ici = functools.partial(pltpu.make_async_remote_copy, device_id=(px, py, c), device_id_type=MESH)
            sends.append(ici(src_ref=wm_ref.at[mine], dst_ref=gm_ref.at[me, mine], send_sem=send_sems.at[j], recv_sem=recv_sems.at[j]))
            lands.append(ici(src_ref=wm_ref.at[mine], dst_ref=gm_ref.at[peer, mine], send_sem=send_sems.at[j], recv_sem=recv_sems.at[j]))
            sends.append(ici(src_ref=wv_ref, dst_ref=gv_ref.at[me], send_sem=send_sems.at[3 + j], recv_sem=recv_sems.at[3 + j]))
            lands.append(ici(src_ref=wv_ref, dst_ref=gv_ref.at[peer], send_sem=send_sems.at[3 + j], recv_sem=recv_sems.at[3 + j]))
            d2d = functools.partial(pltpu.make_async_remote_copy, send_sem=send_sems.at[6 + j], recv_sem=recv_sems.at[6 + j],
                                    device_id=(x, y, 1 - c), device_id_type=MESH)
            passes.append(d2d(src_ref=gm_ref.at[peer, mine], dst_ref=gm_ref.at[peer, mine]))
            from_sibling.append(d2d(src_ref=gm_ref.at[peer, other], dst_ref=gm_ref.at[peer, other]))
        for cp in sends:
            cp.start()
        for j in range(N_XY - 1):
            lands[2 * j].wait_recv()
            passes[j].start()
        for j in range(N_XY - 1):
            lands[2 * j + 1].wait_recv()
        for cp in from_sibling:
            cp.wait_recv()
        for cp in sends + passes:
            cp.wait_send()
        for cp in local:
            cp.wait()

    return pl.pallas_call(
        body, name="all_gather_weights",
        in_specs=[_ANY, _ANY], out_specs=[_ANY, _ANY],
        out_shape=[jax.ShapeDtypeStruct((N_XY,) + wm.shape, wm.dtype), jax.ShapeDtypeStruct((N_XY,) + wv.shape, wv.dtype)],
        scratch_shapes=[pltpu.SemaphoreType.DMA((9,)), pltpu.SemaphoreType.DMA((9,)), pltpu.SemaphoreType.DMA((2,))],
    )(wm, wv)


_HBM = pl.BlockSpec(memory_space=pltpu.HBM)
_SEM = pl.BlockSpec(memory_space=pltpu.SEMAPHORE)
_SPLIT_COPY = functools.partial(pltpu.CompilerParams, has_side_effects=pltpu.SideEffectType.DATAFLOW_SIDE_EFFECTING)


def _gather_xy_start(name, buf, after):
    def body(src_ref, land_ref, after_ref, send_sems, recv_sems, src_thru, land_thru, token):
        x, y, c = lax.axis_index("x"), lax.axis_index("y"), lax.axis_index("c")
        for j, (px, py) in enumerate(_xy_peers()):
            pltpu.make_async_remote_copy(src_ref=src_ref, dst_ref=land_ref.at[2 * x + y], send_sem=send_sems.at[j],
                                         recv_sem=recv_sems.at[j], device_id=(px, py, c), device_id_type=MESH).start()
        token[...] = jnp.zeros_like(token)

    n_peers = N_XY - 1
    land = pltpu.with_memory_space_constraint(lax.empty((N_XY,) + buf.shape, buf.dtype), pltpu.HBM)
    return pl.pallas_call(
        body, name=name,
        out_shape=(pltpu.SemaphoreType.DMA((n_peers,)), pltpu.SemaphoreType.DMA((n_peers,)), pltpu.HBM(buf.shape, buf.dtype),
                   pltpu.HBM(land.shape, buf.dtype), jax.ShapeDtypeStruct((SUBLANES, LANES), F32)),
        in_specs=(_HBM, _HBM, _ANY), out_specs=(_SEM, _SEM, _HBM, _HBM, pl.BlockSpec(memory_space=pltpu.VMEM)),
        input_output_aliases={0: 2, 1: 3}, compiler_params=_SPLIT_COPY(),
    )(pltpu.with_memory_space_constraint(buf, pltpu.HBM), land, after)


def _gather_xy_wait(name, send_sems, recv_sems, src_thru, land_thru, after):
    def body(src_ref, land_ref, send_sems, recv_sems, after_ref, src_dead, got_ref):
        c = lax.axis_index("c")
        for j, (px, py) in enumerate(_xy_peers()):
            cp = pltpu.make_async_remote_copy(src_ref=src_ref, dst_ref=land_ref.at[2 * px + py], send_sem=send_sems.at[j],
                                              recv_sem=recv_sems.at[j], device_id=(px, py, c), device_id_type=MESH)
            cp.wait_send()
            cp.wait_recv()

    return pl.pallas_call(
        body, name=name,
        out_shape=(pltpu.HBM(src_thru.shape, src_thru.dtype), pltpu.HBM(land_thru.shape, land_thru.dtype)),
        in_specs=(_HBM, _HBM, _SEM, _SEM, _ANY), out_specs=(_HBM, _HBM),
        input_output_aliases={0: 0, 1: 1}, compiler_params=_SPLIT_COPY(),
    )(src_thru, land_thru, send_sems, recv_sems, after)[1]


def _exchange_xy_start(name, gsend, after):
    def body(src_ref, land_ref, after_ref, send_sems, recv_sems, src_thru, land_thru, token):
        c = lax.axis_index("c")
        for j, (px, py) in enumerate(_xy_peers()):
            pltpu.make_async_remote_copy(src_ref=src_ref.at[2 * px + py], dst_ref=land_ref.at[j], send_sem=send_sems.at[j],
                                         recv_sem=recv_sems.at[j], device_id=(px, py, c), device_id_type=MESH).start()
        token[...] = jnp.zeros_like(token)

    n_peers = N_XY - 1
    land = pltpu.with_memory_space_constraint(lax.empty((n_peers,) + gsend.shape[1:], gsend.dtype), pltpu.HBM)
    return pl.pallas_call(
        body, name=name,
        out_shape=(pltpu.SemaphoreType.DMA((n_peers,)), pltpu.SemaphoreType.DMA((n_peers,)), pltpu.HBM(gsend.shape, gsend.dtype),
                   pltpu.HBM(land.shape, gsend.dtype), jax.ShapeDtypeStruct((SUBLANES, LANES), F32)),
        in_specs=(_HBM, _HBM, _ANY), out_specs=(_SEM, _SEM, _HBM, _HBM, pl.BlockSpec(memory_space=pltpu.VMEM)),
        input_output_aliases={0: 2, 1: 3}, compiler_params=_SPLIT_COPY(),
    )(pltpu.with_memory_space_constraint(gsend, pltpu.HBM), land, after)


def _exchange_xy_wait(name, send_sems, recv_sems, src_thru, land_thru, after):
    def body(src_ref, land_ref, send_sems, recv_sems, after_ref, src_dead, got_ref):
        c = lax.axis_index("c")
        for j, (px, py) in enumerate(_xy_peers()):
            cp = pltpu.make_async_remote_copy(src_ref=src_ref.at[2 * px + py], dst_ref=land_ref.at[j], send_sem=send_sems.at[j],
                                              recv_sem=recv_sems.at[j], device_id=(px, py, c), device_id_type=MESH)
            cp.wait_send()
            cp.wait_recv()

    return pl.pallas_call(
        body, name=name,
        out_shape=(pltpu.HBM(src_thru.shape, src_thru.dtype), pltpu.HBM(land_thru.shape, land_thru.dtype)),
        in_specs=(_HBM, _HBM, _SEM, _SEM, _ANY), out_specs=(_HBM, _HBM),
        input_output_aliases={0: 0, 1: 1}, compiler_params=_SPLIT_COPY(),
    )(src_thru, land_thru, send_sems, recv_sems, after)[1]


def _exchange_xy(gsend):
    def body(gs_ref, recv_ref, send_sems, recv_sems):
        c = lax.axis_index("c")
        sends = []
        for j, (px, py) in enumerate(_xy_peers()):
            sends.append(pltpu.make_async_remote_copy(
                src_ref=gs_ref.at[2 * px + py], dst_ref=recv_ref.at[j], send_sem=send_sems.at[j], recv_sem=recv_sems.at[j],
                device_id=(px, py, c), device_id_type=MESH))
        for cp in sends:
            cp.start()
        for cp in sends:
            cp.wait_recv()
        for cp in sends:
            cp.wait_send()

    return pl.pallas_call(
        body, name="exchange_grads",
        in_specs=[_ANY], out_specs=_ANY,
        out_shape=jax.ShapeDtypeStruct((N_XY - 1,) + gsend.shape[1:], gsend.dtype),
        scratch_shapes=[pltpu.SemaphoreType.DMA((3,)), pltpu.SemaphoreType.DMA((3,))],
    )(gsend)


def _swap_with_sibling(name, part):
    def body(p_ref, got_ref, send_sem, recv_sem):
        x, y, c = lax.axis_index("x"), lax.axis_index("y"), lax.axis_index("c")
        cp = pltpu.make_async_remote_copy(src_ref=p_ref, dst_ref=got_ref, send_sem=send_sem, recv_sem=recv_sem,
                                          device_id=(x, y, 1 - c), device_id_type=MESH)
        cp.start()
        cp.wait_recv()
        cp.wait_send()

    return pl.pallas_call(
        body, name=name,
        in_specs=[_ANY], out_specs=_ANY, out_shape=jax.ShapeDtypeStruct(part.shape, part.dtype),
        scratch_shapes=[pltpu.SemaphoreType.DMA, pltpu.SemaphoreType.DMA],
    )(part)


def _all_gather_all(vec):
    def body(v_ref, out_ref, send_sems, recv_sems, local_sem):
        x, y, c = lax.axis_index("x"), lax.axis_index("y"), lax.axis_index("c")
        me = 4 * x + 2 * y + c
        local = pltpu.make_async_copy(v_ref, out_ref.at[me], local_sem)
        local.start()
        sends, recvs = [], []
        for f in range(1, N_DEV):
            fx, fy, fc = (f >> 2) & 1, (f >> 1) & 1, f & 1
            px = (1 - x) if fx else x
            py = (1 - y) if fy else y
            pc = (1 - c) if fc else c
            mk = functools.partial(pltpu.make_async_remote_copy, src_ref=v_ref, send_sem=send_sems.at[f - 1],
                                   recv_sem=recv_sems.at[f - 1], device_id=(px, py, pc), device_id_type=MESH)
            sends.append(mk(dst_ref=out_ref.at[me]))
            recvs.append(mk(dst_ref=out_ref.at[4 * px + 2 * py + pc]))
        for cp in sends:
            cp.start()
        for cp in recvs:
            cp.wait_recv()
        for cp in sends:
            cp.wait_send()
        local.wait()

    return pl.pallas_call(
        body, name="all_gather_replicated",
        in_specs=[_ANY], out_specs=_ANY, out_shape=jax.ShapeDtypeStruct((N_DEV,) + vec.shape, vec.dtype),
        scratch_shapes=[pltpu.SemaphoreType.DMA((N_DEV - 1,)), pltpu.SemaphoreType.DMA((N_DEV - 1,)), pltpu.SemaphoreType.DMA],
    )(vec)


def _adamw(g, w, m, v):
    m2 = ADAM_B1 * m + (1.0 - ADAM_B1) * g
    v2 = ADAM_B2 * v + (1.0 - ADAM_B2) * g * g
    m_hat = m2 / (1.0 - ADAM_B1 ** ADAM_STEP)
    v_hat = v2 / (1.0 - ADAM_B2 ** ADAM_STEP)
    return -ADAM_LR * (m_hat / (jnp.sqrt(v_hat) + ADAM_EPS) + ADAM_WD * w), m2, v2


def _sum_contributions(name, own, recv):
    def body(rv, cv, pv, nv):
        return [((rv[0] + _f32(rv[1])) + _f32(rv[2])) + _f32(rv[3])], []
    stacked = recv.reshape(-1, PACK_W)
    (part,), _ = _rowwise(name, body, [own] + [(stacked, PACK_W, 0, j * own.shape[0]) for j in range(N_XY - 1)], [],
                          [(PACK_W, F32)])
    return part


def _adamw_sharded(name, part, sib, first, w, m, v, w_first=0, n_rows=None):
    def body(rv, cv, pv, nv):
        g = rv[3] + rv[4]
        return [g, *_adamw(g, rv[0], rv[1], rv[2])], []
    n_rows = w.shape[0] if n_rows is None else n_rows
    rows = [(z, PACK_W, 0, w_first) for z in (w, m, v)] + [(part, PACK_W, 0, first), (sib, PACK_W, 0, first)]
    outs, _ = _rowwise(name, body, rows, [], [(PACK_W, F32)] * 4, tm=256, n_rows=n_rows)
    return outs


def _adamw_replicated(parts, w, m, v):
    def body(rv, cv, pv, nv):
        g = rv[0]
        for i in range(1, N_DEV):
            g = g + rv[i]
        return [g, *_adamw(g, rv[N_DEV], rv[N_DEV + 1], rv[N_DEV + 2])], []
    outs, _ = _rowwise("adamw_replicated", body, [parts[i] for i in range(N_DEV)] + [w, m, v], [], [(PACK_W, F32)] * 4)
    return outs


def _row(v):
    return v.reshape(1, -1).astype(F32)


def _local_step(x3, mem3, target3, fw, late_weights=None, on_grads=None):
    def with_late(stage, after):
        if late_weights is None:
            return fw
        got = late_weights(stage, after)
        return {**fw, **{n: ({**fw.get(n, {}), **v} if isinstance(v, dict) else v) for n, v in got.items()}}

    nseq, seq, d = x3.shape
    mem_len = mem3.shape[1]
    t = nseq * seq
    x0 = x3.reshape(t, d)
    mem2 = mem3.reshape(nseq * mem_len, d)
    target = target3.reshape(t, d)
    ln = fw["ln_gains"]
    gains = [[_row(ln[i, j]) for j in range(6)] for i in range(2)]
    nh = d // RWKV_HEAD
    seg = (jnp.arange(d)[:, None] // RWKV_HEAD == jnp.arange(LANES)[None, :]).astype(BF16)

    w_gate = fw["a_gate_w"][0]
    pa = dict(g_pre=gains[0][0], g_post=gains[0][1], w_in=fw["a_w_in"][0], b_in_y=_row(fw["a_b_in"][0, :d]),
              b_in_u=_row(fw["a_b_in"][0, d:]), conv_w=fw["a_conv_w"][0].astype(F32), conv_b=_row(fw["a_conv_b"][0]),
              w_gate=w_gate, gate_b0=_row(fw["a_gate_b"][0, 0]), gate_b1=_row(fw["a_gate_b"][0, 1]), lam=_row(fw["a_lambda"][0]),
              w_out=fw["a_w_out"][0], b_out=_row(fw["a_b_out"][0]))
    pta = dict(w_in_t=_t(pa["w_in"]), w_out_t=_t(pa["w_out"]))
    mem_g = _row(fw["mem_norm"])

    mem_n = _norm_fwd("mem_norm", mem2, mem_g)
    to_c0, to_m0, to_c1, to_m1, to_loss = ({"gain": gains[0][2]}, {"gain": gains[0][4]}, {"gain": gains[1][2]},
                                           {"gain": gains[1][4]}, {"target": target})
    x1, sv_a = _rglru_fwd(x0, pa, seq, chain=to_c0)
    fw = with_late("cm0", x1)
    x2, sv_c0 = _xattn_fwd("c0", x1, mem_n, gains[0][2], gains[0][3], fw["c_w_q"][0], fw["c_w_kv"][0], fw["c_w_o"][0], seq, mem_len,
                           hn=to_c0["hn"], chain=to_m0)
    x3_, sv_m0 = _mlp_fwd("m0", x2, gains[0][4], gains[0][5], fw["m_w_up"][0], fw["m_w_down"][0], hn=to_m0["hn"])
    fw = with_late("layer1", x3_)
    pb = dict(g_pre=gains[1][0], g_post=gains[1][1], mu=fw["b_mu"][0].astype(F32), w_r=fw["b_w_rkv"][0, 0],
              w_k=fw["b_w_rkv"][0, 1], w_v=fw["b_w_rkv"][0, 2], w0=_row(fw["b_w0"][0]), w1=fw["b_w1"][0], w2=fw["b_w2"][0],
              a0=_row(fw["b_a0"][0]), a1=fw["b_a1"][0], a2=fw["b_a2"][0], g1=fw["b_g1"][0], g2=fw["b_g2"][0],
              k_k=_row(fw["b_k_k"][0]), k_a=_row(fw["b_k_a"][0]), r_k=_row(fw["b_r_k"][0]), gn_g=_row(fw["b_gn_g"][0]),
              gn_b=_row(fw["b_gn_b"][0]), w_o=fw["b_w_o"][0], seg=seg)
    ptb = {k + "_t": _t(pb[k]) for k in ("w_r", "w_k", "w_v", "w_o", "w1", "w2", "a1", "a2", "g1", "g2")}
    x4, sv_b = _rwkv_fwd(x3_, pb, seq, chain=to_c1)
    x5, sv_c1 = _xattn_fwd("c1", x4, mem_n, gains[1][2], gains[1][3], fw["c_w_q"][1], fw["c_w_kv"][1], fw["c_w_o"][1], seq, mem_len,
                           hn=to_c1["hn"], chain=to_m1)
    _, sv_m1 = _mlp_fwd("m1", x5, gains[1][4], gains[1][5], fw["m_w_up"][1], fw["m_w_down"][1], hn=to_m1["hn"], chain=to_loss)
    dx = to_loss["dy"]
    loss_part = 0.5 / d * jnp.sum(to_loss["sq"])

    dx, g_m1 = _mlp_bwd("m1", sv_m1, dx, gains[1][4], gains[1][5], _t(fw["m_w_up"][1]), _t(fw["m_w_down"][1]))
    dx, dmem1, g_c1 = _xattn_bwd("c1", sv_c1, dx, mem_n, gains[1][2], gains[1][3], _t(fw["c_w_q"][1]), _t(fw["c_w_kv"][1]),
                                 _t(fw["c_w_o"][1]), seq, mem_len)
    dx, g_b = _rwkv_bwd(sv_b, dx, pb, ptb, seq)
    grads_b = dict(
        b_mu=g_b["mu"][None], b_w_rkv=jnp.stack([g_b["w_r"], g_b["w_k"], g_b["w_v"]])[None], b_w0=g_b["w0"][None],
        b_w1=g_b["w1"][None], b_w2=g_b["w2"][None], b_a0=g_b["a0"][None], b_a1=g_b["a1"][None], b_a2=g_b["a2"][None],
        b_g1=g_b["g1"][None], b_g2=g_b["g2"][None], b_k_k=g_b["k_k"][None], b_k_a=g_b["k_a"][None],
        b_r_k=g_b["r_k"].reshape(1, nh, RWKV_HEAD), b_gn_g=g_b["gn_g"][None], b_gn_b=g_b["gn_b"][None], b_w_o=g_b["w_o"][None])
    g_m0_pre = gains[0][4]
    if on_grads is not None:
        layer1 = dict(c_w_q=g_c1["w_q"], c_w_kv=g_c1["w_kv"], c_w_o=g_c1["w_o"], m_w_up=g_m1["w_up"], m_w_down=g_m1["w_down"])
        g_m0_pre = g_m0_pre + on_grads("layer1", {**grads_b, **{n: [None, g] for n, g in layer1.items()}})[0, 0]
    dx, g_m0 = _mlp_bwd("m0", sv_m0, dx, g_m0_pre, gains[0][5], _t(fw["m_w_up"][0]), _t(fw["m_w_down"][0]))
    dx, dmem0, g_c0 = _xattn_bwd("c0", sv_c0, dx, mem_n, gains[0][2], gains[0][3], _t(fw["c_w_q"][0]), _t(fw["c_w_kv"][0]),
                                 _t(fw["c_w_o"][0]), seq, mem_len)
    if on_grads is not None:
        cm0 = dict(c_w_q=g_c0["w_q"], c_w_kv=g_c0["w_kv"], c_w_o=g_c0["w_o"], m_w_up=g_m0["w_up"], m_w_down=g_m0["w_down"])
        pa = {**pa, "g_post": pa["g_post"] + on_grads("cm0", {n: [g, None] for n, g in cm0.items()})[0, 0]}
    dx, g_a = _rglru_bwd(sv_a, dx, pa, pta, seq)

    def dmem_body(rv, cv, pv, nv):
        _, dg = _rms_bwd(rv[1] + rv[2], rv[0], cv[0])
        return [], [dg]
    _, (dmem_g,) = _rowwise("mem_norm_grad", dmem_body, [mem2, dmem0, dmem1], [mem_g], [], [d])

    lru_heads = fw["a_gate_w"].shape[2]
    blk = d // lru_heads
    grads = dict(
        ln_gains=jnp.stack([jnp.stack([g_a["g_pre"], g_a["g_post"], g_c0["g_pre"], g_c0["g_post"], g_m0["g_pre"], g_m0["g_post"]]),
                            jnp.stack([g_b["g_pre"], g_b["g_post"], g_c1["g_pre"], g_c1["g_post"], g_m1["g_pre"], g_m1["g_post"]])]),
        mem_norm=dmem_g.sum(0),
        a_conv_w=g_a["conv_w"][None], a_conv_b=g_a["conv_b"][None], a_w_in=g_a["w_in"][None], a_b_in=g_a["b_in"][None],
        a_gate_w=g_a["w_gate"][None],
        a_gate_b=jnp.stack([g_a["gate_b0"], g_a["gate_b1"]]).reshape(1, 2, lru_heads, blk),
        a_lambda=g_a["lam"][None], a_w_out=g_a["w_out"][None], a_b_out=g_a["b_out"][None],
        **grads_b,
        c_w_q=[g_c0["w_q"], g_c1["w_q"]], c_w_kv=[g_c0["w_kv"], g_c1["w_kv"]], c_w_o=[g_c0["w_o"], g_c1["w_o"]],
        m_w_up=[g_m0["w_up"], g_m1["w_up"]], m_w_down=[g_m0["w_down"], g_m1["w_down"]],
    )
    return loss_part, dx.reshape(nseq, seq, d), grads


def kernel(x, mem, ln_gains, mem_norm, a_conv_w, a_conv_b, a_w_in, a_b_in, a_gate_w, a_gate_b, a_lambda, a_w_out, a_b_out, b_mu, b_w_rkv, b_w0, b_w1, b_w2, b_a0, b_a1, b_a2, b_g1, b_g2, b_k_k, b_k_a, b_r_k, b_gn_g, b_gn_b, b_w_o, c_w_q, c_w_kv, c_w_o, m_w_up, m_w_down, loss_target, m_ln_gains, m_mem_norm, m_a_conv_w, m_a_conv_b, m_a_w_in, m_a_b_in, m_a_gate_w, m_a_gate_b, m_a_lambda, m_a_w_out, m_a_b_out, m_b_mu, m_b_w_rkv, m_b_w0, m_b_w1, m_b_w2, m_b_a0, m_b_a1, m_b_a2, m_b_g1, m_b_g2, m_b_k_k, m_b_k_a, m_b_r_k, m_b_gn_g, m_b_gn_b, m_b_w_o, m_c_w_q, m_c_w_kv, m_c_w_o, m_m_w_up, m_m_w_down, v_ln_gains, v_mem_norm, v_a_conv_w, v_a_conv_b, v_a_w_in, v_a_b_in, v_a_gate_w, v_a_gate_b, v_a_lambda, v_a_w_out, v_a_b_out, v_b_mu, v_b_w_rkv, v_b_w0, v_b_w1, v_b_w2, v_b_a0, v_b_a1, v_b_a2, v_b_g1, v_b_g2, v_b_k_k, v_b_k_a, v_b_r_k, v_b_gn_g, v_b_gn_b, v_b_w_o, v_c_w_q, v_c_w_kv, v_c_w_o, v_m_w_up, v_m_w_down):
    given = dict(locals())
    w = {n: given[n] for n in _WEIGHTS}
    mom1 = {n: given["m_" + n] for n in _WEIGHTS}
    mom2 = {n: given["v_" + n] for n in _WEIGHTS}

    me = 2 * lax.axis_index("x") + lax.axis_index("y")
    per_layer = [n for n in _MATRICES if n[0] in "cm"]
    early = [(n, None) for n in _MATRICES if n[0] == "a"]
    late = dict(cm0=[(n, 0) for n in per_layer], layer1=[(n, None) for n in _MATRICES if n[0] == "b"] + [(n, 1) for n in per_layer])
    piece = lambda n, layer: w[n] if layer is None else w[n][layer:layer + 1]

    def gathered(entries, buf):
        shards = [_unpack(buf[s], [piece(n, layer).shape for n, layer in entries]) for s in range(N_XY)]
        out = {}
        for i, (n, layer) in enumerate(entries):
            full = jnp.concatenate([shards[s][i] for s in range(N_XY)], axis=_SHARD_AXIS[n])
            if layer is None:
                out[n] = full
            else:
                out.setdefault(n, {})[layer] = full[0]
        return out

    late_bufs = {stage: _pack([piece(n, layer) for n, layer in entries], BF16) for stage, entries in late.items()}
    gm, gv = _all_gather_xy(_pack([piece(n, layer) for n, layer in early], BF16), _pack([w[n] for n in _VECTORS], F32, SUBLANES))
    in_flight = {}
    token = gv
    for stage in late:
        *in_flight[stage], token = _gather_xy_start("gather_%s_start" % stage, late_bufs[stage], token)
    fw = {n: w[n] for n in _REPLICATED} | gathered(early, gm)
    vec_shards = [_unpack(gv[s], [w[n].shape for n in _VECTORS]) for s in range(N_XY)]
    for i, n in enumerate(_VECTORS):
        fw[n] = jnp.concatenate([vec_shards[s][i] for s in range(N_XY)], axis=_SHARD_AXIS[n])
    fw["ln_gains"] = fw["ln_gains"] + token[0, 0]

    def late_weights(stage, after):
        land = _gather_xy_wait("gather_%s_wait" % stage, *in_flight[stage], after)
        return gathered(late[stage], lax.dynamic_update_index_in_dim(land, late_bufs[stage], me, 0))

    tile_rows = 256
    groups = dict(layer1=[(n, 1 if n in per_layer else None) for n in _SHARDED if n in per_layer or n[0] == "b"],
                  cm0=[(n, 0) for n in _SHARDED if n in per_layer],
                  rest=[(n, None) for n in _SHARDED if n not in per_layer and n[0] != "b"])

    def piece_shape(n, layer):
        return w[n].shape if layer is None else (1,) + w[n].shape[1:]

    def rows_of(n, layer):
        size = 1
        for s_ in piece_shape(n, layer):
            size *= s_
        return size // PACK_W

    def split(entries):
        in_place = [e for e in entries if w[e[0]].shape[-1] == PACK_W and rows_of(*e) % tile_rows == 0 and rows_of(*e) > 0]
        in_place.sort(key=lambda e: -rows_of(*e))
        return in_place, [e for e in entries if e not in in_place]

    def buffers(entries, grads):
        in_place, packed = split(entries)

        def pieces(s):
            out = []
            for n, layer in in_place + packed:
                ax = _SHARD_AXIS[n]
                size = w[n].shape[ax]
                g = grads[n] if layer is None else grads[n][layer]
                out.append(lax.dynamic_slice_in_dim(g, s * size, size, axis=ax if layer is None else ax - 1))
            return out
        own = _pack(pieces(me), F32)
        n_rows = own.shape[0]
        used = sum(-(-rows_elems // PACK_W) for rows_elems in (p.size for p in pieces(0)))
        fill = [jnp.zeros(((n_rows - used) * PACK_W,), BF16)] if n_rows > used else []
        gsend = _pack([p for s in range(N_XY) for p in pieces(s) + fill], BF16, row_mult=1)
        return gsend.reshape(N_XY, n_rows, PACK_W), own

    def update(tag, entries, own, recv):
        in_place, packed = split(entries)
        part = _sum_contributions("sum_grads_" + tag, own, recv)
        sib = _swap_with_sibling("swap_sibling_" + tag, part)
        out = {}
        first = 0
        for n, layer in in_place:
            flat = [src[n].reshape(-1, PACK_W) for src in (w, mom1, mom2)]
            rows = rows_of(n, layer)
            res = _adamw_sharded("adamw_%s_%s" % (n, tag), part, sib, first, *flat, w_first=(layer or 0) * rows, n_rows=rows)
            out[(n, layer)] = [o.reshape(piece_shape(n, layer)) for o in res]
            first += rows
        take = lambda src, n, layer: src[n] if layer is None else src[n][layer:layer + 1]
        flat = [_pack([take(src, n, layer) for n, layer in packed], F32) for src in (w, mom1, mom2)]
        assert first + flat[0].shape[0] == part.shape[0], (first, flat[0].shape, part.shape)
        res = _adamw_sharded("adamw_packed_" + tag, part, sib, first, *flat)
        tail = [_unpack(o, [piece_shape(n, layer) for n, layer in packed]) for o in res]
        for i, e in enumerate(packed):
            out[e] = [tail[kind][i] for kind in range(4)]
        return out

    sent = {}

    def on_grads(stage, grads_so_far):
        gsend, own = buffers(groups[stage], grads_so_far)
        *handles, token = _exchange_xy_start("exchange_%s_start" % stage, gsend, own)
        sent[stage] = (own, handles)
        return token

    loss_part, grad_x, grads = _local_step(x, mem, loss_target, fw, late_weights, on_grads)

    gsend, own = buffers(groups["rest"], grads)
    out = update("rest", groups["rest"], own, _exchange_xy(gsend))
    for stage, (own, handles) in sent.items():
        out |= update(stage, groups[stage], own, _exchange_xy_wait("exchange_%s_wait" % stage, *handles, grad_x))
    sharded_out = [[] for _ in range(4)]
    for n in _SHARDED:
        for kind in range(4):
            if n in per_layer:
                sharded_out[kind].append(jnp.concatenate([out[(n, 0)][kind], out[(n, 1)][kind]], axis=0))
            else:
                sharded_out[kind].append(out[(n, None)][kind])

    small = _pack([grads[n] for n in _REPLICATED] + [loss_part.reshape(1)], F32, SUBLANES)
    parts = _all_gather_all(small)
    zero = jnp.zeros((1,), F32)
    flat = [_pack([src[n] for n in _REPLICATED] + [zero], F32, SUBLANES) for src in (w, mom1, mom2)]
    repl_out = [_unpack(o, [w[n].shape for n in _REPLICATED] + [(1,)]) for o in _adamw_replicated(parts, *flat)]
    loss = repl_out[0][-1][0]

    result = [loss, grad_x]
    for kind in range(4):
        by_name = dict(zip(_SHARDED, sharded_out[kind])) | dict(zip(_REPLICATED, repl_out[kind][:-1]))
        result += [by_name[n] for n in _WEIGHTS]
    return tuple(result)
```

```python
import functools

import jax
import jax.numpy as jnp
from jax import lax
from jax.experimental import pallas as pl
from jax.experimental.pallas import tpu as pltpu

F32 = jnp.float32
BF16 = jnp.bfloat16
MESH = pl.DeviceIdType.MESH

LANES = 128
SUBLANES = 8
VMEM_LIMIT_BYTES = 48 * 1024 * 1024

RMS_EPS = 1e-6
LRU_C = 8.0
LRU_HEADS = 4
CONV_WIDTH = 4
RWKV_HEAD = 64
RWKV_GN_EPS = 64e-5
MEM_HEADS = 4
ADAM_LR = 0.001
ADAM_B1 = 0.9
ADAM_B2 = 0.999
ADAM_EPS = 1e-08
ADAM_WD = 0.01
ADAM_STEP = 10
WKV_CHUNK = 64

_PARAMS = functools.partial(pltpu.CompilerParams, vmem_limit_bytes=VMEM_LIMIT_BYTES)


def _tile(n, want):
    if n <= want:
        return n
    t = want
    while t >= SUBLANES:
        if n % t == 0 and t % SUBLANES == 0:
            return t
        t -= SUBLANES
    return n


def _fold8(v):
    tm, d = v.shape
    if tm == SUBLANES:
        return v
    return jnp.sum(v.reshape(tm // SUBLANES, SUBLANES, d), axis=0)


def _rowwise(name, body, rows, consts=(), out_rows=(), out_accs=(), prev=(), nxt=(), tm=512, seq=None, n_rows=None):
    rows = [r if isinstance(r, tuple) else (r, r.shape[1], 0) for r in rows]
    rows = [r if len(r) == 4 else r + (0,) for r in rows]
    t = rows[0][0].shape[0] if n_rows is None else n_rows
    tm = _tile(t, tm)
    if seq is not None:
        tm = _tile(seq, tm)
    nblk = t // tm
    nrow, ncst, nprev, nnxt = len(rows), len(consts), len(prev), len(nxt)
    nor, noa = len(out_rows), len(out_accs)
    hb = tm // SUBLANES

    def kern(*refs):
        i = pl.program_id(0)
        rv = [r[...] for r in refs[:nrow]]
        cv = [c[...] for c in refs[nrow:nrow + ncst]]
        o = nrow + ncst
        pv = []
        for j in range(nprev):
            at_start = (i * tm) % seq == 0
            h = refs[o + j][...]
            pv.append(jnp.where(at_start, jnp.zeros_like(h), h))
        o += nprev
        nv = []
        for j in range(nnxt):
            at_end = ((i + 1) * tm) % seq == 0
            h = refs[o + j][...]
            nv.append(jnp.where(at_end, jnp.zeros_like(h), h))
        o += nnxt
        outs, accs = body(rv, cv, pv, nv)
        for j in range(nor):
            refs[o + j][...] = outs[j].astype(refs[o + j].dtype)
        o += nor
        if noa:
            @pl.when(i == 0)
            def _():
                for j in range(noa):
                    refs[o + j][...] = jnp.zeros_like(refs[o + j])
            for j in range(noa):
                refs[o + j][...] += _fold8(accs[j].astype(F32))

    assert all(first % tm == 0 for (_, _, _, first) in rows), name
    in_specs = [pl.BlockSpec((tm, w), functools.partial(lambda i, c, o: (i + o, c), c=cb, o=first // tm))
                for (_, w, cb, first) in rows]
    in_specs += [pl.BlockSpec(c.shape, lambda i: (0, 0)) for c in consts]
    in_specs += [pl.BlockSpec((SUBLANES, rows[j][1]),
                              functools.partial(lambda i, c: (jnp.maximum(i * hb - 1, 0), c), c=rows[j][2])) for j in prev]
    in_specs += [pl.BlockSpec((SUBLANES, rows[j][1]),
                              functools.partial(lambda i, c: (jnp.minimum((i + 1) * hb, t // SUBLANES - 1), c), c=rows[j][2]))
                 for j in nxt]
    out_shape = [jax.ShapeDtypeStruct((t, w), dt) for (w, dt) in out_rows]
    out_shape += [jax.ShapeDtypeStruct((SUBLANES, w), F32) for w in out_accs]
    out_specs = [pl.BlockSpec((tm, w), lambda i: (i, 0)) for (w, _) in out_rows]
    out_specs += [pl.BlockSpec((SUBLANES, w), lambda i: (0, 0)) for w in out_accs]
    args = [r[0] for r in rows] + list(consts) + [rows[j][0] for j in prev] + [rows[j][0] for j in nxt]
    res = pl.pallas_call(
        kern, name=name, grid=(nblk,), in_specs=in_specs, out_specs=out_specs, out_shape=out_shape,
        compiler_params=_PARAMS(dimension_semantics=("arbitrary",)),
    )(*args)
    return list(res[:nor]), list(res[nor:])


def _shift_down(x, halo, k):
    rolled = pltpu.roll(x, k, 0)
    row = lax.broadcasted_iota(jnp.int32, (SUBLANES, x.shape[1]), 0)
    first = jnp.where(row < k, pltpu.roll(halo, k, 0), rolled[:SUBLANES])
    if x.shape[0] == SUBLANES:
        return first
    return jnp.concatenate([first, rolled[SUBLANES:]], axis=0)


def _shift_up(x, halo, k):
    n = x.shape[0]
    rolled = pltpu.roll(x, n - k, 0)
    row = lax.broadcasted_iota(jnp.int32, (SUBLANES, x.shape[1]), 0)
    last = jnp.where(row >= SUBLANES - k, pltpu.roll(halo, SUBLANES - k, 0), rolled[n - SUBLANES:])
    if n == SUBLANES:
        return last
    return jnp.concatenate([rolled[:n - SUBLANES], last], axis=0)


class _Transposed:
    def __init__(self, w):
        self.w = w


def _t(w):
    return _Transposed(w)


def _mm(name, a, b, out_dtype=F32, trans_a=False, tm=1024, tn=1024, tk=1024, epilogue=None, extra=None):
    trans_b = isinstance(b, _Transposed)
    assert not (trans_a and trans_b)
    if trans_b:
        b = b.w
    if trans_a:
        kdim, m = a.shape
    else:
        m, kdim = a.shape
    n = b.shape[0] if trans_b else b.shape[1]
    assert b.shape[1 if trans_b else 0] == kdim, (name, a.shape, b.shape)
    tm, tn, tk = _tile(m, tm), _tile(n, tn), _tile(kdim, tk)
    nk = kdim // tk
    dims = (((0,), (0,)), ((), ())) if trans_a else (((1,), (1 if trans_b else 0,)), ((), ()))

    n_in = 2 if extra is None else 3

    def kern(*refs):
        a_ref, b_ref, o_ref, acc = refs[0], refs[1], refs[n_in], refs[n_in + 1:]

        def store(res):
            if epilogue is not None:
                res = epilogue(res) if extra is None else epilogue(res, refs[2][...])
            o_ref[...] = res.astype(o_ref.dtype)

        part = lax.dot_general(a_ref[...].astype(BF16), b_ref[...].astype(BF16), dims, preferred_element_type=F32)
        if nk == 1:
            store(part)
        else:
            k = pl.program_id(2)

            @pl.when(k == 0)
            def _():
                acc[0][...] = part

            @pl.when(k > 0)
            def _():
                acc[0][...] += part

            @pl.when(k == nk - 1)
            def _():
                store(acc[0][...])

    a_spec = pl.BlockSpec((tk, tm), lambda i, j, k: (k, i)) if trans_a else pl.BlockSpec((tm, tk), lambda i, j, k: (i, k))
    b_spec = pl.BlockSpec((tn, tk), lambda i, j, k: (j, k)) if trans_b else pl.BlockSpec((tk, tn), lambda i, j, k: (k, j))
    out_spec = pl.BlockSpec((tm, tn), lambda i, j, k: (i, j))
    return pl.pallas_call(
        kern, name=name, grid=(m // tm, n // tn, nk),
        in_specs=[a_spec, b_spec] + ([] if extra is None else [out_spec]),
        out_specs=out_spec,
        out_shape=jax.ShapeDtypeStruct((m, n), out_dtype),
        scratch_shapes=[] if nk == 1 else [pltpu.VMEM((tm, tn), F32)],
        compiler_params=_PARAMS(dimension_semantics=("parallel", "parallel", "arbitrary")),
    )(*((a, b) if extra is None else (a, b, extra)))


def _gates_mm(name, x, gate_w, mode, dz=None, tm=1024):
    _, nh, blk, _ = gate_w.shape
    d = nh * blk
    t = x.shape[0]
    tm = _tile(t, tm)
    cols = lambda g, h: slice(g * d + h * blk, g * d + (h + 1) * blk)

    def fwd(x_ref, w_ref, o_ref):
        for h in range(nh):
            xh = x_ref[:, cols(0, h)].astype(BF16)
            for g in range(2):
                o_ref[:, cols(g, h)] = jnp.dot(xh, w_ref[g, h].astype(BF16), preferred_element_type=F32)

    def dx(dz_ref, w_ref, o_ref):
        for h in range(nh):
            o_ref[:, cols(0, h)] = sum(lax.dot_general(dz_ref[:, cols(g, h)].astype(BF16), w_ref[g, h].astype(BF16), _NT,
                                                       preferred_element_type=F32) for g in range(2))

    def dw(x_ref, dz_ref, o_ref):
        @pl.when(pl.program_id(0) == 0)
        def _():
            o_ref[...] = jnp.zeros_like(o_ref)

        for h in range(nh):
            xh = x_ref[:, cols(0, h)].astype(BF16)
            for g in range(2):
                o_ref[g, h] += lax.dot_general(xh, dz_ref[:, cols(g, h)].astype(BF16), _TN, preferred_element_type=F32)

    row = lambda width: pl.BlockSpec((tm, width), lambda i: (i, 0))
    whole = pl.BlockSpec(gate_w.shape, lambda i: (0, 0, 0, 0))
    kern, args, in_specs, out_spec, out_shape, sem = {
        "fwd": (fwd, (x, gate_w), [row(d), whole], row(2 * d), jax.ShapeDtypeStruct((t, 2 * d), F32), "parallel"),
        "dx": (dx, (x, gate_w), [row(2 * d), whole], row(d), jax.ShapeDtypeStruct((t, d), F32), "parallel"),
        "dw": (dw, (x, dz), [row(d), row(2 * d)], whole, jax.ShapeDtypeStruct(gate_w.shape, F32), "arbitrary"),
    }[mode]
    return pl.pallas_call(kern, name=name, grid=(t // tm,), in_specs=in_specs, out_specs=out_spec, out_shape=out_shape,
                          compiler_params=_PARAMS(dimension_semantics=(sem,)))(*args)


def _scan(name, a, b, seq, reverse=False, tm=256):
    t, d = a.shape
    tm = _tile(seq, tm)
    nblk = t // tm
    ntile = tm // SUBLANES

    def kern(a_ref, b_ref, h_ref, carry_h, carry_a):
        i = pl.program_id(0)
        blk = (nblk - 1 - i) if reverse else i
        edge = (((blk + 1) * tm) % seq == 0) if reverse else ((blk * tm) % seq == 0)

        @pl.when(edge)
        def _():
            carry_h[...] = jnp.zeros_like(carry_h)
            carry_a[...] = jnp.zeros_like(carry_a)

        def tile_step(j, c):
            jj = (ntile - 1 - j) if reverse else j
            rows = pl.ds(pl.multiple_of(jj * SUBLANES, SUBLANES), SUBLANES)
            a8 = a_ref[rows, :]
            b8 = b_ref[rows, :]
            h, an = c
            out = [None] * SUBLANES
            order = range(SUBLANES - 1, -1, -1) if reverse else range(SUBLANES)
            for r in order:
                if reverse:
                    h = b8[r:r + 1, :] + an * h
                    an = a8[r:r + 1, :]
                else:
                    h = a8[r:r + 1, :] * h + b8[r:r + 1, :]
                out[r] = h
            h_ref[rows, :] = jnp.concatenate(out, axis=0)
            return (h, an)

        h, an = lax.fori_loop(0, ntile, tile_step, (carry_h[...], carry_a[...]))
        carry_h[...] = h
        carry_a[...] = an

    idx = (lambda i: (nblk - 1 - i, 0)) if reverse else (lambda i: (i, 0))
    return pl.pallas_call(
        kern, name=name, grid=(nblk,),
        in_specs=[pl.BlockSpec((tm, d), idx), pl.BlockSpec((tm, d), idx)],
        out_specs=pl.BlockSpec((tm, d), idx),
        out_shape=jax.ShapeDtypeStruct((t, d), F32),
        scratch_shapes=[pltpu.VMEM((1, d), F32), pltpu.VMEM((1, d), F32)],
        compiler_params=_PARAMS(dimension_semantics=("arbitrary",)),
    )(a, b)


_NN = (((1,), (0,)), ((), ()))
_NT = (((1,), (1,)), ((), ()))
_TN = (((0,), (0,)), ((), ()))


def _dot1(a, b, dims):
    return lax.dot_general(a.astype(BF16), b.astype(BF16), dims, preferred_element_type=F32)


def _dot3(a, b, dims):
    a_hi, b_hi = a.astype(BF16), b.astype(BF16)
    a_lo, b_lo = (a - a_hi.astype(F32)).astype(BF16), (b - b_hi.astype(F32)).astype(BF16)
    dg = lambda p, q: lax.dot_general(p, q, dims, preferred_element_type=F32)
    return dg(a_hi, b_hi) + (dg(a_hi, b_lo) + dg(a_lo, b_hi))


def _make_bmm(dot):
    def make(dims, da_rule, db_rule):
        @jax.custom_vjp
        def f(a, b):
            return dot(a, b, dims)

        def fwd(a, b):
            return dot(a, b, dims), (a, b)

        def bwd(res, g):
            a, b = res
            return da_rule(a, b, g), db_rule(a, b, g)

        f.defvjp(fwd, bwd)
        return f

    return dict(nn=make(_NN, lambda a, b, g: dot(g, b, _NT), lambda a, b, g: dot(a, g, _TN)),
                nt=make(_NT, lambda a, b, g: dot(g, b, _NN), lambda a, b, g: dot(g, a, _TN)),
                tn=make(_TN, lambda a, b, g: dot(b, g, _NT), lambda a, b, g: dot(a, g, _NN)))


_BMM = {1: _make_bmm(_dot1), 3: _make_bmm(_dot3)}
_WKV_PASSES = dict(pair=1, read=1, inv=3, apply=1, write=1)


def _running_sum(x, reverse):
    c = x.shape[0]
    row = lax.broadcasted_iota(jnp.int32, x.shape, 0)
    k = 1
    while k < c:
        if reverse:
            x = x + jnp.where(row < c - k, pltpu.roll(x, c - k, 0), 0.0)
        else:
            x = x + jnp.where(row >= k, pltpu.roll(x, k, 0), 0.0)
        k *= 2
    return x


@jax.custom_vjp
def _cumsum_rows(x):
    return _running_sum(x, False)


_cumsum_rows.defvjp(lambda x: (_running_sum(x, False), None), lambda _, g: (_running_sum(g, True),))


@jax.custom_vjp
def _unit_lower_inverse(nl):
    c = nl[0].shape[0]
    mm = _BMM[_WKV_PASSES["inv"]]["nn"]
    eye = jnp.where(lax.broadcasted_iota(jnp.int32, (c, c), 0) == lax.broadcasted_iota(jnp.int32, (c, c), 1), 1.0, 0.0)
    inv = [eye + z for z in nl]
    p = nl
    for _ in range(c.bit_length() - 2):
        p = [mm(z, z) for z in p]
        inv = [i_ + mm(p_, i_) for i_, p_ in zip(inv, p)]
    return inv


def _unit_lower_inverse_fwd(nl):
    inv = _unit_lower_inverse(nl)
    return inv, inv


def _unit_lower_inverse_bwd(inv, g):
    mm = _BMM[_WKV_PASSES["inv"]]
    left = [mm["tn"](x, g_) for x, g_ in zip(inv, g)]
    return ([mm["nt"](l_, x) for l_, x in zip(left, inv)],)


_unit_lower_inverse.defvjp(_unit_lower_inverse_fwd, _unit_lower_inverse_bwd)


@jax.custom_vjp
def _kept_inverse(nl, inv):
    return inv


_kept_inverse.defvjp(lambda nl, inv: (inv, inv),
                     lambda inv, g: (_unit_lower_inverse_bwd(inv, g)[0], [jnp.zeros_like(x) for x in inv]))


def _wkv_chunk(r, lw, k, v, a, b, s0, kept_inv=None):
    c = r[0].shape[0]
    ti = lax.broadcasted_iota(jnp.int32, (c, 2 * c), 0)
    tj = lax.broadcasted_iota(jnp.int32, (c, 2 * c), 1)
    right = tj >= c
    tau = jnp.where(right, tj - c, tj)
    strict_left = jnp.logical_and(jnp.logical_not(right), tau < ti)[:, :c]
    strict_right = jnp.logical_and(right, tau < ti)
    incl = tau <= ti
    last = lax.broadcasted_iota(jnp.int32, r[0].shape, 0) == c - 1
    each = lambda f, *ls: [f(*z) for z in zip(*ls)]
    rows2 = lambda x, y: jnp.concatenate([x, y], axis=0)
    pair, read, inv_, apply_, write = (_BMM[_WKV_PASSES[role]] for role in ("pair", "read", "inv", "apply", "write"))
    cum = each(_cumsum_rows, lw)
    w_incl = each(jnp.exp, cum)
    w_inv = each(lambda z: jnp.exp(-z), cum)
    at = each(lambda a_, c_, l_: a_ * jnp.exp(c_ - l_), a, cum, lw)
    ar = each(rows2, at, each(jnp.multiply, r, w_incl))
    bk = each(rows2, each(jnp.multiply, b, w_inv), each(jnp.multiply, k, w_inv))
    pp = each(pair["nt"], ar, bk)
    sr = each(read["nt"], ar, s0)
    nl = [jnp.where(strict_left, z[:c, :c], 0.0) for z in pp]
    zero_v = each(lambda v_: rows2(jnp.zeros_like(v_), v_), v)
    rhs = each(lambda s, z, zv: s[:c] + apply_["nn"](jnp.where(strict_right, z[:c], 0.0), zv), sr, pp, zero_v)
    inv = _unit_lower_inverse(nl) if kept_inv is None else _kept_inverse(nl, kept_inv)
    ut = each(inv_["nn"], inv, rhs)
    uv = each(rows2, ut, v)
    y = each(lambda s, z, uv_: s[c:] + apply_["nn"](jnp.where(incl, z[c:], 0.0), uv_), sr, pp, uv)
    w_end = each(lambda z: jnp.exp(jnp.sum(jnp.where(last, z, 0.0), axis=0, keepdims=True)), cum)
    s1 = each(lambda s, uv_, bk_, w_: (s + write["tn"](uv_, bk_)) * w_, s0, uv, bk, w_end)
    return y, s1, inv


def _wkv_fwd(r, lw, k, v, a, b, seq, hb=16):
    t, d = r.shape
    n = RWKV_HEAD
    nh = d // n
    hb = min(hb, nh)
    chunk = min(WKV_CHUNK, seq)
    ncs = seq // chunk

    def kern(r_ref, lw_ref, k_ref, v_ref, a_ref, b_ref, y_ref, st_ref, inv_ref, s_scr):
        @pl.when(pl.program_id(2) == 0)
        def _():
            s_scr[...] = jnp.zeros_like(s_scr)

        heads = lambda ref: [ref[:, h * n:(h + 1) * n] for h in range(hb)]
        s0 = [s_scr[h] for h in range(hb)]
        y, s1, inv = _wkv_chunk(heads(r_ref), heads(lw_ref), heads(k_ref), heads(v_ref), heads(a_ref), heads(b_ref), s0)
        for h in range(hb):
            st_ref[0, h] = s0[h]
            inv_ref[0, h] = inv[h]
            y_ref[:, h * n:(h + 1) * n] = y[h]
            s_scr[h] = s1[h]

    vec = pl.BlockSpec((chunk, hb * n), lambda bb, g, c: (bb * ncs + c, g))
    per_chunk = lambda rows: pl.BlockSpec((1, hb, rows, rows), lambda bb, g, c: (bb * ncs + c, g, 0, 0))
    return pl.pallas_call(
        kern, name="wkv_fwd", grid=(t // seq, nh // hb, ncs), in_specs=[vec] * 6,
        out_specs=[vec, per_chunk(n), per_chunk(chunk)],
        out_shape=[jax.ShapeDtypeStruct((t, d), F32), jax.ShapeDtypeStruct((t // chunk, nh, n, n), F32),
                   jax.ShapeDtypeStruct((t // chunk, nh, chunk, chunk), F32)],
        scratch_shapes=[pltpu.VMEM((hb, n, n), F32)],
        compiler_params=_PARAMS(dimension_semantics=("parallel", "parallel", "arbitrary")),
    )(r, lw, k, v, a, b)


def _wkv_bwd(r, lw, k, v, a, b, st, inv, dy, seq, hb=16):
    t, d = r.shape
    n = RWKV_HEAD
    nh = d // n
    hb = min(hb, nh)
    chunk = min(WKV_CHUNK, seq)
    ncs = seq // chunk

    def kern(r_ref, lw_ref, k_ref, v_ref, a_ref, b_ref, st_ref, inv_ref, dy_ref,
             dr_ref, dlw_ref, dk_ref, dv_ref, da_ref, db_ref, ds_scr):
        @pl.when(pl.program_id(2) == 0)
        def _():
            ds_scr[...] = jnp.zeros_like(ds_scr)

        heads = lambda ref: [ref[:, h * n:(h + 1) * n] for h in range(hb)]
        kept = [inv_ref[0, h] for h in range(hb)]
        _, vjp = jax.vjp(lambda *args: _wkv_chunk(*args, kept_inv=kept)[:2],
                         heads(r_ref), heads(lw_ref), heads(k_ref), heads(v_ref), heads(a_ref), heads(b_ref),
                         [st_ref[0, h] for h in range(hb)])
        grads = vjp((heads(dy_ref), [ds_scr[h] for h in range(hb)]))
        for h in range(hb):
            for ref, g in zip((dr_ref, dlw_ref, dk_ref, dv_ref, da_ref, db_ref), grads[:6]):
                ref[:, h * n:(h + 1) * n] = g[h]
            ds_scr[h] = grads[6][h]

    vec = pl.BlockSpec((chunk, hb * n), lambda bb, g, c: (bb * ncs + ncs - 1 - c, g))
    per_chunk = lambda rows: pl.BlockSpec((1, hb, rows, rows), lambda bb, g, c: (bb * ncs + ncs - 1 - c, g, 0, 0))
    return pl.pallas_call(
        kern, name="wkv_bwd", grid=(t // seq, nh // hb, ncs),
        in_specs=[vec] * 6 + [per_chunk(n), per_chunk(chunk), vec],
        out_specs=[vec] * 6, out_shape=[jax.ShapeDtypeStruct((t, d), F32)] * 6,
        scratch_shapes=[pltpu.VMEM((hb, n, n), F32)],
        compiler_params=_PARAMS(dimension_semantics=("parallel", "parallel", "arbitrary")),
    )(r, lw, k, v, a, b, st, inv, dy)


def _softmax_rows(s):
    e = jnp.exp(s - jnp.max(s, axis=-1, keepdims=True))
    return e / jnp.sum(e, axis=-1, keepdims=True)


def _attn_fwd(q, kv, seq, mem_len, tq=1024):
    t, d = q.shape
    dh = d // MEM_HEADS
    scale = dh ** -0.5
    tq = _tile(seq, tq)
    nq = seq // tq

    def kern(q_ref, kv_ref, o_ref):
        for h in range(MEM_HEADS):
            cols = slice(h * dh, (h + 1) * dh)
            vcols = slice(d + h * dh, d + (h + 1) * dh)
            s = lax.dot_general(q_ref[:, cols], kv_ref[:, cols], _NT, preferred_element_type=F32) * scale
            p = _softmax_rows(s)
            o_ref[:, cols] = jnp.dot(p.astype(BF16), kv_ref[:, vcols], preferred_element_type=F32).astype(o_ref.dtype)

    return pl.pallas_call(
        kern, name="attn_fwd", grid=(t // seq, nq),
        in_specs=[pl.BlockSpec((tq, d), lambda b, i: (b * nq + i, 0)), pl.BlockSpec((mem_len, 2 * d), lambda b, i: (b, 0))],
        out_specs=pl.BlockSpec((tq, d), lambda b, i: (b * nq + i, 0)),
        out_shape=jax.ShapeDtypeStruct((t, d), BF16),
        compiler_params=_PARAMS(dimension_semantics=("parallel", "parallel")),
    )(q, kv)


def _attn_bwd(q, kv, do, seq, mem_len, tq=1024):
    t, d = q.shape
    dh = d // MEM_HEADS
    scale = dh ** -0.5
    tq = _tile(seq, tq)
    nq = seq // tq

    def kern(q_ref, kv_ref, do_ref, dq_ref, dkv_ref):
        @pl.when(pl.program_id(1) == 0)
        def _():
            dkv_ref[...] = jnp.zeros_like(dkv_ref)

        for h in range(MEM_HEADS):
            cols = slice(h * dh, (h + 1) * dh)
            vcols = slice(d + h * dh, d + (h + 1) * dh)
            qh, kh, vh, doh = q_ref[:, cols], kv_ref[:, cols], kv_ref[:, vcols], do_ref[:, cols]
            p = _softmax_rows(lax.dot_general(qh, kh, _NT, preferred_element_type=F32) * scale)
            dp = lax.dot_general(doh, vh, _NT, preferred_element_type=F32)
            ds = (p * (dp - jnp.sum(p * dp, axis=-1, keepdims=True)) * scale).astype(BF16)
            dq_ref[:, cols] = jnp.dot(ds, kh, preferred_element_type=F32).astype(dq_ref.dtype)
            dkv_ref[:, cols] += lax.dot_general(ds, qh, _TN, preferred_element_type=F32)
            dkv_ref[:, vcols] += lax.dot_general(p.astype(BF16), doh, _TN, preferred_element_type=F32)

    return pl.pallas_call(
        kern, name="attn_bwd", grid=(t // seq, nq),
        in_specs=[pl.BlockSpec((tq, d), lambda b, i: (b * nq + i, 0)), pl.BlockSpec((mem_len, 2 * d), lambda b, i: (b, 0)),
                  pl.BlockSpec((tq, d), lambda b, i: (b * nq + i, 0))],
        out_specs=[pl.BlockSpec((tq, d), lambda b, i: (b * nq + i, 0)), pl.BlockSpec((mem_len, 2 * d), lambda b, i: (b, 0))],
        out_shape=[jax.ShapeDtypeStruct((t, d), BF16), jax.ShapeDtypeStruct(kv.shape, F32)],
        compiler_params=_PARAMS(dimension_semantics=("parallel", "arbitrary")),
    )(q, kv, do)


def _rstd(x):
    return lax.rsqrt(jnp.mean(x * x, axis=-1, keepdims=True) + RMS_EPS)


def _rms(x, g):
    return x * _rstd(x) * g


def _rms_bwd(dy, x, g):
    rstd = _rstd(x)
    xhat = x * rstd
    dxhat = dy * g
    return rstd * (dxhat - xhat * jnp.mean(dxhat * xhat, axis=-1, keepdims=True)), dy * xhat


def _softplus(x):
    return jnp.maximum(x, 0.0) + jnp.log1p(jnp.exp(-jnp.abs(x)))


def _one_minus_exp(x):
    series = -x * (1.0 + x * (0.5 + x * (1.0 / 6.0 + x * (1.0 / 24.0 + x * (1.0 / 120.0)))))
    return jnp.where(x > -0.05, series, 1.0 - jnp.exp(x))


_GELU_C = 0.7978845608028654
_GELU_K = 0.044715


def _gelu(x):
    return 0.5 * x * (1.0 + jnp.tanh(_GELU_C * (x + _GELU_K * x * x * x)))


def _gelu_grad(x):
    th = jnp.tanh(_GELU_C * (x + _GELU_K * x * x * x))
    return 0.5 * (1.0 + th) + 0.5 * x * (1.0 - th * th) * _GELU_C * (1.0 + 3.0 * _GELU_K * x * x)


def _seg_sum(x, seg):
    mm = lambda v, dims: lax.dot_general(v.astype(BF16), seg, dims, preferred_element_type=F32)
    return mm(mm(x, _NN), _NT)


def _f32(v):
    return v.astype(F32)


def _norm_fwd(name, x, g, dtype=BF16):
    (hn,), _ = _rowwise(name, lambda rv, cv, pv, nv: ([_rms(_f32(rv[0]), cv[0])], []), [x], [g], [(x.shape[1], dtype)])
    return hn


def _resid_norm_fwd(name, x, t, g, bias=None, chain=None):
    d = x.shape[1]
    chain = {} if chain is None else chain

    def body(rv, cv, pv, nv):
        tt = rv[1] if bias is None else rv[1] + cv[1]
        y = rv[0] + _rms(tt, cv[0])
        if "gain" in chain:
            return [y, _rms(y, cv[-1])], []
        if "target" in chain:
            err = y - rv[2]
            return [err * (1.0 / d)], [err * err]
        return [y], []
    consts = [g] if bias is None else [g, bias]
    if "gain" in chain:
        (y, chain["hn"]), _ = _rowwise(name, body, [x, t], consts + [chain["gain"]], [(d, F32), (d, BF16)])
        return y
    if "target" in chain:
        (chain["dy"],), (chain["sq"],) = _rowwise(name, body, [x, t, chain["target"]], consts, [(d, F32)], [d])
        return None
    (y,), _ = _rowwise(name, body, [x, t], consts, [(d, F32)])
    return y


def _resid_norm_bwd(name, dxn, t, g, bias=None):
    def body(rv, cv, pv, nv):
        tt = rv[1] if bias is None else rv[1] + cv[1]
        dt, dg = _rms_bwd(rv[0], tt, cv[0])
        return [dt], [dg, dt]
    d = t.shape[1]
    (dt,), (dg, db) = _rowwise(name, body, [dxn, t], [g] if bias is None else [g, bias], [(d, BF16)], [d, d])
    return dt, dg.sum(0), db.sum(0)


def _prenorm_bwd(name, dxn, dhn, x, g):
    def body(rv, cv, pv, nv):
        dx, dg = _rms_bwd(_f32(rv[1]), rv[2], cv[0])
        return [rv[0] + dx], [dg]
    d = x.shape[1]
    (dx,), (dg,) = _rowwise(name, body, [dxn, dhn, x], [g], [(d, F32)], [d])
    return dx, dg.sum(0)


def _mlp_fwd(tag, x, g_pre, g_post, w_up, w_down, hn=None, chain=None):
    hn = _norm_fwd(tag + "_norm", x, g_pre) if hn is None else hn
    act = _mm(tag + "_up", hn, w_up, out_dtype=BF16, epilogue=lambda up: jnp.square(jnp.maximum(up, 0.0)))
    m = _mm(tag + "_down", act, w_down)
    y = _resid_norm_fwd(tag + "_res", x, m, g_post, chain=chain)
    return y, (x, hn, act, m)


def _mlp_bwd(tag, saved, dy, g_pre, g_post, w_up_t, w_down_t):
    x, hn, act, m = saved
    dm, dg_post, _ = _resid_norm_bwd(tag + "_dres", dy, m, g_post)
    dup = _mm(tag + "_dup", dm, w_down_t, out_dtype=BF16, extra=act,
              epilogue=lambda dact, act_: dact * 2.0 * jnp.sqrt(_f32(act_)))
    dw_down = _mm(tag + "_dwdown", act, dm, trans_a=True)
    dw_up = _mm(tag + "_dwup", hn, dup, trans_a=True)
    dhn = _mm(tag + "_dhn", dup, w_up_t)
    dx, dg_pre = _prenorm_bwd(tag + "_dnorm", dy, dhn, x, g_pre)
    return dx, dict(g_pre=dg_pre, g_post=dg_post, w_up=dw_up, w_down=dw_down)


def _xattn_fwd(tag, x, mem_n, g_pre, g_post, w_q, w_kv, w_o, seq, mem_len, hn=None, chain=None):
    hn = _norm_fwd(tag + "_norm", x, g_pre) if hn is None else hn
    q = _mm(tag + "_q", hn, w_q, out_dtype=BF16)
    kv = _mm(tag + "_kv", mem_n, w_kv, out_dtype=BF16)
    o = _attn_fwd(q, kv, seq, mem_len)
    c = _mm(tag + "_o", o, w_o)
    y = _resid_norm_fwd(tag + "_res", x, c, g_post, chain=chain)
    return y, (x, hn, q, kv, o, c)


def _xattn_bwd(tag, saved, dy, mem_n, g_pre, g_post, w_q_t, w_kv_t, w_o_t, seq, mem_len):
    x, hn, q, kv, o, c = saved
    dc, dg_post, _ = _resid_norm_bwd(tag + "_dres", dy, c, g_post)
    do = _mm(tag + "_do", dc, w_o_t, out_dtype=BF16)
    dw_o = _mm(tag + "_dwo", o, dc, trans_a=True)
    dq, dkv = _attn_bwd(q, kv, do, seq, mem_len)
    dw_q = _mm(tag + "_dwq", hn, dq, trans_a=True)
    dhn = _mm(tag + "_dhn", dq, w_q_t)
    dw_kv = _mm(tag + "_dwkv", mem_n, dkv, trans_a=True)
    dmem_n = _mm(tag + "_dmem", dkv, w_kv_t)
    dx, dg_pre = _prenorm_bwd(tag + "_dnorm", dy, dhn, x, g_pre)
    return dx, dmem_n, dict(g_pre=dg_pre, g_post=dg_post, w_q=dw_q, w_kv=dw_kv, w_o=dw_o)


def _lru_gates(z0, z1, gb0, gb1, sp):
    r = jax.nn.sigmoid(z0 + gb0)
    i = jax.nn.sigmoid(z1 + gb1)
    log_a = -LRU_C * r * sp
    a = jnp.exp(log_a)
    mult = jnp.sqrt(_one_minus_exp(2.0 * log_a))
    return r, i, a, mult


def _rglru_fwd(x, p, seq, chain=None):
    d = x.shape[1]
    hn = _norm_fwd("a_norm", x, p["g_pre"])
    proj = _mm("a_in", hn, p["w_in"])

    def conv_body(rv, cv, pv, nv):
        u = rv[0] + cv[0]
        halo = pv[0] + cv[0]
        i = pl.program_id(0)
        halo = jnp.where((i * rv[0].shape[0]) % seq == 0, jnp.zeros_like(halo), halo)
        conv = cv[2] + u * cv[1][CONV_WIDTH - 1:CONV_WIDTH]
        for tap in range(CONV_WIDTH - 1):
            conv = conv + _shift_down(u, halo, CONV_WIDTH - 1 - tap) * cv[1][tap:tap + 1]
        return [conv], []
    (conv,), _ = _rowwise("a_conv", conv_body, [(proj, d, 1)], [p["b_in_u"], p["conv_w"], p["conv_b"]], [(d, F32)],
                          prev=[0], seq=seq)
    z = _gates_mm("a_gate", conv, p["w_gate"], "fwd")

    def gate_body(rv, cv, pv, nv):
        r, i, a, mult = _lru_gates(rv[0], rv[1], cv[0], cv[1], _softplus(-cv[2]))
        return [a, mult * i * rv[2]], []
    (a, bb), _ = _rowwise("a_gates", gate_body, [(z, d, 0), (z, d, 1), conv], [p["gate_b0"], p["gate_b1"], p["lam"]],
                          [(d, F32), (d, F32)])
    h = _scan("a_scan", a, bb, seq)

    def hy_body(rv, cv, pv, nv):
        return [rv[0] * _gelu(rv[1] + cv[0])], []
    (hy,), _ = _rowwise("a_hy", hy_body, [h, (proj, d, 0)], [p["b_in_y"]], [(d, BF16)])
    out = _mm("a_out", hy, p["w_out"])
    y = _resid_norm_fwd("a_res", x, out, p["g_post"], bias=p["b_out"], chain=chain)
    return y, (x, hn, proj, conv, z, a, h, hy, out)


def _rglru_bwd(saved, dy, p, pt, seq):
    x, hn, proj, conv, z, a, h, hy, out = saved
    d = x.shape[1]
    dt, dg_post, db_out = _resid_norm_bwd("a_dres", dy, out, p["g_post"], bias=p["b_out"])
    dhy = _mm("a_dhy", dt, pt["w_out_t"])
    dw_out = _mm("a_dwout", hy, dt, trans_a=True)

    def dh_body(rv, cv, pv, nv):
        yb = rv[2] + cv[0]
        return [rv[0] * _gelu(yb), rv[0] * rv[1] * _gelu_grad(yb)], []
    (dh, dyb), _ = _rowwise("a_dh", dh_body, [dhy, h, (proj, d, 0)], [p["b_in_y"]], [(d, F32), (d, BF16)])
    g = _scan("a_rscan", a, dh, seq, reverse=True)

    def dgate_body(rv, cv, pv, nv):
        gg, hh, z0, z1, cnv = rv
        sp = _softplus(-cv[2])
        r, i, aa, mult = _lru_gates(z0, z1, cv[0], cv[1], sp)
        i_blk = pl.program_id(0)
        halo = jnp.where((i_blk * gg.shape[0]) % seq == 0, jnp.zeros_like(pv[0]), pv[0])
        da = gg * _shift_down(hh, halo, 1)
        dmult = gg * i * cnv
        di = gg * mult * cnv
        dconv = gg * mult * i
        dlog_a = da * aa - dmult * aa * aa / mult
        dz0 = dlog_a * (-LRU_C * sp) * r * (1.0 - r)
        dz1 = di * i * (1.0 - i)
        dsp = dlog_a * (-LRU_C * r)
        dlam = dsp * (-jax.nn.sigmoid(-cv[2]))
        return [jnp.concatenate([dz0, dz1], axis=1), dconv], [dz0, dz1, dlam]
    (dz, dconv1), (dgb0, dgb1, dlam) = _rowwise(
        "a_dgates", dgate_body, [g, h, (z, d, 0), (z, d, 1), conv], [p["gate_b0"], p["gate_b1"], p["lam"]],
        [(2 * d, BF16), (d, F32)], [d, d, d], prev=[1], seq=seq)
    dconv2 = _gates_mm("a_dconv", dz, p["w_gate"], "dx")
    dw_gate = _gates_mm("a_dwgate", conv, p["w_gate"], "dw", dz=dz)

    def dconv_body(rv, cv, pv, nv):
        dc1, dc2, pu, dyb_ = rv
        dc = dc1 + dc2
        dc_next = nv[0] + nv[1]
        u = pu + cv[0]
        i_blk = pl.program_id(0)
        halo = jnp.where((i_blk * u.shape[0]) % seq == 0, jnp.zeros_like(pv[0]), pv[0] + cv[0])
        du = dc * cv[1][CONV_WIDTH - 1:CONV_WIDTH]
        dws = []
        for tap in range(CONV_WIDTH - 1):
            k = CONV_WIDTH - 1 - tap
            du = du + _shift_up(dc, dc_next, k) * cv[1][tap:tap + 1]
            dws.append(dc * _shift_down(u, halo, k))
        dws.append(dc * u)
        return [jnp.concatenate([_f32(dyb_), du], axis=1)], dws + [dc, _f32(dyb_), du]
    (dproj,), accs = _rowwise(
        "a_dconvw", dconv_body, [dconv1, dconv2, (proj, d, 1), dyb], [p["b_in_u"], p["conv_w"]],
        [(2 * d, BF16)], [d] * (CONV_WIDTH + 3), prev=[2], nxt=[0, 1], seq=seq)
    dconv_w = jnp.stack([acc.sum(0) for acc in accs[:CONV_WIDTH]])
    dconv_b = accs[CONV_WIDTH].sum(0)
    db_in = jnp.concatenate([accs[CONV_WIDTH + 1].sum(0), accs[CONV_WIDTH + 2].sum(0)])
    dhn = _mm("a_dhn", dproj, pt["w_in_t"])
    dw_in = _mm("a_dwin", hn, dproj, trans_a=True)
    dx, dg_pre = _prenorm_bwd("a_dnorm", dy, dhn, x, p["g_pre"])
    grads = dict(g_pre=dg_pre, g_post=dg_post, b_out=db_out, w_out=dw_out, gate_b0=dgb0.sum(0), gate_b1=dgb1.sum(0),
                 lam=dlam.sum(0), w_gate=dw_gate, conv_w=dconv_w, conv_b=dconv_b, b_in=db_in, w_in=dw_in)
    return dx, grads


def _rwkv_prep(k, wl, za, w0, a0, k_k, k_a, seg):
    w_in = wl + w0
    e_w = jnp.exp(-_softplus(-w_in) - 0.5)
    a = jax.nn.sigmoid(za + a0)
    q = k * k_k
    norm = jnp.sqrt(_seg_sum(q * q, seg))
    n = jnp.maximum(norm, 1e-12)
    kk = q / n
    return w_in, e_w, a, norm, n, kk


def _rwkv_out(y, r, k2, v, gn_g, gn_b, r_k, seg):
    inv = 1.0 / RWKV_HEAD
    yc = y - _seg_sum(y, seg) * inv
    rstd = lax.rsqrt(_seg_sum(yc * yc, seg) * inv + RWKV_GN_EPS)
    yhat = yc * rstd
    s = _seg_sum(r * k2 * r_k, seg)
    return rstd, yhat, s, yhat * gn_g + gn_b + s * v


def _rwkv_fwd(x, p, seq, chain=None):
    t, d = x.shape
    nseq = t // seq

    def mix_body(rv, cv, pv, nv):
        hn = _rms(rv[0], cv[0])
        xx = _shift_down(hn, _rms(pv[0], cv[0]), 1) - hn
        return [hn] + [hn + xx * cv[1][c:c + 1] for c in range(6)], []
    (hn, xr, xw, xk, xv, xa, xg), _ = _rowwise("b_mix", mix_body, [x], [p["g_pre"], p["mu"]],
                                               [(d, F32)] + [(d, BF16)] * 6, prev=[0], seq=seq)
    r = _mm("b_r", xr, p["w_r"])
    k = _mm("b_k", xk, p["w_k"])
    v = _mm("b_v", xv, p["w_v"])
    lw = _mm("b_w1", xw, p["w1"])
    la = _mm("b_a1", xa, p["a1"], out_dtype=BF16)
    lg = _mm("b_g1", xg, p["g1"])
    (th,), _ = _rowwise("b_tanh", lambda rv, cv, pv, nv: ([jnp.tanh(rv[0])], []), [lw], [], [(lw.shape[1], BF16)])
    (sg,), _ = _rowwise("b_sig", lambda rv, cv, pv, nv: ([jax.nn.sigmoid(rv[0])], []), [lg], [], [(lg.shape[1], BF16)])
    wl = _mm("b_w2", th, p["w2"])
    za = _mm("b_a2", la, p["a2"])
    g = _mm("b_g2", sg, p["g2"])

    def prep_body(rv, cv, pv, nv):
        kk_, wl_, za_ = rv
        _, e_w, a, _, _, kk = _rwkv_prep(kk_, wl_, za_, cv[0], cv[1], cv[2], cv[3], cv[4])
        return [-e_w, kk_ * (1.0 + (a - 1.0) * cv[3]), -kk, kk * a], []
    (log_w, k2, rem_a, rem_b), _ = _rowwise("b_prep", prep_body, [k, wl, za],
                                            [p["w0"], p["a0"], p["k_k"], p["k_a"], p["seg"]], [(d, F32)] * 4, tm=256)
    rec_in = (r, log_w, k2, v, rem_a, rem_b)
    y, *states = _wkv_fwd(*rec_in, seq)

    def out_body(rv, cv, pv, nv):
        y_, r_, k2_, v_, g_ = rv
        _, _, _, out = _rwkv_out(y_, r_, k2_, v_, cv[0], cv[1], cv[2], cv[3])
        return [out * g_], []
    (og,), _ = _rowwise("b_out", out_body, [y, r, k2, v, g], [p["gn_g"], p["gn_b"], p["r_k"], p["seg"]], [(d, BF16)], tm=256)
    o = _mm("b_o", og, p["w_o"])
    res = _resid_norm_fwd("b_res", x, o, p["g_post"], chain=chain)
    return res, (x, hn, xr, xw, xk, xv, xa, xg, r, k, v, th, la, sg, wl, za, g, k2, rec_in, states, y, og, o)


def _rwkv_bwd(saved, dres, p, pt, seq):
    x, hn, xr, xw, xk, xv, xa, xg, r, k, v, th, la, sg, wl, za, g, k2, rec_in, states, y, og, o = saved
    t, d = x.shape
    nseq = t // seq
    do, dg_post, _ = _resid_norm_bwd("b_dres", dres, o, p["g_post"])
    dog = _mm("b_dog", do, pt["w_o_t"])
    dw_o = _mm("b_dwo", og, do, trans_a=True)

    def dout_body(rv, cv, pv, nv):
        dog_, y_, r_, k2_, v_, g_ = rv
        gn_g, gn_b, r_k, bd = cv
        inv = 1.0 / RWKV_HEAD
        rstd, yhat, s, out = _rwkv_out(y_, r_, k2_, v_, gn_g, gn_b, r_k, bd)
        dout = dog_ * g_
        ds = _seg_sum(dout * v_, bd)
        dyhat = dout * gn_g
        dy = rstd * (dyhat - _seg_sum(dyhat, bd) * inv - yhat * _seg_sum(dyhat * yhat, bd) * inv)
        return [dy, dog_ * out, dout * s, ds * k2_ * r_k, ds * r_ * r_k], [ds * r_ * k2_, dout * yhat, dout]
    (dy, dgate, dv_b, dr_b, dk2_b), (dr_k, dgn_g, dgn_b) = _rowwise(
        "b_dout", dout_body, [dog, y, r, k2, v, g], [p["gn_g"], p["gn_b"], p["r_k"], p["seg"]],
        [(d, F32), (d, BF16), (d, BF16), (d, BF16), (d, BF16)], [d, d, d], tm=256)
    dr_rec, dlw_rec, dk2_rec, dv_rec, da_rec, db_rec = _wkv_bwd(*rec_in, *states, dy, seq)

    def dprep_body(rv, cv, pv, nv):
        dr_rec_, dlw_rec_, dk2_rec_, dv_rec_, da_rec_, db_rec_, dr_b_, dk2_b_, dv_b_, k_, wl_, za_ = rv
        w0, a0, k_k, k_a, bd = cv
        w_in, e_w, a, norm, n, kk = _rwkv_prep(k_, wl_, za_, w0, a0, k_k, k_a, bd)
        dk2 = dk2_rec_ + dk2_b_
        dkk = db_rec_ * a - da_rec_
        da = db_rec_ * kk + dk2 * k_ * k_a
        dq = jnp.where(norm > 1e-12, dkk - kk * _seg_sum(kk * dkk, bd), dkk) / n
        dk = dk2 * (1.0 + (a - 1.0) * k_a) + dq * k_k
        dza = da * a * (1.0 - a)
        dwl = dlw_rec_ * (-e_w) * jax.nn.sigmoid(-w_in)
        return [dr_rec_ + dr_b_, dk, dv_rec_ + dv_b_, dza, dwl], [dk2 * k_ * (a - 1.0), dq * k_, dza, dwl]
    (dr, dk, dv, dza, dwl), (dk_a, dk_k, da0, dw0) = _rowwise(
        "b_dprep", dprep_body, [dr_rec, dlw_rec, dk2_rec, dv_rec, da_rec, db_rec, dr_b, dk2_b, dv_b, k, wl, za],
        [p["w0"], p["a0"], p["k_k"], p["k_a"], p["seg"]], [(d, BF16)] * 5, [d] * 4, tm=256)

    dw_r = _mm("b_dwr", xr, dr, trans_a=True)
    dw_k = _mm("b_dwk", xk, dk, trans_a=True)
    dw_v = _mm("b_dwv", xv, dv, trans_a=True)
    dxr = _mm("b_dxr", dr, pt["w_r_t"])
    dxk = _mm("b_dxk", dk, pt["w_k_t"])
    dxv = _mm("b_dxv", dv, pt["w_v_t"])
    da2 = _mm("b_da2", la, dza, trans_a=True)
    dla = _mm("b_dla", dza, pt["a2_t"], out_dtype=BF16)
    da1 = _mm("b_da1", xa, dla, trans_a=True)
    dxa = _mm("b_dxa", dla, pt["a1_t"])
    dw2 = _mm("b_dw2", th, dwl, trans_a=True)
    dth = _mm("b_dth", dwl, pt["w2_t"])
    (dzw,), _ = _rowwise("b_dtanh", lambda rv, cv, pv, nv: ([rv[0] * (1.0 - _f32(rv[1]) * _f32(rv[1]))], []),
                         [dth, th], [], [(th.shape[1], BF16)])
    dw1 = _mm("b_dw1", xw, dzw, trans_a=True)
    dxw = _mm("b_dxw", dzw, pt["w1_t"])
    dg2 = _mm("b_dg2", sg, dgate, trans_a=True)
    dsg = _mm("b_dsg", dgate, pt["g2_t"])
    (dzg,), _ = _rowwise("b_dsig", lambda rv, cv, pv, nv: ([rv[0] * _f32(rv[1]) * (1.0 - _f32(rv[1]))], []),
                         [dsg, sg], [], [(sg.shape[1], BF16)])
    dg1 = _mm("b_dg1", xg, dzg, trans_a=True)
    dxg = _mm("b_dxg", dzg, pt["g1_t"])

    def dmix_body(rv, cv, pv, nv):
        hn_ = rv[0]
        dxs = rv[1:]
        mu = cv[0]
        xx = _shift_down(hn_, pv[0], 1) - hn_
        dsum = dxs[0]
        dxx = dxs[0] * mu[0:1]
        dxx_next = nv[0] * mu[0:1]
        for c in range(1, 6):
            dsum = dsum + dxs[c]
            dxx = dxx + dxs[c] * mu[c:c + 1]
            dxx_next = dxx_next + nv[c] * mu[c:c + 1]
        return [dsum - dxx + _shift_up(dxx, dxx_next, 1)], [dxs[c] * xx for c in range(6)]
    (dhn,), dmu = _rowwise("b_dmix", dmix_body, [hn, dxr, dxw, dxk, dxv, dxa, dxg], [p["mu"]], [(d, F32)], [d] * 6,
                           prev=[0], nxt=[1, 2, 3, 4, 5, 6], seq=seq, tm=256)
    dx, dg_pre = _prenorm_bwd("b_dnorm", dres, dhn, x, p["g_pre"])
    grads = dict(g_pre=dg_pre, g_post=dg_post, mu=jnp.stack([m.sum(0) for m in dmu]), w_r=dw_r, w_k=dw_k, w_v=dw_v,
                 w0=dw0.sum(0), w1=dw1, w2=dw2, a0=da0.sum(0), a1=da1, a2=da2, g1=dg1, g2=dg2, k_k=dk_k.sum(0),
                 k_a=dk_a.sum(0), r_k=dr_k.sum(0), gn_g=dgn_g.sum(0), gn_b=dgn_b.sum(0), w_o=dw_o)
    return dx, grads


_WEIGHTS = ['ln_gains', 'mem_norm', 'a_conv_w', 'a_conv_b', 'a_w_in', 'a_b_in', 'a_gate_w', 'a_gate_b', 'a_lambda', 'a_w_out',
            'a_b_out', 'b_mu', 'b_w_rkv', 'b_w0', 'b_w1', 'b_w2', 'b_a0', 'b_a1', 'b_a2', 'b_g1', 'b_g2', 'b_k_k', 'b_k_a',
            'b_r_k', 'b_gn_g', 'b_gn_b', 'b_w_o', 'c_w_q', 'c_w_kv', 'c_w_o', 'm_w_up', 'm_w_down']
_SHARD_AXIS = dict(ln_gains=2, mem_norm=None, a_conv_w=2, a_conv_b=None, a_w_in=2, a_b_in=None, a_gate_w=3, a_gate_b=3,
                   a_lambda=None, a_w_out=1, a_b_out=None, b_mu=2, b_w_rkv=2, b_w0=1, b_w1=1, b_w2=2, b_a0=1, b_a1=1, b_a2=2,
                   b_g1=1, b_g2=2, b_k_k=1, b_k_a=1, b_r_k=None, b_gn_g=1, b_gn_b=1, b_w_o=1, c_w_q=1, c_w_kv=2, c_w_o=1,
                   m_w_up=2, m_w_down=1)
_MATRICES = ['a_w_in', 'a_gate_w', 'a_w_out', 'b_w_rkv', 'b_w1', 'b_w2', 'b_a1', 'b_a2', 'b_g1', 'b_g2', 'b_w_o', 'c_w_q',
             'c_w_kv', 'c_w_o', 'm_w_up', 'm_w_down']
_SHARDED = [n for n in _WEIGHTS if _SHARD_AXIS[n] is not None]
_VECTORS = [n for n in _SHARDED if n not in _MATRICES]
_REPLICATED = [n for n in _WEIGHTS if _SHARD_AXIS[n] is None]
N_XY = 4
N_DEV = 8
PACK_W = 1024
PACK_ROWS = 256


def _pack(arrs, dtype, row_mult=PACK_ROWS):
    parts = []
    rows = 0
    for a in arrs:
        n = a.size
        r = -(-n // PACK_W)
        parts.append(jnp.pad(a.reshape(-1).astype(dtype), (0, r * PACK_W - n)))
        rows += r
    pad_rows = -(-rows // row_mult) * row_mult - rows
    if pad_rows:
        parts.append(jnp.zeros((pad_rows * PACK_W,), dtype))
    return jnp.concatenate(parts).reshape(-1, PACK_W)


def _unpack(flat, shapes):
    out = []
    row = 0
    for shp in shapes:
        n = 1
        for s in shp:
            n *= s
        r = -(-n // PACK_W)
        out.append(flat[row:row + r].reshape(-1)[:n].reshape(shp))
        row += r
    return out


_ANY = pl.BlockSpec(memory_space=pl.ANY)


def _xy_peers():
    x, y = lax.axis_index("x"), lax.axis_index("y")
    return [(1 - x, y), (x, 1 - y), (1 - x, 1 - y)]


def _all_gather_xy(wm, wv):
    half = wm.shape[0] // 2

    def body(wm_ref, wv_ref, gm_ref, gv_ref, send_sems, recv_sems, local_sems):
        x, y, c = lax.axis_index("x"), lax.axis_index("y"), lax.axis_index("c")
        me = 2 * x + y
        mine = pl.ds(pl.multiple_of(c * half, SUBLANES), half)
        other = pl.ds(pl.multiple_of((1 - c) * half, SUBLANES), half)
        local = [pltpu.make_async_copy(wm_ref, gm_ref.at[me], local_sems.at[0]),
                 pltpu.make_async_copy(wv_ref, gv_ref.at[me], local_sems.at[1])]
        for cp in local:
            cp.start()
        sends, lands, passes, from_sibling = [], [], [], []
        for j, (px, py) in enumerate(_xy_peers()):
            peer = 2 * px + py
            ici = functools.partial(pltpu.make_async_remote_copy, device_id=(px, py, c), device_id_type=MESH)
            sends.append(ici(src_ref=wm_ref.at[mine], dst_ref=gm_ref.at[me, mine], send_sem=send_sems.at[j], recv_sem=recv_sems.at[j]))
            lands.append(ici(src_ref=wm_ref.at[mine], dst_ref=gm_ref.at[peer, mine], send_sem=send_sems.at[j], recv_sem=recv_sems.at[j]))
            sends.append(ici(src_ref=wv_ref, dst_ref=gv_ref.at[me], send_sem=send_sems.at[3 + j], recv_sem=recv_sems.at[3 + j]))
            lands.append(ici(src_ref=wv_ref, dst_ref=gv_ref.at[peer], send_sem=send_sems.at[3 + j], recv_sem=recv_sems.at[3 + j]))
            d2d = functools.partial(pltpu.make_async_remote_copy, send_sem=send_sems.at[6 + j], recv_sem=recv_sems.at[6 + j],
                                    device_id=(x, y, 1 - c), device_id_type=MESH)
            passes.append(d2d(src_ref=gm_ref.at[peer, mine], dst_ref=gm_ref.at[peer, mine]))
            from_sibling.append(d2d(src_ref=gm_ref.at[peer, other], dst_ref=gm_ref.at[peer, other]))
        for cp in sends:
            cp.start()
        for j in range(N_XY - 1):
            lands[2 * j].wait_recv()
            passes[j].start()
        for j in range(N_XY - 1):
            lands[2 * j + 1].wait_recv()
        for cp in from_sibling:
            cp.wait_recv()
        for cp in sends + passes:
            cp.wait_send()
        for cp in local:
            cp.wait()

    return pl.pallas_call(
        body, name="all_gather_weights",
        in_specs=[_ANY, _ANY], out_specs=[_ANY, _ANY],
        out_shape=[jax.ShapeDtypeStruct((N_XY,) + wm.shape, wm.dtype), jax.ShapeDtypeStruct((N_XY,) + wv.shape, wv.dtype)],
        scratch_shapes=[pltpu.SemaphoreType.DMA((9,)), pltpu.SemaphoreType.DMA((9,)), pltpu.SemaphoreType.DMA((2,))],
    )(wm, wv)


_HBM = pl.BlockSpec(memory_space=pltpu.HBM)
_SEM = pl.BlockSpec(memory_space=pltpu.SEMAPHORE)
_SPLIT_COPY = functools.partial(pltpu.CompilerParams, has_side_effects=pltpu.SideEffectType.DATAFLOW_SIDE_EFFECTING)


def _gather_xy_start(name, buf, after):
    def body(src_ref, land_ref, after_ref, send_sems, recv_sems, src_thru, land_thru, token):
        x, y, c = lax.axis_index("x"), lax.axis_index("y"), lax.axis_index("c")
        for j, (px, py) in enumerate(_xy_peers()):
            pltpu.make_async_remote_copy(src_ref=src_ref, dst_ref=land_ref.at[2 * x + y], send_sem=send_sems.at[j],
                                         recv_sem=recv_sems.at[j], device_id=(px, py, c), device_id_type=MESH).start()
        token[...] = jnp.zeros_like(token)

    n_peers = N_XY - 1
    land = pltpu.with_memory_space_constraint(lax.empty((N_XY,) + buf.shape, buf.dtype), pltpu.HBM)
    return pl.pallas_call(
        body, name=name,
        out_shape=(pltpu.SemaphoreType.DMA((n_peers,)), pltpu.SemaphoreType.DMA((n_peers,)), pltpu.HBM(buf.shape, buf.dtype),
                   pltpu.HBM(land.shape, buf.dtype), jax.ShapeDtypeStruct((SUBLANES, LANES), F32)),
        in_specs=(_HBM, _HBM, _ANY), out_specs=(_SEM, _SEM, _HBM, _HBM, pl.BlockSpec(memory_space=pltpu.VMEM)),
        input_output_aliases={0: 2, 1: 3}, compiler_params=_SPLIT_COPY(),
    )(pltpu.with_memory_space_constraint(buf, pltpu.HBM), land, after)


def _gather_xy_wait(name, send_sems, recv_sems, src_thru, land_thru, after):
    def body(src_ref, land_ref, send_sems, recv_sems, after_ref, src_dead, got_ref):
        c = lax.axis_index("c")
        for j, (px, py) in enumerate(_xy_peers()):
            cp = pltpu.make_async_remote_copy(src_ref=src_ref, dst_ref=land_ref.at[2 * px + py], send_sem=send_sems.at[j],
                                              recv_sem=recv_sems.at[j], device_id=(px, py, c), device_id_type=MESH)
            cp.wait_send()
            cp.wait_recv()

    return pl.pallas_call(
        body, name=name,
        out_shape=(pltpu.HBM(src_thru.shape, src_thru.dtype), pltpu.HBM(land_thru.shape, land_thru.dtype)),
        in_specs=(_HBM, _HBM, _SEM, _SEM, _ANY), out_specs=(_HBM, _HBM),
        input_output_aliases={0: 0, 1: 1}, compiler_params=_SPLIT_COPY(),
    )(src_thru, land_thru, send_sems, recv_sems, after)[1]


def _exchange_xy_start(name, gsend, after):
    def body(src_ref, land_ref, after_ref, send_sems, recv_sems, src_thru, land_thru, token):
        c = lax.axis_index("c")
        for j, (px, py) in enumerate(_xy_peers()):
            pltpu.make_async_remote_copy(src_ref=src_ref.at[2 * px + py], dst_ref=land_ref.at[j], send_sem=send_sems.at[j],
                                         recv_sem=recv_sems.at[j], device_id=(px, py, c), device_id_type=MESH).start()
        token[...] = jnp.zeros_like(token)

    n_peers = N_XY - 1
    land = pltpu.with_memory_space_constraint(lax.empty((n_peers,) + gsend.shape[1:], gsend.dtype), pltpu.HBM)
    return pl.pallas_call(
        body, name=name,
        out_shape=(pltpu.SemaphoreType.DMA((n_peers,)), pltpu.SemaphoreType.DMA((n_peers,)), pltpu.HBM(gsend.shape, gsend.dtype),
                   pltpu.HBM(land.shape, gsend.dtype), jax.ShapeDtypeStruct((SUBLANES, LANES), F32)),
        in_specs=(_HBM, _HBM, _ANY), out_specs=(_SEM, _SEM, _HBM, _HBM, pl.BlockSpec(memory_space=pltpu.VMEM)),
        input_output_aliases={0: 2, 1: 3}, compiler_params=_SPLIT_COPY(),
    )(pltpu.with_memory_space_constraint(gsend, pltpu.HBM), land, after)


def _exchange_xy_wait(name, send_sems, recv_sems, src_thru, land_thru, after):
    def body(src_ref, land_ref, send_sems, recv_sems, after_ref, src_dead, got_ref):
        c = lax.axis_index("c")
        for j, (px, py) in enumerate(_xy_peers()):
            cp = pltpu.make_async_remote_copy(src_ref=src_ref.at[2 * px + py], dst_ref=land_ref.at[j], send_sem=send_sems.at[j],
                                              recv_sem=recv_sems.at[j], device_id=(px, py, c), device_id_type=MESH)
            cp.wait_send()
            cp.wait_recv()

    return pl.pallas_call(
        body, name=name,
        out_shape=(pltpu.HBM(src_thru.shape, src_thru.dtype), pltpu.HBM(land_thru.shape, land_thru.dtype)),
        in_specs=(_HBM, _HBM, _SEM, _SEM, _ANY), out_specs=(_HBM, _HBM),
        input_output_aliases={0: 0, 1: 1}, compiler_params=_SPLIT_COPY(),
    )(src_thru, land_thru, send_sems, recv_sems, after)[1]


def _exchange_xy(gsend):
    def body(gs_ref, recv_ref, send_sems, recv_sems):
        c = lax.axis_index("c")
        sends = []
        for j, (px, py) in enumerate(_xy_peers()):
            sends.append(pltpu.make_async_remote_copy(
                src_ref=gs_ref.at[2 * px + py], dst_ref=recv_ref.at[j], send_sem=send_sems.at[j], recv_sem=recv_sems.at[j],
                device_id=(px, py, c), device_id_type=MESH))
        for cp in sends:
            cp.start()
        for cp in sends:
            cp.wait_recv()
        for cp in sends:
            cp.wait_send()

    return pl.pallas_call(
        body, name="exchange_grads",
        in_specs=[_ANY], out_specs=_ANY,
        out_shape=jax.ShapeDtypeStruct((N_XY - 1,) + gsend.shape[1:], gsend.dtype),
        scratch_shapes=[pltpu.SemaphoreType.DMA((3,)), pltpu.SemaphoreType.DMA((3,))],
    )(gsend)


def _swap_with_sibling(name, part):
    def body(p_ref, got_ref, send_sem, recv_sem):
        x, y, c = lax.axis_index("x"), lax.axis_index("y"), lax.axis_index("c")
        cp = pltpu.make_async_remote_copy(src_ref=p_ref, dst_ref=got_ref, send_sem=send_sem, recv_sem=recv_sem,
                                          device_id=(x, y, 1 - c), device_id_type=MESH)
        cp.start()
        cp.wait_recv()
        cp.wait_send()

    return pl.pallas_call(
        body, name=name,
        in_specs=[_ANY], out_specs=_ANY, out_shape=jax.ShapeDtypeStruct(part.shape, part.dtype),
        scratch_shapes=[pltpu.SemaphoreType.DMA, pltpu.SemaphoreType.DMA],
    )(part)


def _all_gather_all(vec):
    def body(v_ref, out_ref, send_sems, recv_sems, local_sem):
        x, y, c = lax.axis_index("x"), lax.axis_index("y"), lax.axis_index("c")
        me = 4 * x + 2 * y + c
        local = pltpu.make_async_copy(v_ref, out_ref.at[me], local_sem)
        local.start()
        sends, recvs = [], []
        for f in range(1, N_DEV):
            fx, fy, fc = (f >> 2) & 1, (f >> 1) & 1, f & 1
            px = (1 - x) if fx else x
            py = (1 - y) if fy else y
            pc = (1 - c) if fc else c
            mk = functools.partial(pltpu.make_async_remote_copy, src_ref=v_ref, send_sem=send_sems.at[f - 1],
                                   recv_sem=recv_sems.at[f - 1], device_id=(px, py, pc), device_id_type=MESH)
            sends.append(mk(dst_ref=out_ref.at[me]))
            recvs.append(mk(dst_ref=out_ref.at[4 * px + 2 * py + pc]))
        for cp in sends:
            cp.start()
        for cp in recvs:
            cp.wait_recv()
        for cp in sends:
            cp.wait_send()
        local.wait()

    return pl.pallas_call(
        body, name="all_gather_replicated",
        in_specs=[_ANY], out_specs=_ANY, out_shape=jax.ShapeDtypeStruct((N_DEV,) + vec.shape, vec.dtype),
        scratch_shapes=[pltpu.SemaphoreType.DMA((N_DEV - 1,)), pltpu.SemaphoreType.DMA((N_DEV - 1,)), pltpu.SemaphoreType.DMA],
    )(vec)


def _adamw(g, w, m, v):
    m2 = ADAM_B1 * m + (1.0 - ADAM_B1) * g
    v2 = ADAM_B2 * v + (1.0 - ADAM_B2) * g * g
    m_hat = m2 / (1.0 - ADAM_B1 ** ADAM_STEP)
    v_hat = v2 / (1.0 - ADAM_B2 ** ADAM_STEP)
    return -ADAM_LR * (m_hat / (jnp.sqrt(v_hat) + ADAM_EPS) + ADAM_WD * w), m2, v2


def _sum_contributions(name, own, recv):
    def body(rv, cv, pv, nv):
        return [((rv[0] + _f32(rv[1])) + _f32(rv[2])) + _f32(rv[3])], []
    stacked = recv.reshape(-1, PACK_W)
    (part,), _ = _rowwise(name, body, [own] + [(stacked, PACK_W, 0, j * own.shape[0]) for j in range(N_XY - 1)], [],
                          [(PACK_W, F32)])
    return part


def _adamw_sharded(name, part, sib, first, w, m, v, w_first=0, n_rows=None):
    def body(rv, cv, pv, nv):
        g = rv[3] + rv[4]
        return [g, *_adamw(g, rv[0], rv[1], rv[2])], []
    n_rows = w.shape[0] if n_rows is None else n_rows
    rows = [(z, PACK_W, 0, w_first) for z in (w, m, v)] + [(part, PACK_W, 0, first), (sib, PACK_W, 0, first)]
    outs, _ = _rowwise(name, body, rows, [], [(PACK_W, F32)] * 4, tm=256, n_rows=n_rows)
    return outs


def _adamw_replicated(parts, w, m, v):
    def body(rv, cv, pv, nv):
        g = rv[0]
        for i in range(1, N_DEV):
            g = g + rv[i]
        return [g, *_adamw(g, rv[N_DEV], rv[N_DEV + 1], rv[N_DEV + 2])], []
    outs, _ = _rowwise("adamw_replicated", body, [parts[i] for i in range(N_DEV)] + [w, m, v], [], [(PACK_W, F32)] * 4)
    return outs


def _row(v):
    return v.reshape(1, -1).astype(F32)


def _local_step(x3, mem3, target3, fw, late_weights=None, on_grads=None):
    def with_late(stage, after):
        if late_weights is None:
            return fw
        got = late_weights(stage, after)
        return {**fw, **{n: ({**fw.get(n, {}), **v} if isinstance(v, dict) else v) for n, v in got.items()}}

    nseq, seq, d = x3.shape
    mem_len = mem3.shape[1]
    t = nseq * seq
    x0 = x3.reshape(t, d)
    mem2 = mem3.reshape(nseq * mem_len, d)
    target = target3.reshape(t, d)
    ln = fw["ln_gains"]
    gains = [[_row(ln[i, j]) for j in range(6)] for i in range(2)]
    nh = d // RWKV_HEAD
    seg = (jnp.arange(d)[:, None] // RWKV_HEAD == jnp.arange(LANES)[None, :]).astype(BF16)

    w_gate = fw["a_gate_w"][0]
    pa = dict(g_pre=gains[0][0], g_post=gains[0][1], w_in=fw["a_w_in"][0], b_in_y=_row(fw["a_b_in"][0, :d]),
              b_in_u=_row(fw["a_b_in"][0, d:]), conv_w=fw["a_conv_w"][0].astype(F32), conv_b=_row(fw["a_conv_b"][0]),
              w_gate=w_gate, gate_b0=_row(fw["a_gate_b"][0, 0]), gate_b1=_row(fw["a_gate_b"][0, 1]), lam=_row(fw["a_lambda"][0]),
              w_out=fw["a_w_out"][0], b_out=_row(fw["a_b_out"][0]))
    pta = dict(w_in_t=_t(pa["w_in"]), w_out_t=_t(pa["w_out"]))
    mem_g = _row(fw["mem_norm"])

    mem_n = _norm_fwd("mem_norm", mem2, mem_g)
    to_c0, to_m0, to_c1, to_m1, to_loss = ({"gain": gains[0][2]}, {"gain": gains[0][4]}, {"gain": gains[1][2]},
                                           {"gain": gains[1][4]}, {"target": target})
    x1, sv_a = _rglru_fwd(x0, pa, seq, chain=to_c0)
    fw = with_late("cm0", x1)
    x2, sv_c0 = _xattn_fwd("c0", x1, mem_n, gains[0][2], gains[0][3], fw["c_w_q"][0], fw["c_w_kv"][0], fw["c_w_o"][0], seq, mem_len,
                           hn=to_c0["hn"], chain=to_m0)
    x3_, sv_m0 = _mlp_fwd("m0", x2, gains[0][4], gains[0][5], fw["m_w_up"][0], fw["m_w_down"][0], hn=to_m0["hn"])
    fw = with_late("layer1", x3_)
    pb = dict(g_pre=gains[1][0], g_post=gains[1][1], mu=fw["b_mu"][0].astype(F32), w_r=fw["b_w_rkv"][0, 0],
              w_k=fw["b_w_rkv"][0, 1], w_v=fw["b_w_rkv"][0, 2], w0=_row(fw["b_w0"][0]), w1=fw["b_w1"][0], w2=fw["b_w2"][0],
              a0=_row(fw["b_a0"][0]), a1=fw["b_a1"][0], a2=fw["b_a2"][0], g1=fw["b_g1"][0], g2=fw["b_g2"][0],
              k_k=_row(fw["b_k_k"][0]), k_a=_row(fw["b_k_a"][0]), r_k=_row(fw["b_r_k"][0]), gn_g=_row(fw["b_gn_g"][0]),
              gn_b=_row(fw["b_gn_b"][0]), w_o=fw["b_w_o"][0], seg=seg)
    ptb = {k + "_t": _t(pb[k]) for k in ("w_r", "w_k", "w_v", "w_o", "w1", "w2", "a1", "a2", "g1", "g2")}
    x4, sv_b = _rwkv_fwd(x3_, pb, seq, chain=to_c1)
    x5, sv_c1 = _xattn_fwd("c1", x4, mem_n, gains[1][2], gains[1][3], fw["c_w_q"][1], fw["c_w_kv"][1], fw["c_w_o"][1], seq, mem_len,
                           hn=to_c1["hn"], chain=to_m1)
    _, sv_m1 = _mlp_fwd("m1", x5, gains[1][4], gains[1][5], fw["m_w_up"][1], fw["m_w_down"][1], hn=to_m1["hn"], chain=to_loss)
    dx = to_loss["dy"]
    loss_part = 0.5 / d * jnp.sum(to_loss["sq"])

    dx, g_m1 = _mlp_bwd("m1", sv_m1, dx, gains[1][4], gains[1][5], _t(fw["m_w_up"][1]), _t(fw["m_w_down"][1]))
    dx, dmem1, g_c1 = _xattn_bwd("c1", sv_c1, dx, mem_n, gains[1][2], gains[1][3], _t(fw["c_w_q"][1]), _t(fw["c_w_kv"][1]),
                                 _t(fw["c_w_o"][1]), seq, mem_len)
    dx, g_b = _rwkv_bwd(sv_b, dx, pb, ptb, seq)
    grads_b = dict(
        b_mu=g_b["mu"][None], b_w_rkv=jnp.stack([g_b["w_r"], g_b["w_k"], g_b["w_v"]])[None], b_w0=g_b["w0"][None],
        b_w1=g_b["w1"][None], b_w2=g_b["w2"][None], b_a0=g_b["a0"][None], b_a1=g_b["a1"][None], b_a2=g_b["a2"][None],
        b_g1=g_b["g1"][None], b_g2=g_b["g2"][None], b_k_k=g_b["k_k"][None], b_k_a=g_b["k_a"][None],
        b_r_k=g_b["r_k"].reshape(1, nh, RWKV_HEAD), b_gn_g=g_b["gn_g"][None], b_gn_b=g_b["gn_b"][None], b_w_o=g_b["w_o"][None])
    g_m0_pre = gains[0][4]
    if on_grads is not None:
        layer1 = dict(c_w_q=g_c1["w_q"], c_w_kv=g_c1["w_kv"], c_w_o=g_c1["w_o"], m_w_up=g_m1["w_up"], m_w_down=g_m1["w_down"])
        g_m0_pre = g_m0_pre + on_grads("layer1", {**grads_b, **{n: [None, g] for n, g in layer1.items()}})[0, 0]
    dx, g_m0 = _mlp_bwd("m0", sv_m0, dx, g_m0_pre, gains[0][5], _t(fw["m_w_up"][0]), _t(fw["m_w_down"][0]))
    dx, dmem0, g_c0 = _xattn_bwd("c0", sv_c0, dx, mem_n, gains[0][2], gains[0][3], _t(fw["c_w_q"][0]), _t(fw["c_w_kv"][0]),
                                 _t(fw["c_w_o"][0]), seq, mem_len)
    if on_grads is not None:
        cm0 = dict(c_w_q=g_c0["w_q"], c_w_kv=g_c0["w_kv"], c_w_o=g_c0["w_o"], m_w_up=g_m0["w_up"], m_w_down=g_m0["w_down"])
        pa = {**pa, "g_post": pa["g_post"] + on_grads("cm0", {n: [g, None] for n, g in cm0.items()})[0, 0]}
    dx, g_a = _rglru_bwd(sv_a, dx, pa, pta, seq)

    def dmem_body(rv, cv, pv, nv):
        _, dg = _rms_bwd(rv[1] + rv[2], rv[0], cv[0])
        return [], [dg]
    _, (dmem_g,) = _rowwise("mem_norm_grad", dmem_body, [mem2, dmem0, dmem1], [mem_g], [], [d])

    lru_heads = fw["a_gate_w"].shape[2]
    blk = d // lru_heads
    grads = dict(
        ln_gains=jnp.stack([jnp.stack([g_a["g_pre"], g_a["g_post"], g_c0["g_pre"], g_c0["g_post"], g_m0["g_pre"], g_m0["g_post"]]),
                            jnp.stack([g_b["g_pre"], g_b["g_post"], g_c1["g_pre"], g_c1["g_post"], g_m1["g_pre"], g_m1["g_post"]])]),
        mem_norm=dmem_g.sum(0),
        a_conv_w=g_a["conv_w"][None], a_conv_b=g_a["conv_b"][None], a_w_in=g_a["w_in"][None], a_b_in=g_a["b_in"][None],
        a_gate_w=g_a["w_gate"][None],
        a_gate_b=jnp.stack([g_a["gate_b0"], g_a["gate_b1"]]).reshape(1, 2, lru_heads, blk),
        a_lambda=g_a["lam"][None], a_w_out=g_a["w_out"][None], a_b_out=g_a["b_out"][None],
        **grads_b,
        c_w_q=[g_c0["w_q"], g_c1["w_q"]], c_w_kv=[g_c0["w_kv"], g_c1["w_kv"]], c_w_o=[g_c0["w_o"], g_c1["w_o"]],
        m_w_up=[g_m0["w_up"], g_m1["w_up"]], m_w_down=[g_m0["w_down"], g_m1["w_down"]],
    )
    return loss_part, dx.reshape(nseq, seq, d), grads


def kernel(x, mem, ln_gains, mem_norm, a_conv_w, a_conv_b, a_w_in, a_b_in, a_gate_w, a_gate_b, a_lambda, a_w_out, a_b_out, b_mu, b_w_rkv, b_w0, b_w1, b_w2, b_a0, b_a1, b_a2, b_g1, b_g2, b_k_k, b_k_a, b_r_k, b_gn_g, b_gn_b, b_w_o, c_w_q, c_w_kv, c_w_o, m_w_up, m_w_down, loss_target, m_ln_gains, m_mem_norm, m_a_conv_w, m_a_conv_b, m_a_w_in, m_a_b_in, m_a_gate_w, m_a_gate_b, m_a_lambda, m_a_w_out, m_a_b_out, m_b_mu, m_b_w_rkv, m_b_w0, m_b_w1, m_b_w2, m_b_a0, m_b_a1, m_b_a2, m_b_g1, m_b_g2, m_b_k_k, m_b_k_a, m_b_r_k, m_b_gn_g, m_b_gn_b, m_b_w_o, m_c_w_q, m_c_w_kv, m_c_w_o, m_m_w_up, m_m_w_down, v_ln_gains, v_mem_norm, v_a_conv_w, v_a_conv_b, v_a_w_in, v_a_b_in, v_a_gate_w, v_a_gate_b, v_a_lambda, v_a_w_out, v_a_b_out, v_b_mu, v_b_w_rkv, v_b_w0, v_b_w1, v_b_w2, v_b_a0, v_b_a1, v_b_a2, v_b_g1, v_b_g2, v_b_k_k, v_b_k_a, v_b_r_k, v_b_gn_g, v_b_gn_b, v_b_w_o, v_c_w_q, v_c_w_kv, v_c_w_o, v_m_w_up, v_m_w_down):
    given = dict(locals())
    w = {n: given[n] for n in _WEIGHTS}
    mom1 = {n: given["m_" + n] for n in _WEIGHTS}
    mom2 = {n: given["v_" + n] for n in _WEIGHTS}

    me = 2 * lax.axis_index("x") + lax.axis_index("y")
    per_layer = [n for n in _MATRICES if n[0] in "cm"]
    early = [(n, None) for n in _MATRICES if n[0] == "a"]
    late = dict(cm0=[(n, 0) for n in per_layer], layer1=[(n, None) for n in _MATRICES if n[0] == "b"] + [(n, 1) for n in per_layer])
    piece = lambda n, layer: w[n] if layer is None else w[n][layer:layer + 1]

    def gathered(entries, buf):
        shards = [_unpack(buf[s], [piece(n, layer).shape for n, layer in entries]) for s in range(N_XY)]
        out = {}
        for i, (n, layer) in enumerate(entries):
            full = jnp.concatenate([shards[s][i] for s in range(N_XY)], axis=_SHARD_AXIS[n])
            if layer is None:
                out[n] = full
            else:
                out.setdefault(n, {})[layer] = full[0]
        return out

    late_bufs = {stage: _pack([piece(n, layer) for n, layer in entries], BF16) for stage, entries in late.items()}
    gm, gv = _all_gather_xy(_pack([piece(n, layer) for n, layer in early], BF16), _pack([w[n] for n in _VECTORS], F32, SUBLANES))
    in_flight = {}
    token = gv
    for stage in late:
        *in_flight[stage], token = _gather_xy_start("gather_%s_start" % stage, late_bufs[stage], token)
    fw = {n: w[n] for n in _REPLICATED} | gathered(early, gm)
    vec_shards = [_unpack(gv[s], [w[n].shape for n in _VECTORS]) for s in range(N_XY)]
    for i, n in enumerate(_VECTORS):
        fw[n] = jnp.concatenate([vec_shards[s][i] for s in range(N_XY)], axis=_SHARD_AXIS[n])
    fw["ln_gains"] = fw["ln_gains"] + token[0, 0]

    def late_weights(stage, after):
        land = _gather_xy_wait("gather_%s_wait" % stage, *in_flight[stage], after)
        return gathered(late[stage], lax.dynamic_update_index_in_dim(land, late_bufs[stage], me, 0))

    tile_rows = 256
    groups = dict(layer1=[(n, 1 if n in per_layer else None) for n in _SHARDED if n in per_layer or n[0] == "b"],
                  cm0=[(n, 0) for n in _SHARDED if n in per_layer],
                  rest=[(n, None) for n in _SHARDED if n not in per_layer and n[0] != "b"])

    def piece_shape(n, layer):
        return w[n].shape if layer is None else (1,) + w[n].shape[1:]

    def rows_of(n, layer):
        size = 1
        for s_ in piece_shape(n, layer):
            size *= s_
        return size // PACK_W

    def split(entries):
        in_place = [e for e in entries if w[e[0]].shape[-1] == PACK_W and rows_of(*e) % tile_rows == 0 and rows_of(*e) > 0]
        in_place.sort(key=lambda e: -rows_of(*e))
        return in_place, [e for e in entries if e not in in_place]

    def buffers(entries, grads):
        in_place, packed = split(entries)

        def pieces(s):
            out = []
            for n, layer in in_place + packed:
                ax = _SHARD_AXIS[n]
                size = w[n].shape[ax]
                g = grads[n] if layer is None else grads[n][layer]
                out.append(lax.dynamic_slice_in_dim(g, s * size, size, axis=ax if layer is None else ax - 1))
            return out
        own = _pack(pieces(me), F32)
        n_rows = own.shape[0]
        used = sum(-(-rows_elems // PACK_W) for rows_elems in (p.size for p in pieces(0)))
        fill = [jnp.zeros(((n_rows - used) * PACK_W,), BF16)] if n_rows > used else []
        gsend = _pack([p for s in range(N_XY) for p in pieces(s) + fill], BF16, row_mult=1)
        return gsend.reshape(N_XY, n_rows, PACK_W), own

    def update(tag, entries, own, recv):
        in_place, packed = split(entries)
        part = _sum_contributions("sum_grads_" + tag, own, recv)
        sib = _swap_with_sibling("swap_sibling_" + tag, part)
        out = {}
        first = 0
        for n, layer in in_place:
            flat = [src[n].reshape(-1, PACK_W) for src in (w, mom1, mom2)]
            rows = rows_of(n, layer)
            res = _adamw_sharded("adamw_%s_%s" % (n, tag), part, sib, first, *flat, w_first=(layer or 0) * rows, n_rows=rows)
            out[(n, layer)] = [o.reshape(piece_shape(n, layer)) for o in res]
            first += rows
        take = lambda src, n, layer: src[n] if layer is None else src[n][layer:layer + 1]
        flat = [_pack([take(src, n, layer) for n, layer in packed], F32) for src in (w, mom1, mom2)]
        assert first + flat[0].shape[0] == part.shape[0], (first, flat[0].shape, part.shape)
        res = _adamw_sharded("adamw_packed_" + tag, part, sib, first, *flat)
        tail = [_unpack(o, [piece_shape(n, layer) for n, layer in packed]) for o in res]
        for i, e in enumerate(packed):
            out[e] = [tail[kind][i] for kind in range(4)]
        return out

    sent = {}

    def on_grads(stage, grads_so_far):
        gsend, own = buffers(groups[stage], grads_so_far)
        *handles, token = _exchange_xy_start("exchange_%s_start" % stage, gsend, own)
        sent[stage] = (own, handles)
        return token

    loss_part, grad_x, grads = _local_step(x, mem, loss_target, fw, late_weights, on_grads)

    gsend, own = buffers(groups["rest"], grads)
    out = update("rest", groups["rest"], own, _exchange_xy(gsend))
    for stage, (own, handles) in sent.items():
        out |= update(stage, groups[stage], own, _exchange_xy_wait("exchange_%s_wait" % stage, *handles, grad_x))
    sharded_out = [[] for _ in range(4)]
    for n in _SHARDED:
        for kind in range(4):
            if n in per_layer:
                sharded_out[kind].append(jnp.concatenate([out[(n, 0)][kind], out[(n, 1)][kind]], axis=0))
            else:
                sharded_out[kind].append(out[(n, None)][kind])

    small = _pack([grads[n] for n in _REPLICATED] + [loss_part.reshape(1)], F32, SUBLANES)
    parts = _all_gather_all(small)
    zero = jnp.zeros((1,), F32)
    flat = [_pack([src[n] for n in _REPLICATED] + [zero], F32, SUBLANES) for src in (w, mom1, mom2)]
    repl_out = [_unpack(o, [w[n].shape for n in _REPLICATED] + [(1,)]) for o in _adamw_replicated(parts, *flat)]
    loss = repl_out[0][-1][0]

    result = [loss, grad_x]
    for kind in range(4):
        by_name = dict(zip(_SHARDED, sharded_out[kind])) | dict(zip(_REPLICATED, repl_out[kind][:-1]))
        result += [by_name[n] for n in _WEIGHTS]
    return tuple(result)
```

```python
import functools

import jax
import jax.numpy as jnp
from jax import lax
from jax.experimental import pallas as pl
from jax.experimental.pallas import tpu as pltpu

F32 = jnp.float32
BF16 = jnp.bfloat16
MESH = pl.DeviceIdType.MESH

LANES = 128
SUBLANES = 8
VMEM_LIMIT_BYTES = 48 * 1024 * 1024

RMS_EPS = 1e-6
LRU_C = 8.0
LRU_HEADS = 4
CONV_WIDTH = 4
RWKV_HEAD = 64
RWKV_GN_EPS = 64e-5
MEM_HEADS = 4
ADAM_LR = 0.001
ADAM_B1 = 0.9
ADAM_B2 = 0.999
ADAM_EPS = 1e-08
ADAM_WD = 0.01
ADAM_STEP = 10
WKV_CHUNK = 64

_PARAMS = functools.partial(pltpu.CompilerParams, vmem_limit_bytes=VMEM_LIMIT_BYTES)


def _tile(n, want):
    if n <= want:
        return n
    t = want
    while t >= SUBLANES:
        if n % t == 0 and t % SUBLANES == 0:
            return t
        t -= SUBLANES
    return n


def _fold8(v):
    tm, d = v.shape
    if tm == SUBLANES:
        return v
    return jnp.sum(v.reshape(tm // SUBLANES, SUBLANES, d), axis=0)


def _rowwise(name, body, rows, consts=(), out_rows=(), out_accs=(), prev=(), nxt=(), tm=512, seq=None, n_rows=None):
    rows = [r if isinstance(r, tuple) else (r, r.shape[1], 0) for r in rows]
    rows = [r if len(r) == 4 else r + (0,) for r in rows]
    t = rows[0][0].shape[0] if n_rows is None else n_rows
    tm = _tile(t, tm)
    if seq is not None:
        tm = _tile(seq, tm)
    nblk = t // tm
    nrow, ncst, nprev, nnxt = len(rows), len(consts), len(prev), len(nxt)
    nor, noa = len(out_rows), len(out_accs)
    hb = tm // SUBLANES

    def kern(*refs):
        i = pl.program_id(0)
        rv = [r[...] for r in refs[:nrow]]
        cv = [c[...] for c in refs[nrow:nrow + ncst]]
        o = nrow + ncst
        pv = []
        for j in range(nprev):
            at_start = (i * tm) % seq == 0
            h = refs[o + j][...]
            pv.append(jnp.where(at_start, jnp.zeros_like(h), h))
        o += nprev
        nv = []
        for j in range(nnxt):
            at_end = ((i + 1) * tm) % seq == 0
            h = refs[o + j][...]
            nv.append(jnp.where(at_end, jnp.zeros_like(h), h))
        o += nnxt
        outs, accs = body(rv, cv, pv, nv)
        for j in range(nor):
            refs[o + j][...] = outs[j].astype(refs[o + j].dtype)
        o += nor
        if noa:
            @pl.when(i == 0)
            def _():
                for j in range(noa):
                    refs[o + j][...] = jnp.zeros_like(refs[o + j])
            for j in range(noa):
                refs[o + j][...] += _fold8(accs[j].astype(F32))

    assert all(first % tm == 0 for (_, _, _, first) in rows), name
    in_specs = [pl.BlockSpec((tm, w), functools.partial(lambda i, c, o: (i + o, c), c=cb, o=first // tm))
                for (_, w, cb, first) in rows]
    in_specs += [pl.BlockSpec(c.shape, lambda i: (0, 0)) for c in consts]
    in_specs += [pl.BlockSpec((SUBLANES, rows[j][1]),
                              functools.partial(lambda i, c: (jnp.maximum(i * hb - 1, 0), c), c=rows[j][2])) for j in prev]
    in_specs += [pl.BlockSpec((SUBLANES, rows[j][1]),
                              functools.partial(lambda i, c: (jnp.minimum((i + 1) * hb, t // SUBLANES - 1), c), c=rows[j][2]))
                 for j in nxt]
    out_shape = [jax.ShapeDtypeStruct((t, w), dt) for (w, dt) in out_rows]
    out_shape += [jax.ShapeDtypeStruct((SUBLANES, w), F32) for w in out_accs]
    out_specs = [pl.BlockSpec((tm, w), lambda i: (i, 0)) for (w, _) in out_rows]
    out_specs += [pl.BlockSpec((SUBLANES, w), lambda i: (0, 0)) for w in out_accs]
    args = [r[0] for r in rows] + list(consts) + [rows[j][0] for j in prev] + [rows[j][0] for j in nxt]
    res = pl.pallas_call(
        kern, name=name, grid=(nblk,), in_specs=in_specs, out_specs=out_specs, out_shape=out_shape,
        compiler_params=_PARAMS(dimension_semantics=("arbitrary",)),
    )(*args)
    return list(res[:nor]), list(res[nor:])


def _shift_down(x, halo, k):
    rolled = pltpu.roll(x, k, 0)
    row = lax.broadcasted_iota(jnp.int32, (SUBLANES, x.shape[1]), 0)
    first = jnp.where(row < k, pltpu.roll(halo, k, 0), rolled[:SUBLANES])
    if x.shape[0] == SUBLANES:
        return first
    return jnp.concatenate([first, rolled[SUBLANES:]], axis=0)


def _shift_up(x, halo, k):
    n = x.shape[0]
    rolled = pltpu.roll(x, n - k, 0)
    row = lax.broadcasted_iota(jnp.int32, (SUBLANES, x.shape[1]), 0)
    last = jnp.where(row >= SUBLANES - k, pltpu.roll(halo, SUBLANES - k, 0), rolled[n - SUBLANES:])
    if n == SUBLANES:
        return last
    return jnp.concatenate([rolled[:n - SUBLANES], last], axis=0)


class _Transposed:
    def __init__(self, w):
        self.w = w


def _t(w):
    return _Transposed(w)


def _mm(name, a, b, out_dtype=F32, trans_a=False, tm=1024, tn=1024, tk=1024, epilogue=None, extra=None):
    trans_b = isinstance(b, _Transposed)
    assert not (trans_a and trans_b)
    if trans_b:
        b = b.w
    if trans_a:
        kdim, m = a.shape
    else:
        m, kdim = a.shape
    n = b.shape[0] if trans_b else b.shape[1]
    assert b.shape[1 if trans_b else 0] == kdim, (name, a.shape, b.shape)
    tm, tn, tk = _tile(m, tm), _tile(n, tn), _tile(kdim, tk)
    nk = kdim // tk
    dims = (((0,), (0,)), ((), ())) if trans_a else (((1,), (1 if trans_b else 0,)), ((), ()))

    n_in = 2 if extra is None else 3

    def kern(*refs):
        a_ref, b_ref, o_ref, acc = refs[0], refs[1], refs[n_in], refs[n_in + 1:]

        def store(res):
            if epilogue is not None:
                res = epilogue(res) if extra is None else epilogue(res, refs[2][...])
            o_ref[...] = res.astype(o_ref.dtype)

        part = lax.dot_general(a_ref[...].astype(BF16), b_ref[...].astype(BF16), dims, preferred_element_type=F32)
        if nk == 1:
            store(part)
        else:
            k = pl.program_id(2)

            @pl.when(k == 0)
            def _():
                acc[0][...] = part

            @pl.when(k > 0)
            def _():
                acc[0][...] += part

            @pl.when(k == nk - 1)
            def _():
                store(acc[0][...])

    a_spec = pl.BlockSpec((tk, tm), lambda i, j, k: (k, i)) if trans_a else pl.BlockSpec((tm, tk), lambda i, j, k: (i, k))
    b_spec = pl.BlockSpec((tn, tk), lambda i, j, k: (j, k)) if trans_b else pl.BlockSpec((tk, tn), lambda i, j, k: (k, j))
    out_spec = pl.BlockSpec((tm, tn), lambda i, j, k: (i, j))
    return pl.pallas_call(
        kern, name=name, grid=(m // tm, n // tn, nk),
        in_specs=[a_spec, b_spec] + ([] if extra is None else [out_spec]),
        out_specs=out_spec,
        out_shape=jax.ShapeDtypeStruct((m, n), out_dtype),
        scratch_shapes=[] if nk == 1 else [pltpu.VMEM((tm, tn), F32)],
        compiler_params=_PARAMS(dimension_semantics=("parallel", "parallel", "arbitrary")),
    )(*((a, b) if extra is None else (a, b, extra)))


def _gates_mm(name, x, gate_w, mode, dz=None, tm=1024):
    _, nh, blk, _ = gate_w.shape
    d = nh * blk
    t = x.shape[0]
    tm = _tile(t, tm)
    cols = lambda g, h: slice(g * d + h * blk, g * d + (h + 1) * blk)

    def fwd(x_ref, w_ref, o_ref):
        for h in range(nh):
            xh = x_ref[:, cols(0, h)].astype(BF16)
            for g in range(2):
                o_ref[:, cols(g, h)] = jnp.dot(xh, w_ref[g, h].astype(BF16), preferred_element_type=F32)

    def dx(dz_ref, w_ref, o_ref):
        for h in range(nh):
            o_ref[:, cols(0, h)] = sum(lax.dot_general(dz_ref[:, cols(g, h)].astype(BF16), w_ref[g, h].astype(BF16), _NT,
                                                       preferred_element_type=F32) for g in range(2))

    def dw(x_ref, dz_ref, o_ref):
        @pl.when(pl.program_id(0) == 0)
        def _():
            o_ref[...] = jnp.zeros_like(o_ref)

        for h in range(nh):
            xh = x_ref[:, cols(0, h)].astype(BF16)
            for g in range(2):
                o_ref[g, h] += lax.dot_general(xh, dz_ref[:, cols(g, h)].astype(BF16), _TN, preferred_element_type=F32)

    row = lambda width: pl.BlockSpec((tm, width), lambda i: (i, 0))
    whole = pl.BlockSpec(gate_w.shape, lambda i: (0, 0, 0, 0))
    kern, args, in_specs, out_spec, out_shape, sem = {
        "fwd": (fwd, (x, gate_w), [row(d), whole], row(2 * d), jax.ShapeDtypeStruct((t, 2 * d), F32), "parallel"),
        "dx": (dx, (x, gate_w), [row(2 * d), whole], row(d), jax.ShapeDtypeStruct((t, d), F32), "parallel"),
        "dw": (dw, (x, dz), [row(d), row(2 * d)], whole, jax.ShapeDtypeStruct(gate_w.shape, F32), "arbitrary"),
    }[mode]
    return pl.pallas_call(kern, name=name, grid=(t // tm,), in_specs=in_specs, out_specs=out_spec, out_shape=out_shape,
                          compiler_params=_PARAMS(dimension_semantics=(sem,)))(*args)


def _scan(name, a, b, seq, reverse=False, tm=256):
    t, d = a.shape
    tm = _tile(seq, tm)
    nblk = t // tm
    ntile = tm // SUBLANES

    def kern(a_ref, b_ref, h_ref, carry_h, carry_a):
        i = pl.program_id(0)
        blk = (nblk - 1 - i) if reverse else i
        edge = (((blk + 1) * tm) % seq == 0) if reverse else ((blk * tm) % seq == 0)

        @pl.when(edge)
        def _():
            carry_h[...] = jnp.zeros_like(carry_h)
            carry_a[...] = jnp.zeros_like(carry_a)

        def tile_step(j, c):
            jj = (ntile - 1 - j) if reverse else j
            rows = pl.ds(pl.multiple_of(jj * SUBLANES, SUBLANES), SUBLANES)
            a8 = a_ref[rows, :]
            b8 = b_ref[rows, :]
            h, an = c
            out = [None] * SUBLANES
            order = range(SUBLANES - 1, -1, -1) if reverse else range(SUBLANES)
            for r in order:
                if reverse:
                    h = b8[r:r + 1, :] + an * h
                    an = a8[r:r + 1, :]
                else:
                    h = a8[r:r + 1, :] * h + b8[r:r + 1, :]
                out[r] = h
            h_ref[rows, :] = jnp.concatenate(out, axis=0)
            return (h, an)

        h, an = lax.fori_loop(0, ntile, tile_step, (carry_h[...], carry_a[...]))
        carry_h[...] = h
        carry_a[...] = an

    idx = (lambda i: (nblk - 1 - i, 0)) if reverse else (lambda i: (i, 0))
    return pl.pallas_call(
        kern, name=name, grid=(nblk,),
        in_specs=[pl.BlockSpec((tm, d), idx), pl.BlockSpec((tm, d), idx)],
        out_specs=pl.BlockSpec((tm, d), idx),
        out_shape=jax.ShapeDtypeStruct((t, d), F32),
        scratch_shapes=[pltpu.VMEM((1, d), F32), pltpu.VMEM((1, d), F32)],
        compiler_params=_PARAMS(dimension_semantics=("arbitrary",)),
    )(a, b)


_NN = (((1,), (0,)), ((), ()))
_NT = (((1,), (1,)), ((), ()))
_TN = (((0,), (0,)), ((), ()))


def _dot1(a, b, dims):
    return lax.dot_general(a.astype(BF16), b.astype(BF16), dims, preferred_element_type=F32)


def _dot3(a, b, dims):
    a_hi, b_hi = a.astype(BF16), b.astype(BF16)
    a_lo, b_lo = (a - a_hi.astype(F32)).astype(BF16), (b - b_hi.astype(F32)).astype(BF16)
    dg = lambda p, q: lax.dot_general(p, q, dims, preferred_element_type=F32)
    return dg(a_hi, b_hi) + (dg(a_hi, b_lo) + dg(a_lo, b_hi))


def _make_bmm(dot):
    def make(dims, da_rule, db_rule):
        @jax.custom_vjp
        def f(a, b):
            return dot(a, b, dims)

        def fwd(a, b):
            return dot(a, b, dims), (a, b)

        def bwd(res, g):
            a, b = res
            return da_rule(a, b, g), db_rule(a, b, g)

        f.defvjp(fwd, bwd)
        return f

    return dict(nn=make(_NN, lambda a, b, g: dot(g, b, _NT), lambda a, b, g: dot(a, g, _TN)),
                nt=make(_NT, lambda a, b, g: dot(g, b, _NN), lambda a, b, g: dot(g, a, _TN)),
                tn=make(_TN, lambda a, b, g: dot(b, g, _NT), lambda a, b, g: dot(a, g, _NN)))


_BMM = {1: _make_bmm(_dot1), 3: _make_bmm(_dot3)}
_WKV_PASSES = dict(pair=1, read=1, inv=3, apply=1, write=1)


def _running_sum(x, reverse):
    c = x.shape[0]
    row = lax.broadcasted_iota(jnp.int32, x.shape, 0)
    k = 1
    while k < c:
        if reverse:
            x = x + jnp.where(row < c - k, pltpu.roll(x, c - k, 0), 0.0)
        else:
            x = x + jnp.where(row >= k, pltpu.roll(x, k, 0), 0.0)
        k *= 2
    return x


@jax.custom_vjp
def _cumsum_rows(x):
    return _running_sum(x, False)


_cumsum_rows.defvjp(lambda x: (_running_sum(x, False), None), lambda _, g: (_running_sum(g, True),))


@jax.custom_vjp
def _unit_lower_inverse(nl):
    c = nl[0].shape[0]
    mm = _BMM[_WKV_PASSES["inv"]]["nn"]
    eye = jnp.where(lax.broadcasted_iota(jnp.int32, (c, c), 0) == lax.broadcasted_iota(jnp.int32, (c, c), 1), 1.0, 0.0)
    inv = [eye + z for z in nl]
    p = nl
    for _ in range(c.bit_length() - 2):
        p = [mm(z, z) for z in p]
        inv = [i_ + mm(p_, i_) for i_, p_ in zip(inv, p)]
    return inv


def _unit_lower_inverse_fwd(nl):
    inv = _unit_lower_inverse(nl)
    return inv, inv


def _unit_lower_inverse_bwd(inv, g):
    mm = _BMM[_WKV_PASSES["inv"]]
    left = [mm["tn"](x, g_) for x, g_ in zip(inv, g)]
    return ([mm["nt"](l_, x) for l_, x in zip(left, inv)],)


_unit_lower_inverse.defvjp(_unit_lower_inverse_fwd, _unit_lower_inverse_bwd)


@jax.custom_vjp
def _kept_inverse(nl, inv):
    return inv


_kept_inverse.defvjp(lambda nl, inv: (inv, inv),
                     lambda inv, g: (_unit_lower_inverse_bwd(inv, g)[0], [jnp.zeros_like(x) for x in inv]))


def _wkv_chunk(r, lw, k, v, a, b, s0, kept_inv=None):
    c = r[0].shape[0]
    ti = lax.broadcasted_iota(jnp.int32, (c, 2 * c), 0)
    tj = lax.broadcasted_iota(jnp.int32, (c, 2 * c), 1)
    right = tj >= c
    tau = jnp.where(right, tj - c, tj)
    strict_left = jnp.logical_and(jnp.logical_not(right), tau < ti)[:, :c]
    strict_right = jnp.logical_and(right, tau < ti)
    incl = tau <= ti
    last = lax.broadcasted_iota(jnp.int32, r[0].shape, 0) == c - 1
    each = lambda f, *ls: [f(*z) for z in zip(*ls)]
    rows2 = lambda x, y: jnp.concatenate([x, y], axis=0)
    pair, read, inv_, apply_, write = (_BMM[_WKV_PASSES[role]] for role in ("pair", "read", "inv", "apply", "write"))
    cum = each(_cumsum_rows, lw)
    w_incl = each(jnp.exp, cum)
    w_inv = each(lambda z: jnp.exp(-z), cum)
    at = each(lambda a_, c_, l_: a_ * jnp.exp(c_ - l_), a, cum, lw)
    ar = each(rows2, at, each(jnp.multiply, r, w_incl))
    bk = each(rows2, each(jnp.multiply, b, w_inv), each(jnp.multiply, k, w_inv))
    pp = each(pair["nt"], ar, bk)
    sr = each(read["nt"], ar, s0)
    nl = [jnp.where(strict_left, z[:c, :c], 0.0) for z in pp]
    zero_v = each(lambda v_: rows2(jnp.zeros_like(v_), v_), v)
    rhs = each(lambda s, z, zv: s[:c] + apply_["nn"](jnp.where(strict_right, z[:c], 0.0), zv), sr, pp, zero_v)
    inv = _unit_lower_inverse(nl) if kept_inv is None else _kept_inverse(nl, kept_inv)
    ut = each(inv_["nn"], inv, rhs)
    uv = each(rows2, ut, v)
    y = each(lambda s, z, uv_: s[c:] + apply_["nn"](jnp.where(incl, z[c:], 0.0), uv_), sr, pp, uv)
    w_end = each(lambda z: jnp.exp(jnp.sum(jnp.where(last, z, 0.0), axis=0, keepdims=True)), cum)
    s1 = each(lambda s, uv_, bk_, w_: (s + write["tn"](uv_, bk_)) * w_, s0, uv, bk, w_end)
    return y, s1, inv


def _wkv_fwd(r, lw, k, v, a, b, seq, hb=16):
    t, d = r.shape
    n = RWKV_HEAD
    nh = d // n
    hb = min(hb, nh)
    chunk = min(WKV_CHUNK, seq)
    ncs = seq // chunk

    def kern(r_ref, lw_ref, k_ref, v_ref, a_ref, b_ref, y_ref, st_ref, inv_ref, s_scr):
        @pl.when(pl.program_id(2) == 0)
        def _():
            s_scr[...] = jnp.zeros_like(s_scr)

        heads = lambda ref: [ref[:, h * n:(h + 1) * n] for h in range(hb)]
        s0 = [s_scr[h] for h in range(hb)]
        y, s1, inv = _wkv_chunk(heads(r_ref), heads(lw_ref), heads(k_ref), heads(v_ref), heads(a_ref), heads(b_ref), s0)
        for h in range(hb):
            st_ref[0, h] = s0[h]
            inv_ref[0, h] = inv[h]
            y_ref[:, h * n:(h + 1) * n] = y[h]
            s_scr[h] = s1[h]

    vec = pl.BlockSpec((chunk, hb * n), lambda bb, g, c: (bb * ncs + c, g))
    per_chunk = lambda rows: pl.BlockSpec((1, hb, rows, rows), lambda bb, g, c: (bb * ncs + c, g, 0, 0))
    return pl.pallas_call(
        kern, name="wkv_fwd", grid=(t // seq, nh // hb, ncs), in_specs=[vec] * 6,
        out_specs=[vec, per_chunk(n), per_chunk(chunk)],
        out_shape=[jax.ShapeDtypeStruct((t, d), F32), jax.ShapeDtypeStruct((t // chunk, nh, n, n), F32),
                   jax.ShapeDtypeStruct((t // chunk, nh, chunk, chunk), F32)],
        scratch_shapes=[pltpu.VMEM((hb, n, n), F32)],
        compiler_params=_PARAMS(dimension_semantics=("parallel", "parallel", "arbitrary")),
    )(r, lw, k, v, a, b)


def _wkv_bwd(r, lw, k, v, a, b, st, inv, dy, seq, hb=16):
    t, d = r.shape
    n = RWKV_HEAD
    nh = d // n
    hb = min(hb, nh)
    chunk = min(WKV_CHUNK, seq)
    ncs = seq // chunk

    def kern(r_ref, lw_ref, k_ref, v_ref, a_ref, b_ref, st_ref, inv_ref, dy_ref,
             dr_ref, dlw_ref, dk_ref, dv_ref, da_ref, db_ref, ds_scr):
        @pl.when(pl.program_id(2) == 0)
        def _():
            ds_scr[...] = jnp.zeros_like(ds_scr)

        heads = lambda ref: [ref[:, h * n:(h + 1) * n] for h in range(hb)]
        kept = [inv_ref[0, h] for h in range(hb)]
        _, vjp = jax.vjp(lambda *args: _wkv_chunk(*args, kept_inv=kept)[:2],
                         heads(r_ref), heads(lw_ref), heads(k_ref), heads(v_ref), heads(a_ref), heads(b_ref),
                         [st_ref[0, h] for h in range(hb)])
        grads = vjp((heads(dy_ref), [ds_scr[h] for h in range(hb)]))
        for h in range(hb):
            for ref, g in zip((dr_ref, dlw_ref, dk_ref, dv_ref, da_ref, db_ref), grads[:6]):
                ref[:, h * n:(h + 1) * n] = g[h]
            ds_scr[h] = grads[6][h]

    vec = pl.BlockSpec((chunk, hb * n), lambda bb, g, c: (bb * ncs + ncs - 1 - c, g))
    per_chunk = lambda rows: pl.BlockSpec((1, hb, rows, rows), lambda bb, g, c: (bb * ncs + ncs - 1 - c, g, 0, 0))
    return pl.pallas_call(
        kern, name="wkv_bwd", grid=(t // seq, nh // hb, ncs),
        in_specs=[vec] * 6 + [per_chunk(n), per_chunk(chunk), vec],
        out_specs=[vec] * 6, out_shape=[jax.ShapeDtypeStruct((t, d), F32)] * 6,
        scratch_shapes=[pltpu.VMEM((hb, n, n), F32)],
        compiler_params=_PARAMS(dimension_semantics=("parallel", "parallel", "arbitrary")),
    )(r, lw, k, v, a, b, st, inv, dy)


def _softmax_rows(s):
    e = jnp.exp(s - jnp.max(s, axis=-1, keepdims=True))
    return e / jnp.sum(e, axis=-1, keepdims=True)


def _attn_fwd(q, kv, seq, mem_len, tq=1024):
    t, d = q.shape
    dh = d // MEM_HEADS
    scale = dh ** -0.5
    tq = _tile(seq, tq)
    nq = seq // tq

    def kern(q_ref, kv_ref, o_ref):
        for h in range(MEM_HEADS):
            cols = slice(h * dh, (h + 1) * dh)
            vcols = slice(d + h * dh, d + (h + 1) * dh)
            s = lax.dot_general(q_ref[:, cols], kv_ref[:, cols], _NT, preferred_element_type=F32) * scale
            p = _softmax_rows(s)
            o_ref[:, cols] = jnp.dot(p.astype(BF16), kv_ref[:, vcols], preferred_element_type=F32).astype(o_ref.dtype)

    return pl.pallas_call(
        kern, name="attn_fwd", grid=(t // seq, nq),
        in_specs=[pl.BlockSpec((tq, d), lambda b, i: (b * nq + i, 0)), pl.BlockSpec((mem_len, 2 * d), lambda b, i: (b, 0))],
        out_specs=pl.BlockSpec((tq, d), lambda b, i: (b * nq + i, 0)),
        out_shape=jax.ShapeDtypeStruct((t, d), BF16),
        compiler_params=_PARAMS(dimension_semantics=("parallel", "parallel")),
    )(q, kv)


def _attn_bwd(q, kv, do, seq, mem_len, tq=1024):
    t, d = q.shape
    dh = d // MEM_HEADS
    scale = dh ** -0.5
    tq = _tile(seq, tq)
    nq = seq // tq

    def kern(q_ref, kv_ref, do_ref, dq_ref, dkv_ref):
        @pl.when(pl.program_id(1) == 0)
        def _():
            dkv_ref[...] = jnp.zeros_like(dkv_ref)

        for h in range(MEM_HEADS):
            cols = slice(h * dh, (h + 1) * dh)
            vcols = slice(d + h * dh, d + (h + 1) * dh)
            qh, kh, vh, doh = q_ref[:, cols], kv_ref[:, cols], kv_ref[:, vcols], do_ref[:, cols]
            p = _softmax_rows(lax.dot_general(qh, kh, _NT, preferred_element_type=F32) * scale)
            dp = lax.dot_general(doh, vh, _NT, preferred_element_type=F32)
            ds = (p * (dp - jnp.sum(p * dp, axis=-1, keepdims=True)) * scale).astype(BF16)
            dq_ref[:, cols] = jnp.dot(ds, kh, preferred_element_type=F32).astype(dq_ref.dtype)
            dkv_ref[:, cols] += lax.dot_general(ds, qh, _TN, preferred_element_type=F32)
            dkv_ref[:, vcols] += lax.dot_general(p.astype(BF16), doh, _TN, preferred_element_type=F32)

    return pl.pallas_call(
        kern, name="attn_bwd", grid=(t // seq, nq),
        in_specs=[pl.BlockSpec((tq, d), lambda b, i: (b * nq + i, 0)), pl.BlockSpec((mem_len, 2 * d), lambda b, i: (b, 0)),
                  pl.BlockSpec((tq, d), lambda b, i: (b * nq + i, 0))],
        out_specs=[pl.BlockSpec((tq, d), lambda b, i: (b * nq + i, 0)), pl.BlockSpec((mem_len, 2 * d), lambda b, i: (b, 0))],
        out_shape=[jax.ShapeDtypeStruct((t, d), BF16), jax.ShapeDtypeStruct(kv.shape, F32)],
        compiler_params=_PARAMS(dimension_semantics=("parallel", "arbitrary")),
    )(q, kv, do)


def _rstd(x):
    return lax.rsqrt(jnp.mean(x * x, axis=-1, keepdims=True) + RMS_EPS)


def _rms(x, g):
    return x * _rstd(x) * g


def _rms_bwd(dy, x, g):
    rstd = _rstd(x)
    xhat = x * rstd
    dxhat = dy * g
    return rstd * (dxhat - xhat * jnp.mean(dxhat * xhat, axis=-1, keepdims=True)), dy * xhat


def _softplus(x):
    return jnp.maximum(x, 0.0) + jnp.log1p(jnp.exp(-jnp.abs(x)))


def _one_minus_exp(x):
    series = -x * (1.0 + x * (0.5 + x * (1.0 / 6.0 + x * (1.0 / 24.0 + x * (1.0 / 120.0)))))
    return jnp.where(x > -0.05, series, 1.0 - jnp.exp(x))


_GELU_C = 0.7978845608028654
_GELU_K = 0.044715


def _gelu(x):
    return 0.5 * x * (1.0 + jnp.tanh(_GELU_C * (x + _GELU_K * x * x * x)))


def _gelu_grad(x):
    th = jnp.tanh(_GELU_C * (x + _GELU_K * x * x * x))
    return 0.5 * (1.0 + th) + 0.5 * x * (1.0 - th * th) * _GELU_C * (1.0 + 3.0 * _GELU_K * x * x)


def _seg_sum(x, seg):
    mm = lambda v, dims: lax.dot_general(v.astype(BF16), seg, dims, preferred_element_type=F32)
    return mm(mm(x, _NN), _NT)


def _f32(v):
    return v.astype(F32)


def _norm_fwd(name, x, g, dtype=BF16):
    (hn,), _ = _rowwise(name, lambda rv, cv, pv, nv: ([_rms(_f32(rv[0]), cv[0])], []), [x], [g], [(x.shape[1], dtype)])
    return hn


def _resid_norm_fwd(name, x, t, g, bias=None, chain=None):
    d = x.shape[1]
    chain = {} if chain is None else chain

    def body(rv, cv, pv, nv):
        tt = rv[1] if bias is None else rv[1] + cv[1]
        y = rv[0] + _rms(tt, cv[0])
        if "gain" in chain:
            return [y, _rms(y, cv[-1])], []
        if "target" in chain:
            err = y - rv[2]
            return [err * (1.0 / d)], [err * err]
        return [y], []
    consts = [g] if bias is None else [g, bias]
    if "gain" in chain:
        (y, chain["hn"]), _ = _rowwise(name, body, [x, t], consts + [chain["gain"]], [(d, F32), (d, BF16)])
        return y
    if "target" in chain:
        (chain["dy"],), (chain["sq"],) = _rowwise(name, body, [x, t, chain["target"]], consts, [(d, F32)], [d])
        return None
    (y,), _ = _rowwise(name, body, [x, t], consts, [(d, F32)])
    return y


def _resid_norm_bwd(name, dxn, t, g, bias=None, done=None):
    if done is not None:
        return done
    return _resid_norm_bwd_kernel(name, dxn, t, g, bias)


def _resid_norm_bwd_kernel(name, dxn, t, g, bias=None):
    def body(rv, cv, pv, nv):
        tt = rv[1] if bias is None else rv[1] + cv[1]
        dt, dg = _rms_bwd(rv[0], tt, cv[0])
        return [dt], [dg, dt]
    d = t.shape[1]
    (dt,), (dg, db) = _rowwise(name, body, [dxn, t], [g] if bias is None else [g, bias], [(d, BF16)], [d, d])
    return dt, dg.sum(0), db.sum(0)


def _prenorm_bwd(name, dxn, dhn, x, g, below=None):
    d = x.shape[1]

    def body(rv, cv, pv, nv):
        dx, dg = _rms_bwd(_f32(rv[1]), rv[2], cv[0])
        dx = rv[0] + dx
        if below is None:
            return [dx], [dg]
        tt = rv[3] if below["bias"] is None else rv[3] + cv[2]
        dt, dg_below = _rms_bwd(dx, tt, cv[1])
        return [dx, dt], [dg, dg_below, dt]
    if below is None:
        (dx,), (dg,) = _rowwise(name, body, [dxn, dhn, x], [g], [(d, F32)], [d])
        return dx, dg.sum(0)
    consts = [g, below["g"]] + ([] if below["bias"] is None else [below["bias"]])
    (dx, dt), (dg, dg_below, db) = _rowwise(name, body, [dxn, dhn, x, below["t"]], consts, [(d, F32), (d, BF16)], [d, d, d], tm=256)
    below["done"] = (dt, dg_below.sum(0), db.sum(0))
    return dx, dg.sum(0)


def _mlp_fwd(tag, x, g_pre, g_post, w_up, w_down, hn=None, chain=None):
    hn = _norm_fwd(tag + "_norm", x, g_pre) if hn is None else hn
    act = _mm(tag + "_up", hn, w_up, out_dtype=BF16, epilogue=lambda up: jnp.square(jnp.maximum(up, 0.0)))
    m = _mm(tag + "_down", act, w_down)
    y = _resid_norm_fwd(tag + "_res", x, m, g_post, chain=chain)
    return y, (x, hn, act, m)


def _mlp_bwd(tag, saved, dy, g_pre, g_post, w_up_t, w_down_t, done=None, below=None):
    x, hn, act, m = saved
    dm, dg_post, _ = _resid_norm_bwd(tag + "_dres", dy, m, g_post, done=done)
    dup = _mm(tag + "_dup", dm, w_down_t, out_dtype=BF16, extra=act,
              epilogue=lambda dact, act_: dact * 2.0 * jnp.sqrt(_f32(act_)))
    dw_down = _mm(tag + "_dwdown", act, dm, trans_a=True)
    dw_up = _mm(tag + "_dwup", hn, dup, trans_a=True)
    dhn = _mm(tag + "_dhn", dup, w_up_t)
    dx, dg_pre = _prenorm_bwd(tag + "_dnorm", dy, dhn, x, g_pre, below=below)
    return dx, dict(g_pre=dg_pre, g_post=dg_post, w_up=dw_up, w_down=dw_down)


def _xattn_fwd(tag, x, mem_n, g_pre, g_post, w_q, w_kv, w_o, seq, mem_len, hn=None, chain=None):
    hn = _norm_fwd(tag + "_norm", x, g_pre) if hn is None else hn
    q = _mm(tag + "_q", hn, w_q, out_dtype=BF16)
    kv = _mm(tag + "_kv", mem_n, w_kv, out_dtype=BF16)
    o = _attn_fwd(q, kv, seq, mem_len)
    c = _mm(tag + "_o", o, w_o)
    y = _resid_norm_fwd(tag + "_res", x, c, g_post, chain=chain)
    return y, (x, hn, q, kv, o, c)


def _xattn_bwd(tag, saved, dy, mem_n, g_pre, g_post, w_q_t, w_kv_t, w_o_t, seq, mem_len, done=None, below=None):
    x, hn, q, kv, o, c = saved
    dc, dg_post, _ = _resid_norm_bwd(tag + "_dres", dy, c, g_post, done=done)
    do = _mm(tag + "_do", dc, w_o_t, out_dtype=BF16)
    dw_o = _mm(tag + "_dwo", o, dc, trans_a=True)
    dq, dkv = _attn_bwd(q, kv, do, seq, mem_len)
    dw_q = _mm(tag + "_dwq", hn, dq, trans_a=True)
    dhn = _mm(tag + "_dhn", dq, w_q_t)
    dw_kv = _mm(tag + "_dwkv", mem_n, dkv, trans_a=True)
    dmem_n = _mm(tag + "_dmem", dkv, w_kv_t)
    dx, dg_pre = _prenorm_bwd(tag + "_dnorm", dy, dhn, x, g_pre, below=below)
    return dx, dmem_n, dict(g_pre=dg_pre, g_post=dg_post, w_q=dw_q, w_kv=dw_kv, w_o=dw_o)


def _lru_gates(z0, z1, gb0, gb1, sp):
    r = jax.nn.sigmoid(z0 + gb0)
    i = jax.nn.sigmoid(z1 + gb1)
    log_a = -LRU_C * r * sp
    a = jnp.exp(log_a)
    mult = jnp.sqrt(_one_minus_exp(2.0 * log_a))
    return r, i, a, mult


def _rglru_fwd(x, p, seq, chain=None):
    d = x.shape[1]
    hn = _norm_fwd("a_norm", x, p["g_pre"])
    proj = _mm("a_in", hn, p["w_in"])

    def conv_body(rv, cv, pv, nv):
        u = rv[0] + cv[0]
        halo = pv[0] + cv[0]
        i = pl.program_id(0)
        halo = jnp.where((i * rv[0].shape[0]) % seq == 0, jnp.zeros_like(halo), halo)
        conv = cv[2] + u * cv[1][CONV_WIDTH - 1:CONV_WIDTH]
        for tap in range(CONV_WIDTH - 1):
            conv = conv + _shift_down(u, halo, CONV_WIDTH - 1 - tap) * cv[1][tap:tap + 1]
        return [conv], []
    (conv,), _ = _rowwise("a_conv", conv_body, [(proj, d, 1)], [p["b_in_u"], p["conv_w"], p["conv_b"]], [(d, F32)],
                          prev=[0], seq=seq)
    z = _gates_mm("a_gate", conv, p["w_gate"], "fwd")

    def gate_body(rv, cv, pv, nv):
        r, i, a, mult = _lru_gates(rv[0], rv[1], cv[0], cv[1], _softplus(-cv[2]))
        return [a, mult * i * rv[2]], []
    (a, bb), _ = _rowwise("a_gates", gate_body, [(z, d, 0), (z, d, 1), conv], [p["gate_b0"], p["gate_b1"], p["lam"]],
                          [(d, F32), (d, F32)])
    h = _scan("a_scan", a, bb, seq)

    def hy_body(rv, cv, pv, nv):
        return [rv[0] * _gelu(rv[1] + cv[0])], []
    (hy,), _ = _rowwise("a_hy", hy_body, [h, (proj, d, 0)], [p["b_in_y"]], [(d, BF16)])
    out = _mm("a_out", hy, p["w_out"])
    y = _resid_norm_fwd("a_res", x, out, p["g_post"], bias=p["b_out"], chain=chain)
    return y, (x, hn, proj, conv, z, a, h, hy, out)


def _rglru_bwd(saved, dy, p, pt, seq, done=None):
    x, hn, proj, conv, z, a, h, hy, out = saved
    d = x.shape[1]
    dt, dg_post, db_out = _resid_norm_bwd("a_dres", dy, out, p["g_post"], bias=p["b_out"], done=done)
    dhy = _mm("a_dhy", dt, pt["w_out_t"])
    dw_out = _mm("a_dwout", hy, dt, trans_a=True)

    def dh_body(rv, cv, pv, nv):
        yb = rv[2] + cv[0]
        return [rv[0] * _gelu(yb), rv[0] * rv[1] * _gelu_grad(yb)], []
    (dh, dyb), _ = _rowwise("a_dh", dh_body, [dhy, h, (proj, d, 0)], [p["b_in_y"]], [(d, F32), (d, BF16)])
    g = _scan("a_rscan", a, dh, seq, reverse=True)

    def dgate_body(rv, cv, pv, nv):
        gg, hh, z0, z1, cnv = rv
        sp = _softplus(-cv[2])
        r, i, aa, mult = _lru_gates(z0, z1, cv[0], cv[1], sp)
        i_blk = pl.program_id(0)
        halo = jnp.where((i_blk * gg.shape[0]) % seq == 0, jnp.zeros_like(pv[0]), pv[0])
        da = gg * _shift_down(hh, halo, 1)
        dmult = gg * i * cnv
        di = gg * mult * cnv
        dconv = gg * mult * i
        dlog_a = da * aa - dmult * aa * aa / mult
        dz0 = dlog_a * (-LRU_C * sp) * r * (1.0 - r)
        dz1 = di * i * (1.0 - i)
        dsp = dlog_a * (-LRU_C * r)
        dlam = dsp * (-jax.nn.sigmoid(-cv[2]))
        return [jnp.concatenate([dz0, dz1], axis=1), dconv], [dz0, dz1, dlam]
    (dz, dconv1), (dgb0, dgb1, dlam) = _rowwise(
        "a_dgates", dgate_body, [g, h, (z, d, 0), (z, d, 1), conv], [p["gate_b0"], p["gate_b1"], p["lam"]],
        [(2 * d, BF16), (d, F32)], [d, d, d], prev=[1], seq=seq)
    dconv2 = _gates_mm("a_dconv", dz, p["w_gate"], "dx")
    dw_gate = _gates_mm("a_dwgate", conv, p["w_gate"], "dw", dz=dz)

    def dconv_body(rv, cv, pv, nv):
        dc1, dc2, pu, dyb_ = rv
        dc = dc1 + dc2
        dc_next = nv[0] + nv[1]
        u = pu + cv[0]
        i_blk = pl.program_id(0)
        halo = jnp.where((i_blk * u.shape[0]) % seq == 0, jnp.zeros_like(pv[0]), pv[0] + cv[0])
        du = dc * cv[1][CONV_WIDTH - 1:CONV_WIDTH]
        dws = []
        for tap in range(CONV_WIDTH - 1):
            k = CONV_WIDTH - 1 - tap
            du = du + _shift_up(dc, dc_next, k) * cv[1][tap:tap + 1]
            dws.append(dc * _shift_down(u, halo, k))
        dws.append(dc * u)
        return [jnp.concatenate([_f32(dyb_), du], axis=1)], dws + [dc, _f32(dyb_), du]
    (dproj,), accs = _rowwise(
        "a_dconvw", dconv_body, [dconv1, dconv2, (proj, d, 1), dyb], [p["b_in_u"], p["conv_w"]],
        [(2 * d, BF16)], [d] * (CONV_WIDTH + 3), prev=[2], nxt=[0, 1], seq=seq)
    dconv_w = jnp.stack([acc.sum(0) for acc in accs[:CONV_WIDTH]])
    dconv_b = accs[CONV_WIDTH].sum(0)
    db_in = jnp.concatenate([accs[CONV_WIDTH + 1].sum(0), accs[CONV_WIDTH + 2].sum(0)])
    dhn = _mm("a_dhn", dproj, pt["w_in_t"])
    dw_in = _mm("a_dwin", hn, dproj, trans_a=True)
    dx, dg_pre = _prenorm_bwd("a_dnorm", dy, dhn, x, p["g_pre"])
    grads = dict(g_pre=dg_pre, g_post=dg_post, b_out=db_out, w_out=dw_out, gate_b0=dgb0.sum(0), gate_b1=dgb1.sum(0),
                 lam=dlam.sum(0), w_gate=dw_gate, conv_w=dconv_w, conv_b=dconv_b, b_in=db_in, w_in=dw_in)
    return dx, grads


def _rwkv_prep(k, wl, za, w0, a0, k_k, k_a, seg):
    w_in = wl + w0
    e_w = jnp.exp(-_softplus(-w_in) - 0.5)
    a = jax.nn.sigmoid(za + a0)
    q = k * k_k
    norm = jnp.sqrt(_seg_sum(q * q, seg))
    n = jnp.maximum(norm, 1e-12)
    kk = q / n
    return w_in, e_w, a, norm, n, kk


def _rwkv_out(y, r, k2, v, gn_g, gn_b, r_k, seg):
    inv = 1.0 / RWKV_HEAD
    yc = y - _seg_sum(y, seg) * inv
    rstd = lax.rsqrt(_seg_sum(yc * yc, seg) * inv + RWKV_GN_EPS)
    yhat = yc * rstd
    s = _seg_sum(r * k2 * r_k, seg)
    return rstd, yhat, s, yhat * gn_g + gn_b + s * v


def _rwkv_fwd(x, p, seq, chain=None):
    t, d = x.shape
    nseq = t // seq

    def mix_body(rv, cv, pv, nv):
        hn = _rms(rv[0], cv[0])
        xx = _shift_down(hn, _rms(pv[0], cv[0]), 1) - hn
        return [hn] + [hn + xx * cv[1][c:c + 1] for c in range(6)], []
    (hn, xr, xw, xk, xv, xa, xg), _ = _rowwise("b_mix", mix_body, [x], [p["g_pre"], p["mu"]],
                                               [(d, F32)] + [(d, BF16)] * 6, prev=[0], seq=seq)
    r = _mm("b_r", xr, p["w_r"])
    k = _mm("b_k", xk, p["w_k"])
    v = _mm("b_v", xv, p["w_v"])
    lw = _mm("b_w1", xw, p["w1"])
    la = _mm("b_a1", xa, p["a1"], out_dtype=BF16)
    lg = _mm("b_g1", xg, p["g1"])
    (th,), _ = _rowwise("b_tanh", lambda rv, cv, pv, nv: ([jnp.tanh(rv[0])], []), [lw], [], [(lw.shape[1], BF16)])
    (sg,), _ = _rowwise("b_sig", lambda rv, cv, pv, nv: ([jax.nn.sigmoid(rv[0])], []), [lg], [], [(lg.shape[1], BF16)])
    wl = _mm("b_w2", th, p["w2"])
    za = _mm("b_a2", la, p["a2"])
    g = _mm("b_g2", sg, p["g2"])

    def prep_body(rv, cv, pv, nv):
        kk_, wl_, za_ = rv
        _, e_w, a, _, _, kk = _rwkv_prep(kk_, wl_, za_, cv[0], cv[1], cv[2], cv[3], cv[4])
        return [-e_w, kk_ * (1.0 + (a - 1.0) * cv[3]), -kk, kk * a], []
    (log_w, k2, rem_a, rem_b), _ = _rowwise("b_prep", prep_body, [k, wl, za],
                                            [p["w0"], p["a0"], p["k_k"], p["k_a"], p["seg"]], [(d, F32)] * 4, tm=256)
    rec_in = (r, log_w, k2, v, rem_a, rem_b)
    y, *states = _wkv_fwd(*rec_in, seq)

    def out_body(rv, cv, pv, nv):
        y_, r_, k2_, v_, g_ = rv
        _, _, _, out = _rwkv_out(y_, r_, k2_, v_, cv[0], cv[1], cv[2], cv[3])
        return [out * g_], []
    (og,), _ = _rowwise("b_out", out_body, [y, r, k2, v, g], [p["gn_g"], p["gn_b"], p["r_k"], p["seg"]], [(d, BF16)], tm=256)
    o = _mm("b_o", og, p["w_o"])
    res = _resid_norm_fwd("b_res", x, o, p["g_post"], chain=chain)
    return res, (x, hn, xr, xw, xk, xv, xa, xg, r, k, v, th, la, sg, wl, za, g, k2, rec_in, states, y, og, o)


def _rwkv_bwd(saved, dres, p, pt, seq, done=None, below=None):
    x, hn, xr, xw, xk, xv, xa, xg, r, k, v, th, la, sg, wl, za, g, k2, rec_in, states, y, og, o = saved
    t, d = x.shape
    nseq = t // seq
    do, dg_post, _ = _resid_norm_bwd("b_dres", dres, o, p["g_post"], done=done)
    dog = _mm("b_dog", do, pt["w_o_t"])
    dw_o = _mm("b_dwo", og, do, trans_a=True)

    def dout_body(rv, cv, pv, nv):
        dog_, y_, r_, k2_, v_, g_ = rv
        gn_g, gn_b, r_k, bd = cv
        inv = 1.0 / RWKV_HEAD
        rstd, yhat, s, out = _rwkv_out(y_, r_, k2_, v_, gn_g, gn_b, r_k, bd)
        dout = dog_ * g_
        ds = _seg_sum(dout * v_, bd)
        dyhat = dout * gn_g
        dy = rstd * (dyhat - _seg_sum(dyhat, bd) * inv - yhat * _seg_sum(dyhat * yhat, bd) * inv)
        return [dy, dog_ * out, dout * s, ds * k2_ * r_k, ds * r_ * r_k], [ds * r_ * k2_, dout * yhat, dout]
    (dy, dgate, dv_b, dr_b, dk2_b), (dr_k, dgn_g, dgn_b) = _rowwise(
        "b_dout", dout_body, [dog, y, r, k2, v, g], [p["gn_g"], p["gn_b"], p["r_k"], p["seg"]],
        [(d, F32), (d, BF16), (d, BF16), (d, BF16), (d, BF16)], [d, d, d], tm=256)
    dr_rec, dlw_rec, dk2_rec, dv_rec, da_rec, db_rec = _wkv_bwd(*rec_in, *states, dy, seq)

    def dprep_body(rv, cv, pv, nv):
        dr_rec_, dlw_rec_, dk2_rec_, dv_rec_, da_rec_, db_rec_, dr_b_, dk2_b_, dv_b_, k_, wl_, za_ = rv
        w0, a0, k_k, k_a, bd = cv
        w_in, e_w, a, norm, n, kk = _rwkv_prep(k_, wl_, za_, w0, a0, k_k, k_a, bd)
        dk2 = dk2_rec_ + dk2_b_
        dkk = db_rec_ * a - da_rec_
        da = db_rec_ * kk + dk2 * k_ * k_a
        dq = jnp.where(norm > 1e-12, dkk - kk * _seg_sum(kk * dkk, bd), dkk) / n
        dk = dk2 * (1.0 + (a - 1.0) * k_a) + dq * k_k
        dza = da * a * (1.0 - a)
        dwl = dlw_rec_ * (-e_w) * jax.nn.sigmoid(-w_in)
        return [dr_rec_ + dr_b_, dk, dv_rec_ + dv_b_, dza, dwl], [dk2 * k_ * (a - 1.0), dq * k_, dza, dwl]
    (dr, dk, dv, dza, dwl), (dk_a, dk_k, da0, dw0) = _rowwise(
        "b_dprep", dprep_body, [dr_rec, dlw_rec, dk2_rec, dv_rec, da_rec, db_rec, dr_b, dk2_b, dv_b, k, wl, za],
        [p["w0"], p["a0"], p["k_k"], p["k_a"], p["seg"]], [(d, BF16)] * 5, [d] * 4, tm=256)

    dw_r = _mm("b_dwr", xr, dr, trans_a=True)
    dw_k = _mm("b_dwk", xk, dk, trans_a=True)
    dw_v = _mm("b_dwv", xv, dv, trans_a=True)
    dxr = _mm("b_dxr", dr, pt["w_r_t"])
    dxk = _mm("b_dxk", dk, pt["w_k_t"])
    dxv = _mm("b_dxv", dv, pt["w_v_t"])
    da2 = _mm("b_da2", la, dza, trans_a=True)
    dla = _mm("b_dla", dza, pt["a2_t"], out_dtype=BF16)
    da1 = _mm("b_da1", xa, dla, trans_a=True)
    dxa = _mm("b_dxa", dla, pt["a1_t"])
    dw2 = _mm("b_dw2", th, dwl, trans_a=True)
    dth = _mm("b_dth", dwl, pt["w2_t"])
    (dzw,), _ = _rowwise("b_dtanh", lambda rv, cv, pv, nv: ([rv[0] * (1.0 - _f32(rv[1]) * _f32(rv[1]))], []),
                         [dth, th], [], [(th.shape[1], BF16)])
    dw1 = _mm("b_dw1", xw, dzw, trans_a=True)
    dxw = _mm("b_dxw", dzw, pt["w1_t"])
    dg2 = _mm("b_dg2", sg, dgate, trans_a=True)
    dsg = _mm("b_dsg", dgate, pt["g2_t"])
    (dzg,), _ = _rowwise("b_dsig", lambda rv, cv, pv, nv: ([rv[0] * _f32(rv[1]) * (1.0 - _f32(rv[1]))], []),
                         [dsg, sg], [], [(sg.shape[1], BF16)])
    dg1 = _mm("b_dg1", xg, dzg, trans_a=True)
    dxg = _mm("b_dxg", dzg, pt["g1_t"])

    def dmix_body(rv, cv, pv, nv):
        hn_ = rv[0]
        dxs = rv[1:]
        mu = cv[0]
        xx = _shift_down(hn_, pv[0], 1) - hn_
        dsum = dxs[0]
        dxx = dxs[0] * mu[0:1]
        dxx_next = nv[0] * mu[0:1]
        for c in range(1, 6):
            dsum = dsum + dxs[c]
            dxx = dxx + dxs[c] * mu[c:c + 1]
            dxx_next = dxx_next + nv[c] * mu[c:c + 1]
        return [dsum - dxx + _shift_up(dxx, dxx_next, 1)], [dxs[c] * xx for c in range(6)]
    (dhn,), dmu = _rowwise("b_dmix", dmix_body, [hn, dxr, dxw, dxk, dxv, dxa, dxg], [p["mu"]], [(d, F32)], [d] * 6,
                           prev=[0], nxt=[1, 2, 3, 4, 5, 6], seq=seq, tm=256)
    dx, dg_pre = _prenorm_bwd("b_dnorm", dres, dhn, x, p["g_pre"], below=below)
    grads = dict(g_pre=dg_pre, g_post=dg_post, mu=jnp.stack([m.sum(0) for m in dmu]), w_r=dw_r, w_k=dw_k, w_v=dw_v,
                 w0=dw0.sum(0), w1=dw1, w2=dw2, a0=da0.sum(0), a1=da1, a2=da2, g1=dg1, g2=dg2, k_k=dk_k.sum(0),
                 k_a=dk_a.sum(0), r_k=dr_k.sum(0), gn_g=dgn_g.sum(0), gn_b=dgn_b.sum(0), w_o=dw_o)
    return dx, grads


_WEIGHTS = ['ln_gains', 'mem_norm', 'a_conv_w', 'a_conv_b', 'a_w_in', 'a_b_in', 'a_gate_w', 'a_gate_b', 'a_lambda', 'a_w_out',
            'a_b_out', 'b_mu', 'b_w_rkv', 'b_w0', 'b_w1', 'b_w2', 'b_a0', 'b_a1', 'b_a2', 'b_g1', 'b_g2', 'b_k_k', 'b_k_a',
            'b_r_k', 'b_gn_g', 'b_gn_b', 'b_w_o', 'c_w_q', 'c_w_kv', 'c_w_o', 'm_w_up', 'm_w_down']
_SHARD_AXIS = dict(ln_gains=2, mem_norm=None, a_conv_w=2, a_conv_b=None, a_w_in=2, a_b_in=None, a_gate_w=3, a_gate_b=3,
                   a_lambda=None, a_w_out=1, a_b_out=None, b_mu=2, b_w_rkv=2, b_w0=1, b_w1=1, b_w2=2, b_a0=1, b_a1=1, b_a2=2,
                   b_g1=1, b_g2=2, b_k_k=1, b_k_a=1, b_r_k=None, b_gn_g=1, b_gn_b=1, b_w_o=1, c_w_q=1, c_w_kv=2, c_w_o=1,
                   m_w_up=2, m_w_down=1)
_MATRICES = ['a_w_in', 'a_gate_w', 'a_w_out', 'b_w_rkv', 'b_w1', 'b_w2', 'b_a1', 'b_a2', 'b_g1', 'b_g2', 'b_w_o', 'c_w_q',
             'c_w_kv', 'c_w_o', 'm_w_up', 'm_w_down']
_SHARDED = [n for n in _WEIGHTS if _SHARD_AXIS[n] is not None]
_VECTORS = [n for n in _SHARDED if n not in _MATRICES]
_REPLICATED = [n for n in _WEIGHTS if _SHARD_AXIS[n] is None]
N_XY = 4
N_DEV = 8
PACK_W = 1024
PACK_ROWS = 256


def _pack(arrs, dtype, row_mult=PACK_ROWS):
    parts = []
    rows = 0
    for a in arrs:
        n = a.size
        r = -(-n // PACK_W)
        parts.append(jnp.pad(a.reshape(-1).astype(dtype), (0, r * PACK_W - n)))
        rows += r
    pad_rows = -(-rows // row_mult) * row_mult - rows
    if pad_rows:
        parts.append(jnp.zeros((pad_rows * PACK_W,), dtype))
    return jnp.concatenate(parts).reshape(-1, PACK_W)


def _unpack(flat, shapes):
    out = []
    row = 0
    for shp in shapes:
        n = 1
        for s in shp:
            n *= s
        r = -(-n // PACK_W)
        out.append(flat[row:row + r].reshape(-1)[:n].reshape(shp))
        row += r
    return out


_ANY = pl.BlockSpec(memory_space=pl.ANY)


def _xy_peers():
    x, y = lax.axis_index("x"), lax.axis_index("y")
    return [(1 - x, y), (x, 1 - y), (1 - x, 1 - y)]


def _all_gather_xy(wm, wv):
    half = wm.shape[0] // 2

    def body(wm_ref, wv_ref, gm_ref, gv_ref, send_sems, recv_sems, local_sems):
        x, y, c = lax.axis_index("x"), lax.axis_index("y"), lax.axis_index("c")
        me = 2 * x + y
        mine = pl.ds(pl.multiple_of(c * half, SUBLANES), half)
        other = pl.ds(pl.multiple_of((1 - c) * half, SUBLANES), half)
        local = [pltpu.make_async_copy(wm_ref, gm_ref.at[me], local_sems.at[0]),
                 pltpu.make_async_copy(wv_ref, gv_ref.at[me], local_sems.at[1])]
        for cp in local:
            cp.start()
        sends, lands, passes, from_sibling = [], [], [], []
        for j, (px, py) in enumerate(_xy_peers()):
            peer = 2 * px + py
            ici = functools.partial(pltpu.make_async_remote_copy, device_id=(px, py, c), device_id_type=MESH)
            sends.append(ici(src_ref=wm_ref.at[mine], dst_ref=gm_ref.at[me, mine], send_sem=send_sems.at[j], recv_sem=recv_sems.at[j]))
            lands.append(ici(src_ref=wm_ref.at[mine], dst_ref=gm_ref.at[peer, mine], send_sem=send_sems.at[j], recv_sem=recv_sems.at[j]))
            sends.append(ici(src_ref=wv_ref, dst_ref=gv_ref.at[me], send_sem=send_sems.at[3 + j], recv_sem=recv_sems.at[3 + j]))
            lands.append(ici(src_ref=wv_ref, dst_ref=gv_ref.at[peer], send_sem=send_sems.at[3 + j], recv_sem=recv_sems.at[3 + j]))
            d2d = functools.partial(pltpu.make_async_remote_copy, send_sem=send_sems.at[6 + j], recv_sem=recv_sems.at[6 + j],
                                    device_id=(x, y, 1 - c), device_id_type=MESH)
            passes.append(d2d(src_ref=gm_ref.at[peer, mine], dst_ref=gm_ref.at[peer, mine]))
            from_sibling.append(d2d(src_ref=gm_ref.at[peer, other], dst_ref=gm_ref.at[peer, other]))
        for cp in sends:
            cp.start()
        for j in range(N_XY - 1):
            lands[2 * j].wait_recv()
            passes[j].start()
        for j in range(N_XY - 1):
            lands[2 * j + 1].wait_recv()
        for cp in from_sibling:
            cp.wait_recv()
        for cp in sends + passes:
            cp.wait_send()
        for cp in local:
            cp.wait()

    return pl.pallas_call(
        body, name="all_gather_weights",
        in_specs=[_ANY, _ANY], out_specs=[_ANY, _ANY],
        out_shape=[jax.ShapeDtypeStruct((N_XY,) + wm.shape, wm.dtype), jax.ShapeDtypeStruct((N_XY,) + wv.shape, wv.dtype)],
        scratch_shapes=[pltpu.SemaphoreType.DMA((9,)), pltpu.SemaphoreType.DMA((9,)), pltpu.SemaphoreType.DMA((2,))],
    )(wm, wv)


_HBM = pl.BlockSpec(memory_space=pltpu.HBM)
_SEM = pl.BlockSpec(memory_space=pltpu.SEMAPHORE)
_SPLIT_COPY = functools.partial(pltpu.CompilerParams, has_side_effects=pltpu.SideEffectType.DATAFLOW_SIDE_EFFECTING)


def _gather_xy_start(name, buf, after):
    def body(src_ref, land_ref, after_ref, send_sems, recv_sems, src_thru, land_thru, token):
        x, y, c = lax.axis_index("x"), lax.axis_index("y"), lax.axis_index("c")
        for j, (px, py) in enumerate(_xy_peers()):
            pltpu.make_async_remote_copy(src_ref=src_ref, dst_ref=land_ref.at[2 * x + y], send_sem=send_sems.at[j],
                                         recv_sem=recv_sems.at[j], device_id=(px, py, c), device_id_type=MESH).start()
        token[...] = jnp.zeros_like(token)

    n_peers = N_XY - 1
    land = pltpu.with_memory_space_constraint(lax.empty((N_XY,) + buf.shape, buf.dtype), pltpu.HBM)
    return pl.pallas_call(
        body, name=name,
        out_shape=(pltpu.SemaphoreType.DMA((n_peers,)), pltpu.SemaphoreType.DMA((n_peers,)), pltpu.HBM(buf.shape, buf.dtype),
                   pltpu.HBM(land.shape, buf.dtype), jax.ShapeDtypeStruct((SUBLANES, LANES), F32)),
        in_specs=(_HBM, _HBM, _ANY), out_specs=(_SEM, _SEM, _HBM, _HBM, pl.BlockSpec(memory_space=pltpu.VMEM)),
        input_output_aliases={0: 2, 1: 3}, compiler_params=_SPLIT_COPY(),
    )(pltpu.with_memory_space_constraint(buf, pltpu.HBM), land, after)


def _gather_xy_wait(name, send_sems, recv_sems, src_thru, land_thru, after):
    def body(src_ref, land_ref, send_sems, recv_sems, after_ref, src_dead, got_ref):
        c = lax.axis_index("c")
        for j, (px, py) in enumerate(_xy_peers()):
            cp = pltpu.make_async_remote_copy(src_ref=src_ref, dst_ref=land_ref.at[2 * px + py], send_sem=send_sems.at[j],
                                              recv_sem=recv_sems.at[j], device_id=(px, py, c), device_id_type=MESH)
            cp.wait_send()
            cp.wait_recv()

    return pl.pallas_call(
        body, name=name,
        out_shape=(pltpu.HBM(src_thru.shape, src_thru.dtype), pltpu.HBM(land_thru.shape, land_thru.dtype)),
        in_specs=(_HBM, _HBM, _SEM, _SEM, _ANY), out_specs=(_HBM, _HBM),
        input_output_aliases={0: 0, 1: 1}, compiler_params=_SPLIT_COPY(),
    )(src_thru, land_thru, send_sems, recv_sems, after)[1]


def _exchange_xy_start(name, gsend, after):
    def body(src_ref, land_ref, after_ref, send_sems, recv_sems, src_thru, land_thru, token):
        c = lax.axis_index("c")
        for j, (px, py) in enumerate(_xy_peers()):
            pltpu.make_async_remote_copy(src_ref=src_ref.at[2 * px + py], dst_ref=land_ref.at[j], send_sem=send_sems.at[j],
                                         recv_sem=recv_sems.at[j], device_id=(px, py, c), device_id_type=MESH).start()
        token[...] = jnp.zeros_like(token)

    n_peers = N_XY - 1
    land = pltpu.with_memory_space_constraint(lax.empty((n_peers,) + gsend.shape[1:], gsend.dtype), pltpu.HBM)
    return pl.pallas_call(
        body, name=name,
        out_shape=(pltpu.SemaphoreType.DMA((n_peers,)), pltpu.SemaphoreType.DMA((n_peers,)), pltpu.HBM(gsend.shape, gsend.dtype),
                   pltpu.HBM(land.shape, gsend.dtype), jax.ShapeDtypeStruct((SUBLANES, LANES), F32)),
        in_specs=(_HBM, _HBM, _ANY), out_specs=(_SEM, _SEM, _HBM, _HBM, pl.BlockSpec(memory_space=pltpu.VMEM)),
        input_output_aliases={0: 2, 1: 3}, compiler_params=_SPLIT_COPY(),
    )(pltpu.with_memory_space_constraint(gsend, pltpu.HBM), land, after)


def _exchange_xy_wait(name, send_sems, recv_sems, src_thru, land_thru, after):
    def body(src_ref, land_ref, send_sems, recv_sems, after_ref, src_dead, got_ref):
        c = lax.axis_index("c")
        for j, (px, py) in enumerate(_xy_peers()):
            cp = pltpu.make_async_remote_copy(src_ref=src_ref.at[2 * px + py], dst_ref=land_ref.at[j], send_sem=send_sems.at[j],
                                              recv_sem=recv_sems.at[j], device_id=(px, py, c), device_id_type=MESH)
            cp.wait_send()
            cp.wait_recv()

    return pl.pallas_call(
        body, name=name,
        out_shape=(pltpu.HBM(src_thru.shape, src_thru.dtype), pltpu.HBM(land_thru.shape, land_thru.dtype)),
        in_specs=(_HBM, _HBM, _SEM, _SEM, _ANY), out_specs=(_HBM, _HBM),
        input_output_aliases={0: 0, 1: 1}, compiler_params=_SPLIT_COPY(),
    )(src_thru, land_thru, send_sems, recv_sems, after)[1]


def _exchange_xy(gsend):
    def body(gs_ref, recv_ref, send_sems, recv_sems):
        c = lax.axis_index("c")
        sends = []
        for j, (px, py) in enumerate(_xy_peers()):
            sends.append(pltpu.make_async_remote_copy(
                src_ref=gs_ref.at[2 * px + py], dst_ref=recv_ref.at[j], send_sem=send_sems.at[j], recv_sem=recv_sems.at[j],
                device_id=(px, py, c), device_id_type=MESH))
        for cp in sends:
            cp.start()
        for cp in sends:
            cp.wait_recv()
        for cp in sends:
            cp.wait_send()

    return pl.pallas_call(
        body, name="exchange_grads",
        in_specs=[_ANY], out_specs=_ANY,
        out_shape=jax.ShapeDtypeStruct((N_XY - 1,) + gsend.shape[1:], gsend.dtype),
        scratch_shapes=[pltpu.SemaphoreType.DMA((3,)), pltpu.SemaphoreType.DMA((3,))],
    )(gsend)


def _swap_with_sibling(name, part):
    def body(p_ref, got_ref, send_sem, recv_sem):
        x, y, c = lax.axis_index("x"), lax.axis_index("y"), lax.axis_index("c")
        cp = pltpu.make_async_remote_copy(src_ref=p_ref, dst_ref=got_ref, send_sem=send_sem, recv_sem=recv_sem,
                                          device_id=(x, y, 1 - c), device_id_type=MESH)
        cp.start()
        cp.wait_recv()
        cp.wait_send()

    return pl.pallas_call(
        body, name=name,
        in_specs=[_ANY], out_specs=_ANY, out_shape=jax.ShapeDtypeStruct(part.shape, part.dtype),
        scratch_shapes=[pltpu.SemaphoreType.DMA, pltpu.SemaphoreType.DMA],
    )(part)


def _all_gather_all(vec):
    def body(v_ref, out_ref, send_sems, recv_sems, local_sem):
        x, y, c = lax.axis_index("x"), lax.axis_index("y"), lax.axis_index("c")
        me = 4 * x + 2 * y + c
        local = pltpu.make_async_copy(v_ref, out_ref.at[me], local_sem)
        local.start()
        sends, recvs = [], []
        for f in range(1, N_DEV):
            fx, fy, fc = (f >> 2) & 1, (f >> 1) & 1, f & 1
            px = (1 - x) if fx else x
            py = (1 - y) if fy else y
            pc = (1 - c) if fc else c
            mk = functools.partial(pltpu.make_async_remote_copy, src_ref=v_ref, send_sem=send_sems.at[f - 1],
                                   recv_sem=recv_sems.at[f - 1], device_id=(px, py, pc), device_id_type=MESH)
            sends.append(mk(dst_ref=out_ref.at[me]))
            recvs.append(mk(dst_ref=out_ref.at[4 * px + 2 * py + pc]))
        for cp in sends:
            cp.start()
        for cp in recvs:
            cp.wait_recv()
        for cp in sends:
            cp.wait_send()
        local.wait()

    return pl.pallas_call(
        body, name="all_gather_replicated",
        in_specs=[_ANY], out_specs=_ANY, out_shape=jax.ShapeDtypeStruct((N_DEV,) + vec.shape, vec.dtype),
        scratch_shapes=[pltpu.SemaphoreType.DMA((N_DEV - 1,)), pltpu.SemaphoreType.DMA((N_DEV - 1,)), pltpu.SemaphoreType.DMA],
    )(vec)


def _adamw(g, w, m, v):
    m2 = ADAM_B1 * m + (1.0 - ADAM_B1) * g
    v2 = ADAM_B2 * v + (1.0 - ADAM_B2) * g * g
    m_hat = m2 / (1.0 - ADAM_B1 ** ADAM_STEP)
    v_hat = v2 / (1.0 - ADAM_B2 ** ADAM_STEP)
    return -ADAM_LR * (m_hat / (jnp.sqrt(v_hat) + ADAM_EPS) + ADAM_WD * w), m2, v2


def _sum_contributions(name, own, recv):
    def body(rv, cv, pv, nv):
        return [((rv[0] + _f32(rv[1])) + _f32(rv[2])) + _f32(rv[3])], []
    stacked = recv.reshape(-1, PACK_W)
    (part,), _ = _rowwise(name, body, [own] + [(stacked, PACK_W, 0, j * own.shape[0]) for j in range(N_XY - 1)], [],
                          [(PACK_W, F32)])
    return part


def _adamw_sharded(name, part, sib, first, w, m, v, w_first=0, n_rows=None):
    def body(rv, cv, pv, nv):
        g = rv[3] + rv[4]
        return [g, *_adamw(g, rv[0], rv[1], rv[2])], []
    n_rows = w.shape[0] if n_rows is None else n_rows
    rows = [(z, PACK_W, 0, w_first) for z in (w, m, v)] + [(part, PACK_W, 0, first), (sib, PACK_W, 0, first)]
    outs, _ = _rowwise(name, body, rows, [], [(PACK_W, F32)] * 4, tm=256, n_rows=n_rows)
    return outs


def _adamw_replicated(parts, w, m, v):
    def body(rv, cv, pv, nv):
        g = rv[0]
        for i in range(1, N_DEV):
            g = g + rv[i]
        return [g, *_adamw(g, rv[N_DEV], rv[N_DEV + 1], rv[N_DEV + 2])], []
    outs, _ = _rowwise("adamw_replicated", body, [parts[i] for i in range(N_DEV)] + [w, m, v], [], [(PACK_W, F32)] * 4)
    return outs


def _row(v):
    return v.reshape(1, -1).astype(F32)


def _local_step(x3, mem3, target3, fw, late_weights=None, on_grads=None):
    def with_late(stage, after):
        if late_weights is None:
            return fw
        got = late_weights(stage, after)
        return {**fw, **{n: ({**fw.get(n, {}), **v} if isinstance(v, dict) else v) for n, v in got.items()}}

    nseq, seq, d = x3.shape
    mem_len = mem3.shape[1]
    t = nseq * seq
    x0 = x3.reshape(t, d)
    mem2 = mem3.reshape(nseq * mem_len, d)
    target = target3.reshape(t, d)
    ln = fw["ln_gains"]
    gains = [[_row(ln[i, j]) for j in range(6)] for i in range(2)]
    nh = d // RWKV_HEAD
    seg = (jnp.arange(d)[:, None] // RWKV_HEAD == jnp.arange(LANES)[None, :]).astype(BF16)

    w_gate = fw["a_gate_w"][0]
    pa = dict(g_pre=gains[0][0], g_post=gains[0][1], w_in=fw["a_w_in"][0], b_in_y=_row(fw["a_b_in"][0, :d]),
              b_in_u=_row(fw["a_b_in"][0, d:]), conv_w=fw["a_conv_w"][0].astype(F32), conv_b=_row(fw["a_conv_b"][0]),
              w_gate=w_gate, gate_b0=_row(fw["a_gate_b"][0, 0]), gate_b1=_row(fw["a_gate_b"][0, 1]), lam=_row(fw["a_lambda"][0]),
              w_out=fw["a_w_out"][0], b_out=_row(fw["a_b_out"][0]))
    pta = dict(w_in_t=_t(pa["w_in"]), w_out_t=_t(pa["w_out"]))
    mem_g = _row(fw["mem_norm"])

    mem_n = _norm_fwd("mem_norm", mem2, mem_g)
    to_c0, to_m0, to_c1, to_m1, to_loss = ({"gain": gains[0][2]}, {"gain": gains[0][4]}, {"gain": gains[1][2]},
                                           {"gain": gains[1][4]}, {"target": target})
    x1, sv_a = _rglru_fwd(x0, pa, seq, chain=to_c0)
    fw = with_late("cm0", x1)
    x2, sv_c0 = _xattn_fwd("c0", x1, mem_n, gains[0][2], gains[0][3], fw["c_w_q"][0], fw["c_w_kv"][0], fw["c_w_o"][0], seq, mem_len,
                           hn=to_c0["hn"], chain=to_m0)
    x3_, sv_m0 = _mlp_fwd("m0", x2, gains[0][4], gains[0][5], fw["m_w_up"][0], fw["m_w_down"][0], hn=to_m0["hn"])
    fw = with_late("layer1", x3_)
    pb = dict(g_pre=gains[1][0], g_post=gains[1][1], mu=fw["b_mu"][0].astype(F32), w_r=fw["b_w_rkv"][0, 0],
              w_k=fw["b_w_rkv"][0, 1], w_v=fw["b_w_rkv"][0, 2], w0=_row(fw["b_w0"][0]), w1=fw["b_w1"][0], w2=fw["b_w2"][0],
              a0=_row(fw["b_a0"][0]), a1=fw["b_a1"][0], a2=fw["b_a2"][0], g1=fw["b_g1"][0], g2=fw["b_g2"][0],
              k_k=_row(fw["b_k_k"][0]), k_a=_row(fw["b_k_a"][0]), r_k=_row(fw["b_r_k"][0]), gn_g=_row(fw["b_gn_g"][0]),
              gn_b=_row(fw["b_gn_b"][0]), w_o=fw["b_w_o"][0], seg=seg)
    ptb = {k + "_t": _t(pb[k]) for k in ("w_r", "w_k", "w_v", "w_o", "w1", "w2", "a1", "a2", "g1", "g2")}
    x4, sv_b = _rwkv_fwd(x3_, pb, seq, chain=to_c1)
    x5, sv_c1 = _xattn_fwd("c1", x4, mem_n, gains[1][2], gains[1][3], fw["c_w_q"][1], fw["c_w_kv"][1], fw["c_w_o"][1], seq, mem_len,
                           hn=to_c1["hn"], chain=to_m1)
    _, sv_m1 = _mlp_fwd("m1", x5, gains[1][4], gains[1][5], fw["m_w_up"][1], fw["m_w_down"][1], hn=to_m1["hn"], chain=to_loss)
    dx = to_loss["dy"]
    loss_part = 0.5 / d * jnp.sum(to_loss["sq"])

    post = lambda saved, g, bias=None: {"t": saved[-1], "g": g, "bias": bias}
    of_c1, of_b, of_m0, of_c0, of_a = (post(sv_c1, gains[1][3]), post(sv_b, gains[1][1]), post(sv_m0, gains[0][5]),
                                       post(sv_c0, gains[0][3]), post(sv_a, gains[0][1], pa["b_out"]))
    dx, g_m1 = _mlp_bwd("m1", sv_m1, dx, gains[1][4], gains[1][5], _t(fw["m_w_up"][1]), _t(fw["m_w_down"][1]), below=of_c1)
    dx, dmem1, g_c1 = _xattn_bwd("c1", sv_c1, dx, mem_n, gains[1][2], gains[1][3], _t(fw["c_w_q"][1]), _t(fw["c_w_kv"][1]),
                                 _t(fw["c_w_o"][1]), seq, mem_len, done=of_c1["done"], below=of_b)
    dx, g_b = _rwkv_bwd(sv_b, dx, pb, ptb, seq, done=of_b["done"], below=of_m0)
    grads_b = dict(
        b_mu=g_b["mu"][None], b_w_rkv=jnp.stack([g_b["w_r"], g_b["w_k"], g_b["w_v"]])[None], b_w0=g_b["w0"][None],
        b_w1=g_b["w1"][None], b_w2=g_b["w2"][None], b_a0=g_b["a0"][None], b_a1=g_b["a1"][None], b_a2=g_b["a2"][None],
        b_g1=g_b["g1"][None], b_g2=g_b["g2"][None], b_k_k=g_b["k_k"][None], b_k_a=g_b["k_a"][None],
        b_r_k=g_b["r_k"].reshape(1, nh, RWKV_HEAD), b_gn_g=g_b["gn_g"][None], b_gn_b=g_b["gn_b"][None], b_w_o=g_b["w_o"][None])
    g_m0_pre = gains[0][4]
    if on_grads is not None:
        layer1 = dict(c_w_q=g_c1["w_q"], c_w_kv=g_c1["w_kv"], c_w_o=g_c1["w_o"], m_w_up=g_m1["w_up"], m_w_down=g_m1["w_down"])
        g_m0_pre = g_m0_pre + on_grads("layer1", {**grads_b, **{n: [None, g] for n, g in layer1.items()}})[0, 0]
    dx, g_m0 = _mlp_bwd("m0", sv_m0, dx, g_m0_pre, gains[0][5], _t(fw["m_w_up"][0]), _t(fw["m_w_down"][0]),
                        done=of_m0["done"], below=of_c0)
    dx, dmem0, g_c0 = _xattn_bwd("c0", sv_c0, dx, mem_n, gains[0][2], gains[0][3], _t(fw["c_w_q"][0]), _t(fw["c_w_kv"][0]),
                                 _t(fw["c_w_o"][0]), seq, mem_len, done=of_c0["done"], below=of_a)
    if on_grads is not None:
        cm0 = dict(c_w_q=g_c0["w_q"], c_w_kv=g_c0["w_kv"], c_w_o=g_c0["w_o"], m_w_up=g_m0["w_up"], m_w_down=g_m0["w_down"])
        pa = {**pa, "g_pre": pa["g_pre"] + on_grads("cm0", {n: [g, None] for n, g in cm0.items()})[0, 0]}
    dx, g_a = _rglru_bwd(sv_a, dx, pa, pta, seq, done=of_a["done"])

    def dmem_body(rv, cv, pv, nv):
        _, dg = _rms_bwd(rv[1] + rv[2], rv[0], cv[0])
        return [], [dg]
    _, (dmem_g,) = _rowwise("mem_norm_grad", dmem_body, [mem2, dmem0, dmem1], [mem_g], [], [d])

    lru_heads = fw["a_gate_w"].shape[2]
    blk = d // lru_heads
    grads = dict(
        ln_gains=jnp.stack([jnp.stack([g_a["g_pre"], g_a["g_post"], g_c0["g_pre"], g_c0["g_post"], g_m0["g_pre"], g_m0["g_post"]]),
                            jnp.stack([g_b["g_pre"], g_b["g_post"], g_c1["g_pre"], g_c1["g_post"], g_m1["g_pre"], g_m1["g_post"]])]),
        mem_norm=dmem_g.sum(0),
        a_conv_w=g_a["conv_w"][None], a_conv_b=g_a["conv_b"][None], a_w_in=g_a["w_in"][None], a_b_in=g_a["b_in"][None],
        a_gate_w=g_a["w_gate"][None],
        a_gate_b=jnp.stack([g_a["gate_b0"], g_a["gate_b1"]]).reshape(1, 2, lru_heads, blk),
        a_lambda=g_a["lam"][None], a_w_out=g_a["w_out"][None], a_b_out=g_a["b_out"][None],
        **grads_b,
        c_w_q=[g_c0["w_q"], g_c1["w_q"]], c_w_kv=[g_c0["w_kv"], g_c1["w_kv"]], c_w_o=[g_c0["w_o"], g_c1["w_o"]],
        m_w_up=[g_m0["w_up"], g_m1["w_up"]], m_w_down=[g_m0["w_down"], g_m1["w_down"]],
    )
    return loss_part, dx.reshape(nseq, seq, d), grads


def kernel(x, mem, ln_gains, mem_norm, a_conv_w, a_conv_b, a_w_in, a_b_in, a_gate_w, a_gate_b, a_lambda, a_w_out, a_b_out, b_mu, b_w_rkv, b_w0, b_w1, b_w2, b_a0, b_a1, b_a2, b_g1, b_g2, b_k_k, b_k_a, b_r_k, b_gn_g, b_gn_b, b_w_o, c_w_q, c_w_kv, c_w_o, m_w_up, m_w_down, loss_target, m_ln_gains, m_mem_norm, m_a_conv_w, m_a_conv_b, m_a_w_in, m_a_b_in, m_a_gate_w, m_a_gate_b, m_a_lambda, m_a_w_out, m_a_b_out, m_b_mu, m_b_w_rkv, m_b_w0, m_b_w1, m_b_w2, m_b_a0, m_b_a1, m_b_a2, m_b_g1, m_b_g2, m_b_k_k, m_b_k_a, m_b_r_k, m_b_gn_g, m_b_gn_b, m_b_w_o, m_c_w_q, m_c_w_kv, m_c_w_o, m_m_w_up, m_m_w_down, v_ln_gains, v_mem_norm, v_a_conv_w, v_a_conv_b, v_a_w_in, v_a_b_in, v_a_gate_w, v_a_gate_b, v_a_lambda, v_a_w_out, v_a_b_out, v_b_mu, v_b_w_rkv, v_b_w0, v_b_w1, v_b_w2, v_b_a0, v_b_a1, v_b_a2, v_b_g1, v_b_g2, v_b_k_k, v_b_k_a, v_b_r_k, v_b_gn_g, v_b_gn_b, v_b_w_o, v_c_w_q, v_c_w_kv, v_c_w_o, v_m_w_up, v_m_w_down):
    given = dict(locals())
    w = {n: given[n] for n in _WEIGHTS}
    mom1 = {n: given["m_" + n] for n in _WEIGHTS}
    mom2 = {n: given["v_" + n] for n in _WEIGHTS}

    me = 2 * lax.axis_index("x") + lax.axis_index("y")
    per_layer = [n for n in _MATRICES if n[0] in "cm"]
    early = [(n, None) for n in _MATRICES if n[0] == "a"]
    late = dict(cm0=[(n, 0) for n in per_layer], layer1=[(n, None) for n in _MATRICES if n[0] == "b"] + [(n, 1) for n in per_layer])
    piece = lambda n, layer: w[n] if layer is None else w[n][layer:layer + 1]

    def gathered(entries, buf):
        shards = [_unpack(buf[s], [piece(n, layer).shape for n, layer in entries]) for s in range(N_XY)]
        out = {}
        for i, (n, layer) in enumerate(entries):
            full = jnp.concatenate([shards[s][i] for s in range(N_XY)], axis=_SHARD_AXIS[n])
            if layer is None:
                out[n] = full
            else:
                out.setdefault(n, {})[layer] = full[0]
        return out

    late_bufs = {stage: _pack([piece(n, layer) for n, layer in entries], BF16) for stage, entries in late.items()}
    gm, gv = _all_gather_xy(_pack([piece(n, layer) for n, layer in early], BF16), _pack([w[n] for n in _VECTORS], F32, SUBLANES))
    in_flight = {}
    token = gv
    for stage in late:
        *in_flight[stage], token = _gather_xy_start("gather_%s_start" % stage, late_bufs[stage], token)
    fw = {n: w[n] for n in _REPLICATED} | gathered(early, gm)
    vec_shards = [_unpack(gv[s], [w[n].shape for n in _VECTORS]) for s in range(N_XY)]
    for i, n in enumerate(_VECTORS):
        fw[n] = jnp.concatenate([vec_shards[s][i] for s in range(N_XY)], axis=_SHARD_AXIS[n])
    fw["ln_gains"] = fw["ln_gains"] + token[0, 0]

    def late_weights(stage, after):
        land = _gather_xy_wait("gather_%s_wait" % stage, *in_flight[stage], after)
        return gathered(late[stage], lax.dynamic_update_index_in_dim(land, late_bufs[stage], me, 0))

    tile_rows = 256
    groups = dict(layer1=[(n, 1 if n in per_layer else None) for n in _SHARDED if n in per_layer or n[0] == "b"],
                  cm0=[(n, 0) for n in _SHARDED if n in per_layer],
                  rest=[(n, None) for n in _SHARDED if n not in per_layer and n[0] != "b"])

    def piece_shape(n, layer):
        return w[n].shape if layer is None else (1,) + w[n].shape[1:]

    def rows_of(n, layer):
        size = 1
        for s_ in piece_shape(n, layer):
            size *= s_
        return size // PACK_W

    def split(entries):
        in_place = [e for e in entries if w[e[0]].shape[-1] == PACK_W and rows_of(*e) % tile_rows == 0 and rows_of(*e) > 0]
        in_place.sort(key=lambda e: -rows_of(*e))
        return in_place, [e for e in entries if e not in in_place]

    def buffers(entries, grads):
        in_place, packed = split(entries)

        def pieces(s):
            out = []
            for n, layer in in_place + packed:
                ax = _SHARD_AXIS[n]
                size = w[n].shape[ax]
                g = grads[n] if layer is None else grads[n][layer]
                out.append(lax.dynamic_slice_in_dim(g, s * size, size, axis=ax if layer is None else ax - 1))
            return out
        own = _pack(pieces(me), F32)
        n_rows = own.shape[0]
        used = sum(-(-rows_elems // PACK_W) for rows_elems in (p.size for p in pieces(0)))
        fill = [jnp.zeros(((n_rows - used) * PACK_W,), BF16)] if n_rows > used else []
        gsend = _pack([p for s in range(N_XY) for p in pieces(s) + fill], BF16, row_mult=1)
        return gsend.reshape(N_XY, n_rows, PACK_W), own

    def update(tag, entries, own, recv):
        in_place, packed = split(entries)
        part = _sum_contributions("sum_grads_" + tag, own, recv)
        sib = _swap_with_sibling("swap_sibling_" + tag, part)
        out = {}
        first = 0
        for n, layer in in_place:
            flat = [src[n].reshape(-1, PACK_W) for src in (w, mom1, mom2)]
            rows = rows_of(n, layer)
            res = _adamw_sharded("adamw_%s_%s" % (n, tag), part, sib, first, *flat, w_first=(layer or 0) * rows, n_rows=rows)
            out[(n, layer)] = [o.reshape(piece_shape(n, layer)) for o in res]
            first += rows
        take = lambda src, n, layer: src[n] if layer is None else src[n][layer:layer + 1]
        flat = [_pack([take(src, n, layer) for n, layer in packed], F32) for src in (w, mom1, mom2)]
        assert first + flat[0].shape[0] == part.shape[0], (first, flat[0].shape, part.shape)
        res = _adamw_sharded("adamw_packed_" + tag, part, sib, first, *flat)
        tail = [_unpack(o, [piece_shape(n, layer) for n, layer in packed]) for o in res]
        for i, e in enumerate(packed):
            out[e] = [tail[kind][i] for kind in range(4)]
        return out

    sent = {}

    def on_grads(stage, grads_so_far):
        gsend, own = buffers(groups[stage], grads_so_far)
        *handles, token = _exchange_xy_start("exchange_%s_start" % stage, gsend, own)
        sent[stage] = (own, handles)
        return token

    loss_part, grad_x, grads = _local_step(x, mem, loss_target, fw, late_weights, on_grads)

    gsend, own = buffers(groups["rest"], grads)
    out = update("rest", groups["rest"], own, _exchange_xy(gsend))
    for stage, (own, handles) in sent.items():
        out |= update(stage, groups[stage], own, _exchange_xy_wait("exchange_%s_wait" % stage, *handles, grad_x))
    sharded_out = [[] for _ in range(4)]
    for n in _SHARDED:
        for kind in range(4):
            if n in per_layer:
                sharded_out[kind].append(jnp.concatenate([out[(n, 0)][kind], out[(n, 1)][kind]], axis=0))
            else:
                sharded_out[kind].append(out[(n, None)][kind])

    small = _pack([grads[n] for n in _REPLICATED] + [loss_part.reshape(1)], F32, SUBLANES)
    parts = _all_gather_all(small)
    zero = jnp.zeros((1,), F32)
    flat = [_pack([src[n] for n in _REPLICATED] + [zero], F32, SUBLANES) for src in (w, mom1, mom2)]
    repl_out = [_unpack(o, [w[n].shape for n in _REPLICATED] + [(1,)]) for o in _adamw_replicated(parts, *flat)]
    loss = repl_out[0][-1][0]

    result = [loss, grad_x]
    for kind in range(4):
        by_name = dict(zip(_SHARDED, sharded_out[kind])) | dict(zip(_REPLICATED, repl_out[kind][:-1]))
        result += [by_name[n] for n in _WEIGHTS]
    return tuple(result)
```
